```python
import math
import jax
import jax.numpy as jnp
from jax import lax
import numpy as np


D_MODEL = 1024
BATCH = 2
SEQ = 8192
DEPTH = 2

GRID_W = 64
CTX_LEN = 256
EPS = 1e-6
NEG_INF = -1e30
N_MOD = 6
D_FF = 4 * D_MODEL
ROPE_BASE = 10000.0

HY_WIDTH = D_MODEL // 2
HY_SHORT = 3
HY_EMB = 33
HY_BANDS = (HY_EMB - 1) // 2
HY_HIDDEN = 64
HY_TARGET = 1e-2
HY_FAST_PCT = 0.3
HY_SLOW_PCT = 1.5
HY_MIN_DECAY = -math.log(HY_TARGET) / HY_SLOW_PCT
HY_MAX_DECAY = -math.log(HY_TARGET) / HY_FAST_PCT

ATT_HEADS = 8
ATT_KV_HEADS = 2
ATT_GROUP = ATT_HEADS // ATT_KV_HEADS
HEAD_DIM = 64
ATT_WIDTH = ATT_HEADS * HEAD_DIM
KV_WIDTH = ATT_KV_HEADS * HEAD_DIM
WINDOW = 128
BLOCK = 128

EV_IN = 3 * HY_WIDTH + ATT_WIDTH + 2 * KV_WIDTH
EV_OUT = HY_WIDTH + ATT_WIDTH

RET_HEADS = 4
RET_DK = D_MODEL // RET_HEADS
RET_DV = 2 * RET_DK
RET_CHUNK = 128
RET_QK = RET_HEADS * RET_DK
RET_V = RET_HEADS * RET_DV
OD_IN = 2 * RET_QK + 3 * RET_V
OD_OUT = RET_V

N_EVEN = (DEPTH + 1) // 2
N_ODD = DEPTH // 2

kernel_name = 'hybrid_hyena_swa_retention_dit'


def rms_norm(x, g):
    xf = x.astype(jnp.float32)
    y = xf * lax.rsqrt(jnp.mean(xf * xf, axis=-1, keepdims=True) + EPS)
    return (y * g.astype(jnp.float32)).astype(x.dtype)


def modulate(h, shift, scale):
    return h * (1 + scale) + shift


def sq_relu_mlp(h, w1, w2):
    return jnp.square(jax.nn.relu(h @ w1)) @ w2


def rotate_half(x, cos, sin):
    x1, x2 = jnp.split(x, 2, axis=-1)
    return jnp.concatenate([x1 * cos - x2 * sin, x2 * cos + x1 * sin], axis=-1)


def rope_2d(x, pos_row, pos_col):
    half, quarter = HEAD_DIM // 2, HEAD_DIM // 4
    inv = ROPE_BASE ** (-jnp.arange(quarter, dtype=jnp.float32) / quarter)
    parts = []
    for xa, pos in ((x[..., :half], pos_row), (x[..., half:], pos_col)):
        ang = pos[:, None] * inv[None, :]
        parts.append(rotate_half(xa, jnp.cos(ang)[:, None, :].astype(x.dtype), jnp.sin(ang)[:, None, :].astype(x.dtype)))
    return jnp.concatenate(parts, axis=-1)


def rope_1d(x, pos):
    inv = ROPE_BASE ** (-jnp.linspace(0.0, 1.0, RET_DK // 2, dtype=jnp.float32))
    ang = pos[:, None] * inv[None, :]
    return rotate_half(x, jnp.cos(ang), jnp.sin(ang))


def short_conv(u, w, b):
    L = u.shape[1]
    up = jnp.pad(u, ((0, 0), (1, 1), (0, 0)))
    return up[:, :L] * w[0] + up[:, 1:L + 1] * w[1] + up[:, 2:] * w[2] + b


def hyena_filter_freq(L, w1, b1, w2, b2, w3, freq, decay):
    f32 = jnp.float32
    t = (jnp.arange(L, dtype=f32) / L)[:, None]
    bands = jnp.linspace(1e-4, HY_BANDS - 1, HY_BANDS, dtype=f32)
    ang = 2.0 * math.pi * t * bands[None, :]
    z = jnp.concatenate([t, jnp.cos(ang), -jnp.sin(ang)], axis=-1)
    fr = freq.astype(f32)
    hid = jnp.sin(fr * (z @ w1.astype(f32) + b1.astype(f32)))
    hid = jnp.sin(fr * (hid @ w2.astype(f32) + b2.astype(f32)))
    h = (hid @ w3.astype(f32)).reshape(L, 2, HY_WIDTH)
    h = h * jnp.exp(-t[:, :, None] * jnp.abs(decay.astype(f32))[None])
    h_pos, h_neg = h[:, 0], h[1:, 1]
    norm = jnp.sqrt(jnp.sum(h_pos * h_pos, axis=0) + jnp.sum(h_neg * h_neg, axis=0))
    h_circ = jnp.concatenate([h_pos, jnp.zeros((1, HY_WIDTH), f32), h_neg[::-1]], axis=0) / norm
    return jnp.fft.rfft(h_circ, axis=0)


def hyena(u, conv_w, conv_b, bias, filt):
    L = u.shape[1]
    x0, x1, v = jnp.split(short_conv(u, conv_w, conv_b), 3, axis=-1)
    z = v * x1
    hf = hyena_filter_freq(L, *filt)
    zf = jnp.fft.rfft(z.astype(jnp.float32), n=2 * L, axis=1)
    y = jnp.fft.irfft(zf * hf[None], n=2 * L, axis=1)[:, :L].astype(z.dtype)
    return x0 * (y + z * bias)


def softmax_with_sink(s, sink):
    col = jnp.broadcast_to(sink, s.shape[:-1] + (1,))
    return jax.nn.softmax(jnp.concatenate([s, col], axis=-1), axis=-1)[..., :-1]


def latent_window_attention(q, k, v, k_c, v_c, sink):
    B, S = q.shape[:2]
    nb = S // BLOCK
    n_win = 3 * BLOCK
    scale = HEAD_DIM ** -0.5
    qb = q.reshape(B, nb, BLOCK, ATT_KV_HEADS, ATT_GROUP, HEAD_DIM)

    def band(t):
        tp = jnp.pad(t, ((0, 0), (BLOCK, BLOCK), (0, 0), (0, 0))).reshape(B, nb + 2, BLOCK, ATT_KV_HEADS, HEAD_DIM)
        return jnp.concatenate([tp[:, :-2], tp[:, 1:-1], tp[:, 2:]], axis=2)

    kw, vw = band(k), band(v)
    q_off = jnp.arange(BLOCK)[:, None]
    k_off = jnp.arange(n_win)[None, :] - BLOCK
    key_pos = jnp.arange(nb)[:, None, None] * BLOCK + k_off[None]
    valid = (jnp.abs(q_off - k_off) <= WINDOW)[None] & (key_pos >= 0) & (key_pos < S)
    s_loc = jnp.einsum('bnqkgd,bnskd->bkgnqs', qb, kw).astype(jnp.float32) * scale
    s_loc = jnp.where(valid, s_loc, NEG_INF)
    s_ctx = jnp.einsum('bnqkgd,bckd->bkgnqc', qb, k_c).astype(jnp.float32) * scale
    sink_b = sink.astype(jnp.float32).reshape(1, ATT_KV_HEADS, ATT_GROUP, 1, 1, 1)
    p = softmax_with_sink(jnp.concatenate([s_loc, s_ctx], axis=-1), sink_b).astype(v.dtype)
    o = (jnp.einsum('bkgnqs,bnskd->bnqkgd', p[..., :n_win], vw)
         + jnp.einsum('bkgnqc,bckd->bnqkgd', p[..., n_win:], v_c))
    return o.reshape(B, S, ATT_WIDTH)


def context_attention(q_c, k_c, v_c, sink):
    B, Lc = q_c.shape[:2]
    qg = q_c.reshape(B, Lc, ATT_KV_HEADS, ATT_GROUP, HEAD_DIM)
    s = jnp.einsum('bqkgd,bskd->bkgqs', qg, k_c).astype(jnp.float32) * HEAD_DIM ** -0.5
    p = softmax_with_sink(s, sink.astype(jnp.float32).reshape(1, ATT_KV_HEADS, ATT_GROUP, 1, 1)).astype(v_c.dtype)
    return jnp.einsum('bkgqs,bskd->bqkgd', p, v_c).reshape(B, Lc, ATT_WIDTH)


def even_mixer(h_c, h_l, w_in, w_out, conv_w, conv_b, hy_w1, hy_b1, hy_w2, hy_b2, hy_w3, hy_freq,
               hy_decay, hy_bias, sink, pos_row, pos_col, need_ctx_out):
    filt = (hy_w1, hy_b1, hy_w2, hy_b2, hy_w3, hy_freq, hy_decay)
    i_q = 3 * HY_WIDTH
    i_k = i_q + ATT_WIDTH
    i_v = i_k + KV_WIDTH

    def heads(p):
        B, L, _ = p.shape
        return (p[..., i_q:i_k].reshape(B, L, ATT_HEADS, HEAD_DIM),
                p[..., i_k:i_v].reshape(B, L, ATT_KV_HEADS, HEAD_DIM),
                p[..., i_v:].reshape(B, L, ATT_KV_HEADS, HEAD_DIM))

    p_l = h_l @ w_in
    p_c = h_c @ w_in
    q_l, k_l, v_l = heads(p_l)
    q_c, k_c, v_c = heads(p_c)
    q_l = rope_2d(q_l, pos_row, pos_col)
    k_l = rope_2d(k_l, pos_row, pos_col)
    y_l = jnp.concatenate([hyena(p_l[..., :i_q], conv_w, conv_b, hy_bias, filt),
                           latent_window_attention(q_l, k_l, v_l, k_c, v_c, sink)], axis=-1) @ w_out
    y_c = None
    if need_ctx_out:
        y_c = jnp.concatenate([hyena(p_c[..., :i_q], conv_w, conv_b, hy_bias, filt),
                               context_attention(q_c, k_c, v_c, sink)], axis=-1) @ w_out
    return y_c, y_l


def chunk_retention(q, k, v, state, log_gamma):
    B, H, L, _ = q.shape
    n = L // RET_CHUNK
    idx = jnp.arange(RET_CHUNK, dtype=jnp.float32)
    diff = idx[:, None] - idx[None, :]
    dmask = jnp.exp(jnp.where(diff >= 0, diff[None] * log_gamma[:, None, None], -jnp.inf))
    xi = jnp.exp((idx + 1)[None, :] * log_gamma[:, None])[..., None]
    zeta = jnp.exp((RET_CHUNK - 1 - idx)[None, :] * log_gamma[:, None])[..., None]
    g_chunk = jnp.exp(RET_CHUNK * log_gamma)[:, None, None]

    def to_chunks(t):
        return t.reshape(B, H, n, RET_CHUNK, t.shape[-1]).transpose(2, 0, 1, 3, 4)

    def step(R, inp):
        qn, kn, vn = inp
        inner = jnp.einsum('bhid,bhjd->bhij', qn, kn) * dmask
        o = jnp.einsum('bhij,bhje->bhie', inner, vn) + jnp.einsum('bhid,bhde->bhie', qn * xi, R)
        R = g_chunk * R + jnp.einsum('bhjd,bhje->bhde', kn * zeta, vn)
        return R, o

    state, o = lax.scan(step, state, (to_chunks(q), to_chunks(k), to_chunks(v)))
    return o.transpose(1, 2, 0, 3, 4).reshape(B, H, L, RET_DV), state


def retention_direction(q_c, k_c, v_c, q_l, k_l, v_l, log_gamma):
    B, H, Lc = q_c.shape[:3]
    Ll = q_l.shape[2]
    pos_c = jnp.arange(Lc, dtype=jnp.float32)
    pos_l = Lc + jnp.arange(Ll, dtype=jnp.float32)
    state0 = jnp.zeros((B, H, RET_DK, RET_DV), jnp.float32)
    o_c, state_c = chunk_retention(rope_1d(q_c, pos_c), rope_1d(k_c, pos_c), v_c, state0, log_gamma)
    o_l, _ = chunk_retention(rope_1d(q_l, pos_l), rope_1d(k_l, pos_l), v_l, state_c, log_gamma)
    return o_c, o_l


def head_rms(o):
    o = o * lax.rsqrt(jnp.mean(o * o, axis=-1, keepdims=True) + EPS)
    B, H, L, dv = o.shape
    return o.transpose(0, 2, 1, 3).reshape(B, L, H * dv)


def retention_mixer(h_c, h_l, w_in, w_out, log_rate, need_ctx_out):
    def split(p):
        B, L, _ = p.shape
        f32 = jnp.float32
        q = p[..., :RET_QK].reshape(B, L, RET_HEADS, RET_DK).transpose(0, 2, 1, 3).astype(f32)
        k = p[..., RET_QK:2 * RET_QK].reshape(B, L, RET_HEADS, RET_DK).transpose(0, 2, 1, 3).astype(f32) * RET_DK ** -0.5
        v = p[..., 2 * RET_QK:2 * RET_QK + RET_V].reshape(B, L, RET_HEADS, RET_DV).transpose(0, 2, 1, 3).astype(f32)
        g_f = p[..., 2 * RET_QK + RET_V:2 * RET_QK + 2 * RET_V]
        g_b = p[..., 2 * RET_QK + 2 * RET_V:]
        return q, k, v, g_f, g_b

    q_l, k_l, v_l, gf_l, gb_l = split(h_l @ w_in)
    q_c, k_c, v_c, gf_c, gb_c = split(h_c @ w_in)
    log_gamma = -jnp.exp(log_rate.astype(jnp.float32))

    def flip(t):
        return t[:, :, ::-1]

    of_c, of_l = retention_direction(q_c, k_c, v_c, q_l, k_l, v_l, log_gamma[0])
    ob_c, ob_l = retention_direction(flip(q_c), flip(k_c), flip(v_c), flip(q_l), flip(k_l), flip(v_l), log_gamma[1])

    def merge(o_f, o_b, g_f, g_b):
        y = jax.nn.silu(g_f) * head_rms(o_f).astype(g_f.dtype) + jax.nn.silu(g_b) * head_rms(o_b).astype(g_b.dtype)
        return y @ w_out

    y_l = merge(of_l, flip(ob_l), gf_l, gb_l)
    y_c = None
    if need_ctx_out:
        y_c = merge(of_c, flip(ob_c), gf_c, gb_c)
    return y_c, y_l


def setup_inputs(seed: int = 0) -> dict:
    key = jax.random.key(seed)
    ks = jax.random.split(key, 32)
    D = D_MODEL

    def nrm(k, shape, scale):
        return jax.random.normal(k, shape, jnp.float32) * scale

    hy_base = jnp.linspace(HY_MIN_DECAY, HY_MAX_DECAY, HY_WIDTH, dtype=jnp.float32)
    ret_base = -(5.0 + jnp.arange(RET_HEADS, dtype=jnp.float32)) * math.log(2.0)
    return {
        'x': nrm(ks[0], (BATCH, SEQ, D), 1.0),
        'c': nrm(ks[1], (BATCH, D), 1.0),
        'ctx': nrm(ks[2], (BATCH, CTX_LEN, D), 1.0),
        'c_ctx': nrm(ks[3], (D,), 1.0),
        'ada_w': nrm(ks[4], (DEPTH, D, N_MOD * D), 0.5 * D ** -0.5),
        'ada_b': nrm(ks[5], (DEPTH, N_MOD * D), 0.01),
        'norm_mix_g': 1.0 + nrm(ks[6], (DEPTH, D), 0.02),
        'norm_mlp_g': 1.0 + nrm(ks[7], (DEPTH, D), 0.02),
        'mlp_w1': nrm(ks[8], (DEPTH, D, D_FF), D ** -0.5),
        'mlp_w2': nrm(ks[9], (DEPTH, D_FF, D), D_FF ** -0.5),
        'ev_w_in': nrm(ks[10], (N_EVEN, D, EV_IN), D ** -0.5),
        'ev_w_out': nrm(ks[11], (N_EVEN, EV_OUT, D), EV_OUT ** -0.5),
        'hy_conv_w': nrm(ks[12], (N_EVEN, HY_SHORT, 3 * HY_WIDTH), HY_SHORT ** -0.5),
        'hy_conv_b': nrm(ks[13], (N_EVEN, 3 * HY_WIDTH), 0.02),
        'hy_w1': nrm(ks[14], (N_EVEN, HY_EMB, HY_HIDDEN), HY_EMB ** -0.5),
        'hy_b1': nrm(ks[15], (N_EVEN, HY_HIDDEN), 0.1),
        'hy_w2': nrm(ks[16], (N_EVEN, HY_HIDDEN, HY_HIDDEN), HY_HIDDEN ** -0.5),
        'hy_b2': nrm(ks[17], (N_EVEN, HY_HIDDEN), 0.1),
        'hy_w3': nrm(ks[18], (N_EVEN, HY_HIDDEN, 2 * HY_WIDTH), HY_HIDDEN ** -0.5),
        'hy_freq': 1.0 + nrm(ks[19], (N_EVEN, HY_HIDDEN), 0.1),
        'hy_decay': hy_base * (1.0 + nrm(ks[20], (N_EVEN, 2, HY_WIDTH), 0.05)),
        'hy_bias': nrm(ks[21], (N_EVEN, HY_WIDTH), 1.0),
        'attn_sink': nrm(ks[22], (N_EVEN, ATT_HEADS), 0.5),
        'od_w_in': nrm(ks[23], (N_ODD, D, OD_IN), D ** -0.5),
        'od_w_out': nrm(ks[24], (N_ODD, OD_OUT, D), OD_OUT ** -0.5),
        'ret_log_rate': ret_base + nrm(ks[25], (N_ODD, 2, RET_HEADS), 0.05),
        'final_g': 1.0 + nrm(ks[26], (D,), 0.02),
    }


def reference(x, c, ctx, c_ctx, ada_w, ada_b, norm_mix_g, norm_mlp_g, mlp_w1, mlp_w2,
              ev_w_in, ev_w_out, hy_conv_w, hy_conv_b, hy_w1, hy_b1, hy_w2, hy_b2, hy_w3, hy_freq,
              hy_decay, hy_bias, attn_sink, od_w_in, od_w_out, ret_log_rate, final_g):
    n_tok = x.shape[1]
    rows = n_tok // GRID_W
    grid_r, grid_c = jnp.meshgrid(jnp.arange(rows, dtype=jnp.float32),
                                  jnp.arange(GRID_W, dtype=jnp.float32), indexing='ij')
    pos_row = grid_r.reshape(-1)
    pos_col = grid_c.reshape(-1)
    c_act = jax.nn.silu(c)
    cc_act = jax.nn.silu(c_ctx)

    for i in range(DEPTH):
        need_ctx_out = i < DEPTH - 1
        mod_l = (c_act @ ada_w[i] + ada_b[i])[:, None, :]
        mod_c = cc_act @ ada_w[i] + ada_b[i]
        sh_a, sc_a, g_a, sh_m, sc_m, g_m = jnp.split(mod_l, N_MOD, axis=-1)
        csh_a, csc_a, cg_a, csh_m, csc_m, cg_m = jnp.split(mod_c, N_MOD, axis=-1)
        h_l = modulate(rms_norm(x, norm_mix_g[i]), sh_a, sc_a)
        h_c = modulate(rms_norm(ctx, norm_mix_g[i]), csh_a, csc_a)
        j = i // 2
        if i % 2 == 0:
            y_c, y_l = even_mixer(h_c, h_l, ev_w_in[j], ev_w_out[j], hy_conv_w[j], hy_conv_b[j],
                                  hy_w1[j], hy_b1[j], hy_w2[j], hy_b2[j], hy_w3[j], hy_freq[j],
                                  hy_decay[j], hy_bias[j], attn_sink[j], pos_row, pos_col, need_ctx_out)
        else:
            y_c, y_l = retention_mixer(h_c, h_l, od_w_in[j], od_w_out[j], ret_log_rate[j], need_ctx_out)
        x = x + g_a * y_l
        x = x + g_m * sq_relu_mlp(modulate(rms_norm(x, norm_mlp_g[i]), sh_m, sc_m), mlp_w1[i], mlp_w2[i])
        if need_ctx_out:
            ctx = ctx + cg_a * y_c
            ctx = ctx + cg_m * sq_relu_mlp(modulate(rms_norm(ctx, norm_mlp_g[i]), csh_m, csc_m), mlp_w1[i], mlp_w2[i])

    return rms_norm(x, final_g)
```

```python
import functools
import math

import numpy as np
import jax
import jax.numpy as jnp
from jax import lax
from jax.experimental import pallas as pl
from jax.experimental.pallas import tpu as pltpu

F32 = jnp.float32
BF16 = jnp.bfloat16
HIGHEST = lax.Precision.HIGHEST

D_MODEL = 1024
BATCH = 2
SEQ = 8192
DEPTH = 2
GRID_W = 64
CTX_LEN = 256
EPS = 1e-6
NEG_INF = -1e30
N_MOD = 6
D_FF = 4 * D_MODEL
ROPE_BASE = 10000.0

HY_WIDTH = D_MODEL // 2
HY_EMB = 33
HY_BANDS = (HY_EMB - 1) // 2
HY_HIDDEN = 64

ATT_HEADS = 8
ATT_KV_HEADS = 2
ATT_GROUP = ATT_HEADS // ATT_KV_HEADS
HEAD_DIM = 64
ATT_WIDTH = ATT_HEADS * HEAD_DIM
KV_WIDTH = ATT_KV_HEADS * HEAD_DIM
BLOCK = 128

RET_HEADS = 4
RET_DK = D_MODEL // RET_HEADS
RET_DV = 2 * RET_DK
RET_QK = RET_HEADS * RET_DK
RET_V = RET_HEADS * RET_DV
OD_IN = 2 * RET_QK + 3 * RET_V

LANES = 128
MIB = 1024 * 1024

EV_U = 3 * HY_WIDTH
EV_QKV = ATT_WIDTH + 4 * LANES
EV_COLS = EV_U + EV_QKV

FFT_N = 2 * SEQ
FFT_R = 128
FFT_PITCH = FFT_R + 8
RET_C = 256


def _params(sem, vmem_mib):
    return pltpu.CompilerParams(dimension_semantics=sem, vmem_limit_bytes=vmem_mib * MIB)


@functools.lru_cache(maxsize=None)
def _rope2d_tables():
    quarter = HEAD_DIM // 4
    inv = ROPE_BASE ** (-np.arange(quarter, dtype=np.float64) / quarter)
    t = np.arange(SEQ)
    pos = np.stack([t // GRID_W, t % GRID_W], axis=1).astype(np.float64)
    lane = np.arange(HEAD_DIM)
    half = lane // (HEAD_DIM // 2)
    e = lane % (HEAD_DIM // 2)
    ang = pos[:, half] * inv[e % quarter][None, :]
    sign = np.where(e < quarter, -1.0, 1.0)[None, :]
    cos = np.tile(np.cos(ang), (1, 2)).astype(np.float32)
    sin = np.tile(np.sin(ang) * sign, (1, 2)).astype(np.float32)
    return cos, sin


@functools.lru_cache(maxsize=None)
def _rope1d_tables():
    n = RET_DK // 2
    inv = ROPE_BASE ** (-np.linspace(0.0, 1.0, n))
    pos = np.arange(CTX_LEN + SEQ, dtype=np.float64)
    ang = pos[:, None] * inv[None, :]
    cos, sin = np.cos(ang), np.sin(ang)

    def rev(a):
        return a.reshape(-1, RET_C, n)[:, ::-1].reshape(-1, n)

    cos2 = np.stack([cos, rev(cos)]).astype(np.float32)
    sin2 = np.stack([sin, rev(sin)]).astype(np.float32)
    return cos2, sin2


@functools.lru_cache(maxsize=None)
def _fft_mats():
    N, R = FFT_N, FFT_R
    k1 = np.arange(R)
    n2 = np.arange(R)[:, None, None]
    n1 = np.arange(R)[None, None, :]
    n = n2 + R * n1
    idx = (k1[None, :, None] * n) % N
    ang = 2.0 * np.pi * idx / N
    c, s = np.cos(ang), np.sin(ang)
    m1 = np.concatenate([c, -s], axis=1).astype(np.float32)
    m1i = np.concatenate([c.transpose(0, 2, 1), -s.transpose(0, 2, 1)], axis=2)
    m1i = m1i[:, :R // 2].astype(np.float32)
    a2 = 2.0 * np.pi * ((np.arange(R)[:, None] * np.arange(R)[None, :]) % R) / R
    fr, fi = np.cos(a2), -np.sin(a2)
    f2 = np.block([[fr, -fi], [fi, fr]]).astype(np.float32)
    g2 = (np.block([[fr, fi], [-fi, fr]]) / N).astype(np.float32)
    return m1, m1i, f2, g2


@functools.lru_cache(maxsize=None)
def _ctx_fft_mats():
    N = 2 * CTX_LEN
    k = np.arange(N)[:, None]
    n = np.arange(N)[None, :]
    ang = 2.0 * np.pi * ((k * n) % N) / N
    c, s = np.cos(ang), np.sin(ang)
    fwd = np.concatenate([c, -s], axis=0).astype(np.float32)
    inv = (np.concatenate([c, -s], axis=1)[:CTX_LEN] / N).astype(np.float32)
    return fwd, inv


@functools.lru_cache(maxsize=None)
def _hyena_bands():
    b = np.zeros((1, LANES), np.float32)
    bands = np.linspace(1e-4, HY_BANDS - 1, HY_BANDS)
    b[0, 1:1 + HY_BANDS] = bands
    b[0, 1 + HY_BANDS:1 + 2 * HY_BANDS] = bands
    return b


def _ada_kernel(c_ref, w_ref, b_ref, o_ref):
    c = c_ref[...]
    a = c * (1.0 / (1.0 + jnp.exp(-c)))
    o_ref[...] = jnp.dot(a, w_ref[...], precision=HIGHEST, preferred_element_type=F32) + b_ref[...]


def _ada(cvec, ada_w, ada_b):
    tn = 1536
    return pl.pallas_call(
        _ada_kernel,
        out_shape=jax.ShapeDtypeStruct((DEPTH, 8, N_MOD * D_MODEL), F32),
        grid=(DEPTH, N_MOD * D_MODEL // tn),
        in_specs=[
            pl.BlockSpec((8, D_MODEL), lambda i, j: (0, 0)),
            pl.BlockSpec((None, D_MODEL, tn), lambda i, j: (i, 0, j)),
            pl.BlockSpec((None, 1, tn), lambda i, j: (i, 0, j)),
        ],
        out_specs=pl.BlockSpec((None, 8, tn), lambda i, j: (i, 0, j)),
        compiler_params=_params(("parallel", "parallel"), 40),
        name="ada_mod",
    )(cvec, ada_w, ada_b.reshape(DEPTH, 1, N_MOD * D_MODEL))


def _mod_row(ref, row):
    if isinstance(row, int):
        return ref[row:row + 1, :]
    return ref[pl.ds(row, 1), :]


def _norm_mod(x, g, shift, scale):
    y = x * lax.rsqrt(jnp.mean(x * x, axis=-1, keepdims=True) + EPS)
    return (y * g) * (1.0 + scale) + shift


def _rope_tile(x, cos, sin_signed):
    lane = lax.broadcasted_iota(jnp.int32, x.shape, 1)
    first = (lane % 32) < 16
    partner = jnp.where(first, pltpu.roll(x, LANES - 16, axis=1), pltpu.roll(x, 16, axis=1))
    return x * cos + partner * sin_signed


def _inproj0_kernel(x_ref, g_ref, sh_ref, sc_ref, w_ref, cos_ref, sin_ref, u_ref, qkv_ref, *, row, rope):
    r = pl.program_id(0) if row is None else row
    h = _norm_mod(x_ref[...], g_ref[...], _mod_row(sh_ref, r), _mod_row(sc_ref, r)).astype(BF16)
    tn = 512
    for j in range(EV_U // tn):
        u_ref[:, j * tn:(j + 1) * tn] = jnp.dot(h, w_ref[:, j * tn:(j + 1) * tn], preferred_element_type=F32)
    n_rot = (ATT_WIDTH + 2 * LANES) // LANES
    for t in range(EV_QKV // LANES):
        c0 = EV_U + t * LANES
        y = jnp.dot(h, w_ref[:, c0:c0 + LANES], preferred_element_type=F32)
        if rope and t < n_rot:
            y = _rope_tile(y, cos_ref[...], sin_ref[...])
        qkv_ref[:, t * LANES:(t + 1) * LANES] = y.astype(BF16)


def _inproj0(x, g, mod, w, layer, *, is_ctx, tm):
    G, R, _ = x.shape
    cos, sin = _rope2d_tables()
    cos, sin = jnp.asarray(cos), jnp.asarray(sin)
    if is_ctx:
        tab = pl.BlockSpec((tm, LANES), lambda b, i: (0, 0))
    else:
        tab = pl.BlockSpec((tm, LANES), lambda b, i: (i, 0))
    kern = functools.partial(_inproj0_kernel, row=2 if is_ctx else None, rope=not is_ctx)
    return pl.pallas_call(
        kern,
        out_shape=(jax.ShapeDtypeStruct((G, R, EV_U), F32), jax.ShapeDtypeStruct((G, R, EV_QKV), BF16)),
        grid=(G, R // tm),
        in_specs=[
            pl.BlockSpec((None, tm, D_MODEL), lambda b, i: (b, i, 0)),
            pl.BlockSpec((1, D_MODEL), lambda b, i: (0, 0)),
            pl.BlockSpec((None, 8, D_MODEL), lambda b, i: (layer, 0, 0)),
            pl.BlockSpec((None, 8, D_MODEL), lambda b, i: (layer, 0, 1)),
            pl.BlockSpec((D_MODEL, EV_COLS), lambda b, i: (0, 0)),
            tab, tab,
        ],
        out_specs=(pl.BlockSpec((None, tm, EV_U), lambda b, i: (b, i, 0)),
                   pl.BlockSpec((None, tm, EV_QKV), lambda b, i: (b, i, 0))),
        compiler_params=_params(("parallel", "parallel"), 48),
        name="inproj0_ctx" if is_ctx else "inproj0",
    )(x, g, mod, mod, w, cos, sin)


def _inproj1_kernel(x_ref, g_ref, sh_ref, sc_ref, w_ref, o_ref, h_scr, *, row):
    r = pl.program_id(0) if row is None else row

    @pl.when(pl.program_id(2) == 0)
    def _():
        h_scr[...] = _norm_mod(x_ref[...], g_ref[...], _mod_row(sh_ref, r), _mod_row(sc_ref, r)).astype(BF16)

    o_ref[...] = jnp.dot(h_scr[...], w_ref[...], preferred_element_type=F32).astype(o_ref.dtype)


def _inproj1(x, g, mod, w, layer, *, is_ctx, tm, tn):
    G, R, _ = x.shape
    N = w.shape[1]
    kern = functools.partial(_inproj1_kernel, row=2 if is_ctx else None)
    return pl.pallas_call(
        kern,
        out_shape=jax.ShapeDtypeStruct((G, R, N), BF16),
        grid=(G, R // tm, N // tn),
        in_specs=[
            pl.BlockSpec((None, tm, D_MODEL), lambda b, i, j: (b, i, 0)),
            pl.BlockSpec((1, D_MODEL), lambda b, i, j: (0, 0)),
            pl.BlockSpec((None, 8, D_MODEL), lambda b, i, j: (layer, 0, 0)),
            pl.BlockSpec((None, 8, D_MODEL), lambda b, i, j: (layer, 0, 1)),
            pl.BlockSpec((D_MODEL, tn), lambda b, i, j: (0, j)),
        ],
        out_specs=pl.BlockSpec((None, tm, tn), lambda b, i, j: (b, i, j)),
        scratch_shapes=[pltpu.VMEM((tm, D_MODEL), BF16)],
        compiler_params=_params(("parallel", "parallel", "arbitrary"), 48),
        name="inproj1_ctx" if is_ctx else "inproj1",
    )(x, g, mod, mod, w)


def _hyena_pre_kernel(u_ref, up_ref, un_ref, w_ref, b_ref, bias_ref, z_ref, x0_ref, zb_ref, *, n_tiles):
    i = pl.program_id(1)
    u = u_ref[...]
    tm = u.shape[0]
    prev_row = jnp.where(i > 0, up_ref[7:8, :], 0.0)
    next_row = jnp.where(i < n_tiles - 1, un_ref[0:1, :], 0.0)
    row = lax.broadcasted_iota(jnp.int32, u.shape, 0)
    before = jnp.where(row == 0, prev_row, pltpu.roll(u, 1, axis=0))
    after = jnp.where(row == tm - 1, next_row, pltpu.roll(u, tm - 1, axis=0))
    conv = before * w_ref[0:1, :] + u * w_ref[1:2, :] + after * w_ref[2:3, :] + b_ref[...]
    x0 = conv[:, :HY_WIDTH]
    z = conv[:, 2 * HY_WIDTH:] * conv[:, HY_WIDTH:2 * HY_WIDTH]
    z_ref[...] = z
    x0_ref[...] = x0
    zb_ref[...] = x0 * (z * bias_ref[...])


def _hyena_pre(u, conv_w, conv_b, bias, *, tm):
    G, R, _ = u.shape
    n_tiles = R // tm
    sub = tm // 8
    last8 = R // 8 - 1
    kern = functools.partial(_hyena_pre_kernel, n_tiles=n_tiles)
    out = jax.ShapeDtypeStruct((G, R, HY_WIDTH), F32)
    ospec = pl.BlockSpec((None, tm, HY_WIDTH), lambda b, i: (b, i, 0))
    return pl.pallas_call(
        kern,
        out_shape=(out, out, out),
        grid=(G, n_tiles),
        in_specs=[
            pl.BlockSpec((None, tm, EV_U), lambda b, i: (b, i, 0)),
            pl.BlockSpec((None, 8, EV_U), lambda b, i: (b, jnp.maximum(i * sub - 1, 0), 0)),
            pl.BlockSpec((None, 8, EV_U), lambda b, i: (b, jnp.minimum((i + 1) * sub, last8), 0)),
            pl.BlockSpec((3, EV_U), lambda b, i: (0, 0)),
            pl.BlockSpec((1, EV_U), lambda b, i: (0, 0)),
            pl.BlockSpec((1, HY_WIDTH), lambda b, i: (0, 0)),
        ],
        out_specs=(ospec, ospec, ospec),
        compiler_params=_params(("parallel", "parallel"), 48),
        name="hyena_pre",
    )(u, u, u, conv_w, conv_b, bias)


def _filter_kernel(w1_ref, b1_ref, w2_ref, b2_ref, w3_ref, fr_ref, dec_ref, bands_ref, h_ref, ss_ref, *, L, tm):
    i = pl.program_id(0)
    r = lax.broadcasted_iota(jnp.int32, (tm, 1), 0) + i * tm
    m = jnp.where(r < L, r, 2 * L - r)
    t = m.astype(F32) / L
    lane = lax.broadcasted_iota(jnp.int32, (tm, LANES), 1)
    ang = (2.0 * math.pi * t) * bands_ref[...]
    feat = jnp.where(lane == 0, t,
                     jnp.where(lane <= HY_BANDS, jnp.cos(ang),
                               jnp.where(lane <= 2 * HY_BANDS, -jnp.sin(ang), 0.0)))
    fr = fr_ref[...]
    hid = jnp.sin(fr * (jnp.dot(feat, w1_ref[...], precision=HIGHEST, preferred_element_type=F32) + b1_ref[...]))
    hid = jnp.sin(fr * (jnp.dot(hid, w2_ref[...], precision=HIGHEST, preferred_element_type=F32) + b2_ref[...]))
    h2 = jnp.dot(hid, w3_ref[...], precision=HIGHEST, preferred_element_type=F32)
    neg = r > L
    h = jnp.where(neg, h2[:, HY_WIDTH:], h2[:, :HY_WIDTH])
    dec = jnp.where(neg, jnp.abs(dec_ref[1:2, :]), jnp.abs(dec_ref[0:1, :]))
    h = h * jnp.exp(-t * dec)
    h = jnp.where(r == L, 0.0, h)
    h_ref[...] = h

    @pl.when(i == 0)
    def _():
        ss_ref[...] = jnp.zeros_like(ss_ref)

    ss_ref[...] += jnp.broadcast_to(jnp.sum(h * h, axis=0, keepdims=True), ss_ref.shape)


def _hyena_filter(w1p, b1, w2, b2, w3, freq, decay, *, L):
    tm = min(2 * L, 1024)
    kern = functools.partial(_filter_kernel, L=L, tm=tm)
    full = lambda a: pl.BlockSpec(a.shape, lambda i: (0,) * a.ndim)
    bands = jnp.asarray(_hyena_bands())
    args = (w1p, b1, w2, b2, w3, freq, decay, bands)
    return pl.pallas_call(
        kern,
        out_shape=(jax.ShapeDtypeStruct((2 * L, HY_WIDTH), F32), jax.ShapeDtypeStruct((8, HY_WIDTH), F32)),
        grid=(2 * L // tm,),
        in_specs=[full(a) for a in args],
        out_specs=(pl.BlockSpec((tm, HY_WIDTH), lambda i: (i, 0)), pl.BlockSpec((8, HY_WIDTH), lambda i: (0, 0))),
        compiler_params=_params(("arbitrary",), 48),
        name="hyena_filter_%d" % L,
    )(*args)


def _stack_bf16(re, im):
    return jnp.concatenate([re, im], axis=0).astype(BF16)


def _filter_fft_kernel(h_ref, ss_ref, m1_ref, f2_ref, o_ref, a_scr):
    R, P = FFT_R, FFT_PITCH
    inv_norm = lax.rsqrt(ss_ref[0:1, :])

    def stage1(n2, carry):
        xs = (h_ref[pl.ds(n2, R, stride=R), :] * inv_norm).astype(BF16)
        a = jnp.dot(m1_ref[n2], xs, preferred_element_type=F32)
        base = pl.multiple_of(n2 * P, 8)
        a_scr[0, pl.ds(base, R), :] = a[:R]
        a_scr[1, pl.ds(base, R), :] = a[R:]
        return carry

    lax.fori_loop(0, R, stage1, 0)

    def stage2(k1, carry):
        ar = a_scr[0, pl.ds(k1, R, stride=P), :]
        ai = a_scr[1, pl.ds(k1, R, stride=P), :]
        x = jnp.dot(f2_ref[...], _stack_bf16(ar, ai), preferred_element_type=F32)
        base = pl.multiple_of(k1 * R, R)
        o_ref[0, pl.ds(base, R), :] = x[:R].astype(BF16)
        o_ref[1, pl.ds(base, R), :] = x[R:].astype(BF16)
        return carry

    lax.fori_loop(0, R, stage2, 0)


def _filter_fft(h_circ, ss, m1, f2):
    ct = LANES
    once = pl.Buffered(1)
    return pl.pallas_call(
        _filter_fft_kernel,
        out_shape=jax.ShapeDtypeStruct((2, FFT_N, HY_WIDTH), BF16),
        grid=(HY_WIDTH // ct,),
        in_specs=[
            pl.BlockSpec((FFT_N, ct), lambda c: (0, c), pipeline_mode=once),
            pl.BlockSpec((8, ct), lambda c: (0, c)),
            pl.BlockSpec(m1.shape, lambda c: (0, 0, 0), pipeline_mode=once),
            pl.BlockSpec(f2.shape, lambda c: (0, 0), pipeline_mode=once),
        ],
        out_specs=pl.BlockSpec((2, FFT_N, ct), lambda c: (0, 0, c)),
        scratch_shapes=[pltpu.VMEM((2, FFT_R * FFT_PITCH, ct), F32)],
        compiler_params=_params(("parallel",), 58),
        name="hyena_filter_fft",
    )(h_circ, ss, m1, f2)


def _fft_conv_kernel(z_ref, hs_ref, m1_ref, f2_ref, g2_ref, m1i_ref, y_ref, a_scr):
    R, P = FFT_R, FFT_PITCH

    def stage1(n2, carry):
        xs = z_ref[pl.ds(n2, R // 2, stride=R), :].astype(BF16)
        a = jnp.dot(m1_ref[n2, :, 0:R // 2], xs, preferred_element_type=F32)
        base = pl.multiple_of(n2 * P, 8)
        a_scr[0, pl.ds(base, R), :] = a[:R]
        a_scr[1, pl.ds(base, R), :] = a[R:]
        return carry

    lax.fori_loop(0, R, stage1, 0)

    def stage2(k1, carry):
        ar = a_scr[0, pl.ds(k1, R, stride=P), :]
        ai = a_scr[1, pl.ds(k1, R, stride=P), :]
        x = jnp.dot(f2_ref[...], _stack_bf16(ar, ai), preferred_element_type=F32)
        base = pl.multiple_of(k1 * R, R)
        hr = hs_ref[0, pl.ds(base, R), :].astype(F32)
        hi = hs_ref[1, pl.ds(base, R), :].astype(F32)
        xr, xi = x[:R], x[R:]
        yr = xr * hr - xi * hi
        yi = xr * hi + xi * hr
        bq = jnp.dot(g2_ref[...], _stack_bf16(yr, yi), preferred_element_type=F32)
        a_scr[0, pl.ds(k1, R, stride=P), :] = bq[:R]
        a_scr[1, pl.ds(k1, R, stride=P), :] = bq[R:]
        return carry

    lax.fori_loop(0, R, stage2, 0)

    def stage3(n2, carry):
        base = pl.multiple_of(n2 * P, 8)
        br = a_scr[0, pl.ds(base, R), :]
        bi = a_scr[1, pl.ds(base, R), :]
        y = jnp.dot(m1i_ref[n2], _stack_bf16(br, bi), preferred_element_type=F32)
        y_ref[pl.ds(n2, R // 2, stride=R), :] = y
        return carry

    lax.fori_loop(0, R, stage3, 0)


def _fft_conv(z, hspec, m1, f2, g2, m1i):
    G = z.shape[0]
    ct = LANES
    once = pl.Buffered(1)
    return pl.pallas_call(
        _fft_conv_kernel,
        out_shape=jax.ShapeDtypeStruct((G, SEQ, HY_WIDTH), F32),
        grid=(HY_WIDTH // ct, G),
        in_specs=[
            pl.BlockSpec((None, SEQ, ct), lambda c, b: (b, 0, c)),
            pl.BlockSpec((2, FFT_N, ct), lambda c, b: (0, 0, c), pipeline_mode=once),
            pl.BlockSpec(m1.shape, lambda c, b: (0, 0, 0), pipeline_mode=once),
            pl.BlockSpec(f2.shape, lambda c, b: (0, 0), pipeline_mode=once),
            pl.BlockSpec(g2.shape, lambda c, b: (0, 0), pipeline_mode=once),
            pl.BlockSpec(m1i.shape, lambda c, b: (0, 0, 0), pipeline_mode=once),
        ],
        out_specs=pl.BlockSpec((None, SEQ, ct), lambda c, b: (b, 0, c)),
        scratch_shapes=[pltpu.VMEM((2, FFT_R * FFT_PITCH, ct), F32)],
        compiler_params=_params(("parallel", "arbitrary"), 58),
        name="hyena_fft_conv",
    )(z, hspec, m1, f2, g2, m1i)


def _ctx_conv_kernel(z_ref, h_ref, ss_ref, fwd_ref, inv_ref, y_ref):
    N = 2 * CTX_LEN
    hn = (h_ref[...] * lax.rsqrt(ss_ref[0:1, :])).astype(BF16)
    hs = jnp.dot(fwd_ref[...], hn, preferred_element_type=F32)
    zs = jnp.dot(fwd_ref[:, :CTX_LEN], z_ref[...].astype(BF16), preferred_element_type=F32)
    hr, hi, zr, zi = hs[:N], hs[N:], zs[:N], zs[N:]
    ys = _stack_bf16(zr * hr - zi * hi, zr * hi + zi * hr)
    y_ref[...] = jnp.dot(inv_ref[...], ys, preferred_element_type=F32)


def _ctx_conv(z, h_circ, ss, fwd, inv):
    G = z.shape[0]
    full = lambda a: pl.BlockSpec(a.shape, lambda b: (0,) * a.ndim)
    return pl.pallas_call(
        _ctx_conv_kernel,
        out_shape=jax.ShapeDtypeStruct((G, CTX_LEN, HY_WIDTH), F32),
        grid=(G,),
        in_specs=[pl.BlockSpec((None, CTX_LEN, HY_WIDTH), lambda b: (b, 0, 0)),
                  full(h_circ), full(ss), full(fwd), full(inv)],
        out_specs=pl.BlockSpec((None, CTX_LEN, HY_WIDTH), lambda b: (b, 0, 0)),
        compiler_params=_params(("parallel",), 48),
        name="hyena_ctx_conv",
    )(z, h_circ, ss, fwd, inv)


def _attn_kernel(sink_ref, q_ref, *refs, local, n_blocks):
    if local:
        kp_ref, kc_ref, kn_ref, kx_ref, vp_ref, vc_ref, vn_ref, vx_ref, o_ref = refs
    else:
        kx_ref, vx_ref, o_ref = refs
    n = pl.program_id(1)
    q = q_ref[...]
    lane = lax.broadcasted_iota(jnp.int32, (BLOCK, LANES), 1)
    low = lane < HEAD_DIM
    zero = jnp.zeros((), BF16)
    scale = HEAD_DIM ** -0.5
    n_ctx = kx_ref.shape[0]

    if local:
        qi = lax.broadcasted_iota(jnp.int32, (BLOCK, BLOCK), 0)
        ki = lax.broadcasted_iota(jnp.int32, (BLOCK, BLOCK), 1)
        ok_prev = (ki >= qi) & (n > 0)
        ok_next = (ki <= qi) & (n < n_blocks - 1)
        ok = jnp.concatenate([ok_prev, jnp.ones((BLOCK, BLOCK), jnp.bool_), ok_next,
                              jnp.ones((BLOCK, n_ctx), jnp.bool_)], axis=1)
        ok = jnp.concatenate([ok] * ATT_GROUP, axis=0)

    for g in range(ATT_KV_HEADS):
        kcols = slice(g * LANES, (g + 1) * LANES)
        if local:
            k = jnp.concatenate([kp_ref[:, kcols], kc_ref[:, kcols], kn_ref[:, kcols], kx_ref[:, kcols]], axis=0)
            v = jnp.concatenate([vp_ref[:, kcols], vc_ref[:, kcols], vn_ref[:, kcols], vx_ref[:, kcols]], axis=0)
        else:
            k, v = kx_ref[:, kcols], vx_ref[:, kcols]
        parts = []
        for hh in range(ATT_GROUP):
            h = g * ATT_GROUP + hh
            tile = q[:, (h // 2) * LANES:(h // 2 + 1) * LANES]
            parts.append(jnp.where(low if h % 2 == 0 else ~low, tile, zero))
        qs = jnp.concatenate(parts, axis=0)
        s = lax.dot_general(qs, k, (((1,), (1,)), ((), ())), preferred_element_type=F32) * scale
        if local:
            s = jnp.where(ok, s, NEG_INF)
        sink = jnp.concatenate(
            [jnp.full((BLOCK, 1), sink_ref[g * ATT_GROUP + hh], F32) for hh in range(ATT_GROUP)], axis=0)
        m = jnp.maximum(jnp.max(s, axis=-1, keepdims=True), sink)
        e = jnp.exp(s - m)
        denom = jnp.sum(e, axis=-1, keepdims=True) + jnp.exp(sink - m)
        o = jnp.dot(e.astype(BF16), v, preferred_element_type=F32) / denom
        for pair in range(ATT_GROUP // 2):
            even = o[(2 * pair) * BLOCK:(2 * pair + 1) * BLOCK]
            odd = o[(2 * pair + 1) * BLOCK:(2 * pair + 2) * BLOCK]
            t = g * (ATT_GROUP // 2) + pair
            o_ref[:, t * LANES:(t + 1) * LANES] = jnp.where(low, even, odd).astype(o_ref.dtype)


def _attention(sink, qkv, qkv_ctx, *, local):
    src = qkv if local else qkv_ctx
    G, R, _ = src.shape
    nb = R // BLOCK
    kcol, vcol = ATT_WIDTH // 256, ATT_WIDTH // 256 + 1
    q_spec = pl.BlockSpec((None, BLOCK, ATT_WIDTH), lambda b, n: (b, n, 0))
    ctx_k = pl.BlockSpec((None, CTX_LEN, 256), lambda b, n: (b, 0, kcol))
    ctx_v = pl.BlockSpec((None, CTX_LEN, 256), lambda b, n: (b, 0, vcol))
    if local:
        def nbr(col, off):
            return pl.BlockSpec((None, BLOCK, 256), lambda b, n: (b, jnp.clip(n + off, 0, nb - 1), col))
        in_specs = [q_spec, nbr(kcol, -1), nbr(kcol, 0), nbr(kcol, 1), ctx_k,
                    nbr(vcol, -1), nbr(vcol, 0), nbr(vcol, 1), ctx_v]
        args = (qkv, qkv, qkv, qkv, qkv_ctx, qkv, qkv, qkv, qkv_ctx)
    else:
        in_specs = [q_spec, ctx_k, ctx_v]
        args = (qkv_ctx, qkv_ctx, qkv_ctx)
    kern = functools.partial(_attn_kernel, local=local, n_blocks=nb)
    return pl.pallas_call(
        kern,
        out_shape=jax.ShapeDtypeStruct((G, R, ATT_WIDTH), BF16),
        grid=(G, nb),
        in_specs=[pl.BlockSpec(memory_space=pltpu.SMEM)] + in_specs,
        out_specs=pl.BlockSpec((None, BLOCK, ATT_WIDTH), lambda b, n: (b, n, 0)),
        compiler_params=_params(("parallel", "parallel"), 48),
        name="attn_local" if local else "attn_ctx",
    )(sink, *args)


def _mlp_step(f, n_f, x_ref, gm_ref, shm_ref, scm_ref, gtm_ref, w1_ref, w2_ref, fg_ref, o_ref,
              x1_scr, h_scr, acc_scr, r, final_norm):
    a = jnp.dot(h_scr[...], w1_ref[...], preferred_element_type=F32)
    a = jnp.maximum(a, 0.0)
    acc_scr[...] += jnp.dot((a * a).astype(BF16), w2_ref[...], preferred_element_type=F32)

    @pl.when(f == n_f - 1)
    def _():
        out = x1_scr[...] + _mod_row(gtm_ref, r) * acc_scr[...]
        if final_norm:
            out = (out * lax.rsqrt(jnp.mean(out * out, axis=-1, keepdims=True) + EPS)) * fg_ref[...]
        o_ref[...] = out


def _mix_prologue(y, x_ref, gta_ref, gm_ref, shm_ref, scm_ref, x1_scr, h_scr, acc_scr, r):
    x1 = x_ref[...] + _mod_row(gta_ref, r) * y
    x1_scr[...] = x1
    h_scr[...] = _norm_mod(x1, gm_ref[...], _mod_row(shm_ref, r), _mod_row(scm_ref, r)).astype(BF16)
    acc_scr[...] = jnp.zeros_like(acc_scr)


def _mix0_kernel(x_ref, x0_ref, y_ref, zb_ref, att_ref, wo_ref, gta_ref, gm_ref, shm_ref, scm_ref, gtm_ref,
                 w1_ref, w2_ref, fg_ref, o_ref, x1_scr, h_scr, acc_scr, *, row, n_f, final_norm):
    r = pl.program_id(0) if row is None else row
    f = pl.program_id(2)

    @pl.when(f == 0)
    def _():
        hy = (x0_ref[...] * y_ref[...] + zb_ref[...]).astype(BF16)
        y = (jnp.dot(hy, wo_ref[:HY_WIDTH, :], preferred_element_type=F32)
             + jnp.dot(att_ref[...], wo_ref[HY_WIDTH:, :], preferred_element_type=F32))
        _mix_prologue(y, x_ref, gta_ref, gm_ref, shm_ref, scm_ref, x1_scr, h_scr, acc_scr, r)

    _mlp_step(f, n_f, x_ref, gm_ref, shm_ref, scm_ref, gtm_ref, w1_ref, w2_ref, fg_ref, o_ref,
              x1_scr, h_scr, acc_scr, r, final_norm)


def _mix1_kernel(x_ref, of_ref, ob_ref, wo_ref, gta_ref, gm_ref, shm_ref, scm_ref, gtm_ref,
                 w1_ref, w2_ref, fg_ref, o_ref, x1_scr, h_scr, acc_scr, *, row, n_f, final_norm):
    r = pl.program_id(0) if row is None else row
    f = pl.program_id(2)

    @pl.when(f == 0)
    def _():
        a = (of_ref[...].astype(F32) + ob_ref[...].astype(F32)).astype(BF16)
        y = jnp.dot(a, wo_ref[...], preferred_element_type=F32)
        _mix_prologue(y, x_ref, gta_ref, gm_ref, shm_ref, scm_ref, x1_scr, h_scr, acc_scr, r)

    _mlp_step(f, n_f, x_ref, gm_ref, shm_ref, scm_ref, gtm_ref, w1_ref, w2_ref, fg_ref, o_ref,
              x1_scr, h_scr, acc_scr, r, final_norm)


def _mix_mlp(kind, x, mix_in, wo, mod, gm, w1, w2, fg, layer, *, is_ctx, tm, fc, final_norm):
    G, R, _ = x.shape
    n_f = D_FF // fc
    row_spec = lambda w: pl.BlockSpec((None, tm, w), lambda b, i, f: (b, i, 0))
    modk = lambda k: pl.BlockSpec((None, 8, D_MODEL), lambda b, i, f: (layer, 0, k))
    vec = pl.BlockSpec((1, D_MODEL), lambda b, i, f: (0, 0))
    if kind == 0:
        body = _mix0_kernel
        mix_specs = [row_spec(HY_WIDTH)] * 3 + [row_spec(ATT_WIDTH)]
    else:
        body = _mix1_kernel
        mix_specs = [pl.BlockSpec((None, None, tm, RET_V), lambda b, i, f, d=d: (d, b, i, 0)) for d in (0, 1)]
    kern = functools.partial(body, row=2 if is_ctx else None, n_f=n_f, final_norm=final_norm)
    return pl.pallas_call(
        kern,
        out_shape=jax.ShapeDtypeStruct((G, R, D_MODEL), F32),
        grid=(G, R // tm, n_f),
        in_specs=[row_spec(D_MODEL)] + mix_specs + [
            pl.BlockSpec(wo.shape, lambda b, i, f: (0, 0)),
            modk(2), vec, modk(3), modk(4), modk(5),
            pl.BlockSpec((D_MODEL, fc), lambda b, i, f: (0, f)),
            pl.BlockSpec((fc, D_MODEL), lambda b, i, f: (f, 0)),
            vec,
        ],
        out_specs=row_spec(D_MODEL),
        scratch_shapes=[pltpu.VMEM((tm, D_MODEL), F32), pltpu.VMEM((tm, D_MODEL), BF16),
                        pltpu.VMEM((tm, D_MODEL), F32)],
        compiler_params=_params(("parallel", "parallel", "arbitrary"), 56),
        name="mix_mlp%d%s" % (kind, "_ctx" if is_ctx else ""),
    )(x, *mix_in, wo, mod, gm, mod, mod, mod, w1, w2, fg)


def _ret_kernel(lr_ref, pc_q, pc_k, pc_v, pl_q, pl_k, pl_v, pl_g, cos_ref, sin_ref, o_ref,
                state, dmask, xi, zeta, gch):
    h = pl.program_id(1)
    d = pl.program_id(2)
    j = pl.program_id(3)
    C = RET_C

    @pl.when(j == 0)
    def _():
        state[...] = jnp.zeros_like(state)
        lg = -jnp.exp(jnp.full((C, C), lr_ref[d, h], F32))
        sgn = jnp.where(d == 0, 1.0, -1.0)
        row = lax.broadcasted_iota(jnp.int32, (C, C), 0).astype(F32)
        col = lax.broadcasted_iota(jnp.int32, (C, C), 1).astype(F32)
        e = sgn * (row - col)
        dmask[...] = jnp.where(e >= 0, jnp.exp(e * lg), 0.0)
        p = jnp.where(d == 0, row, (C - 1) - row)
        xi[...] = jnp.exp((p + 1.0) * lg)
        zeta[...] = jnp.exp(((C - 1) - p) * lg)
        gch[...] = jnp.exp(C * -jnp.exp(jnp.full((8, C), lr_ref[d, h], F32)))

    def step(q_ref, k_ref, v_ref, g_ref):
        cos, sin = cos_ref[...], sin_ref[...]
        half = RET_DK // 2

        def rot(t):
            t = t.astype(F32)
            t1, t2 = t[:, :half], t[:, half:]
            return jnp.concatenate([t1 * cos - t2 * sin, t2 * cos + t1 * sin], axis=1)

        q = rot(q_ref[...])
        k = rot(k_ref[...]) * (RET_DK ** -0.5)
        v = v_ref[...]
        inner = lax.dot_general(q.astype(BF16), k.astype(BF16), (((1,), (1,)), ((), ())),
                                preferred_element_type=F32) * dmask[...]
        s_old = state[...]
        o = (jnp.dot(inner.astype(BF16), v, preferred_element_type=F32)
             + jnp.dot((q * xi[...]).astype(BF16), s_old.astype(BF16), preferred_element_type=F32))
        kv = lax.dot_general((k * zeta[...]).astype(BF16), v, (((0,), (0,)), ((), ())),
                             preferred_element_type=F32)
        state[...] = gch[0:1, 0:1] * s_old + kv
        if g_ref is not None:
            on = o * lax.rsqrt(jnp.mean(o * o, axis=-1, keepdims=True) + EPS)
            gate = g_ref[...].astype(F32)
            o_ref[...] = (gate * (1.0 / (1.0 + jnp.exp(-gate))) * on).astype(o_ref.dtype)

    @pl.when(j == 0)
    def _():
        step(pc_q, pc_k, pc_v, None)

    @pl.when(j > 0)
    def _():
        step(pl_q, pl_k, pl_v, pl_g)


def _retention(log_rate, p_ctx, p_lat):
    G = p_lat.shape[0]
    C = RET_C
    n_lat = SEQ // C
    cos, sin = _rope1d_tables()
    cos, sin = jnp.asarray(cos), jnp.asarray(sin)

    def lat_chunk(d, j):
        s = jnp.maximum(j - 1, 0)
        return jnp.where(d == 0, s, n_lat - 1 - s)

    qw, vw = RET_DK, RET_DV
    k0 = RET_QK // qw
    v0 = 2 * RET_QK // vw
    g0 = (2 * RET_QK + RET_V) // vw
    in_specs = [
        pl.BlockSpec(memory_space=pltpu.SMEM),
        pl.BlockSpec((None, C, qw), lambda b, h, d, j: (b, 0, h)),
        pl.BlockSpec((None, C, qw), lambda b, h, d, j: (b, 0, k0 + h)),
        pl.BlockSpec((None, C, vw), lambda b, h, d, j: (b, 0, v0 + h)),
        pl.BlockSpec((None, C, qw), lambda b, h, d, j: (b, lat_chunk(d, j), h)),
        pl.BlockSpec((None, C, qw), lambda b, h, d, j: (b, lat_chunk(d, j), k0 + h)),
        pl.BlockSpec((None, C, vw), lambda b, h, d, j: (b, lat_chunk(d, j), v0 + h)),
        pl.BlockSpec((None, C, vw), lambda b, h, d, j: (b, lat_chunk(d, j), g0 + RET_HEADS * d + h)),
        pl.BlockSpec((None, C, RET_DK // 2), lambda b, h, d, j: (d, j, 0)),
        pl.BlockSpec((None, C, RET_DK // 2), lambda b, h, d, j: (d, j, 0)),
    ]
    return pl.pallas_call(
        _ret_kernel,
        out_shape=jax.ShapeDtypeStruct((2, G, SEQ, RET_V), BF16),
        grid=(G, RET_HEADS, 2, 1 + n_lat),
        in_specs=in_specs,
        out_specs=pl.BlockSpec((None, None, C, vw), lambda b, h, d, j: (d, b, lat_chunk(d, j), h)),
        scratch_shapes=[pltpu.VMEM((RET_DK, RET_DV), F32), pltpu.VMEM((C, C), F32), pltpu.VMEM((C, C), F32),
                        pltpu.VMEM((C, C), F32), pltpu.VMEM((8, C), F32)],
        compiler_params=_params(("parallel", "parallel", "parallel", "arbitrary"), 48),
        name="retention",
    )(log_rate, p_ctx, p_ctx, p_ctx, p_lat, p_lat, p_lat, p_lat, cos, sin)


def _ev_weight(w_in):
    i_q = EV_U
    i_k = i_q + ATT_WIDTH
    i_v = i_k + KV_WIDTH
    cols = [w_in[:, :i_k]]
    for base in (i_k, i_v):
        for g in range(ATT_KV_HEADS):
            part = w_in[:, base + g * HEAD_DIM: base + (g + 1) * HEAD_DIM]
            cols += [part, part]
    return jnp.concatenate(cols, axis=1).astype(BF16)


def kernel(x, c, ctx, c_ctx, ada_w, ada_b, norm_mix_g, norm_mlp_g, mlp_w1, mlp_w2, ev_w_in, ev_w_out, hy_conv_w, hy_conv_b, hy_w1, hy_b1, hy_w2, hy_b2, hy_w3, hy_freq, hy_decay, hy_bias, attn_sink, od_w_in, od_w_out, ret_log_rate, final_g):
    D = D_MODEL
    cvec = jnp.concatenate([c, c_ctx[None, :], jnp.zeros((8 - BATCH - 1, D), F32)], axis=0)
    mod = _ada(cvec, ada_w, ada_b)

    m1, m1i, f2, g2 = (jnp.asarray(a).astype(BF16) for a in _fft_mats())
    cfwd, cinv = (jnp.asarray(a).astype(BF16) for a in _ctx_fft_mats())
    fg = final_g.reshape(1, D)

    gmix = norm_mix_g[0].reshape(1, D)
    gmlp = norm_mlp_g[0].reshape(1, D)
    w_in = _ev_weight(ev_w_in[0])
    u_l, qkv_l = _inproj0(x, gmix, mod, w_in, 0, is_ctx=False, tm=512)
    u_c, qkv_c = _inproj0(ctx, gmix, mod, w_in, 0, is_ctx=True, tm=CTX_LEN)

    conv_b = hy_conv_b[0].reshape(1, EV_U)
    bias = hy_bias[0].reshape(1, HY_WIDTH)
    z_l, x0_l, zb_l = _hyena_pre(u_l, hy_conv_w[0], conv_b, bias, tm=512)
    z_c, x0_c, zb_c = _hyena_pre(u_c, hy_conv_w[0], conv_b, bias, tm=CTX_LEN)

    w1p = jnp.pad(hy_w1[0], ((0, LANES - HY_EMB), (0, 0)))
    filt = (w1p, hy_b1[0].reshape(1, -1), hy_w2[0], hy_b2[0].reshape(1, -1), hy_w3[0],
            hy_freq[0].reshape(1, -1), hy_decay[0])
    h_l, ss_l = _hyena_filter(*filt, L=SEQ)
    h_c, ss_c = _hyena_filter(*filt, L=CTX_LEN)
    hspec = _filter_fft(h_l, ss_l, m1, f2)
    y_l = _fft_conv(z_l, hspec, m1, f2, g2, m1i)
    y_c = _ctx_conv(z_c, h_c, ss_c, cfwd, cinv)

    sink = attn_sink[0]
    att_l = _attention(sink, qkv_l, qkv_c, local=True)
    att_c = _attention(sink, qkv_l, qkv_c, local=False)

    wo = ev_w_out[0].astype(BF16)
    w1 = mlp_w1[0].astype(BF16)
    w2 = mlp_w2[0].astype(BF16)
    x1 = _mix_mlp(0, x, (x0_l, y_l, zb_l, att_l), wo, mod, gmlp, w1, w2, fg, 0,
                  is_ctx=False, tm=512, fc=1024, final_norm=False)
    ctx1 = _mix_mlp(0, ctx, (x0_c, y_c, zb_c, att_c), wo, mod, gmlp, w1, w2, fg, 0,
                    is_ctx=True, tm=CTX_LEN, fc=1024, final_norm=False)

    gmix = norm_mix_g[1].reshape(1, D)
    gmlp = norm_mlp_g[1].reshape(1, D)
    w_in = od_w_in[0].astype(BF16)
    p_l = _inproj1(x1, gmix, mod, w_in, 1, is_ctx=False, tm=1024, tn=2048)
    p_c = _inproj1(ctx1, gmix, mod, w_in, 1, is_ctx=True, tm=CTX_LEN, tn=2048)
    o_ret = _retention(ret_log_rate[0], p_c, p_l)

    wo = od_w_out[0].astype(BF16)
    w1 = mlp_w1[1].astype(BF16)
    w2 = mlp_w2[1].astype(BF16)
    return _mix_mlp(1, x1, (o_ret, o_ret), wo, mod, gmlp, w1, w2, fg, 1,
                    is_ctx=False, tm=512, fc=1024, final_norm=True)
```

```python
import functools
import math

import numpy as np
import jax
import jax.numpy as jnp
from jax import lax
from jax.experimental import pallas as pl
from jax.experimental.pallas import tpu as pltpu

F32 = jnp.float32
BF16 = jnp.bfloat16
HIGHEST = lax.Precision.HIGHEST

D_MODEL = 1024
BATCH = 2
SEQ = 8192
DEPTH = 2
GRID_W = 64
CTX_LEN = 256
EPS = 1e-6
NEG_INF = -1e30
N_MOD = 6
D_FF = 4 * D_MODEL
ROPE_BASE = 10000.0

HY_WIDTH = D_MODEL // 2
HY_EMB = 33
HY_BANDS = (HY_EMB - 1) // 2
HY_HIDDEN = 64

ATT_HEADS = 8
ATT_KV_HEADS = 2
ATT_GROUP = ATT_HEADS // ATT_KV_HEADS
HEAD_DIM = 64
ATT_WIDTH = ATT_HEADS * HEAD_DIM
KV_WIDTH = ATT_KV_HEADS * HEAD_DIM
BLOCK = 128

RET_HEADS = 4
RET_DK = D_MODEL // RET_HEADS
RET_DV = 2 * RET_DK
RET_QK = RET_HEADS * RET_DK
RET_V = RET_HEADS * RET_DV
OD_IN = 2 * RET_QK + 3 * RET_V

LANES = 128
MIB = 1024 * 1024

EV_U = 3 * HY_WIDTH
EV_QKV = ATT_WIDTH + 4 * LANES
EV_COLS = EV_U + EV_QKV

FFT_N = 2 * SEQ
FFT_R = 128
FFT_PITCH = FFT_R + 8
FFT_UNROLL = 16
RET_C = 256


def _params(sem, vmem_mib):
    return pltpu.CompilerParams(dimension_semantics=sem, vmem_limit_bytes=vmem_mib * MIB)


@functools.lru_cache(maxsize=None)
def _rope2d_tables():
    quarter = HEAD_DIM // 4
    inv = ROPE_BASE ** (-np.arange(quarter, dtype=np.float64) / quarter)
    t = np.arange(SEQ)
    pos = np.stack([t // GRID_W, t % GRID_W], axis=1).astype(np.float64)
    lane = np.arange(HEAD_DIM)
    half = lane // (HEAD_DIM // 2)
    e = lane % (HEAD_DIM // 2)
    ang = pos[:, half] * inv[e % quarter][None, :]
    sign = np.where(e < quarter, -1.0, 1.0)[None, :]
    cos = np.tile(np.cos(ang), (1, 2)).astype(np.float32)
    sin = np.tile(np.sin(ang) * sign, (1, 2)).astype(np.float32)
    return cos, sin


@functools.lru_cache(maxsize=None)
def _rope1d_tables():
    n = RET_DK // 2
    inv = ROPE_BASE ** (-np.linspace(0.0, 1.0, n))
    pos = np.arange(CTX_LEN + SEQ, dtype=np.float64)
    ang = pos[:, None] * inv[None, :]
    cos, sin = np.cos(ang), np.sin(ang)

    def rev(a):
        return a.reshape(-1, RET_C, n)[:, ::-1].reshape(-1, n)

    cos2 = np.stack([cos, rev(cos)]).astype(np.float32)
    sin2 = np.stack([sin, rev(sin)]).astype(np.float32)
    return cos2, sin2


@functools.lru_cache(maxsize=None)
def _fft_mats():
    N, R = FFT_N, FFT_R
    k1 = np.arange(R)
    n2 = np.arange(R)[:, None, None]
    n1 = np.arange(R)[None, None, :]
    n = n2 + R * n1
    idx = (k1[None, :, None] * n) % N
    ang = 2.0 * np.pi * idx / N
    c, s = np.cos(ang), np.sin(ang)
    m1 = np.concatenate([c, -s], axis=1).astype(np.float32)
    m1i = np.concatenate([c.transpose(0, 2, 1), -s.transpose(0, 2, 1)], axis=2)
    m1i = m1i[:, :R // 2].astype(np.float32)
    a2 = 2.0 * np.pi * ((np.arange(R)[:, None] * np.arange(R)[None, :]) % R) / R
    fr, fi = np.cos(a2), -np.sin(a2)
    f2 = np.block([[fr, -fi], [fi, fr]]).astype(np.float32)
    g2 = (np.block([[fr, fi], [-fi, fr]]) / N).astype(np.float32)
    return m1, m1i, f2, g2


@functools.lru_cache(maxsize=None)
def _ctx_fft_mats():
    N = 2 * CTX_LEN
    k = np.arange(N)[:, None]
    n = np.arange(N)[None, :]
    ang = 2.0 * np.pi * ((k * n) % N) / N
    c, s = np.cos(ang), np.sin(ang)
    fwd = np.concatenate([c, -s], axis=0).astype(np.float32)
    inv = (np.concatenate([c, -s], axis=1)[:CTX_LEN] / N).astype(np.float32)
    return fwd, inv


@functools.lru_cache(maxsize=None)
def _hyena_bands():
    b = np.zeros((1, LANES), np.float32)
    bands = np.linspace(1e-4, HY_BANDS - 1, HY_BANDS)
    b[0, 1:1 + HY_BANDS] = bands
    b[0, 1 + HY_BANDS:1 + 2 * HY_BANDS] = bands
    return b


def _ada_kernel(c_ref, w_ref, b_ref, o_ref):
    c = c_ref[...]
    a = c * (1.0 / (1.0 + jnp.exp(-c)))
    o_ref[...] = jnp.dot(a, w_ref[...], precision=HIGHEST, preferred_element_type=F32) + b_ref[...]


def _ada(cvec, ada_w, ada_b):
    tn = 1536
    return pl.pallas_call(
        _ada_kernel,
        out_shape=jax.ShapeDtypeStruct((DEPTH, 8, N_MOD * D_MODEL), F32),
        grid=(DEPTH, N_MOD * D_MODEL // tn),
        in_specs=[
            pl.BlockSpec((8, D_MODEL), lambda i, j: (0, 0)),
            pl.BlockSpec((None, D_MODEL, tn), lambda i, j: (i, 0, j)),
            pl.BlockSpec((None, 1, tn), lambda i, j: (i, 0, j)),
        ],
        out_specs=pl.BlockSpec((None, 8, tn), lambda i, j: (i, 0, j)),
        compiler_params=_params(("parallel", "parallel"), 40),
        name="ada_mod",
    )(cvec, ada_w, ada_b.reshape(DEPTH, 1, N_MOD * D_MODEL))


def _mod_row(ref, row):
    if isinstance(row, int):
        return ref[row:row + 1, :]
    return ref[pl.ds(row, 1), :]


def _norm_mod(x, g, shift, scale):
    y = x * lax.rsqrt(jnp.mean(x * x, axis=-1, keepdims=True) + EPS)
    return (y * g) * (1.0 + scale) + shift


def _rope_tile(x, cos, sin_signed):
    lane = lax.broadcasted_iota(jnp.int32, x.shape, 1)
    first = (lane % 32) < 16
    partner = jnp.where(first, pltpu.roll(x, LANES - 16, axis=1), pltpu.roll(x, 16, axis=1))
    return x * cos + partner * sin_signed


def _inproj0_kernel(x_ref, g_ref, sh_ref, sc_ref, w_ref, cos_ref, sin_ref, u_ref, qkv_ref, *, row, rope):
    r = pl.program_id(0) if row is None else row
    h = _norm_mod(x_ref[...], g_ref[...], _mod_row(sh_ref, r), _mod_row(sc_ref, r)).astype(BF16)
    tn = 512
    for j in range(EV_U // tn):
        u_ref[:, j * tn:(j + 1) * tn] = jnp.dot(h, w_ref[:, j * tn:(j + 1) * tn], preferred_element_type=F32)
    n_rot = (ATT_WIDTH + 2 * LANES) // LANES
    for t in range(EV_QKV // LANES):
        c0 = EV_U + t * LANES
        y = jnp.dot(h, w_ref[:, c0:c0 + LANES], preferred_element_type=F32)
        if rope and t < n_rot:
            y = _rope_tile(y, cos_ref[...], sin_ref[...])
        qkv_ref[:, t * LANES:(t + 1) * LANES] = y.astype(BF16)


def _inproj0(x, g, mod, w, layer, *, is_ctx, tm):
    G, R, _ = x.shape
    cos, sin = _rope2d_tables()
    cos, sin = jnp.asarray(cos), jnp.asarray(sin)
    if is_ctx:
        tab = pl.BlockSpec((tm, LANES), lambda b, i: (0, 0))
    else:
        tab = pl.BlockSpec((tm, LANES), lambda b, i: (i, 0))
    kern = functools.partial(_inproj0_kernel, row=2 if is_ctx else None, rope=not is_ctx)
    return pl.pallas_call(
        kern,
        out_shape=(jax.ShapeDtypeStruct((G, R, EV_U), F32), jax.ShapeDtypeStruct((G, R, EV_QKV), BF16)),
        grid=(G, R // tm),
        in_specs=[
            pl.BlockSpec((None, tm, D_MODEL), lambda b, i: (b, i, 0)),
            pl.BlockSpec((1, D_MODEL), lambda b, i: (0, 0)),
            pl.BlockSpec((None, 8, D_MODEL), lambda b, i: (layer, 0, 0)),
            pl.BlockSpec((None, 8, D_MODEL), lambda b, i: (layer, 0, 1)),
            pl.BlockSpec((D_MODEL, EV_COLS), lambda b, i: (0, 0)),
            tab, tab,
        ],
        out_specs=(pl.BlockSpec((None, tm, EV_U), lambda b, i: (b, i, 0)),
                   pl.BlockSpec((None, tm, EV_QKV), lambda b, i: (b, i, 0))),
        compiler_params=_params(("parallel", "parallel"), 48),
        name="inproj0_ctx" if is_ctx else "inproj0",
    )(x, g, mod, mod, w, cos, sin)


def _inproj1_kernel(x_ref, g_ref, sh_ref, sc_ref, w_ref, o_ref, h_scr, *, row):
    r = pl.program_id(0) if row is None else row

    @pl.when(pl.program_id(2) == 0)
    def _():
        h_scr[...] = _norm_mod(x_ref[...], g_ref[...], _mod_row(sh_ref, r), _mod_row(sc_ref, r)).astype(BF16)

    o_ref[...] = jnp.dot(h_scr[...], w_ref[...], preferred_element_type=F32).astype(o_ref.dtype)


def _inproj1(x, g, mod, w, layer, *, is_ctx, tm, tn):
    G, R, _ = x.shape
    N = w.shape[1]
    kern = functools.partial(_inproj1_kernel, row=2 if is_ctx else None)
    return pl.pallas_call(
        kern,
        out_shape=jax.ShapeDtypeStruct((G, R, N), BF16),
        grid=(G, R // tm, N // tn),
        in_specs=[
            pl.BlockSpec((None, tm, D_MODEL), lambda b, i, j: (b, i, 0)),
            pl.BlockSpec((1, D_MODEL), lambda b, i, j: (0, 0)),
            pl.BlockSpec((None, 8, D_MODEL), lambda b, i, j: (layer, 0, 0)),
            pl.BlockSpec((None, 8, D_MODEL), lambda b, i, j: (layer, 0, 1)),
            pl.BlockSpec((D_MODEL, tn), lambda b, i, j: (0, j)),
        ],
        out_specs=pl.BlockSpec((None, tm, tn), lambda b, i, j: (b, i, j)),
        scratch_shapes=[pltpu.VMEM((tm, D_MODEL), BF16)],
        compiler_params=_params(("parallel", "parallel", "arbitrary"), 48),
        name="inproj1_ctx" if is_ctx else "inproj1",
    )(x, g, mod, mod, w)


def _hyena_pre_kernel(u_ref, up_ref, un_ref, w_ref, b_ref, bias_ref, z_ref, x0_ref, zb_ref, *, n_tiles):
    i = pl.program_id(1)
    u = u_ref[...]
    tm = u.shape[0]
    prev_row = jnp.where(i > 0, up_ref[7:8, :], 0.0)
    next_row = jnp.where(i < n_tiles - 1, un_ref[0:1, :], 0.0)
    row = lax.broadcasted_iota(jnp.int32, u.shape, 0)
    before = jnp.where(row == 0, prev_row, pltpu.roll(u, 1, axis=0))
    after = jnp.where(row == tm - 1, next_row, pltpu.roll(u, tm - 1, axis=0))
    conv = before * w_ref[0:1, :] + u * w_ref[1:2, :] + after * w_ref[2:3, :] + b_ref[...]
    x0 = conv[:, :HY_WIDTH]
    z = conv[:, 2 * HY_WIDTH:] * conv[:, HY_WIDTH:2 * HY_WIDTH]
    z_ref[...] = z
    x0_ref[...] = x0
    zb_ref[...] = x0 * (z * bias_ref[...])


def _hyena_pre(u, conv_w, conv_b, bias, *, tm):
    G, R, _ = u.shape
    n_tiles = R // tm
    sub = tm // 8
    last8 = R // 8 - 1
    kern = functools.partial(_hyena_pre_kernel, n_tiles=n_tiles)
    out = jax.ShapeDtypeStruct((G, R, HY_WIDTH), F32)
    ospec = pl.BlockSpec((None, tm, HY_WIDTH), lambda b, i: (b, i, 0))
    return pl.pallas_call(
        kern,
        out_shape=(out, out, out),
        grid=(G, n_tiles),
        in_specs=[
            pl.BlockSpec((None, tm, EV_U), lambda b, i: (b, i, 0)),
            pl.BlockSpec((None, 8, EV_U), lambda b, i: (b, jnp.maximum(i * sub - 1, 0), 0)),
            pl.BlockSpec((None, 8, EV_U), lambda b, i: (b, jnp.minimum((i + 1) * sub, last8), 0)),
            pl.BlockSpec((3, EV_U), lambda b, i: (0, 0)),
            pl.BlockSpec((1, EV_U), lambda b, i: (0, 0)),
            pl.BlockSpec((1, HY_WIDTH), lambda b, i: (0, 0)),
        ],
        out_specs=(ospec, ospec, ospec),
        compiler_params=_params(("parallel", "parallel"), 48),
        name="hyena_pre",
    )(u, u, u, conv_w, conv_b, bias)


def _filter_kernel(w1_ref, b1_ref, w2_ref, b2_ref, w3_ref, fr_ref, dec_ref, bands_ref, h_ref, ss_ref, *, L, tm):
    i = pl.program_id(0)
    r = lax.broadcasted_iota(jnp.int32, (tm, 1), 0) + i * tm
    m = jnp.where(r < L, r, 2 * L - r)
    t = m.astype(F32) / L
    lane = lax.broadcasted_iota(jnp.int32, (tm, LANES), 1)
    ang = (2.0 * math.pi * t) * bands_ref[...]
    feat = jnp.where(lane == 0, t,
                     jnp.where(lane <= HY_BANDS, jnp.cos(ang),
                               jnp.where(lane <= 2 * HY_BANDS, -jnp.sin(ang), 0.0)))
    fr = fr_ref[...]
    hid = jnp.sin(fr * (jnp.dot(feat, w1_ref[...], precision=HIGHEST, preferred_element_type=F32) + b1_ref[...]))
    hid = jnp.sin(fr * (jnp.dot(hid, w2_ref[...], precision=HIGHEST, preferred_element_type=F32) + b2_ref[...]))
    h2 = jnp.dot(hid, w3_ref[...], precision=HIGHEST, preferred_element_type=F32)
    neg = r > L
    h = jnp.where(neg, h2[:, HY_WIDTH:], h2[:, :HY_WIDTH])
    dec = jnp.where(neg, jnp.abs(dec_ref[1:2, :]), jnp.abs(dec_ref[0:1, :]))
    h = h * jnp.exp(-t * dec)
    h = jnp.where(r == L, 0.0, h)
    h_ref[...] = h

    @pl.when(i == 0)
    def _():
        ss_ref[...] = jnp.zeros_like(ss_ref)

    ss_ref[...] += jnp.broadcast_to(jnp.sum(h * h, axis=0, keepdims=True), ss_ref.shape)


def _hyena_filter(w1p, b1, w2, b2, w3, freq, decay, *, L):
    tm = min(2 * L, 1024)
    kern = functools.partial(_filter_kernel, L=L, tm=tm)
    full = lambda a: pl.BlockSpec(a.shape, lambda i: (0,) * a.ndim)
    bands = jnp.asarray(_hyena_bands())
    args = (w1p, b1, w2, b2, w3, freq, decay, bands)
    return pl.pallas_call(
        kern,
        out_shape=(jax.ShapeDtypeStruct((2 * L, HY_WIDTH), F32), jax.ShapeDtypeStruct((8, HY_WIDTH), F32)),
        grid=(2 * L // tm,),
        in_specs=[full(a) for a in args],
        out_specs=(pl.BlockSpec((tm, HY_WIDTH), lambda i: (i, 0)), pl.BlockSpec((8, HY_WIDTH), lambda i: (0, 0))),
        compiler_params=_params(("arbitrary",), 48),
        name="hyena_filter_%d" % L,
    )(*args)


def _stack_bf16(re, im):
    return jnp.concatenate([re, im], axis=0).astype(BF16)


def _filter_fft_kernel(h_ref, ss_ref, m1_ref, f2_ref, o_ref, a_scr):
    R, P = FFT_R, FFT_PITCH
    inv_norm = lax.rsqrt(ss_ref[0:1, :])

    def stage1(n2, carry):
        xs = (h_ref[pl.ds(n2, R, stride=R), :] * inv_norm).astype(BF16)
        a = jnp.dot(m1_ref[n2], xs, preferred_element_type=F32)
        base = pl.multiple_of(n2 * P, 8)
        a_scr[0, pl.ds(base, R), :] = a[:R]
        a_scr[1, pl.ds(base, R), :] = a[R:]
        return carry

    lax.fori_loop(0, R, stage1, 0, unroll=FFT_UNROLL)

    def stage2(k1, carry):
        ar = a_scr[0, pl.ds(k1, R, stride=P), :]
        ai = a_scr[1, pl.ds(k1, R, stride=P), :]
        x = jnp.dot(f2_ref[...], _stack_bf16(ar, ai), preferred_element_type=F32)
        base = pl.multiple_of(k1 * R, R)
        o_ref[0, pl.ds(base, R), :] = x[:R].astype(BF16)
        o_ref[1, pl.ds(base, R), :] = x[R:].astype(BF16)
        return carry

    lax.fori_loop(0, R, stage2, 0, unroll=FFT_UNROLL)


def _filter_fft(h_circ, ss, m1, f2):
    ct = LANES
    once = pl.Buffered(1)
    return pl.pallas_call(
        _filter_fft_kernel,
        out_shape=jax.ShapeDtypeStruct((2, FFT_N, HY_WIDTH), BF16),
        grid=(HY_WIDTH // ct,),
        in_specs=[
            pl.BlockSpec((FFT_N, ct), lambda c: (0, c), pipeline_mode=once),
            pl.BlockSpec((8, ct), lambda c: (0, c)),
            pl.BlockSpec(m1.shape, lambda c: (0, 0, 0), pipeline_mode=once),
            pl.BlockSpec(f2.shape, lambda c: (0, 0), pipeline_mode=once),
        ],
        out_specs=pl.BlockSpec((2, FFT_N, ct), lambda c: (0, 0, c)),
        scratch_shapes=[pltpu.VMEM((2, FFT_R * FFT_PITCH, ct), F32)],
        compiler_params=_params(("parallel",), 58),
        name="hyena_filter_fft",
    )(h_circ, ss, m1, f2)


def _fft_conv_kernel(z_ref, hs_ref, m1_ref, f2_ref, g2_ref, m1i_ref, y_ref, a_scr):
    R, P = FFT_R, FFT_PITCH

    def stage1(n2, carry):
        xs = z_ref[pl.ds(n2, R // 2, stride=R), :].astype(BF16)
        a = jnp.dot(m1_ref[n2, :, 0:R // 2], xs, preferred_element_type=F32)
        base = pl.multiple_of(n2 * P, 8)
        a_scr[0, pl.ds(base, R), :] = a[:R]
        a_scr[1, pl.ds(base, R), :] = a[R:]
        return carry

    lax.fori_loop(0, R, stage1, 0, unroll=FFT_UNROLL)

    def stage2(k1, carry):
        ar = a_scr[0, pl.ds(k1, R, stride=P), :]
        ai = a_scr[1, pl.ds(k1, R, stride=P), :]
        x = jnp.dot(f2_ref[...], _stack_bf16(ar, ai), preferred_element_type=F32)
        base = pl.multiple_of(k1 * R, R)
        hr = hs_ref[0, pl.ds(base, R), :].astype(F32)
        hi = hs_ref[1, pl.ds(base, R), :].astype(F32)
        xr, xi = x[:R], x[R:]
        yr = xr * hr - xi * hi
        yi = xr * hi + xi * hr
        bq = jnp.dot(g2_ref[...], _stack_bf16(yr, yi), preferred_element_type=F32)
        a_scr[0, pl.ds(k1, R, stride=P), :] = bq[:R]
        a_scr[1, pl.ds(k1, R, stride=P), :] = bq[R:]
        return carry

    lax.fori_loop(0, R, stage2, 0, unroll=FFT_UNROLL)

    def stage3(n2, carry):
        base = pl.multiple_of(n2 * P, 8)
        br = a_scr[0, pl.ds(base, R), :]
        bi = a_scr[1, pl.ds(base, R), :]
        y = jnp.dot(m1i_ref[n2], _stack_bf16(br, bi), preferred_element_type=F32)
        y_ref[pl.ds(n2, R // 2, stride=R), :] = y
        return carry

    lax.fori_loop(0, R, stage3, 0, unroll=FFT_UNROLL)


def _fft_conv(z, hspec, m1, f2, g2, m1i):
    G = z.shape[0]
    ct = LANES
    once = pl.Buffered(1)
    return pl.pallas_call(
        _fft_conv_kernel,
        out_shape=jax.ShapeDtypeStruct((G, SEQ, HY_WIDTH), F32),
        grid=(HY_WIDTH // ct, G),
        in_specs=[
            pl.BlockSpec((None, SEQ, ct), lambda c, b: (b, 0, c)),
            pl.BlockSpec((2, FFT_N, ct), lambda c, b: (0, 0, c), pipeline_mode=once),
            pl.BlockSpec(m1.shape, lambda c, b: (0, 0, 0), pipeline_mode=once),
            pl.BlockSpec(f2.shape, lambda c, b: (0, 0), pipeline_mode=once),
            pl.BlockSpec(g2.shape, lambda c, b: (0, 0), pipeline_mode=once),
            pl.BlockSpec(m1i.shape, lambda c, b: (0, 0, 0), pipeline_mode=once),
        ],
        out_specs=pl.BlockSpec((None, SEQ, ct), lambda c, b: (b, 0, c)),
        scratch_shapes=[pltpu.VMEM((2, FFT_R * FFT_PITCH, ct), F32)],
        compiler_params=_params(("parallel", "arbitrary"), 58),
        name="hyena_fft_conv",
    )(z, hspec, m1, f2, g2, m1i)


def _ctx_conv_kernel(z_ref, h_ref, ss_ref, fwd_ref, inv_ref, y_ref):
    N = 2 * CTX_LEN
    hn = (h_ref[...] * lax.rsqrt(ss_ref[0:1, :])).astype(BF16)
    hs = jnp.dot(fwd_ref[...], hn, preferred_element_type=F32)
    zs = jnp.dot(fwd_ref[:, :CTX_LEN], z_ref[...].astype(BF16), preferred_element_type=F32)
    hr, hi, zr, zi = hs[:N], hs[N:], zs[:N], zs[N:]
    ys = _stack_bf16(zr * hr - zi * hi, zr * hi + zi * hr)
    y_ref[...] = jnp.dot(inv_ref[...], ys, preferred_element_type=F32)


def _ctx_conv(z, h_circ, ss, fwd, inv):
    G = z.shape[0]
    full = lambda a: pl.BlockSpec(a.shape, lambda b: (0,) * a.ndim)
    return pl.pallas_call(
        _ctx_conv_kernel,
        out_shape=jax.ShapeDtypeStruct((G, CTX_LEN, HY_WIDTH), F32),
        grid=(G,),
        in_specs=[pl.BlockSpec((None, CTX_LEN, HY_WIDTH), lambda b: (b, 0, 0)),
                  full(h_circ), full(ss), full(fwd), full(inv)],
        out_specs=pl.BlockSpec((None, CTX_LEN, HY_WIDTH), lambda b: (b, 0, 0)),
        compiler_params=_params(("parallel",), 48),
        name="hyena_ctx_conv",
    )(z, h_circ, ss, fwd, inv)


def _attn_kernel(sink_ref, q_ref, *refs, local, n_blocks):
    if local:
        kp_ref, kc_ref, kn_ref, kx_ref, vp_ref, vc_ref, vn_ref, vx_ref, o_ref = refs
    else:
        kx_ref, vx_ref, o_ref = refs
    n = pl.program_id(1)
    q = q_ref[...]
    lane = lax.broadcasted_iota(jnp.int32, (BLOCK, LANES), 1)
    low = lane < HEAD_DIM
    zero = jnp.zeros((), BF16)
    scale = HEAD_DIM ** -0.5
    n_ctx = kx_ref.shape[0]

    if local:
        qi = lax.broadcasted_iota(jnp.int32, (BLOCK, BLOCK), 0)
        ki = lax.broadcasted_iota(jnp.int32, (BLOCK, BLOCK), 1)
        ok_prev = (ki >= qi) & (n > 0)
        ok_next = (ki <= qi) & (n < n_blocks - 1)
        ok = jnp.concatenate([ok_prev, jnp.ones((BLOCK, BLOCK), jnp.bool_), ok_next,
                              jnp.ones((BLOCK, n_ctx), jnp.bool_)], axis=1)
        ok = jnp.concatenate([ok] * ATT_GROUP, axis=0)

    for g in range(ATT_KV_HEADS):
        kcols = slice(g * LANES, (g + 1) * LANES)
        if local:
            k = jnp.concatenate([kp_ref[:, kcols], kc_ref[:, kcols], kn_ref[:, kcols], kx_ref[:, kcols]], axis=0)
            v = jnp.concatenate([vp_ref[:, kcols], vc_ref[:, kcols], vn_ref[:, kcols], vx_ref[:, kcols]], axis=0)
        else:
            k, v = kx_ref[:, kcols], vx_ref[:, kcols]
        parts = []
        for hh in range(ATT_GROUP):
            h = g * ATT_GROUP + hh
            tile = q[:, (h // 2) * LANES:(h // 2 + 1) * LANES]
            parts.append(jnp.where(low if h % 2 == 0 else ~low, tile, zero))
        qs = jnp.concatenate(parts, axis=0)
        s = lax.dot_general(qs, k, (((1,), (1,)), ((), ())), preferred_element_type=F32) * scale
        if local:
            s = jnp.where(ok, s, NEG_INF)
        sink = jnp.concatenate(
            [jnp.full((BLOCK, 1), sink_ref[g * ATT_GROUP + hh], F32) for hh in range(ATT_GROUP)], axis=0)
        m = jnp.maximum(jnp.max(s, axis=-1, keepdims=True), sink)
        e = jnp.exp(s - m)
        denom = jnp.sum(e, axis=-1, keepdims=True) + jnp.exp(sink - m)
        o = jnp.dot(e.astype(BF16), v, preferred_element_type=F32) / denom
        for pair in range(ATT_GROUP // 2):
            even = o[(2 * pair) * BLOCK:(2 * pair + 1) * BLOCK]
            odd = o[(2 * pair + 1) * BLOCK:(2 * pair + 2) * BLOCK]
            t = g * (ATT_GROUP // 2) + pair
            o_ref[:, t * LANES:(t + 1) * LANES] = jnp.where(low, even, odd).astype(o_ref.dtype)


def _attention(sink, qkv, qkv_ctx, *, local):
    src = qkv if local else qkv_ctx
    G, R, _ = src.shape
    nb = R // BLOCK
    kcol, vcol = ATT_WIDTH // 256, ATT_WIDTH // 256 + 1
    q_spec = pl.BlockSpec((None, BLOCK, ATT_WIDTH), lambda b, n: (b, n, 0))
    ctx_k = pl.BlockSpec((None, CTX_LEN, 256), lambda b, n: (b, 0, kcol))
    ctx_v = pl.BlockSpec((None, CTX_LEN, 256), lambda b, n: (b, 0, vcol))
    if local:
        def nbr(col, off):
            return pl.BlockSpec((None, BLOCK, 256), lambda b, n: (b, jnp.clip(n + off, 0, nb - 1), col))
        in_specs = [q_spec, nbr(kcol, -1), nbr(kcol, 0), nbr(kcol, 1), ctx_k,
                    nbr(vcol, -1), nbr(vcol, 0), nbr(vcol, 1), ctx_v]
        args = (qkv, qkv, qkv, qkv, qkv_ctx, qkv, qkv, qkv, qkv_ctx)
    else:
        in_specs = [q_spec, ctx_k, ctx_v]
        args = (qkv_ctx, qkv_ctx, qkv_ctx)
    kern = functools.partial(_attn_kernel, local=local, n_blocks=nb)
    return pl.pallas_call(
        kern,
        out_shape=jax.ShapeDtypeStruct((G, R, ATT_WIDTH), BF16),
        grid=(G, nb),
        in_specs=[pl.BlockSpec(memory_space=pltpu.SMEM)] + in_specs,
        out_specs=pl.BlockSpec((None, BLOCK, ATT_WIDTH), lambda b, n: (b, n, 0)),
        compiler_params=_params(("parallel", "parallel"), 48),
        name="attn_local" if local else "attn_ctx",
    )(sink, *args)


def _mlp_step(f, n_f, x_ref, gm_ref, shm_ref, scm_ref, gtm_ref, w1_ref, w2_ref, fg_ref, o_ref,
              x1_scr, h_scr, acc_scr, r, final_norm):
    a = jnp.dot(h_scr[...], w1_ref[...], preferred_element_type=F32)
    a = jnp.maximum(a, 0.0)
    acc_scr[...] += jnp.dot((a * a).astype(BF16), w2_ref[...], preferred_element_type=F32)

    @pl.when(f == n_f - 1)
    def _():
        out = x1_scr[...] + _mod_row(gtm_ref, r) * acc_scr[...]
        if final_norm:
            out = (out * lax.rsqrt(jnp.mean(out * out, axis=-1, keepdims=True) + EPS)) * fg_ref[...]
        o_ref[...] = out


def _mix_prologue(y, x_ref, gta_ref, gm_ref, shm_ref, scm_ref, x1_scr, h_scr, acc_scr, r):
    x1 = x_ref[...] + _mod_row(gta_ref, r) * y
    x1_scr[...] = x1
    h_scr[...] = _norm_mod(x1, gm_ref[...], _mod_row(shm_ref, r), _mod_row(scm_ref, r)).astype(BF16)
    acc_scr[...] = jnp.zeros_like(acc_scr)


def _mix0_kernel(x_ref, x0_ref, y_ref, zb_ref, att_ref, wo_ref, gta_ref, gm_ref, shm_ref, scm_ref, gtm_ref,
                 w1_ref, w2_ref, fg_ref, o_ref, x1_scr, h_scr, acc_scr, *, row, n_f, final_norm):
    r = pl.program_id(0) if row is None else row
    f = pl.program_id(2)

    @pl.when(f == 0)
    def _():
        hy = (x0_ref[...] * y_ref[...] + zb_ref[...]).astype(BF16)
        y = (jnp.dot(hy, wo_ref[:HY_WIDTH, :], preferred_element_type=F32)
             + jnp.dot(att_ref[...], wo_ref[HY_WIDTH:, :], preferred_element_type=F32))
        _mix_prologue(y, x_ref, gta_ref, gm_ref, shm_ref, scm_ref, x1_scr, h_scr, acc_scr, r)

    _mlp_step(f, n_f, x_ref, gm_ref, shm_ref, scm_ref, gtm_ref, w1_ref, w2_ref, fg_ref, o_ref,
              x1_scr, h_scr, acc_scr, r, final_norm)


def _mix1_kernel(x_ref, of_ref, ob_ref, wo_ref, gta_ref, gm_ref, shm_ref, scm_ref, gtm_ref,
                 w1_ref, w2_ref, fg_ref, o_ref, x1_scr, h_scr, acc_scr, *, row, n_f, final_norm):
    r = pl.program_id(0) if row is None else row
    f = pl.program_id(2)

    @pl.when(f == 0)
    def _():
        a = (of_ref[...].astype(F32) + ob_ref[...].astype(F32)).astype(BF16)
        y = jnp.dot(a, wo_ref[...], preferred_element_type=F32)
        _mix_prologue(y, x_ref, gta_ref, gm_ref, shm_ref, scm_ref, x1_scr, h_scr, acc_scr, r)

    _mlp_step(f, n_f, x_ref, gm_ref, shm_ref, scm_ref, gtm_ref, w1_ref, w2_ref, fg_ref, o_ref,
              x1_scr, h_scr, acc_scr, r, final_norm)


def _mix_mlp(kind, x, mix_in, wo, mod, gm, w1, w2, fg, layer, *, is_ctx, tm, fc, final_norm):
    G, R, _ = x.shape
    n_f = D_FF // fc
    row_spec = lambda w: pl.BlockSpec((None, tm, w), lambda b, i, f: (b, i, 0))
    modk = lambda k: pl.BlockSpec((None, 8, D_MODEL), lambda b, i, f: (layer, 0, k))
    vec = pl.BlockSpec((1, D_MODEL), lambda b, i, f: (0, 0))
    if kind == 0:
        body = _mix0_kernel
        mix_specs = [row_spec(HY_WIDTH)] * 3 + [row_spec(ATT_WIDTH)]
    else:
        body = _mix1_kernel
        mix_specs = [row_spec(RET_V)] * 2
    kern = functools.partial(body, row=2 if is_ctx else None, n_f=n_f, final_norm=final_norm)
    return pl.pallas_call(
        kern,
        out_shape=jax.ShapeDtypeStruct((G, R, D_MODEL), F32),
        grid=(G, R // tm, n_f),
        in_specs=[row_spec(D_MODEL)] + mix_specs + [
            pl.BlockSpec(wo.shape, lambda b, i, f: (0, 0)),
            modk(2), vec, modk(3), modk(4), modk(5),
            pl.BlockSpec((D_MODEL, fc), lambda b, i, f: (0, f)),
            pl.BlockSpec((fc, D_MODEL), lambda b, i, f: (f, 0)),
            vec,
        ],
        out_specs=row_spec(D_MODEL),
        scratch_shapes=[pltpu.VMEM((tm, D_MODEL), F32), pltpu.VMEM((tm, D_MODEL), BF16),
                        pltpu.VMEM((tm, D_MODEL), F32)],
        compiler_params=_params(("parallel", "parallel", "arbitrary"), 56),
        name="mix_mlp%d%s" % (kind, "_ctx" if is_ctx else ""),
    )(x, *mix_in, wo, mod, gm, mod, mod, mod, w1, w2, fg)


def _ret_kernel(lr_ref, qkv_c, qkv_f, g_f, qkv_b, g_b, cos_ref, sin_ref, of_ref, ob_ref,
                state, dmask, xi, zeta, gch):
    j = pl.program_id(1)
    C = RET_C
    kscale = RET_DK ** -0.5

    @pl.when(j == 0)
    def _():
        state[...] = jnp.zeros_like(state)
        row = lax.broadcasted_iota(jnp.int32, (C, C), 0).astype(F32)
        col = lax.broadcasted_iota(jnp.int32, (C, C), 1).astype(F32)
        for d in range(2):
            e = row - col if d == 0 else col - row
            p = row if d == 0 else (C - 1) - row
            for h in range(RET_HEADS):
                lg = -jnp.exp(jnp.full((C, C), lr_ref[d, h], F32))
                dmask[d, h] = jnp.where(e >= 0, jnp.exp(e * lg), 0.0) * kscale
                xi[d, h] = jnp.exp((p + 1.0) * lg)
                zeta[d, h] = jnp.exp(((C - 1) - p) * lg) * kscale
                gch[d, h] = jnp.exp(C * -jnp.exp(jnp.full((8, C), lr_ref[d, h], F32)))

    half = RET_DK // 2

    def chain(d, h, src_ref, g_ref, o_ref):
        cos, sin = cos_ref[d], sin_ref[d]

        def rot(c0):
            t1 = src_ref[:, c0:c0 + half].astype(F32)
            t2 = src_ref[:, c0 + half:c0 + RET_DK].astype(F32)
            return jnp.concatenate([t1 * cos - t2 * sin, t2 * cos + t1 * sin], axis=1)

        q = rot(h * RET_DK)
        k = rot(RET_QK + h * RET_DK)
        v = src_ref[:, 2 * RET_QK + h * RET_DV:2 * RET_QK + (h + 1) * RET_DV]
        inner = lax.dot_general(q.astype(BF16), k.astype(BF16), (((1,), (1,)), ((), ())),
                                preferred_element_type=F32) * dmask[d, h]
        s_old = state[d, h]
        kv = lax.dot_general((k * zeta[d, h]).astype(BF16), v, (((0,), (0,)), ((), ())),
                             preferred_element_type=F32)
        state[d, h] = gch[d, h, 0:1, 0:1] * s_old + kv
        if o_ref is not None:
            o = (jnp.dot(inner.astype(BF16), v, preferred_element_type=F32)
                 + jnp.dot((q * xi[d, h]).astype(BF16), s_old.astype(BF16), preferred_element_type=F32))
            on = o * lax.rsqrt(jnp.mean(o * o, axis=-1, keepdims=True) + EPS)
            gate = g_ref[:, h * RET_DV:(h + 1) * RET_DV].astype(F32)
            o_ref[:, h * RET_DV:(h + 1) * RET_DV] = (gate * (1.0 / (1.0 + jnp.exp(-gate))) * on).astype(o_ref.dtype)

    @pl.when(j == 0)
    def _():
        for d in range(2):
            for h in range(RET_HEADS):
                chain(d, h, qkv_c, None, None)

    @pl.when(j > 0)
    def _():
        for h in range(RET_HEADS):
            chain(0, h, qkv_f, g_f, of_ref)
            chain(1, h, qkv_b, g_b, ob_ref)


def _retention(log_rate, p_ctx, p_lat):
    G = p_lat.shape[0]
    C = RET_C
    n_lat = SEQ // C
    cos, sin = _rope1d_tables()
    cos, sin = jnp.asarray(cos), jnp.asarray(sin)

    def fwd_chunk(j):
        return jnp.maximum(j - 1, 0)

    def bwd_chunk(j):
        return n_lat - 1 - jnp.maximum(j - 1, 0)

    qkv_w = 2 * RET_QK + RET_V
    gcol = qkv_w // RET_V
    tab = pl.BlockSpec((2, C, RET_DK // 2), lambda b, j: (0, j, 0))
    in_specs = [
        pl.BlockSpec(memory_space=pltpu.SMEM),
        pl.BlockSpec((None, C, qkv_w), lambda b, j: (b, 0, 0)),
        pl.BlockSpec((None, C, qkv_w), lambda b, j: (b, fwd_chunk(j), 0)),
        pl.BlockSpec((None, C, RET_V), lambda b, j: (b, fwd_chunk(j), gcol)),
        pl.BlockSpec((None, C, qkv_w), lambda b, j: (b, bwd_chunk(j), 0)),
        pl.BlockSpec((None, C, RET_V), lambda b, j: (b, bwd_chunk(j), gcol + 1)),
        tab, tab,
    ]
    out = jax.ShapeDtypeStruct((G, SEQ, RET_V), BF16)
    per_chain = lambda *tail: pltpu.VMEM((2, RET_HEADS) + tail, F32)
    return pl.pallas_call(
        _ret_kernel,
        out_shape=(out, out),
        grid=(G, 1 + n_lat),
        in_specs=in_specs,
        out_specs=(pl.BlockSpec((None, C, RET_V), lambda b, j: (b, fwd_chunk(j), 0)),
                   pl.BlockSpec((None, C, RET_V), lambda b, j: (b, bwd_chunk(j), 0))),
        scratch_shapes=[per_chain(RET_DK, RET_DV), per_chain(C, C), per_chain(C, C), per_chain(C, C),
                        per_chain(8, C)],
        compiler_params=_params(("parallel", "arbitrary"), 48),
        name="retention",
    )(log_rate, p_ctx, p_lat, p_lat, p_lat, p_lat, cos, sin)


def _ev_weight(w_in):
    i_q = EV_U
    i_k = i_q + ATT_WIDTH
    i_v = i_k + KV_WIDTH
    cols = [w_in[:, :i_k]]
    for base in (i_k, i_v):
        for g in range(ATT_KV_HEADS):
            part = w_in[:, base + g * HEAD_DIM: base + (g + 1) * HEAD_DIM]
            cols += [part, part]
    return jnp.concatenate(cols, axis=1).astype(BF16)


def kernel(x, c, ctx, c_ctx, ada_w, ada_b, norm_mix_g, norm_mlp_g, mlp_w1, mlp_w2, ev_w_in, ev_w_out, hy_conv_w, hy_conv_b, hy_w1, hy_b1, hy_w2, hy_b2, hy_w3, hy_freq, hy_decay, hy_bias, attn_sink, od_w_in, od_w_out, ret_log_rate, final_g):
    D = D_MODEL
    cvec = jnp.concatenate([c, c_ctx[None, :], jnp.zeros((8 - BATCH - 1, D), F32)], axis=0)
    mod = _ada(cvec, ada_w, ada_b)

    m1, m1i, f2, g2 = (jnp.asarray(a).astype(BF16) for a in _fft_mats())
    cfwd, cinv = (jnp.asarray(a).astype(BF16) for a in _ctx_fft_mats())
    fg = final_g.reshape(1, D)

    gmix = norm_mix_g[0].reshape(1, D)
    gmlp = norm_mlp_g[0].reshape(1, D)
    w_in = _ev_weight(ev_w_in[0])
    u_l, qkv_l = _inproj0(x, gmix, mod, w_in, 0, is_ctx=False, tm=512)
    u_c, qkv_c = _inproj0(ctx, gmix, mod, w_in, 0, is_ctx=True, tm=CTX_LEN)

    conv_b = hy_conv_b[0].reshape(1, EV_U)
    bias = hy_bias[0].reshape(1, HY_WIDTH)
    z_l, x0_l, zb_l = _hyena_pre(u_l, hy_conv_w[0], conv_b, bias, tm=512)
    z_c, x0_c, zb_c = _hyena_pre(u_c, hy_conv_w[0], conv_b, bias, tm=CTX_LEN)

    w1p = jnp.pad(hy_w1[0], ((0, LANES - HY_EMB), (0, 0)))
    filt = (w1p, hy_b1[0].reshape(1, -1), hy_w2[0], hy_b2[0].reshape(1, -1), hy_w3[0],
            hy_freq[0].reshape(1, -1), hy_decay[0])
    h_l, ss_l = _hyena_filter(*filt, L=SEQ)
    h_c, ss_c = _hyena_filter(*filt, L=CTX_LEN)
    hspec = _filter_fft(h_l, ss_l, m1, f2)
    y_l = _fft_conv(z_l, hspec, m1, f2, g2, m1i)
    y_c = _ctx_conv(z_c, h_c, ss_c, cfwd, cinv)

    sink = attn_sink[0]
    att_l = _attention(sink, qkv_l, qkv_c, local=True)
    att_c = _attention(sink, qkv_l, qkv_c, local=False)

    wo = ev_w_out[0].astype(BF16)
    w1 = mlp_w1[0].astype(BF16)
    w2 = mlp_w2[0].astype(BF16)
    x1 = _mix_mlp(0, x, (x0_l, y_l, zb_l, att_l), wo, mod, gmlp, w1, w2, fg, 0,
                  is_ctx=False, tm=512, fc=1024, final_norm=False)
    ctx1 = _mix_mlp(0, ctx, (x0_c, y_c, zb_c, att_c), wo, mod, gmlp, w1, w2, fg, 0,
                    is_ctx=True, tm=CTX_LEN, fc=1024, final_norm=False)

    gmix = norm_mix_g[1].reshape(1, D)
    gmlp = norm_mlp_g[1].reshape(1, D)
    w_in = od_w_in[0].astype(BF16)
    p_l = _inproj1(x1, gmix, mod, w_in, 1, is_ctx=False, tm=1024, tn=2048)
    p_c = _inproj1(ctx1, gmix, mod, w_in, 1, is_ctx=True, tm=CTX_LEN, tn=2048)
    o_fwd, o_bwd = _retention(ret_log_rate[0], p_c, p_l)

    wo = od_w_out[0].astype(BF16)
    w1 = mlp_w1[1].astype(BF16)
    w2 = mlp_w2[1].astype(BF16)
    return _mix_mlp(1, x1, (o_fwd, o_bwd), wo, mod, gmlp, w1, w2, fg, 1,
                    is_ctx=False, tm=512, fc=1024, final_norm=True)
```

```python
import functools
import math

import numpy as np
import jax
import jax.numpy as jnp
from jax import lax
from jax.experimental import pallas as pl
from jax.experimental.pallas import tpu as pltpu

F32 = jnp.float32
BF16 = jnp.bfloat16
HIGHEST = lax.Precision.HIGHEST

D_MODEL = 1024
BATCH = 2
SEQ = 8192
DEPTH = 2
GRID_W = 64
CTX_LEN = 256
EPS = 1e-6
NEG_INF = -1e30
N_MOD = 6
D_FF = 4 * D_MODEL
ROPE_BASE = 10000.0

HY_WIDTH = D_MODEL // 2
HY_EMB = 33
HY_BANDS = (HY_EMB - 1) // 2
HY_HIDDEN = 64

ATT_HEADS = 8
ATT_KV_HEADS = 2
ATT_GROUP = ATT_HEADS // ATT_KV_HEADS
HEAD_DIM = 64
ATT_WIDTH = ATT_HEADS * HEAD_DIM
KV_WIDTH = ATT_KV_HEADS * HEAD_DIM
BLOCK = 128

RET_HEADS = 4
RET_DK = D_MODEL // RET_HEADS
RET_DV = 2 * RET_DK
RET_QK = RET_HEADS * RET_DK
RET_V = RET_HEADS * RET_DV
OD_IN = 2 * RET_QK + 3 * RET_V

LOG2E = 1.4426950408889634
LANES = 128
MIB = 1024 * 1024

EV_U = 3 * HY_WIDTH
EV_QKV = ATT_WIDTH + 4 * LANES
EV_COLS = EV_U + EV_QKV

FFT_N = 2 * SEQ
FFT_R = 128
FFT_PITCH = FFT_R + 8
FFT_UNROLL = 16
RET_C = 256


def _params(sem, vmem_mib):
    return pltpu.CompilerParams(dimension_semantics=sem, vmem_limit_bytes=vmem_mib * MIB)


@functools.lru_cache(maxsize=None)
def _rope2d_tables():
    quarter = HEAD_DIM // 4
    inv = ROPE_BASE ** (-np.arange(quarter, dtype=np.float64) / quarter)
    t = np.arange(SEQ)
    pos = np.stack([t // GRID_W, t % GRID_W], axis=1).astype(np.float64)
    lane = np.arange(HEAD_DIM)
    half = lane // (HEAD_DIM // 2)
    e = lane % (HEAD_DIM // 2)
    ang = pos[:, half] * inv[e % quarter][None, :]
    sign = np.where(e < quarter, -1.0, 1.0)[None, :]
    cos = np.tile(np.cos(ang), (1, 2)).astype(np.float32)
    sin = np.tile(np.sin(ang) * sign, (1, 2)).astype(np.float32)
    return cos, sin


@functools.lru_cache(maxsize=None)
def _rope1d_tables():
    n = RET_DK // 2
    inv = ROPE_BASE ** (-np.linspace(0.0, 1.0, n))
    pos = np.arange(CTX_LEN + SEQ, dtype=np.float64)
    ang = pos[:, None] * inv[None, :]
    cos, sin = np.cos(ang), np.sin(ang)

    def rev(a):
        return a.reshape(-1, RET_C, n)[:, ::-1].reshape(-1, n)

    cos2 = np.stack([cos, rev(cos)]).astype(np.float32)
    sin2 = np.stack([sin, rev(sin)]).astype(np.float32)
    return cos2, sin2


@functools.lru_cache(maxsize=None)
def _fft_mats():
    N, R = FFT_N, FFT_R
    k1 = np.arange(R)
    n2 = np.arange(R)[:, None, None]
    n1 = np.arange(R)[None, None, :]
    n = n2 + R * n1
    idx = (k1[None, :, None] * n) % N
    ang = 2.0 * np.pi * idx / N
    c, s = np.cos(ang), np.sin(ang)
    m1 = np.concatenate([c, -s], axis=1).astype(np.float32)
    m1i = np.concatenate([c.transpose(0, 2, 1), -s.transpose(0, 2, 1)], axis=2)
    m1i = m1i[:, :R // 2].astype(np.float32)
    a2 = 2.0 * np.pi * ((np.arange(R)[:, None] * np.arange(R)[None, :]) % R) / R
    fr, fi = np.cos(a2), -np.sin(a2)
    f2 = np.block([[fr, -fi], [fi, fr]]).astype(np.float32)
    g2 = (np.block([[fr, fi], [-fi, fr]]) / N).astype(np.float32)
    return m1, m1i, f2, g2


@functools.lru_cache(maxsize=None)
def _ctx_fft_mats():
    N = 2 * CTX_LEN
    k = np.arange(N)[:, None]
    n = np.arange(N)[None, :]
    ang = 2.0 * np.pi * ((k * n) % N) / N
    c, s = np.cos(ang), np.sin(ang)
    fwd = np.concatenate([c, -s], axis=0).astype(np.float32)
    inv = (np.concatenate([c, -s], axis=1)[:CTX_LEN] / N).astype(np.float32)
    return fwd, inv


def _ada_kernel(c_ref, w_ref, b_ref, o_ref):
    c = c_ref[...]
    a = c * (1.0 / (1.0 + jnp.exp(-c)))
    o_ref[...] = jnp.dot(a, w_ref[...], precision=HIGHEST, preferred_element_type=F32) + b_ref[...]


def _ada(cvec, ada_w, ada_b):
    tn = 1536
    return pl.pallas_call(
        _ada_kernel,
        out_shape=jax.ShapeDtypeStruct((DEPTH, 8, N_MOD * D_MODEL), F32),
        grid=(DEPTH, N_MOD * D_MODEL // tn),
        in_specs=[
            pl.BlockSpec((8, D_MODEL), lambda i, j: (0, 0)),
            pl.BlockSpec((None, D_MODEL, tn), lambda i, j: (i, 0, j)),
            pl.BlockSpec((None, 1, tn), lambda i, j: (i, 0, j)),
        ],
        out_specs=pl.BlockSpec((None, 8, tn), lambda i, j: (i, 0, j)),
        compiler_params=_params(("parallel", "parallel"), 40),
        name="ada_mod",
    )(cvec, ada_w, ada_b.reshape(DEPTH, 1, N_MOD * D_MODEL))


def _mod_row(ref, row):
    if isinstance(row, int):
        return ref[row:row + 1, :]
    return ref[pl.ds(row, 1), :]


def _norm_mod(x, g, shift, scale):
    y = x * lax.rsqrt(jnp.mean(x * x, axis=-1, keepdims=True) + EPS)
    return (y * g) * (1.0 + scale) + shift


def _rope_tile(x, cos, sin_signed):
    lane = lax.broadcasted_iota(jnp.int32, x.shape, 1)
    first = (lane % 32) < 16
    partner = jnp.where(first, pltpu.roll(x, LANES - 16, axis=1), pltpu.roll(x, 16, axis=1))
    return x * cos + partner * sin_signed


def _inproj0_kernel(x_ref, g_ref, sh_ref, sc_ref, w_ref, cos_ref, sin_ref, u_ref, qkv_ref, *, row, rope):
    r = pl.program_id(0) if row is None else row
    h = _norm_mod(x_ref[...], g_ref[...], _mod_row(sh_ref, r), _mod_row(sc_ref, r)).astype(BF16)
    tn = 512
    for j in range(EV_U // tn):
        u_ref[:, j * tn:(j + 1) * tn] = jnp.dot(h, w_ref[:, j * tn:(j + 1) * tn], preferred_element_type=F32)
    n_rot = (ATT_WIDTH + 2 * LANES) // LANES
    for t in range(EV_QKV // LANES):
        c0 = EV_U + t * LANES
        y = jnp.dot(h, w_ref[:, c0:c0 + LANES], preferred_element_type=F32)
        if rope and t < n_rot:
            y = _rope_tile(y, cos_ref[...], sin_ref[...])
        qkv_ref[:, t * LANES:(t + 1) * LANES] = y.astype(BF16)


def _inproj0(x, g, mod, w, layer, *, is_ctx, tm):
    G, R, _ = x.shape
    cos, sin = _rope2d_tables()
    cos, sin = jnp.asarray(cos), jnp.asarray(sin)
    if is_ctx:
        tab = pl.BlockSpec((tm, LANES), lambda b, i: (0, 0))
    else:
        tab = pl.BlockSpec((tm, LANES), lambda b, i: (i, 0))
    kern = functools.partial(_inproj0_kernel, row=2 if is_ctx else None, rope=not is_ctx)
    return pl.pallas_call(
        kern,
        out_shape=(jax.ShapeDtypeStruct((G, R, EV_U), F32), jax.ShapeDtypeStruct((G, R, EV_QKV), BF16)),
        grid=(G, R // tm),
        in_specs=[
            pl.BlockSpec((None, tm, D_MODEL), lambda b, i: (b, i, 0)),
            pl.BlockSpec((1, D_MODEL), lambda b, i: (0, 0)),
            pl.BlockSpec((None, 8, D_MODEL), lambda b, i: (layer, 0, 0)),
            pl.BlockSpec((None, 8, D_MODEL), lambda b, i: (layer, 0, 1)),
            pl.BlockSpec((D_MODEL, EV_COLS), lambda b, i: (0, 0)),
            tab, tab,
        ],
        out_specs=(pl.BlockSpec((None, tm, EV_U), lambda b, i: (b, i, 0)),
                   pl.BlockSpec((None, tm, EV_QKV), lambda b, i: (b, i, 0))),
        compiler_params=_params(("parallel", "parallel"), 48),
        name="inproj0_ctx" if is_ctx else "inproj0",
    )(x, g, mod, mod, w, cos, sin)


def _inproj1_kernel(x_ref, g_ref, sh_ref, sc_ref, w_ref, o_ref, h_scr, *, row):
    r = pl.program_id(0) if row is None else row

    @pl.when(pl.program_id(2) == 0)
    def _():
        h_scr[...] = _norm_mod(x_ref[...], g_ref[...], _mod_row(sh_ref, r), _mod_row(sc_ref, r)).astype(BF16)

    o_ref[...] = jnp.dot(h_scr[...], w_ref[...], preferred_element_type=F32).astype(o_ref.dtype)


def _inproj1(x, g, mod, w, layer, *, is_ctx, tm, tn):
    G, R, _ = x.shape
    N = w.shape[1]
    kern = functools.partial(_inproj1_kernel, row=2 if is_ctx else None)
    return pl.pallas_call(
        kern,
        out_shape=jax.ShapeDtypeStruct((G, R, N), BF16),
        grid=(G, R // tm, N // tn),
        in_specs=[
            pl.BlockSpec((None, tm, D_MODEL), lambda b, i, j: (b, i, 0)),
            pl.BlockSpec((1, D_MODEL), lambda b, i, j: (0, 0)),
            pl.BlockSpec((None, 8, D_MODEL), lambda b, i, j: (layer, 0, 0)),
            pl.BlockSpec((None, 8, D_MODEL), lambda b, i, j: (layer, 0, 1)),
            pl.BlockSpec((D_MODEL, tn), lambda b, i, j: (0, j)),
        ],
        out_specs=pl.BlockSpec((None, tm, tn), lambda b, i, j: (b, i, j)),
        scratch_shapes=[pltpu.VMEM((tm, D_MODEL), BF16)],
        compiler_params=_params(("parallel", "parallel", "arbitrary"), 48),
        name="inproj1_ctx" if is_ctx else "inproj1",
    )(x, g, mod, mod, w)


def _hyena_pre_kernel(u_ref, up_ref, un_ref, w_ref, b_ref, bias_ref, z_ref, x0_ref, zb_ref, *, n_tiles):
    i = pl.program_id(1)
    u = u_ref[...]
    tm = u.shape[0]
    prev_row = jnp.where(i > 0, up_ref[7:8, :], 0.0)
    next_row = jnp.where(i < n_tiles - 1, un_ref[0:1, :], 0.0)
    row = lax.broadcasted_iota(jnp.int32, u.shape, 0)
    before = jnp.where(row == 0, prev_row, pltpu.roll(u, 1, axis=0))
    after = jnp.where(row == tm - 1, next_row, pltpu.roll(u, tm - 1, axis=0))
    conv = before * w_ref[0:1, :] + u * w_ref[1:2, :] + after * w_ref[2:3, :] + b_ref[...]
    x0 = conv[:, :HY_WIDTH]
    z = conv[:, 2 * HY_WIDTH:] * conv[:, HY_WIDTH:2 * HY_WIDTH]
    z_ref[...] = z
    x0_ref[...] = x0
    zb_ref[...] = x0 * (z * bias_ref[...])


def _hyena_pre(u, conv_w, conv_b, bias, *, tm):
    G, R, _ = u.shape
    n_tiles = R // tm
    sub = tm // 8
    last8 = R // 8 - 1
    kern = functools.partial(_hyena_pre_kernel, n_tiles=n_tiles)
    out = jax.ShapeDtypeStruct((G, R, HY_WIDTH), F32)
    ospec = pl.BlockSpec((None, tm, HY_WIDTH), lambda b, i: (b, i, 0))
    return pl.pallas_call(
        kern,
        out_shape=(out, out, out),
        grid=(G, n_tiles),
        in_specs=[
            pl.BlockSpec((None, tm, EV_U), lambda b, i: (b, i, 0)),
            pl.BlockSpec((None, 8, EV_U), lambda b, i: (b, jnp.maximum(i * sub - 1, 0), 0)),
            pl.BlockSpec((None, 8, EV_U), lambda b, i: (b, jnp.minimum((i + 1) * sub, last8), 0)),
            pl.BlockSpec((3, EV_U), lambda b, i: (0, 0)),
            pl.BlockSpec((1, EV_U), lambda b, i: (0, 0)),
            pl.BlockSpec((1, HY_WIDTH), lambda b, i: (0, 0)),
        ],
        out_specs=(ospec, ospec, ospec),
        compiler_params=_params(("parallel", "parallel"), 48),
        name="hyena_pre",
    )(u, u, u, conv_w, conv_b, bias)


def _split_bf16(a):
    hi = a.astype(BF16)
    return hi, (a - hi.astype(F32)).astype(BF16)


def _dot3(a, b):
    a_hi, a_lo = _split_bf16(a)
    b_hi, b_lo = _split_bf16(b)
    dot = functools.partial(jnp.dot, preferred_element_type=F32)
    return dot(a_hi, b_hi) + (dot(a_lo, b_hi) + dot(a_hi, b_lo))


def _filter_kernel(w1t_ref, b1_ref, w2t_ref, b2_ref, fr_ref, w3_ref, dec_ref, bands_ref, h_ref, ss_ref, *, L, tm):
    i = pl.program_id(0)

    def time_of(r):
        return jnp.where(r < L, r, 2 * L - r).astype(F32) / L

    t_row = time_of(lax.broadcasted_iota(jnp.int32, (1, tm), 1) + i * tm)
    ang = (2.0 * math.pi * t_row) * bands_ref[...]
    t8 = jnp.where(lax.broadcasted_iota(jnp.int32, (8, tm), 0) == 0, t_row, 0.0)
    feat = jnp.concatenate([t8, jnp.cos(ang), -jnp.sin(ang)], axis=0)
    fr = fr_ref[...]
    hid = jnp.sin(fr * (jnp.dot(w1t_ref[...], feat, precision=HIGHEST, preferred_element_type=F32) + b1_ref[...]))
    hid = jnp.sin(fr * (jnp.dot(w2t_ref[...], hid, precision=HIGHEST, preferred_element_type=F32) + b2_ref[...]))
    h = _dot3(hid.T, w3_ref[...])
    r_col = lax.broadcasted_iota(jnp.int32, (tm, 1), 0) + i * tm
    h = h * jnp.exp(-time_of(r_col) * jnp.abs(dec_ref[...]))
    h = jnp.where(r_col == L, 0.0, h)
    h_ref[...] = h

    @pl.when(i == 0)
    def _():
        ss_ref[...] = jnp.zeros_like(ss_ref)

    ss_ref[...] += jnp.broadcast_to(jnp.sum(h * h, axis=0, keepdims=True), ss_ref.shape)


def _hyena_filter(w1, b1, w2, b2, w3, freq, decay, *, L):
    tm = min(L, 1024)
    tiles_per_side = L // tm
    kern = functools.partial(_filter_kernel, L=L, tm=tm)
    w1t = jnp.concatenate([w1[0:1], jnp.zeros((7, HY_HIDDEN), F32), w1[1:]], axis=0).T
    col = lambda a: a.reshape(HY_HIDDEN, 1)
    w3s = jnp.stack([w3[:, :HY_WIDTH], w3[:, HY_WIDTH:]])
    bands = jnp.asarray(np.linspace(1e-4, HY_BANDS - 1, HY_BANDS).astype(np.float32).reshape(HY_BANDS, 1))
    small = (w1t, col(b1), w2.T, col(b2), col(freq))
    full = lambda a: pl.BlockSpec(a.shape, lambda i: (0,) * a.ndim)
    side = lambda *blk: pl.BlockSpec((None,) + blk, lambda i: (i // tiles_per_side, 0, 0))
    return pl.pallas_call(
        kern,
        out_shape=(jax.ShapeDtypeStruct((2 * L, HY_WIDTH), F32), jax.ShapeDtypeStruct((8, HY_WIDTH), F32)),
        grid=(2 * L // tm,),
        in_specs=[full(a) for a in small] + [side(HY_HIDDEN, HY_WIDTH), side(1, HY_WIDTH), full(bands)],
        out_specs=(pl.BlockSpec((tm, HY_WIDTH), lambda i: (i, 0)), pl.BlockSpec((8, HY_WIDTH), lambda i: (0, 0))),
        compiler_params=_params(("arbitrary",), 48),
        name="hyena_filter_%d" % L,
    )(*small, w3s, decay.reshape(2, 1, HY_WIDTH), bands)


def _stack_bf16(re, im):
    return jnp.concatenate([re, im], axis=0).astype(BF16)


def _filter_fft_kernel(h_ref, ss_ref, m1_ref, f2_ref, o_ref, a_scr):
    R, P = FFT_R, FFT_PITCH
    inv_norm = lax.rsqrt(ss_ref[0:1, :])

    def stage1(n2, carry):
        xs = (h_ref[pl.ds(n2, R, stride=R), :] * inv_norm).astype(BF16)
        a = jnp.dot(m1_ref[n2], xs, preferred_element_type=F32)
        base = pl.multiple_of(n2 * P, 8)
        a_scr[0, pl.ds(base, R), :] = a[:R]
        a_scr[1, pl.ds(base, R), :] = a[R:]
        return carry

    lax.fori_loop(0, R, stage1, 0, unroll=FFT_UNROLL)

    def stage2(k1, carry):
        ar = a_scr[0, pl.ds(k1, R, stride=P), :]
        ai = a_scr[1, pl.ds(k1, R, stride=P), :]
        x = jnp.dot(f2_ref[...], _stack_bf16(ar, ai), preferred_element_type=F32)
        base = pl.multiple_of(k1 * R, R)
        o_ref[0, pl.ds(base, R), :] = x[:R].astype(BF16)
        o_ref[1, pl.ds(base, R), :] = x[R:].astype(BF16)
        return carry

    lax.fori_loop(0, R, stage2, 0, unroll=FFT_UNROLL)


def _filter_fft(h_circ, ss, m1, f2):
    ct = LANES
    once = pl.Buffered(1)
    return pl.pallas_call(
        _filter_fft_kernel,
        out_shape=jax.ShapeDtypeStruct((2, FFT_N, HY_WIDTH), BF16),
        grid=(HY_WIDTH // ct,),
        in_specs=[
            pl.BlockSpec((FFT_N, ct), lambda c: (0, c), pipeline_mode=once),
            pl.BlockSpec((8, ct), lambda c: (0, c)),
            pl.BlockSpec(m1.shape, lambda c: (0, 0, 0), pipeline_mode=once),
            pl.BlockSpec(f2.shape, lambda c: (0, 0), pipeline_mode=once),
        ],
        out_specs=pl.BlockSpec((2, FFT_N, ct), lambda c: (0, 0, c)),
        scratch_shapes=[pltpu.VMEM((2, FFT_R * FFT_PITCH, ct), F32)],
        compiler_params=_params(("parallel",), 58),
        name="hyena_filter_fft",
    )(h_circ, ss, m1, f2)


def _fft_conv_kernel(z_ref, hs_ref, m1_ref, f2_ref, g2_ref, m1i_ref, y_ref, a_scr):
    R, P = FFT_R, FFT_PITCH

    def stage1(n2, carry):
        xs = z_ref[pl.ds(n2, R // 2, stride=R), :].astype(BF16)
        a = jnp.dot(m1_ref[n2, :, 0:R // 2], xs, preferred_element_type=F32)
        base = pl.multiple_of(n2 * P, 8)
        a_scr[0, pl.ds(base, R), :] = a[:R]
        a_scr[1, pl.ds(base, R), :] = a[R:]
        return carry

    lax.fori_loop(0, R, stage1, 0, unroll=FFT_UNROLL)

    def stage2(k1, carry):
        ar = a_scr[0, pl.ds(k1, R, stride=P), :]
        ai = a_scr[1, pl.ds(k1, R, stride=P), :]
        x = jnp.dot(f2_ref[...], _stack_bf16(ar, ai), preferred_element_type=F32)
        base = pl.multiple_of(k1 * R, R)
        hr = hs_ref[0, pl.ds(base, R), :].astype(F32)
        hi = hs_ref[1, pl.ds(base, R), :].astype(F32)
        xr, xi = x[:R], x[R:]
        yr = xr * hr - xi * hi
        yi = xr * hi + xi * hr
        bq = jnp.dot(g2_ref[...], _stack_bf16(yr, yi), preferred_element_type=F32)
        a_scr[0, pl.ds(k1, R, stride=P), :] = bq[:R]
        a_scr[1, pl.ds(k1, R, stride=P), :] = bq[R:]
        return carry

    lax.fori_loop(0, R, stage2, 0, unroll=FFT_UNROLL)

    def stage3(n2, carry):
        base = pl.multiple_of(n2 * P, 8)
        br = a_scr[0, pl.ds(base, R), :]
        bi = a_scr[1, pl.ds(base, R), :]
        y = jnp.dot(m1i_ref[n2], _stack_bf16(br, bi), preferred_element_type=F32)
        y_ref[pl.ds(n2, R // 2, stride=R), :] = y
        return carry

    lax.fori_loop(0, R, stage3, 0, unroll=FFT_UNROLL)


def _fft_conv(z, hspec, m1, f2, g2, m1i):
    G = z.shape[0]
    ct = LANES
    once = pl.Buffered(1)
    return pl.pallas_call(
        _fft_conv_kernel,
        out_shape=jax.ShapeDtypeStruct((G, SEQ, HY_WIDTH), F32),
        grid=(HY_WIDTH // ct, G),
        in_specs=[
            pl.BlockSpec((None, SEQ, ct), lambda c, b: (b, 0, c)),
            pl.BlockSpec((2, FFT_N, ct), lambda c, b: (0, 0, c), pipeline_mode=once),
            pl.BlockSpec(m1.shape, lambda c, b: (0, 0, 0), pipeline_mode=once),
            pl.BlockSpec(f2.shape, lambda c, b: (0, 0), pipeline_mode=once),
            pl.BlockSpec(g2.shape, lambda c, b: (0, 0), pipeline_mode=once),
            pl.BlockSpec(m1i.shape, lambda c, b: (0, 0, 0), pipeline_mode=once),
        ],
        out_specs=pl.BlockSpec((None, SEQ, ct), lambda c, b: (b, 0, c)),
        scratch_shapes=[pltpu.VMEM((2, FFT_R * FFT_PITCH, ct), F32)],
        compiler_params=_params(("parallel", "arbitrary"), 58),
        name="hyena_fft_conv",
    )(z, hspec, m1, f2, g2, m1i)


def _ctx_conv_kernel(z_ref, h_ref, ss_ref, fwd_ref, inv_ref, y_ref):
    N = 2 * CTX_LEN
    hn = (h_ref[...] * lax.rsqrt(ss_ref[0:1, :])).astype(BF16)
    hs = jnp.dot(fwd_ref[...], hn, preferred_element_type=F32)
    zs = jnp.dot(fwd_ref[:, :CTX_LEN], z_ref[...].astype(BF16), preferred_element_type=F32)
    hr, hi, zr, zi = hs[:N], hs[N:], zs[:N], zs[N:]
    ys = _stack_bf16(zr * hr - zi * hi, zr * hi + zi * hr)
    y_ref[...] = jnp.dot(inv_ref[...], ys, preferred_element_type=F32)


def _ctx_conv(z, h_circ, ss, fwd, inv):
    G = z.shape[0]
    full = lambda a: pl.BlockSpec(a.shape, lambda b: (0,) * a.ndim)
    return pl.pallas_call(
        _ctx_conv_kernel,
        out_shape=jax.ShapeDtypeStruct((G, CTX_LEN, HY_WIDTH), F32),
        grid=(G,),
        in_specs=[pl.BlockSpec((None, CTX_LEN, HY_WIDTH), lambda b: (b, 0, 0)),
                  full(h_circ), full(ss), full(fwd), full(inv)],
        out_specs=pl.BlockSpec((None, CTX_LEN, HY_WIDTH), lambda b: (b, 0, 0)),
        compiler_params=_params(("parallel",), 48),
        name="hyena_ctx_conv",
    )(z, h_circ, ss, fwd, inv)


def _attn_kernel(sink_ref, q_ref, *refs, local, n_blocks):
    if local:
        kp_ref, kc_ref, kn_ref, kx_ref, vp_ref, vc_ref, vn_ref, vx_ref, o_ref = refs
    else:
        kx_ref, vx_ref, o_ref = refs
    n = pl.program_id(1)
    q = q_ref[...]
    lane = lax.broadcasted_iota(jnp.int32, (BLOCK, LANES), 1)
    low = lane < HEAD_DIM
    zero = jnp.zeros((), BF16)

    if local:
        qi = lax.broadcasted_iota(jnp.int32, (ATT_GROUP * BLOCK, BLOCK), 0) % BLOCK
        ki = lax.broadcasted_iota(jnp.int32, (ATT_GROUP * BLOCK, BLOCK), 1)
        ok_prev = (ki >= qi) & (n > 0)
        ok_next = (ki <= qi) & (n < n_blocks - 1)

    def keys(refs_g, g):
        kcols = slice(g * LANES, (g + 1) * LANES)
        return jnp.concatenate([r[:, kcols] for r in refs_g], axis=0) if local else refs_g[0][:, kcols]

    def logits(g):
        k = keys((kp_ref, kc_ref, kn_ref, kx_ref) if local else (kx_ref,), g)
        parts = []
        for hh in range(ATT_GROUP):
            h = g * ATT_GROUP + hh
            tile = q[:, (h // 2) * LANES:(h // 2 + 1) * LANES]
            parts.append(jnp.where(low if h % 2 == 0 else ~low, tile, zero))
        qs = jnp.concatenate(parts, axis=0)
        s = lax.dot_general(qs, k, (((1,), (1,)), ((), ())), preferred_element_type=F32)
        if local:
            s = jnp.concatenate([jnp.where(ok_prev, s[:, :BLOCK], NEG_INF), s[:, BLOCK:2 * BLOCK],
                                 jnp.where(ok_next, s[:, 2 * BLOCK:3 * BLOCK], NEG_INF), s[:, 3 * BLOCK:]], axis=1)
        return s

    all_logits = [logits(g) for g in range(ATT_KV_HEADS)]

    low4 = jnp.concatenate([low] * ATT_GROUP, axis=0)
    for g in range(ATT_KV_HEADS):
        s = all_logits[g]
        v = keys((vp_ref, vc_ref, vn_ref, vx_ref) if local else (vx_ref,), g)
        v_aug = jnp.where(lax.broadcasted_iota(jnp.int32, v.shape, 1) < HEAD_DIM, v, jnp.ones((), BF16))
        sink = jnp.concatenate(
            [jnp.full((BLOCK, 1), sink_ref[g * ATT_GROUP + hh] * LOG2E, F32) for hh in range(ATT_GROUP)], axis=0)
        m = jnp.maximum(jnp.max(s, axis=-1, keepdims=True), sink)
        e = jnp.exp2(s - m).astype(BF16)
        o = jnp.dot(e, v_aug, preferred_element_type=F32) + jnp.where(low4, 0.0, jnp.exp2(sink - m))
        swapped = pltpu.roll(o, HEAD_DIM, axis=1)
        for pair in range(ATT_GROUP // 2):
            ev = slice((2 * pair) * BLOCK, (2 * pair + 1) * BLOCK)
            od = slice((2 * pair + 1) * BLOCK, (2 * pair + 2) * BLOCK)
            even = o[ev] / swapped[ev]
            odd = swapped[od] / o[od]
            t = g * (ATT_GROUP // 2) + pair
            o_ref[:, t * LANES:(t + 1) * LANES] = jnp.where(low, even, odd).astype(o_ref.dtype)


def _attention(sink, qkv, qkv_ctx, *, local):
    src = qkv if local else qkv_ctx
    G, R, _ = src.shape
    nb = R // BLOCK
    kcol, vcol = ATT_WIDTH // 256, ATT_WIDTH // 256 + 1
    q_spec = pl.BlockSpec((None, BLOCK, ATT_WIDTH), lambda b, n: (b, n, 0))
    ctx_k = pl.BlockSpec((None, CTX_LEN, 256), lambda b, n: (b, 0, kcol))
    ctx_v = pl.BlockSpec((None, CTX_LEN, 256), lambda b, n: (b, 0, vcol))
    if local:
        def nbr(col, off):
            return pl.BlockSpec((None, BLOCK, 256), lambda b, n: (b, jnp.clip(n + off, 0, nb - 1), col))
        in_specs = [q_spec, nbr(kcol, -1), nbr(kcol, 0), nbr(kcol, 1), ctx_k,
                    nbr(vcol, -1), nbr(vcol, 0), nbr(vcol, 1), ctx_v]
        args = (qkv, qkv, qkv, qkv, qkv_ctx, qkv, qkv, qkv, qkv_ctx)
    else:
        in_specs = [q_spec, ctx_k, ctx_v]
        args = (qkv_ctx, qkv_ctx, qkv_ctx)
    kern = functools.partial(_attn_kernel, local=local, n_blocks=nb)
    return pl.pallas_call(
        kern,
        out_shape=jax.ShapeDtypeStruct((G, R, ATT_WIDTH), BF16),
        grid=(G, nb),
        in_specs=[pl.BlockSpec(memory_space=pltpu.SMEM)] + in_specs,
        out_specs=pl.BlockSpec((None, BLOCK, ATT_WIDTH), lambda b, n: (b, n, 0)),
        compiler_params=_params(("parallel", "parallel"), 48),
        name="attn_local" if local else "attn_ctx",
    )(sink, *args)


def _mlp_step(f, n_f, x_ref, gm_ref, shm_ref, scm_ref, gtm_ref, w1_ref, w2_ref, fg_ref, o_ref,
              x1_scr, h_scr, acc_scr, r, final_norm):
    a = jnp.dot(h_scr[...], w1_ref[...], preferred_element_type=F32)
    a = jnp.maximum(a, 0.0)
    acc_scr[...] += jnp.dot((a * a).astype(BF16), w2_ref[...], preferred_element_type=F32)

    @pl.when(f == n_f - 1)
    def _():
        out = x1_scr[...] + _mod_row(gtm_ref, r) * acc_scr[...]
        if final_norm:
            out = (out * lax.rsqrt(jnp.mean(out * out, axis=-1, keepdims=True) + EPS)) * fg_ref[...]
        o_ref[...] = out


def _mix_prologue(y, x_ref, gta_ref, gm_ref, shm_ref, scm_ref, x1_scr, h_scr, acc_scr, r):
    x1 = x_ref[...] + _mod_row(gta_ref, r) * y
    x1_scr[...] = x1
    h_scr[...] = _norm_mod(x1, gm_ref[...], _mod_row(shm_ref, r), _mod_row(scm_ref, r)).astype(BF16)
    acc_scr[...] = jnp.zeros_like(acc_scr)


def _mix0_kernel(x_ref, x0_ref, y_ref, zb_ref, att_ref, wo_ref, gta_ref, gm_ref, shm_ref, scm_ref, gtm_ref,
                 w1_ref, w2_ref, fg_ref, o_ref, x1_scr, h_scr, acc_scr, *, row, n_f, final_norm):
    r = pl.program_id(0) if row is None else row
    f = pl.program_id(2)

    @pl.when(f == 0)
    def _():
        hy = (x0_ref[...] * y_ref[...] + zb_ref[...]).astype(BF16)
        y = (jnp.dot(hy, wo_ref[:HY_WIDTH, :], preferred_element_type=F32)
             + jnp.dot(att_ref[...], wo_ref[HY_WIDTH:, :], preferred_element_type=F32))
        _mix_prologue(y, x_ref, gta_ref, gm_ref, shm_ref, scm_ref, x1_scr, h_scr, acc_scr, r)

    _mlp_step(f, n_f, x_ref, gm_ref, shm_ref, scm_ref, gtm_ref, w1_ref, w2_ref, fg_ref, o_ref,
              x1_scr, h_scr, acc_scr, r, final_norm)


def _mix1_kernel(x_ref, of_ref, ob_ref, wo_ref, gta_ref, gm_ref, shm_ref, scm_ref, gtm_ref,
                 w1_ref, w2_ref, fg_ref, o_ref, x1_scr, h_scr, acc_scr, *, row, n_f, final_norm):
    r = pl.program_id(0) if row is None else row
    f = pl.program_id(2)

    @pl.when(f == 0)
    def _():
        a = (of_ref[...].astype(F32) + ob_ref[...].astype(F32)).astype(BF16)
        y = jnp.dot(a, wo_ref[...], preferred_element_type=F32)
        _mix_prologue(y, x_ref, gta_ref, gm_ref, shm_ref, scm_ref, x1_scr, h_scr, acc_scr, r)

    _mlp_step(f, n_f, x_ref, gm_ref, shm_ref, scm_ref, gtm_ref, w1_ref, w2_ref, fg_ref, o_ref,
              x1_scr, h_scr, acc_scr, r, final_norm)


def _mix_mlp(kind, x, mix_in, wo, mod, gm, w1, w2, fg, layer, *, is_ctx, tm, fc, final_norm):
    G, R, _ = x.shape
    n_f = D_FF // fc
    row_spec = lambda w: pl.BlockSpec((None, tm, w), lambda b, i, f: (b, i, 0))
    modk = lambda k: pl.BlockSpec((None, 8, D_MODEL), lambda b, i, f: (layer, 0, k))
    vec = pl.BlockSpec((1, D_MODEL), lambda b, i, f: (0, 0))
    if kind == 0:
        body = _mix0_kernel
        mix_specs = [row_spec(HY_WIDTH)] * 3 + [row_spec(ATT_WIDTH)]
    else:
        body = _mix1_kernel
        mix_specs = [row_spec(RET_V)] * 2
    kern = functools.partial(body, row=2 if is_ctx else None, n_f=n_f, final_norm=final_norm)
    return pl.pallas_call(
        kern,
        out_shape=jax.ShapeDtypeStruct((G, R, D_MODEL), F32),
        grid=(G, R // tm, n_f),
        in_specs=[row_spec(D_MODEL)] + mix_specs + [
            pl.BlockSpec(wo.shape, lambda b, i, f: (0, 0)),
            modk(2), vec, modk(3), modk(4), modk(5),
            pl.BlockSpec((D_MODEL, fc), lambda b, i, f: (0, f)),
            pl.BlockSpec((fc, D_MODEL), lambda b, i, f: (f, 0)),
            vec,
        ],
        out_specs=row_spec(D_MODEL),
        scratch_shapes=[pltpu.VMEM((tm, D_MODEL), F32), pltpu.VMEM((tm, D_MODEL), BF16),
                        pltpu.VMEM((tm, D_MODEL), F32)],
        compiler_params=_params(("parallel", "parallel", "arbitrary"), 56),
        name="mix_mlp%d%s" % (kind, "_ctx" if is_ctx else ""),
    )(x, *mix_in, wo, mod, gm, mod, mod, mod, w1, w2, fg)


def _ret_kernel(lr_ref, qkv_c, qkv_f, g_f, qkv_b, g_b, cos_ref, sin_ref, of_ref, ob_ref,
                state, dmask, xi, zeta, gch):
    j = pl.program_id(1)
    C = RET_C
    kscale = RET_DK ** -0.5

    @pl.when(j == 0)
    def _():
        state[...] = jnp.zeros_like(state)
        row = lax.broadcasted_iota(jnp.int32, (C, C), 0).astype(F32)
        col = lax.broadcasted_iota(jnp.int32, (C, C), 1).astype(F32)
        for d in range(2):
            e = row - col if d == 0 else col - row
            p = row if d == 0 else (C - 1) - row
            for h in range(RET_HEADS):
                lg = -jnp.exp(jnp.full((C, C), lr_ref[d, h], F32))
                dmask[d, h] = jnp.where(e >= 0, jnp.exp(e * lg), 0.0) * kscale
                xi[d, h] = jnp.exp((p + 1.0) * lg)
                zeta[d, h] = jnp.exp(((C - 1) - p) * lg) * kscale
                gch[d, h] = jnp.exp(C * -jnp.exp(jnp.full((8, C), lr_ref[d, h], F32)))

    half = RET_DK // 2

    def chain(d, h, src_ref, g_ref, o_ref):
        cos, sin = cos_ref[d], sin_ref[d]

        def rot(c0):
            t1 = src_ref[:, c0:c0 + half].astype(F32)
            t2 = src_ref[:, c0 + half:c0 + RET_DK].astype(F32)
            return jnp.concatenate([t1 * cos - t2 * sin, t2 * cos + t1 * sin], axis=1)

        q = rot(h * RET_DK)
        k = rot(RET_QK + h * RET_DK)
        v = src_ref[:, 2 * RET_QK + h * RET_DV:2 * RET_QK + (h + 1) * RET_DV]
        inner = lax.dot_general(q.astype(BF16), k.astype(BF16), (((1,), (1,)), ((), ())),
                                preferred_element_type=F32) * dmask[d, h]
        s_old = state[d, h]
        kv = lax.dot_general((k * zeta[d, h]).astype(BF16), v, (((0,), (0,)), ((), ())),
                             preferred_element_type=F32)
        state[d, h] = gch[d, h, 0:1, 0:1] * s_old + kv
        if o_ref is not None:
            o = (jnp.dot(inner.astype(BF16), v, preferred_element_type=F32)
                 + jnp.dot((q * xi[d, h]).astype(BF16), s_old.astype(BF16), preferred_element_type=F32))
            on = o * lax.rsqrt(jnp.mean(o * o, axis=-1, keepdims=True) + EPS)
            gate = g_ref[:, h * RET_DV:(h + 1) * RET_DV].astype(F32)
            o_ref[:, h * RET_DV:(h + 1) * RET_DV] = (gate * (1.0 / (1.0 + jnp.exp(-gate))) * on).astype(o_ref.dtype)

    @pl.when(j == 0)
    def _():
        for d in range(2):
            for h in range(RET_HEADS):
                chain(d, h, qkv_c, None, None)

    @pl.when(j > 0)
    def _():
        for h in range(RET_HEADS):
            chain(0, h, qkv_f, g_f, of_ref)
            chain(1, h, qkv_b, g_b, ob_ref)


def _retention(log_rate, p_ctx, p_lat):
    G = p_lat.shape[0]
    C = RET_C
    n_lat = SEQ // C
    cos, sin = _rope1d_tables()
    cos, sin = jnp.asarray(cos), jnp.asarray(sin)

    def fwd_chunk(j):
        return jnp.maximum(j - 1, 0)

    def bwd_chunk(j):
        return n_lat - 1 - jnp.maximum(j - 1, 0)

    qkv_w = 2 * RET_QK + RET_V
    gcol = qkv_w // RET_V
    tab = pl.BlockSpec((2, C, RET_DK // 2), lambda b, j: (0, j, 0))
    in_specs = [
        pl.BlockSpec(memory_space=pltpu.SMEM),
        pl.BlockSpec((None, C, qkv_w), lambda b, j: (b, 0, 0)),
        pl.BlockSpec((None, C, qkv_w), lambda b, j: (b, fwd_chunk(j), 0)),
        pl.BlockSpec((None, C, RET_V), lambda b, j: (b, fwd_chunk(j), gcol)),
        pl.BlockSpec((None, C, qkv_w), lambda b, j: (b, bwd_chunk(j), 0)),
        pl.BlockSpec((None, C, RET_V), lambda b, j: (b, bwd_chunk(j), gcol + 1)),
        tab, tab,
    ]
    out = jax.ShapeDtypeStruct((G, SEQ, RET_V), BF16)
    per_chain = lambda *tail: pltpu.VMEM((2, RET_HEADS) + tail, F32)
    return pl.pallas_call(
        _ret_kernel,
        out_shape=(out, out),
        grid=(G, 1 + n_lat),
        in_specs=in_specs,
        out_specs=(pl.BlockSpec((None, C, RET_V), lambda b, j: (b, fwd_chunk(j), 0)),
                   pl.BlockSpec((None, C, RET_V), lambda b, j: (b, bwd_chunk(j), 0))),
        scratch_shapes=[per_chain(RET_DK, RET_DV), per_chain(C, C), per_chain(C, C), per_chain(C, C),
                        per_chain(8, C)],
        compiler_params=_params(("parallel", "arbitrary"), 48),
        name="retention",
    )(log_rate, p_ctx, p_lat, p_lat, p_lat, p_lat, cos, sin)


def _ev_weight(w_in):
    i_q = EV_U
    i_k = i_q + ATT_WIDTH
    i_v = i_k + KV_WIDTH
    cols = [w_in[:, :i_q], w_in[:, i_q:i_k] * (HEAD_DIM ** -0.5 * LOG2E)]
    for base in (i_k, i_v):
        for g in range(ATT_KV_HEADS):
            part = w_in[:, base + g * HEAD_DIM: base + (g + 1) * HEAD_DIM]
            cols += [part, part]
    return jnp.concatenate(cols, axis=1).astype(BF16)


def kernel(x, c, ctx, c_ctx, ada_w, ada_b, norm_mix_g, norm_mlp_g, mlp_w1, mlp_w2, ev_w_in, ev_w_out, hy_conv_w, hy_conv_b, hy_w1, hy_b1, hy_w2, hy_b2, hy_w3, hy_freq, hy_decay, hy_bias, attn_sink, od_w_in, od_w_out, ret_log_rate, final_g):
    D = D_MODEL
    cvec = jnp.concatenate([c, c_ctx[None, :], jnp.zeros((8 - BATCH - 1, D), F32)], axis=0)
    mod = _ada(cvec, ada_w, ada_b)

    m1, m1i, f2, g2 = (jnp.asarray(a).astype(BF16) for a in _fft_mats())
    cfwd, cinv = (jnp.asarray(a).astype(BF16) for a in _ctx_fft_mats())
    fg = final_g.reshape(1, D)

    gmix = norm_mix_g[0].reshape(1, D)
    gmlp = norm_mlp_g[0].reshape(1, D)
    w_in = _ev_weight(ev_w_in[0])
    u_l, qkv_l = _inproj0(x, gmix, mod, w_in, 0, is_ctx=False, tm=512)
    u_c, qkv_c = _inproj0(ctx, gmix, mod, w_in, 0, is_ctx=True, tm=CTX_LEN)

    conv_b = hy_conv_b[0].reshape(1, EV_U)
    bias = hy_bias[0].reshape(1, HY_WIDTH)
    z_l, x0_l, zb_l = _hyena_pre(u_l, hy_conv_w[0], conv_b, bias, tm=512)
    z_c, x0_c, zb_c = _hyena_pre(u_c, hy_conv_w[0], conv_b, bias, tm=CTX_LEN)

    filt = (hy_w1[0], hy_b1[0], hy_w2[0], hy_b2[0], hy_w3[0], hy_freq[0], hy_decay[0])
    h_l, ss_l = _hyena_filter(*filt, L=SEQ)
    h_c, ss_c = _hyena_filter(*filt, L=CTX_LEN)
    hspec = _filter_fft(h_l, ss_l, m1, f2)
    y_l = _fft_conv(z_l, hspec, m1, f2, g2, m1i)
    y_c = _ctx_conv(z_c, h_c, ss_c, cfwd, cinv)

    sink = attn_sink[0]
    att_l = _attention(sink, qkv_l, qkv_c, local=True)
    att_c = _attention(sink, qkv_l, qkv_c, local=False)

    wo = ev_w_out[0].astype(BF16)
    w1 = mlp_w1[0].astype(BF16)
    w2 = mlp_w2[0].astype(BF16)
    x1 = _mix_mlp(0, x, (x0_l, y_l, zb_l, att_l), wo, mod, gmlp, w1, w2, fg, 0,
                  is_ctx=False, tm=512, fc=1024, final_norm=False)
    ctx1 = _mix_mlp(0, ctx, (x0_c, y_c, zb_c, att_c), wo, mod, gmlp, w1, w2, fg, 0,
                    is_ctx=True, tm=CTX_LEN, fc=1024, final_norm=False)

    gmix = norm_mix_g[1].reshape(1, D)
    gmlp = norm_mlp_g[1].reshape(1, D)
    w_in = od_w_in[0].astype(BF16)
    p_l = _inproj1(x1, gmix, mod, w_in, 1, is_ctx=False, tm=1024, tn=2048)
    p_c = _inproj1(ctx1, gmix, mod, w_in, 1, is_ctx=True, tm=CTX_LEN, tn=2048)
    o_fwd, o_bwd = _retention(ret_log_rate[0], p_c, p_l)

    wo = od_w_out[0].astype(BF16)
    w1 = mlp_w1[1].astype(BF16)
    w2 = mlp_w2[1].astype(BF16)
    return _mix_mlp(1, x1, (o_fwd, o_bwd), wo, mod, gmlp, w1, w2, fg, 1,
                    is_ctx=False, tm=512, fc=1024, final_norm=True)
```

```python
import functools
import math

import numpy as np
import jax
import jax.numpy as jnp
from jax import lax
from jax.experimental import pallas as pl
from jax.experimental.pallas import tpu as pltpu

F32 = jnp.float32
BF16 = jnp.bfloat16
HIGHEST = lax.Precision.HIGHEST

D_MODEL = 1024
BATCH = 2
SEQ = 8192
DEPTH = 2
GRID_W = 64
CTX_LEN = 256
EPS = 1e-6
NEG_INF = -1e30
N_MOD = 6
D_FF = 4 * D_MODEL
ROPE_BASE = 10000.0

HY_WIDTH = D_MODEL // 2
HY_EMB = 33
HY_BANDS = (HY_EMB - 1) // 2
HY_HIDDEN = 64

ATT_HEADS = 8
ATT_KV_HEADS = 2
ATT_GROUP = ATT_HEADS // ATT_KV_HEADS
HEAD_DIM = 64
ATT_WIDTH = ATT_HEADS * HEAD_DIM
KV_WIDTH = ATT_KV_HEADS * HEAD_DIM
BLOCK = 128

RET_HEADS = 4
RET_DK = D_MODEL // RET_HEADS
RET_DV = 2 * RET_DK
RET_QK = RET_HEADS * RET_DK
RET_V = RET_HEADS * RET_DV
OD_IN = 2 * RET_QK + 3 * RET_V

LOG2E = 1.4426950408889634
LANES = 128
MIB = 1024 * 1024

EV_U = 3 * HY_WIDTH
EV_QKV = ATT_WIDTH + 4 * LANES
EV_COLS = EV_U + EV_QKV

FFT_N = 2 * SEQ
FFT_R = 128
FFT_PITCH = FFT_R + 8
SUB_ROWS = 256
FFT_UNROLL = 16
RET_C = 256


def _params(sem, vmem_mib):
    return pltpu.CompilerParams(dimension_semantics=sem, vmem_limit_bytes=vmem_mib * MIB)


@functools.lru_cache(maxsize=None)
def _rope2d_tables():
    quarter = HEAD_DIM // 4
    inv = ROPE_BASE ** (-np.arange(quarter, dtype=np.float64) / quarter)
    t = np.arange(SEQ)
    pos = np.stack([t // GRID_W, t % GRID_W], axis=1).astype(np.float64)
    lane = np.arange(HEAD_DIM)
    half = lane // (HEAD_DIM // 2)
    e = lane % (HEAD_DIM // 2)
    ang = pos[:, half] * inv[e % quarter][None, :]
    sign = np.where(e < quarter, -1.0, 1.0)[None, :]
    cos = np.tile(np.cos(ang), (1, 2)).astype(np.float32)
    sin = np.tile(np.sin(ang) * sign, (1, 2)).astype(np.float32)
    return cos, sin


@functools.lru_cache(maxsize=None)
def _rope1d_tables():
    n = RET_DK // 2
    inv = ROPE_BASE ** (-np.linspace(0.0, 1.0, n))
    pos = np.arange(CTX_LEN + SEQ, dtype=np.float64)
    ang = pos[:, None] * inv[None, :]
    cos, sin = np.cos(ang), np.sin(ang)

    def rev(a):
        return a.reshape(-1, RET_C, n)[:, ::-1].reshape(-1, n)

    cos2 = np.stack([cos, rev(cos)]).astype(np.float32)
    sin2 = np.stack([sin, rev(sin)]).astype(np.float32)
    return cos2, sin2


@functools.lru_cache(maxsize=None)
def _fft_mats():
    N, R = FFT_N, FFT_R
    k1 = np.arange(R)
    n2 = np.arange(R)[:, None, None]
    n1 = np.arange(R)[None, None, :]
    n = n2 + R * n1
    idx = (k1[None, :, None] * n) % N
    ang = 2.0 * np.pi * idx / N
    c, s = np.cos(ang), np.sin(ang)
    m1 = np.concatenate([c, -s], axis=1).astype(np.float32)
    m1i = np.concatenate([c.transpose(0, 2, 1), -s.transpose(0, 2, 1)], axis=2)
    m1i = m1i[:, :R // 2].astype(np.float32)
    a2 = 2.0 * np.pi * ((np.arange(R)[:, None] * np.arange(R)[None, :]) % R) / R
    fr, fi = np.cos(a2), -np.sin(a2)
    f2 = np.block([[fr, -fi], [fi, fr]]).astype(np.float32)
    g2 = (np.block([[fr, fi], [-fi, fr]]) / N).astype(np.float32)
    return m1, m1i, f2, g2


@functools.lru_cache(maxsize=None)
def _ctx_fft_mats():
    N = 2 * CTX_LEN
    k = np.arange(N)[:, None]
    n = np.arange(N)[None, :]
    ang = 2.0 * np.pi * ((k * n) % N) / N
    c, s = np.cos(ang), np.sin(ang)
    fwd = np.concatenate([c, -s], axis=0).astype(np.float32)
    inv = (np.concatenate([c, -s], axis=1)[:CTX_LEN] / N).astype(np.float32)
    return fwd, inv


def _ada_kernel(c_ref, w_ref, b_ref, o_ref):
    c = c_ref[...]
    a = c * (1.0 / (1.0 + jnp.exp(-c)))
    o_ref[...] = jnp.dot(a, w_ref[...], precision=HIGHEST, preferred_element_type=F32) + b_ref[...]


def _ada(cvec, ada_w, ada_b):
    tn = 1536
    return pl.pallas_call(
        _ada_kernel,
        out_shape=jax.ShapeDtypeStruct((DEPTH, 8, N_MOD * D_MODEL), F32),
        grid=(DEPTH, N_MOD * D_MODEL // tn),
        in_specs=[
            pl.BlockSpec((8, D_MODEL), lambda i, j: (0, 0)),
            pl.BlockSpec((None, D_MODEL, tn), lambda i, j: (i, 0, j)),
            pl.BlockSpec((None, 1, tn), lambda i, j: (i, 0, j)),
        ],
        out_specs=pl.BlockSpec((None, 8, tn), lambda i, j: (i, 0, j)),
        compiler_params=_params(("parallel", "parallel"), 40),
        name="ada_mod",
    )(cvec, ada_w, ada_b.reshape(DEPTH, 1, N_MOD * D_MODEL))


def _mod_row(ref, row):
    if isinstance(row, int):
        return ref[row:row + 1, :]
    return ref[pl.ds(row, 1), :]


def _norm_mod(x, g, shift, scale):
    y = x * lax.rsqrt(jnp.mean(x * x, axis=-1, keepdims=True) + EPS)
    return (y * g) * (1.0 + scale) + shift


def _rope_tile(x, cos, sin_signed):
    lane = lax.broadcasted_iota(jnp.int32, x.shape, 1)
    first = (lane % 32) < 16
    partner = jnp.where(first, pltpu.roll(x, LANES - 16, axis=1), pltpu.roll(x, 16, axis=1))
    return x * cos + partner * sin_signed


def _inproj0_kernel(x_ref, g_ref, sh_ref, sc_ref, w_ref, cos_ref, sin_ref, u_ref, qkv_ref, *, row, rope):
    r = pl.program_id(0) if row is None else row
    g, sh, sc = g_ref[...], _mod_row(sh_ref, r), _mod_row(sc_ref, r)
    tn = 512
    n_rot = (ATT_WIDTH + 2 * LANES) // LANES
    sub = min(x_ref.shape[0], SUB_ROWS)
    for s in range(x_ref.shape[0] // sub):
        rows = slice(s * sub, (s + 1) * sub)
        h = _norm_mod(x_ref[rows, :], g, sh, sc).astype(BF16)
        for j in range(EV_U // tn):
            u_ref[rows, j * tn:(j + 1) * tn] = jnp.dot(h, w_ref[:, j * tn:(j + 1) * tn],
                                                       preferred_element_type=F32)
        for j in range(EV_QKV // tn):
            y = jnp.dot(h, w_ref[:, EV_U + j * tn:EV_U + (j + 1) * tn], preferred_element_type=F32)
            for t in range(tn // LANES):
                yt = y[:, t * LANES:(t + 1) * LANES]
                if rope and j * (tn // LANES) + t < n_rot:
                    yt = _rope_tile(yt, cos_ref[rows, :], sin_ref[rows, :])
                qkv_ref[rows, j * tn + t * LANES:j * tn + (t + 1) * LANES] = yt.astype(BF16)


def _inproj0(x, g, mod, w, layer, *, is_ctx, tm):
    G, R, _ = x.shape
    cos, sin = _rope2d_tables()
    cos, sin = jnp.asarray(cos), jnp.asarray(sin)
    if is_ctx:
        tab = pl.BlockSpec((tm, LANES), lambda b, i: (0, 0))
    else:
        tab = pl.BlockSpec((tm, LANES), lambda b, i: (i, 0))
    kern = functools.partial(_inproj0_kernel, row=2 if is_ctx else None, rope=not is_ctx)
    return pl.pallas_call(
        kern,
        out_shape=(jax.ShapeDtypeStruct((G, R, EV_U), F32), jax.ShapeDtypeStruct((G, R, EV_QKV), BF16)),
        grid=(G, R // tm),
        in_specs=[
            pl.BlockSpec((None, tm, D_MODEL), lambda b, i: (b, i, 0)),
            pl.BlockSpec((1, D_MODEL), lambda b, i: (0, 0)),
            pl.BlockSpec((None, 8, D_MODEL), lambda b, i: (layer, 0, 0)),
            pl.BlockSpec((None, 8, D_MODEL), lambda b, i: (layer, 0, 1)),
            pl.BlockSpec((D_MODEL, EV_COLS), lambda b, i: (0, 0), pipeline_mode=pl.Buffered(1)),
            tab, tab,
        ],
        out_specs=(pl.BlockSpec((None, tm, EV_U), lambda b, i: (b, i, 0)),
                   pl.BlockSpec((None, tm, EV_QKV), lambda b, i: (b, i, 0))),
        compiler_params=_params(("parallel", "parallel"), 48),
        name="inproj0_ctx" if is_ctx else "inproj0",
    )(x, g, mod, mod, w, cos, sin)


def _inproj1_kernel(x_ref, g_ref, sh_ref, sc_ref, w_ref, o_ref, *, row, tn):
    r = pl.program_id(0) if row is None else row
    g, sh, sc = g_ref[...], _mod_row(sh_ref, r), _mod_row(sc_ref, r)
    sub = min(x_ref.shape[0], SUB_ROWS)
    for s in range(x_ref.shape[0] // sub):
        rows = slice(s * sub, (s + 1) * sub)
        h = _norm_mod(x_ref[rows, :], g, sh, sc).astype(BF16)
        for j in range(w_ref.shape[1] // tn):
            cols = slice(j * tn, (j + 1) * tn)
            o_ref[rows, cols] = jnp.dot(h, w_ref[:, cols], preferred_element_type=F32).astype(o_ref.dtype)


def _inproj1(x, g, mod, w, layer, *, is_ctx, tm, tn):
    G, R, _ = x.shape
    N = w.shape[1]
    kern = functools.partial(_inproj1_kernel, row=2 if is_ctx else None, tn=tn)
    return pl.pallas_call(
        kern,
        out_shape=jax.ShapeDtypeStruct((G, R, N), BF16),
        grid=(G, R // tm),
        in_specs=[
            pl.BlockSpec((None, tm, D_MODEL), lambda b, i: (b, i, 0)),
            pl.BlockSpec((1, D_MODEL), lambda b, i: (0, 0)),
            pl.BlockSpec((None, 8, D_MODEL), lambda b, i: (layer, 0, 0)),
            pl.BlockSpec((None, 8, D_MODEL), lambda b, i: (layer, 0, 1)),
            pl.BlockSpec((D_MODEL, N), lambda b, i: (0, 0), pipeline_mode=pl.Buffered(1)),
        ],
        out_specs=pl.BlockSpec((None, tm, N), lambda b, i: (b, i, 0)),
        compiler_params=_params(("parallel", "parallel"), 48),
        name="inproj1_ctx" if is_ctx else "inproj1",
    )(x, g, mod, mod, w)


def _hyena_pre_kernel(u_ref, up_ref, un_ref, w_ref, b_ref, bias_ref, z_ref, x0_ref, zb_ref, *, n_tiles):
    i = pl.program_id(1)
    u = u_ref[...]
    tm = u.shape[0]
    prev_row = jnp.where(i > 0, up_ref[7:8, :], 0.0)
    next_row = jnp.where(i < n_tiles - 1, un_ref[0:1, :], 0.0)
    row = lax.broadcasted_iota(jnp.int32, u.shape, 0)
    before = jnp.where(row == 0, prev_row, pltpu.roll(u, 1, axis=0))
    after = jnp.where(row == tm - 1, next_row, pltpu.roll(u, tm - 1, axis=0))
    conv = before * w_ref[0:1, :] + u * w_ref[1:2, :] + after * w_ref[2:3, :] + b_ref[...]
    x0 = conv[:, :HY_WIDTH]
    z = conv[:, 2 * HY_WIDTH:] * conv[:, HY_WIDTH:2 * HY_WIDTH]
    z_ref[...] = z
    x0_ref[...] = x0
    zb_ref[...] = x0 * (z * bias_ref[...])


def _hyena_pre(u, conv_w, conv_b, bias, *, tm):
    G, R, _ = u.shape
    n_tiles = R // tm
    sub = tm // 8
    last8 = R // 8 - 1
    kern = functools.partial(_hyena_pre_kernel, n_tiles=n_tiles)
    out = jax.ShapeDtypeStruct((G, R, HY_WIDTH), F32)
    ospec = pl.BlockSpec((None, tm, HY_WIDTH), lambda b, i: (b, i, 0))
    return pl.pallas_call(
        kern,
        out_shape=(out, out, out),
        grid=(G, n_tiles),
        in_specs=[
            pl.BlockSpec((None, tm, EV_U), lambda b, i: (b, i, 0)),
            pl.BlockSpec((None, 8, EV_U), lambda b, i: (b, jnp.maximum(i * sub - 1, 0), 0)),
            pl.BlockSpec((None, 8, EV_U), lambda b, i: (b, jnp.minimum((i + 1) * sub, last8), 0)),
            pl.BlockSpec((3, EV_U), lambda b, i: (0, 0)),
            pl.BlockSpec((1, EV_U), lambda b, i: (0, 0)),
            pl.BlockSpec((1, HY_WIDTH), lambda b, i: (0, 0)),
        ],
        out_specs=(ospec, ospec, ospec),
        compiler_params=_params(("parallel", "parallel"), 48),
        name="hyena_pre",
    )(u, u, u, conv_w, conv_b, bias)


def _split_bf16(a):
    hi = a.astype(BF16)
    return hi, (a - hi.astype(F32)).astype(BF16)


def _dot3(a, b):
    a_hi, a_lo = _split_bf16(a)
    b_hi, b_lo = _split_bf16(b)
    dot = functools.partial(jnp.dot, preferred_element_type=F32)
    return dot(a_hi, b_hi) + (dot(a_lo, b_hi) + dot(a_hi, b_lo))


def _filter_kernel(w1t_ref, b1_ref, w2t_ref, b2_ref, fr_ref, w3_ref, dec_ref, bands_ref, h_ref, ss_ref, *, L, tm):
    i = pl.program_id(0)

    def time_of(r):
        return jnp.where(r < L, r, 2 * L - r).astype(F32) / L

    t_row = time_of(lax.broadcasted_iota(jnp.int32, (1, tm), 1) + i * tm)
    ang = (2.0 * math.pi * t_row) * bands_ref[...]
    t8 = jnp.where(lax.broadcasted_iota(jnp.int32, (8, tm), 0) == 0, t_row, 0.0)
    feat = jnp.concatenate([t8, jnp.cos(ang), -jnp.sin(ang)], axis=0)
    fr = fr_ref[...]
    hid = jnp.sin(fr * (jnp.dot(w1t_ref[...], feat, precision=HIGHEST, preferred_element_type=F32) + b1_ref[...]))
    hid = jnp.sin(fr * (jnp.dot(w2t_ref[...], hid, precision=HIGHEST, preferred_element_type=F32) + b2_ref[...]))
    h = _dot3(hid.T, w3_ref[...])
    r_col = lax.broadcasted_iota(jnp.int32, (tm, 1), 0) + i * tm
    h = h * jnp.exp(-time_of(r_col) * jnp.abs(dec_ref[...]))
    h = jnp.where(r_col == L, 0.0, h)
    h_ref[...] = h

    @pl.when(i == 0)
    def _():
        ss_ref[...] = jnp.zeros_like(ss_ref)

    ss_ref[...] += jnp.broadcast_to(jnp.sum(h * h, axis=0, keepdims=True), ss_ref.shape)


def _hyena_filter(w1, b1, w2, b2, w3, freq, decay, *, L):
    tm = min(L, 1024)
    tiles_per_side = L // tm
    kern = functools.partial(_filter_kernel, L=L, tm=tm)
    w1t = jnp.concatenate([w1[0:1], jnp.zeros((7, HY_HIDDEN), F32), w1[1:]], axis=0).T
    col = lambda a: a.reshape(HY_HIDDEN, 1)
    w3s = jnp.stack([w3[:, :HY_WIDTH], w3[:, HY_WIDTH:]])
    bands = jnp.asarray(np.linspace(1e-4, HY_BANDS - 1, HY_BANDS).astype(np.float32).reshape(HY_BANDS, 1))
    small = (w1t, col(b1), w2.T, col(b2), col(freq))
    full = lambda a: pl.BlockSpec(a.shape, lambda i: (0,) * a.ndim)
    side = lambda *blk: pl.BlockSpec((None,) + blk, lambda i: (i // tiles_per_side, 0, 0))
    return pl.pallas_call(
        kern,
        out_shape=(jax.ShapeDtypeStruct((2 * L, HY_WIDTH), F32), jax.ShapeDtypeStruct((8, HY_WIDTH), F32)),
        grid=(2 * L // tm,),
        in_specs=[full(a) for a in small] + [side(HY_HIDDEN, HY_WIDTH), side(1, HY_WIDTH), full(bands)],
        out_specs=(pl.BlockSpec((tm, HY_WIDTH), lambda i: (i, 0)), pl.BlockSpec((8, HY_WIDTH), lambda i: (0, 0))),
        compiler_params=_params(("arbitrary",), 48),
        name="hyena_filter_%d" % L,
    )(*small, w3s, decay.reshape(2, 1, HY_WIDTH), bands)


def _stack_bf16(re, im):
    return jnp.concatenate([re, im], axis=0).astype(BF16)


def _filter_fft_kernel(h_ref, ss_ref, m1_ref, f2_ref, o_ref, a_scr):
    R, P = FFT_R, FFT_PITCH
    inv_norm = lax.rsqrt(ss_ref[0:1, :])

    def stage1(n2, carry):
        xs = (h_ref[pl.ds(n2, R, stride=R), :] * inv_norm).astype(BF16)
        a = jnp.dot(m1_ref[n2], xs, preferred_element_type=F32)
        base = pl.multiple_of(n2 * P, 8)
        a_scr[0, pl.ds(base, R), :] = a[:R]
        a_scr[1, pl.ds(base, R), :] = a[R:]
        return carry

    lax.fori_loop(0, R, stage1, 0, unroll=FFT_UNROLL)

    def stage2(k1, carry):
        ar = a_scr[0, pl.ds(k1, R, stride=P), :]
        ai = a_scr[1, pl.ds(k1, R, stride=P), :]
        x = jnp.dot(f2_ref[...], _stack_bf16(ar, ai), preferred_element_type=F32)
        base = pl.multiple_of(k1 * R, R)
        o_ref[0, pl.ds(base, R), :] = x[:R].astype(BF16)
        o_ref[1, pl.ds(base, R), :] = x[R:].astype(BF16)
        return carry

    lax.fori_loop(0, R, stage2, 0, unroll=FFT_UNROLL)


def _filter_fft(h_circ, ss, m1, f2):
    ct = LANES
    once = pl.Buffered(1)
    return pl.pallas_call(
        _filter_fft_kernel,
        out_shape=jax.ShapeDtypeStruct((2, FFT_N, HY_WIDTH), BF16),
        grid=(HY_WIDTH // ct,),
        in_specs=[
            pl.BlockSpec((FFT_N, ct), lambda c: (0, c), pipeline_mode=once),
            pl.BlockSpec((8, ct), lambda c: (0, c)),
            pl.BlockSpec(m1.shape, lambda c: (0, 0, 0), pipeline_mode=once),
            pl.BlockSpec(f2.shape, lambda c: (0, 0), pipeline_mode=once),
        ],
        out_specs=pl.BlockSpec((2, FFT_N, ct), lambda c: (0, 0, c)),
        scratch_shapes=[pltpu.VMEM((2, FFT_R * FFT_PITCH, ct), F32)],
        compiler_params=_params(("parallel",), 58),
        name="hyena_filter_fft",
    )(h_circ, ss, m1, f2)


def _fft_conv_kernel(z_ref, hs_ref, m1_ref, f2_ref, g2_ref, m1i_ref, y_ref, a_scr):
    R, P = FFT_R, FFT_PITCH

    def stage1(n2, carry):
        xs = z_ref[pl.ds(n2, R // 2, stride=R), :].astype(BF16)
        a = jnp.dot(m1_ref[n2, :, 0:R // 2], xs, preferred_element_type=F32)
        base = pl.multiple_of(n2 * P, 8)
        a_scr[0, pl.ds(base, R), :] = a[:R]
        a_scr[1, pl.ds(base, R), :] = a[R:]
        return carry

    lax.fori_loop(0, R, stage1, 0, unroll=FFT_UNROLL)

    def stage2(k1, carry):
        ar = a_scr[0, pl.ds(k1, R, stride=P), :]
        ai = a_scr[1, pl.ds(k1, R, stride=P), :]
        x = jnp.dot(f2_ref[...], _stack_bf16(ar, ai), preferred_element_type=F32)
        base = pl.multiple_of(k1 * R, R)
        hr = hs_ref[0, pl.ds(base, R), :].astype(F32)
        hi = hs_ref[1, pl.ds(base, R), :].astype(F32)
        xr, xi = x[:R], x[R:]
        yr = xr * hr - xi * hi
        yi = xr * hi + xi * hr
        bq = jnp.dot(g2_ref[...], _stack_bf16(yr, yi), preferred_element_type=F32)
        a_scr[0, pl.ds(k1, R, stride=P), :] = bq[:R]
        a_scr[1, pl.ds(k1, R, stride=P), :] = bq[R:]
        return carry

    lax.fori_loop(0, R, stage2, 0, unroll=FFT_UNROLL)

    def stage3(n2, carry):
        base = pl.multiple_of(n2 * P, 8)
        br = a_scr[0, pl.ds(base, R), :]
        bi = a_scr[1, pl.ds(base, R), :]
        y = jnp.dot(m1i_ref[n2], _stack_bf16(br, bi), preferred_element_type=F32)
        y_ref[pl.ds(n2, R // 2, stride=R), :] = y
        return carry

    lax.fori_loop(0, R, stage3, 0, unroll=FFT_UNROLL)


def _fft_conv(z, hspec, m1, f2, g2, m1i):
    G = z.shape[0]
    ct = LANES
    once = pl.Buffered(1)
    return pl.pallas_call(
        _fft_conv_kernel,
        out_shape=jax.ShapeDtypeStruct((G, SEQ, HY_WIDTH), F32),
        grid=(HY_WIDTH // ct, G),
        in_specs=[
            pl.BlockSpec((None, SEQ, ct), lambda c, b: (b, 0, c)),
            pl.BlockSpec((2, FFT_N, ct), lambda c, b: (0, 0, c), pipeline_mode=once),
            pl.BlockSpec(m1.shape, lambda c, b: (0, 0, 0), pipeline_mode=once),
            pl.BlockSpec(f2.shape, lambda c, b: (0, 0), pipeline_mode=once),
            pl.BlockSpec(g2.shape, lambda c, b: (0, 0), pipeline_mode=once),
            pl.BlockSpec(m1i.shape, lambda c, b: (0, 0, 0), pipeline_mode=once),
        ],
        out_specs=pl.BlockSpec((None, SEQ, ct), lambda c, b: (b, 0, c)),
        scratch_shapes=[pltpu.VMEM((2, FFT_R * FFT_PITCH, ct), F32)],
        compiler_params=_params(("parallel", "arbitrary"), 58),
        name="hyena_fft_conv",
    )(z, hspec, m1, f2, g2, m1i)


def _ctx_conv_kernel(z_ref, h_ref, ss_ref, fwd_ref, inv_ref, y_ref):
    N = 2 * CTX_LEN
    hn = (h_ref[...] * lax.rsqrt(ss_ref[0:1, :])).astype(BF16)
    hs = jnp.dot(fwd_ref[...], hn, preferred_element_type=F32)
    zs = jnp.dot(fwd_ref[:, :CTX_LEN], z_ref[...].astype(BF16), preferred_element_type=F32)
    hr, hi, zr, zi = hs[:N], hs[N:], zs[:N], zs[N:]
    ys = _stack_bf16(zr * hr - zi * hi, zr * hi + zi * hr)
    y_ref[...] = jnp.dot(inv_ref[...], ys, preferred_element_type=F32)


def _ctx_conv(z, h_circ, ss, fwd, inv):
    G = z.shape[0]
    full = lambda a: pl.BlockSpec(a.shape, lambda b: (0,) * a.ndim)
    return pl.pallas_call(
        _ctx_conv_kernel,
        out_shape=jax.ShapeDtypeStruct((G, CTX_LEN, HY_WIDTH), F32),
        grid=(G,),
        in_specs=[pl.BlockSpec((None, CTX_LEN, HY_WIDTH), lambda b: (b, 0, 0)),
                  full(h_circ), full(ss), full(fwd), full(inv)],
        out_specs=pl.BlockSpec((None, CTX_LEN, HY_WIDTH), lambda b: (b, 0, 0)),
        compiler_params=_params(("parallel",), 48),
        name="hyena_ctx_conv",
    )(z, h_circ, ss, fwd, inv)


def _attn_kernel(sink_ref, q_ref, *refs, local, n_blocks):
    if local:
        kp_ref, kc_ref, kn_ref, kx_ref, vp_ref, vc_ref, vn_ref, vx_ref, o_ref = refs
    else:
        kx_ref, vx_ref, o_ref = refs
    n = pl.program_id(1)
    q = q_ref[...]
    lane = lax.broadcasted_iota(jnp.int32, (BLOCK, LANES), 1)
    low = lane < HEAD_DIM
    zero = jnp.zeros((), BF16)

    if local:
        qi = lax.broadcasted_iota(jnp.int32, (ATT_GROUP * BLOCK, BLOCK), 0) % BLOCK
        ki = lax.broadcasted_iota(jnp.int32, (ATT_GROUP * BLOCK, BLOCK), 1)
        ok_prev = (ki >= qi) & (n > 0)
        ok_next = (ki <= qi) & (n < n_blocks - 1)

    def keys(refs_g, g):
        kcols = slice(g * LANES, (g + 1) * LANES)
        return jnp.concatenate([r[:, kcols] for r in refs_g], axis=0) if local else refs_g[0][:, kcols]

    def logits(g):
        k = keys((kp_ref, kc_ref, kn_ref, kx_ref) if local else (kx_ref,), g)
        parts = []
        for hh in range(ATT_GROUP):
            h = g * ATT_GROUP + hh
            tile = q[:, (h // 2) * LANES:(h // 2 + 1) * LANES]
            parts.append(jnp.where(low if h % 2 == 0 else ~low, tile, zero))
        qs = jnp.concatenate(parts, axis=0)
        s = lax.dot_general(qs, k, (((1,), (1,)), ((), ())), preferred_element_type=F32)
        if local:
            s = jnp.concatenate([jnp.where(ok_prev, s[:, :BLOCK], NEG_INF), s[:, BLOCK:2 * BLOCK],
                                 jnp.where(ok_next, s[:, 2 * BLOCK:3 * BLOCK], NEG_INF), s[:, 3 * BLOCK:]], axis=1)
        return s

    all_logits = [logits(g) for g in range(ATT_KV_HEADS)]

    low4 = jnp.concatenate([low] * ATT_GROUP, axis=0)
    for g in range(ATT_KV_HEADS):
        s = all_logits[g]
        v = keys((vp_ref, vc_ref, vn_ref, vx_ref) if local else (vx_ref,), g)
        v_aug = jnp.where(lax.broadcasted_iota(jnp.int32, v.shape, 1) < HEAD_DIM, v, jnp.ones((), BF16))
        sink = jnp.concatenate(
            [jnp.full((BLOCK, 1), sink_ref[g * ATT_GROUP + hh] * LOG2E, F32) for hh in range(ATT_GROUP)], axis=0)
        m = jnp.maximum(jnp.max(s, axis=-1, keepdims=True), sink)
        e = jnp.exp2(s - m).astype(BF16)
        o = jnp.dot(e, v_aug, preferred_element_type=F32) + jnp.where(low4, 0.0, jnp.exp2(sink - m))
        swapped = pltpu.roll(o, HEAD_DIM, axis=1)
        for pair in range(ATT_GROUP // 2):
            ev = slice((2 * pair) * BLOCK, (2 * pair + 1) * BLOCK)
            od = slice((2 * pair + 1) * BLOCK, (2 * pair + 2) * BLOCK)
            even = o[ev] / swapped[ev]
            odd = swapped[od] / o[od]
            t = g * (ATT_GROUP // 2) + pair
            o_ref[:, t * LANES:(t + 1) * LANES] = jnp.where(low, even, odd).astype(o_ref.dtype)


def _attention(sink, qkv, qkv_ctx, *, local):
    src = qkv if local else qkv_ctx
    G, R, _ = src.shape
    nb = R // BLOCK
    kcol, vcol = ATT_WIDTH // 256, ATT_WIDTH // 256 + 1
    q_spec = pl.BlockSpec((None, BLOCK, ATT_WIDTH), lambda b, n: (b, n, 0))
    ctx_k = pl.BlockSpec((None, CTX_LEN, 256), lambda b, n: (b, 0, kcol))
    ctx_v = pl.BlockSpec((None, CTX_LEN, 256), lambda b, n: (b, 0, vcol))
    if local:
        def nbr(col, off):
            return pl.BlockSpec((None, BLOCK, 256), lambda b, n: (b, jnp.clip(n + off, 0, nb - 1), col))
        in_specs = [q_spec, nbr(kcol, -1), nbr(kcol, 0), nbr(kcol, 1), ctx_k,
                    nbr(vcol, -1), nbr(vcol, 0), nbr(vcol, 1), ctx_v]
        args = (qkv, qkv, qkv, qkv, qkv_ctx, qkv, qkv, qkv, qkv_ctx)
    else:
        in_specs = [q_spec, ctx_k, ctx_v]
        args = (qkv_ctx, qkv_ctx, qkv_ctx)
    kern = functools.partial(_attn_kernel, local=local, n_blocks=nb)
    return pl.pallas_call(
        kern,
        out_shape=jax.ShapeDtypeStruct((G, R, ATT_WIDTH), BF16),
        grid=(G, nb),
        in_specs=[pl.BlockSpec(memory_space=pltpu.SMEM)] + in_specs,
        out_specs=pl.BlockSpec((None, BLOCK, ATT_WIDTH), lambda b, n: (b, n, 0)),
        compiler_params=_params(("parallel", "parallel"), 48),
        name="attn_local" if local else "attn_ctx",
    )(sink, *args)


def _mixer_out0(rows, x0_ref, y_ref, zb_ref, att_ref, wo_ref):
    hy = (x0_ref[rows, :] * y_ref[rows, :] + zb_ref[rows, :]).astype(BF16)
    return (jnp.dot(hy, wo_ref[:HY_WIDTH, :], preferred_element_type=F32)
            + jnp.dot(att_ref[rows, :], wo_ref[HY_WIDTH:, :], preferred_element_type=F32))


def _mixer_out1(rows, of_ref, ob_ref, wo_ref):
    a = (of_ref[rows, :].astype(F32) + ob_ref[rows, :].astype(F32)).astype(BF16)
    return jnp.dot(a, wo_ref[...], preferred_element_type=F32)


def _mix_kernel(*refs, mixer_out, n_mix, row, fc, final_norm):
    x_ref = refs[0]
    mix_refs = refs[1:2 + n_mix]
    gta_ref, gm_ref, shm_ref, scm_ref, gtm_ref, w1_ref, w2_ref, fg_ref, o_ref, a_scr = refs[2 + n_mix:]
    r = pl.program_id(0) if row is None else row
    gta, gtm = _mod_row(gta_ref, r), _mod_row(gtm_ref, r)
    gm, shm, scm = gm_ref[...], _mod_row(shm_ref, r), _mod_row(scm_ref, r)
    sub = min(x_ref.shape[0], SUB_ROWS)
    for s in range(x_ref.shape[0] // sub):
        rows = slice(s * sub, (s + 1) * sub)
        x1 = x_ref[rows, :] + gta * mixer_out(rows, *mix_refs)
        h = _norm_mod(x1, gm, shm, scm).astype(BF16)
        for c in range(D_FF // fc):
            a = jnp.maximum(jnp.dot(h, w1_ref[:, c * fc:(c + 1) * fc], preferred_element_type=F32), 0.0)
            a_scr[rows, c * fc:(c + 1) * fc] = (a * a).astype(BF16)
        out = x1 + gtm * jnp.dot(a_scr[rows, :], w2_ref[...], preferred_element_type=F32)
        if final_norm:
            out = (out * lax.rsqrt(jnp.mean(out * out, axis=-1, keepdims=True) + EPS)) * fg_ref[...]
        o_ref[rows, :] = out


def _mix_mlp(kind, x, mix_in, wo, mod, gm, w1, w2, fg, layer, *, is_ctx, tm, fc, final_norm):
    G, R, _ = x.shape
    row_spec = lambda w: pl.BlockSpec((None, tm, w), lambda b, i: (b, i, 0))
    modk = lambda k: pl.BlockSpec((None, 8, D_MODEL), lambda b, i: (layer, 0, k))
    vec = pl.BlockSpec((1, D_MODEL), lambda b, i: (0, 0))
    resident = lambda a: pl.BlockSpec(a.shape, lambda b, i: (0, 0), pipeline_mode=pl.Buffered(1))
    if kind == 0:
        mixer_out = _mixer_out0
        mix_specs = [row_spec(HY_WIDTH)] * 3 + [row_spec(ATT_WIDTH)]
    else:
        mixer_out = _mixer_out1
        mix_specs = [row_spec(RET_V)] * 2
    kern = functools.partial(_mix_kernel, mixer_out=mixer_out, n_mix=len(mix_in), row=2 if is_ctx else None,
                             fc=fc, final_norm=final_norm)
    return pl.pallas_call(
        kern,
        out_shape=jax.ShapeDtypeStruct((G, R, D_MODEL), F32),
        grid=(G, R // tm),
        in_specs=[row_spec(D_MODEL)] + mix_specs + [
            resident(wo), modk(2), vec, modk(3), modk(4), modk(5), resident(w1), resident(w2), vec,
        ],
        out_specs=row_spec(D_MODEL),
        scratch_shapes=[pltpu.VMEM((tm, D_FF), BF16)],
        compiler_params=_params(("parallel", "parallel"), 56),
        name="mix_mlp%d%s" % (kind, "_ctx" if is_ctx else ""),
    )(x, *mix_in, wo, mod, gm, mod, mod, mod, w1, w2, fg)


def _ret_kernel(lr_ref, qkv_c, qkv_f, g_f, qkv_b, g_b, cos_ref, sin_ref, of_ref, ob_ref,
                state, dmask, xi, zeta, gch):
    j = pl.program_id(1)
    C = RET_C
    kscale = RET_DK ** -0.5

    @pl.when(j == 0)
    def _():
        state[...] = jnp.zeros_like(state)
        row = lax.broadcasted_iota(jnp.int32, (C, C), 0).astype(F32)
        col = lax.broadcasted_iota(jnp.int32, (C, C), 1).astype(F32)
        for d in range(2):
            e = row - col if d == 0 else col - row
            p = row if d == 0 else (C - 1) - row
            for h in range(RET_HEADS):
                lg = -jnp.exp(jnp.full((C, C), lr_ref[d, h], F32))
                dmask[d, h] = jnp.where(e >= 0, jnp.exp(e * lg), 0.0) * kscale
                xi[d, h] = jnp.exp((p + 1.0) * lg)
                zeta[d, h] = jnp.exp(((C - 1) - p) * lg) * kscale
                gch[d, h] = jnp.exp(C * -jnp.exp(jnp.full((8, C), lr_ref[d, h], F32)))

    half = RET_DK // 2

    def chain(d, h, src_ref, g_ref, o_ref):
        cos, sin = cos_ref[d], sin_ref[d]

        def rot(c0):
            t1 = src_ref[:, c0:c0 + half].astype(F32)
            t2 = src_ref[:, c0 + half:c0 + RET_DK].astype(F32)
            return jnp.concatenate([t1 * cos - t2 * sin, t2 * cos + t1 * sin], axis=1)

        q = rot(h * RET_DK)
        k = rot(RET_QK + h * RET_DK)
        v = src_ref[:, 2 * RET_QK + h * RET_DV:2 * RET_QK + (h + 1) * RET_DV]
        inner = lax.dot_general(q.astype(BF16), k.astype(BF16), (((1,), (1,)), ((), ())),
                                preferred_element_type=F32) * dmask[d, h]
        s_old = state[d, h]
        kv = lax.dot_general((k * zeta[d, h]).astype(BF16), v, (((0,), (0,)), ((), ())),
                             preferred_element_type=F32)
        state[d, h] = gch[d, h, 0:1, 0:1] * s_old + kv
        if o_ref is not None:
            o = (jnp.dot(inner.astype(BF16), v, preferred_element_type=F32)
                 + jnp.dot((q * xi[d, h]).astype(BF16), s_old.astype(BF16), preferred_element_type=F32))
            on = o * lax.rsqrt(jnp.mean(o * o, axis=-1, keepdims=True) + EPS)
            gate = g_ref[:, h * RET_DV:(h + 1) * RET_DV].astype(F32)
            o_ref[:, h * RET_DV:(h + 1) * RET_DV] = (gate * (1.0 / (1.0 + jnp.exp(-gate))) * on).astype(o_ref.dtype)

    @pl.when(j == 0)
    def _():
        for d in range(2):
            for h in range(RET_HEADS):
                chain(d, h, qkv_c, None, None)

    @pl.when(j > 0)
    def _():
        for h in range(RET_HEADS):
            chain(0, h, qkv_f, g_f, of_ref)
            chain(1, h, qkv_b, g_b, ob_ref)


def _retention(log_rate, p_ctx, p_lat):
    G = p_lat.shape[0]
    C = RET_C
    n_lat = SEQ // C
    cos, sin = _rope1d_tables()
    cos, sin = jnp.asarray(cos), jnp.asarray(sin)

    def fwd_chunk(j):
        return jnp.maximum(j - 1, 0)

    def bwd_chunk(j):
        return n_lat - 1 - jnp.maximum(j - 1, 0)

    qkv_w = 2 * RET_QK + RET_V
    gcol = qkv_w // RET_V
    tab = pl.BlockSpec((2, C, RET_DK // 2), lambda b, j: (0, j, 0))
    in_specs = [
        pl.BlockSpec(memory_space=pltpu.SMEM),
        pl.BlockSpec((None, C, qkv_w), lambda b, j: (b, 0, 0)),
        pl.BlockSpec((None, C, qkv_w), lambda b, j: (b, fwd_chunk(j), 0)),
        pl.BlockSpec((None, C, RET_V), lambda b, j: (b, fwd_chunk(j), gcol)),
        pl.BlockSpec((None, C, qkv_w), lambda b, j: (b, bwd_chunk(j), 0)),
        pl.BlockSpec((None, C, RET_V), lambda b, j: (b, bwd_chunk(j), gcol + 1)),
        tab, tab,
    ]
    out = jax.ShapeDtypeStruct((G, SEQ, RET_V), BF16)
    per_chain = lambda *tail: pltpu.VMEM((2, RET_HEADS) + tail, F32)
    return pl.pallas_call(
        _ret_kernel,
        out_shape=(out, out),
        grid=(G, 1 + n_lat),
        in_specs=in_specs,
        out_specs=(pl.BlockSpec((None, C, RET_V), lambda b, j: (b, fwd_chunk(j), 0)),
                   pl.BlockSpec((None, C, RET_V), lambda b, j: (b, bwd_chunk(j), 0))),
        scratch_shapes=[per_chain(RET_DK, RET_DV), per_chain(C, C), per_chain(C, C), per_chain(C, C),
                        per_chain(8, C)],
        compiler_params=_params(("parallel", "arbitrary"), 48),
        name="retention",
    )(log_rate, p_ctx, p_lat, p_lat, p_lat, p_lat, cos, sin)


def _ev_weight(w_in):
    i_q = EV_U
    i_k = i_q + ATT_WIDTH
    i_v = i_k + KV_WIDTH
    cols = [w_in[:, :i_q], w_in[:, i_q:i_k] * (HEAD_DIM ** -0.5 * LOG2E)]
    for base in (i_k, i_v):
        for g in range(ATT_KV_HEADS):
            part = w_in[:, base + g * HEAD_DIM: base + (g + 1) * HEAD_DIM]
            cols += [part, part]
    return jnp.concatenate(cols, axis=1).astype(BF16)


def kernel(x, c, ctx, c_ctx, ada_w, ada_b, norm_mix_g, norm_mlp_g, mlp_w1, mlp_w2, ev_w_in, ev_w_out, hy_conv_w, hy_conv_b, hy_w1, hy_b1, hy_w2, hy_b2, hy_w3, hy_freq, hy_decay, hy_bias, attn_sink, od_w_in, od_w_out, ret_log_rate, final_g):
    D = D_MODEL
    cvec = jnp.concatenate([c, c_ctx[None, :], jnp.zeros((8 - BATCH - 1, D), F32)], axis=0)
    mod = _ada(cvec, ada_w, ada_b)

    m1, m1i, f2, g2 = (jnp.asarray(a).astype(BF16) for a in _fft_mats())
    cfwd, cinv = (jnp.asarray(a).astype(BF16) for a in _ctx_fft_mats())
    fg = final_g.reshape(1, D)

    gmix = norm_mix_g[0].reshape(1, D)
    gmlp = norm_mlp_g[0].reshape(1, D)
    w_in = _ev_weight(ev_w_in[0])
    u_l, qkv_l = _inproj0(x, gmix, mod, w_in, 0, is_ctx=False, tm=512)
    u_c, qkv_c = _inproj0(ctx, gmix, mod, w_in, 0, is_ctx=True, tm=CTX_LEN)

    conv_b = hy_conv_b[0].reshape(1, EV_U)
    bias = hy_bias[0].reshape(1, HY_WIDTH)
    z_l, x0_l, zb_l = _hyena_pre(u_l, hy_conv_w[0], conv_b, bias, tm=512)
    z_c, x0_c, zb_c = _hyena_pre(u_c, hy_conv_w[0], conv_b, bias, tm=CTX_LEN)

    filt = (hy_w1[0], hy_b1[0], hy_w2[0], hy_b2[0], hy_w3[0], hy_freq[0], hy_decay[0])
    h_l, ss_l = _hyena_filter(*filt, L=SEQ)
    h_c, ss_c = _hyena_filter(*filt, L=CTX_LEN)
    hspec = _filter_fft(h_l, ss_l, m1, f2)
    y_l = _fft_conv(z_l, hspec, m1, f2, g2, m1i)
    y_c = _ctx_conv(z_c, h_c, ss_c, cfwd, cinv)

    sink = attn_sink[0]
    att_l = _attention(sink, qkv_l, qkv_c, local=True)
    att_c = _attention(sink, qkv_l, qkv_c, local=False)

    wo = ev_w_out[0].astype(BF16)
    w1 = mlp_w1[0].astype(BF16)
    w2 = mlp_w2[0].astype(BF16)
    x1 = _mix_mlp(0, x, (x0_l, y_l, zb_l, att_l), wo, mod, gmlp, w1, w2, fg, 0,
                  is_ctx=False, tm=512, fc=512, final_norm=False)
    ctx1 = _mix_mlp(0, ctx, (x0_c, y_c, zb_c, att_c), wo, mod, gmlp, w1, w2, fg, 0,
                    is_ctx=True, tm=CTX_LEN, fc=512, final_norm=False)

    gmix = norm_mix_g[1].reshape(1, D)
    gmlp = norm_mlp_g[1].reshape(1, D)
    w_in = od_w_in[0].astype(BF16)
    p_l = _inproj1(x1, gmix, mod, w_in, 1, is_ctx=False, tm=512, tn=512)
    p_c = _inproj1(ctx1, gmix, mod, w_in, 1, is_ctx=True, tm=CTX_LEN, tn=512)
    o_fwd, o_bwd = _retention(ret_log_rate[0], p_c, p_l)

    wo = od_w_out[0].astype(BF16)
    w1 = mlp_w1[1].astype(BF16)
    w2 = mlp_w2[1].astype(BF16)
    return _mix_mlp(1, x1, (o_fwd, o_bwd), wo, mod, gmlp, w1, w2, fg, 1,
                    is_ctx=False, tm=512, fc=512, final_norm=True)
```

```python
import functools
import math

import numpy as np
import jax
import jax.numpy as jnp
from jax import lax
from jax.experimental import pallas as pl
from jax.experimental.pallas import tpu as pltpu

F32 = jnp.float32
BF16 = jnp.bfloat16
HIGHEST = lax.Precision.HIGHEST

D_MODEL = 1024
BATCH = 2
SEQ = 8192
DEPTH = 2
GRID_W = 64
CTX_LEN = 256
EPS = 1e-6
NEG_INF = -1e30
N_MOD = 6
D_FF = 4 * D_MODEL
ROPE_BASE = 10000.0

HY_WIDTH = D_MODEL // 2
HY_EMB = 33
HY_BANDS = (HY_EMB - 1) // 2
HY_HIDDEN = 64

ATT_HEADS = 8
ATT_KV_HEADS = 2
ATT_GROUP = ATT_HEADS // ATT_KV_HEADS
HEAD_DIM = 64
ATT_WIDTH = ATT_HEADS * HEAD_DIM
KV_WIDTH = ATT_KV_HEADS * HEAD_DIM
BLOCK = 128

RET_HEADS = 4
RET_DK = D_MODEL // RET_HEADS
RET_DV = 2 * RET_DK
RET_QK = RET_HEADS * RET_DK
RET_V = RET_HEADS * RET_DV
OD_IN = 2 * RET_QK + 3 * RET_V

LOG2E = 1.4426950408889634
LANES = 128
MIB = 1024 * 1024

EV_U = 3 * HY_WIDTH
EV_QKV = ATT_WIDTH + 4 * LANES
EV_COLS = EV_U + EV_QKV

FFT_N = 2 * SEQ
FFT_R = 128
FFT_PITCH = FFT_R + 8
SUB_ROWS = 256
FFT_UNROLL = 16
RET_C = 256


def _params(sem, vmem_mib):
    return pltpu.CompilerParams(dimension_semantics=sem, vmem_limit_bytes=vmem_mib * MIB)


@functools.lru_cache(maxsize=None)
def _rope2d_tables():
    quarter = HEAD_DIM // 4
    inv = ROPE_BASE ** (-np.arange(quarter, dtype=np.float64) / quarter)
    t = np.arange(SEQ)
    pos = np.stack([t // GRID_W, t % GRID_W], axis=1).astype(np.float64)
    lane = np.arange(HEAD_DIM)
    half = lane // (HEAD_DIM // 2)
    e = lane % (HEAD_DIM // 2)
    ang = pos[:, half] * inv[e % quarter][None, :]
    sign = np.where(e < quarter, -1.0, 1.0)[None, :]
    cos = np.tile(np.cos(ang), (1, 2)).astype(np.float32)
    sin = np.tile(np.sin(ang) * sign, (1, 2)).astype(np.float32)
    return cos, sin


@functools.lru_cache(maxsize=None)
def _rope1d_tables():
    n = RET_DK // 2
    inv = ROPE_BASE ** (-np.linspace(0.0, 1.0, n))
    pos = np.arange(CTX_LEN + SEQ, dtype=np.float64)
    ang = pos[:, None] * inv[None, :]
    cos, sin = np.cos(ang), np.sin(ang)

    def rev(a):
        return a.reshape(-1, RET_C, n)[:, ::-1].reshape(-1, n)

    cos2 = np.stack([cos, rev(cos)]).astype(np.float32)
    sin2 = np.stack([sin, rev(sin)]).astype(np.float32)
    return cos2, sin2


@functools.lru_cache(maxsize=None)
def _fft_mats():
    N, R = FFT_N, FFT_R
    k1 = np.arange(R)
    n2 = np.arange(R)[:, None, None]
    n1 = np.arange(R)[None, None, :]
    n = n2 + R * n1
    idx = (k1[None, :, None] * n) % N
    ang = 2.0 * np.pi * idx / N
    c, s = np.cos(ang), np.sin(ang)
    m1 = np.concatenate([c, -s], axis=1).astype(np.float32)
    m1i = np.concatenate([c.transpose(0, 2, 1), -s.transpose(0, 2, 1)], axis=2)
    m1i = m1i[:, :R // 2].astype(np.float32)
    a2 = 2.0 * np.pi * ((np.arange(R)[:, None] * np.arange(R)[None, :]) % R) / R
    fr, fi = np.cos(a2), -np.sin(a2)
    f2 = np.block([[fr, -fi], [fi, fr]]).astype(np.float32)
    g2 = (np.block([[fr, fi], [-fi, fr]]) / N).astype(np.float32)
    return m1, m1i, f2, g2


@functools.lru_cache(maxsize=None)
def _ctx_fft_mats():
    N = 2 * CTX_LEN
    k = np.arange(N)[:, None]
    n = np.arange(N)[None, :]
    ang = 2.0 * np.pi * ((k * n) % N) / N
    c, s = np.cos(ang), np.sin(ang)
    fwd = np.concatenate([c, -s], axis=0).astype(np.float32)
    inv = (np.concatenate([c, -s], axis=1)[:CTX_LEN] / N).astype(np.float32)
    return fwd, inv


def _ada_kernel(c_ref, w_ref, b_ref, o_ref):
    c = c_ref[...]
    a = c * (1.0 / (1.0 + jnp.exp(-c)))
    o_ref[...] = jnp.dot(a, w_ref[...], precision=HIGHEST, preferred_element_type=F32) + b_ref[...]


def _ada(cvec, ada_w, ada_b):
    tn = 1536
    return pl.pallas_call(
        _ada_kernel,
        out_shape=jax.ShapeDtypeStruct((DEPTH, 8, N_MOD * D_MODEL), F32),
        grid=(DEPTH, N_MOD * D_MODEL // tn),
        in_specs=[
            pl.BlockSpec((8, D_MODEL), lambda i, j: (0, 0)),
            pl.BlockSpec((None, D_MODEL, tn), lambda i, j: (i, 0, j)),
            pl.BlockSpec((None, 1, tn), lambda i, j: (i, 0, j)),
        ],
        out_specs=pl.BlockSpec((None, 8, tn), lambda i, j: (i, 0, j)),
        compiler_params=_params(("parallel", "parallel"), 40),
        name="ada_mod",
    )(cvec, ada_w, ada_b.reshape(DEPTH, 1, N_MOD * D_MODEL))


def _mod_row(ref, row):
    if isinstance(row, int):
        return ref[row:row + 1, :]
    return ref[pl.ds(row, 1), :]


def _norm_mod(x, g, shift, scale):
    y = x * lax.rsqrt(jnp.mean(x * x, axis=-1, keepdims=True) + EPS)
    return (y * g) * (1.0 + scale) + shift


def _rope_tile(x, cos, sin_signed):
    lane = lax.broadcasted_iota(jnp.int32, x.shape, 1)
    first = (lane % 32) < 16
    partner = jnp.where(first, pltpu.roll(x, LANES - 16, axis=1), pltpu.roll(x, 16, axis=1))
    return x * cos + partner * sin_signed


def _inproj0_kernel(x_ref, g_ref, sh_ref, sc_ref, w_ref, cos_ref, sin_ref, u_ref, qkv_ref, *, row, rope):
    r = pl.program_id(0) if row is None else row
    g, sh, sc = g_ref[...], _mod_row(sh_ref, r), _mod_row(sc_ref, r)
    tn = 512
    n_rot = (ATT_WIDTH + 2 * LANES) // LANES
    sub = min(x_ref.shape[0], SUB_ROWS)
    for s in range(x_ref.shape[0] // sub):
        rows = slice(s * sub, (s + 1) * sub)
        h = _norm_mod(x_ref[rows, :], g, sh, sc).astype(BF16)
        for j in range(EV_U // tn):
            u_ref[rows, j * tn:(j + 1) * tn] = jnp.dot(h, w_ref[:, j * tn:(j + 1) * tn],
                                                       preferred_element_type=F32)
        for j in range(EV_QKV // tn):
            y = jnp.dot(h, w_ref[:, EV_U + j * tn:EV_U + (j + 1) * tn], preferred_element_type=F32)
            for t in range(tn // LANES):
                yt = y[:, t * LANES:(t + 1) * LANES]
                if rope and j * (tn // LANES) + t < n_rot:
                    yt = _rope_tile(yt, cos_ref[rows, :], sin_ref[rows, :])
                qkv_ref[rows, j * tn + t * LANES:j * tn + (t + 1) * LANES] = yt.astype(BF16)


def _inproj0(x, g, mod, w, layer, *, is_ctx, tm):
    G, R, _ = x.shape
    cos, sin = _rope2d_tables()
    cos, sin = jnp.asarray(cos), jnp.asarray(sin)
    if is_ctx:
        tab = pl.BlockSpec((tm, LANES), lambda b, i: (0, 0))
    else:
        tab = pl.BlockSpec((tm, LANES), lambda b, i: (i, 0))
    kern = functools.partial(_inproj0_kernel, row=2 if is_ctx else None, rope=not is_ctx)
    return pl.pallas_call(
        kern,
        out_shape=(jax.ShapeDtypeStruct((G, R, EV_U), F32), jax.ShapeDtypeStruct((G, R, EV_QKV), BF16)),
        grid=(G, R // tm),
        in_specs=[
            pl.BlockSpec((None, tm, D_MODEL), lambda b, i: (b, i, 0)),
            pl.BlockSpec((1, D_MODEL), lambda b, i: (0, 0)),
            pl.BlockSpec((None, 8, D_MODEL), lambda b, i: (layer, 0, 0)),
            pl.BlockSpec((None, 8, D_MODEL), lambda b, i: (layer, 0, 1)),
            pl.BlockSpec((D_MODEL, EV_COLS), lambda b, i: (0, 0), pipeline_mode=pl.Buffered(1)),
            tab, tab,
        ],
        out_specs=(pl.BlockSpec((None, tm, EV_U), lambda b, i: (b, i, 0)),
                   pl.BlockSpec((None, tm, EV_QKV), lambda b, i: (b, i, 0))),
        compiler_params=_params(("parallel", "parallel"), 48),
        name="inproj0_ctx" if is_ctx else "inproj0",
    )(x, g, mod, mod, w, cos, sin)


def _inproj1_kernel(x_ref, g_ref, sh_ref, sc_ref, w_ref, o_ref, *, row, tn):
    r = pl.program_id(0) if row is None else row
    g, sh, sc = g_ref[...], _mod_row(sh_ref, r), _mod_row(sc_ref, r)
    sub = min(x_ref.shape[0], SUB_ROWS)
    for s in range(x_ref.shape[0] // sub):
        rows = slice(s * sub, (s + 1) * sub)
        h = _norm_mod(x_ref[rows, :], g, sh, sc).astype(BF16)
        for j in range(w_ref.shape[1] // tn):
            cols = slice(j * tn, (j + 1) * tn)
            o_ref[rows, cols] = jnp.dot(h, w_ref[:, cols], preferred_element_type=F32).astype(o_ref.dtype)


def _inproj1(x, g, mod, w, layer, *, is_ctx, tm, tn):
    G, R, _ = x.shape
    N = w.shape[1]
    kern = functools.partial(_inproj1_kernel, row=2 if is_ctx else None, tn=tn)
    return pl.pallas_call(
        kern,
        out_shape=jax.ShapeDtypeStruct((G, R, N), BF16),
        grid=(G, R // tm),
        in_specs=[
            pl.BlockSpec((None, tm, D_MODEL), lambda b, i: (b, i, 0)),
            pl.BlockSpec((1, D_MODEL), lambda b, i: (0, 0)),
            pl.BlockSpec((None, 8, D_MODEL), lambda b, i: (layer, 0, 0)),
            pl.BlockSpec((None, 8, D_MODEL), lambda b, i: (layer, 0, 1)),
            pl.BlockSpec((D_MODEL, N), lambda b, i: (0, 0), pipeline_mode=pl.Buffered(1)),
        ],
        out_specs=pl.BlockSpec((None, tm, N), lambda b, i: (b, i, 0)),
        compiler_params=_params(("parallel", "parallel"), 48),
        name="inproj1_ctx" if is_ctx else "inproj1",
    )(x, g, mod, mod, w)


def _hyena_pre_kernel(u_ref, up_ref, un_ref, w_ref, b_ref, bias_ref, z_ref, x0_ref, zb_ref, *, n_tiles):
    i = pl.program_id(1)
    u = u_ref[...]
    tm = u.shape[0]
    prev_row = jnp.where(i > 0, up_ref[7:8, :], 0.0)
    next_row = jnp.where(i < n_tiles - 1, un_ref[0:1, :], 0.0)
    row = lax.broadcasted_iota(jnp.int32, u.shape, 0)
    before = jnp.where(row == 0, prev_row, pltpu.roll(u, 1, axis=0))
    after = jnp.where(row == tm - 1, next_row, pltpu.roll(u, tm - 1, axis=0))
    conv = before * w_ref[0:1, :] + u * w_ref[1:2, :] + after * w_ref[2:3, :] + b_ref[...]
    x0 = conv[:, :HY_WIDTH]
    z = conv[:, 2 * HY_WIDTH:] * conv[:, HY_WIDTH:2 * HY_WIDTH]
    z_ref[...] = z
    x0_ref[...] = x0
    zb_ref[...] = x0 * (z * bias_ref[...])


def _hyena_pre(u, conv_w, conv_b, bias, *, tm):
    G, R, _ = u.shape
    n_tiles = R // tm
    sub = tm // 8
    last8 = R // 8 - 1
    kern = functools.partial(_hyena_pre_kernel, n_tiles=n_tiles)
    out = jax.ShapeDtypeStruct((G, R, HY_WIDTH), F32)
    ospec = pl.BlockSpec((None, tm, HY_WIDTH), lambda b, i: (b, i, 0))
    return pl.pallas_call(
        kern,
        out_shape=(out, out, out),
        grid=(G, n_tiles),
        in_specs=[
            pl.BlockSpec((None, tm, EV_U), lambda b, i: (b, i, 0)),
            pl.BlockSpec((None, 8, EV_U), lambda b, i: (b, jnp.maximum(i * sub - 1, 0), 0)),
            pl.BlockSpec((None, 8, EV_U), lambda b, i: (b, jnp.minimum((i + 1) * sub, last8), 0)),
            pl.BlockSpec((3, EV_U), lambda b, i: (0, 0)),
            pl.BlockSpec((1, EV_U), lambda b, i: (0, 0)),
            pl.BlockSpec((1, HY_WIDTH), lambda b, i: (0, 0)),
        ],
        out_specs=(ospec, ospec, ospec),
        compiler_params=_params(("parallel", "parallel"), 48),
        name="hyena_pre",
    )(u, u, u, conv_w, conv_b, bias)


def _split_bf16(a):
    hi = a.astype(BF16)
    return hi, (a - hi.astype(F32)).astype(BF16)


def _dot3(a, b):
    a_hi, a_lo = _split_bf16(a)
    b_hi, b_lo = _split_bf16(b)
    dot = functools.partial(jnp.dot, preferred_element_type=F32)
    return dot(a_hi, b_hi) + (dot(a_lo, b_hi) + dot(a_hi, b_lo))


def _filter_kernel(w1t_ref, b1_ref, w2t_ref, b2_ref, fr_ref, w3_ref, dec_ref, bands_ref, h_ref, ss_ref, *, L, tm):
    i = pl.program_id(0)

    def time_of(r):
        return jnp.where(r < L, r, 2 * L - r).astype(F32) / L

    t_row = time_of(lax.broadcasted_iota(jnp.int32, (1, tm), 1) + i * tm)
    ang = (2.0 * math.pi * t_row) * bands_ref[...]
    t8 = jnp.where(lax.broadcasted_iota(jnp.int32, (8, tm), 0) == 0, t_row, 0.0)
    feat = jnp.concatenate([t8, jnp.cos(ang), -jnp.sin(ang)], axis=0)
    fr = fr_ref[...]
    hid = jnp.sin(fr * (jnp.dot(w1t_ref[...], feat, precision=HIGHEST, preferred_element_type=F32) + b1_ref[...]))
    hid = jnp.sin(fr * (jnp.dot(w2t_ref[...], hid, precision=HIGHEST, preferred_element_type=F32) + b2_ref[...]))
    h = _dot3(hid.T, w3_ref[...])
    r_col = lax.broadcasted_iota(jnp.int32, (tm, 1), 0) + i * tm
    h = h * jnp.exp(-time_of(r_col) * jnp.abs(dec_ref[...]))
    h = jnp.where(r_col == L, 0.0, h)
    h_ref[...] = h

    @pl.when(i == 0)
    def _():
        ss_ref[...] = jnp.zeros_like(ss_ref)

    ss_ref[...] += jnp.broadcast_to(jnp.sum(h * h, axis=0, keepdims=True), ss_ref.shape)


def _hyena_filter(w1, b1, w2, b2, w3, freq, decay, *, L):
    tm = min(L, 1024)
    tiles_per_side = L // tm
    kern = functools.partial(_filter_kernel, L=L, tm=tm)
    w1t = jnp.concatenate([w1[0:1], jnp.zeros((7, HY_HIDDEN), F32), w1[1:]], axis=0).T
    col = lambda a: a.reshape(HY_HIDDEN, 1)
    w3s = jnp.stack([w3[:, :HY_WIDTH], w3[:, HY_WIDTH:]])
    bands = jnp.asarray(np.linspace(1e-4, HY_BANDS - 1, HY_BANDS).astype(np.float32).reshape(HY_BANDS, 1))
    small = (w1t, col(b1), w2.T, col(b2), col(freq))
    full = lambda a: pl.BlockSpec(a.shape, lambda i: (0,) * a.ndim)
    side = lambda *blk: pl.BlockSpec((None,) + blk, lambda i: (i // tiles_per_side, 0, 0))
    return pl.pallas_call(
        kern,
        out_shape=(jax.ShapeDtypeStruct((2 * L, HY_WIDTH), F32), jax.ShapeDtypeStruct((8, HY_WIDTH), F32)),
        grid=(2 * L // tm,),
        in_specs=[full(a) for a in small] + [side(HY_HIDDEN, HY_WIDTH), side(1, HY_WIDTH), full(bands)],
        out_specs=(pl.BlockSpec((tm, HY_WIDTH), lambda i: (i, 0)), pl.BlockSpec((8, HY_WIDTH), lambda i: (0, 0))),
        compiler_params=_params(("arbitrary",), 48),
        name="hyena_filter_%d" % L,
    )(*small, w3s, decay.reshape(2, 1, HY_WIDTH), bands)


def _stack_bf16(re, im):
    return jnp.concatenate([re, im], axis=0).astype(BF16)


def _filter_fft_kernel(h_ref, ss_ref, m1_ref, f2_ref, o_ref, a_scr):
    R, P = FFT_R, FFT_PITCH
    inv_norm = lax.rsqrt(ss_ref[0:1, :])

    def stage1(n2, carry):
        xs = (h_ref[pl.ds(n2, R, stride=R), :] * inv_norm).astype(BF16)
        a = jnp.dot(m1_ref[n2], xs, preferred_element_type=F32)
        base = pl.multiple_of(n2 * P, 8)
        a_scr[0, pl.ds(base, R), :] = a[:R]
        a_scr[1, pl.ds(base, R), :] = a[R:]
        return carry

    lax.fori_loop(0, R, stage1, 0, unroll=FFT_UNROLL)

    def stage2(k1, carry):
        ar = a_scr[0, pl.ds(k1, R, stride=P), :]
        ai = a_scr[1, pl.ds(k1, R, stride=P), :]
        x = jnp.dot(f2_ref[...], _stack_bf16(ar, ai), preferred_element_type=F32)
        base = pl.multiple_of(k1 * R, R)
        o_ref[0, pl.ds(base, R), :] = x[:R].astype(BF16)
        o_ref[1, pl.ds(base, R), :] = x[R:].astype(BF16)
        return carry

    lax.fori_loop(0, R, stage2, 0, unroll=FFT_UNROLL)


def _filter_fft(h_circ, ss, m1, f2):
    ct = LANES
    once = pl.Buffered(1)
    return pl.pallas_call(
        _filter_fft_kernel,
        out_shape=jax.ShapeDtypeStruct((2, FFT_N, HY_WIDTH), BF16),
        grid=(HY_WIDTH // ct,),
        in_specs=[
            pl.BlockSpec((FFT_N, ct), lambda c: (0, c), pipeline_mode=once),
            pl.BlockSpec((8, ct), lambda c: (0, c)),
            pl.BlockSpec(m1.shape, lambda c: (0, 0, 0), pipeline_mode=once),
            pl.BlockSpec(f2.shape, lambda c: (0, 0), pipeline_mode=once),
        ],
        out_specs=pl.BlockSpec((2, FFT_N, ct), lambda c: (0, 0, c)),
        scratch_shapes=[pltpu.VMEM((2, FFT_R * FFT_PITCH, ct), F32)],
        compiler_params=_params(("parallel",), 58),
        name="hyena_filter_fft",
    )(h_circ, ss, m1, f2)


def _fft_conv_kernel(z_ref, hs_ref, m1_ref, f2_ref, g2_ref, m1i_ref, y_ref, a_scr):
    R, P = FFT_R, FFT_PITCH
    ct = a_scr.shape[-1]

    def stage1(n2, carry):
        rows = pl.ds(n2, R // 2, stride=R)
        xs = jnp.concatenate([z_ref[0, rows, :], z_ref[1, rows, :]], axis=1).astype(BF16)
        t = jnp.dot(m1_ref[n2, :, 0:R // 2], xs, preferred_element_type=F32)
        base = pl.multiple_of(n2 * P, 8)
        a_scr[0, pl.ds(base, R), :] = t[:R, :ct] - t[R:, ct:]
        a_scr[1, pl.ds(base, R), :] = t[R:, :ct] + t[:R, ct:]
        return carry

    lax.fori_loop(0, R, stage1, 0, unroll=FFT_UNROLL)

    def stage2(k1, carry):
        ar = a_scr[0, pl.ds(k1, R, stride=P), :]
        ai = a_scr[1, pl.ds(k1, R, stride=P), :]
        x = jnp.dot(f2_ref[...], _stack_bf16(ar, ai), preferred_element_type=F32)
        base = pl.multiple_of(k1 * R, R)
        hr = hs_ref[0, pl.ds(base, R), :].astype(F32)
        hi = hs_ref[1, pl.ds(base, R), :].astype(F32)
        xr, xi = x[:R], x[R:]
        yr = xr * hr - xi * hi
        yi = xr * hi + xi * hr
        bq = jnp.dot(g2_ref[...], _stack_bf16(yr, yi), preferred_element_type=F32)
        a_scr[0, pl.ds(k1, R, stride=P), :] = bq[:R]
        a_scr[1, pl.ds(k1, R, stride=P), :] = bq[R:]
        return carry

    lax.fori_loop(0, R, stage2, 0, unroll=FFT_UNROLL)

    def stage3(n2, carry):
        base = pl.multiple_of(n2 * P, 8)
        br = a_scr[0, pl.ds(base, R), :]
        bi = a_scr[1, pl.ds(base, R), :]
        rhs = jnp.concatenate([jnp.concatenate([br, bi], axis=1),
                               jnp.concatenate([bi, -br], axis=1)], axis=0).astype(BF16)
        y = jnp.dot(m1i_ref[n2], rhs, preferred_element_type=F32)
        rows = pl.ds(n2, R // 2, stride=R)
        y_ref[0, rows, :] = y[:, :ct]
        y_ref[1, rows, :] = y[:, ct:]
        return carry

    lax.fori_loop(0, R, stage3, 0, unroll=FFT_UNROLL)


def _fft_conv(z, hspec, m1, f2, g2, m1i):
    assert z.shape[0] == 2, "the batch pair is packed into one complex signal"
    ct = LANES
    once = pl.Buffered(1)
    return pl.pallas_call(
        _fft_conv_kernel,
        out_shape=jax.ShapeDtypeStruct((2, SEQ, HY_WIDTH), F32),
        grid=(HY_WIDTH // ct,),
        in_specs=[
            pl.BlockSpec((2, SEQ, ct), lambda c: (0, 0, c), pipeline_mode=once),
            pl.BlockSpec((2, FFT_N, ct), lambda c: (0, 0, c), pipeline_mode=once),
            pl.BlockSpec(m1.shape, lambda c: (0, 0, 0), pipeline_mode=once),
            pl.BlockSpec(f2.shape, lambda c: (0, 0), pipeline_mode=once),
            pl.BlockSpec(g2.shape, lambda c: (0, 0), pipeline_mode=once),
            pl.BlockSpec(m1i.shape, lambda c: (0, 0, 0), pipeline_mode=once),
        ],
        out_specs=pl.BlockSpec((2, SEQ, ct), lambda c: (0, 0, c), pipeline_mode=once),
        scratch_shapes=[pltpu.VMEM((2, FFT_R * FFT_PITCH, ct), F32)],
        compiler_params=_params(("arbitrary",), 58),
        name="hyena_fft_conv",
    )(z, hspec, m1, f2, g2, m1i)


def _ctx_conv_kernel(z_ref, h_ref, ss_ref, fwd_ref, inv_ref, y_ref):
    N = 2 * CTX_LEN
    hn = (h_ref[...] * lax.rsqrt(ss_ref[0:1, :])).astype(BF16)
    hs = jnp.dot(fwd_ref[...], hn, preferred_element_type=F32)
    zs = jnp.dot(fwd_ref[:, :CTX_LEN], z_ref[...].astype(BF16), preferred_element_type=F32)
    hr, hi, zr, zi = hs[:N], hs[N:], zs[:N], zs[N:]
    ys = _stack_bf16(zr * hr - zi * hi, zr * hi + zi * hr)
    y_ref[...] = jnp.dot(inv_ref[...], ys, preferred_element_type=F32)


def _ctx_conv(z, h_circ, ss, fwd, inv):
    G = z.shape[0]
    full = lambda a: pl.BlockSpec(a.shape, lambda b: (0,) * a.ndim)
    return pl.pallas_call(
        _ctx_conv_kernel,
        out_shape=jax.ShapeDtypeStruct((G, CTX_LEN, HY_WIDTH), F32),
        grid=(G,),
        in_specs=[pl.BlockSpec((None, CTX_LEN, HY_WIDTH), lambda b: (b, 0, 0)),
                  full(h_circ), full(ss), full(fwd), full(inv)],
        out_specs=pl.BlockSpec((None, CTX_LEN, HY_WIDTH), lambda b: (b, 0, 0)),
        compiler_params=_params(("parallel",), 48),
        name="hyena_ctx_conv",
    )(z, h_circ, ss, fwd, inv)


def _kv_group(refs, g):
    cols = slice(g * LANES, (g + 1) * LANES)
    return refs[0][:, cols] if len(refs) == 1 else jnp.concatenate([r[:, cols] for r in refs], axis=0)


def _attn_logits(q_ref, k_refs, g, masks):
    low = lax.broadcasted_iota(jnp.int32, (BLOCK, LANES), 1) < HEAD_DIM
    zero = jnp.zeros((), BF16)
    parts = []
    for hh in range(ATT_GROUP):
        h = g * ATT_GROUP + hh
        tile = q_ref[:, (h // 2) * LANES:(h // 2 + 1) * LANES]
        parts.append(jnp.where(low if h % 2 == 0 else ~low, tile, zero))
    s = lax.dot_general(jnp.concatenate(parts, axis=0), _kv_group(k_refs, g), (((1,), (1,)), ((), ())),
                        preferred_element_type=F32)
    if masks is not None:
        ok_prev, ok_next = masks
        s = jnp.concatenate([jnp.where(ok_prev, s[:, :BLOCK], NEG_INF), s[:, BLOCK:2 * BLOCK],
                             jnp.where(ok_next, s[:, 2 * BLOCK:3 * BLOCK], NEG_INF), s[:, 3 * BLOCK:]], axis=1)
    return s


def _attn_output(s, sink_ref, v_refs, g, o_ref):
    low = lax.broadcasted_iota(jnp.int32, (BLOCK, LANES), 1) < HEAD_DIM
    low4 = jnp.concatenate([low] * ATT_GROUP, axis=0)
    v = _kv_group(v_refs, g)
    v_aug = jnp.where(lax.broadcasted_iota(jnp.int32, v.shape, 1) < HEAD_DIM, v, jnp.ones((), BF16))
    sink = jnp.concatenate(
        [jnp.full((BLOCK, 1), sink_ref[g * ATT_GROUP + hh] * LOG2E, F32) for hh in range(ATT_GROUP)], axis=0)
    m = jnp.maximum(jnp.max(s, axis=-1, keepdims=True), sink)
    e = jnp.exp2(s - m).astype(BF16)
    o = jnp.dot(e, v_aug, preferred_element_type=F32) + jnp.where(low4, 0.0, jnp.exp2(sink - m))
    swapped = pltpu.roll(o, HEAD_DIM, axis=1)
    for pair in range(ATT_GROUP // 2):
        ev = slice((2 * pair) * BLOCK, (2 * pair + 1) * BLOCK)
        od = slice((2 * pair + 1) * BLOCK, (2 * pair + 2) * BLOCK)
        even = o[ev] / swapped[ev]
        odd = swapped[od] / o[od]
        t = g * (ATT_GROUP // 2) + pair
        o_ref[:, t * LANES:(t + 1) * LANES] = jnp.where(low, even, odd).astype(o_ref.dtype)


def _attn_ctx_kernel(sink_ref, q_ref, kx_ref, vx_ref, o_ref):
    for g in range(ATT_KV_HEADS):
        _attn_output(_attn_logits(q_ref, (kx_ref,), g, None), sink_ref, (vx_ref,), g, o_ref)


def _attn_local_kernel(sink_ref, q_ref, kp_ref, kc_ref, kn_ref, kx_ref, vp_ref, vc_ref, vn_ref, vx_ref,
                       o_ref, s_even, s_odd, *, n_blocks):
    n = pl.program_id(1)

    @pl.when(n == 0)
    def _():
        s_odd[...] = jnp.zeros_like(s_odd)

    def step(s_new, s_prev):
        nq = jnp.minimum(n, n_blocks - 1)
        qi = lax.broadcasted_iota(jnp.int32, (ATT_GROUP * BLOCK, BLOCK), 0) % BLOCK
        ki = lax.broadcasted_iota(jnp.int32, (ATT_GROUP * BLOCK, BLOCK), 1)
        masks = ((ki >= qi) & (nq > 0), (ki <= qi) & (nq < n_blocks - 1))
        for g in range(ATT_KV_HEADS):
            s_new[g] = _attn_logits(q_ref, (kp_ref, kc_ref, kn_ref, kx_ref), g, masks)
        for g in range(ATT_KV_HEADS):
            _attn_output(s_prev[g], sink_ref, (vp_ref, vc_ref, vn_ref, vx_ref), g, o_ref)

    @pl.when(n % 2 == 0)
    def _():
        step(s_even, s_odd)

    @pl.when(n % 2 == 1)
    def _():
        step(s_odd, s_even)


def _attention(sink, qkv, qkv_ctx, *, local):
    src = qkv if local else qkv_ctx
    G, R, _ = src.shape
    nb = R // BLOCK
    kcol, vcol = ATT_WIDTH // 256, ATT_WIDTH // 256 + 1
    ctx_k = pl.BlockSpec((None, CTX_LEN, 256), lambda b, n: (b, 0, kcol))
    ctx_v = pl.BlockSpec((None, CTX_LEN, 256), lambda b, n: (b, 0, vcol))
    smem = pl.BlockSpec(memory_space=pltpu.SMEM)
    out_shape = jax.ShapeDtypeStruct((G, R, ATT_WIDTH), BF16)
    if not local:
        rows = pl.BlockSpec((None, BLOCK, ATT_WIDTH), lambda b, n: (b, n, 0))
        return pl.pallas_call(
            _attn_ctx_kernel, out_shape=out_shape, grid=(G, nb),
            in_specs=[smem, rows, ctx_k, ctx_v], out_specs=rows,
            compiler_params=_params(("parallel", "parallel"), 48), name="attn_ctx",
        )(sink, qkv_ctx, qkv_ctx, qkv_ctx)

    q_blk = lambda n: jnp.minimum(n, nb - 1)
    o_blk = lambda n: jnp.maximum(n - 1, 0)

    def nbr(col, blk, off):
        return pl.BlockSpec((None, BLOCK, 256), lambda b, n: (b, jnp.clip(blk(n) + off, 0, nb - 1), col))

    in_specs = [smem, pl.BlockSpec((None, BLOCK, ATT_WIDTH), lambda b, n: (b, q_blk(n), 0)),
                nbr(kcol, q_blk, -1), nbr(kcol, q_blk, 0), nbr(kcol, q_blk, 1), ctx_k,
                nbr(vcol, o_blk, -1), nbr(vcol, o_blk, 0), nbr(vcol, o_blk, 1), ctx_v]
    logits_scratch = pltpu.VMEM((ATT_KV_HEADS, ATT_GROUP * BLOCK, 3 * BLOCK + CTX_LEN), F32)
    return pl.pallas_call(
        functools.partial(_attn_local_kernel, n_blocks=nb),
        out_shape=out_shape,
        grid=(G, nb + 1),
        in_specs=in_specs,
        out_specs=pl.BlockSpec((None, BLOCK, ATT_WIDTH), lambda b, n: (b, o_blk(n), 0)),
        scratch_shapes=[logits_scratch, logits_scratch],
        compiler_params=_params(("parallel", "arbitrary"), 48),
        name="attn_local",
    )(sink, qkv, qkv, qkv, qkv, qkv_ctx, qkv, qkv, qkv, qkv_ctx)


def _mixer_out0(rows, x0_ref, y_ref, zb_ref, att_ref, wo_ref):
    hy = (x0_ref[rows, :] * y_ref[rows, :] + zb_ref[rows, :]).astype(BF16)
    return (jnp.dot(hy, wo_ref[:HY_WIDTH, :], preferred_element_type=F32)
            + jnp.dot(att_ref[rows, :], wo_ref[HY_WIDTH:, :], preferred_element_type=F32))


def _mixer_out1(rows, of_ref, ob_ref, wo_ref):
    a = (of_ref[rows, :].astype(F32) + ob_ref[rows, :].astype(F32)).astype(BF16)
    return jnp.dot(a, wo_ref[...], preferred_element_type=F32)


def _mix_kernel(*refs, mixer_out, n_mix, row, fc, final_norm):
    x_ref = refs[0]
    mix_refs = refs[1:2 + n_mix]
    gta_ref, gm_ref, shm_ref, scm_ref, gtm_ref, w1_ref, w2_ref, fg_ref, o_ref, a_scr = refs[2 + n_mix:]
    r = pl.program_id(0) if row is None else row
    gta, gtm = _mod_row(gta_ref, r), _mod_row(gtm_ref, r)
    gm, shm, scm = gm_ref[...], _mod_row(shm_ref, r), _mod_row(scm_ref, r)
    sub = min(x_ref.shape[0], SUB_ROWS)
    for s in range(x_ref.shape[0] // sub):
        rows = slice(s * sub, (s + 1) * sub)
        x1 = x_ref[rows, :] + gta * mixer_out(rows, *mix_refs)
        h = _norm_mod(x1, gm, shm, scm).astype(BF16)
        for c in range(D_FF // fc):
            a = jnp.maximum(jnp.dot(h, w1_ref[:, c * fc:(c + 1) * fc], preferred_element_type=F32), 0.0)
            a_scr[rows, c * fc:(c + 1) * fc] = (a * a).astype(BF16)
        out = x1 + gtm * jnp.dot(a_scr[rows, :], w2_ref[...], preferred_element_type=F32)
        if final_norm:
            out = (out * lax.rsqrt(jnp.mean(out * out, axis=-1, keepdims=True) + EPS)) * fg_ref[...]
        o_ref[rows, :] = out


def _mix_mlp(kind, x, mix_in, wo, mod, gm, w1, w2, fg, layer, *, is_ctx, tm, fc, final_norm):
    G, R, _ = x.shape
    row_spec = lambda w: pl.BlockSpec((None, tm, w), lambda b, i: (b, i, 0))
    modk = lambda k: pl.BlockSpec((None, 8, D_MODEL), lambda b, i: (layer, 0, k))
    vec = pl.BlockSpec((1, D_MODEL), lambda b, i: (0, 0))
    resident = lambda a: pl.BlockSpec(a.shape, lambda b, i: (0, 0), pipeline_mode=pl.Buffered(1))
    if kind == 0:
        mixer_out = _mixer_out0
        mix_specs = [row_spec(HY_WIDTH)] * 3 + [row_spec(ATT_WIDTH)]
    else:
        mixer_out = _mixer_out1
        mix_specs = [row_spec(RET_V)] * 2
    kern = functools.partial(_mix_kernel, mixer_out=mixer_out, n_mix=len(mix_in), row=2 if is_ctx else None,
                             fc=fc, final_norm=final_norm)
    return pl.pallas_call(
        kern,
        out_shape=jax.ShapeDtypeStruct((G, R, D_MODEL), F32),
        grid=(G, R // tm),
        in_specs=[row_spec(D_MODEL)] + mix_specs + [
            resident(wo), modk(2), vec, modk(3), modk(4), modk(5), resident(w1), resident(w2), vec,
        ],
        out_specs=row_spec(D_MODEL),
        scratch_shapes=[pltpu.VMEM((tm, D_FF), BF16)],
        compiler_params=_params(("parallel", "parallel"), 56),
        name="mix_mlp%d%s" % (kind, "_ctx" if is_ctx else ""),
    )(x, *mix_in, wo, mod, gm, mod, mod, mod, w1, w2, fg)


def _ret_kernel(lr_ref, qkv_c, qkv_f, g_f, qkv_b, g_b, cos_ref, sin_ref, of_ref, ob_ref,
                state, dmask, xi, zeta, gch):
    j = pl.program_id(1)
    C = RET_C
    kscale = RET_DK ** -0.5

    @pl.when(j == 0)
    def _():
        state[...] = jnp.zeros_like(state)
        row = lax.broadcasted_iota(jnp.int32, (C, C), 0).astype(F32)
        col = lax.broadcasted_iota(jnp.int32, (C, C), 1).astype(F32)
        for d in range(2):
            e = row - col if d == 0 else col - row
            p = row if d == 0 else (C - 1) - row
            for h in range(RET_HEADS):
                lg = -jnp.exp(jnp.full((C, C), lr_ref[d, h], F32))
                dmask[d, h] = jnp.where(e >= 0, jnp.exp(e * lg), 0.0) * kscale
                xi[d, h] = jnp.exp((p + 1.0) * lg)
                zeta[d, h] = jnp.exp(((C - 1) - p) * lg) * kscale
                gch[d, h] = jnp.exp(C * -jnp.exp(jnp.full((8, C), lr_ref[d, h], F32)))

    half = RET_DK // 2

    def chain(d, h, src_ref, g_ref, o_ref):
        cos, sin = cos_ref[d], sin_ref[d]

        def rot(c0):
            t1 = src_ref[:, c0:c0 + half].astype(F32)
            t2 = src_ref[:, c0 + half:c0 + RET_DK].astype(F32)
            return jnp.concatenate([t1 * cos - t2 * sin, t2 * cos + t1 * sin], axis=1)

        q = rot(h * RET_DK)
        k = rot(RET_QK + h * RET_DK)
        v = src_ref[:, 2 * RET_QK + h * RET_DV:2 * RET_QK + (h + 1) * RET_DV]
        inner = lax.dot_general(q.astype(BF16), k.astype(BF16), (((1,), (1,)), ((), ())),
                                preferred_element_type=F32) * dmask[d, h]
        s_old = state[d, h]
        kv = lax.dot_general((k * zeta[d, h]).astype(BF16), v, (((0,), (0,)), ((), ())),
                             preferred_element_type=F32)
        state[d, h] = gch[d, h, 0:1, 0:1] * s_old + kv
        if o_ref is not None:
            o = (jnp.dot(inner.astype(BF16), v, preferred_element_type=F32)
                 + jnp.dot((q * xi[d, h]).astype(BF16), s_old.astype(BF16), preferred_element_type=F32))
            on = o * lax.rsqrt(jnp.mean(o * o, axis=-1, keepdims=True) + EPS)
            gate = g_ref[:, h * RET_DV:(h + 1) * RET_DV].astype(F32)
            o_ref[:, h * RET_DV:(h + 1) * RET_DV] = (gate * (1.0 / (1.0 + jnp.exp(-gate))) * on).astype(o_ref.dtype)

    @pl.when(j == 0)
    def _():
        for d in range(2):
            for h in range(RET_HEADS):
                chain(d, h, qkv_c, None, None)

    @pl.when(j > 0)
    def _():
        for h in range(RET_HEADS):
            chain(0, h, qkv_f, g_f, of_ref)
            chain(1, h, qkv_b, g_b, ob_ref)


def _retention(log_rate, p_ctx, p_lat):
    G = p_lat.shape[0]
    C = RET_C
    n_lat = SEQ // C
    cos, sin = _rope1d_tables()
    cos, sin = jnp.asarray(cos), jnp.asarray(sin)

    def fwd_chunk(j):
        return jnp.maximum(j - 1, 0)

    def bwd_chunk(j):
        return n_lat - 1 - jnp.maximum(j - 1, 0)

    qkv_w = 2 * RET_QK + RET_V
    gcol = qkv_w // RET_V
    tab = pl.BlockSpec((2, C, RET_DK // 2), lambda b, j: (0, j, 0))
    in_specs = [
        pl.BlockSpec(memory_space=pltpu.SMEM),
        pl.BlockSpec((None, C, qkv_w), lambda b, j: (b, 0, 0)),
        pl.BlockSpec((None, C, qkv_w), lambda b, j: (b, fwd_chunk(j), 0)),
        pl.BlockSpec((None, C, RET_V), lambda b, j: (b, fwd_chunk(j), gcol)),
        pl.BlockSpec((None, C, qkv_w), lambda b, j: (b, bwd_chunk(j), 0)),
        pl.BlockSpec((None, C, RET_V), lambda b, j: (b, bwd_chunk(j), gcol + 1)),
        tab, tab,
    ]
    out = jax.ShapeDtypeStruct((G, SEQ, RET_V), BF16)
    per_chain = lambda *tail: pltpu.VMEM((2, RET_HEADS) + tail, F32)
    return pl.pallas_call(
        _ret_kernel,
        out_shape=(out, out),
        grid=(G, 1 + n_lat),
        in_specs=in_specs,
        out_specs=(pl.BlockSpec((None, C, RET_V), lambda b, j: (b, fwd_chunk(j), 0)),
                   pl.BlockSpec((None, C, RET_V), lambda b, j: (b, bwd_chunk(j), 0))),
        scratch_shapes=[per_chain(RET_DK, RET_DV), per_chain(C, C), per_chain(C, C), per_chain(C, C),
                        per_chain(8, C)],
        compiler_params=_params(("parallel", "arbitrary"), 48),
        name="retention",
    )(log_rate, p_ctx, p_lat, p_lat, p_lat, p_lat, cos, sin)


def _ev_weight(w_in):
    i_q = EV_U
    i_k = i_q + ATT_WIDTH
    i_v = i_k + KV_WIDTH
    cols = [w_in[:, :i_q], w_in[:, i_q:i_k] * (HEAD_DIM ** -0.5 * LOG2E)]
    for base in (i_k, i_v):
        for g in range(ATT_KV_HEADS):
            part = w_in[:, base + g * HEAD_DIM: base + (g + 1) * HEAD_DIM]
            cols += [part, part]
    return jnp.concatenate(cols, axis=1).astype(BF16)


def kernel(x, c, ctx, c_ctx, ada_w, ada_b, norm_mix_g, norm_mlp_g, mlp_w1, mlp_w2, ev_w_in, ev_w_out, hy_conv_w, hy_conv_b, hy_w1, hy_b1, hy_w2, hy_b2, hy_w3, hy_freq, hy_decay, hy_bias, attn_sink, od_w_in, od_w_out, ret_log_rate, final_g):
    D = D_MODEL
    cvec = jnp.concatenate([c, c_ctx[None, :], jnp.zeros((8 - BATCH - 1, D), F32)], axis=0)
    mod = _ada(cvec, ada_w, ada_b)

    m1, m1i, f2, g2 = (jnp.asarray(a).astype(BF16) for a in _fft_mats())
    cfwd, cinv = (jnp.asarray(a).astype(BF16) for a in _ctx_fft_mats())
    fg = final_g.reshape(1, D)

    gmix = norm_mix_g[0].reshape(1, D)
    gmlp = norm_mlp_g[0].reshape(1, D)
    w_in = _ev_weight(ev_w_in[0])
    u_l, qkv_l = _inproj0(x, gmix, mod, w_in, 0, is_ctx=False, tm=512)
    u_c, qkv_c = _inproj0(ctx, gmix, mod, w_in, 0, is_ctx=True, tm=CTX_LEN)

    conv_b = hy_conv_b[0].reshape(1, EV_U)
    bias = hy_bias[0].reshape(1, HY_WIDTH)
    z_l, x0_l, zb_l = _hyena_pre(u_l, hy_conv_w[0], conv_b, bias, tm=512)
    z_c, x0_c, zb_c = _hyena_pre(u_c, hy_conv_w[0], conv_b, bias, tm=CTX_LEN)

    filt = (hy_w1[0], hy_b1[0], hy_w2[0], hy_b2[0], hy_w3[0], hy_freq[0], hy_decay[0])
    h_l, ss_l = _hyena_filter(*filt, L=SEQ)
    h_c, ss_c = _hyena_filter(*filt, L=CTX_LEN)
    hspec = _filter_fft(h_l, ss_l, m1, f2)
    y_l = _fft_conv(z_l, hspec, m1, f2, g2, m1i)
    y_c = _ctx_conv(z_c, h_c, ss_c, cfwd, cinv)

    sink = attn_sink[0]
    att_l = _attention(sink, qkv_l, qkv_c, local=True)
    att_c = _attention(sink, qkv_l, qkv_c, local=False)

    wo = ev_w_out[0].astype(BF16)
    w1 = mlp_w1[0].astype(BF16)
    w2 = mlp_w2[0].astype(BF16)
    x1 = _mix_mlp(0, x, (x0_l, y_l, zb_l, att_l), wo, mod, gmlp, w1, w2, fg, 0,
                  is_ctx=False, tm=512, fc=512, final_norm=False)
    ctx1 = _mix_mlp(0, ctx, (x0_c, y_c, zb_c, att_c), wo, mod, gmlp, w1, w2, fg, 0,
                    is_ctx=True, tm=CTX_LEN, fc=512, final_norm=False)

    gmix = norm_mix_g[1].reshape(1, D)
    gmlp = norm_mlp_g[1].reshape(1, D)
    w_in = od_w_in[0].astype(BF16)
    p_l = _inproj1(x1, gmix, mod, w_in, 1, is_ctx=False, tm=512, tn=512)
    p_c = _inproj1(ctx1, gmix, mod, w_in, 1, is_ctx=True, tm=CTX_LEN, tn=512)
    o_fwd, o_bwd = _retention(ret_log_rate[0], p_c, p_l)

    wo = od_w_out[0].astype(BF16)
    w1 = mlp_w1[1].astype(BF16)
    w2 = mlp_w2[1].astype(BF16)
    return _mix_mlp(1, x1, (o_fwd, o_bwd), wo, mod, gmlp, w1, w2, fg, 1,
                    is_ctx=False, tm=512, fc=512, final_norm=True)
```

```python
import functools
import math

import numpy as np
import jax
import jax.numpy as jnp
from jax import lax
from jax.experimental import pallas as pl
from jax.experimental.pallas import tpu as pltpu

F32 = jnp.float32
BF16 = jnp.bfloat16
HIGHEST = lax.Precision.HIGHEST

D_MODEL = 1024
BATCH = 2
SEQ = 8192
DEPTH = 2
GRID_W = 64
CTX_LEN = 256
EPS = 1e-6
NEG_INF = -1e30
N_MOD = 6
D_FF = 4 * D_MODEL
ROPE_BASE = 10000.0

HY_WIDTH = D_MODEL // 2
HY_EMB = 33
HY_BANDS = (HY_EMB - 1) // 2
HY_HIDDEN = 64

ATT_HEADS = 8
ATT_KV_HEADS = 2
ATT_GROUP = ATT_HEADS // ATT_KV_HEADS
HEAD_DIM = 64
ATT_WIDTH = ATT_HEADS * HEAD_DIM
KV_WIDTH = ATT_KV_HEADS * HEAD_DIM
BLOCK = 128

RET_HEADS = 4
RET_DK = D_MODEL // RET_HEADS
RET_DV = 2 * RET_DK
RET_QK = RET_HEADS * RET_DK
RET_V = RET_HEADS * RET_DV
OD_IN = 2 * RET_QK + 3 * RET_V

LOG2E = 1.4426950408889634
LANES = 128
MIB = 1024 * 1024

EV_U = 3 * HY_WIDTH
EV_QKV = ATT_WIDTH + 4 * LANES
EV_COLS = EV_U + EV_QKV

FFT_N = 2 * SEQ
FFT_R = 128
FFT_PITCH = FFT_R + 8
CONV_ROWS = 64
CONV_PAD = 8
SUB_ROWS = 256
FFT_UNROLL = 16
RET_C = 256


def _params(sem, vmem_mib):
    return pltpu.CompilerParams(dimension_semantics=sem, vmem_limit_bytes=vmem_mib * MIB)


@functools.lru_cache(maxsize=None)
def _rope2d_tables():
    quarter = HEAD_DIM // 4
    inv = ROPE_BASE ** (-np.arange(quarter, dtype=np.float64) / quarter)
    t = np.arange(SEQ)
    pos = np.stack([t // GRID_W, t % GRID_W], axis=1).astype(np.float64)
    lane = np.arange(HEAD_DIM)
    half = lane // (HEAD_DIM // 2)
    e = lane % (HEAD_DIM // 2)
    ang = pos[:, half] * inv[e % quarter][None, :]
    sign = np.where(e < quarter, -1.0, 1.0)[None, :]
    cos = np.tile(np.cos(ang), (1, 2)).astype(np.float32)
    sin = np.tile(np.sin(ang) * sign, (1, 2)).astype(np.float32)
    return cos, sin


@functools.lru_cache(maxsize=None)
def _rope1d_tables():
    n = RET_DK // 2
    inv = ROPE_BASE ** (-np.linspace(0.0, 1.0, n))
    pos = np.arange(CTX_LEN + SEQ, dtype=np.float64)
    ang = pos[:, None] * inv[None, :]
    cos, sin = np.cos(ang), np.sin(ang)

    def rev(a):
        return a.reshape(-1, RET_C, n)[:, ::-1].reshape(-1, n)

    cos2 = np.stack([cos, rev(cos)]).astype(np.float32)
    sin2 = np.stack([sin, rev(sin)]).astype(np.float32)
    return cos2, sin2


@functools.lru_cache(maxsize=None)
def _fft_mats():
    N, R = FFT_N, FFT_R
    k1 = np.arange(R)
    n2 = np.arange(R)[:, None, None]
    n1 = np.arange(R)[None, None, :]
    n = n2 + R * n1
    idx = (k1[None, :, None] * n) % N
    ang = 2.0 * np.pi * idx / N
    c, s = np.cos(ang), np.sin(ang)
    m1 = np.concatenate([c, -s], axis=1).astype(np.float32)
    m1i = np.concatenate([c.transpose(0, 2, 1), -s.transpose(0, 2, 1)], axis=2)
    m1i = m1i[:, :R // 2].astype(np.float32)
    a2 = 2.0 * np.pi * ((np.arange(R)[:, None] * np.arange(R)[None, :]) % R) / R
    fr, fi = np.cos(a2), -np.sin(a2)
    f2 = np.block([[fr, -fi], [fi, fr]]).astype(np.float32)
    g2 = (np.block([[fr, fi], [-fi, fr]]) / N).astype(np.float32)
    return m1, m1i, f2, g2


@functools.lru_cache(maxsize=None)
def _ctx_fft_mats():
    N = 2 * CTX_LEN
    k = np.arange(N)[:, None]
    n = np.arange(N)[None, :]
    ang = 2.0 * np.pi * ((k * n) % N) / N
    c, s = np.cos(ang), np.sin(ang)
    fwd = np.concatenate([c, -s], axis=0).astype(np.float32)
    inv = (np.concatenate([c, -s], axis=1)[:CTX_LEN] / N).astype(np.float32)
    return fwd, inv


def _ada_kernel(c_ref, w_ref, b_ref, o_ref):
    c = c_ref[...]
    a = c * (1.0 / (1.0 + jnp.exp(-c)))
    o_ref[...] = jnp.dot(a, w_ref[...], precision=HIGHEST, preferred_element_type=F32) + b_ref[...]


def _ada(cvec, ada_w, ada_b):
    tn = 1536
    return pl.pallas_call(
        _ada_kernel,
        out_shape=jax.ShapeDtypeStruct((DEPTH, 8, N_MOD * D_MODEL), F32),
        grid=(DEPTH, N_MOD * D_MODEL // tn),
        in_specs=[
            pl.BlockSpec((8, D_MODEL), lambda i, j: (0, 0)),
            pl.BlockSpec((None, D_MODEL, tn), lambda i, j: (i, 0, j)),
            pl.BlockSpec((None, 1, tn), lambda i, j: (i, 0, j)),
        ],
        out_specs=pl.BlockSpec((None, 8, tn), lambda i, j: (i, 0, j)),
        compiler_params=_params(("parallel", "parallel"), 40),
        name="ada_mod",
    )(cvec, ada_w, ada_b.reshape(DEPTH, 1, N_MOD * D_MODEL))


def _mod_row(ref, row):
    if isinstance(row, int):
        return ref[row:row + 1, :]
    return ref[pl.ds(row, 1), :]


def _norm_mod(x, g, shift, scale):
    y = x * lax.rsqrt(jnp.mean(x * x, axis=-1, keepdims=True) + EPS)
    return (y * g) * (1.0 + scale) + shift


def _rope_tile(x, cos, sin_signed):
    lane = lax.broadcasted_iota(jnp.int32, x.shape, 1)
    first = (lane % 32) < 16
    partner = jnp.where(first, pltpu.roll(x, LANES - 16, axis=1), pltpu.roll(x, 16, axis=1))
    return x * cos + partner * sin_signed


def _project0(x_ref, g, sh, sc, w_ref, cos_ref, sin_ref, u_ref, qkv_ref, rope):
    tn = 512
    n_rot = (ATT_WIDTH + 2 * LANES) // LANES
    sub = min(x_ref.shape[0], SUB_ROWS)
    for s in range(x_ref.shape[0] // sub):
        rows = slice(s * sub, (s + 1) * sub)
        urows = slice(CONV_PAD + s * sub, CONV_PAD + (s + 1) * sub)
        h = _norm_mod(x_ref[rows, :], g, sh, sc).astype(BF16)
        for j in range(EV_U // tn):
            u_ref[urows, j * tn:(j + 1) * tn] = jnp.dot(h, w_ref[:, j * tn:(j + 1) * tn],
                                                        preferred_element_type=F32)
        for j in range(EV_QKV // tn):
            y = jnp.dot(h, w_ref[:, EV_U + j * tn:EV_U + (j + 1) * tn], preferred_element_type=F32)
            for t in range(tn // LANES):
                yt = y[:, t * LANES:(t + 1) * LANES]
                if rope and j * (tn // LANES) + t < n_rot:
                    yt = _rope_tile(yt, cos_ref[rows, :], sin_ref[rows, :])
                qkv_ref[rows, j * tn + t * LANES:j * tn + (t + 1) * LANES] = yt.astype(BF16)


def _short_conv_gate(u_ref, cw_ref, cb_ref, bias_ref, z_ref, x0_ref, zb_ref):
    tm = u_ref.shape[0] - 2 * CONV_PAD
    rc = CONV_ROWS

    def conv(r0, c0):
        cols = slice(c0, c0 + LANES)
        taps = [u_ref[CONV_PAD - 1 + k + r0:CONV_PAD - 1 + k + r0 + rc, cols] for k in range(3)]
        return (taps[0] * cw_ref[0:1, cols] + taps[1] * cw_ref[1:2, cols] + taps[2] * cw_ref[2:3, cols]
                + cb_ref[:, cols])

    for r0 in range(0, tm, rc):
        rows = slice(r0, r0 + rc)
        for c0 in range(0, HY_WIDTH, LANES):
            cols = slice(c0, c0 + LANES)
            x0 = conv(r0, c0)
            z = conv(r0, 2 * HY_WIDTH + c0) * conv(r0, HY_WIDTH + c0)
            z_ref[rows, cols] = z
            x0_ref[rows, cols] = x0
            zb_ref[rows, cols] = x0 * (z * bias_ref[:, cols])


def _inproj0_kernel(x_ref, g_ref, sh_ref, sc_ref, w_ref, cos_ref, sin_ref, cw_ref, cb_ref, bias_ref,
                    z_ref, x0_ref, zb_ref, qkv_ref, u_even, u_odd, *, row, rope, n_tiles):
    r = pl.program_id(0) if row is None else row
    i = pl.program_id(1)
    g, sh, sc = g_ref[...], _mod_row(sh_ref, r), _mod_row(sc_ref, r)
    conv_refs = (cw_ref, cb_ref, bias_ref, z_ref, x0_ref, zb_ref)
    zero_row = jnp.zeros((1, EV_U), F32)
    tm = x_ref.shape[0]
    above, first, last, below = CONV_PAD - 1, CONV_PAD, CONV_PAD + tm - 1, CONV_PAD + tm

    if n_tiles == 1:
        u_even[above:first, :] = zero_row
        u_even[below:below + 1, :] = zero_row
        _project0(x_ref, g, sh, sc, w_ref, cos_ref, sin_ref, u_even, qkv_ref, rope)
        _short_conv_gate(u_even, *conv_refs)
        return

    @pl.when(i == 0)
    def _():
        u_even[...] = jnp.zeros_like(u_even)
        u_odd[...] = jnp.zeros_like(u_odd)

    def step(u_new, u_old):
        j = i - 1
        u_old[above:first, :] = jnp.where(j > 0, u_new[last:last + 1, :], zero_row)
        _project0(x_ref, g, sh, sc, w_ref, cos_ref, sin_ref, u_new, qkv_ref, rope)
        u_old[below:below + 1, :] = jnp.where(j < n_tiles - 1, u_new[first:first + 1, :], zero_row)
        _short_conv_gate(u_old, *conv_refs)

    @pl.when(i % 2 == 0)
    def _():
        step(u_even, u_odd)

    @pl.when(i % 2 == 1)
    def _():
        step(u_odd, u_even)


def _inproj0(x, g, mod, w, conv_w, conv_b, bias, layer, *, is_ctx, tm):
    G, R, _ = x.shape
    n_tiles = R // tm
    n_steps = n_tiles + (1 if n_tiles > 1 else 0)
    cos, sin = _rope2d_tables()
    cos, sin = jnp.asarray(cos), jnp.asarray(sin)
    proj_tile = lambda i: jnp.minimum(i, n_tiles - 1)
    conv_tile = (lambda i: jnp.maximum(i - 1, 0)) if n_tiles > 1 else (lambda i: i)
    if is_ctx:
        tab = pl.BlockSpec((tm, LANES), lambda b, i: (0, 0))
    else:
        tab = pl.BlockSpec((tm, LANES), lambda b, i: (proj_tile(i), 0))
    const = lambda a: pl.BlockSpec(a.shape, lambda b, i: (0, 0))
    hy_out = jax.ShapeDtypeStruct((G, R, HY_WIDTH), F32)
    hy_spec = pl.BlockSpec((None, tm, HY_WIDTH), lambda b, i: (b, conv_tile(i), 0))
    kern = functools.partial(_inproj0_kernel, row=2 if is_ctx else None, rope=not is_ctx, n_tiles=n_tiles)
    return pl.pallas_call(
        kern,
        out_shape=(hy_out, hy_out, hy_out, jax.ShapeDtypeStruct((G, R, EV_QKV), BF16)),
        grid=(G, n_steps),
        in_specs=[
            pl.BlockSpec((None, tm, D_MODEL), lambda b, i: (b, proj_tile(i), 0)),
            pl.BlockSpec((1, D_MODEL), lambda b, i: (0, 0)),
            pl.BlockSpec((None, 8, D_MODEL), lambda b, i: (layer, 0, 0)),
            pl.BlockSpec((None, 8, D_MODEL), lambda b, i: (layer, 0, 1)),
            pl.BlockSpec((D_MODEL, EV_COLS), lambda b, i: (0, 0), pipeline_mode=pl.Buffered(1)),
            tab, tab, const(conv_w), const(conv_b), const(bias),
        ],
        out_specs=(hy_spec, hy_spec, hy_spec,
                   pl.BlockSpec((None, tm, EV_QKV), lambda b, i: (b, proj_tile(i), 0))),
        scratch_shapes=[pltpu.VMEM((tm + 2 * CONV_PAD, EV_U), F32)] * 2,
        compiler_params=_params(("parallel", "arbitrary"), 48),
        name="inproj0_ctx" if is_ctx else "inproj0",
    )(x, g, mod, mod, w, cos, sin, conv_w, conv_b, bias)


def _inproj1_kernel(x_ref, g_ref, sh_ref, sc_ref, w_ref, o_ref, *, row, tn):
    r = pl.program_id(0) if row is None else row
    g, sh, sc = g_ref[...], _mod_row(sh_ref, r), _mod_row(sc_ref, r)
    sub = min(x_ref.shape[0], SUB_ROWS)
    for s in range(x_ref.shape[0] // sub):
        rows = slice(s * sub, (s + 1) * sub)
        h = _norm_mod(x_ref[rows, :], g, sh, sc).astype(BF16)
        for j in range(w_ref.shape[1] // tn):
            cols = slice(j * tn, (j + 1) * tn)
            o_ref[rows, cols] = jnp.dot(h, w_ref[:, cols], preferred_element_type=F32).astype(o_ref.dtype)


def _inproj1(x, g, mod, w, layer, *, is_ctx, tm, tn):
    G, R, _ = x.shape
    N = w.shape[1]
    kern = functools.partial(_inproj1_kernel, row=2 if is_ctx else None, tn=tn)
    return pl.pallas_call(
        kern,
        out_shape=jax.ShapeDtypeStruct((G, R, N), BF16),
        grid=(G, R // tm),
        in_specs=[
            pl.BlockSpec((None, tm, D_MODEL), lambda b, i: (b, i, 0)),
            pl.BlockSpec((1, D_MODEL), lambda b, i: (0, 0)),
            pl.BlockSpec((None, 8, D_MODEL), lambda b, i: (layer, 0, 0)),
            pl.BlockSpec((None, 8, D_MODEL), lambda b, i: (layer, 0, 1)),
            pl.BlockSpec((D_MODEL, N), lambda b, i: (0, 0), pipeline_mode=pl.Buffered(1)),
        ],
        out_specs=pl.BlockSpec((None, tm, N), lambda b, i: (b, i, 0)),
        compiler_params=_params(("parallel", "parallel"), 48),
        name="inproj1_ctx" if is_ctx else "inproj1",
    )(x, g, mod, mod, w)


def _split_bf16(a):
    hi = a.astype(BF16)
    return hi, (a - hi.astype(F32)).astype(BF16)


def _dot3(a, b):
    a_hi, a_lo = _split_bf16(a)
    b_hi, b_lo = _split_bf16(b)
    dot = functools.partial(jnp.dot, preferred_element_type=F32)
    return dot(a_hi, b_hi) + (dot(a_lo, b_hi) + dot(a_hi, b_lo))


def _filter_kernel(w1t_ref, b1_ref, w2t_ref, b2_ref, fr_ref, w3_ref, dec_ref, bands_ref, h_ref, ss_ref, *, L, tm):
    i = pl.program_id(0)

    def time_of(r):
        return jnp.where(r < L, r, 2 * L - r).astype(F32) / L

    t_row = time_of(lax.broadcasted_iota(jnp.int32, (1, tm), 1) + i * tm)
    ang = (2.0 * math.pi * t_row) * bands_ref[...]
    t8 = jnp.where(lax.broadcasted_iota(jnp.int32, (8, tm), 0) == 0, t_row, 0.0)
    feat = jnp.concatenate([t8, jnp.cos(ang), -jnp.sin(ang)], axis=0)
    fr = fr_ref[...]
    hid = jnp.sin(fr * (jnp.dot(w1t_ref[...], feat, precision=HIGHEST, preferred_element_type=F32) + b1_ref[...]))
    hid = jnp.sin(fr * (jnp.dot(w2t_ref[...], hid, precision=HIGHEST, preferred_element_type=F32) + b2_ref[...]))
    h = _dot3(hid.T, w3_ref[...])
    r_col = lax.broadcasted_iota(jnp.int32, (tm, 1), 0) + i * tm
    h = h * jnp.exp(-time_of(r_col) * jnp.abs(dec_ref[...]))
    h = jnp.where(r_col == L, 0.0, h)
    h_ref[...] = h

    @pl.when(i == 0)
    def _():
        ss_ref[...] = jnp.zeros_like(ss_ref)

    ss_ref[...] += jnp.broadcast_to(jnp.sum(h * h, axis=0, keepdims=True), ss_ref.shape)


def _hyena_filter(w1, b1, w2, b2, w3, freq, decay, *, L):
    tm = min(L, 1024)
    tiles_per_side = L // tm
    kern = functools.partial(_filter_kernel, L=L, tm=tm)
    w1t = jnp.concatenate([w1[0:1], jnp.zeros((7, HY_HIDDEN), F32), w1[1:]], axis=0).T
    col = lambda a: a.reshape(HY_HIDDEN, 1)
    w3s = jnp.stack([w3[:, :HY_WIDTH], w3[:, HY_WIDTH:]])
    bands = jnp.asarray(np.linspace(1e-4, HY_BANDS - 1, HY_BANDS).astype(np.float32).reshape(HY_BANDS, 1))
    small = (w1t, col(b1), w2.T, col(b2), col(freq))
    full = lambda a: pl.BlockSpec(a.shape, lambda i: (0,) * a.ndim)
    side = lambda *blk: pl.BlockSpec((None,) + blk, lambda i: (i // tiles_per_side, 0, 0))
    return pl.pallas_call(
        kern,
        out_shape=(jax.ShapeDtypeStruct((2 * L, HY_WIDTH), F32), jax.ShapeDtypeStruct((8, HY_WIDTH), F32)),
        grid=(2 * L // tm,),
        in_specs=[full(a) for a in small] + [side(HY_HIDDEN, HY_WIDTH), side(1, HY_WIDTH), full(bands)],
        out_specs=(pl.BlockSpec((tm, HY_WIDTH), lambda i: (i, 0)), pl.BlockSpec((8, HY_WIDTH), lambda i: (0, 0))),
        compiler_params=_params(("arbitrary",), 48),
        name="hyena_filter_%d" % L,
    )(*small, w3s, decay.reshape(2, 1, HY_WIDTH), bands)


def _stack_bf16(re, im):
    return jnp.concatenate([re, im], axis=0).astype(BF16)


def _filter_fft_kernel(h_ref, ss_ref, m1_ref, f2_ref, o_ref, a_scr):
    R, P = FFT_R, FFT_PITCH
    inv_norm = lax.rsqrt(ss_ref[0:1, :])

    def stage1(n2, carry):
        xs = (h_ref[pl.ds(n2, R, stride=R), :] * inv_norm).astype(BF16)
        a = jnp.dot(m1_ref[n2], xs, preferred_element_type=F32)
        base = pl.multiple_of(n2 * P, 8)
        a_scr[0, pl.ds(base, R), :] = a[:R]
        a_scr[1, pl.ds(base, R), :] = a[R:]
        return carry

    lax.fori_loop(0, R, stage1, 0, unroll=FFT_UNROLL)

    def stage2(k1, carry):
        ar = a_scr[0, pl.ds(k1, R, stride=P), :]
        ai = a_scr[1, pl.ds(k1, R, stride=P), :]
        x = jnp.dot(f2_ref[...], _stack_bf16(ar, ai), preferred_element_type=F32)
        base = pl.multiple_of(k1 * R, R)
        o_ref[0, pl.ds(base, R), :] = x[:R].astype(BF16)
        o_ref[1, pl.ds(base, R), :] = x[R:].astype(BF16)
        return carry

    lax.fori_loop(0, R, stage2, 0, unroll=FFT_UNROLL)


def _filter_fft(h_circ, ss, m1, f2):
    ct = LANES
    once = pl.Buffered(1)
    return pl.pallas_call(
        _filter_fft_kernel,
        out_shape=jax.ShapeDtypeStruct((2, FFT_N, HY_WIDTH), BF16),
        grid=(HY_WIDTH // ct,),
        in_specs=[
            pl.BlockSpec((FFT_N, ct), lambda c: (0, c), pipeline_mode=once),
            pl.BlockSpec((8, ct), lambda c: (0, c)),
            pl.BlockSpec(m1.shape, lambda c: (0, 0, 0), pipeline_mode=once),
            pl.BlockSpec(f2.shape, lambda c: (0, 0), pipeline_mode=once),
        ],
        out_specs=pl.BlockSpec((2, FFT_N, ct), lambda c: (0, 0, c)),
        scratch_shapes=[pltpu.VMEM((2, FFT_R * FFT_PITCH, ct), F32)],
        compiler_params=_params(("parallel",), 58),
        name="hyena_filter_fft",
    )(h_circ, ss, m1, f2)


def _fft_conv_kernel(z_ref, hs_ref, m1_ref, f2_ref, g2_ref, m1i_ref, y_ref, a_scr):
    R, P = FFT_R, FFT_PITCH
    ct = a_scr.shape[-1]

    def stage1(n2, carry):
        rows = pl.ds(n2, R // 2, stride=R)
        xs = jnp.concatenate([z_ref[0, rows, :], z_ref[1, rows, :]], axis=1).astype(BF16)
        t = jnp.dot(m1_ref[n2, :, 0:R // 2], xs, preferred_element_type=F32)
        base = pl.multiple_of(n2 * P, 8)
        a_scr[0, pl.ds(base, R), :] = t[:R, :ct] - t[R:, ct:]
        a_scr[1, pl.ds(base, R), :] = t[R:, :ct] + t[:R, ct:]
        return carry

    lax.fori_loop(0, R, stage1, 0, unroll=FFT_UNROLL)

    def stage2(k1, carry):
        ar = a_scr[0, pl.ds(k1, R, stride=P), :]
        ai = a_scr[1, pl.ds(k1, R, stride=P), :]
        x = jnp.dot(f2_ref[...], _stack_bf16(ar, ai), preferred_element_type=F32)
        base = pl.multiple_of(k1 * R, R)
        hr = hs_ref[0, pl.ds(base, R), :].astype(F32)
        hi = hs_ref[1, pl.ds(base, R), :].astype(F32)
        xr, xi = x[:R], x[R:]
        yr = xr * hr - xi * hi
        yi = xr * hi + xi * hr
        bq = jnp.dot(g2_ref[...], _stack_bf16(yr, yi), preferred_element_type=F32)
        a_scr[0, pl.ds(k1, R, stride=P), :] = bq[:R]
        a_scr[1, pl.ds(k1, R, stride=P), :] = bq[R:]
        return carry

    lax.fori_loop(0, R, stage2, 0, unroll=FFT_UNROLL)

    def stage3(n2, carry):
        base = pl.multiple_of(n2 * P, 8)
        br = a_scr[0, pl.ds(base, R), :]
        bi = a_scr[1, pl.ds(base, R), :]
        rhs = jnp.concatenate([jnp.concatenate([br, bi], axis=1),
                               jnp.concatenate([bi, -br], axis=1)], axis=0).astype(BF16)
        y = jnp.dot(m1i_ref[n2], rhs, preferred_element_type=F32)
        rows = pl.ds(n2, R // 2, stride=R)
        y_ref[0, rows, :] = y[:, :ct]
        y_ref[1, rows, :] = y[:, ct:]
        return carry

    lax.fori_loop(0, R, stage3, 0, unroll=FFT_UNROLL)


def _fft_conv(z, hspec, m1, f2, g2, m1i):
    assert z.shape[0] == 2, "the batch pair is packed into one complex signal"
    ct = LANES
    once = pl.Buffered(1)
    return pl.pallas_call(
        _fft_conv_kernel,
        out_shape=jax.ShapeDtypeStruct((2, SEQ, HY_WIDTH), F32),
        grid=(HY_WIDTH // ct,),
        in_specs=[
            pl.BlockSpec((2, SEQ, ct), lambda c: (0, 0, c), pipeline_mode=once),
            pl.BlockSpec((2, FFT_N, ct), lambda c: (0, 0, c), pipeline_mode=once),
            pl.BlockSpec(m1.shape, lambda c: (0, 0, 0), pipeline_mode=once),
            pl.BlockSpec(f2.shape, lambda c: (0, 0), pipeline_mode=once),
            pl.BlockSpec(g2.shape, lambda c: (0, 0), pipeline_mode=once),
            pl.BlockSpec(m1i.shape, lambda c: (0, 0, 0), pipeline_mode=once),
        ],
        out_specs=pl.BlockSpec((2, SEQ, ct), lambda c: (0, 0, c), pipeline_mode=once),
        scratch_shapes=[pltpu.VMEM((2, FFT_R * FFT_PITCH, ct), F32)],
        compiler_params=_params(("arbitrary",), 58),
        name="hyena_fft_conv",
    )(z, hspec, m1, f2, g2, m1i)


def _ctx_conv_kernel(z_ref, h_ref, ss_ref, fwd_ref, inv_ref, y_ref):
    N = 2 * CTX_LEN
    hn = (h_ref[...] * lax.rsqrt(ss_ref[0:1, :])).astype(BF16)
    hs = jnp.dot(fwd_ref[...], hn, preferred_element_type=F32)
    zs = jnp.dot(fwd_ref[:, :CTX_LEN], z_ref[...].astype(BF16), preferred_element_type=F32)
    hr, hi, zr, zi = hs[:N], hs[N:], zs[:N], zs[N:]
    ys = _stack_bf16(zr * hr - zi * hi, zr * hi + zi * hr)
    y_ref[...] = jnp.dot(inv_ref[...], ys, preferred_element_type=F32)


def _ctx_conv(z, h_circ, ss, fwd, inv):
    G = z.shape[0]
    full = lambda a: pl.BlockSpec(a.shape, lambda b: (0,) * a.ndim)
    return pl.pallas_call(
        _ctx_conv_kernel,
        out_shape=jax.ShapeDtypeStruct((G, CTX_LEN, HY_WIDTH), F32),
        grid=(G,),
        in_specs=[pl.BlockSpec((None, CTX_LEN, HY_WIDTH), lambda b: (b, 0, 0)),
                  full(h_circ), full(ss), full(fwd), full(inv)],
        out_specs=pl.BlockSpec((None, CTX_LEN, HY_WIDTH), lambda b: (b, 0, 0)),
        compiler_params=_params(("parallel",), 48),
        name="hyena_ctx_conv",
    )(z, h_circ, ss, fwd, inv)


def _kv_group(refs, g):
    cols = slice(g * LANES, (g + 1) * LANES)
    return refs[0][:, cols] if len(refs) == 1 else jnp.concatenate([r[:, cols] for r in refs], axis=0)


def _attn_logits(q_ref, k_refs, g, masks):
    low = lax.broadcasted_iota(jnp.int32, (BLOCK, LANES), 1) < HEAD_DIM
    zero = jnp.zeros((), BF16)
    parts = []
    for hh in range(ATT_GROUP):
        h = g * ATT_GROUP + hh
        tile = q_ref[:, (h // 2) * LANES:(h // 2 + 1) * LANES]
        parts.append(jnp.where(low if h % 2 == 0 else ~low, tile, zero))
    s = lax.dot_general(jnp.concatenate(parts, axis=0), _kv_group(k_refs, g), (((1,), (1,)), ((), ())),
                        preferred_element_type=F32)
    if masks is not None:
        ok_prev, ok_next = masks
        s = jnp.concatenate([jnp.where(ok_prev, s[:, :BLOCK], NEG_INF), s[:, BLOCK:2 * BLOCK],
                             jnp.where(ok_next, s[:, 2 * BLOCK:3 * BLOCK], NEG_INF), s[:, 3 * BLOCK:]], axis=1)
    return s


def _attn_output(s, sink_ref, v_refs, g, o_ref):
    low = lax.broadcasted_iota(jnp.int32, (BLOCK, LANES), 1) < HEAD_DIM
    low4 = jnp.concatenate([low] * ATT_GROUP, axis=0)
    v = _kv_group(v_refs, g)
    v_aug = jnp.where(lax.broadcasted_iota(jnp.int32, v.shape, 1) < HEAD_DIM, v, jnp.ones((), BF16))
    sink = jnp.concatenate(
        [jnp.full((BLOCK, 1), sink_ref[g * ATT_GROUP + hh] * LOG2E, F32) for hh in range(ATT_GROUP)], axis=0)
    m = jnp.maximum(jnp.max(s, axis=-1, keepdims=True), sink)
    e = jnp.exp2(s - m).astype(BF16)
    o = jnp.dot(e, v_aug, preferred_element_type=F32) + jnp.where(low4, 0.0, jnp.exp2(sink - m))
    swapped = pltpu.roll(o, HEAD_DIM, axis=1)
    for pair in range(ATT_GROUP // 2):
        ev = slice((2 * pair) * BLOCK, (2 * pair + 1) * BLOCK)
        od = slice((2 * pair + 1) * BLOCK, (2 * pair + 2) * BLOCK)
        even = o[ev] / swapped[ev]
        odd = swapped[od] / o[od]
        t = g * (ATT_GROUP // 2) + pair
        o_ref[:, t * LANES:(t + 1) * LANES] = jnp.where(low, even, odd).astype(o_ref.dtype)


def _attn_ctx_kernel(sink_ref, q_ref, kx_ref, vx_ref, o_ref):
    for g in range(ATT_KV_HEADS):
        _attn_output(_attn_logits(q_ref, (kx_ref,), g, None), sink_ref, (vx_ref,), g, o_ref)


def _attn_local_kernel(sink_ref, q_ref, kp_ref, kc_ref, kn_ref, kx_ref, vp_ref, vc_ref, vn_ref, vx_ref,
                       o_ref, s_even, s_odd, *, n_blocks):
    n = pl.program_id(1)

    @pl.when(n == 0)
    def _():
        s_odd[...] = jnp.zeros_like(s_odd)

    def step(s_new, s_prev):
        nq = jnp.minimum(n, n_blocks - 1)
        qi = lax.broadcasted_iota(jnp.int32, (ATT_GROUP * BLOCK, BLOCK), 0) % BLOCK
        ki = lax.broadcasted_iota(jnp.int32, (ATT_GROUP * BLOCK, BLOCK), 1)
        masks = ((ki >= qi) & (nq > 0), (ki <= qi) & (nq < n_blocks - 1))
        for g in range(ATT_KV_HEADS):
            s_new[g] = _attn_logits(q_ref, (kp_ref, kc_ref, kn_ref, kx_ref), g, masks)
        for g in range(ATT_KV_HEADS):
            _attn_output(s_prev[g], sink_ref, (vp_ref, vc_ref, vn_ref, vx_ref), g, o_ref)

    @pl.when(n % 2 == 0)
    def _():
        step(s_even, s_odd)

    @pl.when(n % 2 == 1)
    def _():
        step(s_odd, s_even)


def _attention(sink, qkv, qkv_ctx, *, local):
    src = qkv if local else qkv_ctx
    G, R, _ = src.shape
    nb = R // BLOCK
    kcol, vcol = ATT_WIDTH // 256, ATT_WIDTH // 256 + 1
    ctx_k = pl.BlockSpec((None, CTX_LEN, 256), lambda b, n: (b, 0, kcol))
    ctx_v = pl.BlockSpec((None, CTX_LEN, 256), lambda b, n: (b, 0, vcol))
    smem = pl.BlockSpec(memory_space=pltpu.SMEM)
    out_shape = jax.ShapeDtypeStruct((G, R, ATT_WIDTH), BF16)
    if not local:
        rows = pl.BlockSpec((None, BLOCK, ATT_WIDTH), lambda b, n: (b, n, 0))
        return pl.pallas_call(
            _attn_ctx_kernel, out_shape=out_shape, grid=(G, nb),
            in_specs=[smem, rows, ctx_k, ctx_v], out_specs=rows,
            compiler_params=_params(("parallel", "parallel"), 48), name="attn_ctx",
        )(sink, qkv_ctx, qkv_ctx, qkv_ctx)

    q_blk = lambda n: jnp.minimum(n, nb - 1)
    o_blk = lambda n: jnp.maximum(n - 1, 0)

    def nbr(col, blk, off):
        return pl.BlockSpec((None, BLOCK, 256), lambda b, n: (b, jnp.clip(blk(n) + off, 0, nb - 1), col))

    in_specs = [smem, pl.BlockSpec((None, BLOCK, ATT_WIDTH), lambda b, n: (b, q_blk(n), 0)),
                nbr(kcol, q_blk, -1), nbr(kcol, q_blk, 0), nbr(kcol, q_blk, 1), ctx_k,
                nbr(vcol, o_blk, -1), nbr(vcol, o_blk, 0), nbr(vcol, o_blk, 1), ctx_v]
    logits_scratch = pltpu.VMEM((ATT_KV_HEADS, ATT_GROUP * BLOCK, 3 * BLOCK + CTX_LEN), F32)
    return pl.pallas_call(
        functools.partial(_attn_local_kernel, n_blocks=nb),
        out_shape=out_shape,
        grid=(G, nb + 1),
        in_specs=in_specs,
        out_specs=pl.BlockSpec((None, BLOCK, ATT_WIDTH), lambda b, n: (b, o_blk(n), 0)),
        scratch_shapes=[logits_scratch, logits_scratch],
        compiler_params=_params(("parallel", "arbitrary"), 48),
        name="attn_local",
    )(sink, qkv, qkv, qkv, qkv, qkv_ctx, qkv, qkv, qkv, qkv_ctx)


def _mixer_out0(rows, x0_ref, y_ref, zb_ref, att_ref, wo_ref):
    hy = (x0_ref[rows, :] * y_ref[rows, :] + zb_ref[rows, :]).astype(BF16)
    return (jnp.dot(hy, wo_ref[:HY_WIDTH, :], preferred_element_type=F32)
            + jnp.dot(att_ref[rows, :], wo_ref[HY_WIDTH:, :], preferred_element_type=F32))


def _mixer_out1(rows, of_ref, ob_ref, wo_ref):
    a = (of_ref[rows, :].astype(F32) + ob_ref[rows, :].astype(F32)).astype(BF16)
    return jnp.dot(a, wo_ref[...], preferred_element_type=F32)


def _mix_kernel(*refs, mixer_out, n_mix, row, fc, final_norm):
    x_ref = refs[0]
    mix_refs = refs[1:2 + n_mix]
    gta_ref, gm_ref, shm_ref, scm_ref, gtm_ref, w1_ref, w2_ref, fg_ref, o_ref, a_scr = refs[2 + n_mix:]
    r = pl.program_id(0) if row is None else row
    gta, gtm = _mod_row(gta_ref, r), _mod_row(gtm_ref, r)
    gm, shm, scm = gm_ref[...], _mod_row(shm_ref, r), _mod_row(scm_ref, r)
    sub = min(x_ref.shape[0], SUB_ROWS)
    for s in range(x_ref.shape[0] // sub):
        rows = slice(s * sub, (s + 1) * sub)
        x1 = x_ref[rows, :] + gta * mixer_out(rows, *mix_refs)
        h = _norm_mod(x1, gm, shm, scm).astype(BF16)
        for c in range(D_FF // fc):
            a = jnp.maximum(jnp.dot(h, w1_ref[:, c * fc:(c + 1) * fc], preferred_element_type=F32), 0.0)
            a_scr[rows, c * fc:(c + 1) * fc] = (a * a).astype(BF16)
        out = x1 + gtm * jnp.dot(a_scr[rows, :], w2_ref[...], preferred_element_type=F32)
        if final_norm:
            out = (out * lax.rsqrt(jnp.mean(out * out, axis=-1, keepdims=True) + EPS)) * fg_ref[...]
        o_ref[rows, :] = out


def _mix_mlp(kind, x, mix_in, wo, mod, gm, w1, w2, fg, layer, *, is_ctx, tm, fc, final_norm):
    G, R, _ = x.shape
    row_spec = lambda w: pl.BlockSpec((None, tm, w), lambda b, i: (b, i, 0))
    modk = lambda k: pl.BlockSpec((None, 8, D_MODEL), lambda b, i: (layer, 0, k))
    vec = pl.BlockSpec((1, D_MODEL), lambda b, i: (0, 0))
    resident = lambda a: pl.BlockSpec(a.shape, lambda b, i: (0, 0), pipeline_mode=pl.Buffered(1))
    if kind == 0:
        mixer_out = _mixer_out0
        mix_specs = [row_spec(HY_WIDTH)] * 3 + [row_spec(ATT_WIDTH)]
    else:
        mixer_out = _mixer_out1
        mix_specs = [row_spec(RET_V)] * 2
    kern = functools.partial(_mix_kernel, mixer_out=mixer_out, n_mix=len(mix_in), row=2 if is_ctx else None,
                             fc=fc, final_norm=final_norm)
    return pl.pallas_call(
        kern,
        out_shape=jax.ShapeDtypeStruct((G, R, D_MODEL), F32),
        grid=(G, R // tm),
        in_specs=[row_spec(D_MODEL)] + mix_specs + [
            resident(wo), modk(2), vec, modk(3), modk(4), modk(5), resident(w1), resident(w2), vec,
        ],
        out_specs=row_spec(D_MODEL),
        scratch_shapes=[pltpu.VMEM((tm, D_FF), BF16)],
        compiler_params=_params(("parallel", "parallel"), 56),
        name="mix_mlp%d%s" % (kind, "_ctx" if is_ctx else ""),
    )(x, *mix_in, wo, mod, gm, mod, mod, mod, w1, w2, fg)


def _ret_kernel(lr_ref, qkv_c, qkv_f, g_f, qkv_b, g_b, cos_ref, sin_ref, of_ref, ob_ref,
                state, dmask, xi, zeta, gch):
    j = pl.program_id(1)
    C = RET_C
    kscale = RET_DK ** -0.5

    @pl.when(j == 0)
    def _():
        state[...] = jnp.zeros_like(state)
        row = lax.broadcasted_iota(jnp.int32, (C, C), 0).astype(F32)
        col = lax.broadcasted_iota(jnp.int32, (C, C), 1).astype(F32)
        for d in range(2):
            e = row - col if d == 0 else col - row
            p = row if d == 0 else (C - 1) - row
            for h in range(RET_HEADS):
                lg = -jnp.exp(jnp.full((C, C), lr_ref[d, h], F32))
                dmask[d, h] = jnp.where(e >= 0, jnp.exp(e * lg), 0.0) * kscale
                xi[d, h] = jnp.exp((p + 1.0) * lg)
                zeta[d, h] = jnp.exp(((C - 1) - p) * lg) * kscale
                gch[d, h] = jnp.exp(C * -jnp.exp(jnp.full((8, C), lr_ref[d, h], F32)))

    half = RET_DK // 2

    def chain(d, h, src_ref, g_ref, o_ref):
        cos, sin = cos_ref[d], sin_ref[d]

        def rot(c0):
            t1 = src_ref[:, c0:c0 + half].astype(F32)
            t2 = src_ref[:, c0 + half:c0 + RET_DK].astype(F32)
            return jnp.concatenate([t1 * cos - t2 * sin, t2 * cos + t1 * sin], axis=1)

        q = rot(h * RET_DK)
        k = rot(RET_QK + h * RET_DK)
        v = src_ref[:, 2 * RET_QK + h * RET_DV:2 * RET_QK + (h + 1) * RET_DV]
        inner = lax.dot_general(q.astype(BF16), k.astype(BF16), (((1,), (1,)), ((), ())),
                                preferred_element_type=F32) * dmask[d, h]
        s_old = state[d, h]
        kv = lax.dot_general((k * zeta[d, h]).astype(BF16), v, (((0,), (0,)), ((), ())),
                             preferred_element_type=F32)
        state[d, h] = gch[d, h, 0:1, 0:1] * s_old + kv
        if o_ref is not None:
            o = (jnp.dot(inner.astype(BF16), v, preferred_element_type=F32)
                 + jnp.dot((q * xi[d, h]).astype(BF16), s_old.astype(BF16), preferred_element_type=F32))
            on = o * lax.rsqrt(jnp.mean(o * o, axis=-1, keepdims=True) + EPS)
            gate = g_ref[:, h * RET_DV:(h + 1) * RET_DV].astype(F32)
            o_ref[:, h * RET_DV:(h + 1) * RET_DV] = (gate * (1.0 / (1.0 + jnp.exp(-gate))) * on).astype(o_ref.dtype)

    @pl.when(j == 0)
    def _():
        for d in range(2):
            for h in range(RET_HEADS):
                chain(d, h, qkv_c, None, None)

    @pl.when(j > 0)
    def _():
        for h in range(RET_HEADS):
            chain(0, h, qkv_f, g_f, of_ref)
            chain(1, h, qkv_b, g_b, ob_ref)


def _retention(log_rate, p_ctx, p_lat):
    G = p_lat.shape[0]
    C = RET_C
    n_lat = SEQ // C
    cos, sin = _rope1d_tables()
    cos, sin = jnp.asarray(cos), jnp.asarray(sin)

    def fwd_chunk(j):
        return jnp.maximum(j - 1, 0)

    def bwd_chunk(j):
        return n_lat - 1 - jnp.maximum(j - 1, 0)

    qkv_w = 2 * RET_QK + RET_V
    gcol = qkv_w // RET_V
    tab = pl.BlockSpec((2, C, RET_DK // 2), lambda b, j: (0, j, 0))
    in_specs = [
        pl.BlockSpec(memory_space=pltpu.SMEM),
        pl.BlockSpec((None, C, qkv_w), lambda b, j: (b, 0, 0)),
        pl.BlockSpec((None, C, qkv_w), lambda b, j: (b, fwd_chunk(j), 0)),
        pl.BlockSpec((None, C, RET_V), lambda b, j: (b, fwd_chunk(j), gcol)),
        pl.BlockSpec((None, C, qkv_w), lambda b, j: (b, bwd_chunk(j), 0)),
        pl.BlockSpec((None, C, RET_V), lambda b, j: (b, bwd_chunk(j), gcol + 1)),
        tab, tab,
    ]
    out = jax.ShapeDtypeStruct((G, SEQ, RET_V), BF16)
    per_chain = lambda *tail: pltpu.VMEM((2, RET_HEADS) + tail, F32)
    return pl.pallas_call(
        _ret_kernel,
        out_shape=(out, out),
        grid=(G, 1 + n_lat),
        in_specs=in_specs,
        out_specs=(pl.BlockSpec((None, C, RET_V), lambda b, j: (b, fwd_chunk(j), 0)),
                   pl.BlockSpec((None, C, RET_V), lambda b, j: (b, bwd_chunk(j), 0))),
        scratch_shapes=[per_chain(RET_DK, RET_DV), per_chain(C, C), per_chain(C, C), per_chain(C, C),
                        per_chain(8, C)],
        compiler_params=_params(("parallel", "arbitrary"), 48),
        name="retention",
    )(log_rate, p_ctx, p_lat, p_lat, p_lat, p_lat, cos, sin)


def _ev_weight(w_in):
    i_q = EV_U
    i_k = i_q + ATT_WIDTH
    i_v = i_k + KV_WIDTH
    cols = [w_in[:, :i_q], w_in[:, i_q:i_k] * (HEAD_DIM ** -0.5 * LOG2E)]
    for base in (i_k, i_v):
        for g in range(ATT_KV_HEADS):
            part = w_in[:, base + g * HEAD_DIM: base + (g + 1) * HEAD_DIM]
            cols += [part, part]
    return jnp.concatenate(cols, axis=1).astype(BF16)


def kernel(x, c, ctx, c_ctx, ada_w, ada_b, norm_mix_g, norm_mlp_g, mlp_w1, mlp_w2, ev_w_in, ev_w_out, hy_conv_w, hy_conv_b, hy_w1, hy_b1, hy_w2, hy_b2, hy_w3, hy_freq, hy_decay, hy_bias, attn_sink, od_w_in, od_w_out, ret_log_rate, final_g):
    D = D_MODEL
    cvec = jnp.concatenate([c, c_ctx[None, :], jnp.zeros((8 - BATCH - 1, D), F32)], axis=0)
    mod = _ada(cvec, ada_w, ada_b)

    m1, m1i, f2, g2 = (jnp.asarray(a).astype(BF16) for a in _fft_mats())
    cfwd, cinv = (jnp.asarray(a).astype(BF16) for a in _ctx_fft_mats())
    fg = final_g.reshape(1, D)

    gmix = norm_mix_g[0].reshape(1, D)
    gmlp = norm_mlp_g[0].reshape(1, D)
    w_in = _ev_weight(ev_w_in[0])
    conv = (hy_conv_w[0], hy_conv_b[0].reshape(1, EV_U), hy_bias[0].reshape(1, HY_WIDTH))
    z_l, x0_l, zb_l, qkv_l = _inproj0(x, gmix, mod, w_in, *conv, 0, is_ctx=False, tm=512)
    z_c, x0_c, zb_c, qkv_c = _inproj0(ctx, gmix, mod, w_in, *conv, 0, is_ctx=True, tm=CTX_LEN)

    filt = (hy_w1[0], hy_b1[0], hy_w2[0], hy_b2[0], hy_w3[0], hy_freq[0], hy_decay[0])
    h_l, ss_l = _hyena_filter(*filt, L=SEQ)
    h_c, ss_c = _hyena_filter(*filt, L=CTX_LEN)
    hspec = _filter_fft(h_l, ss_l, m1, f2)
    y_l = _fft_conv(z_l, hspec, m1, f2, g2, m1i)
    y_c = _ctx_conv(z_c, h_c, ss_c, cfwd, cinv)

    sink = attn_sink[0]
    att_l = _attention(sink, qkv_l, qkv_c, local=True)
    att_c = _attention(sink, qkv_l, qkv_c, local=False)

    wo = ev_w_out[0].astype(BF16)
    w1 = mlp_w1[0].astype(BF16)
    w2 = mlp_w2[0].astype(BF16)
    x1 = _mix_mlp(0, x, (x0_l, y_l, zb_l, att_l), wo, mod, gmlp, w1, w2, fg, 0,
                  is_ctx=False, tm=512, fc=512, final_norm=False)
    ctx1 = _mix_mlp(0, ctx, (x0_c, y_c, zb_c, att_c), wo, mod, gmlp, w1, w2, fg, 0,
                    is_ctx=True, tm=CTX_LEN, fc=512, final_norm=False)

    gmix = norm_mix_g[1].reshape(1, D)
    gmlp = norm_mlp_g[1].reshape(1, D)
    w_in = od_w_in[0].astype(BF16)
    p_l = _inproj1(x1, gmix, mod, w_in, 1, is_ctx=False, tm=512, tn=512)
    p_c = _inproj1(ctx1, gmix, mod, w_in, 1, is_ctx=True, tm=CTX_LEN, tn=512)
    o_fwd, o_bwd = _retention(ret_log_rate[0], p_c, p_l)

    wo = od_w_out[0].astype(BF16)
    w1 = mlp_w1[1].astype(BF16)
    w2 = mlp_w2[1].astype(BF16)
    return _mix_mlp(1, x1, (o_fwd, o_bwd), wo, mod, gmlp, w1, w2, fg, 1,
                    is_ctx=False, tm=512, fc=512, final_norm=True)
```

```python
import functools
import math

import numpy as np
import jax
import jax.numpy as jnp
from jax import lax
from jax.experimental import pallas as pl
from jax.experimental.pallas import tpu as pltpu

F32 = jnp.float32
BF16 = jnp.bfloat16
HIGHEST = lax.Precision.HIGHEST

D_MODEL = 1024
BATCH = 2
SEQ = 8192
DEPTH = 2
GRID_W = 64
CTX_LEN = 256
EPS = 1e-6
NEG_INF = -1e30
N_MOD = 6
D_FF = 4 * D_MODEL
ROPE_BASE = 10000.0

HY_WIDTH = D_MODEL // 2
HY_EMB = 33
HY_BANDS = (HY_EMB - 1) // 2
HY_HIDDEN = 64

ATT_HEADS = 8
ATT_KV_HEADS = 2
ATT_GROUP = ATT_HEADS // ATT_KV_HEADS
HEAD_DIM = 64
ATT_WIDTH = ATT_HEADS * HEAD_DIM
KV_WIDTH = ATT_KV_HEADS * HEAD_DIM
BLOCK = 128

RET_HEADS = 4
RET_DK = D_MODEL // RET_HEADS
RET_DV = 2 * RET_DK
RET_QK = RET_HEADS * RET_DK
RET_V = RET_HEADS * RET_DV
OD_IN = 2 * RET_QK + 3 * RET_V

LOG2E = 1.4426950408889634
LANES = 128
MIB = 1024 * 1024

EV_U = 3 * HY_WIDTH
EV_QKV = ATT_WIDTH + 4 * LANES
EV_COLS = EV_U + EV_QKV

FFT_N = 2 * SEQ
FFT_R = 128
FFT_PITCH = FFT_R + 8
CONV_ROWS = 64
CONV_PAD = 8
SUB_ROWS = 256
FFT_UNROLL = 16
RET_C = 256


def _params(sem, vmem_mib):
    return pltpu.CompilerParams(dimension_semantics=sem, vmem_limit_bytes=vmem_mib * MIB)


@functools.lru_cache(maxsize=None)
def _rope2d_tables():
    quarter = HEAD_DIM // 4
    inv = ROPE_BASE ** (-np.arange(quarter, dtype=np.float64) / quarter)
    t = np.arange(SEQ)
    pos = np.stack([t // GRID_W, t % GRID_W], axis=1).astype(np.float64)
    lane = np.arange(HEAD_DIM)
    half = lane // (HEAD_DIM // 2)
    e = lane % (HEAD_DIM // 2)
    ang = pos[:, half] * inv[e % quarter][None, :]
    sign = np.where(e < quarter, -1.0, 1.0)[None, :]
    cos = np.tile(np.cos(ang), (1, 2)).astype(np.float32)
    sin = np.tile(np.sin(ang) * sign, (1, 2)).astype(np.float32)
    return cos, sin


@functools.lru_cache(maxsize=None)
def _rope1d_tables():
    n = RET_DK // 2
    inv = ROPE_BASE ** (-np.linspace(0.0, 1.0, n))

    def tables(pos_fwd, pos_bwd):
        ang = np.stack([pos_fwd, pos_bwd])[:, :, None] * inv[None, None, :]
        return np.cos(ang).astype(np.float32), np.sin(ang).astype(np.float32)

    s = np.arange(CTX_LEN, dtype=np.float64)
    t = CTX_LEN + np.arange(SEQ, dtype=np.float64)
    return tables(s, -(s + CTX_LEN + SEQ)), tables(t, -t)


@functools.lru_cache(maxsize=None)
def _fft_mats():
    N, R = FFT_N, FFT_R
    k1 = np.arange(R)
    n2 = np.arange(R)[:, None, None]
    n1 = np.arange(R)[None, None, :]
    n = n2 + R * n1
    idx = (k1[None, :, None] * n) % N
    ang = 2.0 * np.pi * idx / N
    c, s = np.cos(ang), np.sin(ang)
    m1 = np.concatenate([c, -s], axis=1).astype(np.float32)
    m1i = np.concatenate([c.transpose(0, 2, 1), -s.transpose(0, 2, 1)], axis=2)
    m1i = m1i[:, :R // 2].astype(np.float32)
    a2 = 2.0 * np.pi * ((np.arange(R)[:, None] * np.arange(R)[None, :]) % R) / R
    fr, fi = np.cos(a2), -np.sin(a2)
    f2 = np.block([[fr, -fi], [fi, fr]]).astype(np.float32)
    g2 = (np.block([[fr, fi], [-fi, fr]]) / N).astype(np.float32)
    return m1, m1i, f2, g2


@functools.lru_cache(maxsize=None)
def _ctx_fft_mats():
    N = 2 * CTX_LEN
    k = np.arange(N)[:, None]
    n = np.arange(N)[None, :]
    ang = 2.0 * np.pi * ((k * n) % N) / N
    c, s = np.cos(ang), np.sin(ang)
    fwd = np.concatenate([c, -s], axis=0).astype(np.float32)
    inv = (np.concatenate([c, -s], axis=1)[:CTX_LEN] / N).astype(np.float32)
    return fwd, inv


def _ada_kernel(c_ref, w_ref, b_ref, o_ref):
    c = c_ref[...]
    a = c * (1.0 / (1.0 + jnp.exp(-c)))
    o_ref[...] = _dot3(a, w_ref[...]) + b_ref[...]


def _ada(cvec, ada_w, ada_b):
    tn = 1536
    return pl.pallas_call(
        _ada_kernel,
        out_shape=jax.ShapeDtypeStruct((DEPTH, 8, N_MOD * D_MODEL), F32),
        grid=(DEPTH, N_MOD * D_MODEL // tn),
        in_specs=[
            pl.BlockSpec((8, D_MODEL), lambda i, j: (0, 0)),
            pl.BlockSpec((None, D_MODEL, tn), lambda i, j: (i, 0, j)),
            pl.BlockSpec((None, 1, tn), lambda i, j: (i, 0, j)),
        ],
        out_specs=pl.BlockSpec((None, 8, tn), lambda i, j: (i, 0, j)),
        compiler_params=_params(("parallel", "parallel"), 40),
        name="ada_mod",
    )(cvec, ada_w, ada_b.reshape(DEPTH, 1, N_MOD * D_MODEL))


def _mod_row(ref, row):
    if isinstance(row, int):
        return ref[row:row + 1, :]
    return ref[pl.ds(row, 1), :]


def _norm_mod(x, g, shift, scale):
    y = x * lax.rsqrt(jnp.mean(x * x, axis=-1, keepdims=True) + EPS)
    return (y * g) * (1.0 + scale) + shift


def _rope_tile(x, cos, sin_signed):
    lane = lax.broadcasted_iota(jnp.int32, x.shape, 1)
    first = (lane % 32) < 16
    partner = jnp.where(first, pltpu.roll(x, LANES - 16, axis=1), pltpu.roll(x, 16, axis=1))
    return x * cos + partner * sin_signed


def _project0(x_ref, g, sh, sc, w_ref, cos_ref, sin_ref, u_ref, qkv_ref, rope):
    tn = 512
    n_rot = (ATT_WIDTH + 2 * LANES) // LANES
    sub = min(x_ref.shape[0], SUB_ROWS)
    for s in range(x_ref.shape[0] // sub):
        rows = slice(s * sub, (s + 1) * sub)
        urows = slice(CONV_PAD + s * sub, CONV_PAD + (s + 1) * sub)
        h = _norm_mod(x_ref[rows, :], g, sh, sc).astype(BF16)
        for j in range(EV_U // tn):
            y = jnp.dot(h, w_ref[:, j * tn:(j + 1) * tn], preferred_element_type=F32)
            u_ref[urows, j * tn:(j + 1) * tn] = y
            yield y[0:8, 0:LANES]
        for j in range(EV_QKV // tn):
            y = jnp.dot(h, w_ref[:, EV_U + j * tn:EV_U + (j + 1) * tn], preferred_element_type=F32)
            for t in range(tn // LANES):
                yt = y[:, t * LANES:(t + 1) * LANES]
                if rope and j * (tn // LANES) + t < n_rot:
                    yt = _rope_tile(yt, cos_ref[rows, :], sin_ref[rows, :])
                qkv_ref[rows, j * tn + t * LANES:j * tn + (t + 1) * LANES] = yt.astype(BF16)
            yield y[0:8, 0:LANES]


def _short_conv_pieces(u_ref, never, cw_ref, cb_ref, bias_ref, z_ref, x0_ref, zb_ref):
    tm = u_ref.shape[0] - 2 * CONV_PAD
    rc = CONV_ROWS

    def conv(r0, c0, anchor):
        cols = slice(c0, c0 + LANES)
        taps = [u_ref[CONV_PAD - 1 + k + r0:CONV_PAD - 1 + k + r0 + rc, cols] for k in range(3)]
        if anchor is not None:
            taps[1] = jnp.where(never, jnp.concatenate([anchor] * (rc // 8), axis=0), taps[1])
        return (taps[0] * cw_ref[0:1, cols] + taps[1] * cw_ref[1:2, cols] + taps[2] * cw_ref[2:3, cols]
                + cb_ref[:, cols])

    def piece(r0, c0, anchor):
        rows, cols = slice(r0, r0 + rc), slice(c0, c0 + LANES)
        x0 = conv(r0, c0, anchor)
        z = conv(r0, 2 * HY_WIDTH + c0, anchor) * conv(r0, HY_WIDTH + c0, anchor)
        z_ref[rows, cols] = z
        x0_ref[rows, cols] = x0
        zb_ref[rows, cols] = x0 * (z * bias_ref[:, cols])

    return [functools.partial(piece, r0, c0) for r0 in range(0, tm, rc) for c0 in range(0, HY_WIDTH, LANES)]


def _inproj0_kernel(x_ref, g_ref, sh_ref, sc_ref, w_ref, cos_ref, sin_ref, cw_ref, cb_ref, bias_ref,
                    z_ref, x0_ref, zb_ref, qkv_ref, u_even, u_odd, *, row, rope, n_tiles):
    r = pl.program_id(0) if row is None else row
    i = pl.program_id(1)
    g, sh, sc = g_ref[...], _mod_row(sh_ref, r), _mod_row(sc_ref, r)
    conv_refs = (cw_ref, cb_ref, bias_ref, z_ref, x0_ref, zb_ref)
    zero_row = jnp.zeros((1, EV_U), F32)
    tm = x_ref.shape[0]
    above, first, last, below = CONV_PAD - 1, CONV_PAD, CONV_PAD + tm - 1, CONV_PAD + tm

    if n_tiles == 1:
        u_even[above:first, :] = zero_row
        u_even[below:below + 1, :] = zero_row
        for _ in _project0(x_ref, g, sh, sc, w_ref, cos_ref, sin_ref, u_even, qkv_ref, rope):
            pass
        for piece in _short_conv_pieces(u_even, None, *conv_refs):
            piece(None)
        return

    @pl.when(i == 0)
    def _():
        u_even[...] = jnp.zeros_like(u_even)
        u_odd[...] = jnp.zeros_like(u_odd)

    def step(u_new, u_old):
        j = i - 1
        u_old[above:first, :] = jnp.where(j > 0, u_new[last:last + 1, :], zero_row)
        pieces = _short_conv_pieces(u_old, i < 0, *conv_refs)
        last_chunk = len(pieces) - HY_WIDTH // LANES
        n_dots = (tm // min(tm, SUB_ROWS)) * (EV_COLS // 512)
        per_dot = -(-last_chunk // (n_dots - 1))
        done = 0
        for anchor in _project0(x_ref, g, sh, sc, w_ref, cos_ref, sin_ref, u_new, qkv_ref, rope):
            for piece in pieces[done:min(done + per_dot, last_chunk)]:
                piece(anchor)
            done = min(done + per_dot, last_chunk)
        u_old[below:below + 1, :] = jnp.where(j < n_tiles - 1, u_new[first:first + 1, :], zero_row)
        for piece in pieces[last_chunk:]:
            piece(anchor)

    @pl.when(i % 2 == 0)
    def _():
        step(u_even, u_odd)

    @pl.when(i % 2 == 1)
    def _():
        step(u_odd, u_even)


def _inproj0(x, g, mod, w, conv_w, conv_b, bias, layer, *, is_ctx, tm):
    G, R, _ = x.shape
    n_tiles = R // tm
    n_steps = n_tiles + (1 if n_tiles > 1 else 0)
    cos, sin = _rope2d_tables()
    cos, sin = jnp.asarray(cos), jnp.asarray(sin)
    proj_tile = lambda i: jnp.minimum(i, n_tiles - 1)
    conv_tile = (lambda i: jnp.maximum(i - 1, 0)) if n_tiles > 1 else (lambda i: i)
    if is_ctx:
        tab = pl.BlockSpec((tm, LANES), lambda b, i: (0, 0))
    else:
        tab = pl.BlockSpec((tm, LANES), lambda b, i: (proj_tile(i), 0))
    const = lambda a: pl.BlockSpec(a.shape, lambda b, i: (0, 0))
    hy_out = jax.ShapeDtypeStruct((G, R, HY_WIDTH), F32)
    hy_spec = pl.BlockSpec((None, tm, HY_WIDTH), lambda b, i: (b, conv_tile(i), 0))
    kern = functools.partial(_inproj0_kernel, row=2 if is_ctx else None, rope=not is_ctx, n_tiles=n_tiles)
    return pl.pallas_call(
        kern,
        out_shape=(hy_out, hy_out, hy_out, jax.ShapeDtypeStruct((G, R, EV_QKV), BF16)),
        grid=(G, n_steps),
        in_specs=[
            pl.BlockSpec((None, tm, D_MODEL), lambda b, i: (b, proj_tile(i), 0)),
            pl.BlockSpec((1, D_MODEL), lambda b, i: (0, 0)),
            pl.BlockSpec((None, 8, D_MODEL), lambda b, i: (layer, 0, 0)),
            pl.BlockSpec((None, 8, D_MODEL), lambda b, i: (layer, 0, 1)),
            pl.BlockSpec((D_MODEL, EV_COLS), lambda b, i: (0, 0), pipeline_mode=pl.Buffered(1)),
            tab, tab, const(conv_w), const(conv_b), const(bias),
        ],
        out_specs=(hy_spec, hy_spec, hy_spec,
                   pl.BlockSpec((None, tm, EV_QKV), lambda b, i: (b, proj_tile(i), 0))),
        scratch_shapes=[pltpu.VMEM((tm + 2 * CONV_PAD, EV_U), F32)] * 2,
        compiler_params=_params(("parallel", "arbitrary"), 48),
        name="inproj0_ctx" if is_ctx else "inproj0",
    )(x, g, mod, mod, w, cos, sin, conv_w, conv_b, bias)


def _inproj1_kernel(x_ref, g_ref, sh_ref, sc_ref, w_ref, cos_ref, sin_ref, o_ref, *, row, tn):
    r = pl.program_id(0) if row is None else row
    g, sh, sc = g_ref[...], _mod_row(sh_ref, r), _mod_row(sc_ref, r)
    qk_w = 2 * RET_QK
    half = RET_DK // 2
    sub = min(x_ref.shape[0], SUB_ROWS)
    for s in range(x_ref.shape[0] // sub):
        rows = slice(s * sub, (s + 1) * sub)
        h = _norm_mod(x_ref[rows, :], g, sh, sc).astype(BF16)
        for j in range(w_ref.shape[1] // tn):
            y = jnp.dot(h, w_ref[:, j * tn:(j + 1) * tn], preferred_element_type=F32)
            if j * tn >= qk_w:
                o_ref[rows, qk_w + j * tn:qk_w + (j + 1) * tn] = y.astype(o_ref.dtype)
                continue
            for hh in range(tn // RET_DK):
                x1 = y[:, hh * RET_DK:hh * RET_DK + half]
                x2 = y[:, hh * RET_DK + half:(hh + 1) * RET_DK]
                for d in range(2):
                    c, sn = cos_ref[d, rows, :], sin_ref[d, rows, :]
                    c0 = d * qk_w + j * tn + hh * RET_DK
                    o_ref[rows, c0:c0 + half] = (x1 * c - x2 * sn).astype(o_ref.dtype)
                    o_ref[rows, c0 + half:c0 + RET_DK] = (x2 * c + x1 * sn).astype(o_ref.dtype)


def _inproj1(x, g, mod, w, tables, layer, *, is_ctx, tm, tn):
    G, R, _ = x.shape
    N = w.shape[1] + 2 * RET_QK
    cos, sin = (jnp.asarray(a) for a in tables)
    tab = pl.BlockSpec((2, tm, RET_DK // 2), lambda b, i: (0, i, 0))
    kern = functools.partial(_inproj1_kernel, row=2 if is_ctx else None, tn=tn)
    return pl.pallas_call(
        kern,
        out_shape=jax.ShapeDtypeStruct((G, R, N), BF16),
        grid=(G, R // tm),
        in_specs=[
            pl.BlockSpec((None, tm, D_MODEL), lambda b, i: (b, i, 0)),
            pl.BlockSpec((1, D_MODEL), lambda b, i: (0, 0)),
            pl.BlockSpec((None, 8, D_MODEL), lambda b, i: (layer, 0, 0)),
            pl.BlockSpec((None, 8, D_MODEL), lambda b, i: (layer, 0, 1)),
            pl.BlockSpec(w.shape, lambda b, i: (0, 0), pipeline_mode=pl.Buffered(1)),
            tab, tab,
        ],
        out_specs=pl.BlockSpec((None, tm, N), lambda b, i: (b, i, 0)),
        compiler_params=_params(("parallel", "parallel"), 52),
        name="inproj1_ctx" if is_ctx else "inproj1",
    )(x, g, mod, mod, w, cos, sin)


def _split_bf16(a):
    hi = a.astype(BF16)
    return hi, (a - hi.astype(F32)).astype(BF16)


def _dot3(a, b):
    a_hi, a_lo = _split_bf16(a)
    b_hi, b_lo = _split_bf16(b)
    dot = functools.partial(jnp.dot, preferred_element_type=F32)
    return dot(a_hi, b_hi) + (dot(a_lo, b_hi) + dot(a_hi, b_lo))


def _filter_kernel(w1t_ref, b1_ref, w2t_ref, b2_ref, fr_ref, w3_ref, dec_ref, bands_ref, h_ref, ss_ref, *, L, tm):
    i = pl.program_id(0)

    def time_of(r):
        return jnp.where(r < L, r, 2 * L - r).astype(F32) / L

    t_row = time_of(lax.broadcasted_iota(jnp.int32, (1, tm), 1) + i * tm)
    ang = (2.0 * math.pi * t_row) * bands_ref[...]
    t8 = jnp.where(lax.broadcasted_iota(jnp.int32, (8, tm), 0) == 0, t_row, 0.0)
    feat = jnp.concatenate([t8, jnp.cos(ang), -jnp.sin(ang)], axis=0)
    fr = fr_ref[...]
    hid = jnp.sin(fr * (jnp.dot(w1t_ref[...], feat, precision=HIGHEST, preferred_element_type=F32) + b1_ref[...]))
    hid = jnp.sin(fr * (jnp.dot(w2t_ref[...], hid, precision=HIGHEST, preferred_element_type=F32) + b2_ref[...]))
    h = _dot3(hid.T, w3_ref[...])
    r_col = lax.broadcasted_iota(jnp.int32, (tm, 1), 0) + i * tm
    h = h * jnp.exp(-time_of(r_col) * jnp.abs(dec_ref[...]))
    h = jnp.where(r_col == L, 0.0, h)
    h_ref[...] = h

    @pl.when(i == 0)
    def _():
        ss_ref[...] = jnp.zeros_like(ss_ref)

    ss_ref[...] += jnp.broadcast_to(jnp.sum(h * h, axis=0, keepdims=True), ss_ref.shape)


def _hyena_filter(w1, b1, w2, b2, w3, freq, decay, *, L):
    tm = min(L, 1024)
    tiles_per_side = L // tm
    kern = functools.partial(_filter_kernel, L=L, tm=tm)
    w1t = jnp.concatenate([w1[0:1], jnp.zeros((7, HY_HIDDEN), F32), w1[1:]], axis=0).T
    col = lambda a: a.reshape(HY_HIDDEN, 1)
    w3s = jnp.stack([w3[:, :HY_WIDTH], w3[:, HY_WIDTH:]])
    bands = jnp.asarray(np.linspace(1e-4, HY_BANDS - 1, HY_BANDS).astype(np.float32).reshape(HY_BANDS, 1))
    small = (w1t, col(b1), w2.T, col(b2), col(freq))
    full = lambda a: pl.BlockSpec(a.shape, lambda i: (0,) * a.ndim)
    side = lambda *blk: pl.BlockSpec((None,) + blk, lambda i: (i // tiles_per_side, 0, 0))
    return pl.pallas_call(
        kern,
        out_shape=(jax.ShapeDtypeStruct((2 * L, HY_WIDTH), F32), jax.ShapeDtypeStruct((8, HY_WIDTH), F32)),
        grid=(2 * L // tm,),
        in_specs=[full(a) for a in small] + [side(HY_HIDDEN, HY_WIDTH), side(1, HY_WIDTH), full(bands)],
        out_specs=(pl.BlockSpec((tm, HY_WIDTH), lambda i: (i, 0)), pl.BlockSpec((8, HY_WIDTH), lambda i: (0, 0))),
        compiler_params=_params(("arbitrary",), 48),
        name="hyena_filter_%d" % L,
    )(*small, w3s, decay.reshape(2, 1, HY_WIDTH), bands)


def _stack_bf16(re, im):
    return jnp.concatenate([re, im], axis=0).astype(BF16)


def _filter_fft_kernel(h_ref, ss_ref, m1_ref, f2_ref, o_ref, a_scr):
    R, P = FFT_R, FFT_PITCH
    inv_norm = lax.rsqrt(ss_ref[0:1, :])

    def stage1(n2, carry):
        xs = (h_ref[pl.ds(n2, R, stride=R), :] * inv_norm).astype(BF16)
        a = jnp.dot(m1_ref[n2], xs, preferred_element_type=F32)
        base = pl.multiple_of(n2 * P, 8)
        a_scr[0, pl.ds(base, R), :] = a[:R]
        a_scr[1, pl.ds(base, R), :] = a[R:]
        return carry

    lax.fori_loop(0, R, stage1, 0, unroll=FFT_UNROLL)

    def stage2(k1, carry):
        ar = a_scr[0, pl.ds(k1, R, stride=P), :]
        ai = a_scr[1, pl.ds(k1, R, stride=P), :]
        x = jnp.dot(f2_ref[...], _stack_bf16(ar, ai), preferred_element_type=F32)
        base = pl.multiple_of(k1 * R, R)
        o_ref[0, pl.ds(base, R), :] = x[:R].astype(BF16)
        o_ref[1, pl.ds(base, R), :] = x[R:].astype(BF16)
        return carry

    lax.fori_loop(0, R, stage2, 0, unroll=FFT_UNROLL)


def _filter_fft(h_circ, ss, m1, f2):
    ct = LANES
    once = pl.Buffered(1)
    return pl.pallas_call(
        _filter_fft_kernel,
        out_shape=jax.ShapeDtypeStruct((2, FFT_N, HY_WIDTH), BF16),
        grid=(HY_WIDTH // ct,),
        in_specs=[
            pl.BlockSpec((FFT_N, ct), lambda c: (0, c), pipeline_mode=once),
            pl.BlockSpec((8, ct), lambda c: (0, c)),
            pl.BlockSpec(m1.shape, lambda c: (0, 0, 0), pipeline_mode=once),
            pl.BlockSpec(f2.shape, lambda c: (0, 0), pipeline_mode=once),
        ],
        out_specs=pl.BlockSpec((2, FFT_N, ct), lambda c: (0, 0, c)),
        scratch_shapes=[pltpu.VMEM((2, FFT_R * FFT_PITCH, ct), F32)],
        compiler_params=_params(("parallel",), 58),
        name="hyena_filter_fft",
    )(h_circ, ss, m1, f2)


def _fft_conv_kernel(z_ref, hs_ref, m1_ref, f2_ref, g2_ref, m1i_ref, y_ref, a_scr):
    R, P = FFT_R, FFT_PITCH
    ct = a_scr.shape[-1]

    def stage1(n2, carry):
        rows = pl.ds(n2, R // 2, stride=R)
        xs = jnp.concatenate([z_ref[0, rows, :], z_ref[1, rows, :]], axis=1).astype(BF16)
        t = jnp.dot(m1_ref[n2, :, 0:R // 2], xs, preferred_element_type=F32)
        base = pl.multiple_of(n2 * P, 8)
        a_scr[0, pl.ds(base, R), :] = t[:R, :ct] - t[R:, ct:]
        a_scr[1, pl.ds(base, R), :] = t[R:, :ct] + t[:R, ct:]
        return carry

    lax.fori_loop(0, R, stage1, 0, unroll=FFT_UNROLL)

    def stage2(k1, carry):
        ar = a_scr[0, pl.ds(k1, R, stride=P), :]
        ai = a_scr[1, pl.ds(k1, R, stride=P), :]
        x = jnp.dot(f2_ref[...], _stack_bf16(ar, ai), preferred_element_type=F32)
        base = pl.multiple_of(k1 * R, R)
        hr = hs_ref[0, pl.ds(base, R), :].astype(F32)
        hi = hs_ref[1, pl.ds(base, R), :].astype(F32)
        xr, xi = x[:R], x[R:]
        yr = xr * hr - xi * hi
        yi = xr * hi + xi * hr
        bq = jnp.dot(g2_ref[...], _stack_bf16(yr, yi), preferred_element_type=F32)
        a_scr[0, pl.ds(k1, R, stride=P), :] = bq[:R]
        a_scr[1, pl.ds(k1, R, stride=P), :] = bq[R:]
        return carry

    lax.fori_loop(0, R, stage2, 0, unroll=FFT_UNROLL)

    def stage3(n2, carry):
        base = pl.multiple_of(n2 * P, 8)
        br = a_scr[0, pl.ds(base, R), :]
        bi = a_scr[1, pl.ds(base, R), :]
        rhs = jnp.concatenate([jnp.concatenate([br, bi], axis=1),
                               jnp.concatenate([bi, -br], axis=1)], axis=0).astype(BF16)
        y = jnp.dot(m1i_ref[n2], rhs, preferred_element_type=F32)
        rows = pl.ds(n2, R // 2, stride=R)
        y_ref[0, rows, :] = y[:, :ct]
        y_ref[1, rows, :] = y[:, ct:]
        return carry

    lax.fori_loop(0, R, stage3, 0, unroll=FFT_UNROLL)


def _fft_conv(z, hspec, m1, f2, g2, m1i):
    assert z.shape[0] == 2, "the batch pair is packed into one complex signal"
    ct = LANES
    once = pl.Buffered(1)
    return pl.pallas_call(
        _fft_conv_kernel,
        out_shape=jax.ShapeDtypeStruct((2, SEQ, HY_WIDTH), F32),
        grid=(HY_WIDTH // ct,),
        in_specs=[
            pl.BlockSpec((2, SEQ, ct), lambda c: (0, 0, c), pipeline_mode=once),
            pl.BlockSpec((2, FFT_N, ct), lambda c: (0, 0, c), pipeline_mode=once),
            pl.BlockSpec(m1.shape, lambda c: (0, 0, 0), pipeline_mode=once),
            pl.BlockSpec(f2.shape, lambda c: (0, 0), pipeline_mode=once),
            pl.BlockSpec(g2.shape, lambda c: (0, 0), pipeline_mode=once),
            pl.BlockSpec(m1i.shape, lambda c: (0, 0, 0), pipeline_mode=once),
        ],
        out_specs=pl.BlockSpec((2, SEQ, ct), lambda c: (0, 0, c), pipeline_mode=once),
        scratch_shapes=[pltpu.VMEM((2, FFT_R * FFT_PITCH, ct), F32)],
        compiler_params=_params(("arbitrary",), 58),
        name="hyena_fft_conv",
    )(z, hspec, m1, f2, g2, m1i)


def _ctx_conv_kernel(z_ref, h_ref, ss_ref, fwd_ref, inv_ref, y_ref):
    N = 2 * CTX_LEN
    hn = (h_ref[...] * lax.rsqrt(ss_ref[0:1, :])).astype(BF16)
    hs = jnp.dot(fwd_ref[...], hn, preferred_element_type=F32)
    zs = jnp.dot(fwd_ref[:, :CTX_LEN], z_ref[...].astype(BF16), preferred_element_type=F32)
    hr, hi, zr, zi = hs[:N], hs[N:], zs[:N], zs[N:]
    ys = _stack_bf16(zr * hr - zi * hi, zr * hi + zi * hr)
    y_ref[...] = jnp.dot(inv_ref[...], ys, preferred_element_type=F32)


def _ctx_conv(z, h_circ, ss, fwd, inv):
    G = z.shape[0]
    full = lambda a: pl.BlockSpec(a.shape, lambda b: (0,) * a.ndim)
    return pl.pallas_call(
        _ctx_conv_kernel,
        out_shape=jax.ShapeDtypeStruct((G, CTX_LEN, HY_WIDTH), F32),
        grid=(G,),
        in_specs=[pl.BlockSpec((None, CTX_LEN, HY_WIDTH), lambda b: (b, 0, 0)),
                  full(h_circ), full(ss), full(fwd), full(inv)],
        out_specs=pl.BlockSpec((None, CTX_LEN, HY_WIDTH), lambda b: (b, 0, 0)),
        compiler_params=_params(("parallel",), 48),
        name="hyena_ctx_conv",
    )(z, h_circ, ss, fwd, inv)


def _kv_group(refs, g):
    cols = slice(g * LANES, (g + 1) * LANES)
    return refs[0][:, cols] if len(refs) == 1 else jnp.concatenate([r[:, cols] for r in refs], axis=0)


def _attn_logits(q_ref, k_refs, g, masks):
    low = lax.broadcasted_iota(jnp.int32, (BLOCK, LANES), 1) < HEAD_DIM
    zero = jnp.zeros((), BF16)
    parts = []
    for hh in range(ATT_GROUP):
        h = g * ATT_GROUP + hh
        tile = q_ref[:, (h // 2) * LANES:(h // 2 + 1) * LANES]
        parts.append(jnp.where(low if h % 2 == 0 else ~low, tile, zero))
    s = lax.dot_general(jnp.concatenate(parts, axis=0), _kv_group(k_refs, g), (((1,), (1,)), ((), ())),
                        preferred_element_type=F32)
    if masks is not None:
        ok_prev, ok_next = masks
        s = jnp.concatenate([jnp.where(ok_prev, s[:, :BLOCK], NEG_INF), s[:, BLOCK:2 * BLOCK],
                             jnp.where(ok_next, s[:, 2 * BLOCK:3 * BLOCK], NEG_INF), s[:, 3 * BLOCK:]], axis=1)
    return s


def _attn_output(s, sink_ref, v_refs, g, o_ref):
    low = lax.broadcasted_iota(jnp.int32, (BLOCK, LANES), 1) < HEAD_DIM
    low4 = jnp.concatenate([low] * ATT_GROUP, axis=0)
    v = _kv_group(v_refs, g)
    v_aug = jnp.where(lax.broadcasted_iota(jnp.int32, v.shape, 1) < HEAD_DIM, v, jnp.ones((), BF16))
    sink = jnp.concatenate(
        [jnp.full((BLOCK, 1), sink_ref[g * ATT_GROUP + hh] * LOG2E, F32) for hh in range(ATT_GROUP)], axis=0)
    m = jnp.maximum(jnp.max(s, axis=-1, keepdims=True), sink)
    e = jnp.exp2(s - m).astype(BF16)
    o = jnp.dot(e, v_aug, preferred_element_type=F32) + jnp.where(low4, 0.0, jnp.exp2(sink - m))
    swapped = pltpu.roll(o, HEAD_DIM, axis=1)
    for pair in range(ATT_GROUP // 2):
        ev = slice((2 * pair) * BLOCK, (2 * pair + 1) * BLOCK)
        od = slice((2 * pair + 1) * BLOCK, (2 * pair + 2) * BLOCK)
        even = o[ev] / swapped[ev]
        odd = swapped[od] / o[od]
        t = g * (ATT_GROUP // 2) + pair
        o_ref[:, t * LANES:(t + 1) * LANES] = jnp.where(low, even, odd).astype(o_ref.dtype)


def _attn_ctx_kernel(sink_ref, q_ref, kx_ref, vx_ref, o_ref):
    for g in range(ATT_KV_HEADS):
        _attn_output(_attn_logits(q_ref, (kx_ref,), g, None), sink_ref, (vx_ref,), g, o_ref)


def _attn_local_kernel(sink_ref, q_ref, kp_ref, kc_ref, kn_ref, kx_ref, vp_ref, vc_ref, vn_ref, vx_ref,
                       o_ref, s_even, s_odd, *, n_blocks):
    n = pl.program_id(1)

    @pl.when(n == 0)
    def _():
        s_odd[...] = jnp.zeros_like(s_odd)

    def step(s_new, s_prev):
        nq = jnp.minimum(n, n_blocks - 1)
        qi = lax.broadcasted_iota(jnp.int32, (ATT_GROUP * BLOCK, BLOCK), 0) % BLOCK
        ki = lax.broadcasted_iota(jnp.int32, (ATT_GROUP * BLOCK, BLOCK), 1)
        masks = ((ki >= qi) & (nq > 0), (ki <= qi) & (nq < n_blocks - 1))
        for g in range(ATT_KV_HEADS):
            s_new[g] = _attn_logits(q_ref, (kp_ref, kc_ref, kn_ref, kx_ref), g, masks)
        for g in range(ATT_KV_HEADS):
            _attn_output(s_prev[g], sink_ref, (vp_ref, vc_ref, vn_ref, vx_ref), g, o_ref)

    @pl.when(n % 2 == 0)
    def _():
        step(s_even, s_odd)

    @pl.when(n % 2 == 1)
    def _():
        step(s_odd, s_even)


def _attention(sink, qkv, qkv_ctx, *, local):
    src = qkv if local else qkv_ctx
    G, R, _ = src.shape
    nb = R // BLOCK
    kcol, vcol = ATT_WIDTH // 256, ATT_WIDTH // 256 + 1
    ctx_k = pl.BlockSpec((None, CTX_LEN, 256), lambda b, n: (b, 0, kcol))
    ctx_v = pl.BlockSpec((None, CTX_LEN, 256), lambda b, n: (b, 0, vcol))
    smem = pl.BlockSpec(memory_space=pltpu.SMEM)
    out_shape = jax.ShapeDtypeStruct((G, R, ATT_WIDTH), BF16)
    if not local:
        rows = pl.BlockSpec((None, BLOCK, ATT_WIDTH), lambda b, n: (b, n, 0))
        return pl.pallas_call(
            _attn_ctx_kernel, out_shape=out_shape, grid=(G, nb),
            in_specs=[smem, rows, ctx_k, ctx_v], out_specs=rows,
            compiler_params=_params(("parallel", "parallel"), 48), name="attn_ctx",
        )(sink, qkv_ctx, qkv_ctx, qkv_ctx)

    q_blk = lambda n: jnp.minimum(n, nb - 1)
    o_blk = lambda n: jnp.maximum(n - 1, 0)

    def nbr(col, blk, off):
        return pl.BlockSpec((None, BLOCK, 256), lambda b, n: (b, jnp.clip(blk(n) + off, 0, nb - 1), col))

    in_specs = [smem, pl.BlockSpec((None, BLOCK, ATT_WIDTH), lambda b, n: (b, q_blk(n), 0)),
                nbr(kcol, q_blk, -1), nbr(kcol, q_blk, 0), nbr(kcol, q_blk, 1), ctx_k,
                nbr(vcol, o_blk, -1), nbr(vcol, o_blk, 0), nbr(vcol, o_blk, 1), ctx_v]
    logits_scratch = pltpu.VMEM((ATT_KV_HEADS, ATT_GROUP * BLOCK, 3 * BLOCK + CTX_LEN), F32)
    return pl.pallas_call(
        functools.partial(_attn_local_kernel, n_blocks=nb),
        out_shape=out_shape,
        grid=(G, nb + 1),
        in_specs=in_specs,
        out_specs=pl.BlockSpec((None, BLOCK, ATT_WIDTH), lambda b, n: (b, o_blk(n), 0)),
        scratch_shapes=[logits_scratch, logits_scratch],
        compiler_params=_params(("parallel", "arbitrary"), 48),
        name="attn_local",
    )(sink, qkv, qkv, qkv, qkv, qkv_ctx, qkv, qkv, qkv, qkv_ctx)


def _mixer_out0(rows, x0_ref, y_ref, zb_ref, att_ref, wo_ref):
    hy = (x0_ref[rows, :] * y_ref[rows, :] + zb_ref[rows, :]).astype(BF16)
    return (jnp.dot(hy, wo_ref[:HY_WIDTH, :], preferred_element_type=F32)
            + jnp.dot(att_ref[rows, :], wo_ref[HY_WIDTH:, :], preferred_element_type=F32))


def _mixer_out1(rows, of_ref, ob_ref, wo_ref):
    a = (of_ref[rows, :].astype(F32) + ob_ref[rows, :].astype(F32)).astype(BF16)
    return jnp.dot(a, wo_ref[...], preferred_element_type=F32)


def _mix_kernel(*refs, mixer_out, n_mix, row, fc, final_norm):
    x_ref = refs[0]
    mix_refs = refs[1:2 + n_mix]
    gta_ref, gm_ref, shm_ref, scm_ref, gtm_ref, w1_ref, w2_ref, fg_ref, o_ref, a_scr = refs[2 + n_mix:]
    r = pl.program_id(0) if row is None else row
    gta, gtm = _mod_row(gta_ref, r), _mod_row(gtm_ref, r)
    gm, shm, scm = gm_ref[...], _mod_row(shm_ref, r), _mod_row(scm_ref, r)
    sub = min(x_ref.shape[0], SUB_ROWS)
    for s in range(x_ref.shape[0] // sub):
        rows = slice(s * sub, (s + 1) * sub)
        x1 = x_ref[rows, :] + gta * mixer_out(rows, *mix_refs)
        h = _norm_mod(x1, gm, shm, scm).astype(BF16)
        for c in range(D_FF // fc):
            a = jnp.maximum(jnp.dot(h, w1_ref[:, c * fc:(c + 1) * fc], preferred_element_type=F32), 0.0)
            a_scr[rows, c * fc:(c + 1) * fc] = (a * a).astype(BF16)
        out = x1 + gtm * jnp.dot(a_scr[rows, :], w2_ref[...], preferred_element_type=F32)
        if final_norm:
            out = (out * lax.rsqrt(jnp.mean(out * out, axis=-1, keepdims=True) + EPS)) * fg_ref[...]
        o_ref[rows, :] = out


def _mix_mlp(kind, x, mix_in, wo, mod, gm, w1, w2, fg, layer, *, is_ctx, tm, fc, final_norm):
    G, R, _ = x.shape
    row_spec = lambda w: pl.BlockSpec((None, tm, w), lambda b, i: (b, i, 0))
    modk = lambda k: pl.BlockSpec((None, 8, D_MODEL), lambda b, i: (layer, 0, k))
    vec = pl.BlockSpec((1, D_MODEL), lambda b, i: (0, 0))
    resident = lambda a: pl.BlockSpec(a.shape, lambda b, i: (0, 0), pipeline_mode=pl.Buffered(1))
    if kind == 0:
        mixer_out = _mixer_out0
        mix_specs = [row_spec(HY_WIDTH)] * 3 + [row_spec(ATT_WIDTH)]
    else:
        mixer_out = _mixer_out1
        mix_specs = [row_spec(RET_V)] * 2
    kern = functools.partial(_mix_kernel, mixer_out=mixer_out, n_mix=len(mix_in), row=2 if is_ctx else None,
                             fc=fc, final_norm=final_norm)
    return pl.pallas_call(
        kern,
        out_shape=jax.ShapeDtypeStruct((G, R, D_MODEL), F32),
        grid=(G, R // tm),
        in_specs=[row_spec(D_MODEL)] + mix_specs + [
            resident(wo), modk(2), vec, modk(3), modk(4), modk(5), resident(w1), resident(w2), vec,
        ],
        out_specs=row_spec(D_MODEL),
        scratch_shapes=[pltpu.VMEM((tm, D_FF), BF16)],
        compiler_params=_params(("parallel", "parallel"), 56),
        name="mix_mlp%d%s" % (kind, "_ctx" if is_ctx else ""),
    )(x, *mix_in, wo, mod, gm, mod, mod, mod, w1, w2, fg)


def _ret_kernel(lr_ref, qk_cf, qk_cb, v_c, qk_f, v_f, g_f, qk_b, v_b, g_b, of_ref, ob_ref,
                state, dmask, xi, zeta, gch):
    j = pl.program_id(1)
    C = RET_C
    kscale = RET_DK ** -0.5

    @pl.when(j == 0)
    def _():
        state[...] = jnp.zeros_like(state)
        row = lax.broadcasted_iota(jnp.int32, (C, C), 0).astype(F32)
        col = lax.broadcasted_iota(jnp.int32, (C, C), 1).astype(F32)
        for d in range(2):
            e = row - col if d == 0 else col - row
            p = row if d == 0 else (C - 1) - row
            for h in range(RET_HEADS):
                lg = -jnp.exp(jnp.full((C, C), lr_ref[d, h], F32))
                dmask[d, h] = jnp.where(e >= 0, jnp.exp(e * lg), 0.0) * kscale
                xi[d, h] = jnp.exp((p + 1.0) * lg)
                zeta[d, h] = jnp.exp(((C - 1) - p) * lg) * kscale
                gch[d, h] = jnp.exp(C * -jnp.exp(jnp.full((8, C), lr_ref[d, h], F32)))

    def chain(d, h, qk_ref, v_ref, g_ref, o_ref):
        q = qk_ref[:, h * RET_DK:(h + 1) * RET_DK]
        k = qk_ref[:, RET_QK + h * RET_DK:RET_QK + (h + 1) * RET_DK]
        v = v_ref[:, h * RET_DV:(h + 1) * RET_DV]
        s_old = state[d, h]
        kv = lax.dot_general((k.astype(F32) * zeta[d, h]).astype(BF16), v, (((0,), (0,)), ((), ())),
                             preferred_element_type=F32)
        state[d, h] = gch[d, h, 0:1, 0:1] * s_old + kv
        if o_ref is not None:
            inner = lax.dot_general(q, k, (((1,), (1,)), ((), ())), preferred_element_type=F32) * dmask[d, h]
            o = (jnp.dot(inner.astype(BF16), v, preferred_element_type=F32)
                 + jnp.dot((q.astype(F32) * xi[d, h]).astype(BF16), s_old.astype(BF16),
                           preferred_element_type=F32))
            on = o * lax.rsqrt(jnp.mean(o * o, axis=-1, keepdims=True) + EPS)
            gate = g_ref[:, h * RET_DV:(h + 1) * RET_DV].astype(F32)
            silu = gate * (1.0 / (1.0 + jnp.exp2(gate * -LOG2E)))
            o_ref[:, h * RET_DV:(h + 1) * RET_DV] = (silu * on).astype(o_ref.dtype)

    @pl.when(j == 0)
    def _():
        for h in range(RET_HEADS):
            chain(0, h, qk_cf, v_c, None, None)
            chain(1, h, qk_cb, v_c, None, None)

    @pl.when(j > 0)
    def _():
        for h in range(RET_HEADS):
            chain(0, h, qk_f, v_f, g_f, of_ref)
            chain(1, h, qk_b, v_b, g_b, ob_ref)


def _retention(log_rate, p_ctx, p_lat):
    assert 2 * RET_QK == RET_V
    G = p_lat.shape[0]
    C = RET_C
    n_lat = SEQ // C

    def fwd_chunk(j):
        return jnp.maximum(j - 1, 0)

    def bwd_chunk(j):
        return n_lat - 1 - jnp.maximum(j - 1, 0)

    def blk(chunk, col):
        return pl.BlockSpec((None, C, RET_V), lambda b, j: (b, chunk(j), col))

    ctx = lambda j: 0
    in_specs = [
        pl.BlockSpec(memory_space=pltpu.SMEM),
        blk(ctx, 0), blk(ctx, 1), blk(ctx, 2),
        blk(fwd_chunk, 0), blk(fwd_chunk, 2), blk(fwd_chunk, 3),
        blk(bwd_chunk, 1), blk(bwd_chunk, 2), blk(bwd_chunk, 4),
    ]
    out = jax.ShapeDtypeStruct((G, SEQ, RET_V), BF16)
    per_chain = lambda *tail: pltpu.VMEM((2, RET_HEADS) + tail, F32)
    return pl.pallas_call(
        _ret_kernel,
        out_shape=(out, out),
        grid=(G, 1 + n_lat),
        in_specs=in_specs,
        out_specs=(pl.BlockSpec((None, C, RET_V), lambda b, j: (b, fwd_chunk(j), 0)),
                   pl.BlockSpec((None, C, RET_V), lambda b, j: (b, bwd_chunk(j), 0))),
        scratch_shapes=[per_chain(RET_DK, RET_DV), per_chain(C, C), per_chain(C, C), per_chain(C, C),
                        per_chain(8, C)],
        compiler_params=_params(("parallel", "arbitrary"), 48),
        name="retention",
    )(log_rate, p_ctx, p_ctx, p_ctx, p_lat, p_lat, p_lat, p_lat, p_lat, p_lat)


def _ev_weight(w_in):
    i_q = EV_U
    i_k = i_q + ATT_WIDTH
    i_v = i_k + KV_WIDTH
    cols = [w_in[:, :i_q], w_in[:, i_q:i_k] * (HEAD_DIM ** -0.5 * LOG2E)]
    for base in (i_k, i_v):
        for g in range(ATT_KV_HEADS):
            part = w_in[:, base + g * HEAD_DIM: base + (g + 1) * HEAD_DIM]
            cols += [part, part]
    return jnp.concatenate(cols, axis=1).astype(BF16)


def kernel(x, c, ctx, c_ctx, ada_w, ada_b, norm_mix_g, norm_mlp_g, mlp_w1, mlp_w2, ev_w_in, ev_w_out, hy_conv_w, hy_conv_b, hy_w1, hy_b1, hy_w2, hy_b2, hy_w3, hy_freq, hy_decay, hy_bias, attn_sink, od_w_in, od_w_out, ret_log_rate, final_g):
    D = D_MODEL
    cvec = jnp.concatenate([c, c_ctx[None, :], jnp.zeros((8 - BATCH - 1, D), F32)], axis=0)
    mod = _ada(cvec, ada_w, ada_b)

    m1, m1i, f2, g2 = (jnp.asarray(a).astype(BF16) for a in _fft_mats())
    cfwd, cinv = (jnp.asarray(a).astype(BF16) for a in _ctx_fft_mats())
    fg = final_g.reshape(1, D)

    gmix = norm_mix_g[0].reshape(1, D)
    gmlp = norm_mlp_g[0].reshape(1, D)
    w_in = _ev_weight(ev_w_in[0])
    conv = (hy_conv_w[0], hy_conv_b[0].reshape(1, EV_U), hy_bias[0].reshape(1, HY_WIDTH))
    z_l, x0_l, zb_l, qkv_l = _inproj0(x, gmix, mod, w_in, *conv, 0, is_ctx=False, tm=512)
    z_c, x0_c, zb_c, qkv_c = _inproj0(ctx, gmix, mod, w_in, *conv, 0, is_ctx=True, tm=CTX_LEN)

    filt = (hy_w1[0], hy_b1[0], hy_w2[0], hy_b2[0], hy_w3[0], hy_freq[0], hy_decay[0])
    h_l, ss_l = _hyena_filter(*filt, L=SEQ)
    h_c, ss_c = _hyena_filter(*filt, L=CTX_LEN)
    hspec = _filter_fft(h_l, ss_l, m1, f2)
    y_l = _fft_conv(z_l, hspec, m1, f2, g2, m1i)
    y_c = _ctx_conv(z_c, h_c, ss_c, cfwd, cinv)

    sink = attn_sink[0]
    att_l = _attention(sink, qkv_l, qkv_c, local=True)
    att_c = _attention(sink, qkv_l, qkv_c, local=False)

    wo = ev_w_out[0].astype(BF16)
    w1 = mlp_w1[0].astype(BF16)
    w2 = mlp_w2[0].astype(BF16)
    x1 = _mix_mlp(0, x, (x0_l, y_l, zb_l, att_l), wo, mod, gmlp, w1, w2, fg, 0,
                  is_ctx=False, tm=512, fc=512, final_norm=False)
    ctx1 = _mix_mlp(0, ctx, (x0_c, y_c, zb_c, att_c), wo, mod, gmlp, w1, w2, fg, 0,
                    is_ctx=True, tm=CTX_LEN, fc=512, final_norm=False)

    gmix = norm_mix_g[1].reshape(1, D)
    gmlp = norm_mlp_g[1].reshape(1, D)
    w_in = od_w_in[0].astype(BF16)
    rope_ctx, rope_lat = _rope1d_tables()
    p_l = _inproj1(x1, gmix, mod, w_in, rope_lat, 1, is_ctx=False, tm=512, tn=512)
    p_c = _inproj1(ctx1, gmix, mod, w_in, rope_ctx, 1, is_ctx=True, tm=CTX_LEN, tn=512)
    o_fwd, o_bwd = _retention(ret_log_rate[0], p_c, p_l)

    wo = od_w_out[0].astype(BF16)
    w1 = mlp_w1[1].astype(BF16)
    w2 = mlp_w2[1].astype(BF16)
    return _mix_mlp(1, x1, (o_fwd, o_bwd), wo, mod, gmlp, w1, w2, fg, 1,
                    is_ctx=False, tm=512, fc=512, final_norm=True)
```

```python
import functools
import math

import numpy as np
import jax
import jax.numpy as jnp
from jax import lax
from jax.experimental import pallas as pl
from jax.experimental.pallas import tpu as pltpu

F32 = jnp.float32
BF16 = jnp.bfloat16
HIGHEST = lax.Precision.HIGHEST

D_MODEL = 1024
BATCH = 2
SEQ = 8192
DEPTH = 2
GRID_W = 64
CTX_LEN = 256
EPS = 1e-6
NEG_INF = -1e30
N_MOD = 6
D_FF = 4 * D_MODEL
ROPE_BASE = 10000.0

HY_WIDTH = D_MODEL // 2
HY_EMB = 33
HY_BANDS = (HY_EMB - 1) // 2
HY_HIDDEN = 64

ATT_HEADS = 8
ATT_KV_HEADS = 2
ATT_GROUP = ATT_HEADS // ATT_KV_HEADS
HEAD_DIM = 64
ATT_WIDTH = ATT_HEADS * HEAD_DIM
KV_WIDTH = ATT_KV_HEADS * HEAD_DIM
BLOCK = 128

RET_HEADS = 4
RET_DK = D_MODEL // RET_HEADS
RET_DV = 2 * RET_DK
RET_QK = RET_HEADS * RET_DK
RET_V = RET_HEADS * RET_DV
OD_IN = 2 * RET_QK + 3 * RET_V

LOG2E = 1.4426950408889634
LANES = 128
MIB = 1024 * 1024

EV_U = 3 * HY_WIDTH
EV_QKV = ATT_WIDTH + 4 * LANES
EV_COLS = EV_U + EV_QKV

FFT_N = 2 * SEQ
FFT_R = 128
FFT_PITCH = FFT_R + 8
CONV_ROWS = 64
CONV_PAD = 8
SUB_ROWS = 256
FFT_UNROLL = 16
RET_C = 256


def _params(sem, vmem_mib):
    return pltpu.CompilerParams(dimension_semantics=sem, vmem_limit_bytes=vmem_mib * MIB)


@functools.lru_cache(maxsize=None)
def _rope2d_tables():
    quarter = HEAD_DIM // 4
    inv = ROPE_BASE ** (-np.arange(quarter, dtype=np.float64) / quarter)
    t = np.arange(SEQ)
    pos = np.stack([t // GRID_W, t % GRID_W], axis=1).astype(np.float64)
    lane = np.arange(HEAD_DIM)
    half = lane // (HEAD_DIM // 2)
    e = lane % (HEAD_DIM // 2)
    ang = pos[:, half] * inv[e % quarter][None, :]
    sign = np.where(e < quarter, -1.0, 1.0)[None, :]
    cos = np.tile(np.cos(ang), (1, 2)).astype(np.float32)
    sin = np.tile(np.sin(ang) * sign, (1, 2)).astype(np.float32)
    return cos, sin


@functools.lru_cache(maxsize=None)
def _rope1d_tables():
    n = RET_DK // 2
    inv = ROPE_BASE ** (-np.linspace(0.0, 1.0, n))
    pos = np.arange(CTX_LEN + SEQ, dtype=np.float64)
    ang = pos[:, None] * inv[None, :]
    cos, sin = np.cos(ang), np.sin(ang)

    def rev(a):
        return a.reshape(-1, RET_C, n)[:, ::-1].reshape(-1, n)

    cos2 = np.stack([cos, rev(cos)]).astype(np.float32)
    sin2 = np.stack([sin, rev(sin)]).astype(np.float32)
    return cos2, sin2


@functools.lru_cache(maxsize=None)
def _fft_mats():
    N, R = FFT_N, FFT_R
    k1 = np.arange(R)
    n2 = np.arange(R)[:, None, None]
    n1 = np.arange(R // 2)[None, None, :]
    n = np.concatenate([n2 + R * n1, (N - ((R - n2) % R + R * n1)) % N], axis=2)
    idx = (k1[None, :, None] * n) % N
    ang = 2.0 * np.pi * idx / N
    c, s = np.cos(ang), np.sin(ang)
    m1 = np.concatenate([c, -s], axis=1).astype(np.float32)
    m1i = np.concatenate([c.transpose(0, 2, 1), -s.transpose(0, 2, 1)], axis=2)
    m1i = m1i[:, :R // 2].astype(np.float32)
    a2 = 2.0 * np.pi * ((np.arange(R)[:, None] * np.arange(R)[None, :]) % R) / R
    fr, fi = np.cos(a2), -np.sin(a2)
    f2 = np.block([[fr, -fi], [fi, fr]]).astype(np.float32)
    g2 = (np.block([[fr, fi], [-fi, fr]]) / N).astype(np.float32)
    return m1, m1i, f2, g2


@functools.lru_cache(maxsize=None)
def _ctx_fft_mats():
    N = 2 * CTX_LEN
    k = np.arange(N)[:, None]
    n = np.arange(N)[None, :]
    ang = 2.0 * np.pi * ((k * n) % N) / N
    c, s = np.cos(ang), np.sin(ang)
    fwd = np.concatenate([c, -s], axis=0).astype(np.float32)
    inv = (np.concatenate([c, -s], axis=1)[:CTX_LEN] / N).astype(np.float32)
    src = np.concatenate([np.arange(CTX_LEN), (N - np.arange(CTX_LEN)) % N])
    return fwd, inv, fwd[:, src]


def _ada_kernel(c_ref, w_ref, b_ref, o_ref):
    c = c_ref[...]
    a = c * (1.0 / (1.0 + jnp.exp(-c)))
    o_ref[...] = _dot3(a, w_ref[...]) + b_ref[...]


def _ada(cvec, ada_w, ada_b):
    tn = 1536
    return pl.pallas_call(
        _ada_kernel,
        out_shape=jax.ShapeDtypeStruct((DEPTH, 8, N_MOD * D_MODEL), F32),
        grid=(DEPTH, N_MOD * D_MODEL // tn),
        in_specs=[
            pl.BlockSpec((8, D_MODEL), lambda i, j: (0, 0)),
            pl.BlockSpec((None, D_MODEL, tn), lambda i, j: (i, 0, j)),
            pl.BlockSpec((None, 1, tn), lambda i, j: (i, 0, j)),
        ],
        out_specs=pl.BlockSpec((None, 8, tn), lambda i, j: (i, 0, j)),
        compiler_params=_params(("parallel", "parallel"), 40),
        name="ada_mod",
    )(cvec, ada_w, ada_b.reshape(DEPTH, 1, N_MOD * D_MODEL))


def _mod_row(ref, row):
    if isinstance(row, int):
        return ref[row:row + 1, :]
    return ref[pl.ds(row, 1), :]


def _norm_mod(x, g, shift, scale):
    y = x * lax.rsqrt(jnp.mean(x * x, axis=-1, keepdims=True) + EPS)
    return (y * g) * (1.0 + scale) + shift


def _rope_tile(x, cos, sin_signed):
    lane = lax.broadcasted_iota(jnp.int32, x.shape, 1)
    first = (lane % 32) < 16
    partner = jnp.where(first, pltpu.roll(x, LANES - 16, axis=1), pltpu.roll(x, 16, axis=1))
    return x * cos + partner * sin_signed


def _project0(x_ref, g, sh, sc, w_ref, cos_ref, sin_ref, u_ref, qkv_ref, rope):
    tn = 512
    n_rot = (ATT_WIDTH + 2 * LANES) // LANES
    sub = min(x_ref.shape[0], SUB_ROWS)
    for s in range(x_ref.shape[0] // sub):
        rows = slice(s * sub, (s + 1) * sub)
        urows = slice(CONV_PAD + s * sub, CONV_PAD + (s + 1) * sub)
        h = _norm_mod(x_ref[rows, :], g, sh, sc).astype(BF16)
        for j in range(EV_U // tn):
            u_ref[urows, j * tn:(j + 1) * tn] = jnp.dot(h, w_ref[:, j * tn:(j + 1) * tn],
                                                        preferred_element_type=F32)
        for j in range(EV_QKV // tn):
            y = jnp.dot(h, w_ref[:, EV_U + j * tn:EV_U + (j + 1) * tn], preferred_element_type=F32)
            for t in range(tn // LANES):
                yt = y[:, t * LANES:(t + 1) * LANES]
                if rope and j * (tn // LANES) + t < n_rot:
                    yt = _rope_tile(yt, cos_ref[rows, :], sin_ref[rows, :])
                qkv_ref[rows, j * tn + t * LANES:j * tn + (t + 1) * LANES] = yt.astype(BF16)


def _short_conv_gate(u_ref, cw_ref, cb_ref, bias_ref, z_ref, x0_ref, zb_ref):
    tm = u_ref.shape[0] - 2 * CONV_PAD
    rc = CONV_ROWS

    def conv(r0, c0):
        cols = slice(c0, c0 + LANES)
        taps = [u_ref[CONV_PAD - 1 + k + r0:CONV_PAD - 1 + k + r0 + rc, cols] for k in range(3)]
        return (taps[0] * cw_ref[0:1, cols] + taps[1] * cw_ref[1:2, cols] + taps[2] * cw_ref[2:3, cols]
                + cb_ref[:, cols])

    for r0 in range(0, tm, rc):
        rows = slice(r0, r0 + rc)
        for c0 in range(0, HY_WIDTH, LANES):
            cols = slice(c0, c0 + LANES)
            x0 = conv(r0, c0)
            z = conv(r0, 2 * HY_WIDTH + c0) * conv(r0, HY_WIDTH + c0)
            z_ref[rows, cols] = z
            x0_ref[rows, cols] = x0
            zb_ref[rows, cols] = x0 * (z * bias_ref[:, cols])


def _inproj0_kernel(x_ref, g_ref, sh_ref, sc_ref, w_ref, cos_ref, sin_ref, cw_ref, cb_ref, bias_ref,
                    z_ref, x0_ref, zb_ref, qkv_ref, u_even, u_odd, *, row, rope, n_tiles):
    r = pl.program_id(0) if row is None else row
    i = pl.program_id(1)
    g, sh, sc = g_ref[...], _mod_row(sh_ref, r), _mod_row(sc_ref, r)
    conv_refs = (cw_ref, cb_ref, bias_ref, z_ref, x0_ref, zb_ref)
    zero_row = jnp.zeros((1, EV_U), F32)
    tm = x_ref.shape[0]
    above, first, last, below = CONV_PAD - 1, CONV_PAD, CONV_PAD + tm - 1, CONV_PAD + tm

    if n_tiles == 1:
        u_even[above:first, :] = zero_row
        u_even[below:below + 1, :] = zero_row
        _project0(x_ref, g, sh, sc, w_ref, cos_ref, sin_ref, u_even, qkv_ref, rope)
        _short_conv_gate(u_even, *conv_refs)
        return

    @pl.when(i == 0)
    def _():
        u_even[...] = jnp.zeros_like(u_even)
        u_odd[...] = jnp.zeros_like(u_odd)

    def step(u_new, u_old):
        j = i - 1
        u_old[above:first, :] = jnp.where(j > 0, u_new[last:last + 1, :], zero_row)
        _project0(x_ref, g, sh, sc, w_ref, cos_ref, sin_ref, u_new, qkv_ref, rope)
        u_old[below:below + 1, :] = jnp.where(j < n_tiles - 1, u_new[first:first + 1, :], zero_row)
        _short_conv_gate(u_old, *conv_refs)

    @pl.when(i % 2 == 0)
    def _():
        step(u_even, u_odd)

    @pl.when(i % 2 == 1)
    def _():
        step(u_odd, u_even)


def _inproj0(x, g, mod, w, conv_w, conv_b, bias, layer, *, is_ctx, tm):
    G, R, _ = x.shape
    n_tiles = R // tm
    n_steps = n_tiles + (1 if n_tiles > 1 else 0)
    cos, sin = _rope2d_tables()
    cos, sin = jnp.asarray(cos), jnp.asarray(sin)
    proj_tile = lambda i: jnp.minimum(i, n_tiles - 1)
    conv_tile = (lambda i: jnp.maximum(i - 1, 0)) if n_tiles > 1 else (lambda i: i)
    if is_ctx:
        tab = pl.BlockSpec((tm, LANES), lambda b, i: (0, 0))
    else:
        tab = pl.BlockSpec((tm, LANES), lambda b, i: (proj_tile(i), 0))
    const = lambda a: pl.BlockSpec(a.shape, lambda b, i: (0, 0))
    hy_out = jax.ShapeDtypeStruct((G, R, HY_WIDTH), F32)
    hy_spec = pl.BlockSpec((None, tm, HY_WIDTH), lambda b, i: (b, conv_tile(i), 0))
    kern = functools.partial(_inproj0_kernel, row=2 if is_ctx else None, rope=not is_ctx, n_tiles=n_tiles)
    return pl.pallas_call(
        kern,
        out_shape=(hy_out, hy_out, hy_out, jax.ShapeDtypeStruct((G, R, EV_QKV), BF16)),
        grid=(G, n_steps),
        in_specs=[
            pl.BlockSpec((None, tm, D_MODEL), lambda b, i: (b, proj_tile(i), 0)),
            pl.BlockSpec((1, D_MODEL), lambda b, i: (0, 0)),
            pl.BlockSpec((None, 8, D_MODEL), lambda b, i: (layer, 0, 0)),
            pl.BlockSpec((None, 8, D_MODEL), lambda b, i: (layer, 0, 1)),
            pl.BlockSpec((D_MODEL, EV_COLS), lambda b, i: (0, 0), pipeline_mode=pl.Buffered(1)),
            tab, tab, const(conv_w), const(conv_b), const(bias),
        ],
        out_specs=(hy_spec, hy_spec, hy_spec,
                   pl.BlockSpec((None, tm, EV_QKV), lambda b, i: (b, proj_tile(i), 0))),
        scratch_shapes=[pltpu.VMEM((tm + 2 * CONV_PAD, EV_U), F32)] * 2,
        compiler_params=_params(("parallel", "arbitrary"), 48),
        name="inproj0_ctx" if is_ctx else "inproj0",
    )(x, g, mod, mod, w, cos, sin, conv_w, conv_b, bias)


def _inproj1_kernel(x_ref, g_ref, sh_ref, sc_ref, w_ref, o_ref, *, row, tn):
    r = pl.program_id(0) if row is None else row
    g, sh, sc = g_ref[...], _mod_row(sh_ref, r), _mod_row(sc_ref, r)
    sub = min(x_ref.shape[0], SUB_ROWS)
    for s in range(x_ref.shape[0] // sub):
        rows = slice(s * sub, (s + 1) * sub)
        h = _norm_mod(x_ref[rows, :], g, sh, sc).astype(BF16)
        for j in range(w_ref.shape[1] // tn):
            cols = slice(j * tn, (j + 1) * tn)
            o_ref[rows, cols] = jnp.dot(h, w_ref[:, cols], preferred_element_type=F32).astype(o_ref.dtype)


def _inproj1(x, g, mod, w, layer, *, is_ctx, tm, tn):
    G, R, _ = x.shape
    N = w.shape[1]
    kern = functools.partial(_inproj1_kernel, row=2 if is_ctx else None, tn=tn)
    return pl.pallas_call(
        kern,
        out_shape=jax.ShapeDtypeStruct((G, R, N), BF16),
        grid=(G, R // tm),
        in_specs=[
            pl.BlockSpec((None, tm, D_MODEL), lambda b, i: (b, i, 0)),
            pl.BlockSpec((1, D_MODEL), lambda b, i: (0, 0)),
            pl.BlockSpec((None, 8, D_MODEL), lambda b, i: (layer, 0, 0)),
            pl.BlockSpec((None, 8, D_MODEL), lambda b, i: (layer, 0, 1)),
            pl.BlockSpec((D_MODEL, N), lambda b, i: (0, 0), pipeline_mode=pl.Buffered(1)),
        ],
        out_specs=pl.BlockSpec((None, tm, N), lambda b, i: (b, i, 0)),
        compiler_params=_params(("parallel", "parallel"), 48),
        name="inproj1_ctx" if is_ctx else "inproj1",
    )(x, g, mod, mod, w)


def _split_bf16(a):
    hi = a.astype(BF16)
    return hi, (a - hi.astype(F32)).astype(BF16)


def _dot3_split(a_hi, a_lo, b):
    b_hi, b_lo = _split_bf16(b)
    dot = functools.partial(jnp.dot, preferred_element_type=F32)
    return dot(a_hi, b_hi) + (dot(a_lo, b_hi) + dot(a_hi, b_lo))


def _dot3(a, b):
    return _dot3_split(*_split_bf16(a), b)


def _filter_kernel(w1t_ref, b1_ref, w2t_ref, b2_ref, fr_ref, w3_ref, dec_ref, bands_ref, h_ref, ss_ref, *, L, tm):
    i = pl.program_id(0)
    t_row = (lax.broadcasted_iota(jnp.int32, (1, tm), 1) + i * tm).astype(F32) / L
    ang = (2.0 * math.pi * t_row) * bands_ref[...]
    t8 = jnp.where(lax.broadcasted_iota(jnp.int32, (8, tm), 0) == 0, t_row, 0.0)
    feat = jnp.concatenate([t8, jnp.cos(ang), -jnp.sin(ang)], axis=0)
    fr = fr_ref[...]
    hid = jnp.sin(fr * (jnp.dot(w1t_ref[...], feat, precision=HIGHEST, preferred_element_type=F32) + b1_ref[...]))
    hid = jnp.sin(fr * (jnp.dot(w2t_ref[...], hid, precision=HIGHEST, preferred_element_type=F32) + b2_ref[...]))
    hid_hi, hid_lo = _split_bf16(hid.T)
    m_col = lax.broadcasted_iota(jnp.int32, (tm, 1), 0) + i * tm
    t_col = m_col.astype(F32) / L
    energy = jnp.zeros((1, HY_WIDTH), F32)
    for side in range(2):
        h = _dot3_split(hid_hi, hid_lo, w3_ref[side]) * jnp.exp(-t_col * jnp.abs(dec_ref[side]))
        if side == 1:
            h = jnp.where(m_col == 0, 0.0, h)
        h_ref[side] = h
        energy = energy + jnp.sum(h * h, axis=0, keepdims=True)

    @pl.when(i == 0)
    def _():
        ss_ref[...] = jnp.zeros_like(ss_ref)

    ss_ref[...] += jnp.broadcast_to(energy, ss_ref.shape)


def _hyena_filter(w1, b1, w2, b2, w3, freq, decay, *, L):
    tm = min(L, 1024)
    kern = functools.partial(_filter_kernel, L=L, tm=tm)
    w1t = jnp.concatenate([w1[0:1], jnp.zeros((7, HY_HIDDEN), F32), w1[1:]], axis=0).T
    col = lambda a: a.reshape(HY_HIDDEN, 1)
    w3s = jnp.stack([w3[:, :HY_WIDTH], w3[:, HY_WIDTH:]])
    bands = jnp.asarray(np.linspace(1e-4, HY_BANDS - 1, HY_BANDS).astype(np.float32).reshape(HY_BANDS, 1))
    args = (w1t, col(b1), w2.T, col(b2), col(freq), w3s, decay.reshape(2, 1, HY_WIDTH), bands)
    full = lambda a: pl.BlockSpec(a.shape, lambda i: (0,) * a.ndim)
    h, ss = pl.pallas_call(
        kern,
        out_shape=(jax.ShapeDtypeStruct((2, L, HY_WIDTH), F32), jax.ShapeDtypeStruct((8, HY_WIDTH), F32)),
        grid=(L // tm,),
        in_specs=[full(a) for a in args],
        out_specs=(pl.BlockSpec((2, tm, HY_WIDTH), lambda i: (0, i, 0)),
                   pl.BlockSpec((8, HY_WIDTH), lambda i: (0, 0))),
        compiler_params=_params(("arbitrary",), 48),
        name="hyena_filter_%d" % L,
    )(*args)
    return h.reshape(2 * L, HY_WIDTH), ss


def _stack_bf16(re, im):
    return jnp.concatenate([re, im], axis=0).astype(BF16)


def _filter_fft_kernel(h_ref, ss_ref, m1_ref, f2_ref, o_ref, a_scr):
    R, P = FFT_R, FFT_PITCH
    inv_norm = lax.rsqrt(ss_ref[0:1, :])

    def stage1(n2, carry):
        neg = jnp.where(n2 == 0, 0, R - n2)
        xs = jnp.concatenate([h_ref[pl.ds(n2, R // 2, stride=R), :],
                              h_ref[pl.ds(SEQ + neg, R // 2, stride=R), :]], axis=0)
        xs = (xs * inv_norm).astype(BF16)
        a = jnp.dot(m1_ref[n2], xs, preferred_element_type=F32)
        base = pl.multiple_of(n2 * P, 8)
        a_scr[0, pl.ds(base, R), :] = a[:R]
        a_scr[1, pl.ds(base, R), :] = a[R:]
        return carry

    lax.fori_loop(0, R, stage1, 0, unroll=FFT_UNROLL)

    ct = a_scr.shape[-1]

    def stage2(pair, carry):
        k1s = (2 * pair, 2 * pair + 1)
        rhs = jnp.concatenate([_stack_bf16(a_scr[0, pl.ds(k1, R, stride=P), :],
                                           a_scr[1, pl.ds(k1, R, stride=P), :]) for k1 in k1s], axis=1)
        x = jnp.dot(f2_ref[...], rhs, preferred_element_type=F32)
        for d, k1 in enumerate(k1s):
            base = pl.multiple_of(k1 * R, R)
            o_ref[0, pl.ds(base, R), :] = x[:R, d * ct:(d + 1) * ct].astype(BF16)
            o_ref[1, pl.ds(base, R), :] = x[R:, d * ct:(d + 1) * ct].astype(BF16)
        return carry

    lax.fori_loop(0, R // 2, stage2, 0, unroll=FFT_UNROLL // 2)


def _filter_fft(h_circ, ss, m1, f2):
    ct = LANES
    once = pl.Buffered(1)
    return pl.pallas_call(
        _filter_fft_kernel,
        out_shape=jax.ShapeDtypeStruct((2, FFT_N, HY_WIDTH), BF16),
        grid=(HY_WIDTH // ct,),
        in_specs=[
            pl.BlockSpec((FFT_N, ct), lambda c: (0, c), pipeline_mode=once),
            pl.BlockSpec((8, ct), lambda c: (0, c)),
            pl.BlockSpec(m1.shape, lambda c: (0, 0, 0), pipeline_mode=once),
            pl.BlockSpec(f2.shape, lambda c: (0, 0), pipeline_mode=once),
        ],
        out_specs=pl.BlockSpec((2, FFT_N, ct), lambda c: (0, 0, c)),
        scratch_shapes=[pltpu.VMEM((2, FFT_R * FFT_PITCH, ct), F32)],
        compiler_params=_params(("parallel",), 58),
        name="hyena_filter_fft",
    )(h_circ, ss, m1, f2)


def _fft_conv_kernel(z_ref, hs_ref, m1_ref, f2_ref, g2_ref, m1i_ref, y_ref, a_scr):
    R, P = FFT_R, FFT_PITCH
    ct = a_scr.shape[-1]

    def stage1(n2, carry):
        rows = pl.ds(n2, R // 2, stride=R)
        xs = jnp.concatenate([z_ref[0, rows, :], z_ref[1, rows, :]], axis=1).astype(BF16)
        t = jnp.dot(m1_ref[n2, :, 0:R // 2], xs, preferred_element_type=F32)
        base = pl.multiple_of(n2 * P, 8)
        a_scr[0, pl.ds(base, R), :] = t[:R, :ct] - t[R:, ct:]
        a_scr[1, pl.ds(base, R), :] = t[R:, :ct] + t[:R, ct:]
        return carry

    lax.fori_loop(0, R, stage1, 0, unroll=FFT_UNROLL)

    def stage2(pair, carry):
        k1s = (2 * pair, 2 * pair + 1)
        rhs = jnp.concatenate([_stack_bf16(a_scr[0, pl.ds(k1, R, stride=P), :],
                                           a_scr[1, pl.ds(k1, R, stride=P), :]) for k1 in k1s], axis=1)
        x = jnp.dot(f2_ref[...], rhs, preferred_element_type=F32)
        prods = []
        for d, k1 in enumerate(k1s):
            base = pl.multiple_of(k1 * R, R)
            hr = hs_ref[0, pl.ds(base, R), :].astype(F32)
            hi = hs_ref[1, pl.ds(base, R), :].astype(F32)
            xr, xi = x[:R, d * ct:(d + 1) * ct], x[R:, d * ct:(d + 1) * ct]
            prods.append(_stack_bf16(xr * hr - xi * hi, xr * hi + xi * hr))
        bq = jnp.dot(g2_ref[...], jnp.concatenate(prods, axis=1), preferred_element_type=F32)
        for d, k1 in enumerate(k1s):
            a_scr[0, pl.ds(k1, R, stride=P), :] = bq[:R, d * ct:(d + 1) * ct]
            a_scr[1, pl.ds(k1, R, stride=P), :] = bq[R:, d * ct:(d + 1) * ct]
        return carry

    lax.fori_loop(0, R // 2, stage2, 0, unroll=FFT_UNROLL // 2)

    def stage3(n2, carry):
        base = pl.multiple_of(n2 * P, 8)
        br = a_scr[0, pl.ds(base, R), :]
        bi = a_scr[1, pl.ds(base, R), :]
        rhs = jnp.concatenate([jnp.concatenate([br, bi], axis=1),
                               jnp.concatenate([bi, -br], axis=1)], axis=0).astype(BF16)
        y = jnp.dot(m1i_ref[n2], rhs, preferred_element_type=F32)
        rows = pl.ds(n2, R // 2, stride=R)
        y_ref[0, rows, :] = y[:, :ct]
        y_ref[1, rows, :] = y[:, ct:]
        return carry

    lax.fori_loop(0, R, stage3, 0, unroll=FFT_UNROLL)


def _fft_conv(z, hspec, m1, f2, g2, m1i):
    assert z.shape[0] == 2, "the batch pair is packed into one complex signal"
    ct = LANES
    once = pl.Buffered(1)
    return pl.pallas_call(
        _fft_conv_kernel,
        out_shape=jax.ShapeDtypeStruct((2, SEQ, HY_WIDTH), F32),
        grid=(HY_WIDTH // ct,),
        in_specs=[
            pl.BlockSpec((2, SEQ, ct), lambda c: (0, 0, c), pipeline_mode=once),
            pl.BlockSpec((2, FFT_N, ct), lambda c: (0, 0, c), pipeline_mode=once),
            pl.BlockSpec(m1.shape, lambda c: (0, 0, 0), pipeline_mode=once),
            pl.BlockSpec(f2.shape, lambda c: (0, 0), pipeline_mode=once),
            pl.BlockSpec(g2.shape, lambda c: (0, 0), pipeline_mode=once),
            pl.BlockSpec(m1i.shape, lambda c: (0, 0, 0), pipeline_mode=once),
        ],
        out_specs=pl.BlockSpec((2, SEQ, ct), lambda c: (0, 0, c), pipeline_mode=once),
        scratch_shapes=[pltpu.VMEM((2, FFT_R * FFT_PITCH, ct), F32)],
        compiler_params=_params(("arbitrary",), 58),
        name="hyena_fft_conv",
    )(z, hspec, m1, f2, g2, m1i)


def _ctx_conv_kernel(z_ref, h_ref, ss_ref, fwd_ref, inv_ref, fwd_h_ref, y_ref):
    N = 2 * CTX_LEN
    hn = (h_ref[...] * lax.rsqrt(ss_ref[0:1, :])).astype(BF16)
    hs = jnp.dot(fwd_h_ref[...], hn, preferred_element_type=F32)
    zs = jnp.dot(fwd_ref[:, :CTX_LEN], z_ref[...].astype(BF16), preferred_element_type=F32)
    hr, hi, zr, zi = hs[:N], hs[N:], zs[:N], zs[N:]
    ys = _stack_bf16(zr * hr - zi * hi, zr * hi + zi * hr)
    y_ref[...] = jnp.dot(inv_ref[...], ys, preferred_element_type=F32)


def _ctx_conv(z, h_circ, ss, fwd, inv, fwd_h):
    G = z.shape[0]
    full = lambda a: pl.BlockSpec(a.shape, lambda b: (0,) * a.ndim)
    return pl.pallas_call(
        _ctx_conv_kernel,
        out_shape=jax.ShapeDtypeStruct((G, CTX_LEN, HY_WIDTH), F32),
        grid=(G,),
        in_specs=[pl.BlockSpec((None, CTX_LEN, HY_WIDTH), lambda b: (b, 0, 0)),
                  full(h_circ), full(ss), full(fwd), full(inv), full(fwd_h)],
        out_specs=pl.BlockSpec((None, CTX_LEN, HY_WIDTH), lambda b: (b, 0, 0)),
        compiler_params=_params(("parallel",), 48),
        name="hyena_ctx_conv",
    )(z, h_circ, ss, fwd, inv, fwd_h)


def _kv_group(refs, g):
    cols = slice(g * LANES, (g + 1) * LANES)
    return refs[0][:, cols] if len(refs) == 1 else jnp.concatenate([r[:, cols] for r in refs], axis=0)


def _attn_logits(q_ref, k_refs, g, masks):
    low = lax.broadcasted_iota(jnp.int32, (BLOCK, LANES), 1) < HEAD_DIM
    zero = jnp.zeros((), BF16)
    parts = []
    for hh in range(ATT_GROUP):
        h = g * ATT_GROUP + hh
        tile = q_ref[:, (h // 2) * LANES:(h // 2 + 1) * LANES]
        parts.append(jnp.where(low if h % 2 == 0 else ~low, tile, zero))
    s = lax.dot_general(jnp.concatenate(parts, axis=0), _kv_group(k_refs, g), (((1,), (1,)), ((), ())),
                        preferred_element_type=F32)
    if masks is not None:
        ok_prev, ok_next = masks
        s = jnp.concatenate([jnp.where(ok_prev, s[:, :BLOCK], NEG_INF), s[:, BLOCK:2 * BLOCK],
                             jnp.where(ok_next, s[:, 2 * BLOCK:3 * BLOCK], NEG_INF), s[:, 3 * BLOCK:]], axis=1)
    return s


def _attn_output(s, sink_ref, v_refs, g, o_ref):
    low = lax.broadcasted_iota(jnp.int32, (BLOCK, LANES), 1) < HEAD_DIM
    low4 = jnp.concatenate([low] * ATT_GROUP, axis=0)
    v = _kv_group(v_refs, g)
    v_aug = jnp.where(lax.broadcasted_iota(jnp.int32, v.shape, 1) < HEAD_DIM, v, jnp.ones((), BF16))
    sink = jnp.concatenate(
        [jnp.full((BLOCK, 1), sink_ref[g * ATT_GROUP + hh] * LOG2E, F32) for hh in range(ATT_GROUP)], axis=0)
    m = jnp.maximum(jnp.max(s, axis=-1, keepdims=True), sink)
    e = jnp.exp2(s - m).astype(BF16)
    o = jnp.dot(e, v_aug, preferred_element_type=F32) + jnp.where(low4, 0.0, jnp.exp2(sink - m))
    swapped = pltpu.roll(o, HEAD_DIM, axis=1)
    for pair in range(ATT_GROUP // 2):
        ev = slice((2 * pair) * BLOCK, (2 * pair + 1) * BLOCK)
        od = slice((2 * pair + 1) * BLOCK, (2 * pair + 2) * BLOCK)
        even = o[ev] / swapped[ev]
        odd = swapped[od] / o[od]
        t = g * (ATT_GROUP // 2) + pair
        o_ref[:, t * LANES:(t + 1) * LANES] = jnp.where(low, even, odd).astype(o_ref.dtype)


def _attn_ctx_kernel(sink_ref, q_ref, kx_ref, vx_ref, o_ref):
    for g in range(ATT_KV_HEADS):
        _attn_output(_attn_logits(q_ref, (kx_ref,), g, None), sink_ref, (vx_ref,), g, o_ref)


def _attn_local_kernel(sink_ref, q_ref, kp_ref, kc_ref, kn_ref, kx_ref, vp_ref, vc_ref, vn_ref, vx_ref,
                       o_ref, s_even, s_odd, *, n_blocks):
    n = pl.program_id(1)

    @pl.when(n == 0)
    def _():
        s_odd[...] = jnp.zeros_like(s_odd)

    def step(s_new, s_prev):
        nq = jnp.minimum(n, n_blocks - 1)
        qi = lax.broadcasted_iota(jnp.int32, (ATT_GROUP * BLOCK, BLOCK), 0) % BLOCK
        ki = lax.broadcasted_iota(jnp.int32, (ATT_GROUP * BLOCK, BLOCK), 1)
        masks = ((ki >= qi) & (nq > 0), (ki <= qi) & (nq < n_blocks - 1))
        for g in range(ATT_KV_HEADS):
            s_new[g] = _attn_logits(q_ref, (kp_ref, kc_ref, kn_ref, kx_ref), g, masks)
        for g in range(ATT_KV_HEADS):
            _attn_output(s_prev[g], sink_ref, (vp_ref, vc_ref, vn_ref, vx_ref), g, o_ref)

    @pl.when(n % 2 == 0)
    def _():
        step(s_even, s_odd)

    @pl.when(n % 2 == 1)
    def _():
        step(s_odd, s_even)


def _attention(sink, qkv, qkv_ctx, *, local):
    src = qkv if local else qkv_ctx
    G, R, _ = src.shape
    nb = R // BLOCK
    kcol, vcol = ATT_WIDTH // 256, ATT_WIDTH // 256 + 1
    ctx_k = pl.BlockSpec((None, CTX_LEN, 256), lambda b, n: (b, 0, kcol))
    ctx_v = pl.BlockSpec((None, CTX_LEN, 256), lambda b, n: (b, 0, vcol))
    smem = pl.BlockSpec(memory_space=pltpu.SMEM)
    out_shape = jax.ShapeDtypeStruct((G, R, ATT_WIDTH), BF16)
    if not local:
        rows = pl.BlockSpec((None, BLOCK, ATT_WIDTH), lambda b, n: (b, n, 0))
        return pl.pallas_call(
            _attn_ctx_kernel, out_shape=out_shape, grid=(G, nb),
            in_specs=[smem, rows, ctx_k, ctx_v], out_specs=rows,
            compiler_params=_params(("parallel", "parallel"), 48), name="attn_ctx",
        )(sink, qkv_ctx, qkv_ctx, qkv_ctx)

    q_blk = lambda n: jnp.minimum(n, nb - 1)
    o_blk = lambda n: jnp.maximum(n - 1, 0)

    def nbr(col, blk, off):
        return pl.BlockSpec((None, BLOCK, 256), lambda b, n: (b, jnp.clip(blk(n) + off, 0, nb - 1), col))

    in_specs = [smem, pl.BlockSpec((None, BLOCK, ATT_WIDTH), lambda b, n: (b, q_blk(n), 0)),
                nbr(kcol, q_blk, -1), nbr(kcol, q_blk, 0), nbr(kcol, q_blk, 1), ctx_k,
                nbr(vcol, o_blk, -1), nbr(vcol, o_blk, 0), nbr(vcol, o_blk, 1), ctx_v]
    logits_scratch = pltpu.VMEM((ATT_KV_HEADS, ATT_GROUP * BLOCK, 3 * BLOCK + CTX_LEN), F32)
    return pl.pallas_call(
        functools.partial(_attn_local_kernel, n_blocks=nb),
        out_shape=out_shape,
        grid=(G, nb + 1),
        in_specs=in_specs,
        out_specs=pl.BlockSpec((None, BLOCK, ATT_WIDTH), lambda b, n: (b, o_blk(n), 0)),
        scratch_shapes=[logits_scratch, logits_scratch],
        compiler_params=_params(("parallel", "arbitrary"), 48),
        name="attn_local",
    )(sink, qkv, qkv, qkv, qkv, qkv_ctx, qkv, qkv, qkv, qkv_ctx)


def _mixer_out0(rows, x0_ref, y_ref, zb_ref, att_ref, wo_ref):
    hy = (x0_ref[rows, :] * y_ref[rows, :] + zb_ref[rows, :]).astype(BF16)
    return (jnp.dot(hy, wo_ref[:HY_WIDTH, :], preferred_element_type=F32)
            + jnp.dot(att_ref[rows, :], wo_ref[HY_WIDTH:, :], preferred_element_type=F32))


def _mixer_out1(rows, of_ref, ob_ref, wo_ref):
    a = (of_ref[rows, :].astype(F32) + ob_ref[rows, :].astype(F32)).astype(BF16)
    return jnp.dot(a, wo_ref[...], preferred_element_type=F32)


def _mix_kernel(*refs, mixer_out, n_mix, row, fc, final_norm):
    x_ref = refs[0]
    mix_refs = refs[1:2 + n_mix]
    gta_ref, gm_ref, shm_ref, scm_ref, gtm_ref, w1_ref, w2_ref, fg_ref, o_ref, a_scr = refs[2 + n_mix:]
    r = pl.program_id(0) if row is None else row
    gta, gtm = _mod_row(gta_ref, r), _mod_row(gtm_ref, r)
    gm, shm, scm = gm_ref[...], _mod_row(shm_ref, r), _mod_row(scm_ref, r)
    sub = min(x_ref.shape[0], SUB_ROWS)
    for s in range(x_ref.shape[0] // sub):
        rows = slice(s * sub, (s + 1) * sub)
        x1 = x_ref[rows, :] + gta * mixer_out(rows, *mix_refs)
        h = _norm_mod(x1, gm, shm, scm).astype(BF16)
        for c in range(D_FF // fc):
            a = jnp.maximum(jnp.dot(h, w1_ref[:, c * fc:(c + 1) * fc], preferred_element_type=F32), 0.0)
            a_scr[rows, c * fc:(c + 1) * fc] = (a * a).astype(BF16)
        out = x1 + gtm * jnp.dot(a_scr[rows, :], w2_ref[...], preferred_element_type=F32)
        if final_norm:
            out = (out * lax.rsqrt(jnp.mean(out * out, axis=-1, keepdims=True) + EPS)) * fg_ref[...]
        o_ref[rows, :] = out


def _mix_mlp(kind, x, mix_in, wo, mod, gm, w1, w2, fg, layer, *, is_ctx, tm, fc, final_norm):
    G, R, _ = x.shape
    row_spec = lambda w: pl.BlockSpec((None, tm, w), lambda b, i: (b, i, 0))
    modk = lambda k: pl.BlockSpec((None, 8, D_MODEL), lambda b, i: (layer, 0, k))
    vec = pl.BlockSpec((1, D_MODEL), lambda b, i: (0, 0))
    resident = lambda a: pl.BlockSpec(a.shape, lambda b, i: (0, 0), pipeline_mode=pl.Buffered(1))
    if kind == 0:
        mixer_out = _mixer_out0
        mix_specs = [row_spec(HY_WIDTH)] * 3 + [row_spec(ATT_WIDTH)]
    else:
        mixer_out = _mixer_out1
        mix_specs = [row_spec(RET_V)] * 2
    kern = functools.partial(_mix_kernel, mixer_out=mixer_out, n_mix=len(mix_in), row=2 if is_ctx else None,
                             fc=fc, final_norm=final_norm)
    return pl.pallas_call(
        kern,
        out_shape=jax.ShapeDtypeStruct((G, R, D_MODEL), F32),
        grid=(G, R // tm),
        in_specs=[row_spec(D_MODEL)] + mix_specs + [
            resident(wo), modk(2), vec, modk(3), modk(4), modk(5), resident(w1), resident(w2), vec,
        ],
        out_specs=row_spec(D_MODEL),
        scratch_shapes=[pltpu.VMEM((tm, D_FF), BF16)],
        compiler_params=_params(("parallel", "parallel"), 56),
        name="mix_mlp%d%s" % (kind, "_ctx" if is_ctx else ""),
    )(x, *mix_in, wo, mod, gm, mod, mod, mod, w1, w2, fg)


def _ret_kernel(lr_ref, qkv_c, qkv_f, g_f, qkv_b, g_b, cos_ref, sin_ref, of_ref, ob_ref,
                state, dmask, xi, zeta, gch):
    j = pl.program_id(1)
    C = RET_C
    kscale = RET_DK ** -0.5

    @pl.when(j == 0)
    def _():
        state[...] = jnp.zeros_like(state)
        row = lax.broadcasted_iota(jnp.int32, (C, C), 0).astype(F32)
        col = lax.broadcasted_iota(jnp.int32, (C, C), 1).astype(F32)
        for d in range(2):
            e = row - col if d == 0 else col - row
            p = row if d == 0 else (C - 1) - row
            for h in range(RET_HEADS):
                lg = -jnp.exp(jnp.full((C, C), lr_ref[d, h], F32))
                dmask[d, h] = jnp.where(e >= 0, jnp.exp(e * lg), 0.0) * kscale
                xi[d, h] = jnp.exp((p + 1.0) * lg)
                zeta[d, h] = jnp.exp(((C - 1) - p) * lg) * kscale
                gch[d, h] = jnp.exp(C * -jnp.exp(jnp.full((8, C), lr_ref[d, h], F32)))

    half = RET_DK // 2

    def chain(d, h, src_ref, g_ref, o_ref):
        cos, sin = cos_ref[d], sin_ref[d]

        def rot(c0):
            t1 = src_ref[:, c0:c0 + half].astype(F32)
            t2 = src_ref[:, c0 + half:c0 + RET_DK].astype(F32)
            return jnp.concatenate([t1 * cos - t2 * sin, t2 * cos + t1 * sin], axis=1)

        q = rot(h * RET_DK)
        k = rot(RET_QK + h * RET_DK)
        v = src_ref[:, 2 * RET_QK + h * RET_DV:2 * RET_QK + (h + 1) * RET_DV]
        inner = lax.dot_general(q.astype(BF16), k.astype(BF16), (((1,), (1,)), ((), ())),
                                preferred_element_type=F32) * dmask[d, h]
        s_old = state[d, h]
        kv = lax.dot_general((k * zeta[d, h]).astype(BF16), v, (((0,), (0,)), ((), ())),
                             preferred_element_type=F32)
        state[d, h] = gch[d, h, 0:1, 0:1] * s_old + kv
        if o_ref is not None:
            o = (jnp.dot(inner.astype(BF16), v, preferred_element_type=F32)
                 + jnp.dot((q * xi[d, h]).astype(BF16), s_old.astype(BF16), preferred_element_type=F32))
            on = o * lax.rsqrt(jnp.mean(o * o, axis=-1, keepdims=True) + EPS)
            gate = g_ref[:, h * RET_DV:(h + 1) * RET_DV]
            half_gate = gate * 0.5
            silu = half_gate + half_gate * jnp.tanh(half_gate)
            o_ref[:, h * RET_DV:(h + 1) * RET_DV] = (silu.astype(F32) * on).astype(o_ref.dtype)

    @pl.when(j == 0)
    def _():
        for d in range(2):
            for h in range(RET_HEADS):
                chain(d, h, qkv_c, None, None)

    @pl.when(j > 0)
    def _():
        for h in range(RET_HEADS):
            chain(0, h, qkv_f, g_f, of_ref)
            chain(1, h, qkv_b, g_b, ob_ref)


def _retention(log_rate, p_ctx, p_lat):
    G = p_lat.shape[0]
    C = RET_C
    n_lat = SEQ // C
    cos, sin = _rope1d_tables()
    cos, sin = jnp.asarray(cos), jnp.asarray(sin)

    def fwd_chunk(j):
        return jnp.maximum(j - 1, 0)

    def bwd_chunk(j):
        return n_lat - 1 - jnp.maximum(j - 1, 0)

    qkv_w = 2 * RET_QK + RET_V
    gcol = qkv_w // RET_V
    tab = pl.BlockSpec((2, C, RET_DK // 2), lambda b, j: (0, j, 0))
    in_specs = [
        pl.BlockSpec(memory_space=pltpu.SMEM),
        pl.BlockSpec((None, C, qkv_w), lambda b, j: (b, 0, 0)),
        pl.BlockSpec((None, C, qkv_w), lambda b, j: (b, fwd_chunk(j), 0)),
        pl.BlockSpec((None, C, RET_V), lambda b, j: (b, fwd_chunk(j), gcol)),
        pl.BlockSpec((None, C, qkv_w), lambda b, j: (b, bwd_chunk(j), 0)),
        pl.BlockSpec((None, C, RET_V), lambda b, j: (b, bwd_chunk(j), gcol + 1)),
        tab, tab,
    ]
    out = jax.ShapeDtypeStruct((G, SEQ, RET_V), BF16)
    per_chain = lambda *tail: pltpu.VMEM((2, RET_HEADS) + tail, F32)
    return pl.pallas_call(
        _ret_kernel,
        out_shape=(out, out),
        grid=(G, 1 + n_lat),
        in_specs=in_specs,
        out_specs=(pl.BlockSpec((None, C, RET_V), lambda b, j: (b, fwd_chunk(j), 0)),
                   pl.BlockSpec((None, C, RET_V), lambda b, j: (b, bwd_chunk(j), 0))),
        scratch_shapes=[per_chain(RET_DK, RET_DV), per_chain(C, C), per_chain(C, C), per_chain(C, C),
                        per_chain(8, C)],
        compiler_params=_params(("parallel", "arbitrary"), 48),
        name="retention",
    )(log_rate, p_ctx, p_lat, p_lat, p_lat, p_lat, cos, sin)


def _ev_weight(w_in):
    i_q = EV_U
    i_k = i_q + ATT_WIDTH
    i_v = i_k + KV_WIDTH
    cols = [w_in[:, :i_q], w_in[:, i_q:i_k] * (HEAD_DIM ** -0.5 * LOG2E)]
    for base in (i_k, i_v):
        for g in range(ATT_KV_HEADS):
            part = w_in[:, base + g * HEAD_DIM: base + (g + 1) * HEAD_DIM]
            cols += [part, part]
    return jnp.concatenate(cols, axis=1).astype(BF16)


def kernel(x, c, ctx, c_ctx, ada_w, ada_b, norm_mix_g, norm_mlp_g, mlp_w1, mlp_w2, ev_w_in, ev_w_out, hy_conv_w, hy_conv_b, hy_w1, hy_b1, hy_w2, hy_b2, hy_w3, hy_freq, hy_decay, hy_bias, attn_sink, od_w_in, od_w_out, ret_log_rate, final_g):
    D = D_MODEL
    cvec = jnp.concatenate([c, c_ctx[None, :], jnp.zeros((8 - BATCH - 1, D), F32)], axis=0)
    mod = _ada(cvec, ada_w, ada_b)

    m1, m1i, f2, g2 = (jnp.asarray(a).astype(BF16) for a in _fft_mats())
    cfwd, cinv, cfwd_h = (jnp.asarray(a).astype(BF16) for a in _ctx_fft_mats())
    fg = final_g.reshape(1, D)

    gmix = norm_mix_g[0].reshape(1, D)
    gmlp = norm_mlp_g[0].reshape(1, D)
    w_in = _ev_weight(ev_w_in[0])
    conv = (hy_conv_w[0], hy_conv_b[0].reshape(1, EV_U), hy_bias[0].reshape(1, HY_WIDTH))
    z_l, x0_l, zb_l, qkv_l = _inproj0(x, gmix, mod, w_in, *conv, 0, is_ctx=False, tm=512)
    z_c, x0_c, zb_c, qkv_c = _inproj0(ctx, gmix, mod, w_in, *conv, 0, is_ctx=True, tm=CTX_LEN)

    filt = (hy_w1[0], hy_b1[0], hy_w2[0], hy_b2[0], hy_w3[0], hy_freq[0], hy_decay[0])
    h_l, ss_l = _hyena_filter(*filt, L=SEQ)
    h_c, ss_c = _hyena_filter(*filt, L=CTX_LEN)
    hspec = _filter_fft(h_l, ss_l, m1, f2)
    y_l = _fft_conv(z_l, hspec, m1, f2, g2, m1i)
    y_c = _ctx_conv(z_c, h_c, ss_c, cfwd, cinv, cfwd_h)

    sink = attn_sink[0]
    att_l = _attention(sink, qkv_l, qkv_c, local=True)
    att_c = _attention(sink, qkv_l, qkv_c, local=False)

    wo = ev_w_out[0].astype(BF16)
    w1 = mlp_w1[0].astype(BF16)
    w2 = mlp_w2[0].astype(BF16)
    x1 = _mix_mlp(0, x, (x0_l, y_l, zb_l, att_l), wo, mod, gmlp, w1, w2, fg, 0,
                  is_ctx=False, tm=512, fc=512, final_norm=False)
    ctx1 = _mix_mlp(0, ctx, (x0_c, y_c, zb_c, att_c), wo, mod, gmlp, w1, w2, fg, 0,
                    is_ctx=True, tm=CTX_LEN, fc=512, final_norm=False)

    gmix = norm_mix_g[1].reshape(1, D)
    gmlp = norm_mlp_g[1].reshape(1, D)
    w_in = od_w_in[0].astype(BF16)
    p_l = _inproj1(x1, gmix, mod, w_in, 1, is_ctx=False, tm=512, tn=512)
    p_c = _inproj1(ctx1, gmix, mod, w_in, 1, is_ctx=True, tm=CTX_LEN, tn=512)
    o_fwd, o_bwd = _retention(ret_log_rate[0], p_c, p_l)

    wo = od_w_out[0].astype(BF16)
    w1 = mlp_w1[1].astype(BF16)
    w2 = mlp_w2[1].astype(BF16)
    return _mix_mlp(1, x1, (o_fwd, o_bwd), wo, mod, gmlp, w1, w2, fg, 1,
                    is_ctx=False, tm=512, fc=512, final_norm=True)
```

```python
import functools
import math

import numpy as np
import jax
import jax.numpy as jnp
from jax import lax
from jax.experimental import pallas as pl
from jax.experimental.pallas import tpu as pltpu

F32 = jnp.float32
BF16 = jnp.bfloat16
HIGHEST = lax.Precision.HIGHEST

D_MODEL = 1024
BATCH = 2
SEQ = 8192
DEPTH = 2
GRID_W = 64
CTX_LEN = 256
EPS = 1e-6
NEG_INF = -1e30
N_MOD = 6
D_FF = 4 * D_MODEL
ROPE_BASE = 10000.0

HY_WIDTH = D_MODEL // 2
HY_EMB = 33
HY_BANDS = (HY_EMB - 1) // 2
HY_HIDDEN = 64

ATT_HEADS = 8
ATT_KV_HEADS = 2
ATT_GROUP = ATT_HEADS // ATT_KV_HEADS
HEAD_DIM = 64
ATT_WIDTH = ATT_HEADS * HEAD_DIM
KV_WIDTH = ATT_KV_HEADS * HEAD_DIM
BLOCK = 128

RET_HEADS = 4
RET_DK = D_MODEL // RET_HEADS
RET_DV = 2 * RET_DK
RET_QK = RET_HEADS * RET_DK
RET_V = RET_HEADS * RET_DV
OD_IN = 2 * RET_QK + 3 * RET_V

LOG2E = 1.4426950408889634
LANES = 128
MIB = 1024 * 1024

EV_U = 3 * HY_WIDTH
EV_QKV = ATT_WIDTH + 4 * LANES
EV_COLS = EV_U + EV_QKV

FFT_N = 2 * SEQ
FFT_R = 128
FFT_PITCH = FFT_R + 8
SEQ_P = SEQ // FFT_R * FFT_PITCH
CONV_ROWS = 64
CONV_PAD = 8
SUB_ROWS = 256
FFT_UNROLL = 16
RET_C = 256


def _params(sem, vmem_mib):
    return pltpu.CompilerParams(dimension_semantics=sem, vmem_limit_bytes=vmem_mib * MIB)


@functools.lru_cache(maxsize=None)
def _rope2d_tables():
    quarter = HEAD_DIM // 4
    inv = ROPE_BASE ** (-np.arange(quarter, dtype=np.float64) / quarter)
    t = np.arange(SEQ)
    pos = np.stack([t // GRID_W, t % GRID_W], axis=1).astype(np.float64)
    lane = np.arange(HEAD_DIM)
    half = lane // (HEAD_DIM // 2)
    e = lane % (HEAD_DIM // 2)
    ang = pos[:, half] * inv[e % quarter][None, :]
    sign = np.where(e < quarter, -1.0, 1.0)[None, :]
    cos = np.tile(np.cos(ang), (1, 2)).astype(np.float32)
    sin = np.tile(np.sin(ang) * sign, (1, 2)).astype(np.float32)
    return cos, sin


@functools.lru_cache(maxsize=None)
def _rope1d_tables():
    n = RET_DK // 2
    inv = ROPE_BASE ** (-np.linspace(0.0, 1.0, n))
    pos = np.arange(CTX_LEN + SEQ, dtype=np.float64)
    ang = pos[:, None] * inv[None, :]
    cos, sin = np.cos(ang), np.sin(ang)

    def rev(a):
        return a.reshape(-1, RET_C, n)[:, ::-1].reshape(-1, n)

    cos2 = np.stack([cos, rev(cos)]).astype(np.float32)
    sin2 = np.stack([sin, rev(sin)]).astype(np.float32)
    return cos2, sin2


@functools.lru_cache(maxsize=None)
def _fft_mats():
    N, R = FFT_N, FFT_R
    k1 = np.arange(R)
    n2 = np.arange(R)[:, None, None]
    n1 = np.arange(R // 2)[None, None, :]
    n = np.concatenate([n2 + R * n1, (N - ((R - n2) % R + R * n1)) % N], axis=2)
    idx = (k1[None, :, None] * n) % N
    ang = 2.0 * np.pi * idx / N
    c, s = np.cos(ang), np.sin(ang)
    m1 = np.concatenate([c, -s], axis=1).astype(np.float32)
    m1i = np.concatenate([c.transpose(0, 2, 1), -s.transpose(0, 2, 1)], axis=2)
    m1i = m1i[:, :R // 2].astype(np.float32)
    a2 = 2.0 * np.pi * ((np.arange(R)[:, None] * np.arange(R)[None, :]) % R) / R
    fr, fi = np.cos(a2), -np.sin(a2)
    f2 = np.block([[fr, -fi], [fi, fr]]).astype(np.float32)
    g2 = (np.block([[fr, fi], [-fi, fr]]) / N).astype(np.float32)
    return m1, m1i, f2, g2


@functools.lru_cache(maxsize=None)
def _ctx_fft_mats():
    N = 2 * CTX_LEN
    k = np.arange(N)[:, None]
    n = np.arange(N)[None, :]
    ang = 2.0 * np.pi * ((k * n) % N) / N
    c, s = np.cos(ang), np.sin(ang)
    fwd = np.concatenate([c, -s], axis=0).astype(np.float32)
    inv = (np.concatenate([c, -s], axis=1)[:CTX_LEN] / N).astype(np.float32)
    src = np.concatenate([np.arange(CTX_LEN), (N - np.arange(CTX_LEN)) % N])
    return fwd, inv, fwd[:, src]


def _ada_kernel(c_ref, w_ref, b_ref, o_ref):
    c = c_ref[...]
    a = c * (1.0 / (1.0 + jnp.exp(-c)))
    o_ref[...] = _dot3(a, w_ref[...]) + b_ref[...]


def _ada(cvec, ada_w, ada_b):
    tn = 1536
    return pl.pallas_call(
        _ada_kernel,
        out_shape=jax.ShapeDtypeStruct((DEPTH, 8, N_MOD * D_MODEL), F32),
        grid=(DEPTH, N_MOD * D_MODEL // tn),
        in_specs=[
            pl.BlockSpec((8, D_MODEL), lambda i, j: (0, 0)),
            pl.BlockSpec((None, D_MODEL, tn), lambda i, j: (i, 0, j)),
            pl.BlockSpec((None, 1, tn), lambda i, j: (i, 0, j)),
        ],
        out_specs=pl.BlockSpec((None, 8, tn), lambda i, j: (i, 0, j)),
        compiler_params=_params(("parallel", "parallel"), 40),
        name="ada_mod",
    )(cvec, ada_w, ada_b.reshape(DEPTH, 1, N_MOD * D_MODEL))


def _mod_row(ref, row):
    if isinstance(row, int):
        return ref[row:row + 1, :]
    return ref[pl.ds(row, 1), :]


def _norm_mod(x, g, shift, scale):
    y = x * lax.rsqrt(jnp.mean(x * x, axis=-1, keepdims=True) + EPS)
    return (y * g) * (1.0 + scale) + shift


def _rope_tile(x, cos, sin_signed):
    lane = lax.broadcasted_iota(jnp.int32, x.shape, 1)
    first = (lane % 32) < 16
    partner = jnp.where(first, pltpu.roll(x, LANES - 16, axis=1), pltpu.roll(x, 16, axis=1))
    return x * cos + partner * sin_signed


def _project0(x_ref, g, sh, sc, w_ref, cos_ref, sin_ref, u_ref, qkv_ref, rope):
    tn = 512
    n_rot = (ATT_WIDTH + 2 * LANES) // LANES
    sub = min(x_ref.shape[0], SUB_ROWS)
    for s in range(x_ref.shape[0] // sub):
        rows = slice(s * sub, (s + 1) * sub)
        urows = slice(CONV_PAD + s * sub, CONV_PAD + (s + 1) * sub)
        h = _norm_mod(x_ref[rows, :], g, sh, sc).astype(BF16)
        for j in range(EV_U // tn):
            u_ref[urows, j * tn:(j + 1) * tn] = jnp.dot(h, w_ref[:, j * tn:(j + 1) * tn],
                                                        preferred_element_type=F32)
        for j in range(EV_QKV // tn):
            y = jnp.dot(h, w_ref[:, EV_U + j * tn:EV_U + (j + 1) * tn], preferred_element_type=F32)
            for t in range(tn // LANES):
                yt = y[:, t * LANES:(t + 1) * LANES]
                if rope and j * (tn // LANES) + t < n_rot:
                    yt = _rope_tile(yt, cos_ref[rows, :], sin_ref[rows, :])
                qkv_ref[rows, j * tn + t * LANES:j * tn + (t + 1) * LANES] = yt.astype(BF16)


def _short_conv_gate(u_ref, cw_ref, cb_ref, bias_ref, z_ref, x0_ref, zb_ref):
    tm = u_ref.shape[0] - 2 * CONV_PAD
    rc = CONV_ROWS
    z_pitched = z_ref.shape[0] != tm

    def conv(r0, c0):
        cols = slice(c0, c0 + LANES)
        taps = [u_ref[CONV_PAD - 1 + k + r0:CONV_PAD - 1 + k + r0 + rc, cols] for k in range(3)]
        return (taps[0] * cw_ref[0:1, cols] + taps[1] * cw_ref[1:2, cols] + taps[2] * cw_ref[2:3, cols]
                + cb_ref[:, cols])

    for r0 in range(0, tm, rc):
        rows = slice(r0, r0 + rc)
        z0 = r0 // FFT_R * FFT_PITCH + r0 % FFT_R if z_pitched else r0
        for c0 in range(0, HY_WIDTH, LANES):
            cols = slice(c0, c0 + LANES)
            x0 = conv(r0, c0)
            z = conv(r0, 2 * HY_WIDTH + c0) * conv(r0, HY_WIDTH + c0)
            z_ref[z0:z0 + rc, cols] = z
            x0_ref[rows, cols] = x0
            zb_ref[rows, cols] = x0 * (z * bias_ref[:, cols])
    if z_pitched:
        for k in range(tm // FFT_R):
            z_ref[k * FFT_PITCH + FFT_R:(k + 1) * FFT_PITCH, :] = jnp.zeros((FFT_PITCH - FFT_R, HY_WIDTH), F32)


def _inproj0_kernel(x_ref, g_ref, sh_ref, sc_ref, w_ref, cos_ref, sin_ref, cw_ref, cb_ref, bias_ref,
                    z_ref, x0_ref, zb_ref, qkv_ref, u_even, u_odd, *, row, rope, n_tiles):
    r = pl.program_id(0) if row is None else row
    i = pl.program_id(1)
    g, sh, sc = g_ref[...], _mod_row(sh_ref, r), _mod_row(sc_ref, r)
    conv_refs = (cw_ref, cb_ref, bias_ref, z_ref, x0_ref, zb_ref)
    zero_row = jnp.zeros((1, EV_U), F32)
    tm = x_ref.shape[0]
    above, first, last, below = CONV_PAD - 1, CONV_PAD, CONV_PAD + tm - 1, CONV_PAD + tm

    if n_tiles == 1:
        u_even[above:first, :] = zero_row
        u_even[below:below + 1, :] = zero_row
        _project0(x_ref, g, sh, sc, w_ref, cos_ref, sin_ref, u_even, qkv_ref, rope)
        _short_conv_gate(u_even, *conv_refs)
        return

    @pl.when(i == 0)
    def _():
        u_even[...] = jnp.zeros_like(u_even)
        u_odd[...] = jnp.zeros_like(u_odd)

    def step(u_new, u_old):
        j = i - 1
        u_old[above:first, :] = jnp.where(j > 0, u_new[last:last + 1, :], zero_row)
        _project0(x_ref, g, sh, sc, w_ref, cos_ref, sin_ref, u_new, qkv_ref, rope)
        u_old[below:below + 1, :] = jnp.where(j < n_tiles - 1, u_new[first:first + 1, :], zero_row)
        _short_conv_gate(u_old, *conv_refs)

    @pl.when(i % 2 == 0)
    def _():
        step(u_even, u_odd)

    @pl.when(i % 2 == 1)
    def _():
        step(u_odd, u_even)


def _inproj0(x, g, mod, w, conv_w, conv_b, bias, layer, *, is_ctx, tm):
    G, R, _ = x.shape
    n_tiles = R // tm
    n_steps = n_tiles + (1 if n_tiles > 1 else 0)
    cos, sin = _rope2d_tables()
    cos, sin = jnp.asarray(cos), jnp.asarray(sin)
    proj_tile = lambda i: jnp.minimum(i, n_tiles - 1)
    conv_tile = (lambda i: jnp.maximum(i - 1, 0)) if n_tiles > 1 else (lambda i: i)
    if is_ctx:
        tab = pl.BlockSpec((tm, LANES), lambda b, i: (0, 0))
    else:
        tab = pl.BlockSpec((tm, LANES), lambda b, i: (proj_tile(i), 0))
    const = lambda a: pl.BlockSpec(a.shape, lambda b, i: (0, 0))
    hy_out = jax.ShapeDtypeStruct((G, R, HY_WIDTH), F32)
    hy_spec = pl.BlockSpec((None, tm, HY_WIDTH), lambda b, i: (b, conv_tile(i), 0))
    if R == SEQ:
        z_out = jax.ShapeDtypeStruct((G, SEQ_P, HY_WIDTH), F32)
        z_spec = pl.BlockSpec((None, tm // FFT_R * FFT_PITCH, HY_WIDTH), lambda b, i: (b, conv_tile(i), 0))
    else:
        z_out, z_spec = hy_out, hy_spec
    kern = functools.partial(_inproj0_kernel, row=2 if is_ctx else None, rope=not is_ctx, n_tiles=n_tiles)
    return pl.pallas_call(
        kern,
        out_shape=(z_out, hy_out, hy_out, jax.ShapeDtypeStruct((G, R, EV_QKV), BF16)),
        grid=(G, n_steps),
        in_specs=[
            pl.BlockSpec((None, tm, D_MODEL), lambda b, i: (b, proj_tile(i), 0)),
            pl.BlockSpec((1, D_MODEL), lambda b, i: (0, 0)),
            pl.BlockSpec((None, 8, D_MODEL), lambda b, i: (layer, 0, 0)),
            pl.BlockSpec((None, 8, D_MODEL), lambda b, i: (layer, 0, 1)),
            pl.BlockSpec((D_MODEL, EV_COLS), lambda b, i: (0, 0), pipeline_mode=pl.Buffered(1)),
            tab, tab, const(conv_w), const(conv_b), const(bias),
        ],
        out_specs=(z_spec, hy_spec, hy_spec,
                   pl.BlockSpec((None, tm, EV_QKV), lambda b, i: (b, proj_tile(i), 0))),
        scratch_shapes=[pltpu.VMEM((tm + 2 * CONV_PAD, EV_U), F32)] * 2,
        compiler_params=_params(("parallel", "arbitrary"), 48),
        name="inproj0_ctx" if is_ctx else "inproj0",
    )(x, g, mod, mod, w, cos, sin, conv_w, conv_b, bias)


def _inproj1_kernel(x_ref, g_ref, sh_ref, sc_ref, w_ref, o_ref, *, row, tn):
    r = pl.program_id(0) if row is None else row
    g, sh, sc = g_ref[...], _mod_row(sh_ref, r), _mod_row(sc_ref, r)
    sub = min(x_ref.shape[0], SUB_ROWS)
    for s in range(x_ref.shape[0] // sub):
        rows = slice(s * sub, (s + 1) * sub)
        h = _norm_mod(x_ref[rows, :], g, sh, sc).astype(BF16)
        for j in range(w_ref.shape[1] // tn):
            cols = slice(j * tn, (j + 1) * tn)
            o_ref[rows, cols] = jnp.dot(h, w_ref[:, cols], preferred_element_type=F32).astype(o_ref.dtype)


def _inproj1(x, g, mod, w, layer, *, is_ctx, tm, tn):
    G, R, _ = x.shape
    N = w.shape[1]
    kern = functools.partial(_inproj1_kernel, row=2 if is_ctx else None, tn=tn)
    return pl.pallas_call(
        kern,
        out_shape=jax.ShapeDtypeStruct((G, R, N), BF16),
        grid=(G, R // tm),
        in_specs=[
            pl.BlockSpec((None, tm, D_MODEL), lambda b, i: (b, i, 0)),
            pl.BlockSpec((1, D_MODEL), lambda b, i: (0, 0)),
            pl.BlockSpec((None, 8, D_MODEL), lambda b, i: (layer, 0, 0)),
            pl.BlockSpec((None, 8, D_MODEL), lambda b, i: (layer, 0, 1)),
            pl.BlockSpec((D_MODEL, N), lambda b, i: (0, 0), pipeline_mode=pl.Buffered(1)),
        ],
        out_specs=pl.BlockSpec((None, tm, N), lambda b, i: (b, i, 0)),
        compiler_params=_params(("parallel", "parallel"), 48),
        name="inproj1_ctx" if is_ctx else "inproj1",
    )(x, g, mod, mod, w)


def _split_bf16(a):
    hi = a.astype(BF16)
    return hi, (a - hi.astype(F32)).astype(BF16)


def _dot3_split(a_hi, a_lo, b):
    b_hi, b_lo = _split_bf16(b)
    dot = functools.partial(jnp.dot, preferred_element_type=F32)
    return dot(a_hi, b_hi) + (dot(a_lo, b_hi) + dot(a_hi, b_lo))


def _dot3(a, b):
    return _dot3_split(*_split_bf16(a), b)


def _store_pitched(ref, lead, value):
    for k in range(value.shape[0] // FFT_R):
        ref[lead, k * FFT_PITCH:k * FFT_PITCH + FFT_R, :] = value[k * FFT_R:(k + 1) * FFT_R]
        ref[lead, k * FFT_PITCH + FFT_R:(k + 1) * FFT_PITCH, :] = jnp.zeros(
            (FFT_PITCH - FFT_R, value.shape[1]), value.dtype)


def _filter_kernel(w1t_ref, b1_ref, w2t_ref, b2_ref, fr_ref, w3_ref, dec_ref, bands_ref, h_ref, ss_ref, *,
                   L, tm, pitched):
    i = pl.program_id(0)
    t_row = (lax.broadcasted_iota(jnp.int32, (1, tm), 1) + i * tm).astype(F32) / L
    ang = (2.0 * math.pi * t_row) * bands_ref[...]
    t8 = jnp.where(lax.broadcasted_iota(jnp.int32, (8, tm), 0) == 0, t_row, 0.0)
    feat = jnp.concatenate([t8, jnp.cos(ang), -jnp.sin(ang)], axis=0)
    fr = fr_ref[...]
    hid = jnp.sin(fr * (jnp.dot(w1t_ref[...], feat, precision=HIGHEST, preferred_element_type=F32) + b1_ref[...]))
    hid = jnp.sin(fr * (jnp.dot(w2t_ref[...], hid, precision=HIGHEST, preferred_element_type=F32) + b2_ref[...]))
    hid_hi, hid_lo = _split_bf16(hid.T)
    m_col = lax.broadcasted_iota(jnp.int32, (tm, 1), 0) + i * tm
    t_col = m_col.astype(F32) / L
    energy = jnp.zeros((1, HY_WIDTH), F32)
    for side in range(2):
        h = _dot3_split(hid_hi, hid_lo, w3_ref[side]) * jnp.exp(-t_col * jnp.abs(dec_ref[side]))
        if side == 1:
            h = jnp.where(m_col == 0, 0.0, h)
        if pitched:
            _store_pitched(h_ref, side, h)
        else:
            h_ref[side] = h
        energy = energy + jnp.sum(h * h, axis=0, keepdims=True)

    @pl.when(i == 0)
    def _():
        ss_ref[...] = jnp.zeros_like(ss_ref)

    ss_ref[...] += jnp.broadcast_to(energy, ss_ref.shape)


def _hyena_filter(w1, b1, w2, b2, w3, freq, decay, *, L):
    tm = min(L, 1024)
    pitched = L == SEQ
    tm_out, l_out = (tm // FFT_R * FFT_PITCH, SEQ_P) if pitched else (tm, L)
    kern = functools.partial(_filter_kernel, L=L, tm=tm, pitched=pitched)
    w1t = jnp.concatenate([w1[0:1], jnp.zeros((7, HY_HIDDEN), F32), w1[1:]], axis=0).T
    col = lambda a: a.reshape(HY_HIDDEN, 1)
    w3s = jnp.stack([w3[:, :HY_WIDTH], w3[:, HY_WIDTH:]])
    bands = jnp.asarray(np.linspace(1e-4, HY_BANDS - 1, HY_BANDS).astype(np.float32).reshape(HY_BANDS, 1))
    args = (w1t, col(b1), w2.T, col(b2), col(freq), w3s, decay.reshape(2, 1, HY_WIDTH), bands)
    full = lambda a: pl.BlockSpec(a.shape, lambda i: (0,) * a.ndim)
    h, ss = pl.pallas_call(
        kern,
        out_shape=(jax.ShapeDtypeStruct((2, l_out, HY_WIDTH), F32), jax.ShapeDtypeStruct((8, HY_WIDTH), F32)),
        grid=(L // tm,),
        in_specs=[full(a) for a in args],
        out_specs=(pl.BlockSpec((2, tm_out, HY_WIDTH), lambda i: (0, i, 0)),
                   pl.BlockSpec((8, HY_WIDTH), lambda i: (0, 0))),
        compiler_params=_params(("arbitrary",), 48),
        name="hyena_filter_%d" % L,
    )(*args)
    return h.reshape(2 * l_out, HY_WIDTH), ss


def _stack_bf16(re, im):
    return jnp.concatenate([re, im], axis=0).astype(BF16)


def _filter_fft_kernel(h_ref, ss_ref, m1_ref, f2_ref, o_ref, a_scr):
    R, P = FFT_R, FFT_PITCH
    inv_norm = lax.rsqrt(ss_ref[0:1, :])

    def stage1(n2, carry):
        neg = jnp.where(n2 == 0, 0, R - n2)
        xs = jnp.concatenate([h_ref[pl.ds(n2, R // 2, stride=P), :],
                              h_ref[pl.ds(SEQ_P + neg, R // 2, stride=P), :]], axis=0)
        xs = (xs * inv_norm).astype(BF16)
        a = jnp.dot(m1_ref[n2], xs, preferred_element_type=F32)
        base = pl.multiple_of(n2 * P, 8)
        a_scr[0, pl.ds(base, R), :] = a[:R]
        a_scr[1, pl.ds(base, R), :] = a[R:]
        return carry

    lax.fori_loop(0, R, stage1, 0, unroll=FFT_UNROLL)

    ct = a_scr.shape[-1]

    def stage2(pair, carry):
        k1s = (2 * pair, 2 * pair + 1)
        rhs = jnp.concatenate([_stack_bf16(a_scr[0, pl.ds(k1, R, stride=P), :],
                                           a_scr[1, pl.ds(k1, R, stride=P), :]) for k1 in k1s], axis=1)
        x = jnp.dot(f2_ref[...], rhs, preferred_element_type=F32)
        for d, k1 in enumerate(k1s):
            base = pl.multiple_of(k1 * R, R)
            o_ref[0, pl.ds(base, R), :] = x[:R, d * ct:(d + 1) * ct].astype(BF16)
            o_ref[1, pl.ds(base, R), :] = x[R:, d * ct:(d + 1) * ct].astype(BF16)
        return carry

    lax.fori_loop(0, R // 2, stage2, 0, unroll=FFT_UNROLL // 2)


def _filter_fft(h_circ, ss, m1, f2):
    ct = LANES
    once = pl.Buffered(1)
    return pl.pallas_call(
        _filter_fft_kernel,
        out_shape=jax.ShapeDtypeStruct((2, FFT_N, HY_WIDTH), BF16),
        grid=(HY_WIDTH // ct,),
        in_specs=[
            pl.BlockSpec((2 * SEQ_P, ct), lambda c: (0, c), pipeline_mode=once),
            pl.BlockSpec((8, ct), lambda c: (0, c)),
            pl.BlockSpec(m1.shape, lambda c: (0, 0, 0), pipeline_mode=once),
            pl.BlockSpec(f2.shape, lambda c: (0, 0), pipeline_mode=once),
        ],
        out_specs=pl.BlockSpec((2, FFT_N, ct), lambda c: (0, 0, c)),
        scratch_shapes=[pltpu.VMEM((2, FFT_R * FFT_PITCH, ct), F32)],
        compiler_params=_params(("parallel",), 58),
        name="hyena_filter_fft",
    )(h_circ, ss, m1, f2)


def _fft_conv_kernel(z_ref, hs_ref, m1_ref, f2_ref, g2_ref, m1i_ref, y_ref, a_scr):
    R, P = FFT_R, FFT_PITCH
    ct = a_scr.shape[-1]

    def stage1(n2, carry):
        rows = pl.ds(n2, R // 2, stride=P)
        xs = jnp.concatenate([z_ref[0, rows, :], z_ref[1, rows, :]], axis=1).astype(BF16)
        t = jnp.dot(m1_ref[n2, :, 0:R // 2], xs, preferred_element_type=F32)
        base = pl.multiple_of(n2 * P, 8)
        a_scr[0, pl.ds(base, R), :] = t[:R, :ct] - t[R:, ct:]
        a_scr[1, pl.ds(base, R), :] = t[R:, :ct] + t[:R, ct:]
        return carry

    lax.fori_loop(0, R, stage1, 0, unroll=FFT_UNROLL)

    def stage2(pair, carry):
        k1s = (2 * pair, 2 * pair + 1)
        rhs = jnp.concatenate([_stack_bf16(a_scr[0, pl.ds(k1, R, stride=P), :],
                                           a_scr[1, pl.ds(k1, R, stride=P), :]) for k1 in k1s], axis=1)
        x = jnp.dot(f2_ref[...], rhs, preferred_element_type=F32)
        prods = []
        for d, k1 in enumerate(k1s):
            base = pl.multiple_of(k1 * R, R)
            hr = hs_ref[0, pl.ds(base, R), :].astype(F32)
            hi = hs_ref[1, pl.ds(base, R), :].astype(F32)
            xr, xi = x[:R, d * ct:(d + 1) * ct], x[R:, d * ct:(d + 1) * ct]
            prods.append(_stack_bf16(xr * hr - xi * hi, xr * hi + xi * hr))
        bq = jnp.dot(g2_ref[...], jnp.concatenate(prods, axis=1), preferred_element_type=F32)
        for d, k1 in enumerate(k1s):
            a_scr[0, pl.ds(k1, R, stride=P), :] = bq[:R, d * ct:(d + 1) * ct]
            a_scr[1, pl.ds(k1, R, stride=P), :] = bq[R:, d * ct:(d + 1) * ct]
        return carry

    lax.fori_loop(0, R // 2, stage2, 0, unroll=FFT_UNROLL // 2)

    def stage3(n2, carry):
        base = pl.multiple_of(n2 * P, 8)
        br = a_scr[0, pl.ds(base, R), :]
        bi = a_scr[1, pl.ds(base, R), :]
        rhs = jnp.concatenate([jnp.concatenate([br, bi], axis=1),
                               jnp.concatenate([bi, -br], axis=1)], axis=0).astype(BF16)
        y = jnp.dot(m1i_ref[n2], rhs, preferred_element_type=F32)
        rows = pl.ds(n2, R // 2, stride=P)
        y_ref[0, rows, :] = y[:, :ct]
        y_ref[1, rows, :] = y[:, ct:]
        return carry

    lax.fori_loop(0, R, stage3, 0, unroll=FFT_UNROLL)
    for b in range(2):
        for n1 in range(R // 2):
            y_ref[b, n1 * P + R:(n1 + 1) * P, :] = jnp.zeros((P - R, ct), F32)


def _fft_conv(z, hspec, m1, f2, g2, m1i):
    assert z.shape[0] == 2, "the batch pair is packed into one complex signal"
    ct = LANES
    once = pl.Buffered(1)
    return pl.pallas_call(
        _fft_conv_kernel,
        out_shape=jax.ShapeDtypeStruct((2, SEQ_P, HY_WIDTH), F32),
        grid=(HY_WIDTH // ct,),
        in_specs=[
            pl.BlockSpec((2, SEQ_P, ct), lambda c: (0, 0, c), pipeline_mode=once),
            pl.BlockSpec((2, FFT_N, ct), lambda c: (0, 0, c), pipeline_mode=once),
            pl.BlockSpec(m1.shape, lambda c: (0, 0, 0), pipeline_mode=once),
            pl.BlockSpec(f2.shape, lambda c: (0, 0), pipeline_mode=once),
            pl.BlockSpec(g2.shape, lambda c: (0, 0), pipeline_mode=once),
            pl.BlockSpec(m1i.shape, lambda c: (0, 0, 0), pipeline_mode=once),
        ],
        out_specs=pl.BlockSpec((2, SEQ_P, ct), lambda c: (0, 0, c), pipeline_mode=once),
        scratch_shapes=[pltpu.VMEM((2, FFT_R * FFT_PITCH, ct), F32)],
        compiler_params=_params(("arbitrary",), 58),
        name="hyena_fft_conv",
    )(z, hspec, m1, f2, g2, m1i)


def _ctx_conv_kernel(z_ref, h_ref, ss_ref, fwd_ref, inv_ref, fwd_h_ref, y_ref):
    N = 2 * CTX_LEN
    hn = (h_ref[...] * lax.rsqrt(ss_ref[0:1, :])).astype(BF16)
    hs = jnp.dot(fwd_h_ref[...], hn, preferred_element_type=F32)
    zs = jnp.dot(fwd_ref[:, :CTX_LEN], z_ref[...].astype(BF16), preferred_element_type=F32)
    hr, hi, zr, zi = hs[:N], hs[N:], zs[:N], zs[N:]
    ys = _stack_bf16(zr * hr - zi * hi, zr * hi + zi * hr)
    y_ref[...] = jnp.dot(inv_ref[...], ys, preferred_element_type=F32)


def _ctx_conv(z, h_circ, ss, fwd, inv, fwd_h):
    G = z.shape[0]
    full = lambda a: pl.BlockSpec(a.shape, lambda b: (0,) * a.ndim)
    return pl.pallas_call(
        _ctx_conv_kernel,
        out_shape=jax.ShapeDtypeStruct((G, CTX_LEN, HY_WIDTH), F32),
        grid=(G,),
        in_specs=[pl.BlockSpec((None, CTX_LEN, HY_WIDTH), lambda b: (b, 0, 0)),
                  full(h_circ), full(ss), full(fwd), full(inv), full(fwd_h)],
        out_specs=pl.BlockSpec((None, CTX_LEN, HY_WIDTH), lambda b: (b, 0, 0)),
        compiler_params=_params(("parallel",), 48),
        name="hyena_ctx_conv",
    )(z, h_circ, ss, fwd, inv, fwd_h)


def _kv_group(refs, g):
    cols = slice(g * LANES, (g + 1) * LANES)
    return refs[0][:, cols] if len(refs) == 1 else jnp.concatenate([r[:, cols] for r in refs], axis=0)


def _attn_logits(q_ref, k_refs, g, masks):
    low = lax.broadcasted_iota(jnp.int32, (BLOCK, LANES), 1) < HEAD_DIM
    zero = jnp.zeros((), BF16)
    parts = []
    for hh in range(ATT_GROUP):
        h = g * ATT_GROUP + hh
        tile = q_ref[:, (h // 2) * LANES:(h // 2 + 1) * LANES]
        parts.append(jnp.where(low if h % 2 == 0 else ~low, tile, zero))
    s = lax.dot_general(jnp.concatenate(parts, axis=0), _kv_group(k_refs, g), (((1,), (1,)), ((), ())),
                        preferred_element_type=F32)
    if masks is not None:
        ok_prev, ok_next = masks
        s = jnp.concatenate([jnp.where(ok_prev, s[:, :BLOCK], NEG_INF), s[:, BLOCK:2 * BLOCK],
                             jnp.where(ok_next, s[:, 2 * BLOCK:3 * BLOCK], NEG_INF), s[:, 3 * BLOCK:]], axis=1)
    return s


def _attn_output(s, sink_ref, v_refs, g, o_ref):
    low = lax.broadcasted_iota(jnp.int32, (BLOCK, LANES), 1) < HEAD_DIM
    low4 = jnp.concatenate([low] * ATT_GROUP, axis=0)
    v = _kv_group(v_refs, g)
    v_aug = jnp.where(lax.broadcasted_iota(jnp.int32, v.shape, 1) < HEAD_DIM, v, jnp.ones((), BF16))
    sink = jnp.concatenate(
        [jnp.full((BLOCK, 1), sink_ref[g * ATT_GROUP + hh] * LOG2E, F32) for hh in range(ATT_GROUP)], axis=0)
    m = jnp.maximum(jnp.max(s, axis=-1, keepdims=True), sink)
    e = jnp.exp2(s - m).astype(BF16)
    o = jnp.dot(e, v_aug, preferred_element_type=F32) + jnp.where(low4, 0.0, jnp.exp2(sink - m))
    swapped = pltpu.roll(o, HEAD_DIM, axis=1)
    for pair in range(ATT_GROUP // 2):
        ev = slice((2 * pair) * BLOCK, (2 * pair + 1) * BLOCK)
        od = slice((2 * pair + 1) * BLOCK, (2 * pair + 2) * BLOCK)
        even = o[ev] / swapped[ev]
        odd = swapped[od] / o[od]
        t = g * (ATT_GROUP // 2) + pair
        o_ref[:, t * LANES:(t + 1) * LANES] = jnp.where(low, even, odd).astype(o_ref.dtype)


def _attn_ctx_kernel(sink_ref, q_ref, kx_ref, vx_ref, o_ref):
    for g in range(ATT_KV_HEADS):
        _attn_output(_attn_logits(q_ref, (kx_ref,), g, None), sink_ref, (vx_ref,), g, o_ref)


def _attn_local_kernel(sink_ref, q_ref, kp_ref, kc_ref, kn_ref, kx_ref, vp_ref, vc_ref, vn_ref, vx_ref,
                       o_ref, s_even, s_odd, *, n_blocks):
    n = pl.program_id(1)

    @pl.when(n == 0)
    def _():
        s_odd[...] = jnp.zeros_like(s_odd)

    def step(s_new, s_prev):
        nq = jnp.minimum(n, n_blocks - 1)
        qi = lax.broadcasted_iota(jnp.int32, (ATT_GROUP * BLOCK, BLOCK), 0) % BLOCK
        ki = lax.broadcasted_iota(jnp.int32, (ATT_GROUP * BLOCK, BLOCK), 1)
        masks = ((ki >= qi) & (nq > 0), (ki <= qi) & (nq < n_blocks - 1))
        for g in range(ATT_KV_HEADS):
            s_new[g] = _attn_logits(q_ref, (kp_ref, kc_ref, kn_ref, kx_ref), g, masks)
        for g in range(ATT_KV_HEADS):
            _attn_output(s_prev[g], sink_ref, (vp_ref, vc_ref, vn_ref, vx_ref), g, o_ref)

    @pl.when(n % 2 == 0)
    def _():
        step(s_even, s_odd)

    @pl.when(n % 2 == 1)
    def _():
        step(s_odd, s_even)


def _attention(sink, qkv, qkv_ctx, *, local):
    src = qkv if local else qkv_ctx
    G, R, _ = src.shape
    nb = R // BLOCK
    kcol, vcol = ATT_WIDTH // 256, ATT_WIDTH // 256 + 1
    ctx_k = pl.BlockSpec((None, CTX_LEN, 256), lambda b, n: (b, 0, kcol))
    ctx_v = pl.BlockSpec((None, CTX_LEN, 256), lambda b, n: (b, 0, vcol))
    smem = pl.BlockSpec(memory_space=pltpu.SMEM)
    out_shape = jax.ShapeDtypeStruct((G, R, ATT_WIDTH), BF16)
    if not local:
        rows = pl.BlockSpec((None, BLOCK, ATT_WIDTH), lambda b, n: (b, n, 0))
        return pl.pallas_call(
            _attn_ctx_kernel, out_shape=out_shape, grid=(G, nb),
            in_specs=[smem, rows, ctx_k, ctx_v], out_specs=rows,
            compiler_params=_params(("parallel", "parallel"), 48), name="attn_ctx",
        )(sink, qkv_ctx, qkv_ctx, qkv_ctx)

    q_blk = lambda n: jnp.minimum(n, nb - 1)
    o_blk = lambda n: jnp.maximum(n - 1, 0)

    def nbr(col, blk, off):
        return pl.BlockSpec((None, BLOCK, 256), lambda b, n: (b, jnp.clip(blk(n) + off, 0, nb - 1), col))

    in_specs = [smem, pl.BlockSpec((None, BLOCK, ATT_WIDTH), lambda b, n: (b, q_blk(n), 0)),
                nbr(kcol, q_blk, -1), nbr(kcol, q_blk, 0), nbr(kcol, q_blk, 1), ctx_k,
                nbr(vcol, o_blk, -1), nbr(vcol, o_blk, 0), nbr(vcol, o_blk, 1), ctx_v]
    logits_scratch = pltpu.VMEM((ATT_KV_HEADS, ATT_GROUP * BLOCK, 3 * BLOCK + CTX_LEN), F32)
    return pl.pallas_call(
        functools.partial(_attn_local_kernel, n_blocks=nb),
        out_shape=out_shape,
        grid=(G, nb + 1),
        in_specs=in_specs,
        out_specs=pl.BlockSpec((None, BLOCK, ATT_WIDTH), lambda b, n: (b, o_blk(n), 0)),
        scratch_shapes=[logits_scratch, logits_scratch],
        compiler_params=_params(("parallel", "arbitrary"), 48),
        name="attn_local",
    )(sink, qkv, qkv, qkv, qkv, qkv_ctx, qkv, qkv, qkv, qkv_ctx)


def _mixer_out0(rows, x0_ref, y_ref, zb_ref, att_ref, wo_ref):
    if y_ref.shape[0] != x0_ref.shape[0]:
        y = jnp.concatenate([y_ref[r // FFT_R * FFT_PITCH:r // FFT_R * FFT_PITCH + FFT_R, :]
                             for r in range(rows.start, rows.stop, FFT_R)], axis=0)
    else:
        y = y_ref[rows, :]
    hy = (x0_ref[rows, :] * y + zb_ref[rows, :]).astype(BF16)
    return (jnp.dot(hy, wo_ref[:HY_WIDTH, :], preferred_element_type=F32)
            + jnp.dot(att_ref[rows, :], wo_ref[HY_WIDTH:, :], preferred_element_type=F32))


def _mixer_out1(rows, of_ref, ob_ref, wo_ref):
    a = (of_ref[rows, :].astype(F32) + ob_ref[rows, :].astype(F32)).astype(BF16)
    return jnp.dot(a, wo_ref[...], preferred_element_type=F32)


def _mix_kernel(*refs, mixer_out, n_mix, row, fc, final_norm):
    x_ref = refs[0]
    mix_refs = refs[1:2 + n_mix]
    gta_ref, gm_ref, shm_ref, scm_ref, gtm_ref, w1_ref, w2_ref, fg_ref, o_ref, a_scr = refs[2 + n_mix:]
    r = pl.program_id(0) if row is None else row
    gta, gtm = _mod_row(gta_ref, r), _mod_row(gtm_ref, r)
    gm, shm, scm = gm_ref[...], _mod_row(shm_ref, r), _mod_row(scm_ref, r)
    sub = min(x_ref.shape[0], SUB_ROWS)
    for s in range(x_ref.shape[0] // sub):
        rows = slice(s * sub, (s + 1) * sub)
        x1 = x_ref[rows, :] + gta * mixer_out(rows, *mix_refs)
        h = _norm_mod(x1, gm, shm, scm).astype(BF16)
        for c in range(D_FF // fc):
            a = jnp.maximum(jnp.dot(h, w1_ref[:, c * fc:(c + 1) * fc], preferred_element_type=F32), 0.0)
            a_scr[rows, c * fc:(c + 1) * fc] = (a * a).astype(BF16)
        out = x1 + gtm * jnp.dot(a_scr[rows, :], w2_ref[...], preferred_element_type=F32)
        if final_norm:
            out = (out * lax.rsqrt(jnp.mean(out * out, axis=-1, keepdims=True) + EPS)) * fg_ref[...]
        o_ref[rows, :] = out


def _mix_mlp(kind, x, mix_in, wo, mod, gm, w1, w2, fg, layer, *, is_ctx, tm, fc, final_norm):
    G, R, _ = x.shape
    row_spec = lambda w: pl.BlockSpec((None, tm, w), lambda b, i: (b, i, 0))
    modk = lambda k: pl.BlockSpec((None, 8, D_MODEL), lambda b, i: (layer, 0, k))
    vec = pl.BlockSpec((1, D_MODEL), lambda b, i: (0, 0))
    resident = lambda a: pl.BlockSpec(a.shape, lambda b, i: (0, 0), pipeline_mode=pl.Buffered(1))
    if kind == 0:
        mixer_out = _mixer_out0
        y_rows = tm if mix_in[1].shape[1] == R else tm // FFT_R * FFT_PITCH
        y_spec = pl.BlockSpec((None, y_rows, HY_WIDTH), lambda b, i: (b, i, 0))
        mix_specs = [row_spec(HY_WIDTH), y_spec, row_spec(HY_WIDTH), row_spec(ATT_WIDTH)]
    else:
        mixer_out = _mixer_out1
        mix_specs = [row_spec(RET_V)] * 2
    kern = functools.partial(_mix_kernel, mixer_out=mixer_out, n_mix=len(mix_in), row=2 if is_ctx else None,
                             fc=fc, final_norm=final_norm)
    return pl.pallas_call(
        kern,
        out_shape=jax.ShapeDtypeStruct((G, R, D_MODEL), F32),
        grid=(G, R // tm),
        in_specs=[row_spec(D_MODEL)] + mix_specs + [
            resident(wo), modk(2), vec, modk(3), modk(4), modk(5), resident(w1), resident(w2), vec,
        ],
        out_specs=row_spec(D_MODEL),
        scratch_shapes=[pltpu.VMEM((tm, D_FF), BF16)],
        compiler_params=_params(("parallel", "parallel"), 56),
        name="mix_mlp%d%s" % (kind, "_ctx" if is_ctx else ""),
    )(x, *mix_in, wo, mod, gm, mod, mod, mod, w1, w2, fg)


def _ret_kernel(lr_ref, qkv_c, qkv_f, g_f, qkv_b, g_b, cos_ref, sin_ref, of_ref, ob_ref,
                state, dmask, xi, zeta, gch):
    j = pl.program_id(1)
    C = RET_C
    kscale = RET_DK ** -0.5

    @pl.when(j == 0)
    def _():
        state[...] = jnp.zeros_like(state)
        row = lax.broadcasted_iota(jnp.int32, (C, C), 0).astype(F32)
        col = lax.broadcasted_iota(jnp.int32, (C, C), 1).astype(F32)
        for d in range(2):
            e = row - col if d == 0 else col - row
            p = row if d == 0 else (C - 1) - row
            for h in range(RET_HEADS):
                lg = -jnp.exp(jnp.full((C, C), lr_ref[d, h], F32))
                dmask[d, h] = jnp.where(e >= 0, jnp.exp(e * lg), 0.0) * kscale
                xi[d, h] = jnp.exp((p + 1.0) * lg)
                zeta[d, h] = jnp.exp(((C - 1) - p) * lg) * kscale
                gch[d, h] = jnp.exp(C * -jnp.exp(jnp.full((8, C), lr_ref[d, h], F32)))

    half = RET_DK // 2

    def chain(d, h, src_ref, g_ref, o_ref):
        cos, sin = cos_ref[d], sin_ref[d]

        def rot(c0):
            t1 = src_ref[:, c0:c0 + half].astype(F32)
            t2 = src_ref[:, c0 + half:c0 + RET_DK].astype(F32)
            return jnp.concatenate([t1 * cos - t2 * sin, t2 * cos + t1 * sin], axis=1)

        q = rot(h * RET_DK)
        k = rot(RET_QK + h * RET_DK)
        v = src_ref[:, 2 * RET_QK + h * RET_DV:2 * RET_QK + (h + 1) * RET_DV]
        inner = lax.dot_general(q.astype(BF16), k.astype(BF16), (((1,), (1,)), ((), ())),
                                preferred_element_type=F32) * dmask[d, h]
        s_old = state[d, h]
        kv = lax.dot_general((k * zeta[d, h]).astype(BF16), v, (((0,), (0,)), ((), ())),
                             preferred_element_type=F32)
        state[d, h] = gch[d, h, 0:1, 0:1] * s_old + kv
        if o_ref is not None:
            o = (jnp.dot(inner.astype(BF16), v, preferred_element_type=F32)
                 + jnp.dot((q * xi[d, h]).astype(BF16), s_old.astype(BF16), preferred_element_type=F32))
            on = o * lax.rsqrt(jnp.mean(o * o, axis=-1, keepdims=True) + EPS)
            gate = g_ref[:, h * RET_DV:(h + 1) * RET_DV]
            half_gate = gate * 0.5
            silu = half_gate + half_gate * jnp.tanh(half_gate)
            o_ref[:, h * RET_DV:(h + 1) * RET_DV] = (silu.astype(F32) * on).astype(o_ref.dtype)

    @pl.when(j == 0)
    def _():
        for d in range(2):
            for h in range(RET_HEADS):
                chain(d, h, qkv_c, None, None)

    @pl.when(j > 0)
    def _():
        for h in range(RET_HEADS):
            chain(0, h, qkv_f, g_f, of_ref)
            chain(1, h, qkv_b, g_b, ob_ref)


def _retention(log_rate, p_ctx, p_lat):
    G = p_lat.shape[0]
    C = RET_C
    n_lat = SEQ // C
    cos, sin = _rope1d_tables()
    cos, sin = jnp.asarray(cos), jnp.asarray(sin)

    def fwd_chunk(j):
        return jnp.maximum(j - 1, 0)

    def bwd_chunk(j):
        return n_lat - 1 - jnp.maximum(j - 1, 0)

    qkv_w = 2 * RET_QK + RET_V
    gcol = qkv_w // RET_V
    tab = pl.BlockSpec((2, C, RET_DK // 2), lambda b, j: (0, j, 0))
    in_specs = [
        pl.BlockSpec(memory_space=pltpu.SMEM),
        pl.BlockSpec((None, C, qkv_w), lambda b, j: (b, 0, 0)),
        pl.BlockSpec((None, C, qkv_w), lambda b, j: (b, fwd_chunk(j), 0)),
        pl.BlockSpec((None, C, RET_V), lambda b, j: (b, fwd_chunk(j), gcol)),
        pl.BlockSpec((None, C, qkv_w), lambda b, j: (b, bwd_chunk(j), 0)),
        pl.BlockSpec((None, C, RET_V), lambda b, j: (b, bwd_chunk(j), gcol + 1)),
        tab, tab,
    ]
    out = jax.ShapeDtypeStruct((G, SEQ, RET_V), BF16)
    per_chain = lambda *tail: pltpu.VMEM((2, RET_HEADS) + tail, F32)
    return pl.pallas_call(
        _ret_kernel,
        out_shape=(out, out),
        grid=(G, 1 + n_lat),
        in_specs=in_specs,
        out_specs=(pl.BlockSpec((None, C, RET_V), lambda b, j: (b, fwd_chunk(j), 0)),
                   pl.BlockSpec((None, C, RET_V), lambda b, j: (b, bwd_chunk(j), 0))),
        scratch_shapes=[per_chain(RET_DK, RET_DV), per_chain(C, C), per_chain(C, C), per_chain(C, C),
                        per_chain(8, C)],
        compiler_params=_params(("parallel", "arbitrary"), 48),
        name="retention",
    )(log_rate, p_ctx, p_lat, p_lat, p_lat, p_lat, cos, sin)


def _ev_weight(w_in):
    i_q = EV_U
    i_k = i_q + ATT_WIDTH
    i_v = i_k + KV_WIDTH
    cols = [w_in[:, :i_q], w_in[:, i_q:i_k] * (HEAD_DIM ** -0.5 * LOG2E)]
    for base in (i_k, i_v):
        for g in range(ATT_KV_HEADS):
            part = w_in[:, base + g * HEAD_DIM: base + (g + 1) * HEAD_DIM]
            cols += [part, part]
    return jnp.concatenate(cols, axis=1).astype(BF16)


def kernel(x, c, ctx, c_ctx, ada_w, ada_b, norm_mix_g, norm_mlp_g, mlp_w1, mlp_w2, ev_w_in, ev_w_out, hy_conv_w, hy_conv_b, hy_w1, hy_b1, hy_w2, hy_b2, hy_w3, hy_freq, hy_decay, hy_bias, attn_sink, od_w_in, od_w_out, ret_log_rate, final_g):
    D = D_MODEL
    cvec = jnp.concatenate([c, c_ctx[None, :], jnp.zeros((8 - BATCH - 1, D), F32)], axis=0)
    mod = _ada(cvec, ada_w, ada_b)

    m1, m1i, f2, g2 = (jnp.asarray(a).astype(BF16) for a in _fft_mats())
    cfwd, cinv, cfwd_h = (jnp.asarray(a).astype(BF16) for a in _ctx_fft_mats())
    fg = final_g.reshape(1, D)

    gmix = norm_mix_g[0].reshape(1, D)
    gmlp = norm_mlp_g[0].reshape(1, D)
    w_in = _ev_weight(ev_w_in[0])
    conv = (hy_conv_w[0], hy_conv_b[0].reshape(1, EV_U), hy_bias[0].reshape(1, HY_WIDTH))
    z_l, x0_l, zb_l, qkv_l = _inproj0(x, gmix, mod, w_in, *conv, 0, is_ctx=False, tm=512)
    z_c, x0_c, zb_c, qkv_c = _inproj0(ctx, gmix, mod, w_in, *conv, 0, is_ctx=True, tm=CTX_LEN)

    filt = (hy_w1[0], hy_b1[0], hy_w2[0], hy_b2[0], hy_w3[0], hy_freq[0], hy_decay[0])
    h_l, ss_l = _hyena_filter(*filt, L=SEQ)
    h_c, ss_c = _hyena_filter(*filt, L=CTX_LEN)
    hspec = _filter_fft(h_l, ss_l, m1, f2)
    y_l = _fft_conv(z_l, hspec, m1, f2, g2, m1i)
    y_c = _ctx_conv(z_c, h_c, ss_c, cfwd, cinv, cfwd_h)

    sink = attn_sink[0]
    att_l = _attention(sink, qkv_l, qkv_c, local=True)
    att_c = _attention(sink, qkv_l, qkv_c, local=False)

    wo = ev_w_out[0].astype(BF16)
    w1 = mlp_w1[0].astype(BF16)
    w2 = mlp_w2[0].astype(BF16)
    x1 = _mix_mlp(0, x, (x0_l, y_l, zb_l, att_l), wo, mod, gmlp, w1, w2, fg, 0,
                  is_ctx=False, tm=512, fc=512, final_norm=False)
    ctx1 = _mix_mlp(0, ctx, (x0_c, y_c, zb_c, att_c), wo, mod, gmlp, w1, w2, fg, 0,
                    is_ctx=True, tm=CTX_LEN, fc=512, final_norm=False)

    gmix = norm_mix_g[1].reshape(1, D)
    gmlp = norm_mlp_g[1].reshape(1, D)
    w_in = od_w_in[0].astype(BF16)
    p_l = _inproj1(x1, gmix, mod, w_in, 1, is_ctx=False, tm=512, tn=512)
    p_c = _inproj1(ctx1, gmix, mod, w_in, 1, is_ctx=True, tm=CTX_LEN, tn=512)
    o_fwd, o_bwd = _retention(ret_log_rate[0], p_c, p_l)

    wo = od_w_out[0].astype(BF16)
    w1 = mlp_w1[1].astype(BF16)
    w2 = mlp_w2[1].astype(BF16)
    return _mix_mlp(1, x1, (o_fwd, o_bwd), wo, mod, gmlp, w1, w2, fg, 1,
                    is_ctx=False, tm=512, fc=512, final_norm=True)
```

```python
import functools
import math

import numpy as np
import jax
import jax.numpy as jnp
from jax import lax
from jax.experimental import pallas as pl
from jax.experimental.pallas import tpu as pltpu

F32 = jnp.float32
BF16 = jnp.bfloat16
HIGHEST = lax.Precision.HIGHEST

D_MODEL = 1024
BATCH = 2
SEQ = 8192
DEPTH = 2
GRID_W = 64
CTX_LEN = 256
EPS = 1e-6
NEG_INF = -1e30
N_MOD = 6
D_FF = 4 * D_MODEL
ROPE_BASE = 10000.0

HY_WIDTH = D_MODEL // 2
HY_EMB = 33
HY_BANDS = (HY_EMB - 1) // 2
HY_HIDDEN = 64

ATT_HEADS = 8
ATT_KV_HEADS = 2
ATT_GROUP = ATT_HEADS // ATT_KV_HEADS
HEAD_DIM = 64
ATT_WIDTH = ATT_HEADS * HEAD_DIM
KV_WIDTH = ATT_KV_HEADS * HEAD_DIM
BLOCK = 128

RET_HEADS = 4
RET_DK = D_MODEL // RET_HEADS
RET_DV = 2 * RET_DK
RET_QK = RET_HEADS * RET_DK
RET_V = RET_HEADS * RET_DV
OD_IN = 2 * RET_QK + 3 * RET_V

LOG2E = 1.4426950408889634
LANES = 128
MIB = 1024 * 1024

EV_U = 3 * HY_WIDTH
EV_QKV = ATT_WIDTH + 4 * LANES
EV_COLS = EV_U + EV_QKV

FFT_N = 2 * SEQ
FFT_R = 128
FFT_PITCH = FFT_R + 8
SEQ_P = SEQ // FFT_R * FFT_PITCH
CONV_ROWS = 64
CONV_PAD = 8
SUB_ROWS = 256
FFT_UNROLL = 16
RET_C = 256
STAGE_BYTES = 2 * MIB


def _params(sem, vmem_mib):
    return pltpu.CompilerParams(dimension_semantics=sem, vmem_limit_bytes=vmem_mib * MIB)


@functools.lru_cache(maxsize=None)
def _rope2d_tables():
    quarter = HEAD_DIM // 4
    inv = ROPE_BASE ** (-np.arange(quarter, dtype=np.float64) / quarter)
    t = np.arange(SEQ)
    pos = np.stack([t // GRID_W, t % GRID_W], axis=1).astype(np.float64)
    lane = np.arange(HEAD_DIM)
    half = lane // (HEAD_DIM // 2)
    e = lane % (HEAD_DIM // 2)
    ang = pos[:, half] * inv[e % quarter][None, :]
    sign = np.where(e < quarter, -1.0, 1.0)[None, :]
    cos = np.tile(np.cos(ang), (1, 2)).astype(np.float32)
    sin = np.tile(np.sin(ang) * sign, (1, 2)).astype(np.float32)
    return cos, sin


@functools.lru_cache(maxsize=None)
def _rope1d_tables():
    n = RET_DK // 2
    inv = ROPE_BASE ** (-np.linspace(0.0, 1.0, n))
    pos = np.arange(CTX_LEN + SEQ, dtype=np.float64)
    ang = pos[:, None] * inv[None, :]
    cos, sin = np.cos(ang), np.sin(ang)

    def rev(a):
        return a.reshape(-1, RET_C, n)[:, ::-1].reshape(-1, n)

    cos2 = np.stack([cos, rev(cos)]).astype(np.float32)
    sin2 = np.stack([sin, rev(sin)]).astype(np.float32)
    return cos2, sin2


@functools.lru_cache(maxsize=None)
def _fft_mats():
    N, R = FFT_N, FFT_R
    k1 = np.arange(R)
    n2 = np.arange(R)[:, None, None]
    n1 = np.arange(R // 2)[None, None, :]
    n = np.concatenate([n2 + R * n1, (N - ((R - n2) % R + R * n1)) % N], axis=2)
    idx = (k1[None, :, None] * n) % N
    ang = 2.0 * np.pi * idx / N
    c, s = np.cos(ang), np.sin(ang)
    m1 = np.concatenate([c, -s], axis=1).astype(np.float32)
    m1i = np.concatenate([c.transpose(0, 2, 1), -s.transpose(0, 2, 1)], axis=2)
    m1i = m1i[:, :R // 2].astype(np.float32)
    a2 = 2.0 * np.pi * ((np.arange(R)[:, None] * np.arange(R)[None, :]) % R) / R
    fr, fi = np.cos(a2), -np.sin(a2)
    f2 = np.block([[fr, -fi], [fi, fr]]).astype(np.float32)
    g2 = (np.block([[fr, fi], [-fi, fr]]) / N).astype(np.float32)
    return m1, m1i, f2, g2


@functools.lru_cache(maxsize=None)
def _ctx_fft_mats():
    N = 2 * CTX_LEN
    k = np.arange(N)[:, None]
    n = np.arange(N)[None, :]
    ang = 2.0 * np.pi * ((k * n) % N) / N
    c, s = np.cos(ang), np.sin(ang)
    fwd = np.concatenate([c, -s], axis=0).astype(np.float32)
    inv = (np.concatenate([c, -s], axis=1)[:CTX_LEN] / N).astype(np.float32)
    src = np.concatenate([np.arange(CTX_LEN), (N - np.arange(CTX_LEN)) % N])
    return fwd, inv, fwd[:, src]


def _ada_kernel(c_ref, w_ref, b_ref, o_ref):
    c = c_ref[...]
    a = c * (1.0 / (1.0 + jnp.exp(-c)))
    o_ref[...] = _dot3(a, w_ref[...]) + b_ref[...]


def _ada(cvec, ada_w, ada_b):
    tn = 1536
    return pl.pallas_call(
        _ada_kernel,
        out_shape=jax.ShapeDtypeStruct((DEPTH, 8, N_MOD * D_MODEL), F32),
        grid=(DEPTH, N_MOD * D_MODEL // tn),
        in_specs=[
            pl.BlockSpec((8, D_MODEL), lambda i, j: (0, 0)),
            pl.BlockSpec((None, D_MODEL, tn), lambda i, j: (i, 0, j)),
            pl.BlockSpec((None, 1, tn), lambda i, j: (i, 0, j)),
        ],
        out_specs=pl.BlockSpec((None, 8, tn), lambda i, j: (i, 0, j)),
        compiler_params=_params(("parallel", "parallel"), 40),
        name="ada_mod",
    )(cvec, ada_w, ada_b.reshape(DEPTH, 1, N_MOD * D_MODEL))


def _mod_row(ref, row):
    if isinstance(row, int):
        return ref[row:row + 1, :]
    return ref[pl.ds(row, 1), :]


def _norm_mod(x, g, shift, scale):
    y = x * lax.rsqrt(jnp.mean(x * x, axis=-1, keepdims=True) + EPS)
    return (y * g) * (1.0 + scale) + shift


def _rope_tile(x, cos, sin_signed):
    lane = lax.broadcasted_iota(jnp.int32, x.shape, 1)
    first = (lane % 32) < 16
    partner = jnp.where(first, pltpu.roll(x, LANES - 16, axis=1), pltpu.roll(x, 16, axis=1))
    return x * cos + partner * sin_signed


def _project0(x_ref, g, sh, sc, w_ref, cos_ref, sin_ref, u_ref, qkv_ref, rope):
    tn = 512
    n_rot = (ATT_WIDTH + 2 * LANES) // LANES
    sub = min(x_ref.shape[0], SUB_ROWS)
    for s in range(x_ref.shape[0] // sub):
        rows = slice(s * sub, (s + 1) * sub)
        urows = slice(CONV_PAD + s * sub, CONV_PAD + (s + 1) * sub)
        h = _norm_mod(x_ref[rows, :], g, sh, sc).astype(BF16)
        for j in range(EV_U // tn):
            u_ref[urows, j * tn:(j + 1) * tn] = jnp.dot(h, w_ref[:, j * tn:(j + 1) * tn],
                                                        preferred_element_type=F32)
        for j in range(EV_QKV // tn):
            y = jnp.dot(h, w_ref[:, EV_U + j * tn:EV_U + (j + 1) * tn], preferred_element_type=F32)
            for t in range(tn // LANES):
                yt = y[:, t * LANES:(t + 1) * LANES]
                if rope and j * (tn // LANES) + t < n_rot:
                    yt = _rope_tile(yt, cos_ref[rows, :], sin_ref[rows, :])
                qkv_ref[rows, j * tn + t * LANES:j * tn + (t + 1) * LANES] = yt.astype(BF16)


def _short_conv_gate(u_ref, cw_ref, cb_ref, bias_ref, z_ref, x0_ref, zb_ref):
    tm = u_ref.shape[0] - 2 * CONV_PAD
    rc = CONV_ROWS
    z_pitched = z_ref.shape[0] != tm

    def conv(r0, c0):
        cols = slice(c0, c0 + LANES)
        taps = [u_ref[CONV_PAD - 1 + k + r0:CONV_PAD - 1 + k + r0 + rc, cols] for k in range(3)]
        return (taps[0] * cw_ref[0:1, cols] + taps[1] * cw_ref[1:2, cols] + taps[2] * cw_ref[2:3, cols]
                + cb_ref[:, cols])

    for r0 in range(0, tm, rc):
        rows = slice(r0, r0 + rc)
        z0 = r0 // FFT_R * FFT_PITCH + r0 % FFT_R if z_pitched else r0
        for c0 in range(0, HY_WIDTH, LANES):
            cols = slice(c0, c0 + LANES)
            x0 = conv(r0, c0)
            z = conv(r0, 2 * HY_WIDTH + c0) * conv(r0, HY_WIDTH + c0)
            z_ref[z0:z0 + rc, cols] = z
            x0_ref[rows, cols] = x0
            zb_ref[rows, cols] = x0 * (z * bias_ref[:, cols])
    if z_pitched:
        for k in range(tm // FFT_R):
            z_ref[k * FFT_PITCH + FFT_R:(k + 1) * FFT_PITCH, :] = jnp.zeros((FFT_PITCH - FFT_R, HY_WIDTH), F32)


def _inproj0_kernel(x_ref, g_ref, sh_ref, sc_ref, w_ref, cos_ref, sin_ref, cw_ref, cb_ref, bias_ref,
                    z_ref, x0_ref, zb_ref, qkv_ref, u_even, u_odd, *, row, rope, n_tiles):
    r = pl.program_id(0) if row is None else row
    i = pl.program_id(1)
    g, sh, sc = g_ref[...], _mod_row(sh_ref, r), _mod_row(sc_ref, r)
    conv_refs = (cw_ref, cb_ref, bias_ref, z_ref, x0_ref, zb_ref)
    zero_row = jnp.zeros((1, EV_U), F32)
    tm = x_ref.shape[0]
    above, first, last, below = CONV_PAD - 1, CONV_PAD, CONV_PAD + tm - 1, CONV_PAD + tm

    if n_tiles == 1:
        u_even[above:first, :] = zero_row
        u_even[below:below + 1, :] = zero_row
        _project0(x_ref, g, sh, sc, w_ref, cos_ref, sin_ref, u_even, qkv_ref, rope)
        _short_conv_gate(u_even, *conv_refs)
        return

    @pl.when(i == 0)
    def _():
        u_even[...] = jnp.zeros_like(u_even)
        u_odd[...] = jnp.zeros_like(u_odd)

    def step(u_new, u_old):
        j = i - 1
        u_old[above:first, :] = jnp.where(j > 0, u_new[last:last + 1, :], zero_row)
        _project0(x_ref, g, sh, sc, w_ref, cos_ref, sin_ref, u_new, qkv_ref, rope)
        u_old[below:below + 1, :] = jnp.where(j < n_tiles - 1, u_new[first:first + 1, :], zero_row)
        _short_conv_gate(u_old, *conv_refs)

    @pl.when(i % 2 == 0)
    def _():
        step(u_even, u_odd)

    @pl.when(i % 2 == 1)
    def _():
        step(u_odd, u_even)


def _inproj0(x, g, mod, w, conv_w, conv_b, bias, layer, *, is_ctx, tm):
    G, R, _ = x.shape
    n_tiles = R // tm
    n_steps = n_tiles + (1 if n_tiles > 1 else 0)
    cos, sin = _rope2d_tables()
    cos, sin = jnp.asarray(cos), jnp.asarray(sin)
    proj_tile = lambda i: jnp.minimum(i, n_tiles - 1)
    conv_tile = (lambda i: jnp.maximum(i - 1, 0)) if n_tiles > 1 else (lambda i: i)
    if is_ctx:
        tab = pl.BlockSpec((tm, LANES), lambda b, i: (0, 0))
    else:
        tab = pl.BlockSpec((tm, LANES), lambda b, i: (proj_tile(i), 0))
    const = lambda a: pl.BlockSpec(a.shape, lambda b, i: (0, 0))
    hy_out = jax.ShapeDtypeStruct((G, R, HY_WIDTH), F32)
    hy_spec = pl.BlockSpec((None, tm, HY_WIDTH), lambda b, i: (b, conv_tile(i), 0))
    if R == SEQ:
        z_out = jax.ShapeDtypeStruct((G, SEQ_P, HY_WIDTH), F32)
        z_spec = pl.BlockSpec((None, tm // FFT_R * FFT_PITCH, HY_WIDTH), lambda b, i: (b, conv_tile(i), 0))
    else:
        z_out, z_spec = hy_out, hy_spec
    kern = functools.partial(_inproj0_kernel, row=2 if is_ctx else None, rope=not is_ctx, n_tiles=n_tiles)
    return pl.pallas_call(
        kern,
        out_shape=(z_out, hy_out, hy_out, jax.ShapeDtypeStruct((G, R, EV_QKV), BF16)),
        grid=(G, n_steps),
        in_specs=[
            pl.BlockSpec((None, tm, D_MODEL), lambda b, i: (b, proj_tile(i), 0)),
            pl.BlockSpec((1, D_MODEL), lambda b, i: (0, 0)),
            pl.BlockSpec((None, 8, D_MODEL), lambda b, i: (layer, 0, 0)),
            pl.BlockSpec((None, 8, D_MODEL), lambda b, i: (layer, 0, 1)),
            pl.BlockSpec((D_MODEL, EV_COLS), lambda b, i: (0, 0), pipeline_mode=pl.Buffered(1)),
            tab, tab, const(conv_w), const(conv_b), const(bias),
        ],
        out_specs=(z_spec, hy_spec, hy_spec,
                   pl.BlockSpec((None, tm, EV_QKV), lambda b, i: (b, proj_tile(i), 0))),
        scratch_shapes=[pltpu.VMEM((tm + 2 * CONV_PAD, EV_U), F32)] * 2,
        compiler_params=_params(("parallel", "arbitrary"), 48),
        name="inproj0_ctx" if is_ctx else "inproj0",
    )(x, g, mod, mod, w, cos, sin, conv_w, conv_b, bias)


def _inproj1_kernel(x_ref, g_ref, sh_ref, sc_ref, w_ref, o_ref, *rest, row, tn, stream):
    if stream:
        w_out, w_s, stage, sems = rest

        @pl.when((pl.program_id(0) == 0) & (pl.program_id(1) == 0))
        def _():
            _stream_cast(w_ref, w_s, stage, sems)
            _export((w_s,), (w_out,), sems)

        w_ref = w_s
    r = pl.program_id(0) if row is None else row
    g, sh, sc = g_ref[...], _mod_row(sh_ref, r), _mod_row(sc_ref, r)
    sub = min(x_ref.shape[0], SUB_ROWS)
    for s in range(x_ref.shape[0] // sub):
        rows = slice(s * sub, (s + 1) * sub)
        h = _norm_mod(x_ref[rows, :], g, sh, sc).astype(BF16)
        for j in range(w_ref.shape[1] // tn):
            cols = slice(j * tn, (j + 1) * tn)
            o_ref[rows, cols] = jnp.dot(h, w_ref[:, cols], preferred_element_type=F32).astype(o_ref.dtype)


def _inproj1(x, g, mod, w, layer, *, is_ctx, tm, tn):
    G, R, _ = x.shape
    N = w.shape[1]
    stream = w.dtype == F32
    hbm = pl.BlockSpec(memory_space=pl.ANY)
    kern = functools.partial(_inproj1_kernel, row=2 if is_ctx else None, tn=tn, stream=stream)
    out_shape = [jax.ShapeDtypeStruct((G, R, N), BF16)]
    out_specs = [pl.BlockSpec((None, tm, N), lambda b, i: (b, i, 0))]
    scratch = []
    if stream:
        out_shape.append(jax.ShapeDtypeStruct(w.shape, BF16))
        out_specs.append(hbm)
        scratch = [pltpu.VMEM(w.shape, BF16), pltpu.VMEM((2, STAGE_BYTES // (4 * N), N), F32),
                   pltpu.SemaphoreType.DMA((2,))]
    res = pl.pallas_call(
        kern,
        out_shape=out_shape,
        grid=(G, R // tm),
        in_specs=[
            pl.BlockSpec((None, tm, D_MODEL), lambda b, i: (b, i, 0)),
            pl.BlockSpec((1, D_MODEL), lambda b, i: (0, 0)),
            pl.BlockSpec((None, 8, D_MODEL), lambda b, i: (layer, 0, 0)),
            pl.BlockSpec((None, 8, D_MODEL), lambda b, i: (layer, 0, 1)),
            hbm if stream else pl.BlockSpec((D_MODEL, N), lambda b, i: (0, 0), pipeline_mode=pl.Buffered(1)),
        ],
        out_specs=out_specs,
        scratch_shapes=scratch,
        compiler_params=_params(("arbitrary", "arbitrary") if stream else ("parallel", "parallel"), 48),
        name="inproj1_ctx" if is_ctx else "inproj1",
    )(x, g, mod, mod, w)
    return res if stream else res[0]


def _split_bf16(a):
    hi = a.astype(BF16)
    return hi, (a - hi.astype(F32)).astype(BF16)


def _dot3_split(a_hi, a_lo, b):
    b_hi, b_lo = _split_bf16(b)
    dot = functools.partial(jnp.dot, preferred_element_type=F32)
    return dot(a_hi, b_hi) + (dot(a_lo, b_hi) + dot(a_hi, b_lo))


def _dot3(a, b):
    return _dot3_split(*_split_bf16(a), b)


def _store_pitched(ref, lead, value):
    for k in range(value.shape[0] // FFT_R):
        ref[lead, k * FFT_PITCH:k * FFT_PITCH + FFT_R, :] = value[k * FFT_R:(k + 1) * FFT_R]
        ref[lead, k * FFT_PITCH + FFT_R:(k + 1) * FFT_PITCH, :] = jnp.zeros(
            (FFT_PITCH - FFT_R, value.shape[1]), value.dtype)


def _filter_kernel(w1t_ref, b1_ref, w2t_ref, b2_ref, fr_ref, w3_ref, dec_ref, bands_ref, h_ref, ss_ref, *,
                   L, tm, pitched):
    i = pl.program_id(0)
    t_row = (lax.broadcasted_iota(jnp.int32, (1, tm), 1) + i * tm).astype(F32) / L
    ang = (2.0 * math.pi * t_row) * bands_ref[...]
    t8 = jnp.where(lax.broadcasted_iota(jnp.int32, (8, tm), 0) == 0, t_row, 0.0)
    feat = jnp.concatenate([t8, jnp.cos(ang), -jnp.sin(ang)], axis=0)
    fr = fr_ref[...]
    hid = jnp.sin(fr * (jnp.dot(w1t_ref[...], feat, precision=HIGHEST, preferred_element_type=F32) + b1_ref[...]))
    hid = jnp.sin(fr * (jnp.dot(w2t_ref[...], hid, precision=HIGHEST, preferred_element_type=F32) + b2_ref[...]))
    hid_hi, hid_lo = _split_bf16(hid.T)
    m_col = lax.broadcasted_iota(jnp.int32, (tm, 1), 0) + i * tm
    t_col = m_col.astype(F32) / L
    energy = jnp.zeros((1, HY_WIDTH), F32)
    for side in range(2):
        h = _dot3_split(hid_hi, hid_lo, w3_ref[side]) * jnp.exp(-t_col * jnp.abs(dec_ref[side]))
        if side == 1:
            h = jnp.where(m_col == 0, 0.0, h)
        if pitched:
            _store_pitched(h_ref, side, h)
        else:
            h_ref[side] = h
        energy = energy + jnp.sum(h * h, axis=0, keepdims=True)

    @pl.when(i == 0)
    def _():
        ss_ref[...] = jnp.zeros_like(ss_ref)

    ss_ref[...] += jnp.broadcast_to(energy, ss_ref.shape)


def _hyena_filter(w1, b1, w2, b2, w3, freq, decay, *, L):
    tm = min(L, 1024)
    pitched = L == SEQ
    tm_out, l_out = (tm // FFT_R * FFT_PITCH, SEQ_P) if pitched else (tm, L)
    kern = functools.partial(_filter_kernel, L=L, tm=tm, pitched=pitched)
    w1t = jnp.concatenate([w1[0:1], jnp.zeros((7, HY_HIDDEN), F32), w1[1:]], axis=0).T
    col = lambda a: a.reshape(HY_HIDDEN, 1)
    w3s = jnp.stack([w3[:, :HY_WIDTH], w3[:, HY_WIDTH:]])
    bands = jnp.asarray(np.linspace(1e-4, HY_BANDS - 1, HY_BANDS).astype(np.float32).reshape(HY_BANDS, 1))
    args = (w1t, col(b1), w2.T, col(b2), col(freq), w3s, decay.reshape(2, 1, HY_WIDTH), bands)
    full = lambda a: pl.BlockSpec(a.shape, lambda i: (0,) * a.ndim)
    h, ss = pl.pallas_call(
        kern,
        out_shape=(jax.ShapeDtypeStruct((2, l_out, HY_WIDTH), F32), jax.ShapeDtypeStruct((8, HY_WIDTH), F32)),
        grid=(L // tm,),
        in_specs=[full(a) for a in args],
        out_specs=(pl.BlockSpec((2, tm_out, HY_WIDTH), lambda i: (0, i, 0)),
                   pl.BlockSpec((8, HY_WIDTH), lambda i: (0, 0))),
        compiler_params=_params(("arbitrary",), 48),
        name="hyena_filter_%d" % L,
    )(*args)
    return h.reshape(2 * l_out, HY_WIDTH), ss


def _stack_bf16(re, im):
    return jnp.concatenate([re, im], axis=0).astype(BF16)


def _filter_fft_kernel(h_ref, ss_ref, m1_ref, f2_ref, o_ref, a_scr):
    R, P = FFT_R, FFT_PITCH
    inv_norm = lax.rsqrt(ss_ref[0:1, :])

    def stage1(n2, carry):
        neg = jnp.where(n2 == 0, 0, R - n2)
        xs = jnp.concatenate([h_ref[pl.ds(n2, R // 2, stride=P), :],
                              h_ref[pl.ds(SEQ_P + neg, R // 2, stride=P), :]], axis=0)
        xs = (xs * inv_norm).astype(BF16)
        a = jnp.dot(m1_ref[n2], xs, preferred_element_type=F32)
        base = pl.multiple_of(n2 * P, 8)
        a_scr[0, pl.ds(base, R), :] = a[:R]
        a_scr[1, pl.ds(base, R), :] = a[R:]
        return carry

    lax.fori_loop(0, R, stage1, 0, unroll=FFT_UNROLL)

    ct = a_scr.shape[-1]

    def stage2(pair, carry):
        k1s = (2 * pair, 2 * pair + 1)
        rhs = jnp.concatenate([_stack_bf16(a_scr[0, pl.ds(k1, R, stride=P), :],
                                           a_scr[1, pl.ds(k1, R, stride=P), :]) for k1 in k1s], axis=1)
        x = jnp.dot(f2_ref[...], rhs, preferred_element_type=F32)
        for d, k1 in enumerate(k1s):
            base = pl.multiple_of(k1 * R, R)
            o_ref[0, pl.ds(base, R), :] = x[:R, d * ct:(d + 1) * ct].astype(BF16)
            o_ref[1, pl.ds(base, R), :] = x[R:, d * ct:(d + 1) * ct].astype(BF16)
        return carry

    lax.fori_loop(0, R // 2, stage2, 0, unroll=FFT_UNROLL // 2)


def _filter_fft(h_circ, ss, m1, f2):
    ct = LANES
    once = pl.Buffered(1)
    return pl.pallas_call(
        _filter_fft_kernel,
        out_shape=jax.ShapeDtypeStruct((2, FFT_N, HY_WIDTH), BF16),
        grid=(HY_WIDTH // ct,),
        in_specs=[
            pl.BlockSpec((2 * SEQ_P, ct), lambda c: (0, c), pipeline_mode=once),
            pl.BlockSpec((8, ct), lambda c: (0, c)),
            pl.BlockSpec(m1.shape, lambda c: (0, 0, 0), pipeline_mode=once),
            pl.BlockSpec(f2.shape, lambda c: (0, 0), pipeline_mode=once),
        ],
        out_specs=pl.BlockSpec((2, FFT_N, ct), lambda c: (0, 0, c)),
        scratch_shapes=[pltpu.VMEM((2, FFT_R * FFT_PITCH, ct), F32)],
        compiler_params=_params(("parallel",), 58),
        name="hyena_filter_fft",
    )(h_circ, ss, m1, f2)


def _fft_conv_kernel(z_ref, hs_ref, m1_ref, f2_ref, g2_ref, m1i_ref, y_ref, a_scr):
    R, P = FFT_R, FFT_PITCH
    ct = a_scr.shape[-1]

    def stage1(n2, carry):
        rows = pl.ds(n2, R // 2, stride=P)
        xs = jnp.concatenate([z_ref[0, rows, :], z_ref[1, rows, :]], axis=1).astype(BF16)
        t = jnp.dot(m1_ref[n2, :, 0:R // 2], xs, preferred_element_type=F32)
        base = pl.multiple_of(n2 * P, 8)
        a_scr[0, pl.ds(base, R), :] = t[:R, :ct] - t[R:, ct:]
        a_scr[1, pl.ds(base, R), :] = t[R:, :ct] + t[:R, ct:]
        return carry

    lax.fori_loop(0, R, stage1, 0, unroll=FFT_UNROLL)

    def stage2(pair, carry):
        k1s = (2 * pair, 2 * pair + 1)
        rhs = jnp.concatenate([_stack_bf16(a_scr[0, pl.ds(k1, R, stride=P), :],
                                           a_scr[1, pl.ds(k1, R, stride=P), :]) for k1 in k1s], axis=1)
        x = jnp.dot(f2_ref[...], rhs, preferred_element_type=F32)
        prods = []
        for d, k1 in enumerate(k1s):
            base = pl.multiple_of(k1 * R, R)
            hr = hs_ref[0, pl.ds(base, R), :].astype(F32)
            hi = hs_ref[1, pl.ds(base, R), :].astype(F32)
            xr, xi = x[:R, d * ct:(d + 1) * ct], x[R:, d * ct:(d + 1) * ct]
            prods.append(_stack_bf16(xr * hr - xi * hi, xr * hi + xi * hr))
        bq = jnp.dot(g2_ref[...], jnp.concatenate(prods, axis=1), preferred_element_type=F32)
        for d, k1 in enumerate(k1s):
            a_scr[0, pl.ds(k1, R, stride=P), :] = bq[:R, d * ct:(d + 1) * ct]
            a_scr[1, pl.ds(k1, R, stride=P), :] = bq[R:, d * ct:(d + 1) * ct]
        return carry

    lax.fori_loop(0, R // 2, stage2, 0, unroll=FFT_UNROLL // 2)

    def stage3(n2, carry):
        base = pl.multiple_of(n2 * P, 8)
        br = a_scr[0, pl.ds(base, R), :]
        bi = a_scr[1, pl.ds(base, R), :]
        rhs = jnp.concatenate([jnp.concatenate([br, bi], axis=1),
                               jnp.concatenate([bi, -br], axis=1)], axis=0).astype(BF16)
        y = jnp.dot(m1i_ref[n2], rhs, preferred_element_type=F32)
        rows = pl.ds(n2, R // 2, stride=P)
        y_ref[0, rows, :] = y[:, :ct]
        y_ref[1, rows, :] = y[:, ct:]
        return carry

    lax.fori_loop(0, R, stage3, 0, unroll=FFT_UNROLL)
    for b in range(2):
        for n1 in range(R // 2):
            y_ref[b, n1 * P + R:(n1 + 1) * P, :] = jnp.zeros((P - R, ct), F32)


def _fft_conv(z, hspec, m1, f2, g2, m1i):
    assert z.shape[0] == 2, "the batch pair is packed into one complex signal"
    ct = LANES
    once = pl.Buffered(1)
    return pl.pallas_call(
        _fft_conv_kernel,
        out_shape=jax.ShapeDtypeStruct((2, SEQ_P, HY_WIDTH), F32),
        grid=(HY_WIDTH // ct,),
        in_specs=[
            pl.BlockSpec((2, SEQ_P, ct), lambda c: (0, 0, c), pipeline_mode=once),
            pl.BlockSpec((2, FFT_N, ct), lambda c: (0, 0, c), pipeline_mode=once),
            pl.BlockSpec(m1.shape, lambda c: (0, 0, 0), pipeline_mode=once),
            pl.BlockSpec(f2.shape, lambda c: (0, 0), pipeline_mode=once),
            pl.BlockSpec(g2.shape, lambda c: (0, 0), pipeline_mode=once),
            pl.BlockSpec(m1i.shape, lambda c: (0, 0, 0), pipeline_mode=once),
        ],
        out_specs=pl.BlockSpec((2, SEQ_P, ct), lambda c: (0, 0, c), pipeline_mode=once),
        scratch_shapes=[pltpu.VMEM((2, FFT_R * FFT_PITCH, ct), F32)],
        compiler_params=_params(("arbitrary",), 58),
        name="hyena_fft_conv",
    )(z, hspec, m1, f2, g2, m1i)


def _ctx_conv_kernel(z_ref, h_ref, ss_ref, fwd_ref, inv_ref, fwd_h_ref, y_ref):
    N = 2 * CTX_LEN
    hn = (h_ref[...] * lax.rsqrt(ss_ref[0:1, :])).astype(BF16)
    hs = jnp.dot(fwd_h_ref[...], hn, preferred_element_type=F32)
    zs = jnp.dot(fwd_ref[:, :CTX_LEN], z_ref[...].astype(BF16), preferred_element_type=F32)
    hr, hi, zr, zi = hs[:N], hs[N:], zs[:N], zs[N:]
    ys = _stack_bf16(zr * hr - zi * hi, zr * hi + zi * hr)
    y_ref[...] = jnp.dot(inv_ref[...], ys, preferred_element_type=F32)


def _ctx_conv(z, h_circ, ss, fwd, inv, fwd_h):
    G = z.shape[0]
    full = lambda a: pl.BlockSpec(a.shape, lambda b: (0,) * a.ndim)
    return pl.pallas_call(
        _ctx_conv_kernel,
        out_shape=jax.ShapeDtypeStruct((G, CTX_LEN, HY_WIDTH), F32),
        grid=(G,),
        in_specs=[pl.BlockSpec((None, CTX_LEN, HY_WIDTH), lambda b: (b, 0, 0)),
                  full(h_circ), full(ss), full(fwd), full(inv), full(fwd_h)],
        out_specs=pl.BlockSpec((None, CTX_LEN, HY_WIDTH), lambda b: (b, 0, 0)),
        compiler_params=_params(("parallel",), 48),
        name="hyena_ctx_conv",
    )(z, h_circ, ss, fwd, inv, fwd_h)


def _kv_group(refs, g):
    cols = slice(g * LANES, (g + 1) * LANES)
    return refs[0][:, cols] if len(refs) == 1 else jnp.concatenate([r[:, cols] for r in refs], axis=0)


def _attn_logits(q_ref, k_refs, g, masks):
    low = lax.broadcasted_iota(jnp.int32, (BLOCK, LANES), 1) < HEAD_DIM
    zero = jnp.zeros((), BF16)
    parts = []
    for hh in range(ATT_GROUP):
        h = g * ATT_GROUP + hh
        tile = q_ref[:, (h // 2) * LANES:(h // 2 + 1) * LANES]
        parts.append(jnp.where(low if h % 2 == 0 else ~low, tile, zero))
    s = lax.dot_general(jnp.concatenate(parts, axis=0), _kv_group(k_refs, g), (((1,), (1,)), ((), ())),
                        preferred_element_type=F32)
    if masks is not None:
        ok_prev, ok_next = masks
        s = jnp.concatenate([jnp.where(ok_prev, s[:, :BLOCK], NEG_INF), s[:, BLOCK:2 * BLOCK],
                             jnp.where(ok_next, s[:, 2 * BLOCK:3 * BLOCK], NEG_INF), s[:, 3 * BLOCK:]], axis=1)
    return s


def _attn_output(s, sink_ref, v_refs, g, o_ref):
    low = lax.broadcasted_iota(jnp.int32, (BLOCK, LANES), 1) < HEAD_DIM
    low4 = jnp.concatenate([low] * ATT_GROUP, axis=0)
    v = _kv_group(v_refs, g)
    v_aug = jnp.where(lax.broadcasted_iota(jnp.int32, v.shape, 1) < HEAD_DIM, v, jnp.ones((), BF16))
    sink = jnp.concatenate(
        [jnp.full((BLOCK, 1), sink_ref[g * ATT_GROUP + hh] * LOG2E, F32) for hh in range(ATT_GROUP)], axis=0)
    m = jnp.maximum(jnp.max(s, axis=-1, keepdims=True), sink)
    e = jnp.exp2(s - m).astype(BF16)
    o = jnp.dot(e, v_aug, preferred_element_type=F32) + jnp.where(low4, 0.0, jnp.exp2(sink - m))
    swapped = pltpu.roll(o, HEAD_DIM, axis=1)
    for pair in range(ATT_GROUP // 2):
        ev = slice((2 * pair) * BLOCK, (2 * pair + 1) * BLOCK)
        od = slice((2 * pair + 1) * BLOCK, (2 * pair + 2) * BLOCK)
        even = o[ev] / swapped[ev]
        odd = swapped[od] / o[od]
        t = g * (ATT_GROUP // 2) + pair
        o_ref[:, t * LANES:(t + 1) * LANES] = jnp.where(low, even, odd).astype(o_ref.dtype)


def _attn_ctx_kernel(sink_ref, q_ref, kx_ref, vx_ref, o_ref):
    for g in range(ATT_KV_HEADS):
        _attn_output(_attn_logits(q_ref, (kx_ref,), g, None), sink_ref, (vx_ref,), g, o_ref)


def _attn_local_kernel(sink_ref, q_ref, kp_ref, kc_ref, kn_ref, kx_ref, vp_ref, vc_ref, vn_ref, vx_ref,
                       o_ref, s_even, s_odd, *, n_blocks):
    n = pl.program_id(1)

    @pl.when(n == 0)
    def _():
        s_odd[...] = jnp.zeros_like(s_odd)

    def step(s_new, s_prev):
        nq = jnp.minimum(n, n_blocks - 1)
        qi = lax.broadcasted_iota(jnp.int32, (ATT_GROUP * BLOCK, BLOCK), 0) % BLOCK
        ki = lax.broadcasted_iota(jnp.int32, (ATT_GROUP * BLOCK, BLOCK), 1)
        masks = ((ki >= qi) & (nq > 0), (ki <= qi) & (nq < n_blocks - 1))
        for g in range(ATT_KV_HEADS):
            s_new[g] = _attn_logits(q_ref, (kp_ref, kc_ref, kn_ref, kx_ref), g, masks)
        for g in range(ATT_KV_HEADS):
            _attn_output(s_prev[g], sink_ref, (vp_ref, vc_ref, vn_ref, vx_ref), g, o_ref)

    @pl.when(n % 2 == 0)
    def _():
        step(s_even, s_odd)

    @pl.when(n % 2 == 1)
    def _():
        step(s_odd, s_even)


def _attention(sink, qkv, qkv_ctx, *, local):
    src = qkv if local else qkv_ctx
    G, R, _ = src.shape
    nb = R // BLOCK
    kcol, vcol = ATT_WIDTH // 256, ATT_WIDTH // 256 + 1
    ctx_k = pl.BlockSpec((None, CTX_LEN, 256), lambda b, n: (b, 0, kcol))
    ctx_v = pl.BlockSpec((None, CTX_LEN, 256), lambda b, n: (b, 0, vcol))
    smem = pl.BlockSpec(memory_space=pltpu.SMEM)
    out_shape = jax.ShapeDtypeStruct((G, R, ATT_WIDTH), BF16)
    if not local:
        rows = pl.BlockSpec((None, BLOCK, ATT_WIDTH), lambda b, n: (b, n, 0))
        return pl.pallas_call(
            _attn_ctx_kernel, out_shape=out_shape, grid=(G, nb),
            in_specs=[smem, rows, ctx_k, ctx_v], out_specs=rows,
            compiler_params=_params(("parallel", "parallel"), 48), name="attn_ctx",
        )(sink, qkv_ctx, qkv_ctx, qkv_ctx)

    q_blk = lambda n: jnp.minimum(n, nb - 1)
    o_blk = lambda n: jnp.maximum(n - 1, 0)

    def nbr(col, blk, off):
        return pl.BlockSpec((None, BLOCK, 256), lambda b, n: (b, jnp.clip(blk(n) + off, 0, nb - 1), col))

    in_specs = [smem, pl.BlockSpec((None, BLOCK, ATT_WIDTH), lambda b, n: (b, q_blk(n), 0)),
                nbr(kcol, q_blk, -1), nbr(kcol, q_blk, 0), nbr(kcol, q_blk, 1), ctx_k,
                nbr(vcol, o_blk, -1), nbr(vcol, o_blk, 0), nbr(vcol, o_blk, 1), ctx_v]
    logits_scratch = pltpu.VMEM((ATT_KV_HEADS, ATT_GROUP * BLOCK, 3 * BLOCK + CTX_LEN), F32)
    return pl.pallas_call(
        functools.partial(_attn_local_kernel, n_blocks=nb),
        out_shape=out_shape,
        grid=(G, nb + 1),
        in_specs=in_specs,
        out_specs=pl.BlockSpec((None, BLOCK, ATT_WIDTH), lambda b, n: (b, o_blk(n), 0)),
        scratch_shapes=[logits_scratch, logits_scratch],
        compiler_params=_params(("parallel", "arbitrary"), 48),
        name="attn_local",
    )(sink, qkv, qkv, qkv, qkv, qkv_ctx, qkv, qkv, qkv, qkv_ctx)


def _mixer_out0(rows, x0_ref, y_ref, zb_ref, att_ref, wo_ref):
    if y_ref.shape[0] != x0_ref.shape[0]:
        y = jnp.concatenate([y_ref[r // FFT_R * FFT_PITCH:r // FFT_R * FFT_PITCH + FFT_R, :]
                             for r in range(rows.start, rows.stop, FFT_R)], axis=0)
    else:
        y = y_ref[rows, :]
    hy = (x0_ref[rows, :] * y + zb_ref[rows, :]).astype(BF16)
    return (jnp.dot(hy, wo_ref[:HY_WIDTH, :], preferred_element_type=F32)
            + jnp.dot(att_ref[rows, :], wo_ref[HY_WIDTH:, :], preferred_element_type=F32))


def _mixer_out1(rows, of_ref, ob_ref, wo_ref):
    a = (of_ref[rows, :].astype(F32) + ob_ref[rows, :].astype(F32)).astype(BF16)
    return jnp.dot(a, wo_ref[...], preferred_element_type=F32)


def _stream_cast(src_hbm, dst, stage, sems):
    rows = stage.shape[1]
    n = src_hbm.shape[0] // rows

    def copy(c):
        return pltpu.make_async_copy(src_hbm.at[pl.ds(c * rows, rows), :], stage.at[c % 2], sems.at[c % 2])

    copy(0).start()
    for c in range(n):
        if c + 1 < n:
            copy(c + 1).start()
        copy(c).wait()
        dst[c * rows:(c + 1) * rows, :] = stage[c % 2].astype(BF16)


def _export(srcs, dsts, sems):
    copies = [pltpu.make_async_copy(s, d, sems.at[k]) for k, (s, d) in enumerate(zip(srcs, dsts))]
    for cp in copies:
        cp.start()
    for cp in copies:
        cp.wait()


def _mix_kernel(*refs, mixer_out, n_mix, row, fc, final_norm, stream, export):
    x_ref = refs[0]
    mix_refs = refs[1:1 + n_mix]
    wo_ref, gta_ref, gm_ref, shm_ref, scm_ref, gtm_ref, w1_ref, w2_ref, fg_ref, o_ref = refs[1 + n_mix:11 + n_mix]
    rest = refs[11 + n_mix:]
    if stream:
        exports, rest = (rest[:3], rest[3:]) if export else ((), rest)
        a_scr, wo_s, w1_s, w2_s, stage_wide, stage_tall, sems = rest

        @pl.when((pl.program_id(0) == 0) & (pl.program_id(1) == 0))
        def _():
            _stream_cast(w1_ref, w1_s, stage_wide, sems)
            _stream_cast(w2_ref, w2_s, stage_tall, sems)
            _stream_cast(wo_ref, wo_s, stage_tall, sems)
            if export:
                _export((wo_s, w1_s, w2_s), exports, sems)

        wo_ref, w1_ref, w2_ref = wo_s, w1_s, w2_s
    else:
        a_scr, = rest
    mix_refs = mix_refs + (wo_ref,)
    r = pl.program_id(0) if row is None else row
    gta, gtm = _mod_row(gta_ref, r), _mod_row(gtm_ref, r)
    gm, shm, scm = gm_ref[...], _mod_row(shm_ref, r), _mod_row(scm_ref, r)
    sub = min(x_ref.shape[0], SUB_ROWS)
    for s in range(x_ref.shape[0] // sub):
        rows = slice(s * sub, (s + 1) * sub)
        x1 = x_ref[rows, :] + gta * mixer_out(rows, *mix_refs)
        h = _norm_mod(x1, gm, shm, scm).astype(BF16)
        for c in range(D_FF // fc):
            a = jnp.maximum(jnp.dot(h, w1_ref[:, c * fc:(c + 1) * fc], preferred_element_type=F32), 0.0)
            a_scr[rows, c * fc:(c + 1) * fc] = (a * a).astype(BF16)
        out = x1 + gtm * jnp.dot(a_scr[rows, :], w2_ref[...], preferred_element_type=F32)
        if final_norm:
            out = (out * lax.rsqrt(jnp.mean(out * out, axis=-1, keepdims=True) + EPS)) * fg_ref[...]
        o_ref[rows, :] = out


def _mix_mlp(kind, x, mix_in, wo, mod, gm, w1, w2, fg, layer, *, is_ctx, tm, fc, final_norm, export=False):
    G, R, _ = x.shape
    stream = w1.dtype == F32
    row_spec = lambda w: pl.BlockSpec((None, tm, w), lambda b, i: (b, i, 0))
    modk = lambda k: pl.BlockSpec((None, 8, D_MODEL), lambda b, i: (layer, 0, k))
    vec = pl.BlockSpec((1, D_MODEL), lambda b, i: (0, 0))
    hbm = pl.BlockSpec(memory_space=pl.ANY)
    resident = lambda a: hbm if stream else pl.BlockSpec(a.shape, lambda b, i: (0, 0),
                                                         pipeline_mode=pl.Buffered(1))
    if kind == 0:
        mixer_out = _mixer_out0
        y_rows = tm if mix_in[1].shape[1] == R else tm // FFT_R * FFT_PITCH
        y_spec = pl.BlockSpec((None, y_rows, HY_WIDTH), lambda b, i: (b, i, 0))
        mix_specs = [row_spec(HY_WIDTH), y_spec, row_spec(HY_WIDTH), row_spec(ATT_WIDTH)]
    else:
        mixer_out = _mixer_out1
        mix_specs = [row_spec(RET_V)] * 2
    kern = functools.partial(_mix_kernel, mixer_out=mixer_out, n_mix=len(mix_in), row=2 if is_ctx else None,
                             fc=fc, final_norm=final_norm, stream=stream, export=export)
    out_shape = [jax.ShapeDtypeStruct((G, R, D_MODEL), F32)]
    out_specs = [row_spec(D_MODEL)]
    scratch = [pltpu.VMEM((tm, D_FF), BF16)]
    if stream:
        weights = (wo, w1, w2)
        if export:
            out_shape += [jax.ShapeDtypeStruct(w.shape, BF16) for w in weights]
            out_specs += [hbm] * 3
        scratch += [pltpu.VMEM(w.shape, BF16) for w in weights]
        scratch += [pltpu.VMEM((2, STAGE_BYTES // (4 * D_FF), D_FF), F32),
                    pltpu.VMEM((2, STAGE_BYTES // (4 * D_MODEL), D_MODEL), F32),
                    pltpu.SemaphoreType.DMA((3,))]
    res = pl.pallas_call(
        kern,
        out_shape=out_shape,
        grid=(G, R // tm),
        in_specs=[row_spec(D_MODEL)] + mix_specs + [
            resident(wo), modk(2), vec, modk(3), modk(4), modk(5), resident(w1), resident(w2), vec,
        ],
        out_specs=out_specs,
        scratch_shapes=scratch,
        compiler_params=_params(("arbitrary", "arbitrary") if stream else ("parallel", "parallel"), 56),
        name="mix_mlp%d%s" % (kind, "_ctx" if is_ctx else ""),
    )(x, *mix_in, wo, mod, gm, mod, mod, mod, w1, w2, fg)
    return res if export else res[0]


def _ret_kernel(lr_ref, qkv_c, qkv_f, g_f, qkv_b, g_b, cos_ref, sin_ref, of_ref, ob_ref,
                state, dmask, xi, zeta, gch):
    j = pl.program_id(1)
    C = RET_C
    kscale = RET_DK ** -0.5

    @pl.when(j == 0)
    def _():
        state[...] = jnp.zeros_like(state)
        row = lax.broadcasted_iota(jnp.int32, (C, C), 0).astype(F32)
        col = lax.broadcasted_iota(jnp.int32, (C, C), 1).astype(F32)
        for d in range(2):
            e = row - col if d == 0 else col - row
            p = row if d == 0 else (C - 1) - row
            for h in range(RET_HEADS):
                lg = -jnp.exp(jnp.full((C, C), lr_ref[d, h], F32))
                dmask[d, h] = jnp.where(e >= 0, jnp.exp(e * lg), 0.0) * kscale
                xi[d, h] = jnp.exp((p + 1.0) * lg)
                zeta[d, h] = jnp.exp(((C - 1) - p) * lg) * kscale
                gch[d, h] = jnp.exp(C * -jnp.exp(jnp.full((8, C), lr_ref[d, h], F32)))

    half = RET_DK // 2

    def chain(d, h, src_ref, g_ref, o_ref):
        cos, sin = cos_ref[d], sin_ref[d]

        def rot(c0):
            t1 = src_ref[:, c0:c0 + half].astype(F32)
            t2 = src_ref[:, c0 + half:c0 + RET_DK].astype(F32)
            return jnp.concatenate([t1 * cos - t2 * sin, t2 * cos + t1 * sin], axis=1)

        q = rot(h * RET_DK)
        k = rot(RET_QK + h * RET_DK)
        v = src_ref[:, 2 * RET_QK + h * RET_DV:2 * RET_QK + (h + 1) * RET_DV]
        inner = lax.dot_general(q.astype(BF16), k.astype(BF16), (((1,), (1,)), ((), ())),
                                preferred_element_type=F32) * dmask[d, h]
        s_old = state[d, h]
        kv = lax.dot_general((k * zeta[d, h]).astype(BF16), v, (((0,), (0,)), ((), ())),
                             preferred_element_type=F32)
        state[d, h] = gch[d, h, 0:1, 0:1] * s_old + kv
        if o_ref is not None:
            o = (jnp.dot(inner.astype(BF16), v, preferred_element_type=F32)
                 + jnp.dot((q * xi[d, h]).astype(BF16), s_old.astype(BF16), preferred_element_type=F32))
            on = o * lax.rsqrt(jnp.mean(o * o, axis=-1, keepdims=True) + EPS)
            gate = g_ref[:, h * RET_DV:(h + 1) * RET_DV]
            half_gate = gate * 0.5
            silu = half_gate + half_gate * jnp.tanh(half_gate)
            o_ref[:, h * RET_DV:(h + 1) * RET_DV] = (silu.astype(F32) * on).astype(o_ref.dtype)

    @pl.when(j == 0)
    def _():
        for d in range(2):
            for h in range(RET_HEADS):
                chain(d, h, qkv_c, None, None)

    @pl.when(j > 0)
    def _():
        for h in range(RET_HEADS):
            chain(0, h, qkv_f, g_f, of_ref)
            chain(1, h, qkv_b, g_b, ob_ref)


def _retention(log_rate, p_ctx, p_lat):
    G = p_lat.shape[0]
    C = RET_C
    n_lat = SEQ // C
    cos, sin = _rope1d_tables()
    cos, sin = jnp.asarray(cos), jnp.asarray(sin)

    def fwd_chunk(j):
        return jnp.maximum(j - 1, 0)

    def bwd_chunk(j):
        return n_lat - 1 - jnp.maximum(j - 1, 0)

    qkv_w = 2 * RET_QK + RET_V
    gcol = qkv_w // RET_V
    tab = pl.BlockSpec((2, C, RET_DK // 2), lambda b, j: (0, j, 0))
    in_specs = [
        pl.BlockSpec(memory_space=pltpu.SMEM),
        pl.BlockSpec((None, C, qkv_w), lambda b, j: (b, 0, 0)),
        pl.BlockSpec((None, C, qkv_w), lambda b, j: (b, fwd_chunk(j), 0)),
        pl.BlockSpec((None, C, RET_V), lambda b, j: (b, fwd_chunk(j), gcol)),
        pl.BlockSpec((None, C, qkv_w), lambda b, j: (b, bwd_chunk(j), 0)),
        pl.BlockSpec((None, C, RET_V), lambda b, j: (b, bwd_chunk(j), gcol + 1)),
        tab, tab,
    ]
    out = jax.ShapeDtypeStruct((G, SEQ, RET_V), BF16)
    per_chain = lambda *tail: pltpu.VMEM((2, RET_HEADS) + tail, F32)
    return pl.pallas_call(
        _ret_kernel,
        out_shape=(out, out),
        grid=(G, 1 + n_lat),
        in_specs=in_specs,
        out_specs=(pl.BlockSpec((None, C, RET_V), lambda b, j: (b, fwd_chunk(j), 0)),
                   pl.BlockSpec((None, C, RET_V), lambda b, j: (b, bwd_chunk(j), 0))),
        scratch_shapes=[per_chain(RET_DK, RET_DV), per_chain(C, C), per_chain(C, C), per_chain(C, C),
                        per_chain(8, C)],
        compiler_params=_params(("parallel", "arbitrary"), 48),
        name="retention",
    )(log_rate, p_ctx, p_lat, p_lat, p_lat, p_lat, cos, sin)


def _ev_weight(w_in):
    i_q = EV_U
    i_k = i_q + ATT_WIDTH
    i_v = i_k + KV_WIDTH
    cols = [w_in[:, :i_q], w_in[:, i_q:i_k] * (HEAD_DIM ** -0.5 * LOG2E)]
    for base in (i_k, i_v):
        for g in range(ATT_KV_HEADS):
            part = w_in[:, base + g * HEAD_DIM: base + (g + 1) * HEAD_DIM]
            cols += [part, part]
    return jnp.concatenate(cols, axis=1).astype(BF16)


def kernel(x, c, ctx, c_ctx, ada_w, ada_b, norm_mix_g, norm_mlp_g, mlp_w1, mlp_w2, ev_w_in, ev_w_out, hy_conv_w, hy_conv_b, hy_w1, hy_b1, hy_w2, hy_b2, hy_w3, hy_freq, hy_decay, hy_bias, attn_sink, od_w_in, od_w_out, ret_log_rate, final_g):
    D = D_MODEL
    cvec = jnp.concatenate([c, c_ctx[None, :], jnp.zeros((8 - BATCH - 1, D), F32)], axis=0)
    mod = _ada(cvec, ada_w, ada_b)

    m1, m1i, f2, g2 = (jnp.asarray(a).astype(BF16) for a in _fft_mats())
    cfwd, cinv, cfwd_h = (jnp.asarray(a).astype(BF16) for a in _ctx_fft_mats())
    fg = final_g.reshape(1, D)

    gmix = norm_mix_g[0].reshape(1, D)
    gmlp = norm_mlp_g[0].reshape(1, D)
    w_in = _ev_weight(ev_w_in[0])
    conv = (hy_conv_w[0], hy_conv_b[0].reshape(1, EV_U), hy_bias[0].reshape(1, HY_WIDTH))
    z_l, x0_l, zb_l, qkv_l = _inproj0(x, gmix, mod, w_in, *conv, 0, is_ctx=False, tm=512)
    z_c, x0_c, zb_c, qkv_c = _inproj0(ctx, gmix, mod, w_in, *conv, 0, is_ctx=True, tm=CTX_LEN)

    filt = (hy_w1[0], hy_b1[0], hy_w2[0], hy_b2[0], hy_w3[0], hy_freq[0], hy_decay[0])
    h_l, ss_l = _hyena_filter(*filt, L=SEQ)
    h_c, ss_c = _hyena_filter(*filt, L=CTX_LEN)
    hspec = _filter_fft(h_l, ss_l, m1, f2)
    y_l = _fft_conv(z_l, hspec, m1, f2, g2, m1i)
    y_c = _ctx_conv(z_c, h_c, ss_c, cfwd, cinv, cfwd_h)

    sink = attn_sink[0]
    att_l = _attention(sink, qkv_l, qkv_c, local=True)
    att_c = _attention(sink, qkv_l, qkv_c, local=False)

    x1, wo, w1, w2 = _mix_mlp(0, x, (x0_l, y_l, zb_l, att_l), ev_w_out[0], mod, gmlp, mlp_w1[0], mlp_w2[0], fg, 0,
                              is_ctx=False, tm=512, fc=512, final_norm=False, export=True)
    ctx1 = _mix_mlp(0, ctx, (x0_c, y_c, zb_c, att_c), wo, mod, gmlp, w1, w2, fg, 0,
                    is_ctx=True, tm=CTX_LEN, fc=512, final_norm=False)

    gmix = norm_mix_g[1].reshape(1, D)
    gmlp = norm_mlp_g[1].reshape(1, D)
    p_l, w_in = _inproj1(x1, gmix, mod, od_w_in[0], 1, is_ctx=False, tm=512, tn=512)
    p_c = _inproj1(ctx1, gmix, mod, w_in, 1, is_ctx=True, tm=CTX_LEN, tn=512)
    o_fwd, o_bwd = _retention(ret_log_rate[0], p_c, p_l)

    return _mix_mlp(1, x1, (o_fwd, o_bwd), od_w_out[0], mod, gmlp, mlp_w1[1], mlp_w2[1], fg, 1,
                    is_ctx=False, tm=512, fc=512, final_norm=True)
```

```python
import functools
import math

import numpy as np
import jax
import jax.numpy as jnp
from jax import lax
from jax.experimental import pallas as pl
from jax.experimental.pallas import tpu as pltpu

F32 = jnp.float32
BF16 = jnp.bfloat16
HIGHEST = lax.Precision.HIGHEST

D_MODEL = 1024
BATCH = 2
SEQ = 8192
DEPTH = 2
GRID_W = 64
CTX_LEN = 256
EPS = 1e-6
NEG_INF = -1e30
N_MOD = 6
D_FF = 4 * D_MODEL
ROPE_BASE = 10000.0

HY_WIDTH = D_MODEL // 2
HY_EMB = 33
HY_BANDS = (HY_EMB - 1) // 2
HY_HIDDEN = 64

ATT_HEADS = 8
ATT_KV_HEADS = 2
ATT_GROUP = ATT_HEADS // ATT_KV_HEADS
HEAD_DIM = 64
ATT_WIDTH = ATT_HEADS * HEAD_DIM
KV_WIDTH = ATT_KV_HEADS * HEAD_DIM
BLOCK = 128

RET_HEADS = 4
RET_DK = D_MODEL // RET_HEADS
RET_DV = 2 * RET_DK
RET_QK = RET_HEADS * RET_DK
RET_V = RET_HEADS * RET_DV
OD_IN = 2 * RET_QK + 3 * RET_V

LOG2E = 1.4426950408889634
LANES = 128
MIB = 1024 * 1024

EV_U = 3 * HY_WIDTH
EV_QKV = ATT_WIDTH + 4 * LANES
EV_COLS = EV_U + EV_QKV

FFT_N = 2 * SEQ
FFT_R = 128
FFT_PITCH = FFT_R + 8
SEQ_P = SEQ // FFT_R * FFT_PITCH
CONV_ROWS = 64
CONV_PAD = 8
SUB_ROWS = 256
FFT_UNROLL = 16
RET_C = 256


def _params(sem, vmem_mib):
    return pltpu.CompilerParams(dimension_semantics=sem, vmem_limit_bytes=vmem_mib * MIB)


@functools.lru_cache(maxsize=None)
def _rope2d_tables():
    quarter = HEAD_DIM // 4
    inv = ROPE_BASE ** (-np.arange(quarter, dtype=np.float64) / quarter)
    t = np.arange(SEQ)
    pos = np.stack([t // GRID_W, t % GRID_W], axis=1).astype(np.float64)
    lane = np.arange(HEAD_DIM)
    half = lane // (HEAD_DIM // 2)
    e = lane % (HEAD_DIM // 2)
    ang = pos[:, half] * inv[e % quarter][None, :]
    sign = np.where(e < quarter, -1.0, 1.0)[None, :]
    cos = np.tile(np.cos(ang), (1, 2)).astype(np.float32)
    sin = np.tile(np.sin(ang) * sign, (1, 2)).astype(np.float32)
    return cos, sin


@functools.lru_cache(maxsize=None)
def _rope1d_tables():
    n = RET_DK // 2
    inv = ROPE_BASE ** (-np.linspace(0.0, 1.0, n))
    pos = np.arange(CTX_LEN + SEQ, dtype=np.float64)
    ang = pos[:, None] * inv[None, :]
    cos, sin = np.cos(ang), np.sin(ang)

    def rev(a):
        return a.reshape(-1, RET_C, n)[:, ::-1].reshape(-1, n)

    cos2 = np.stack([cos, rev(cos)]).astype(np.float32)
    sin2 = np.stack([sin, rev(sin)]).astype(np.float32)
    return cos2, sin2


@functools.lru_cache(maxsize=None)
def _fft_mats():
    N, R = FFT_N, FFT_R
    k1 = np.arange(R)
    n2 = np.arange(R)[:, None, None]
    n1 = np.arange(R // 2)[None, None, :]
    n = np.concatenate([n2 + R * n1, (N - ((R - n2) % R + R * n1)) % N], axis=2)
    idx = (k1[None, :, None] * n) % N
    ang = 2.0 * np.pi * idx / N
    c, s = np.cos(ang), np.sin(ang)
    m1 = np.concatenate([c, -s], axis=1).astype(np.float32)
    m1i = np.concatenate([c.transpose(0, 2, 1), -s.transpose(0, 2, 1)], axis=2)
    m1i = m1i[:, :R // 2].astype(np.float32)
    a2 = 2.0 * np.pi * ((np.arange(R)[:, None] * np.arange(R)[None, :]) % R) / R
    fr, fi = np.cos(a2), -np.sin(a2)
    f2 = np.block([[fr, -fi], [fi, fr]]).astype(np.float32)
    g2 = (np.block([[fr, fi], [-fi, fr]]) / N).astype(np.float32)
    return m1, m1i, f2, g2


@functools.lru_cache(maxsize=None)
def _ctx_fft_mats():
    N = 2 * CTX_LEN
    k = np.arange(N)[:, None]
    n = np.arange(N)[None, :]
    ang = 2.0 * np.pi * ((k * n) % N) / N
    c, s = np.cos(ang), np.sin(ang)
    fwd = np.concatenate([c, -s], axis=0).astype(np.float32)
    inv = (np.concatenate([c, -s], axis=1)[:CTX_LEN] / N).astype(np.float32)
    src = np.concatenate([np.arange(CTX_LEN), (N - np.arange(CTX_LEN)) % N])
    return fwd, inv, fwd[:, src]


def _ada_kernel(c_ref, w_ref, b_ref, o_ref):
    c = c_ref[...]
    a = c * (1.0 / (1.0 + jnp.exp(-c)))
    o_ref[...] = _dot3(a, w_ref[...]) + b_ref[...]


def _ada(cvec, ada_w, ada_b):
    tn = 1536
    return pl.pallas_call(
        _ada_kernel,
        out_shape=jax.ShapeDtypeStruct((DEPTH, 8, N_MOD * D_MODEL), F32),
        grid=(DEPTH, N_MOD * D_MODEL // tn),
        in_specs=[
            pl.BlockSpec((8, D_MODEL), lambda i, j: (0, 0)),
            pl.BlockSpec((None, D_MODEL, tn), lambda i, j: (i, 0, j)),
            pl.BlockSpec((None, 1, tn), lambda i, j: (i, 0, j)),
        ],
        out_specs=pl.BlockSpec((None, 8, tn), lambda i, j: (i, 0, j)),
        compiler_params=_params(("parallel", "parallel"), 40),
        name="ada_mod",
    )(cvec, ada_w, ada_b.reshape(DEPTH, 1, N_MOD * D_MODEL))


def _mod_row(ref, row):
    if isinstance(row, int):
        return ref[row:row + 1, :]
    return ref[pl.ds(row, 1), :]


def _norm_mod(x, g, shift, scale):
    y = x * lax.rsqrt(jnp.mean(x * x, axis=-1, keepdims=True) + EPS)
    return (y * g) * (1.0 + scale) + shift


def _rope_tile(x, cos, sin_signed):
    lane = lax.broadcasted_iota(jnp.int32, x.shape, 1)
    first = (lane % 32) < 16
    partner = jnp.where(first, pltpu.roll(x, LANES - 16, axis=1), pltpu.roll(x, 16, axis=1))
    return x * cos + partner * sin_signed


def _project0(x_ref, g, sh, sc, w_ref, cos_ref, sin_ref, u_ref, qkv_ref, rope):
    tn = 512
    n_rot = (ATT_WIDTH + 2 * LANES) // LANES
    sub = min(x_ref.shape[0], SUB_ROWS)
    for s in range(x_ref.shape[0] // sub):
        rows = slice(s * sub, (s + 1) * sub)
        urows = slice(CONV_PAD + s * sub, CONV_PAD + (s + 1) * sub)
        h = _norm_mod(x_ref[rows, :], g, sh, sc).astype(BF16)
        for j in range(EV_U // tn):
            u_ref[urows, j * tn:(j + 1) * tn] = jnp.dot(h, w_ref[:, j * tn:(j + 1) * tn],
                                                        preferred_element_type=F32)
        for j in range(EV_QKV // tn):
            y = jnp.dot(h, w_ref[:, EV_U + j * tn:EV_U + (j + 1) * tn], preferred_element_type=F32)
            for t in range(tn // LANES):
                yt = y[:, t * LANES:(t + 1) * LANES]
                if rope and j * (tn // LANES) + t < n_rot:
                    yt = _rope_tile(yt, cos_ref[rows, :], sin_ref[rows, :])
                qkv_ref[rows, j * tn + t * LANES:j * tn + (t + 1) * LANES] = yt.astype(BF16)


def _short_conv_gate(u_ref, cw_ref, cb_ref, bias_ref, z_ref, x0_ref, zb_ref):
    tm = u_ref.shape[0] - 2 * CONV_PAD
    rc = CONV_ROWS
    z_pitched = z_ref.shape[0] != tm

    def conv(r0, c0):
        cols = slice(c0, c0 + LANES)
        taps = [u_ref[CONV_PAD - 1 + k + r0:CONV_PAD - 1 + k + r0 + rc, cols] for k in range(3)]
        return (taps[0] * cw_ref[0:1, cols] + taps[1] * cw_ref[1:2, cols] + taps[2] * cw_ref[2:3, cols]
                + cb_ref[:, cols])

    for r0 in range(0, tm, rc):
        rows = slice(r0, r0 + rc)
        z0 = r0 // FFT_R * FFT_PITCH + r0 % FFT_R if z_pitched else r0
        for c0 in range(0, HY_WIDTH, LANES):
            cols = slice(c0, c0 + LANES)
            x0 = conv(r0, c0)
            z = conv(r0, 2 * HY_WIDTH + c0) * conv(r0, HY_WIDTH + c0)
            z_ref[z0:z0 + rc, cols] = z
            x0_ref[rows, cols] = x0
            zb_ref[rows, cols] = x0 * (z * bias_ref[:, cols])
    if z_pitched:
        for k in range(tm // FFT_R):
            z_ref[k * FFT_PITCH + FFT_R:(k + 1) * FFT_PITCH, :] = jnp.zeros((FFT_PITCH - FFT_R, HY_WIDTH), F32)


def _inproj0_kernel(x_ref, g_ref, sh_ref, sc_ref, w_ref, cos_ref, sin_ref, cw_ref, cb_ref, bias_ref,
                    z_ref, x0_ref, zb_ref, qkv_ref, u_even, u_odd, *, row, rope, n_tiles):
    r = pl.program_id(0) if row is None else row
    i = pl.program_id(1)
    g, sh, sc = g_ref[...], _mod_row(sh_ref, r), _mod_row(sc_ref, r)
    conv_refs = (cw_ref, cb_ref, bias_ref, z_ref, x0_ref, zb_ref)
    zero_row = jnp.zeros((1, EV_U), F32)
    tm = x_ref.shape[0]
    above, first, last, below = CONV_PAD - 1, CONV_PAD, CONV_PAD + tm - 1, CONV_PAD + tm

    if n_tiles == 1:
        u_even[above:first, :] = zero_row
        u_even[below:below + 1, :] = zero_row
        _project0(x_ref, g, sh, sc, w_ref, cos_ref, sin_ref, u_even, qkv_ref, rope)
        _short_conv_gate(u_even, *conv_refs)
        return

    @pl.when(i == 0)
    def _():
        u_even[...] = jnp.zeros_like(u_even)
        u_odd[...] = jnp.zeros_like(u_odd)

    def step(u_new, u_old):
        j = i - 1
        u_old[above:first, :] = jnp.where(j > 0, u_new[last:last + 1, :], zero_row)
        _project0(x_ref, g, sh, sc, w_ref, cos_ref, sin_ref, u_new, qkv_ref, rope)
        u_old[below:below + 1, :] = jnp.where(j < n_tiles - 1, u_new[first:first + 1, :], zero_row)
        _short_conv_gate(u_old, *conv_refs)

    @pl.when(i % 2 == 0)
    def _():
        step(u_even, u_odd)

    @pl.when(i % 2 == 1)
    def _():
        step(u_odd, u_even)


def _inproj0(x, g, mod, w, conv_w, conv_b, bias, layer, *, is_ctx, tm):
    G, R, _ = x.shape
    n_tiles = R // tm
    n_steps = n_tiles + (1 if n_tiles > 1 else 0)
    cos, sin = _rope2d_tables()
    cos, sin = jnp.asarray(cos), jnp.asarray(sin)
    proj_tile = lambda i: jnp.minimum(i, n_tiles - 1)
    conv_tile = (lambda i: jnp.maximum(i - 1, 0)) if n_tiles > 1 else (lambda i: i)
    if is_ctx:
        tab = pl.BlockSpec((tm, LANES), lambda b, i: (0, 0))
    else:
        tab = pl.BlockSpec((tm, LANES), lambda b, i: (proj_tile(i), 0))
    const = lambda a: pl.BlockSpec(a.shape, lambda b, i: (0, 0))
    hy_out = jax.ShapeDtypeStruct((G, R, HY_WIDTH), F32)
    hy_spec = pl.BlockSpec((None, tm, HY_WIDTH), lambda b, i: (b, conv_tile(i), 0))
    if R == SEQ:
        z_out = jax.ShapeDtypeStruct((G, SEQ_P, HY_WIDTH), F32)
        z_spec = pl.BlockSpec((None, tm // FFT_R * FFT_PITCH, HY_WIDTH), lambda b, i: (b, conv_tile(i), 0))
    else:
        z_out, z_spec = hy_out, hy_spec
    kern = functools.partial(_inproj0_kernel, row=2 if is_ctx else None, rope=not is_ctx, n_tiles=n_tiles)
    return pl.pallas_call(
        kern,
        out_shape=(z_out, hy_out, hy_out, jax.ShapeDtypeStruct((G, R, EV_QKV), BF16)),
        grid=(G, n_steps),
        in_specs=[
            pl.BlockSpec((None, tm, D_MODEL), lambda b, i: (b, proj_tile(i), 0)),
            pl.BlockSpec((1, D_MODEL), lambda b, i: (0, 0)),
            pl.BlockSpec((None, 8, D_MODEL), lambda b, i: (layer, 0, 0)),
            pl.BlockSpec((None, 8, D_MODEL), lambda b, i: (layer, 0, 1)),
            pl.BlockSpec((D_MODEL, EV_COLS), lambda b, i: (0, 0), pipeline_mode=pl.Buffered(1)),
            tab, tab, const(conv_w), const(conv_b), const(bias),
        ],
        out_specs=(z_spec, hy_spec, hy_spec,
                   pl.BlockSpec((None, tm, EV_QKV), lambda b, i: (b, proj_tile(i), 0))),
        scratch_shapes=[pltpu.VMEM((tm + 2 * CONV_PAD, EV_U), F32)] * 2,
        compiler_params=_params(("parallel", "arbitrary"), 48),
        name="inproj0_ctx" if is_ctx else "inproj0",
    )(x, g, mod, mod, w, cos, sin, conv_w, conv_b, bias)


def _inproj1_kernel(x_ref, g_ref, sh_ref, sc_ref, w_ref, o_ref, *, row, tn):
    r = pl.program_id(0) if row is None else row
    g, sh, sc = g_ref[...], _mod_row(sh_ref, r), _mod_row(sc_ref, r)
    sub = min(x_ref.shape[0], SUB_ROWS)
    for s in range(x_ref.shape[0] // sub):
        rows = slice(s * sub, (s + 1) * sub)
        h = _norm_mod(x_ref[rows, :], g, sh, sc).astype(BF16)
        for j in range(w_ref.shape[1] // tn):
            cols = slice(j * tn, (j + 1) * tn)
            o_ref[rows, cols] = jnp.dot(h, w_ref[:, cols], preferred_element_type=F32).astype(o_ref.dtype)


def _inproj1(x, g, mod, w, layer, *, is_ctx, tm, tn):
    G, R, _ = x.shape
    N = w.shape[1]
    kern = functools.partial(_inproj1_kernel, row=2 if is_ctx else None, tn=tn)
    return pl.pallas_call(
        kern,
        out_shape=jax.ShapeDtypeStruct((G, R, N), BF16),
        grid=(G, R // tm),
        in_specs=[
            pl.BlockSpec((None, tm, D_MODEL), lambda b, i: (b, i, 0)),
            pl.BlockSpec((1, D_MODEL), lambda b, i: (0, 0)),
            pl.BlockSpec((None, 8, D_MODEL), lambda b, i: (layer, 0, 0)),
            pl.BlockSpec((None, 8, D_MODEL), lambda b, i: (layer, 0, 1)),
            pl.BlockSpec((D_MODEL, N), lambda b, i: (0, 0), pipeline_mode=pl.Buffered(1)),
        ],
        out_specs=pl.BlockSpec((None, tm, N), lambda b, i: (b, i, 0)),
        compiler_params=_params(("parallel", "parallel"), 48),
        name="inproj1_ctx" if is_ctx else "inproj1",
    )(x, g, mod, mod, w)


def _split_bf16(a):
    hi = a.astype(BF16)
    return hi, (a - hi.astype(F32)).astype(BF16)


def _dot3_split(a_hi, a_lo, b):
    b_hi, b_lo = _split_bf16(b)
    dot = functools.partial(jnp.dot, preferred_element_type=F32)
    return dot(a_hi, b_hi) + (dot(a_lo, b_hi) + dot(a_hi, b_lo))


def _dot3(a, b):
    return _dot3_split(*_split_bf16(a), b)


def _store_pitched(ref, lead, value):
    for k in range(value.shape[0] // FFT_R):
        ref[lead, k * FFT_PITCH:k * FFT_PITCH + FFT_R, :] = value[k * FFT_R:(k + 1) * FFT_R]
        ref[lead, k * FFT_PITCH + FFT_R:(k + 1) * FFT_PITCH, :] = jnp.zeros(
            (FFT_PITCH - FFT_R, value.shape[1]), value.dtype)


def _filter_kernel(w1t_ref, b1_ref, w2t_ref, b2_ref, fr_ref, w3_ref, dec_ref, bands_ref, h_ref, ss_ref, *,
                   L, tm, pitched):
    i = pl.program_id(0)
    t_row = (lax.broadcasted_iota(jnp.int32, (1, tm), 1) + i * tm).astype(F32) / L
    ang = (2.0 * math.pi * t_row) * bands_ref[...]
    t8 = jnp.where(lax.broadcasted_iota(jnp.int32, (8, tm), 0) == 0, t_row, 0.0)
    feat = jnp.concatenate([t8, jnp.cos(ang), -jnp.sin(ang)], axis=0)
    fr = fr_ref[...]
    hid = jnp.sin(fr * (jnp.dot(w1t_ref[...], feat, precision=HIGHEST, preferred_element_type=F32) + b1_ref[...]))
    hid = jnp.sin(fr * (jnp.dot(w2t_ref[...], hid, precision=HIGHEST, preferred_element_type=F32) + b2_ref[...]))
    hid_hi, hid_lo = _split_bf16(hid.T)
    m_col = lax.broadcasted_iota(jnp.int32, (tm, 1), 0) + i * tm
    t_col = m_col.astype(F32) / L
    energy = jnp.zeros((1, HY_WIDTH), F32)
    for side in range(2):
        h = _dot3_split(hid_hi, hid_lo, w3_ref[side]) * jnp.exp(-t_col * jnp.abs(dec_ref[side]))
        if side == 1:
            h = jnp.where(m_col == 0, 0.0, h)
        if pitched:
            _store_pitched(h_ref, side, h)
        else:
            h_ref[side] = h
        energy = energy + jnp.sum(h * h, axis=0, keepdims=True)

    @pl.when(i == 0)
    def _():
        ss_ref[...] = jnp.zeros_like(ss_ref)

    ss_ref[...] += jnp.broadcast_to(energy, ss_ref.shape)


def _hyena_filter(w1, b1, w2, b2, w3, freq, decay, *, L):
    tm = min(L, 1024)
    pitched = L == SEQ
    tm_out, l_out = (tm // FFT_R * FFT_PITCH, SEQ_P) if pitched else (tm, L)
    kern = functools.partial(_filter_kernel, L=L, tm=tm, pitched=pitched)
    w1t = jnp.concatenate([w1[0:1], jnp.zeros((7, HY_HIDDEN), F32), w1[1:]], axis=0).T
    col = lambda a: a.reshape(HY_HIDDEN, 1)
    w3s = jnp.stack([w3[:, :HY_WIDTH], w3[:, HY_WIDTH:]])
    bands = jnp.asarray(np.linspace(1e-4, HY_BANDS - 1, HY_BANDS).astype(np.float32).reshape(HY_BANDS, 1))
    args = (w1t, col(b1), w2.T, col(b2), col(freq), w3s, decay.reshape(2, 1, HY_WIDTH), bands)
    full = lambda a: pl.BlockSpec(a.shape, lambda i: (0,) * a.ndim)
    h, ss = pl.pallas_call(
        kern,
        out_shape=(jax.ShapeDtypeStruct((2, l_out, HY_WIDTH), F32), jax.ShapeDtypeStruct((8, HY_WIDTH), F32)),
        grid=(L // tm,),
        in_specs=[full(a) for a in args],
        out_specs=(pl.BlockSpec((2, tm_out, HY_WIDTH), lambda i: (0, i, 0)),
                   pl.BlockSpec((8, HY_WIDTH), lambda i: (0, 0))),
        compiler_params=_params(("arbitrary",), 48),
        name="hyena_filter_%d" % L,
    )(*args)
    return h.reshape(2 * l_out, HY_WIDTH), ss


def _stack_bf16(re, im):
    return jnp.concatenate([re, im], axis=0).astype(BF16)


def _filter_fft_kernel(h_ref, ss_ref, m1_ref, f2_ref, o_ref, a_scr):
    R, P = FFT_R, FFT_PITCH
    inv_norm = lax.rsqrt(ss_ref[0:1, :])

    def stage1(n2, carry):
        neg = jnp.where(n2 == 0, 0, R - n2)
        xs = jnp.concatenate([h_ref[pl.ds(n2, R // 2, stride=P), :],
                              h_ref[pl.ds(SEQ_P + neg, R // 2, stride=P), :]], axis=0)
        xs = (xs * inv_norm).astype(BF16)
        a = jnp.dot(m1_ref[n2], xs, preferred_element_type=F32)
        base = pl.multiple_of(n2 * P, 8)
        a_scr[0, pl.ds(base, R), :] = a[:R]
        a_scr[1, pl.ds(base, R), :] = a[R:]
        return carry

    lax.fori_loop(0, R, stage1, 0, unroll=FFT_UNROLL)

    ct = a_scr.shape[-1]

    def stage2(pair, carry):
        k1s = (2 * pair, 2 * pair + 1)
        rhs = jnp.concatenate([_stack_bf16(a_scr[0, pl.ds(k1, R, stride=P), :],
                                           a_scr[1, pl.ds(k1, R, stride=P), :]) for k1 in k1s], axis=1)
        x = jnp.dot(f2_ref[...], rhs, preferred_element_type=F32)
        for d, k1 in enumerate(k1s):
            base = pl.multiple_of(k1 * R, R)
            o_ref[0, pl.ds(base, R), :] = x[:R, d * ct:(d + 1) * ct].astype(BF16)
            o_ref[1, pl.ds(base, R), :] = x[R:, d * ct:(d + 1) * ct].astype(BF16)
        return carry

    lax.fori_loop(0, R // 2, stage2, 0, unroll=FFT_UNROLL // 2)


def _filter_fft(h_circ, ss, m1, f2):
    ct = LANES
    once = pl.Buffered(1)
    return pl.pallas_call(
        _filter_fft_kernel,
        out_shape=jax.ShapeDtypeStruct((2, FFT_N, HY_WIDTH), BF16),
        grid=(HY_WIDTH // ct,),
        in_specs=[
            pl.BlockSpec((2 * SEQ_P, ct), lambda c: (0, c), pipeline_mode=once),
            pl.BlockSpec((8, ct), lambda c: (0, c)),
            pl.BlockSpec(m1.shape, lambda c: (0, 0, 0), pipeline_mode=once),
            pl.BlockSpec(f2.shape, lambda c: (0, 0), pipeline_mode=once),
        ],
        out_specs=pl.BlockSpec((2, FFT_N, ct), lambda c: (0, 0, c)),
        scratch_shapes=[pltpu.VMEM((2, FFT_R * FFT_PITCH, ct), F32)],
        compiler_params=_params(("parallel",), 58),
        name="hyena_filter_fft",
    )(h_circ, ss, m1, f2)


def _fft_conv_kernel(z_ref, hs_ref, m1_ref, f2_ref, g2_ref, m1i_ref, y_ref, a_scr):
    R, P = FFT_R, FFT_PITCH
    ct = a_scr.shape[-1]

    def stage1(n2, carry):
        rows = pl.ds(n2, R // 2, stride=P)
        xs = jnp.concatenate([z_ref[0, rows, :], z_ref[1, rows, :]], axis=1).astype(BF16)
        t = jnp.dot(m1_ref[n2, :, 0:R // 2], xs, preferred_element_type=F32)
        base = pl.multiple_of(n2 * P, 8)
        a_scr[0, pl.ds(base, R), :] = t[:R, :ct] - t[R:, ct:]
        a_scr[1, pl.ds(base, R), :] = t[R:, :ct] + t[:R, ct:]
        return carry

    lax.fori_loop(0, R, stage1, 0, unroll=FFT_UNROLL)

    def stage2(pair, carry):
        k1s = (2 * pair, 2 * pair + 1)
        rhs = jnp.concatenate([_stack_bf16(a_scr[0, pl.ds(k1, R, stride=P), :],
                                           a_scr[1, pl.ds(k1, R, stride=P), :]) for k1 in k1s], axis=1)
        x = jnp.dot(f2_ref[...], rhs, preferred_element_type=F32)
        prods = []
        for d, k1 in enumerate(k1s):
            base = pl.multiple_of(k1 * R, R)
            hr = hs_ref[0, pl.ds(base, R), :].astype(F32)
            hi = hs_ref[1, pl.ds(base, R), :].astype(F32)
            xr, xi = x[:R, d * ct:(d + 1) * ct], x[R:, d * ct:(d + 1) * ct]
            prods.append(_stack_bf16(xr * hr - xi * hi, xr * hi + xi * hr))
        bq = jnp.dot(g2_ref[...], jnp.concatenate(prods, axis=1), preferred_element_type=F32)
        for d, k1 in enumerate(k1s):
            a_scr[0, pl.ds(k1, R, stride=P), :] = bq[:R, d * ct:(d + 1) * ct]
            a_scr[1, pl.ds(k1, R, stride=P), :] = bq[R:, d * ct:(d + 1) * ct]
        return carry

    lax.fori_loop(0, R // 2, stage2, 0, unroll=FFT_UNROLL // 2)

    def stage3(n2, carry):
        base = pl.multiple_of(n2 * P, 8)
        br = a_scr[0, pl.ds(base, R), :]
        bi = a_scr[1, pl.ds(base, R), :]
        rhs = jnp.concatenate([jnp.concatenate([br, bi], axis=1),
                               jnp.concatenate([bi, -br], axis=1)], axis=0).astype(BF16)
        y = jnp.dot(m1i_ref[n2], rhs, preferred_element_type=F32)
        rows = pl.ds(n2, R // 2, stride=P)
        y_ref[0, rows, :] = y[:, :ct]
        y_ref[1, rows, :] = y[:, ct:]
        return carry

    lax.fori_loop(0, R, stage3, 0, unroll=FFT_UNROLL)
    for b in range(2):
        for n1 in range(R // 2):
            y_ref[b, n1 * P + R:(n1 + 1) * P, :] = jnp.zeros((P - R, ct), F32)


def _fft_conv(z, hspec, m1, f2, g2, m1i):
    assert z.shape[0] == 2, "the batch pair is packed into one complex signal"
    ct = LANES
    once = pl.Buffered(1)
    return pl.pallas_call(
        _fft_conv_kernel,
        out_shape=jax.ShapeDtypeStruct((2, SEQ_P, HY_WIDTH), F32),
        grid=(HY_WIDTH // ct,),
        in_specs=[
            pl.BlockSpec((2, SEQ_P, ct), lambda c: (0, 0, c), pipeline_mode=once),
            pl.BlockSpec((2, FFT_N, ct), lambda c: (0, 0, c), pipeline_mode=once),
            pl.BlockSpec(m1.shape, lambda c: (0, 0, 0), pipeline_mode=once),
            pl.BlockSpec(f2.shape, lambda c: (0, 0), pipeline_mode=once),
            pl.BlockSpec(g2.shape, lambda c: (0, 0), pipeline_mode=once),
            pl.BlockSpec(m1i.shape, lambda c: (0, 0, 0), pipeline_mode=once),
        ],
        out_specs=pl.BlockSpec((2, SEQ_P, ct), lambda c: (0, 0, c), pipeline_mode=once),
        scratch_shapes=[pltpu.VMEM((2, FFT_R * FFT_PITCH, ct), F32)],
        compiler_params=_params(("arbitrary",), 58),
        name="hyena_fft_conv",
    )(z, hspec, m1, f2, g2, m1i)


def _ctx_conv_kernel(z_ref, h_ref, ss_ref, fwd_ref, inv_ref, fwd_h_ref, y_ref):
    N = 2 * CTX_LEN
    hn = (h_ref[...] * lax.rsqrt(ss_ref[0:1, :])).astype(BF16)
    hs = jnp.dot(fwd_h_ref[...], hn, preferred_element_type=F32)
    zs = jnp.dot(fwd_ref[:, :CTX_LEN], z_ref[...].astype(BF16), preferred_element_type=F32)
    hr, hi, zr, zi = hs[:N], hs[N:], zs[:N], zs[N:]
    ys = _stack_bf16(zr * hr - zi * hi, zr * hi + zi * hr)
    y_ref[...] = jnp.dot(inv_ref[...], ys, preferred_element_type=F32)


def _ctx_conv(z, h_circ, ss, fwd, inv, fwd_h):
    G = z.shape[0]
    full = lambda a: pl.BlockSpec(a.shape, lambda b: (0,) * a.ndim)
    return pl.pallas_call(
        _ctx_conv_kernel,
        out_shape=jax.ShapeDtypeStruct((G, CTX_LEN, HY_WIDTH), F32),
        grid=(G,),
        in_specs=[pl.BlockSpec((None, CTX_LEN, HY_WIDTH), lambda b: (b, 0, 0)),
                  full(h_circ), full(ss), full(fwd), full(inv), full(fwd_h)],
        out_specs=pl.BlockSpec((None, CTX_LEN, HY_WIDTH), lambda b: (b, 0, 0)),
        compiler_params=_params(("parallel",), 48),
        name="hyena_ctx_conv",
    )(z, h_circ, ss, fwd, inv, fwd_h)


def _kv_group(refs, g, col0):
    cols = slice(col0 + g * LANES, col0 + (g + 1) * LANES)
    return refs[0][:, cols] if len(refs) == 1 else jnp.concatenate([r[:, cols] for r in refs], axis=0)


def _attn_logits(q_ref, k_refs, g, masks, col0=0):
    low = lax.broadcasted_iota(jnp.int32, (BLOCK, LANES), 1) < HEAD_DIM
    zero = jnp.zeros((), BF16)
    parts = []
    for hh in range(ATT_GROUP):
        h = g * ATT_GROUP + hh
        tile = q_ref[:, (h // 2) * LANES:(h // 2 + 1) * LANES]
        parts.append(jnp.where(low if h % 2 == 0 else ~low, tile, zero))
    s = lax.dot_general(jnp.concatenate(parts, axis=0), _kv_group(k_refs, g, col0), (((1,), (1,)), ((), ())),
                        preferred_element_type=F32)
    if masks is not None:
        ok_prev, ok_next = masks
        s = jnp.concatenate([jnp.where(ok_prev, s[:, :BLOCK], NEG_INF), s[:, BLOCK:2 * BLOCK],
                             jnp.where(ok_next, s[:, 2 * BLOCK:3 * BLOCK], NEG_INF), s[:, 3 * BLOCK:]], axis=1)
    return s


def _attn_output(s, sink_ref, v_refs, g, o_ref, col0=0):
    low = lax.broadcasted_iota(jnp.int32, (BLOCK, LANES), 1) < HEAD_DIM
    low4 = jnp.concatenate([low] * ATT_GROUP, axis=0)
    v = _kv_group(v_refs, g, col0)
    v_aug = jnp.where(lax.broadcasted_iota(jnp.int32, v.shape, 1) < HEAD_DIM, v, jnp.ones((), BF16))
    sink = jnp.concatenate(
        [jnp.full((BLOCK, 1), sink_ref[g * ATT_GROUP + hh] * LOG2E, F32) for hh in range(ATT_GROUP)], axis=0)
    m = jnp.maximum(jnp.max(s, axis=-1, keepdims=True), sink)
    e = jnp.exp2(s - m).astype(BF16)
    o = jnp.dot(e, v_aug, preferred_element_type=F32) + jnp.where(low4, 0.0, jnp.exp2(sink - m))
    swapped = pltpu.roll(o, HEAD_DIM, axis=1)
    for pair in range(ATT_GROUP // 2):
        ev = slice((2 * pair) * BLOCK, (2 * pair + 1) * BLOCK)
        od = slice((2 * pair + 1) * BLOCK, (2 * pair + 2) * BLOCK)
        even = o[ev] / swapped[ev]
        odd = swapped[od] / o[od]
        t = g * (ATT_GROUP // 2) + pair
        o_ref[:, t * LANES:(t + 1) * LANES] = jnp.where(low, even, odd).astype(o_ref.dtype)


def _attn_ctx_kernel(sink_ref, q_ref, kx_ref, vx_ref, o_ref):
    for g in range(ATT_KV_HEADS):
        _attn_output(_attn_logits(q_ref, (kx_ref,), g, None), sink_ref, (vx_ref,), g, o_ref)


def _attn_local_kernel(sink_ref, q_ref, kv_m2, kv_m1, kv_0, kv_p1, kv_ctx, o_ref, s_even, s_odd, *, n_blocks):
    n = pl.program_id(1)
    k_refs, v_refs, v_col = (kv_m1, kv_0, kv_p1, kv_ctx), (kv_m2, kv_m1, kv_0, kv_ctx), 2 * LANES

    @pl.when(n == 0)
    def _():
        s_odd[...] = jnp.zeros_like(s_odd)

    def step(s_new, s_prev):
        nq = jnp.minimum(n, n_blocks - 1)
        qi = lax.broadcasted_iota(jnp.int32, (ATT_GROUP * BLOCK, BLOCK), 0) % BLOCK
        ki = lax.broadcasted_iota(jnp.int32, (ATT_GROUP * BLOCK, BLOCK), 1)
        masks = ((ki >= qi) & (nq > 0), (ki <= qi) & (nq < n_blocks - 1))
        for g in range(ATT_KV_HEADS):
            s_new[g] = _attn_logits(q_ref, k_refs, g, masks)
        for g in range(ATT_KV_HEADS):
            _attn_output(s_prev[g], sink_ref, v_refs, g, o_ref, v_col)

    @pl.when(n % 2 == 0)
    def _():
        step(s_even, s_odd)

    @pl.when(n % 2 == 1)
    def _():
        step(s_odd, s_even)


def _attention(sink, qkv, qkv_ctx, *, local):
    src = qkv if local else qkv_ctx
    G, R, _ = src.shape
    nb = R // BLOCK
    kcol, vcol = ATT_WIDTH // 256, ATT_WIDTH // 256 + 1
    ctx_k = pl.BlockSpec((None, CTX_LEN, 256), lambda b, n: (b, 0, kcol))
    ctx_v = pl.BlockSpec((None, CTX_LEN, 256), lambda b, n: (b, 0, vcol))
    smem = pl.BlockSpec(memory_space=pltpu.SMEM)
    out_shape = jax.ShapeDtypeStruct((G, R, ATT_WIDTH), BF16)
    if not local:
        rows = pl.BlockSpec((None, BLOCK, ATT_WIDTH), lambda b, n: (b, n, 0))
        return pl.pallas_call(
            _attn_ctx_kernel, out_shape=out_shape, grid=(G, nb),
            in_specs=[smem, rows, ctx_k, ctx_v], out_specs=rows,
            compiler_params=_params(("parallel", "parallel"), 48), name="attn_ctx",
        )(sink, qkv_ctx, qkv_ctx, qkv_ctx)

    q_blk = lambda n: jnp.minimum(n, nb - 1)
    o_blk = lambda n: jnp.maximum(n - 1, 0)

    kv_w = EV_QKV - ATT_WIDTH
    kv_col = ATT_WIDTH // kv_w

    def kv_rows(off):
        return pl.BlockSpec((None, BLOCK, kv_w), lambda b, n: (b, jnp.clip(n + off, 0, nb - 1), kv_col))

    in_specs = [smem, pl.BlockSpec((None, BLOCK, ATT_WIDTH), lambda b, n: (b, q_blk(n), 0)),
                kv_rows(-2), kv_rows(-1), kv_rows(0), kv_rows(1),
                pl.BlockSpec((None, CTX_LEN, kv_w), lambda b, n: (b, 0, kv_col))]
    logits_scratch = pltpu.VMEM((ATT_KV_HEADS, ATT_GROUP * BLOCK, 3 * BLOCK + CTX_LEN), F32)
    return pl.pallas_call(
        functools.partial(_attn_local_kernel, n_blocks=nb),
        out_shape=out_shape,
        grid=(G, nb + 1),
        in_specs=in_specs,
        out_specs=pl.BlockSpec((None, BLOCK, ATT_WIDTH), lambda b, n: (b, o_blk(n), 0)),
        scratch_shapes=[logits_scratch, logits_scratch],
        compiler_params=_params(("parallel", "arbitrary"), 48),
        name="attn_local",
    )(sink, qkv, qkv, qkv, qkv, qkv, qkv_ctx)


def _mixer_out0(rows, x0_ref, y_ref, zb_ref, att_ref, wo_ref):
    if y_ref.shape[0] != x0_ref.shape[0]:
        y = jnp.concatenate([y_ref[r // FFT_R * FFT_PITCH:r // FFT_R * FFT_PITCH + FFT_R, :]
                             for r in range(rows.start, rows.stop, FFT_R)], axis=0)
    else:
        y = y_ref[rows, :]
    hy = (x0_ref[rows, :] * y + zb_ref[rows, :]).astype(BF16)
    return (jnp.dot(hy, wo_ref[:HY_WIDTH, :], preferred_element_type=F32)
            + jnp.dot(att_ref[rows, :], wo_ref[HY_WIDTH:, :], preferred_element_type=F32))


def _mixer_out1(rows, of_ref, ob_ref, wo_ref):
    a = (of_ref[rows, :].astype(F32) + ob_ref[rows, :].astype(F32)).astype(BF16)
    return jnp.dot(a, wo_ref[...], preferred_element_type=F32)


def _mix_kernel(*refs, mixer_out, n_mix, row, fc, final_norm):
    x_ref = refs[0]
    mix_refs = refs[1:2 + n_mix]
    gta_ref, gm_ref, shm_ref, scm_ref, gtm_ref, w1_ref, w2_ref, fg_ref, o_ref, a_scr = refs[2 + n_mix:]
    r = pl.program_id(0) if row is None else row
    gta, gtm = _mod_row(gta_ref, r), _mod_row(gtm_ref, r)
    gm, shm, scm = gm_ref[...], _mod_row(shm_ref, r), _mod_row(scm_ref, r)
    sub = min(x_ref.shape[0], SUB_ROWS)
    for s in range(x_ref.shape[0] // sub):
        rows = slice(s * sub, (s + 1) * sub)
        x1 = x_ref[rows, :] + gta * mixer_out(rows, *mix_refs)
        h = _norm_mod(x1, gm, shm, scm).astype(BF16)
        for c in range(D_FF // fc):
            a = jnp.maximum(jnp.dot(h, w1_ref[:, c * fc:(c + 1) * fc], preferred_element_type=F32), 0.0)
            a_scr[rows, c * fc:(c + 1) * fc] = (a * a).astype(BF16)
        out = x1 + gtm * jnp.dot(a_scr[rows, :], w2_ref[...], preferred_element_type=F32)
        if final_norm:
            out = (out * lax.rsqrt(jnp.mean(out * out, axis=-1, keepdims=True) + EPS)) * fg_ref[...]
        o_ref[rows, :] = out


def _mix_mlp(kind, x, mix_in, wo, mod, gm, w1, w2, fg, layer, *, is_ctx, tm, fc, final_norm):
    G, R, _ = x.shape
    row_spec = lambda w: pl.BlockSpec((None, tm, w), lambda b, i: (b, i, 0))
    modk = lambda k: pl.BlockSpec((None, 8, D_MODEL), lambda b, i: (layer, 0, k))
    vec = pl.BlockSpec((1, D_MODEL), lambda b, i: (0, 0))
    resident = lambda a: pl.BlockSpec(a.shape, lambda b, i: (0, 0), pipeline_mode=pl.Buffered(1))
    if kind == 0:
        mixer_out = _mixer_out0
        y_rows = tm if mix_in[1].shape[1] == R else tm // FFT_R * FFT_PITCH
        y_spec = pl.BlockSpec((None, y_rows, HY_WIDTH), lambda b, i: (b, i, 0))
        mix_specs = [row_spec(HY_WIDTH), y_spec, row_spec(HY_WIDTH), row_spec(ATT_WIDTH)]
    else:
        mixer_out = _mixer_out1
        mix_specs = [row_spec(RET_V)] * 2
    kern = functools.partial(_mix_kernel, mixer_out=mixer_out, n_mix=len(mix_in), row=2 if is_ctx else None,
                             fc=fc, final_norm=final_norm)
    return pl.pallas_call(
        kern,
        out_shape=jax.ShapeDtypeStruct((G, R, D_MODEL), F32),
        grid=(G, R // tm),
        in_specs=[row_spec(D_MODEL)] + mix_specs + [
            resident(wo), modk(2), vec, modk(3), modk(4), modk(5), resident(w1), resident(w2), vec,
        ],
        out_specs=row_spec(D_MODEL),
        scratch_shapes=[pltpu.VMEM((tm, D_FF), BF16)],
        compiler_params=_params(("parallel", "parallel"), 56),
        name="mix_mlp%d%s" % (kind, "_ctx" if is_ctx else ""),
    )(x, *mix_in, wo, mod, gm, mod, mod, mod, w1, w2, fg)


def _ret_kernel(lr_ref, qkv_c, qkv_f, g_f, qkv_b, g_b, cos_ref, sin_ref, of_ref, ob_ref,
                state, dmask, xi, zeta, gch):
    j = pl.program_id(1)
    C = RET_C
    kscale = RET_DK ** -0.5

    @pl.when(j == 0)
    def _():
        state[...] = jnp.zeros_like(state)
        row = lax.broadcasted_iota(jnp.int32, (C, C), 0).astype(F32)
        col = lax.broadcasted_iota(jnp.int32, (C, C), 1).astype(F32)
        for d in range(2):
            e = row - col if d == 0 else col - row
            p = row if d == 0 else (C - 1) - row
            for h in range(RET_HEADS):
                lg = -jnp.exp(jnp.full((C, C), lr_ref[d, h], F32))
                dmask[d, h] = jnp.where(e >= 0, jnp.exp(e * lg), 0.0) * kscale
                xi[d, h] = jnp.exp((p + 1.0) * lg)
                zeta[d, h] = jnp.exp(((C - 1) - p) * lg) * kscale
                gch[d, h] = jnp.exp(C * -jnp.exp(jnp.full((8, C), lr_ref[d, h], F32)))

    half = RET_DK // 2

    def chain(d, h, src_ref, g_ref, o_ref):
        cos, sin = cos_ref[d], sin_ref[d]

        def rot(c0):
            t1 = src_ref[:, c0:c0 + half].astype(F32)
            t2 = src_ref[:, c0 + half:c0 + RET_DK].astype(F32)
            return jnp.concatenate([t1 * cos - t2 * sin, t2 * cos + t1 * sin], axis=1)

        q = rot(h * RET_DK)
        k = rot(RET_QK + h * RET_DK)
        v = src_ref[:, 2 * RET_QK + h * RET_DV:2 * RET_QK + (h + 1) * RET_DV]
        inner = lax.dot_general(q.astype(BF16), k.astype(BF16), (((1,), (1,)), ((), ())),
                                preferred_element_type=F32) * dmask[d, h]
        s_old = state[d, h]
        kv = lax.dot_general((k * zeta[d, h]).astype(BF16), v, (((0,), (0,)), ((), ())),
                             preferred_element_type=F32)
        state[d, h] = gch[d, h, 0:1, 0:1] * s_old + kv
        if o_ref is not None:
            o = (jnp.dot(inner.astype(BF16), v, preferred_element_type=F32)
                 + jnp.dot((q * xi[d, h]).astype(BF16), s_old.astype(BF16), preferred_element_type=F32))
            on = o * lax.rsqrt(jnp.mean(o * o, axis=-1, keepdims=True) + EPS)
            gate = g_ref[:, h * RET_DV:(h + 1) * RET_DV]
            half_gate = gate * 0.5
            silu = half_gate + half_gate * jnp.tanh(half_gate)
            o_ref[:, h * RET_DV:(h + 1) * RET_DV] = (silu.astype(F32) * on).astype(o_ref.dtype)

    @pl.when(j == 0)
    def _():
        for d in range(2):
            for h in range(RET_HEADS):
                chain(d, h, qkv_c, None, None)

    @pl.when(j > 0)
    def _():
        for h in range(RET_HEADS):
            chain(0, h, qkv_f, g_f, of_ref)
            chain(1, h, qkv_b, g_b, ob_ref)


def _retention(log_rate, p_ctx, p_lat):
    G = p_lat.shape[0]
    C = RET_C
    n_lat = SEQ // C
    cos, sin = _rope1d_tables()
    cos, sin = jnp.asarray(cos), jnp.asarray(sin)

    def fwd_chunk(j):
        return jnp.maximum(j - 1, 0)

    def bwd_chunk(j):
        return n_lat - 1 - jnp.maximum(j - 1, 0)

    qkv_w = 2 * RET_QK + RET_V
    gcol = qkv_w // RET_V
    tab = pl.BlockSpec((2, C, RET_DK // 2), lambda b, j: (0, j, 0))
    in_specs = [
        pl.BlockSpec(memory_space=pltpu.SMEM),
        pl.BlockSpec((None, C, qkv_w), lambda b, j: (b, 0, 0)),
        pl.BlockSpec((None, C, qkv_w), lambda b, j: (b, fwd_chunk(j), 0)),
        pl.BlockSpec((None, C, RET_V), lambda b, j: (b, fwd_chunk(j), gcol)),
        pl.BlockSpec((None, C, qkv_w), lambda b, j: (b, bwd_chunk(j), 0)),
        pl.BlockSpec((None, C, RET_V), lambda b, j: (b, bwd_chunk(j), gcol + 1)),
        tab, tab,
    ]
    out = jax.ShapeDtypeStruct((G, SEQ, RET_V), BF16)
    per_chain = lambda *tail: pltpu.VMEM((2, RET_HEADS) + tail, F32)
    return pl.pallas_call(
        _ret_kernel,
        out_shape=(out, out),
        grid=(G, 1 + n_lat),
        in_specs=in_specs,
        out_specs=(pl.BlockSpec((None, C, RET_V), lambda b, j: (b, fwd_chunk(j), 0)),
                   pl.BlockSpec((None, C, RET_V), lambda b, j: (b, bwd_chunk(j), 0))),
        scratch_shapes=[per_chain(RET_DK, RET_DV), per_chain(C, C), per_chain(C, C), per_chain(C, C),
                        per_chain(8, C)],
        compiler_params=_params(("parallel", "arbitrary"), 48),
        name="retention",
    )(log_rate, p_ctx, p_lat, p_lat, p_lat, p_lat, cos, sin)


def _ev_weight(w_in):
    i_q = EV_U
    i_k = i_q + ATT_WIDTH
    i_v = i_k + KV_WIDTH
    cols = [w_in[:, :i_q], w_in[:, i_q:i_k] * (HEAD_DIM ** -0.5 * LOG2E)]
    for base in (i_k, i_v):
        for g in range(ATT_KV_HEADS):
            part = w_in[:, base + g * HEAD_DIM: base + (g + 1) * HEAD_DIM]
            cols += [part, part]
    return jnp.concatenate(cols, axis=1).astype(BF16)


def kernel(x, c, ctx, c_ctx, ada_w, ada_b, norm_mix_g, norm_mlp_g, mlp_w1, mlp_w2, ev_w_in, ev_w_out, hy_conv_w, hy_conv_b, hy_w1, hy_b1, hy_w2, hy_b2, hy_w3, hy_freq, hy_decay, hy_bias, attn_sink, od_w_in, od_w_out, ret_log_rate, final_g):
    D = D_MODEL
    cvec = jnp.concatenate([c, c_ctx[None, :], jnp.zeros((8 - BATCH - 1, D), F32)], axis=0)
    mod = _ada(cvec, ada_w, ada_b)

    m1, m1i, f2, g2 = (jnp.asarray(a).astype(BF16) for a in _fft_mats())
    cfwd, cinv, cfwd_h = (jnp.asarray(a).astype(BF16) for a in _ctx_fft_mats())
    fg = final_g.reshape(1, D)

    gmix = norm_mix_g[0].reshape(1, D)
    gmlp = norm_mlp_g[0].reshape(1, D)
    w_in = _ev_weight(ev_w_in[0])
    conv = (hy_conv_w[0], hy_conv_b[0].reshape(1, EV_U), hy_bias[0].reshape(1, HY_WIDTH))
    z_l, x0_l, zb_l, qkv_l = _inproj0(x, gmix, mod, w_in, *conv, 0, is_ctx=False, tm=512)
    z_c, x0_c, zb_c, qkv_c = _inproj0(ctx, gmix, mod, w_in, *conv, 0, is_ctx=True, tm=CTX_LEN)

    filt = (hy_w1[0], hy_b1[0], hy_w2[0], hy_b2[0], hy_w3[0], hy_freq[0], hy_decay[0])
    h_l, ss_l = _hyena_filter(*filt, L=SEQ)
    h_c, ss_c = _hyena_filter(*filt, L=CTX_LEN)
    hspec = _filter_fft(h_l, ss_l, m1, f2)
    y_l = _fft_conv(z_l, hspec, m1, f2, g2, m1i)
    y_c = _ctx_conv(z_c, h_c, ss_c, cfwd, cinv, cfwd_h)

    sink = attn_sink[0]
    att_l = _attention(sink, qkv_l, qkv_c, local=True)
    att_c = _attention(sink, qkv_l, qkv_c, local=False)

    wo = ev_w_out[0].astype(BF16)
    w1 = mlp_w1[0].astype(BF16)
    w2 = mlp_w2[0].astype(BF16)
    x1 = _mix_mlp(0, x, (x0_l, y_l, zb_l, att_l), wo, mod, gmlp, w1, w2, fg, 0,
                  is_ctx=False, tm=512, fc=512, final_norm=False)
    ctx1 = _mix_mlp(0, ctx, (x0_c, y_c, zb_c, att_c), wo, mod, gmlp, w1, w2, fg, 0,
                    is_ctx=True, tm=CTX_LEN, fc=512, final_norm=False)

    gmix = norm_mix_g[1].reshape(1, D)
    gmlp = norm_mlp_g[1].reshape(1, D)
    w_in = od_w_in[0].astype(BF16)
    p_l = _inproj1(x1, gmix, mod, w_in, 1, is_ctx=False, tm=512, tn=512)
    p_c = _inproj1(ctx1, gmix, mod, w_in, 1, is_ctx=True, tm=CTX_LEN, tn=512)
    o_fwd, o_bwd = _retention(ret_log_rate[0], p_c, p_l)

    wo = od_w_out[0].astype(BF16)
    w1 = mlp_w1[1].astype(BF16)
    w2 = mlp_w2[1].astype(BF16)
    return _mix_mlp(1, x1, (o_fwd, o_bwd), wo, mod, gmlp, w1, w2, fg, 1,
                    is_ctx=False, tm=512, fc=512, final_norm=True)
```

```python
import functools
import math

import numpy as np
import jax
import jax.numpy as jnp
from jax import lax
from jax.experimental import pallas as pl
from jax.experimental.pallas import tpu as pltpu

F32 = jnp.float32
BF16 = jnp.bfloat16
HIGHEST = lax.Precision.HIGHEST

D_MODEL = 1024
BATCH = 2
SEQ = 8192
DEPTH = 2
GRID_W = 64
CTX_LEN = 256
EPS = 1e-6
NEG_INF = -1e30
N_MOD = 6
D_FF = 4 * D_MODEL
ROPE_BASE = 10000.0

HY_WIDTH = D_MODEL // 2
HY_EMB = 33
HY_BANDS = (HY_EMB - 1) // 2
HY_HIDDEN = 64

ATT_HEADS = 8
ATT_KV_HEADS = 2
ATT_GROUP = ATT_HEADS // ATT_KV_HEADS
HEAD_DIM = 64
ATT_WIDTH = ATT_HEADS * HEAD_DIM
KV_WIDTH = ATT_KV_HEADS * HEAD_DIM
BLOCK = 128

RET_HEADS = 4
RET_DK = D_MODEL // RET_HEADS
RET_DV = 2 * RET_DK
RET_QK = RET_HEADS * RET_DK
RET_V = RET_HEADS * RET_DV
OD_IN = 2 * RET_QK + 3 * RET_V

LOG2E = 1.4426950408889634
LANES = 128
MIB = 1024 * 1024

EV_U = 3 * HY_WIDTH
EV_QKV = ATT_WIDTH + 4 * LANES
EV_COLS = EV_U + EV_QKV

FFT_N = 2 * SEQ
FFT_R = 128
FFT_PITCH = FFT_R + 8
SEQ_P = SEQ // FFT_R * FFT_PITCH
CONV_ROWS = 64
CONV_PAD = 8
SUB_ROWS = 256
FFT_UNROLL = 16
RET_C = 256
CAST_CHUNKS = 32


def _params(sem, vmem_mib):
    return pltpu.CompilerParams(dimension_semantics=sem, vmem_limit_bytes=vmem_mib * MIB)


@functools.lru_cache(maxsize=None)
def _rope2d_tables():
    quarter = HEAD_DIM // 4
    inv = ROPE_BASE ** (-np.arange(quarter, dtype=np.float64) / quarter)
    t = np.arange(SEQ)
    pos = np.stack([t // GRID_W, t % GRID_W], axis=1).astype(np.float64)
    lane = np.arange(HEAD_DIM)
    half = lane // (HEAD_DIM // 2)
    e = lane % (HEAD_DIM // 2)
    ang = pos[:, half] * inv[e % quarter][None, :]
    sign = np.where(e < quarter, -1.0, 1.0)[None, :]
    cos = np.tile(np.cos(ang), (1, 2)).astype(np.float32)
    sin = np.tile(np.sin(ang) * sign, (1, 2)).astype(np.float32)
    return cos, sin


@functools.lru_cache(maxsize=None)
def _rope1d_tables():
    n = RET_DK // 2
    inv = ROPE_BASE ** (-np.linspace(0.0, 1.0, n))
    pos = np.arange(CTX_LEN + SEQ, dtype=np.float64)
    ang = pos[:, None] * inv[None, :]
    cos, sin = np.cos(ang), np.sin(ang)

    def rev(a):
        return a.reshape(-1, RET_C, n)[:, ::-1].reshape(-1, n)

    cos2 = np.stack([cos, rev(cos)]).astype(np.float32)
    sin2 = np.stack([sin, rev(sin)]).astype(np.float32)
    return cos2, sin2


@functools.lru_cache(maxsize=None)
def _fft_mats():
    N, R = FFT_N, FFT_R
    k1 = np.arange(R)
    n2 = np.arange(R)[:, None, None]
    n1 = np.arange(R // 2)[None, None, :]
    n = np.concatenate([n2 + R * n1, (N - ((R - n2) % R + R * n1)) % N], axis=2)
    idx = (k1[None, :, None] * n) % N
    ang = 2.0 * np.pi * idx / N
    c, s = np.cos(ang), np.sin(ang)
    m1 = np.concatenate([c, -s], axis=1).astype(np.float32)
    m1i = np.concatenate([c.transpose(0, 2, 1), -s.transpose(0, 2, 1)], axis=2)
    m1i = m1i[:, :R // 2].astype(np.float32)
    a2 = 2.0 * np.pi * ((np.arange(R)[:, None] * np.arange(R)[None, :]) % R) / R
    fr, fi = np.cos(a2), -np.sin(a2)
    f2 = np.block([[fr, -fi], [fi, fr]]).astype(np.float32)
    g2 = (np.block([[fr, fi], [-fi, fr]]) / N).astype(np.float32)
    return m1, m1i, f2, g2


@functools.lru_cache(maxsize=None)
def _ctx_fft_mats():
    N = 2 * CTX_LEN
    k = np.arange(N)[:, None]
    n = np.arange(N)[None, :]
    ang = 2.0 * np.pi * ((k * n) % N) / N
    c, s = np.cos(ang), np.sin(ang)
    fwd = np.concatenate([c, -s], axis=0).astype(np.float32)
    inv = (np.concatenate([c, -s], axis=1)[:CTX_LEN] / N).astype(np.float32)
    src = np.concatenate([np.arange(CTX_LEN), (N - np.arange(CTX_LEN)) % N])
    return fwd, inv, fwd[:, src]


def _ada_kernel(c_ref, w_ref, b_ref, o_ref):
    c = c_ref[...]
    a = c * (1.0 / (1.0 + jnp.exp(-c)))
    o_ref[...] = _dot3(a, w_ref[...]) + b_ref[...]


def _ada(cvec, ada_w, ada_b):
    tn = 1536
    return pl.pallas_call(
        _ada_kernel,
        out_shape=jax.ShapeDtypeStruct((DEPTH, 8, N_MOD * D_MODEL), F32),
        grid=(DEPTH, N_MOD * D_MODEL // tn),
        in_specs=[
            pl.BlockSpec((8, D_MODEL), lambda i, j: (0, 0)),
            pl.BlockSpec((None, D_MODEL, tn), lambda i, j: (i, 0, j)),
            pl.BlockSpec((None, 1, tn), lambda i, j: (i, 0, j)),
        ],
        out_specs=pl.BlockSpec((None, 8, tn), lambda i, j: (i, 0, j)),
        compiler_params=_params(("parallel", "parallel"), 40),
        name="ada_mod",
    )(cvec, ada_w, ada_b.reshape(DEPTH, 1, N_MOD * D_MODEL))


def _with_casts(body, n_in, n_out, n_cast):
    def kern(*refs):
        srcs = refs[n_in:n_in + n_cast]
        outs_at = n_in + n_cast
        dsts = refs[outs_at + n_out:outs_at + n_out + n_cast]
        for src, dst in zip(srcs, dsts):
            dst[...] = src[...].astype(dst.dtype)
        body(*refs[:n_in], *refs[outs_at:outs_at + n_out], *refs[outs_at + n_out + n_cast:])
    return kern


def _cast_jobs(stacked_weights, steps_per_row):
    in_specs, out_specs, out_shapes, args = [], [], [], []
    chunk = lambda b, i: jnp.minimum(b * steps_per_row + i, CAST_CHUNKS - 1)
    for w, lead in stacked_weights:
        _, rows, cols = w.shape
        blk = rows // CAST_CHUNKS
        in_specs.append(pl.BlockSpec((None, blk, cols), lambda b, i, lead=lead: (lead, chunk(b, i), 0)))
        out_specs.append(pl.BlockSpec((blk, cols), lambda b, i: (chunk(b, i), 0)))
        out_shapes.append(jax.ShapeDtypeStruct((rows, cols), BF16))
        args.append(w)
    return in_specs, out_specs, out_shapes, args


def _mod_row(ref, row):
    if isinstance(row, int):
        return ref[row:row + 1, :]
    return ref[pl.ds(row, 1), :]


def _norm_mod(x, g, shift, scale):
    y = x * lax.rsqrt(jnp.mean(x * x, axis=-1, keepdims=True) + EPS)
    return (y * g) * (1.0 + scale) + shift


def _rope_tile(x, cos, sin_signed):
    lane = lax.broadcasted_iota(jnp.int32, x.shape, 1)
    first = (lane % 32) < 16
    partner = jnp.where(first, pltpu.roll(x, LANES - 16, axis=1), pltpu.roll(x, 16, axis=1))
    return x * cos + partner * sin_signed


def _project0(x_ref, g, sh, sc, w_ref, cos_ref, sin_ref, u_ref, qkv_ref, rope):
    tn = 512
    n_rot = (ATT_WIDTH + 2 * LANES) // LANES
    sub = min(x_ref.shape[0], SUB_ROWS)
    for s in range(x_ref.shape[0] // sub):
        rows = slice(s * sub, (s + 1) * sub)
        urows = slice(CONV_PAD + s * sub, CONV_PAD + (s + 1) * sub)
        h = _norm_mod(x_ref[rows, :], g, sh, sc).astype(BF16)
        for j in range(EV_U // tn):
            u_ref[urows, j * tn:(j + 1) * tn] = jnp.dot(h, w_ref[:, j * tn:(j + 1) * tn],
                                                        preferred_element_type=F32)
        for j in range(EV_QKV // tn):
            y = jnp.dot(h, w_ref[:, EV_U + j * tn:EV_U + (j + 1) * tn], preferred_element_type=F32)
            for t in range(tn // LANES):
                yt = y[:, t * LANES:(t + 1) * LANES]
                if rope and j * (tn // LANES) + t < n_rot:
                    yt = _rope_tile(yt, cos_ref[rows, :], sin_ref[rows, :])
                qkv_ref[rows, j * tn + t * LANES:j * tn + (t + 1) * LANES] = yt.astype(BF16)


def _short_conv_gate(u_ref, cw_ref, cb_ref, bias_ref, z_ref, x0_ref, zb_ref):
    tm = u_ref.shape[0] - 2 * CONV_PAD
    rc = CONV_ROWS
    z_pitched = z_ref.shape[0] != tm

    def conv(r0, c0):
        cols = slice(c0, c0 + LANES)
        taps = [u_ref[CONV_PAD - 1 + k + r0:CONV_PAD - 1 + k + r0 + rc, cols] for k in range(3)]
        return (taps[0] * cw_ref[0:1, cols] + taps[1] * cw_ref[1:2, cols] + taps[2] * cw_ref[2:3, cols]
                + cb_ref[:, cols])

    for r0 in range(0, tm, rc):
        rows = slice(r0, r0 + rc)
        z0 = r0 // FFT_R * FFT_PITCH + r0 % FFT_R if z_pitched else r0
        for c0 in range(0, HY_WIDTH, LANES):
            cols = slice(c0, c0 + LANES)
            x0 = conv(r0, c0)
            z = conv(r0, 2 * HY_WIDTH + c0) * conv(r0, HY_WIDTH + c0)
            z_ref[z0:z0 + rc, cols] = z
            x0_ref[rows, cols] = x0
            zb_ref[rows, cols] = x0 * (z * bias_ref[:, cols])
    if z_pitched:
        for k in range(tm // FFT_R):
            z_ref[k * FFT_PITCH + FFT_R:(k + 1) * FFT_PITCH, :] = jnp.zeros((FFT_PITCH - FFT_R, HY_WIDTH), F32)


def _inproj0_kernel(x_ref, g_ref, sh_ref, sc_ref, w_ref, cos_ref, sin_ref, cw_ref, cb_ref, bias_ref,
                    z_ref, x0_ref, zb_ref, qkv_ref, u_even, u_odd, *, row, rope, n_tiles):
    r = pl.program_id(0) if row is None else row
    i = pl.program_id(1)
    g, sh, sc = g_ref[...], _mod_row(sh_ref, r), _mod_row(sc_ref, r)
    conv_refs = (cw_ref, cb_ref, bias_ref, z_ref, x0_ref, zb_ref)
    zero_row = jnp.zeros((1, EV_U), F32)
    tm = x_ref.shape[0]
    above, first, last, below = CONV_PAD - 1, CONV_PAD, CONV_PAD + tm - 1, CONV_PAD + tm

    if n_tiles == 1:
        u_even[above:first, :] = zero_row
        u_even[below:below + 1, :] = zero_row
        _project0(x_ref, g, sh, sc, w_ref, cos_ref, sin_ref, u_even, qkv_ref, rope)
        _short_conv_gate(u_even, *conv_refs)
        return

    @pl.when(i == 0)
    def _():
        u_even[...] = jnp.zeros_like(u_even)
        u_odd[...] = jnp.zeros_like(u_odd)

    def step(u_new, u_old):
        j = i - 1
        u_old[above:first, :] = jnp.where(j > 0, u_new[last:last + 1, :], zero_row)
        _project0(x_ref, g, sh, sc, w_ref, cos_ref, sin_ref, u_new, qkv_ref, rope)
        u_old[below:below + 1, :] = jnp.where(j < n_tiles - 1, u_new[first:first + 1, :], zero_row)
        _short_conv_gate(u_old, *conv_refs)

    @pl.when(i % 2 == 0)
    def _():
        step(u_even, u_odd)

    @pl.when(i % 2 == 1)
    def _():
        step(u_odd, u_even)


def _inproj0(x, g, mod, w, conv_w, conv_b, bias, layer, *, is_ctx, tm, cast=()):
    G, R, _ = x.shape
    n_tiles = R // tm
    n_steps = n_tiles + (1 if n_tiles > 1 else 0)
    cos, sin = _rope2d_tables()
    cos, sin = jnp.asarray(cos), jnp.asarray(sin)
    proj_tile = lambda i: jnp.minimum(i, n_tiles - 1)
    conv_tile = (lambda i: jnp.maximum(i - 1, 0)) if n_tiles > 1 else (lambda i: i)
    if is_ctx:
        tab = pl.BlockSpec((tm, LANES), lambda b, i: (0, 0))
    else:
        tab = pl.BlockSpec((tm, LANES), lambda b, i: (proj_tile(i), 0))
    const = lambda a: pl.BlockSpec(a.shape, lambda b, i: (0, 0))
    hy_out = jax.ShapeDtypeStruct((G, R, HY_WIDTH), F32)
    hy_spec = pl.BlockSpec((None, tm, HY_WIDTH), lambda b, i: (b, conv_tile(i), 0))
    if R == SEQ:
        z_out = jax.ShapeDtypeStruct((G, SEQ_P, HY_WIDTH), F32)
        z_spec = pl.BlockSpec((None, tm // FFT_R * FFT_PITCH, HY_WIDTH), lambda b, i: (b, conv_tile(i), 0))
    else:
        z_out, z_spec = hy_out, hy_spec
    kern = functools.partial(_inproj0_kernel, row=2 if is_ctx else None, rope=not is_ctx, n_tiles=n_tiles)
    in_specs = [
        pl.BlockSpec((None, tm, D_MODEL), lambda b, i: (b, proj_tile(i), 0)),
        pl.BlockSpec((1, D_MODEL), lambda b, i: (0, 0)),
        pl.BlockSpec((None, 8, D_MODEL), lambda b, i: (layer, 0, 0)),
        pl.BlockSpec((None, 8, D_MODEL), lambda b, i: (layer, 0, 1)),
        pl.BlockSpec((D_MODEL, EV_COLS), lambda b, i: (0, 0), pipeline_mode=pl.Buffered(1)),
        tab, tab, const(conv_w), const(conv_b), const(bias),
    ]
    out_specs = [z_spec, hy_spec, hy_spec, pl.BlockSpec((None, tm, EV_QKV), lambda b, i: (b, proj_tile(i), 0))]
    out_shape = [z_out, hy_out, hy_out, jax.ShapeDtypeStruct((G, R, EV_QKV), BF16)]
    c_in, c_out, c_shapes, c_args = _cast_jobs(cast, n_steps)
    if cast:
        assert G * n_steps >= CAST_CHUNKS
        kern = _with_casts(kern, len(in_specs), len(out_specs), len(cast))
    res = pl.pallas_call(
        kern,
        out_shape=out_shape + c_shapes,
        grid=(G, n_steps),
        in_specs=in_specs + c_in,
        out_specs=out_specs + c_out,
        scratch_shapes=[pltpu.VMEM((tm + 2 * CONV_PAD, EV_U), F32)] * 2,
        compiler_params=_params(("arbitrary", "arbitrary") if cast else ("parallel", "arbitrary"), 48),
        name="inproj0_ctx" if is_ctx else "inproj0",
    )(x, g, mod, mod, w, cos, sin, conv_w, conv_b, bias, *c_args)
    return res[:4], res[4:]


def _inproj1_kernel(x_ref, g_ref, sh_ref, sc_ref, w_ref, o_ref, *, row, tn):
    r = pl.program_id(0) if row is None else row
    g, sh, sc = g_ref[...], _mod_row(sh_ref, r), _mod_row(sc_ref, r)
    sub = min(x_ref.shape[0], SUB_ROWS)
    for s in range(x_ref.shape[0] // sub):
        rows = slice(s * sub, (s + 1) * sub)
        h = _norm_mod(x_ref[rows, :], g, sh, sc).astype(BF16)
        for j in range(w_ref.shape[1] // tn):
            cols = slice(j * tn, (j + 1) * tn)
            o_ref[rows, cols] = jnp.dot(h, w_ref[:, cols], preferred_element_type=F32).astype(o_ref.dtype)


def _inproj1(x, g, mod, w, layer, *, is_ctx, tm, tn, cast=()):
    G, R, _ = x.shape
    N = w.shape[1]
    kern = functools.partial(_inproj1_kernel, row=2 if is_ctx else None, tn=tn)
    in_specs = [
        pl.BlockSpec((None, tm, D_MODEL), lambda b, i: (b, i, 0)),
        pl.BlockSpec((1, D_MODEL), lambda b, i: (0, 0)),
        pl.BlockSpec((None, 8, D_MODEL), lambda b, i: (layer, 0, 0)),
        pl.BlockSpec((None, 8, D_MODEL), lambda b, i: (layer, 0, 1)),
        pl.BlockSpec((D_MODEL, N), lambda b, i: (0, 0), pipeline_mode=pl.Buffered(1)),
    ]
    c_in, c_out, c_shapes, c_args = _cast_jobs(cast, R // tm)
    if cast:
        assert G * (R // tm) >= CAST_CHUNKS
        kern = _with_casts(kern, len(in_specs), 1, len(cast))
    res = pl.pallas_call(
        kern,
        out_shape=[jax.ShapeDtypeStruct((G, R, N), BF16)] + c_shapes,
        grid=(G, R // tm),
        in_specs=in_specs + c_in,
        out_specs=[pl.BlockSpec((None, tm, N), lambda b, i: (b, i, 0))] + c_out,
        compiler_params=_params(("arbitrary", "arbitrary") if cast else ("parallel", "parallel"), 48),
        name="inproj1_ctx" if is_ctx else "inproj1",
    )(x, g, mod, mod, w, *c_args)
    return res[0], res[1:]


def _split_bf16(a):
    hi = a.astype(BF16)
    return hi, (a - hi.astype(F32)).astype(BF16)


def _dot3_split(a_hi, a_lo, b):
    b_hi, b_lo = _split_bf16(b)
    dot = functools.partial(jnp.dot, preferred_element_type=F32)
    return dot(a_hi, b_hi) + (dot(a_lo, b_hi) + dot(a_hi, b_lo))


def _dot3(a, b):
    return _dot3_split(*_split_bf16(a), b)


def _store_pitched(ref, lead, value):
    for k in range(value.shape[0] // FFT_R):
        ref[lead, k * FFT_PITCH:k * FFT_PITCH + FFT_R, :] = value[k * FFT_R:(k + 1) * FFT_R]
        ref[lead, k * FFT_PITCH + FFT_R:(k + 1) * FFT_PITCH, :] = jnp.zeros(
            (FFT_PITCH - FFT_R, value.shape[1]), value.dtype)


def _filter_kernel(w1t_ref, b1_ref, w2t_ref, b2_ref, fr_ref, w3_ref, dec_ref, bands_ref, h_ref, ss_ref, *,
                   L, tm, pitched):
    i = pl.program_id(0)
    t_row = (lax.broadcasted_iota(jnp.int32, (1, tm), 1) + i * tm).astype(F32) / L
    ang = (2.0 * math.pi * t_row) * bands_ref[...]
    t8 = jnp.where(lax.broadcasted_iota(jnp.int32, (8, tm), 0) == 0, t_row, 0.0)
    feat = jnp.concatenate([t8, jnp.cos(ang), -jnp.sin(ang)], axis=0)
    fr = fr_ref[...]
    hid = jnp.sin(fr * (jnp.dot(w1t_ref[...], feat, precision=HIGHEST, preferred_element_type=F32) + b1_ref[...]))
    hid = jnp.sin(fr * (jnp.dot(w2t_ref[...], hid, precision=HIGHEST, preferred_element_type=F32) + b2_ref[...]))
    hid_hi, hid_lo = _split_bf16(hid.T)
    m_col = lax.broadcasted_iota(jnp.int32, (tm, 1), 0) + i * tm
    t_col = m_col.astype(F32) / L
    energy = jnp.zeros((1, HY_WIDTH), F32)
    for side in range(2):
        h = _dot3_split(hid_hi, hid_lo, w3_ref[side]) * jnp.exp(-t_col * jnp.abs(dec_ref[side]))
        if side == 1:
            h = jnp.where(m_col == 0, 0.0, h)
        if pitched:
            _store_pitched(h_ref, side, h)
        else:
            h_ref[side] = h
        energy = energy + jnp.sum(h * h, axis=0, keepdims=True)

    @pl.when(i == 0)
    def _():
        ss_ref[...] = jnp.zeros_like(ss_ref)

    ss_ref[...] += jnp.broadcast_to(energy, ss_ref.shape)


def _hyena_filter(w1, b1, w2, b2, w3, freq, decay, *, L):
    tm = min(L, 1024)
    pitched = L == SEQ
    tm_out, l_out = (tm // FFT_R * FFT_PITCH, SEQ_P) if pitched else (tm, L)
    kern = functools.partial(_filter_kernel, L=L, tm=tm, pitched=pitched)
    w1t = jnp.concatenate([w1[0:1], jnp.zeros((7, HY_HIDDEN), F32), w1[1:]], axis=0).T
    col = lambda a: a.reshape(HY_HIDDEN, 1)
    w3s = jnp.stack([w3[:, :HY_WIDTH], w3[:, HY_WIDTH:]])
    bands = jnp.asarray(np.linspace(1e-4, HY_BANDS - 1, HY_BANDS).astype(np.float32).reshape(HY_BANDS, 1))
    args = (w1t, col(b1), w2.T, col(b2), col(freq), w3s, decay.reshape(2, 1, HY_WIDTH), bands)
    full = lambda a: pl.BlockSpec(a.shape, lambda i: (0,) * a.ndim)
    h, ss = pl.pallas_call(
        kern,
        out_shape=(jax.ShapeDtypeStruct((2, l_out, HY_WIDTH), F32), jax.ShapeDtypeStruct((8, HY_WIDTH), F32)),
        grid=(L // tm,),
        in_specs=[full(a) for a in args],
        out_specs=(pl.BlockSpec((2, tm_out, HY_WIDTH), lambda i: (0, i, 0)),
                   pl.BlockSpec((8, HY_WIDTH), lambda i: (0, 0))),
        compiler_params=_params(("arbitrary",), 48),
        name="hyena_filter_%d" % L,
    )(*args)
    return h.reshape(2 * l_out, HY_WIDTH), ss


def _stack_bf16(re, im):
    return jnp.concatenate([re, im], axis=0).astype(BF16)


def _filter_fft_kernel(h_ref, ss_ref, m1_ref, f2_ref, o_ref, a_scr):
    R, P = FFT_R, FFT_PITCH
    inv_norm = lax.rsqrt(ss_ref[0:1, :])

    def stage1(n2, carry):
        neg = jnp.where(n2 == 0, 0, R - n2)
        xs = jnp.concatenate([h_ref[pl.ds(n2, R // 2, stride=P), :],
                              h_ref[pl.ds(SEQ_P + neg, R // 2, stride=P), :]], axis=0)
        xs = (xs * inv_norm).astype(BF16)
        a = jnp.dot(m1_ref[n2], xs, preferred_element_type=F32)
        base = pl.multiple_of(n2 * P, 8)
        a_scr[0, pl.ds(base, R), :] = a[:R]
        a_scr[1, pl.ds(base, R), :] = a[R:]
        return carry

    lax.fori_loop(0, R, stage1, 0, unroll=FFT_UNROLL)

    ct = a_scr.shape[-1]

    def stage2(pair, carry):
        k1s = (2 * pair, 2 * pair + 1)
        rhs = jnp.concatenate([_stack_bf16(a_scr[0, pl.ds(k1, R, stride=P), :],
                                           a_scr[1, pl.ds(k1, R, stride=P), :]) for k1 in k1s], axis=1)
        x = jnp.dot(f2_ref[...], rhs, preferred_element_type=F32)
        for d, k1 in enumerate(k1s):
            base = pl.multiple_of(k1 * R, R)
            o_ref[0, pl.ds(base, R), :] = x[:R, d * ct:(d + 1) * ct].astype(BF16)
            o_ref[1, pl.ds(base, R), :] = x[R:, d * ct:(d + 1) * ct].astype(BF16)
        return carry

    lax.fori_loop(0, R // 2, stage2, 0, unroll=FFT_UNROLL // 2)


def _filter_fft(h_circ, ss, m1, f2):
    ct = LANES
    once = pl.Buffered(1)
    return pl.pallas_call(
        _filter_fft_kernel,
        out_shape=jax.ShapeDtypeStruct((2, FFT_N, HY_WIDTH), BF16),
        grid=(HY_WIDTH // ct,),
        in_specs=[
            pl.BlockSpec((2 * SEQ_P, ct), lambda c: (0, c), pipeline_mode=once),
            pl.BlockSpec((8, ct), lambda c: (0, c)),
            pl.BlockSpec(m1.shape, lambda c: (0, 0, 0), pipeline_mode=once),
            pl.BlockSpec(f2.shape, lambda c: (0, 0), pipeline_mode=once),
        ],
        out_specs=pl.BlockSpec((2, FFT_N, ct), lambda c: (0, 0, c)),
        scratch_shapes=[pltpu.VMEM((2, FFT_R * FFT_PITCH, ct), F32)],
        compiler_params=_params(("parallel",), 58),
        name="hyena_filter_fft",
    )(h_circ, ss, m1, f2)


def _fft_conv_kernel(z_ref, hs_ref, m1_ref, f2_ref, g2_ref, m1i_ref, y_ref, a_scr):
    R, P = FFT_R, FFT_PITCH
    ct = a_scr.shape[-1]

    def stage1(n2, carry):
        rows = pl.ds(n2, R // 2, stride=P)
        xs = jnp.concatenate([z_ref[0, rows, :], z_ref[1, rows, :]], axis=1).astype(BF16)
        t = jnp.dot(m1_ref[n2, :, 0:R // 2], xs, preferred_element_type=F32)
        base = pl.multiple_of(n2 * P, 8)
        a_scr[0, pl.ds(base, R), :] = t[:R, :ct] - t[R:, ct:]
        a_scr[1, pl.ds(base, R), :] = t[R:, :ct] + t[:R, ct:]
        return carry

    lax.fori_loop(0, R, stage1, 0, unroll=FFT_UNROLL)

    def stage2(pair, carry):
        k1s = (2 * pair, 2 * pair + 1)
        rhs = jnp.concatenate([_stack_bf16(a_scr[0, pl.ds(k1, R, stride=P), :],
                                           a_scr[1, pl.ds(k1, R, stride=P), :]) for k1 in k1s], axis=1)
        x = jnp.dot(f2_ref[...], rhs, preferred_element_type=F32)
        prods = []
        for d, k1 in enumerate(k1s):
            base = pl.multiple_of(k1 * R, R)
            hr = hs_ref[0, pl.ds(base, R), :].astype(F32)
            hi = hs_ref[1, pl.ds(base, R), :].astype(F32)
            xr, xi = x[:R, d * ct:(d + 1) * ct], x[R:, d * ct:(d + 1) * ct]
            prods.append(_stack_bf16(xr * hr - xi * hi, xr * hi + xi * hr))
        bq = jnp.dot(g2_ref[...], jnp.concatenate(prods, axis=1), preferred_element_type=F32)
        for d, k1 in enumerate(k1s):
            a_scr[0, pl.ds(k1, R, stride=P), :] = bq[:R, d * ct:(d + 1) * ct]
            a_scr[1, pl.ds(k1, R, stride=P), :] = bq[R:, d * ct:(d + 1) * ct]
        return carry

    lax.fori_loop(0, R // 2, stage2, 0, unroll=FFT_UNROLL // 2)

    def stage3(n2, carry):
        base = pl.multiple_of(n2 * P, 8)
        br = a_scr[0, pl.ds(base, R), :]
        bi = a_scr[1, pl.ds(base, R), :]
        rhs = jnp.concatenate([jnp.concatenate([br, bi], axis=1),
                               jnp.concatenate([bi, -br], axis=1)], axis=0).astype(BF16)
        y = jnp.dot(m1i_ref[n2], rhs, preferred_element_type=F32)
        rows = pl.ds(n2, R // 2, stride=P)
        y_ref[0, rows, :] = y[:, :ct]
        y_ref[1, rows, :] = y[:, ct:]
        return carry

    lax.fori_loop(0, R, stage3, 0, unroll=FFT_UNROLL)
    for b in range(2):
        for n1 in range(R // 2):
            y_ref[b, n1 * P + R:(n1 + 1) * P, :] = jnp.zeros((P - R, ct), F32)


def _fft_conv(z, hspec, m1, f2, g2, m1i):
    assert z.shape[0] == 2, "the batch pair is packed into one complex signal"
    ct = LANES
    once = pl.Buffered(1)
    return pl.pallas_call(
        _fft_conv_kernel,
        out_shape=jax.ShapeDtypeStruct((2, SEQ_P, HY_WIDTH), F32),
        grid=(HY_WIDTH // ct,),
        in_specs=[
            pl.BlockSpec((2, SEQ_P, ct), lambda c: (0, 0, c), pipeline_mode=once),
            pl.BlockSpec((2, FFT_N, ct), lambda c: (0, 0, c), pipeline_mode=once),
            pl.BlockSpec(m1.shape, lambda c: (0, 0, 0), pipeline_mode=once),
            pl.BlockSpec(f2.shape, lambda c: (0, 0), pipeline_mode=once),
            pl.BlockSpec(g2.shape, lambda c: (0, 0), pipeline_mode=once),
            pl.BlockSpec(m1i.shape, lambda c: (0, 0, 0), pipeline_mode=once),
        ],
        out_specs=pl.BlockSpec((2, SEQ_P, ct), lambda c: (0, 0, c), pipeline_mode=once),
        scratch_shapes=[pltpu.VMEM((2, FFT_R * FFT_PITCH, ct), F32)],
        compiler_params=_params(("arbitrary",), 58),
        name="hyena_fft_conv",
    )(z, hspec, m1, f2, g2, m1i)


def _ctx_conv_kernel(z_ref, h_ref, ss_ref, fwd_ref, inv_ref, fwd_h_ref, y_ref):
    N = 2 * CTX_LEN
    hn = (h_ref[...] * lax.rsqrt(ss_ref[0:1, :])).astype(BF16)
    hs = jnp.dot(fwd_h_ref[...], hn, preferred_element_type=F32)
    zs = jnp.dot(fwd_ref[:, :CTX_LEN], z_ref[...].astype(BF16), preferred_element_type=F32)
    hr, hi, zr, zi = hs[:N], hs[N:], zs[:N], zs[N:]
    ys = _stack_bf16(zr * hr - zi * hi, zr * hi + zi * hr)
    y_ref[...] = jnp.dot(inv_ref[...], ys, preferred_element_type=F32)


def _ctx_conv(z, h_circ, ss, fwd, inv, fwd_h):
    G = z.shape[0]
    full = lambda a: pl.BlockSpec(a.shape, lambda b: (0,) * a.ndim)
    return pl.pallas_call(
        _ctx_conv_kernel,
        out_shape=jax.ShapeDtypeStruct((G, CTX_LEN, HY_WIDTH), F32),
        grid=(G,),
        in_specs=[pl.BlockSpec((None, CTX_LEN, HY_WIDTH), lambda b: (b, 0, 0)),
                  full(h_circ), full(ss), full(fwd), full(inv), full(fwd_h)],
        out_specs=pl.BlockSpec((None, CTX_LEN, HY_WIDTH), lambda b: (b, 0, 0)),
        compiler_params=_params(("parallel",), 48),
        name="hyena_ctx_conv",
    )(z, h_circ, ss, fwd, inv, fwd_h)


def _kv_group(refs, g, col0):
    cols = slice(col0 + g * LANES, col0 + (g + 1) * LANES)
    return refs[0][:, cols] if len(refs) == 1 else jnp.concatenate([r[:, cols] for r in refs], axis=0)


def _attn_logits(q_ref, k_refs, g, masks, col0=0):
    low = lax.broadcasted_iota(jnp.int32, (BLOCK, LANES), 1) < HEAD_DIM
    zero = jnp.zeros((), BF16)
    parts = []
    for hh in range(ATT_GROUP):
        h = g * ATT_GROUP + hh
        tile = q_ref[:, (h // 2) * LANES:(h // 2 + 1) * LANES]
        parts.append(jnp.where(low if h % 2 == 0 else ~low, tile, zero))
    s = lax.dot_general(jnp.concatenate(parts, axis=0), _kv_group(k_refs, g, col0), (((1,), (1,)), ((), ())),
                        preferred_element_type=F32)
    if masks is not None:
        ok_prev, ok_next = masks
        s = jnp.concatenate([jnp.where(ok_prev, s[:, :BLOCK], NEG_INF), s[:, BLOCK:2 * BLOCK],
                             jnp.where(ok_next, s[:, 2 * BLOCK:3 * BLOCK], NEG_INF), s[:, 3 * BLOCK:]], axis=1)
    return s


def _attn_output(s, sink_ref, v_refs, g, o_ref, col0=0):
    low = lax.broadcasted_iota(jnp.int32, (BLOCK, LANES), 1) < HEAD_DIM
    low4 = jnp.concatenate([low] * ATT_GROUP, axis=0)
    v = _kv_group(v_refs, g, col0)
    v_aug = jnp.where(lax.broadcasted_iota(jnp.int32, v.shape, 1) < HEAD_DIM, v, jnp.ones((), BF16))
    sink = jnp.concatenate(
        [jnp.full((BLOCK, 1), sink_ref[g * ATT_GROUP + hh] * LOG2E, F32) for hh in range(ATT_GROUP)], axis=0)
    m = jnp.maximum(jnp.max(s, axis=-1, keepdims=True), sink)
    e = jnp.exp2(s - m).astype(BF16)
    o = jnp.dot(e, v_aug, preferred_element_type=F32) + jnp.where(low4, 0.0, jnp.exp2(sink - m))
    swapped = pltpu.roll(o, HEAD_DIM, axis=1)
    for pair in range(ATT_GROUP // 2):
        ev = slice((2 * pair) * BLOCK, (2 * pair + 1) * BLOCK)
        od = slice((2 * pair + 1) * BLOCK, (2 * pair + 2) * BLOCK)
        even = o[ev] / swapped[ev]
        odd = swapped[od] / o[od]
        t = g * (ATT_GROUP // 2) + pair
        o_ref[:, t * LANES:(t + 1) * LANES] = jnp.where(low, even, odd).astype(o_ref.dtype)


def _attn_ctx_kernel(sink_ref, q_ref, kx_ref, vx_ref, o_ref):
    for g in range(ATT_KV_HEADS):
        _attn_output(_attn_logits(q_ref, (kx_ref,), g, None), sink_ref, (vx_ref,), g, o_ref)


def _attn_local_kernel(sink_ref, q_ref, kv_m2, kv_m1, kv_0, kv_p1, kv_ctx, o_ref, s_even, s_odd, *, n_blocks):
    n = pl.program_id(1)
    k_refs, v_refs, v_col = (kv_m1, kv_0, kv_p1, kv_ctx), (kv_m2, kv_m1, kv_0, kv_ctx), 2 * LANES

    @pl.when(n == 0)
    def _():
        s_odd[...] = jnp.zeros_like(s_odd)

    def step(s_new, s_prev):
        nq = jnp.minimum(n, n_blocks - 1)
        qi = lax.broadcasted_iota(jnp.int32, (ATT_GROUP * BLOCK, BLOCK), 0) % BLOCK
        ki = lax.broadcasted_iota(jnp.int32, (ATT_GROUP * BLOCK, BLOCK), 1)
        masks = ((ki >= qi) & (nq > 0), (ki <= qi) & (nq < n_blocks - 1))
        for g in range(ATT_KV_HEADS):
            s_new[g] = _attn_logits(q_ref, k_refs, g, masks)
        for g in range(ATT_KV_HEADS):
            _attn_output(s_prev[g], sink_ref, v_refs, g, o_ref, v_col)

    @pl.when(n % 2 == 0)
    def _():
        step(s_even, s_odd)

    @pl.when(n % 2 == 1)
    def _():
        step(s_odd, s_even)


def _attention(sink, qkv, qkv_ctx, *, local):
    src = qkv if local else qkv_ctx
    G, R, _ = src.shape
    nb = R // BLOCK
    kcol, vcol = ATT_WIDTH // 256, ATT_WIDTH // 256 + 1
    ctx_k = pl.BlockSpec((None, CTX_LEN, 256), lambda b, n: (b, 0, kcol))
    ctx_v = pl.BlockSpec((None, CTX_LEN, 256), lambda b, n: (b, 0, vcol))
    smem = pl.BlockSpec(memory_space=pltpu.SMEM)
    out_shape = jax.ShapeDtypeStruct((G, R, ATT_WIDTH), BF16)
    if not local:
        rows = pl.BlockSpec((None, BLOCK, ATT_WIDTH), lambda b, n: (b, n, 0))
        return pl.pallas_call(
            _attn_ctx_kernel, out_shape=out_shape, grid=(G, nb),
            in_specs=[smem, rows, ctx_k, ctx_v], out_specs=rows,
            compiler_params=_params(("parallel", "parallel"), 48), name="attn_ctx",
        )(sink, qkv_ctx, qkv_ctx, qkv_ctx)

    q_blk = lambda n: jnp.minimum(n, nb - 1)
    o_blk = lambda n: jnp.maximum(n - 1, 0)

    kv_w = EV_QKV - ATT_WIDTH
    kv_col = ATT_WIDTH // kv_w

    def kv_rows(off):
        return pl.BlockSpec((None, BLOCK, kv_w), lambda b, n: (b, jnp.clip(n + off, 0, nb - 1), kv_col))

    in_specs = [smem, pl.BlockSpec((None, BLOCK, ATT_WIDTH), lambda b, n: (b, q_blk(n), 0)),
                kv_rows(-2), kv_rows(-1), kv_rows(0), kv_rows(1),
                pl.BlockSpec((None, CTX_LEN, kv_w), lambda b, n: (b, 0, kv_col))]
    logits_scratch = pltpu.VMEM((ATT_KV_HEADS, ATT_GROUP * BLOCK, 3 * BLOCK + CTX_LEN), F32)
    return pl.pallas_call(
        functools.partial(_attn_local_kernel, n_blocks=nb),
        out_shape=out_shape,
        grid=(G, nb + 1),
        in_specs=in_specs,
        out_specs=pl.BlockSpec((None, BLOCK, ATT_WIDTH), lambda b, n: (b, o_blk(n), 0)),
        scratch_shapes=[logits_scratch, logits_scratch],
        compiler_params=_params(("parallel", "arbitrary"), 48),
        name="attn_local",
    )(sink, qkv, qkv, qkv, qkv, qkv, qkv_ctx)


def _mixer_out0(rows, x0_ref, y_ref, zb_ref, att_ref, wo_ref):
    if y_ref.shape[0] != x0_ref.shape[0]:
        y = jnp.concatenate([y_ref[r // FFT_R * FFT_PITCH:r // FFT_R * FFT_PITCH + FFT_R, :]
                             for r in range(rows.start, rows.stop, FFT_R)], axis=0)
    else:
        y = y_ref[rows, :]
    hy = (x0_ref[rows, :] * y + zb_ref[rows, :]).astype(BF16)
    return (jnp.dot(hy, wo_ref[:HY_WIDTH, :], preferred_element_type=F32)
            + jnp.dot(att_ref[rows, :], wo_ref[HY_WIDTH:, :], preferred_element_type=F32))


def _mixer_out1(rows, of_ref, ob_ref, wo_ref):
    a = (of_ref[rows, :].astype(F32) + ob_ref[rows, :].astype(F32)).astype(BF16)
    return jnp.dot(a, wo_ref[...], preferred_element_type=F32)


def _mix_kernel(*refs, mixer_out, n_mix, row, fc, final_norm):
    x_ref = refs[0]
    mix_refs = refs[1:2 + n_mix]
    gta_ref, gm_ref, shm_ref, scm_ref, gtm_ref, w1_ref, w2_ref, fg_ref, o_ref, a_scr = refs[2 + n_mix:]
    r = pl.program_id(0) if row is None else row
    gta, gtm = _mod_row(gta_ref, r), _mod_row(gtm_ref, r)
    gm, shm, scm = gm_ref[...], _mod_row(shm_ref, r), _mod_row(scm_ref, r)
    sub = min(x_ref.shape[0], SUB_ROWS)
    for s in range(x_ref.shape[0] // sub):
        rows = slice(s * sub, (s + 1) * sub)
        x1 = x_ref[rows, :] + gta * mixer_out(rows, *mix_refs)
        h = _norm_mod(x1, gm, shm, scm).astype(BF16)
        for c in range(D_FF // fc):
            a = jnp.maximum(jnp.dot(h, w1_ref[:, c * fc:(c + 1) * fc], preferred_element_type=F32), 0.0)
            a_scr[rows, c * fc:(c + 1) * fc] = (a * a).astype(BF16)
        out = x1 + gtm * jnp.dot(a_scr[rows, :], w2_ref[...], preferred_element_type=F32)
        if final_norm:
            out = (out * lax.rsqrt(jnp.mean(out * out, axis=-1, keepdims=True) + EPS)) * fg_ref[...]
        o_ref[rows, :] = out


def _mix_mlp(kind, x, mix_in, wo, mod, gm, w1, w2, fg, layer, *, is_ctx, tm, fc, final_norm, cast=()):
    G, R, _ = x.shape
    row_spec = lambda w: pl.BlockSpec((None, tm, w), lambda b, i: (b, i, 0))
    modk = lambda k: pl.BlockSpec((None, 8, D_MODEL), lambda b, i: (layer, 0, k))
    vec = pl.BlockSpec((1, D_MODEL), lambda b, i: (0, 0))
    resident = lambda a: pl.BlockSpec(a.shape, lambda b, i: (0, 0), pipeline_mode=pl.Buffered(1))
    if kind == 0:
        mixer_out = _mixer_out0
        y_rows = tm if mix_in[1].shape[1] == R else tm // FFT_R * FFT_PITCH
        y_spec = pl.BlockSpec((None, y_rows, HY_WIDTH), lambda b, i: (b, i, 0))
        mix_specs = [row_spec(HY_WIDTH), y_spec, row_spec(HY_WIDTH), row_spec(ATT_WIDTH)]
    else:
        mixer_out = _mixer_out1
        mix_specs = [row_spec(RET_V)] * 2
    kern = functools.partial(_mix_kernel, mixer_out=mixer_out, n_mix=len(mix_in), row=2 if is_ctx else None,
                             fc=fc, final_norm=final_norm)
    in_specs = [row_spec(D_MODEL)] + mix_specs + [
        resident(wo), modk(2), vec, modk(3), modk(4), modk(5), resident(w1), resident(w2), vec]
    c_in, c_out, c_shapes, c_args = _cast_jobs(cast, R // tm)
    if cast:
        assert G * (R // tm) >= CAST_CHUNKS
        kern = _with_casts(kern, len(in_specs), 1, len(cast))
    res = pl.pallas_call(
        kern,
        out_shape=[jax.ShapeDtypeStruct((G, R, D_MODEL), F32)] + c_shapes,
        grid=(G, R // tm),
        in_specs=in_specs + c_in,
        out_specs=[row_spec(D_MODEL)] + c_out,
        scratch_shapes=[pltpu.VMEM((tm, D_FF), BF16)],
        compiler_params=_params(("arbitrary", "arbitrary") if cast else ("parallel", "parallel"), 56),
        name="mix_mlp%d%s" % (kind, "_ctx" if is_ctx else ""),
    )(x, *mix_in, wo, mod, gm, mod, mod, mod, w1, w2, fg, *c_args)
    return res[0], res[1:]


def _ret_kernel(lr_ref, qkv_c, qkv_f, g_f, qkv_b, g_b, cos_ref, sin_ref, of_ref, ob_ref,
                state, dmask, xi, zeta, gch):
    j = pl.program_id(1)
    C = RET_C
    kscale = RET_DK ** -0.5

    @pl.when(j == 0)
    def _():
        state[...] = jnp.zeros_like(state)
        row = lax.broadcasted_iota(jnp.int32, (C, C), 0).astype(F32)
        col = lax.broadcasted_iota(jnp.int32, (C, C), 1).astype(F32)
        for d in range(2):
            e = row - col if d == 0 else col - row
            p = row if d == 0 else (C - 1) - row
            for h in range(RET_HEADS):
                lg = -jnp.exp(jnp.full((C, C), lr_ref[d, h], F32))
                dmask[d, h] = jnp.where(e >= 0, jnp.exp(e * lg), 0.0) * kscale
                xi[d, h] = jnp.exp((p + 1.0) * lg)
                zeta[d, h] = jnp.exp(((C - 1) - p) * lg) * kscale
                gch[d, h] = jnp.exp(C * -jnp.exp(jnp.full((8, C), lr_ref[d, h], F32)))

    half = RET_DK // 2

    def chain(d, h, src_ref, g_ref, o_ref):
        cos, sin = cos_ref[d], sin_ref[d]

        def rot(c0):
            t1 = src_ref[:, c0:c0 + half].astype(F32)
            t2 = src_ref[:, c0 + half:c0 + RET_DK].astype(F32)
            return jnp.concatenate([t1 * cos - t2 * sin, t2 * cos + t1 * sin], axis=1)

        q = rot(h * RET_DK)
        k = rot(RET_QK + h * RET_DK)
        v = src_ref[:, 2 * RET_QK + h * RET_DV:2 * RET_QK + (h + 1) * RET_DV]
        inner = lax.dot_general(q.astype(BF16), k.astype(BF16), (((1,), (1,)), ((), ())),
                                preferred_element_type=F32) * dmask[d, h]
        s_old = state[d, h]
        kv = lax.dot_general((k * zeta[d, h]).astype(BF16), v, (((0,), (0,)), ((), ())),
                             preferred_element_type=F32)
        state[d, h] = gch[d, h, 0:1, 0:1] * s_old + kv
        if o_ref is not None:
            o = (jnp.dot(inner.astype(BF16), v, preferred_element_type=F32)
                 + jnp.dot((q * xi[d, h]).astype(BF16), s_old.astype(BF16), preferred_element_type=F32))
            on = o * lax.rsqrt(jnp.mean(o * o, axis=-1, keepdims=True) + EPS)
            gate = g_ref[:, h * RET_DV:(h + 1) * RET_DV]
            half_gate = gate * 0.5
            silu = half_gate + half_gate * jnp.tanh(half_gate)
            o_ref[:, h * RET_DV:(h + 1) * RET_DV] = (silu.astype(F32) * on).astype(o_ref.dtype)

    @pl.when(j == 0)
    def _():
        for d in range(2):
            for h in range(RET_HEADS):
                chain(d, h, qkv_c, None, None)

    @pl.when(j > 0)
    def _():
        for h in range(RET_HEADS):
            chain(0, h, qkv_f, g_f, of_ref)
            chain(1, h, qkv_b, g_b, ob_ref)


def _retention(log_rate, p_ctx, p_lat):
    G = p_lat.shape[0]
    C = RET_C
    n_lat = SEQ // C
    cos, sin = _rope1d_tables()
    cos, sin = jnp.asarray(cos), jnp.asarray(sin)

    def fwd_chunk(j):
        return jnp.maximum(j - 1, 0)

    def bwd_chunk(j):
        return n_lat - 1 - jnp.maximum(j - 1, 0)

    qkv_w = 2 * RET_QK + RET_V
    gcol = qkv_w // RET_V
    tab = pl.BlockSpec((2, C, RET_DK // 2), lambda b, j: (0, j, 0))
    in_specs = [
        pl.BlockSpec(memory_space=pltpu.SMEM),
        pl.BlockSpec((None, C, qkv_w), lambda b, j: (b, 0, 0)),
        pl.BlockSpec((None, C, qkv_w), lambda b, j: (b, fwd_chunk(j), 0)),
        pl.BlockSpec((None, C, RET_V), lambda b, j: (b, fwd_chunk(j), gcol)),
        pl.BlockSpec((None, C, qkv_w), lambda b, j: (b, bwd_chunk(j), 0)),
        pl.BlockSpec((None, C, RET_V), lambda b, j: (b, bwd_chunk(j), gcol + 1)),
        tab, tab,
    ]
    out = jax.ShapeDtypeStruct((G, SEQ, RET_V), BF16)
    per_chain = lambda *tail: pltpu.VMEM((2, RET_HEADS) + tail, F32)
    return pl.pallas_call(
        _ret_kernel,
        out_shape=(out, out),
        grid=(G, 1 + n_lat),
        in_specs=in_specs,
        out_specs=(pl.BlockSpec((None, C, RET_V), lambda b, j: (b, fwd_chunk(j), 0)),
                   pl.BlockSpec((None, C, RET_V), lambda b, j: (b, bwd_chunk(j), 0))),
        scratch_shapes=[per_chain(RET_DK, RET_DV), per_chain(C, C), per_chain(C, C), per_chain(C, C),
                        per_chain(8, C)],
        compiler_params=_params(("parallel", "arbitrary"), 48),
        name="retention",
    )(log_rate, p_ctx, p_lat, p_lat, p_lat, p_lat, cos, sin)


def _ev_weight(w_in):
    i_q = EV_U
    i_k = i_q + ATT_WIDTH
    i_v = i_k + KV_WIDTH
    cols = [w_in[:, :i_q], w_in[:, i_q:i_k] * (HEAD_DIM ** -0.5 * LOG2E)]
    for base in (i_k, i_v):
        for g in range(ATT_KV_HEADS):
            part = w_in[:, base + g * HEAD_DIM: base + (g + 1) * HEAD_DIM]
            cols += [part, part]
    return jnp.concatenate(cols, axis=1).astype(BF16)


def kernel(x, c, ctx, c_ctx, ada_w, ada_b, norm_mix_g, norm_mlp_g, mlp_w1, mlp_w2, ev_w_in, ev_w_out, hy_conv_w, hy_conv_b, hy_w1, hy_b1, hy_w2, hy_b2, hy_w3, hy_freq, hy_decay, hy_bias, attn_sink, od_w_in, od_w_out, ret_log_rate, final_g):
    D = D_MODEL
    cvec = jnp.concatenate([c, c_ctx[None, :], jnp.zeros((8 - BATCH - 1, D), F32)], axis=0)
    mod = _ada(cvec, ada_w, ada_b)

    m1, m1i, f2, g2 = (jnp.asarray(a).astype(BF16) for a in _fft_mats())
    cfwd, cinv, cfwd_h = (jnp.asarray(a).astype(BF16) for a in _ctx_fft_mats())
    fg = final_g.reshape(1, D)

    gmix = norm_mix_g[0].reshape(1, D)
    gmlp = norm_mlp_g[0].reshape(1, D)
    w_in = _ev_weight(ev_w_in[0])
    conv = (hy_conv_w[0], hy_conv_b[0].reshape(1, EV_U), hy_bias[0].reshape(1, HY_WIDTH))
    (z_l, x0_l, zb_l, qkv_l), (wo, w1, w2) = _inproj0(
        x, gmix, mod, w_in, *conv, 0, is_ctx=False, tm=512, cast=((ev_w_out, 0), (mlp_w1, 0), (mlp_w2, 0)))
    (z_c, x0_c, zb_c, qkv_c), _ = _inproj0(ctx, gmix, mod, w_in, *conv, 0, is_ctx=True, tm=CTX_LEN)

    filt = (hy_w1[0], hy_b1[0], hy_w2[0], hy_b2[0], hy_w3[0], hy_freq[0], hy_decay[0])
    h_l, ss_l = _hyena_filter(*filt, L=SEQ)
    h_c, ss_c = _hyena_filter(*filt, L=CTX_LEN)
    hspec = _filter_fft(h_l, ss_l, m1, f2)
    y_l = _fft_conv(z_l, hspec, m1, f2, g2, m1i)
    y_c = _ctx_conv(z_c, h_c, ss_c, cfwd, cinv, cfwd_h)

    sink = attn_sink[0]
    att_l = _attention(sink, qkv_l, qkv_c, local=True)
    att_c = _attention(sink, qkv_l, qkv_c, local=False)

    x1, (w_in,) = _mix_mlp(0, x, (x0_l, y_l, zb_l, att_l), wo, mod, gmlp, w1, w2, fg, 0,
                           is_ctx=False, tm=512, fc=512, final_norm=False, cast=((od_w_in, 0),))
    ctx1, _ = _mix_mlp(0, ctx, (x0_c, y_c, zb_c, att_c), wo, mod, gmlp, w1, w2, fg, 0,
                       is_ctx=True, tm=CTX_LEN, fc=512, final_norm=False)

    gmix = norm_mix_g[1].reshape(1, D)
    gmlp = norm_mlp_g[1].reshape(1, D)
    p_l, (wo, w1, w2) = _inproj1(x1, gmix, mod, w_in, 1, is_ctx=False, tm=512, tn=512,
                                 cast=((od_w_out, 0), (mlp_w1, 1), (mlp_w2, 1)))
    p_c, _ = _inproj1(ctx1, gmix, mod, w_in, 1, is_ctx=True, tm=CTX_LEN, tn=512)
    o_fwd, o_bwd = _retention(ret_log_rate[0], p_c, p_l)

    out, _ = _mix_mlp(1, x1, (o_fwd, o_bwd), wo, mod, gmlp, w1, w2, fg, 1,
                      is_ctx=False, tm=512, fc=512, final_norm=True)
    return out
```

```python
import functools
import math

import numpy as np
import jax
import jax.numpy as jnp
from jax import lax
from jax.experimental import pallas as pl
from jax.experimental.pallas import tpu as pltpu

F32 = jnp.float32
BF16 = jnp.bfloat16
HIGHEST = lax.Precision.HIGHEST

D_MODEL = 1024
BATCH = 2
SEQ = 8192
DEPTH = 2
GRID_W = 64
CTX_LEN = 256
EPS = 1e-6
NEG_INF = -1e30
N_MOD = 6
D_FF = 4 * D_MODEL
ROPE_BASE = 10000.0

HY_WIDTH = D_MODEL // 2
HY_EMB = 33
HY_BANDS = (HY_EMB - 1) // 2
HY_HIDDEN = 64

ATT_HEADS = 8
ATT_KV_HEADS = 2
ATT_GROUP = ATT_HEADS // ATT_KV_HEADS
HEAD_DIM = 64
ATT_WIDTH = ATT_HEADS * HEAD_DIM
KV_WIDTH = ATT_KV_HEADS * HEAD_DIM
BLOCK = 128

RET_HEADS = 4
RET_DK = D_MODEL // RET_HEADS
RET_DV = 2 * RET_DK
RET_QK = RET_HEADS * RET_DK
RET_V = RET_HEADS * RET_DV
OD_IN = 2 * RET_QK + 3 * RET_V

LOG2E = 1.4426950408889634
LANES = 128
MIB = 1024 * 1024

EV_U = 3 * HY_WIDTH
EV_QKV = ATT_WIDTH + 4 * LANES
EV_COLS = EV_U + EV_QKV

FFT_N = 2 * SEQ
FFT_R = 128
FFT_PITCH = FFT_R + 8
SEQ_P = SEQ // FFT_R * FFT_PITCH
CONV_ROWS = 64
CONV_PAD = 8
SUB_ROWS = 256
FFT_UNROLL = 32
RET_C = 256
CAST_CHUNKS = 32


VMEM_SMALL_MIB = 48
VMEM_DENSE_MIB = 56
VMEM_FFT_MIB = 58


def _params(sem, vmem_mib):
    return pltpu.CompilerParams(dimension_semantics=sem, vmem_limit_bytes=vmem_mib * MIB)


@functools.lru_cache(maxsize=None)
def _rope2d_tables():
    quarter = HEAD_DIM // 4
    inv = ROPE_BASE ** (-np.arange(quarter, dtype=np.float64) / quarter)
    t = np.arange(SEQ)
    pos = np.stack([t // GRID_W, t % GRID_W], axis=1).astype(np.float64)
    lane = np.arange(HEAD_DIM)
    half = lane // (HEAD_DIM // 2)
    e = lane % (HEAD_DIM // 2)
    ang = pos[:, half] * inv[e % quarter][None, :]
    sign = np.where(e < quarter, -1.0, 1.0)[None, :]
    cos = np.tile(np.cos(ang), (1, 2)).astype(np.float32)
    sin = np.tile(np.sin(ang) * sign, (1, 2)).astype(np.float32)
    return cos, sin


@functools.lru_cache(maxsize=None)
def _rope1d_tables():
    n = RET_DK // 2
    inv = ROPE_BASE ** (-np.linspace(0.0, 1.0, n))
    pos = np.arange(CTX_LEN + SEQ, dtype=np.float64)
    ang = pos[:, None] * inv[None, :]
    cos, sin = np.cos(ang), np.sin(ang)

    def rev(a):
        return a.reshape(-1, RET_C, n)[:, ::-1].reshape(-1, n)

    cos2 = np.stack([cos, rev(cos)]).astype(np.float32)
    sin2 = np.stack([sin, rev(sin)]).astype(np.float32)
    return cos2, sin2


@functools.lru_cache(maxsize=None)
def _fft_mats():
    N, R = FFT_N, FFT_R
    k1 = np.arange(R)
    n2 = np.arange(R)[:, None, None]
    n1 = np.arange(R // 2)[None, None, :]
    n = np.concatenate([n2 + R * n1, (N - ((R - n2) % R + R * n1)) % N], axis=2)
    idx = (k1[None, :, None] * n) % N
    ang = 2.0 * np.pi * idx / N
    c, s = np.cos(ang), np.sin(ang)
    m1 = np.concatenate([c, -s], axis=1).astype(np.float32)
    m1i = np.concatenate([c.transpose(0, 2, 1), -s.transpose(0, 2, 1)], axis=2)
    m1i = m1i[:, :R // 2].astype(np.float32)
    a2 = 2.0 * np.pi * ((np.arange(R)[:, None] * np.arange(R)[None, :]) % R) / R
    fr, fi = np.cos(a2), -np.sin(a2)
    f2 = np.block([[fr, -fi], [fi, fr]]).astype(np.float32)
    g2 = (np.block([[fr, fi], [-fi, fr]]) / N).astype(np.float32)
    return m1, m1i, f2, g2


@functools.lru_cache(maxsize=None)
def _ctx_fft_mats():
    N = 2 * CTX_LEN
    k = np.arange(N)[:, None]
    n = np.arange(N)[None, :]
    ang = 2.0 * np.pi * ((k * n) % N) / N
    c, s = np.cos(ang), np.sin(ang)
    fwd = np.concatenate([c, -s], axis=0).astype(np.float32)
    inv = (np.concatenate([c, -s], axis=1)[:CTX_LEN] / N).astype(np.float32)
    src = np.concatenate([np.arange(CTX_LEN), (N - np.arange(CTX_LEN)) % N])
    return fwd, inv, fwd[:, src]


def _ada_kernel(c_ref, w_ref, b_ref, o_ref):
    c = c_ref[...]
    a = c * (1.0 / (1.0 + jnp.exp(-c)))
    o_ref[...] = _dot3(a, w_ref[...]) + b_ref[...]


def _ada(cvec, ada_w, ada_b):
    tn = 1536
    return pl.pallas_call(
        _ada_kernel,
        out_shape=jax.ShapeDtypeStruct((DEPTH, 8, N_MOD * D_MODEL), F32),
        grid=(DEPTH, N_MOD * D_MODEL // tn),
        in_specs=[
            pl.BlockSpec((8, D_MODEL), lambda i, j: (0, 0)),
            pl.BlockSpec((None, D_MODEL, tn), lambda i, j: (i, 0, j)),
            pl.BlockSpec((None, 1, tn), lambda i, j: (i, 0, j)),
        ],
        out_specs=pl.BlockSpec((None, 8, tn), lambda i, j: (i, 0, j)),
        compiler_params=_params(("parallel", "parallel"), VMEM_SMALL_MIB),
        name="ada_mod",
    )(cvec, ada_w, ada_b.reshape(DEPTH, 1, N_MOD * D_MODEL))


def _with_casts(body, n_in, n_out, n_cast):
    def kern(*refs):
        srcs = refs[n_in:n_in + n_cast]
        outs_at = n_in + n_cast
        dsts = refs[outs_at + n_out:outs_at + n_out + n_cast]
        for src, dst in zip(srcs, dsts):
            dst[...] = src[...].astype(dst.dtype)
        body(*refs[:n_in], *refs[outs_at:outs_at + n_out], *refs[outs_at + n_out + n_cast:])
    return kern


def _cast_jobs(stacked_weights, steps_per_row):
    in_specs, out_specs, out_shapes, args = [], [], [], []
    chunk = lambda b, i: jnp.minimum(b * steps_per_row + i, CAST_CHUNKS - 1)
    for w, lead in stacked_weights:
        _, rows, cols = w.shape
        blk = rows // CAST_CHUNKS
        in_specs.append(pl.BlockSpec((None, blk, cols), lambda b, i, lead=lead: (lead, chunk(b, i), 0)))
        out_specs.append(pl.BlockSpec((blk, cols), lambda b, i: (chunk(b, i), 0)))
        out_shapes.append(jax.ShapeDtypeStruct((rows, cols), BF16))
        args.append(w)
    return in_specs, out_specs, out_shapes, args


def _mod_row(ref, row):
    if isinstance(row, int):
        return ref[row:row + 1, :]
    return ref[pl.ds(row, 1), :]


def _norm_mod(x, g, shift, scale):
    y = x * lax.rsqrt(jnp.mean(x * x, axis=-1, keepdims=True) + EPS)
    return (y * g) * (1.0 + scale) + shift


def _rope_tile(x, cos, sin_signed):
    lane = lax.broadcasted_iota(jnp.int32, x.shape, 1)
    first = (lane % 32) < 16
    partner = jnp.where(first, pltpu.roll(x, LANES - 16, axis=1), pltpu.roll(x, 16, axis=1))
    return x * cos + partner * sin_signed


def _project0(x_ref, g, sh, sc, w_ref, cos_ref, sin_ref, u_ref, qkv_ref, rope):
    tn = 512
    n_rot = (ATT_WIDTH + 2 * LANES) // LANES
    sub = min(x_ref.shape[0], SUB_ROWS)
    for s in range(x_ref.shape[0] // sub):
        rows = slice(s * sub, (s + 1) * sub)
        urows = slice(CONV_PAD + s * sub, CONV_PAD + (s + 1) * sub)
        h = _norm_mod(x_ref[rows, :], g, sh, sc).astype(BF16)
        for j in range(EV_U // tn):
            u_ref[urows, j * tn:(j + 1) * tn] = jnp.dot(h, w_ref[:, j * tn:(j + 1) * tn],
                                                        preferred_element_type=F32)
        for j in range(EV_QKV // tn):
            y = jnp.dot(h, w_ref[:, EV_U + j * tn:EV_U + (j + 1) * tn], preferred_element_type=F32)
            for t in range(tn // LANES):
                yt = y[:, t * LANES:(t + 1) * LANES]
                if rope and j * (tn // LANES) + t < n_rot:
                    yt = _rope_tile(yt, cos_ref[rows, :], sin_ref[rows, :])
                qkv_ref[rows, j * tn + t * LANES:j * tn + (t + 1) * LANES] = yt.astype(BF16)


def _short_conv_gate(u_ref, cw_ref, cb_ref, bias_ref, z_ref, x0_ref, zb_ref):
    tm = u_ref.shape[0] - 2 * CONV_PAD
    rc = CONV_ROWS
    z_pitched = z_ref.shape[0] != tm

    def conv(r0, c0):
        cols = slice(c0, c0 + LANES)
        taps = [u_ref[CONV_PAD - 1 + k + r0:CONV_PAD - 1 + k + r0 + rc, cols] for k in range(3)]
        return (taps[0] * cw_ref[0:1, cols] + taps[1] * cw_ref[1:2, cols] + taps[2] * cw_ref[2:3, cols]
                + cb_ref[:, cols])

    for r0 in range(0, tm, rc):
        rows = slice(r0, r0 + rc)
        z0 = r0 // FFT_R * FFT_PITCH + r0 % FFT_R if z_pitched else r0
        for c0 in range(0, HY_WIDTH, LANES):
            cols = slice(c0, c0 + LANES)
            x0 = conv(r0, c0)
            z = conv(r0, 2 * HY_WIDTH + c0) * conv(r0, HY_WIDTH + c0)
            z_ref[z0:z0 + rc, cols] = z
            x0_ref[rows, cols] = x0
            zb_ref[rows, cols] = x0 * (z * bias_ref[:, cols])
    if z_pitched:
        for k in range(tm // FFT_R):
            z_ref[k * FFT_PITCH + FFT_R:(k + 1) * FFT_PITCH, :] = jnp.zeros((FFT_PITCH - FFT_R, HY_WIDTH), F32)


def _inproj0_kernel(x_ref, g_ref, sh_ref, sc_ref, w_ref, cos_ref, sin_ref, cw_ref, cb_ref, bias_ref,
                    z_ref, x0_ref, zb_ref, qkv_ref, u_even, u_odd, *, row, rope, n_tiles):
    r = pl.program_id(0) if row is None else row
    i = pl.program_id(1)
    g, sh, sc = g_ref[...], _mod_row(sh_ref, r), _mod_row(sc_ref, r)
    conv_refs = (cw_ref, cb_ref, bias_ref, z_ref, x0_ref, zb_ref)
    zero_row = jnp.zeros((1, EV_U), F32)
    tm = x_ref.shape[0]
    above, first, last, below = CONV_PAD - 1, CONV_PAD, CONV_PAD + tm - 1, CONV_PAD + tm

    if n_tiles == 1:
        u_even[above:first, :] = zero_row
        u_even[below:below + 1, :] = zero_row
        _project0(x_ref, g, sh, sc, w_ref, cos_ref, sin_ref, u_even, qkv_ref, rope)
        _short_conv_gate(u_even, *conv_refs)
        return

    @pl.when(i == 0)
    def _():
        u_even[...] = jnp.zeros_like(u_even)
        u_odd[...] = jnp.zeros_like(u_odd)

    def step(u_new, u_old):
        j = i - 1
        u_old[above:first, :] = jnp.where(j > 0, u_new[last:last + 1, :], zero_row)
        _project0(x_ref, g, sh, sc, w_ref, cos_ref, sin_ref, u_new, qkv_ref, rope)
        u_old[below:below + 1, :] = jnp.where(j < n_tiles - 1, u_new[first:first + 1, :], zero_row)
        _short_conv_gate(u_old, *conv_refs)

    @pl.when(i % 2 == 0)
    def _():
        step(u_even, u_odd)

    @pl.when(i % 2 == 1)
    def _():
        step(u_odd, u_even)


def _inproj0(x, g, mod, w, conv_w, conv_b, bias, layer, *, is_ctx, tm, cast=()):
    G, R, _ = x.shape
    n_tiles = R // tm
    n_steps = n_tiles + (1 if n_tiles > 1 else 0)
    cos, sin = _rope2d_tables()
    cos, sin = jnp.asarray(cos), jnp.asarray(sin)
    proj_tile = lambda i: jnp.minimum(i, n_tiles - 1)
    conv_tile = (lambda i: jnp.maximum(i - 1, 0)) if n_tiles > 1 else (lambda i: i)
    if is_ctx:
        tab = pl.BlockSpec((tm, LANES), lambda b, i: (0, 0))
    else:
        tab = pl.BlockSpec((tm, LANES), lambda b, i: (proj_tile(i), 0))
    const = lambda a: pl.BlockSpec(a.shape, lambda b, i: (0, 0))
    hy_out = jax.ShapeDtypeStruct((G, R, HY_WIDTH), F32)
    hy_spec = pl.BlockSpec((None, tm, HY_WIDTH), lambda b, i: (b, conv_tile(i), 0))
    if R == SEQ:
        z_out = jax.ShapeDtypeStruct((G, SEQ_P, HY_WIDTH), F32)
        z_spec = pl.BlockSpec((None, tm // FFT_R * FFT_PITCH, HY_WIDTH), lambda b, i: (b, conv_tile(i), 0))
    else:
        z_out, z_spec = hy_out, hy_spec
    kern = functools.partial(_inproj0_kernel, row=2 if is_ctx else None, rope=not is_ctx, n_tiles=n_tiles)
    in_specs = [
        pl.BlockSpec((None, tm, D_MODEL), lambda b, i: (b, proj_tile(i), 0)),
        pl.BlockSpec((1, D_MODEL), lambda b, i: (0, 0)),
        pl.BlockSpec((None, 8, D_MODEL), lambda b, i: (layer, 0, 0)),
        pl.BlockSpec((None, 8, D_MODEL), lambda b, i: (layer, 0, 1)),
        pl.BlockSpec((D_MODEL, EV_COLS), lambda b, i: (0, 0), pipeline_mode=pl.Buffered(1)),
        tab, tab, const(conv_w), const(conv_b), const(bias),
    ]
    out_specs = [z_spec, hy_spec, hy_spec, pl.BlockSpec((None, tm, EV_QKV), lambda b, i: (b, proj_tile(i), 0))]
    out_shape = [z_out, hy_out, hy_out, jax.ShapeDtypeStruct((G, R, EV_QKV), BF16)]
    c_in, c_out, c_shapes, c_args = _cast_jobs(cast, n_steps)
    if cast:
        assert G * n_steps >= CAST_CHUNKS
        kern = _with_casts(kern, len(in_specs), len(out_specs), len(cast))
    res = pl.pallas_call(
        kern,
        out_shape=out_shape + c_shapes,
        grid=(G, n_steps),
        in_specs=in_specs + c_in,
        out_specs=out_specs + c_out,
        scratch_shapes=[pltpu.VMEM((tm + 2 * CONV_PAD, EV_U), F32)] * 2,
        compiler_params=_params(("arbitrary", "arbitrary") if cast else ("parallel", "arbitrary"), VMEM_SMALL_MIB),
        name="inproj0_ctx" if is_ctx else "inproj0",
    )(x, g, mod, mod, w, cos, sin, conv_w, conv_b, bias, *c_args)
    return res[:4], res[4:]


def _inproj1_kernel(x_ref, g_ref, sh_ref, sc_ref, w_ref, o_ref, *, row, tn):
    r = pl.program_id(0) if row is None else row
    g, sh, sc = g_ref[...], _mod_row(sh_ref, r), _mod_row(sc_ref, r)
    sub = min(x_ref.shape[0], SUB_ROWS)
    for s in range(x_ref.shape[0] // sub):
        rows = slice(s * sub, (s + 1) * sub)
        h = _norm_mod(x_ref[rows, :], g, sh, sc).astype(BF16)
        for j in range(w_ref.shape[1] // tn):
            cols = slice(j * tn, (j + 1) * tn)
            o_ref[rows, cols] = jnp.dot(h, w_ref[:, cols], preferred_element_type=F32).astype(o_ref.dtype)


def _inproj1(x, g, mod, w, layer, *, is_ctx, tm, tn, cast=()):
    G, R, _ = x.shape
    N = w.shape[1]
    kern = functools.partial(_inproj1_kernel, row=2 if is_ctx else None, tn=tn)
    in_specs = [
        pl.BlockSpec((None, tm, D_MODEL), lambda b, i: (b, i, 0)),
        pl.BlockSpec((1, D_MODEL), lambda b, i: (0, 0)),
        pl.BlockSpec((None, 8, D_MODEL), lambda b, i: (layer, 0, 0)),
        pl.BlockSpec((None, 8, D_MODEL), lambda b, i: (layer, 0, 1)),
        pl.BlockSpec((D_MODEL, N), lambda b, i: (0, 0), pipeline_mode=pl.Buffered(1)),
    ]
    c_in, c_out, c_shapes, c_args = _cast_jobs(cast, R // tm)
    if cast:
        assert G * (R // tm) >= CAST_CHUNKS
        kern = _with_casts(kern, len(in_specs), 1, len(cast))
    res = pl.pallas_call(
        kern,
        out_shape=[jax.ShapeDtypeStruct((G, R, N), BF16)] + c_shapes,
        grid=(G, R // tm),
        in_specs=in_specs + c_in,
        out_specs=[pl.BlockSpec((None, tm, N), lambda b, i: (b, i, 0))] + c_out,
        compiler_params=_params(("arbitrary", "arbitrary") if cast else ("parallel", "parallel"), VMEM_SMALL_MIB),
        name="inproj1_ctx" if is_ctx else "inproj1",
    )(x, g, mod, mod, w, *c_args)
    return res[0], res[1:]


def _split_bf16(a):
    hi = a.astype(BF16)
    return hi, (a - hi.astype(F32)).astype(BF16)


def _dot3_split(a_hi, a_lo, b):
    b_hi, b_lo = _split_bf16(b)
    dot = functools.partial(jnp.dot, preferred_element_type=F32)
    return dot(a_hi, b_hi) + (dot(a_lo, b_hi) + dot(a_hi, b_lo))


def _dot3(a, b):
    return _dot3_split(*_split_bf16(a), b)


def _store_pitched(ref, lead, value):
    for k in range(value.shape[0] // FFT_R):
        ref[lead, k * FFT_PITCH:k * FFT_PITCH + FFT_R, :] = value[k * FFT_R:(k + 1) * FFT_R]
        ref[lead, k * FFT_PITCH + FFT_R:(k + 1) * FFT_PITCH, :] = jnp.zeros(
            (FFT_PITCH - FFT_R, value.shape[1]), value.dtype)


def _filter_kernel(w1t_ref, b1_ref, w2t_ref, b2_ref, fr_ref, w3_ref, dec_ref, bands_ref, h_ref, ss_ref, *,
                   L, tm, pitched):
    i = pl.program_id(0)
    t_row = (lax.broadcasted_iota(jnp.int32, (1, tm), 1) + i * tm).astype(F32) / L
    ang = (2.0 * math.pi * t_row) * bands_ref[...]
    t8 = jnp.where(lax.broadcasted_iota(jnp.int32, (8, tm), 0) == 0, t_row, 0.0)
    feat = jnp.concatenate([t8, jnp.cos(ang), -jnp.sin(ang)], axis=0)
    fr = fr_ref[...]
    hid = jnp.sin(fr * (jnp.dot(w1t_ref[...], feat, precision=HIGHEST, preferred_element_type=F32) + b1_ref[...]))
    hid = jnp.sin(fr * (jnp.dot(w2t_ref[...], hid, precision=HIGHEST, preferred_element_type=F32) + b2_ref[...]))
    hid_hi, hid_lo = _split_bf16(hid.T)
    m_col = lax.broadcasted_iota(jnp.int32, (tm, 1), 0) + i * tm
    t_col = m_col.astype(F32) / L
    energy = jnp.zeros((1, HY_WIDTH), F32)
    for side in range(2):
        h = _dot3_split(hid_hi, hid_lo, w3_ref[side]) * jnp.exp(-t_col * jnp.abs(dec_ref[side]))
        if side == 1:
            h = jnp.where(m_col == 0, 0.0, h)
        if pitched:
            _store_pitched(h_ref, side, h)
        else:
            h_ref[side] = h
        energy = energy + jnp.sum(h * h, axis=0, keepdims=True)

    @pl.when(i == 0)
    def _():
        ss_ref[...] = jnp.zeros_like(ss_ref)

    ss_ref[...] += jnp.broadcast_to(energy, ss_ref.shape)


def _hyena_filter(w1, b1, w2, b2, w3, freq, decay, *, L):
    tm = min(L, 1024)
    pitched = L == SEQ
    tm_out, l_out = (tm // FFT_R * FFT_PITCH, SEQ_P) if pitched else (tm, L)
    kern = functools.partial(_filter_kernel, L=L, tm=tm, pitched=pitched)
    w1t = jnp.concatenate([w1[0:1], jnp.zeros((7, HY_HIDDEN), F32), w1[1:]], axis=0).T
    col = lambda a: a.reshape(HY_HIDDEN, 1)
    w3s = jnp.stack([w3[:, :HY_WIDTH], w3[:, HY_WIDTH:]])
    bands = jnp.asarray(np.linspace(1e-4, HY_BANDS - 1, HY_BANDS).astype(np.float32).reshape(HY_BANDS, 1))
    args = (w1t, col(b1), w2.T, col(b2), col(freq), w3s, decay.reshape(2, 1, HY_WIDTH), bands)
    full = lambda a: pl.BlockSpec(a.shape, lambda i: (0,) * a.ndim)
    h, ss = pl.pallas_call(
        kern,
        out_shape=(jax.ShapeDtypeStruct((2, l_out, HY_WIDTH), F32), jax.ShapeDtypeStruct((8, HY_WIDTH), F32)),
        grid=(L // tm,),
        in_specs=[full(a) for a in args],
        out_specs=(pl.BlockSpec((2, tm_out, HY_WIDTH), lambda i: (0, i, 0)),
                   pl.BlockSpec((8, HY_WIDTH), lambda i: (0, 0))),
        compiler_params=_params(("arbitrary",), VMEM_SMALL_MIB),
        name="hyena_filter_%d" % L,
    )(*args)
    return h.reshape(2 * l_out, HY_WIDTH), ss


def _stack_bf16(re, im):
    return jnp.concatenate([re, im], axis=0).astype(BF16)


def _filter_fft_kernel(h_ref, ss_ref, m1_ref, f2_ref, o_ref, a_scr):
    R, P = FFT_R, FFT_PITCH
    inv_norm = lax.rsqrt(ss_ref[0:1, :])

    def stage1(n2, carry):
        neg = jnp.where(n2 == 0, 0, R - n2)
        xs = jnp.concatenate([h_ref[pl.ds(n2, R // 2, stride=P), :],
                              h_ref[pl.ds(SEQ_P + neg, R // 2, stride=P), :]], axis=0)
        xs = (xs * inv_norm).astype(BF16)
        a = jnp.dot(m1_ref[n2], xs, preferred_element_type=F32)
        base = pl.multiple_of(n2 * P, 8)
        a_scr[0, pl.ds(base, R), :] = a[:R]
        a_scr[1, pl.ds(base, R), :] = a[R:]
        return carry

    lax.fori_loop(0, R, stage1, 0, unroll=FFT_UNROLL)

    ct = a_scr.shape[-1]

    def stage2(pair, carry):
        k1s = (2 * pair, 2 * pair + 1)
        rhs = jnp.concatenate([_stack_bf16(a_scr[0, pl.ds(k1, R, stride=P), :],
                                           a_scr[1, pl.ds(k1, R, stride=P), :]) for k1 in k1s], axis=1)
        x = jnp.dot(f2_ref[...], rhs, preferred_element_type=F32)
        for d, k1 in enumerate(k1s):
            base = pl.multiple_of(k1 * R, R)
            o_ref[0, pl.ds(base, R), :] = x[:R, d * ct:(d + 1) * ct].astype(BF16)
            o_ref[1, pl.ds(base, R), :] = x[R:, d * ct:(d + 1) * ct].astype(BF16)
        return carry

    lax.fori_loop(0, R // 2, stage2, 0, unroll=FFT_UNROLL // 2)


def _filter_fft(h_circ, ss, m1, f2):
    ct = LANES
    once = pl.Buffered(1)
    return pl.pallas_call(
        _filter_fft_kernel,
        out_shape=jax.ShapeDtypeStruct((2, FFT_N, HY_WIDTH), BF16),
        grid=(HY_WIDTH // ct,),
        in_specs=[
            pl.BlockSpec((2 * SEQ_P, ct), lambda c: (0, c), pipeline_mode=once),
            pl.BlockSpec((8, ct), lambda c: (0, c)),
            pl.BlockSpec(m1.shape, lambda c: (0, 0, 0), pipeline_mode=once),
            pl.BlockSpec(f2.shape, lambda c: (0, 0), pipeline_mode=once),
        ],
        out_specs=pl.BlockSpec((2, FFT_N, ct), lambda c: (0, 0, c)),
        scratch_shapes=[pltpu.VMEM((2, FFT_R * FFT_PITCH, ct), F32)],
        compiler_params=_params(("parallel",), VMEM_FFT_MIB),
        name="hyena_filter_fft",
    )(h_circ, ss, m1, f2)


def _fft_conv_kernel(z_ref, hs_ref, m1_ref, f2_ref, g2_ref, m1i_ref, y_ref, a_scr):
    R, P = FFT_R, FFT_PITCH
    ct = a_scr.shape[-1]

    def stage1(n2, carry):
        rows = pl.ds(n2, R // 2, stride=P)
        xs = jnp.concatenate([z_ref[0, rows, :], z_ref[1, rows, :]], axis=1).astype(BF16)
        t = jnp.dot(m1_ref[n2, :, 0:R // 2], xs, preferred_element_type=F32)
        base = pl.multiple_of(n2 * P, 8)
        a_scr[0, pl.ds(base, R), :] = t[:R, :ct] - t[R:, ct:]
        a_scr[1, pl.ds(base, R), :] = t[R:, :ct] + t[:R, ct:]
        return carry

    lax.fori_loop(0, R, stage1, 0, unroll=FFT_UNROLL)

    def stage2(pair, carry):
        k1s = (2 * pair, 2 * pair + 1)
        rhs = jnp.concatenate([_stack_bf16(a_scr[0, pl.ds(k1, R, stride=P), :],
                                           a_scr[1, pl.ds(k1, R, stride=P), :]) for k1 in k1s], axis=1)
        x = jnp.dot(f2_ref[...], rhs, preferred_element_type=F32)
        prods = []
        for d, k1 in enumerate(k1s):
            base = pl.multiple_of(k1 * R, R)
            hr = hs_ref[0, pl.ds(base, R), :].astype(F32)
            hi = hs_ref[1, pl.ds(base, R), :].astype(F32)
            xr, xi = x[:R, d * ct:(d + 1) * ct], x[R:, d * ct:(d + 1) * ct]
            prods.append(_stack_bf16(xr * hr - xi * hi, xr * hi + xi * hr))
        bq = jnp.dot(g2_ref[...], jnp.concatenate(prods, axis=1), preferred_element_type=F32)
        for d, k1 in enumerate(k1s):
            a_scr[0, pl.ds(k1, R, stride=P), :] = bq[:R, d * ct:(d + 1) * ct]
            a_scr[1, pl.ds(k1, R, stride=P), :] = bq[R:, d * ct:(d + 1) * ct]
        return carry

    lax.fori_loop(0, R // 2, stage2, 0, unroll=FFT_UNROLL // 2)

    def stage3(n2, carry):
        base = pl.multiple_of(n2 * P, 8)
        br = a_scr[0, pl.ds(base, R), :]
        bi = a_scr[1, pl.ds(base, R), :]
        rhs = jnp.concatenate([jnp.concatenate([br, bi], axis=1),
                               jnp.concatenate([bi, -br], axis=1)], axis=0).astype(BF16)
        y = jnp.dot(m1i_ref[n2], rhs, preferred_element_type=F32)
        rows = pl.ds(n2, R // 2, stride=P)
        y_ref[0, rows, :] = y[:, :ct]
        y_ref[1, rows, :] = y[:, ct:]
        return carry

    lax.fori_loop(0, R, stage3, 0, unroll=FFT_UNROLL)
    for b in range(2):
        for n1 in range(R // 2):
            y_ref[b, n1 * P + R:(n1 + 1) * P, :] = jnp.zeros((P - R, ct), F32)


def _fft_conv(z, hspec, m1, f2, g2, m1i):
    assert z.shape[0] == 2, "the batch pair is packed into one complex signal"
    ct = LANES
    once = pl.Buffered(1)
    return pl.pallas_call(
        _fft_conv_kernel,
        out_shape=jax.ShapeDtypeStruct((2, SEQ_P, HY_WIDTH), F32),
        grid=(HY_WIDTH // ct,),
        in_specs=[
            pl.BlockSpec((2, SEQ_P, ct), lambda c: (0, 0, c), pipeline_mode=once),
            pl.BlockSpec((2, FFT_N, ct), lambda c: (0, 0, c), pipeline_mode=once),
            pl.BlockSpec(m1.shape, lambda c: (0, 0, 0), pipeline_mode=once),
            pl.BlockSpec(f2.shape, lambda c: (0, 0), pipeline_mode=once),
            pl.BlockSpec(g2.shape, lambda c: (0, 0), pipeline_mode=once),
            pl.BlockSpec(m1i.shape, lambda c: (0, 0, 0), pipeline_mode=once),
        ],
        out_specs=pl.BlockSpec((2, SEQ_P, ct), lambda c: (0, 0, c), pipeline_mode=once),
        scratch_shapes=[pltpu.VMEM((2, FFT_R * FFT_PITCH, ct), F32)],
        compiler_params=_params(("arbitrary",), VMEM_FFT_MIB),
        name="hyena_fft_conv",
    )(z, hspec, m1, f2, g2, m1i)


def _ctx_conv_kernel(z_ref, h_ref, ss_ref, fwd_ref, inv_ref, fwd_h_ref, y_ref):
    N = 2 * CTX_LEN
    hn = (h_ref[...] * lax.rsqrt(ss_ref[0:1, :])).astype(BF16)
    hs = jnp.dot(fwd_h_ref[...], hn, preferred_element_type=F32)
    zs = jnp.dot(fwd_ref[:, :CTX_LEN], z_ref[...].astype(BF16), preferred_element_type=F32)
    hr, hi, zr, zi = hs[:N], hs[N:], zs[:N], zs[N:]
    ys = _stack_bf16(zr * hr - zi * hi, zr * hi + zi * hr)
    y_ref[...] = jnp.dot(inv_ref[...], ys, preferred_element_type=F32)


def _ctx_conv(z, h_circ, ss, fwd, inv, fwd_h):
    G = z.shape[0]
    full = lambda a: pl.BlockSpec(a.shape, lambda b: (0,) * a.ndim)
    return pl.pallas_call(
        _ctx_conv_kernel,
        out_shape=jax.ShapeDtypeStruct((G, CTX_LEN, HY_WIDTH), F32),
        grid=(G,),
        in_specs=[pl.BlockSpec((None, CTX_LEN, HY_WIDTH), lambda b: (b, 0, 0)),
                  full(h_circ), full(ss), full(fwd), full(inv), full(fwd_h)],
        out_specs=pl.BlockSpec((None, CTX_LEN, HY_WIDTH), lambda b: (b, 0, 0)),
        compiler_params=_params(("parallel",), VMEM_SMALL_MIB),
        name="hyena_ctx_conv",
    )(z, h_circ, ss, fwd, inv, fwd_h)


def _kv_group(refs, g, col0):
    cols = slice(col0 + g * LANES, col0 + (g + 1) * LANES)
    return refs[0][:, cols] if len(refs) == 1 else jnp.concatenate([r[:, cols] for r in refs], axis=0)


def _attn_logits(q_ref, k_refs, g, masks, col0=0):
    low = lax.broadcasted_iota(jnp.int32, (BLOCK, LANES), 1) < HEAD_DIM
    zero = jnp.zeros((), BF16)
    parts = []
    for hh in range(ATT_GROUP):
        h = g * ATT_GROUP + hh
        tile = q_ref[:, (h // 2) * LANES:(h // 2 + 1) * LANES]
        parts.append(jnp.where(low if h % 2 == 0 else ~low, tile, zero))
    s = lax.dot_general(jnp.concatenate(parts, axis=0), _kv_group(k_refs, g, col0), (((1,), (1,)), ((), ())),
                        preferred_element_type=F32)
    if masks is not None:
        ok_prev, ok_next = masks
        s = jnp.concatenate([jnp.where(ok_prev, s[:, :BLOCK], NEG_INF), s[:, BLOCK:2 * BLOCK],
                             jnp.where(ok_next, s[:, 2 * BLOCK:3 * BLOCK], NEG_INF), s[:, 3 * BLOCK:]], axis=1)
    return s


def _attn_output(s, sink_ref, v_refs, g, o_ref, col0=0):
    low = lax.broadcasted_iota(jnp.int32, (BLOCK, LANES), 1) < HEAD_DIM
    low4 = jnp.concatenate([low] * ATT_GROUP, axis=0)
    v = _kv_group(v_refs, g, col0)
    v_aug = jnp.where(lax.broadcasted_iota(jnp.int32, v.shape, 1) < HEAD_DIM, v, jnp.ones((), BF16))
    sink = jnp.concatenate(
        [jnp.full((BLOCK, 1), sink_ref[g * ATT_GROUP + hh] * LOG2E, F32) for hh in range(ATT_GROUP)], axis=0)
    m = jnp.maximum(jnp.max(s, axis=-1, keepdims=True), sink)
    e = jnp.exp2(s - m).astype(BF16)
    o = jnp.dot(e, v_aug, preferred_element_type=F32) + jnp.where(low4, 0.0, jnp.exp2(sink - m))
    swapped = pltpu.roll(o, HEAD_DIM, axis=1)
    for pair in range(ATT_GROUP // 2):
        ev = slice((2 * pair) * BLOCK, (2 * pair + 1) * BLOCK)
        od = slice((2 * pair + 1) * BLOCK, (2 * pair + 2) * BLOCK)
        even = o[ev] / swapped[ev]
        odd = swapped[od] / o[od]
        t = g * (ATT_GROUP // 2) + pair
        o_ref[:, t * LANES:(t + 1) * LANES] = jnp.where(low, even, odd).astype(o_ref.dtype)


def _attn_ctx_kernel(sink_ref, q_ref, kx_ref, vx_ref, o_ref):
    for g in range(ATT_KV_HEADS):
        _attn_output(_attn_logits(q_ref, (kx_ref,), g, None), sink_ref, (vx_ref,), g, o_ref)


def _attn_local_kernel(sink_ref, q_ref, kv_m2, kv_m1, kv_0, kv_p1, kv_ctx, o_ref, s_even, s_odd, *, n_blocks):
    n = pl.program_id(1)
    k_refs, v_refs, v_col = (kv_m1, kv_0, kv_p1, kv_ctx), (kv_m2, kv_m1, kv_0, kv_ctx), 2 * LANES

    @pl.when(n == 0)
    def _():
        s_odd[...] = jnp.zeros_like(s_odd)

    def step(s_new, s_prev):
        nq = jnp.minimum(n, n_blocks - 1)
        qi = lax.broadcasted_iota(jnp.int32, (ATT_GROUP * BLOCK, BLOCK), 0) % BLOCK
        ki = lax.broadcasted_iota(jnp.int32, (ATT_GROUP * BLOCK, BLOCK), 1)
        masks = ((ki >= qi) & (nq > 0), (ki <= qi) & (nq < n_blocks - 1))
        for g in range(ATT_KV_HEADS):
            s_new[g] = _attn_logits(q_ref, k_refs, g, masks)
        for g in range(ATT_KV_HEADS):
            _attn_output(s_prev[g], sink_ref, v_refs, g, o_ref, v_col)

    @pl.when(n % 2 == 0)
    def _():
        step(s_even, s_odd)

    @pl.when(n % 2 == 1)
    def _():
        step(s_odd, s_even)


def _attention(sink, qkv, qkv_ctx, *, local):
    src = qkv if local else qkv_ctx
    G, R, _ = src.shape
    nb = R // BLOCK
    kcol, vcol = ATT_WIDTH // 256, ATT_WIDTH // 256 + 1
    ctx_k = pl.BlockSpec((None, CTX_LEN, 256), lambda b, n: (b, 0, kcol))
    ctx_v = pl.BlockSpec((None, CTX_LEN, 256), lambda b, n: (b, 0, vcol))
    smem = pl.BlockSpec(memory_space=pltpu.SMEM)
    out_shape = jax.ShapeDtypeStruct((G, R, ATT_WIDTH), BF16)
    if not local:
        rows = pl.BlockSpec((None, BLOCK, ATT_WIDTH), lambda b, n: (b, n, 0))
        return pl.pallas_call(
            _attn_ctx_kernel, out_shape=out_shape, grid=(G, nb),
            in_specs=[smem, rows, ctx_k, ctx_v], out_specs=rows,
            compiler_params=_params(("parallel", "parallel"), VMEM_SMALL_MIB), name="attn_ctx",
        )(sink, qkv_ctx, qkv_ctx, qkv_ctx)

    q_blk = lambda n: jnp.minimum(n, nb - 1)
    o_blk = lambda n: jnp.maximum(n - 1, 0)

    kv_w = EV_QKV - ATT_WIDTH
    kv_col = ATT_WIDTH // kv_w

    def kv_rows(off):
        return pl.BlockSpec((None, BLOCK, kv_w), lambda b, n: (b, jnp.clip(n + off, 0, nb - 1), kv_col))

    in_specs = [smem, pl.BlockSpec((None, BLOCK, ATT_WIDTH), lambda b, n: (b, q_blk(n), 0)),
                kv_rows(-2), kv_rows(-1), kv_rows(0), kv_rows(1),
                pl.BlockSpec((None, CTX_LEN, kv_w), lambda b, n: (b, 0, kv_col))]
    logits_scratch = pltpu.VMEM((ATT_KV_HEADS, ATT_GROUP * BLOCK, 3 * BLOCK + CTX_LEN), F32)
    return pl.pallas_call(
        functools.partial(_attn_local_kernel, n_blocks=nb),
        out_shape=out_shape,
        grid=(G, nb + 1),
        in_specs=in_specs,
        out_specs=pl.BlockSpec((None, BLOCK, ATT_WIDTH), lambda b, n: (b, o_blk(n), 0)),
        scratch_shapes=[logits_scratch, logits_scratch],
        compiler_params=_params(("parallel", "arbitrary"), VMEM_SMALL_MIB),
        name="attn_local",
    )(sink, qkv, qkv, qkv, qkv, qkv, qkv_ctx)


def _mixer_out0(rows, x0_ref, y_ref, zb_ref, att_ref, wo_ref):
    if y_ref.shape[0] != x0_ref.shape[0]:
        y = jnp.concatenate([y_ref[r // FFT_R * FFT_PITCH:r // FFT_R * FFT_PITCH + FFT_R, :]
                             for r in range(rows.start, rows.stop, FFT_R)], axis=0)
    else:
        y = y_ref[rows, :]
    hy = (x0_ref[rows, :] * y + zb_ref[rows, :]).astype(BF16)
    return (jnp.dot(hy, wo_ref[:HY_WIDTH, :], preferred_element_type=F32)
            + jnp.dot(att_ref[rows, :], wo_ref[HY_WIDTH:, :], preferred_element_type=F32))


def _mixer_out1(rows, of_ref, ob_ref, wo_ref):
    a = (of_ref[rows, :].astype(F32) + ob_ref[rows, :].astype(F32)).astype(BF16)
    return jnp.dot(a, wo_ref[...], preferred_element_type=F32)


def _mix_kernel(*refs, mixer_out, n_mix, row, fc, final_norm):
    x_ref = refs[0]
    mix_refs = refs[1:2 + n_mix]
    gta_ref, gm_ref, shm_ref, scm_ref, gtm_ref, w1_ref, w2_ref, fg_ref, o_ref, a_scr = refs[2 + n_mix:]
    r = pl.program_id(0) if row is None else row
    gta, gtm = _mod_row(gta_ref, r), _mod_row(gtm_ref, r)
    gm, shm, scm = gm_ref[...], _mod_row(shm_ref, r), _mod_row(scm_ref, r)
    sub = min(x_ref.shape[0], SUB_ROWS)
    for s in range(x_ref.shape[0] // sub):
        rows = slice(s * sub, (s + 1) * sub)
        x1 = x_ref[rows, :] + gta * mixer_out(rows, *mix_refs)
        h = _norm_mod(x1, gm, shm, scm).astype(BF16)
        for c in range(D_FF // fc):
            a = jnp.maximum(jnp.dot(h, w1_ref[:, c * fc:(c + 1) * fc], preferred_element_type=F32), 0.0)
            a_scr[rows, c * fc:(c + 1) * fc] = (a * a).astype(BF16)
        out = x1 + gtm * jnp.dot(a_scr[rows, :], w2_ref[...], preferred_element_type=F32)
        if final_norm:
            out = (out * lax.rsqrt(jnp.mean(out * out, axis=-1, keepdims=True) + EPS)) * fg_ref[...]
        o_ref[rows, :] = out


def _mix_mlp(kind, x, mix_in, wo, mod, gm, w1, w2, fg, layer, *, is_ctx, tm, fc, final_norm, cast=()):
    G, R, _ = x.shape
    row_spec = lambda w: pl.BlockSpec((None, tm, w), lambda b, i: (b, i, 0))
    modk = lambda k: pl.BlockSpec((None, 8, D_MODEL), lambda b, i: (layer, 0, k))
    vec = pl.BlockSpec((1, D_MODEL), lambda b, i: (0, 0))
    resident = lambda a: pl.BlockSpec(a.shape, lambda b, i: (0, 0), pipeline_mode=pl.Buffered(1))
    if kind == 0:
        mixer_out = _mixer_out0
        y_rows = tm if mix_in[1].shape[1] == R else tm // FFT_R * FFT_PITCH
        y_spec = pl.BlockSpec((None, y_rows, HY_WIDTH), lambda b, i: (b, i, 0))
        mix_specs = [row_spec(HY_WIDTH), y_spec, row_spec(HY_WIDTH), row_spec(ATT_WIDTH)]
    else:
        mixer_out = _mixer_out1
        mix_specs = [row_spec(RET_V)] * 2
    kern = functools.partial(_mix_kernel, mixer_out=mixer_out, n_mix=len(mix_in), row=2 if is_ctx else None,
                             fc=fc, final_norm=final_norm)
    in_specs = [row_spec(D_MODEL)] + mix_specs + [
        resident(wo), modk(2), vec, modk(3), modk(4), modk(5), resident(w1), resident(w2), vec]
    c_in, c_out, c_shapes, c_args = _cast_jobs(cast, R // tm)
    if cast:
        assert G * (R // tm) >= CAST_CHUNKS
        kern = _with_casts(kern, len(in_specs), 1, len(cast))
    res = pl.pallas_call(
        kern,
        out_shape=[jax.ShapeDtypeStruct((G, R, D_MODEL), F32)] + c_shapes,
        grid=(G, R // tm),
        in_specs=in_specs + c_in,
        out_specs=[row_spec(D_MODEL)] + c_out,
        scratch_shapes=[pltpu.VMEM((tm, D_FF), BF16)],
        compiler_params=_params(("arbitrary", "arbitrary") if cast else ("parallel", "parallel"), VMEM_DENSE_MIB),
        name="mix_mlp%d%s" % (kind, "_ctx" if is_ctx else ""),
    )(x, *mix_in, wo, mod, gm, mod, mod, mod, w1, w2, fg, *c_args)
    return res[0], res[1:]


def _ret_kernel(lr_ref, qkv_c, qkv_f, g_f, qkv_b, g_b, cos_ref, sin_ref, of_ref, ob_ref,
                state, dmask, xi, zeta, gch):
    j = pl.program_id(1)
    C = RET_C
    kscale = RET_DK ** -0.5

    @pl.when(j == 0)
    def _():
        state[...] = jnp.zeros_like(state)
        row = lax.broadcasted_iota(jnp.int32, (C, C), 0).astype(F32)
        col = lax.broadcasted_iota(jnp.int32, (C, C), 1).astype(F32)
        for d in range(2):
            e = row - col if d == 0 else col - row
            p = row if d == 0 else (C - 1) - row
            for h in range(RET_HEADS):
                lg = -jnp.exp(jnp.full((C, C), lr_ref[d, h], F32))
                dmask[d, h] = jnp.where(e >= 0, jnp.exp(e * lg), 0.0) * kscale
                xi[d, h] = jnp.exp((p + 1.0) * lg)
                zeta[d, h] = jnp.exp(((C - 1) - p) * lg) * kscale
                gch[d, h] = jnp.exp(C * -jnp.exp(jnp.full((8, C), lr_ref[d, h], F32)))

    half = RET_DK // 2

    def chain(d, h, src_ref, g_ref, o_ref):
        cos, sin = cos_ref[d], sin_ref[d]

        def rot(c0):
            t1 = src_ref[:, c0:c0 + half].astype(F32)
            t2 = src_ref[:, c0 + half:c0 + RET_DK].astype(F32)
            return jnp.concatenate([t1 * cos - t2 * sin, t2 * cos + t1 * sin], axis=1)

        q = rot(h * RET_DK)
        k = rot(RET_QK + h * RET_DK)
        v = src_ref[:, 2 * RET_QK + h * RET_DV:2 * RET_QK + (h + 1) * RET_DV]
        inner = lax.dot_general(q.astype(BF16), k.astype(BF16), (((1,), (1,)), ((), ())),
                                preferred_element_type=F32) * dmask[d, h]
        s_old = state[d, h]
        kv = lax.dot_general((k * zeta[d, h]).astype(BF16), v, (((0,), (0,)), ((), ())),
                             preferred_element_type=F32)
        state[d, h] = gch[d, h, 0:1, 0:1] * s_old + kv
        if o_ref is not None:
            o = (jnp.dot(inner.astype(BF16), v, preferred_element_type=F32)
                 + jnp.dot((q * xi[d, h]).astype(BF16), s_old.astype(BF16), preferred_element_type=F32))
            on = o * lax.rsqrt(jnp.mean(o * o, axis=-1, keepdims=True) + EPS)
            gate = g_ref[:, h * RET_DV:(h + 1) * RET_DV]
            half_gate = gate * 0.5
            silu = half_gate + half_gate * jnp.tanh(half_gate)
            o_ref[:, h * RET_DV:(h + 1) * RET_DV] = (silu.astype(F32) * on).astype(o_ref.dtype)

    @pl.when(j == 0)
    def _():
        for d in range(2):
            for h in range(RET_HEADS):
                chain(d, h, qkv_c, None, None)

    @pl.when(j > 0)
    def _():
        for h in range(RET_HEADS):
            chain(0, h, qkv_f, g_f, of_ref)
            chain(1, h, qkv_b, g_b, ob_ref)


def _retention(log_rate, p_ctx, p_lat):
    G = p_lat.shape[0]
    C = RET_C
    n_lat = SEQ // C
    cos, sin = _rope1d_tables()
    cos, sin = jnp.asarray(cos), jnp.asarray(sin)

    def fwd_chunk(j):
        return jnp.maximum(j - 1, 0)

    def bwd_chunk(j):
        return n_lat - 1 - jnp.maximum(j - 1, 0)

    qkv_w = 2 * RET_QK + RET_V
    gcol = qkv_w // RET_V
    tab = pl.BlockSpec((2, C, RET_DK // 2), lambda b, j: (0, j, 0))
    in_specs = [
        pl.BlockSpec(memory_space=pltpu.SMEM),
        pl.BlockSpec((None, C, qkv_w), lambda b, j: (b, 0, 0)),
        pl.BlockSpec((None, C, qkv_w), lambda b, j: (b, fwd_chunk(j), 0)),
        pl.BlockSpec((None, C, RET_V), lambda b, j: (b, fwd_chunk(j), gcol)),
        pl.BlockSpec((None, C, qkv_w), lambda b, j: (b, bwd_chunk(j), 0)),
        pl.BlockSpec((None, C, RET_V), lambda b, j: (b, bwd_chunk(j), gcol + 1)),
        tab, tab,
    ]
    out = jax.ShapeDtypeStruct((G, SEQ, RET_V), BF16)
    per_chain = lambda *tail: pltpu.VMEM((2, RET_HEADS) + tail, F32)
    return pl.pallas_call(
        _ret_kernel,
        out_shape=(out, out),
        grid=(G, 1 + n_lat),
        in_specs=in_specs,
        out_specs=(pl.BlockSpec((None, C, RET_V), lambda b, j: (b, fwd_chunk(j), 0)),
                   pl.BlockSpec((None, C, RET_V), lambda b, j: (b, bwd_chunk(j), 0))),
        scratch_shapes=[per_chain(RET_DK, RET_DV), per_chain(C, C), per_chain(C, C), per_chain(C, C),
                        per_chain(8, C)],
        compiler_params=_params(("parallel", "arbitrary"), VMEM_SMALL_MIB),
        name="retention",
    )(log_rate, p_ctx, p_lat, p_lat, p_lat, p_lat, cos, sin)


def _ev_weight(w_in):
    i_q = EV_U
    i_k = i_q + ATT_WIDTH
    i_v = i_k + KV_WIDTH
    cols = [w_in[:, :i_q], w_in[:, i_q:i_k] * (HEAD_DIM ** -0.5 * LOG2E)]
    for base in (i_k, i_v):
        for g in range(ATT_KV_HEADS):
            part = w_in[:, base + g * HEAD_DIM: base + (g + 1) * HEAD_DIM]
            cols += [part, part]
    return jnp.concatenate([c.astype(BF16) for c in cols], axis=1)


def kernel(x, c, ctx, c_ctx, ada_w, ada_b, norm_mix_g, norm_mlp_g, mlp_w1, mlp_w2, ev_w_in, ev_w_out, hy_conv_w, hy_conv_b, hy_w1, hy_b1, hy_w2, hy_b2, hy_w3, hy_freq, hy_decay, hy_bias, attn_sink, od_w_in, od_w_out, ret_log_rate, final_g):
    D = D_MODEL
    cvec = jnp.concatenate([c, c_ctx[None, :], jnp.zeros((8 - BATCH - 1, D), F32)], axis=0)
    mod = _ada(cvec, ada_w, ada_b)

    m1, m1i, f2, g2 = (jnp.asarray(a).astype(BF16) for a in _fft_mats())
    cfwd, cinv, cfwd_h = (jnp.asarray(a).astype(BF16) for a in _ctx_fft_mats())
    fg = final_g.reshape(1, D)

    gmix = norm_mix_g[0].reshape(1, D)
    gmlp = norm_mlp_g[0].reshape(1, D)
    w_in = _ev_weight(ev_w_in[0])
    conv = (hy_conv_w[0], hy_conv_b[0].reshape(1, EV_U), hy_bias[0].reshape(1, HY_WIDTH))
    (z_l, x0_l, zb_l, qkv_l), (wo, w1, w2) = _inproj0(
        x, gmix, mod, w_in, *conv, 0, is_ctx=False, tm=512, cast=((ev_w_out, 0), (mlp_w1, 0), (mlp_w2, 0)))
    (z_c, x0_c, zb_c, qkv_c), _ = _inproj0(ctx, gmix, mod, w_in, *conv, 0, is_ctx=True, tm=CTX_LEN)

    filt = (hy_w1[0], hy_b1[0], hy_w2[0], hy_b2[0], hy_w3[0], hy_freq[0], hy_decay[0])
    h_l, ss_l = _hyena_filter(*filt, L=SEQ)
    h_c, ss_c = _hyena_filter(*filt, L=CTX_LEN)
    hspec = _filter_fft(h_l, ss_l, m1, f2)
    y_l = _fft_conv(z_l, hspec, m1, f2, g2, m1i)
    y_c = _ctx_conv(z_c, h_c, ss_c, cfwd, cinv, cfwd_h)

    sink = attn_sink[0]
    att_l = _attention(sink, qkv_l, qkv_c, local=True)
    att_c = _attention(sink, qkv_l, qkv_c, local=False)

    x1, (w_in,) = _mix_mlp(0, x, (x0_l, y_l, zb_l, att_l), wo, mod, gmlp, w1, w2, fg, 0,
                           is_ctx=False, tm=512, fc=512, final_norm=False, cast=((od_w_in, 0),))
    ctx1, _ = _mix_mlp(0, ctx, (x0_c, y_c, zb_c, att_c), wo, mod, gmlp, w1, w2, fg, 0,
                       is_ctx=True, tm=CTX_LEN, fc=512, final_norm=False)

    gmix = norm_mix_g[1].reshape(1, D)
    gmlp = norm_mlp_g[1].reshape(1, D)
    p_l, (wo, w1, w2) = _inproj1(x1, gmix, mod, w_in, 1, is_ctx=False, tm=512, tn=512,
                                 cast=((od_w_out, 0), (mlp_w1, 1), (mlp_w2, 1)))
    p_c, _ = _inproj1(ctx1, gmix, mod, w_in, 1, is_ctx=True, tm=CTX_LEN, tn=512)
    o_fwd, o_bwd = _retention(ret_log_rate[0], p_c, p_l)

    out, _ = _mix_mlp(1, x1, (o_fwd, o_bwd), wo, mod, gmlp, w1, w2, fg, 1,
                      is_ctx=False, tm=512, fc=512, final_norm=True)
    return out
```

```python
import functools
import math

import numpy as np
import jax
import jax.numpy as jnp
from jax import lax
from jax.experimental import pallas as pl
from jax.experimental.pallas import tpu as pltpu

F32 = jnp.float32
BF16 = jnp.bfloat16
HIGHEST = lax.Precision.HIGHEST

D_MODEL = 1024
BATCH = 2
SEQ = 8192
DEPTH = 2
GRID_W = 64
CTX_LEN = 256
EPS = 1e-6
NEG_INF = -1e30
N_MOD = 6
D_FF = 4 * D_MODEL
ROPE_BASE = 10000.0

HY_WIDTH = D_MODEL // 2
HY_EMB = 33
HY_BANDS = (HY_EMB - 1) // 2
HY_HIDDEN = 64

ATT_HEADS = 8
ATT_KV_HEADS = 2
ATT_GROUP = ATT_HEADS // ATT_KV_HEADS
HEAD_DIM = 64
ATT_WIDTH = ATT_HEADS * HEAD_DIM
KV_WIDTH = ATT_KV_HEADS * HEAD_DIM
BLOCK = 128

RET_HEADS = 4
RET_DK = D_MODEL // RET_HEADS
RET_DV = 2 * RET_DK
RET_QK = RET_HEADS * RET_DK
RET_V = RET_HEADS * RET_DV
OD_IN = 2 * RET_QK + 3 * RET_V

LOG2E = 1.4426950408889634
LANES = 128
MIB = 1024 * 1024

EV_U = 3 * HY_WIDTH
EV_QKV = ATT_WIDTH + 4 * LANES
EV_COLS = EV_U + EV_QKV

FFT_N = 2 * SEQ
FFT_R = 128
FFT_PITCH = FFT_R + 8
SEQ_P = SEQ // FFT_R * FFT_PITCH
CONV_ROWS = 64
CONV_PAD = 8
SUB_ROWS = 256
FFT_UNROLL = 64
RET_C = 256
CAST_CHUNKS = 32


VMEM_SMALL_MIB = 48
VMEM_DENSE_MIB = 56
VMEM_FFT_MIB = 58


def _params(sem, vmem_mib):
    return pltpu.CompilerParams(dimension_semantics=sem, vmem_limit_bytes=vmem_mib * MIB)


@functools.lru_cache(maxsize=None)
def _rope2d_tables():
    quarter = HEAD_DIM // 4
    inv = ROPE_BASE ** (-np.arange(quarter, dtype=np.float64) / quarter)
    t = np.arange(SEQ)
    pos = np.stack([t // GRID_W, t % GRID_W], axis=1).astype(np.float64)
    lane = np.arange(HEAD_DIM)
    half = lane // (HEAD_DIM // 2)
    e = lane % (HEAD_DIM // 2)
    ang = pos[:, half] * inv[e % quarter][None, :]
    sign = np.where(e < quarter, -1.0, 1.0)[None, :]
    cos = np.tile(np.cos(ang), (1, 2)).astype(np.float32)
    sin = np.tile(np.sin(ang) * sign, (1, 2)).astype(np.float32)
    return cos, sin


@functools.lru_cache(maxsize=None)
def _rope1d_tables():
    n = RET_DK // 2
    inv = ROPE_BASE ** (-np.linspace(0.0, 1.0, n))
    pos = np.arange(CTX_LEN + SEQ, dtype=np.float64)
    ang = pos[:, None] * inv[None, :]
    cos, sin = np.cos(ang), np.sin(ang)

    def rev(a):
        return a.reshape(-1, RET_C, n)[:, ::-1].reshape(-1, n)

    cos2 = np.stack([cos, rev(cos)]).astype(np.float32)
    sin2 = np.stack([sin, rev(sin)]).astype(np.float32)
    return cos2, sin2


@functools.lru_cache(maxsize=None)
def _fft_mats():
    N, R = FFT_N, FFT_R
    k1 = np.arange(R)
    n2 = np.arange(R)[:, None, None]
    n1 = np.arange(R // 2)[None, None, :]
    n = np.concatenate([n2 + R * n1, (N - ((R - n2) % R + R * n1)) % N], axis=2)
    idx = (k1[None, :, None] * n) % N
    ang = 2.0 * np.pi * idx / N
    c, s = np.cos(ang), np.sin(ang)
    m1 = np.concatenate([c, -s], axis=1).astype(np.float32)
    m1i = np.concatenate([c.transpose(0, 2, 1), -s.transpose(0, 2, 1)], axis=2)
    m1i = m1i[:, :R // 2].astype(np.float32)
    a2 = 2.0 * np.pi * ((np.arange(R)[:, None] * np.arange(R)[None, :]) % R) / R
    fr, fi = np.cos(a2), -np.sin(a2)
    f2 = np.block([[fr, -fi], [fi, fr]]).astype(np.float32)
    g2 = (np.block([[fr, fi], [-fi, fr]]) / N).astype(np.float32)
    return m1, m1i, f2, g2


@functools.lru_cache(maxsize=None)
def _ctx_fft_mats():
    N = 2 * CTX_LEN
    k = np.arange(N)[:, None]
    n = np.arange(N)[None, :]
    ang = 2.0 * np.pi * ((k * n) % N) / N
    c, s = np.cos(ang), np.sin(ang)
    fwd = np.concatenate([c, -s], axis=0).astype(np.float32)
    inv = (np.concatenate([c, -s], axis=1)[:CTX_LEN] / N).astype(np.float32)
    src = np.concatenate([np.arange(CTX_LEN), (N - np.arange(CTX_LEN)) % N])
    return fwd, inv, fwd[:, src]


def _ada_kernel(c_ref, w_ref, b_ref, o_ref):
    c = c_ref[...]
    a = c * (1.0 / (1.0 + jnp.exp(-c)))
    o_ref[...] = _dot3(a, w_ref[...]) + b_ref[...]


def _ada(cvec, ada_w, ada_b):
    tn = 1536
    return pl.pallas_call(
        _ada_kernel,
        out_shape=jax.ShapeDtypeStruct((DEPTH, 8, N_MOD * D_MODEL), F32),
        grid=(DEPTH, N_MOD * D_MODEL // tn),
        in_specs=[
            pl.BlockSpec((8, D_MODEL), lambda i, j: (0, 0)),
            pl.BlockSpec((None, D_MODEL, tn), lambda i, j: (i, 0, j)),
            pl.BlockSpec((None, 1, tn), lambda i, j: (i, 0, j)),
        ],
        out_specs=pl.BlockSpec((None, 8, tn), lambda i, j: (i, 0, j)),
        compiler_params=_params(("parallel", "parallel"), VMEM_SMALL_MIB),
        name="ada_mod",
    )(cvec, ada_w, ada_b.reshape(DEPTH, 1, N_MOD * D_MODEL))


def _with_casts(body, n_in, n_out, n_cast):
    def kern(*refs):
        srcs = refs[n_in:n_in + n_cast]
        outs_at = n_in + n_cast
        dsts = refs[outs_at + n_out:outs_at + n_out + n_cast]
        for src, dst in zip(srcs, dsts):
            dst[...] = src[...].astype(dst.dtype)
        body(*refs[:n_in], *refs[outs_at:outs_at + n_out], *refs[outs_at + n_out + n_cast:])
    return kern


def _cast_jobs(stacked_weights, steps_per_row):
    in_specs, out_specs, out_shapes, args = [], [], [], []
    chunk = lambda b, i: jnp.minimum(b * steps_per_row + i, CAST_CHUNKS - 1)
    for w, lead in stacked_weights:
        _, rows, cols = w.shape
        blk = rows // CAST_CHUNKS
        in_specs.append(pl.BlockSpec((None, blk, cols), lambda b, i, lead=lead: (lead, chunk(b, i), 0)))
        out_specs.append(pl.BlockSpec((blk, cols), lambda b, i: (chunk(b, i), 0)))
        out_shapes.append(jax.ShapeDtypeStruct((rows, cols), BF16))
        args.append(w)
    return in_specs, out_specs, out_shapes, args


def _mod_row(ref, row):
    if isinstance(row, int):
        return ref[row:row + 1, :]
    return ref[pl.ds(row, 1), :]


def _norm_mod(x, g, shift, scale):
    y = x * lax.rsqrt(jnp.mean(x * x, axis=-1, keepdims=True) + EPS)
    return (y * g) * (1.0 + scale) + shift


def _rope_tile(x, cos, sin_signed):
    lane = lax.broadcasted_iota(jnp.int32, x.shape, 1)
    first = (lane % 32) < 16
    partner = jnp.where(first, pltpu.roll(x, LANES - 16, axis=1), pltpu.roll(x, 16, axis=1))
    return x * cos + partner * sin_signed


def _project0(x_ref, g, sh, sc, w_ref, cos_ref, sin_ref, u_ref, qkv_ref, rope):
    tn = 512
    n_rot = (ATT_WIDTH + 2 * LANES) // LANES
    sub = min(x_ref.shape[0], SUB_ROWS)
    for s in range(x_ref.shape[0] // sub):
        rows = slice(s * sub, (s + 1) * sub)
        urows = slice(CONV_PAD + s * sub, CONV_PAD + (s + 1) * sub)
        h = _norm_mod(x_ref[rows, :], g, sh, sc).astype(BF16)
        for j in range(EV_U // tn):
            u_ref[urows, j * tn:(j + 1) * tn] = jnp.dot(h, w_ref[:, j * tn:(j + 1) * tn],
                                                        preferred_element_type=F32)
        for j in range(EV_QKV // tn):
            y = jnp.dot(h, w_ref[:, EV_U + j * tn:EV_U + (j + 1) * tn], preferred_element_type=F32)
            for t in range(tn // LANES):
                yt = y[:, t * LANES:(t + 1) * LANES]
                if rope and j * (tn // LANES) + t < n_rot:
                    yt = _rope_tile(yt, cos_ref[rows, :], sin_ref[rows, :])
                qkv_ref[rows, j * tn + t * LANES:j * tn + (t + 1) * LANES] = yt.astype(BF16)


def _short_conv_gate(u_ref, cw_ref, cb_ref, bias_ref, z_ref, x0_ref, zb_ref):
    tm = u_ref.shape[0] - 2 * CONV_PAD
    rc = CONV_ROWS
    z_pitched = z_ref.shape[0] != tm

    def conv(r0, c0):
        cols = slice(c0, c0 + LANES)
        taps = [u_ref[CONV_PAD - 1 + k + r0:CONV_PAD - 1 + k + r0 + rc, cols] for k in range(3)]
        return (taps[0] * cw_ref[0:1, cols] + taps[1] * cw_ref[1:2, cols] + taps[2] * cw_ref[2:3, cols]
                + cb_ref[:, cols])

    for r0 in range(0, tm, rc):
        rows = slice(r0, r0 + rc)
        z0 = r0 // FFT_R * FFT_PITCH + r0 % FFT_R if z_pitched else r0
        for c0 in range(0, HY_WIDTH, LANES):
            cols = slice(c0, c0 + LANES)
            x0 = conv(r0, c0)
            z = conv(r0, 2 * HY_WIDTH + c0) * conv(r0, HY_WIDTH + c0)
            z_ref[z0:z0 + rc, cols] = z
            x0_ref[rows, cols] = x0
            zb_ref[rows, cols] = x0 * (z * bias_ref[:, cols])
    if z_pitched:
        for k in range(tm // FFT_R):
            z_ref[k * FFT_PITCH + FFT_R:(k + 1) * FFT_PITCH, :] = jnp.zeros((FFT_PITCH - FFT_R, HY_WIDTH), F32)


def _inproj0_kernel(x_ref, g_ref, sh_ref, sc_ref, w_ref, cos_ref, sin_ref, cw_ref, cb_ref, bias_ref,
                    z_ref, x0_ref, zb_ref, qkv_ref, u_even, u_odd, *, row, rope, n_tiles):
    r = pl.program_id(0) if row is None else row
    i = pl.program_id(1)
    g, sh, sc = g_ref[...], _mod_row(sh_ref, r), _mod_row(sc_ref, r)
    conv_refs = (cw_ref, cb_ref, bias_ref, z_ref, x0_ref, zb_ref)
    zero_row = jnp.zeros((1, EV_U), F32)
    tm = x_ref.shape[0]
    above, first, last, below = CONV_PAD - 1, CONV_PAD, CONV_PAD + tm - 1, CONV_PAD + tm

    if n_tiles == 1:
        u_even[above:first, :] = zero_row
        u_even[below:below + 1, :] = zero_row
        _project0(x_ref, g, sh, sc, w_ref, cos_ref, sin_ref, u_even, qkv_ref, rope)
        _short_conv_gate(u_even, *conv_refs)
        return

    @pl.when(i == 0)
    def _():
        u_even[...] = jnp.zeros_like(u_even)
        u_odd[...] = jnp.zeros_like(u_odd)

    def step(u_new, u_old):
        j = i - 1
        u_old[above:first, :] = jnp.where(j > 0, u_new[last:last + 1, :], zero_row)
        _project0(x_ref, g, sh, sc, w_ref, cos_ref, sin_ref, u_new, qkv_ref, rope)
        u_old[below:below + 1, :] = jnp.where(j < n_tiles - 1, u_new[first:first + 1, :], zero_row)
        _short_conv_gate(u_old, *conv_refs)

    @pl.when((i % 2 == 0) & (i < n_tiles))
    def _():
        step(u_even, u_odd)

    @pl.when((i % 2 == 1) & (i < n_tiles))
    def _():
        step(u_odd, u_even)

    @pl.when(i == n_tiles)
    def _():
        bufs = (u_even, u_odd)
        u_last, u_before = bufs[(n_tiles - 1) % 2], bufs[n_tiles % 2]
        u_last[above:first, :] = u_before[last:last + 1, :]
        u_last[below:below + 1, :] = zero_row
        _short_conv_gate(u_last, *conv_refs)


def _inproj0(x, g, mod, w, conv_w, conv_b, bias, layer, *, is_ctx, tm, cast=()):
    G, R, _ = x.shape
    n_tiles = R // tm
    n_steps = n_tiles + (1 if n_tiles > 1 else 0)
    cos, sin = _rope2d_tables()
    cos, sin = jnp.asarray(cos), jnp.asarray(sin)
    proj_tile = lambda i: jnp.minimum(i, n_tiles - 1)
    conv_tile = (lambda i: jnp.maximum(i - 1, 0)) if n_tiles > 1 else (lambda i: i)
    if is_ctx:
        tab = pl.BlockSpec((tm, LANES), lambda b, i: (0, 0))
    else:
        tab = pl.BlockSpec((tm, LANES), lambda b, i: (proj_tile(i), 0))
    const = lambda a: pl.BlockSpec(a.shape, lambda b, i: (0, 0))
    hy_out = jax.ShapeDtypeStruct((G, R, HY_WIDTH), F32)
    hy_spec = pl.BlockSpec((None, tm, HY_WIDTH), lambda b, i: (b, conv_tile(i), 0))
    if R == SEQ:
        z_out = jax.ShapeDtypeStruct((G, SEQ_P, HY_WIDTH), F32)
        z_spec = pl.BlockSpec((None, tm // FFT_R * FFT_PITCH, HY_WIDTH), lambda b, i: (b, conv_tile(i), 0))
    else:
        z_out, z_spec = hy_out, hy_spec
    kern = functools.partial(_inproj0_kernel, row=2 if is_ctx else None, rope=not is_ctx, n_tiles=n_tiles)
    in_specs = [
        pl.BlockSpec((None, tm, D_MODEL), lambda b, i: (b, proj_tile(i), 0)),
        pl.BlockSpec((1, D_MODEL), lambda b, i: (0, 0)),
        pl.BlockSpec((None, 8, D_MODEL), lambda b, i: (layer, 0, 0)),
        pl.BlockSpec((None, 8, D_MODEL), lambda b, i: (layer, 0, 1)),
        pl.BlockSpec((D_MODEL, EV_COLS), lambda b, i: (0, 0), pipeline_mode=pl.Buffered(1)),
        tab, tab, const(conv_w), const(conv_b), const(bias),
    ]
    out_specs = [z_spec, hy_spec, hy_spec, pl.BlockSpec((None, tm, EV_QKV), lambda b, i: (b, proj_tile(i), 0))]
    out_shape = [z_out, hy_out, hy_out, jax.ShapeDtypeStruct((G, R, EV_QKV), BF16)]
    c_in, c_out, c_shapes, c_args = _cast_jobs(cast, n_steps)
    if cast:
        assert G * n_steps >= CAST_CHUNKS
        kern = _with_casts(kern, len(in_specs), len(out_specs), len(cast))
    res = pl.pallas_call(
        kern,
        out_shape=out_shape + c_shapes,
        grid=(G, n_steps),
        in_specs=in_specs + c_in,
        out_specs=out_specs + c_out,
        scratch_shapes=[pltpu.VMEM((tm + 2 * CONV_PAD, EV_U), F32)] * 2,
        compiler_params=_params(("arbitrary", "arbitrary") if cast else ("parallel", "arbitrary"), VMEM_SMALL_MIB),
        name="inproj0_ctx" if is_ctx else "inproj0",
    )(x, g, mod, mod, w, cos, sin, conv_w, conv_b, bias, *c_args)
    return res[:4], res[4:]


def _inproj1_kernel(x_ref, g_ref, sh_ref, sc_ref, w_ref, o_ref, *, row, tn):
    r = pl.program_id(0) if row is None else row
    g, sh, sc = g_ref[...], _mod_row(sh_ref, r), _mod_row(sc_ref, r)
    sub = min(x_ref.shape[0], SUB_ROWS)
    for s in range(x_ref.shape[0] // sub):
        rows = slice(s * sub, (s + 1) * sub)
        h = _norm_mod(x_ref[rows, :], g, sh, sc).astype(BF16)
        for j in range(w_ref.shape[1] // tn):
            cols = slice(j * tn, (j + 1) * tn)
            o_ref[rows, cols] = jnp.dot(h, w_ref[:, cols], preferred_element_type=F32).astype(o_ref.dtype)


def _inproj1(x, g, mod, w, layer, *, is_ctx, tm, tn, cast=()):
    G, R, _ = x.shape
    N = w.shape[1]
    kern = functools.partial(_inproj1_kernel, row=2 if is_ctx else None, tn=tn)
    in_specs = [
        pl.BlockSpec((None, tm, D_MODEL), lambda b, i: (b, i, 0)),
        pl.BlockSpec((1, D_MODEL), lambda b, i: (0, 0)),
        pl.BlockSpec((None, 8, D_MODEL), lambda b, i: (layer, 0, 0)),
        pl.BlockSpec((None, 8, D_MODEL), lambda b, i: (layer, 0, 1)),
        pl.BlockSpec((D_MODEL, N), lambda b, i: (0, 0), pipeline_mode=pl.Buffered(1)),
    ]
    c_in, c_out, c_shapes, c_args = _cast_jobs(cast, R // tm)
    if cast:
        assert G * (R // tm) >= CAST_CHUNKS
        kern = _with_casts(kern, len(in_specs), 1, len(cast))
    res = pl.pallas_call(
        kern,
        out_shape=[jax.ShapeDtypeStruct((G, R, N), BF16)] + c_shapes,
        grid=(G, R // tm),
        in_specs=in_specs + c_in,
        out_specs=[pl.BlockSpec((None, tm, N), lambda b, i: (b, i, 0))] + c_out,
        compiler_params=_params(("arbitrary", "arbitrary") if cast else ("parallel", "parallel"), VMEM_SMALL_MIB),
        name="inproj1_ctx" if is_ctx else "inproj1",
    )(x, g, mod, mod, w, *c_args)
    return res[0], res[1:]


def _split_bf16(a):
    hi = a.astype(BF16)
    return hi, (a - hi.astype(F32)).astype(BF16)


def _dot3_split(a_hi, a_lo, b):
    b_hi, b_lo = _split_bf16(b)
    dot = functools.partial(jnp.dot, preferred_element_type=F32)
    return dot(a_hi, b_hi) + (dot(a_lo, b_hi) + dot(a_hi, b_lo))


def _dot3(a, b):
    return _dot3_split(*_split_bf16(a), b)


def _store_pitched(ref, lead, value):
    for k in range(value.shape[0] // FFT_R):
        ref[lead, k * FFT_PITCH:k * FFT_PITCH + FFT_R, :] = value[k * FFT_R:(k + 1) * FFT_R]
        ref[lead, k * FFT_PITCH + FFT_R:(k + 1) * FFT_PITCH, :] = jnp.zeros(
            (FFT_PITCH - FFT_R, value.shape[1]), value.dtype)


def _filter_kernel(w1t_ref, b1_ref, w2t_ref, b2_ref, fr_ref, w3_ref, dec_ref, bands_ref, h_ref, ss_ref, *,
                   L, tm, pitched):
    i = pl.program_id(0)
    t_row = (lax.broadcasted_iota(jnp.int32, (1, tm), 1) + i * tm).astype(F32) / L
    ang = (2.0 * math.pi * t_row) * bands_ref[...]
    t8 = jnp.where(lax.broadcasted_iota(jnp.int32, (8, tm), 0) == 0, t_row, 0.0)
    feat = jnp.concatenate([t8, jnp.cos(ang), -jnp.sin(ang)], axis=0)
    fr = fr_ref[...]
    hid = jnp.sin(fr * (jnp.dot(w1t_ref[...], feat, precision=HIGHEST, preferred_element_type=F32) + b1_ref[...]))
    hid = jnp.sin(fr * (jnp.dot(w2t_ref[...], hid, precision=HIGHEST, preferred_element_type=F32) + b2_ref[...]))
    hid_hi, hid_lo = _split_bf16(hid.T)
    m_col = lax.broadcasted_iota(jnp.int32, (tm, 1), 0) + i * tm
    t_col = m_col.astype(F32) / L
    energy = jnp.zeros((1, HY_WIDTH), F32)
    for side in range(2):
        h = _dot3_split(hid_hi, hid_lo, w3_ref[side]) * jnp.exp(-t_col * jnp.abs(dec_ref[side]))
        if side == 1:
            h = jnp.where(m_col == 0, 0.0, h)
        if pitched:
            _store_pitched(h_ref, side, h)
        else:
            h_ref[side] = h
        energy = energy + jnp.sum(h * h, axis=0, keepdims=True)

    @pl.when(i == 0)
    def _():
        ss_ref[...] = jnp.zeros_like(ss_ref)

    ss_ref[...] += jnp.broadcast_to(energy, ss_ref.shape)


def _hyena_filter(w1, b1, w2, b2, w3, freq, decay, *, L):
    tm = min(L, 1024)
    pitched = L == SEQ
    tm_out, l_out = (tm // FFT_R * FFT_PITCH, SEQ_P) if pitched else (tm, L)
    kern = functools.partial(_filter_kernel, L=L, tm=tm, pitched=pitched)
    w1t = jnp.concatenate([w1[0:1], jnp.zeros((7, HY_HIDDEN), F32), w1[1:]], axis=0).T
    col = lambda a: a.reshape(HY_HIDDEN, 1)
    w3s = jnp.stack([w3[:, :HY_WIDTH], w3[:, HY_WIDTH:]])
    bands = jnp.asarray(np.linspace(1e-4, HY_BANDS - 1, HY_BANDS).astype(np.float32).reshape(HY_BANDS, 1))
    args = (w1t, col(b1), w2.T, col(b2), col(freq), w3s, decay.reshape(2, 1, HY_WIDTH), bands)
    full = lambda a: pl.BlockSpec(a.shape, lambda i: (0,) * a.ndim)
    h, ss = pl.pallas_call(
        kern,
        out_shape=(jax.ShapeDtypeStruct((2, l_out, HY_WIDTH), F32), jax.ShapeDtypeStruct((8, HY_WIDTH), F32)),
        grid=(L // tm,),
        in_specs=[full(a) for a in args],
        out_specs=(pl.BlockSpec((2, tm_out, HY_WIDTH), lambda i: (0, i, 0)),
                   pl.BlockSpec((8, HY_WIDTH), lambda i: (0, 0))),
        compiler_params=_params(("arbitrary",), VMEM_SMALL_MIB),
        name="hyena_filter_%d" % L,
    )(*args)
    return h.reshape(2 * l_out, HY_WIDTH), ss


def _stack_bf16(re, im):
    return jnp.concatenate([re, im], axis=0).astype(BF16)


def _filter_fft_kernel(h_ref, ss_ref, m1_ref, f2_ref, o_ref, a_scr):
    R, P = FFT_R, FFT_PITCH
    inv_norm = lax.rsqrt(ss_ref[0:1, :])

    def stage1(n2, carry):
        neg = jnp.where(n2 == 0, 0, R - n2)
        xs = jnp.concatenate([h_ref[pl.ds(n2, R // 2, stride=P), :],
                              h_ref[pl.ds(SEQ_P + neg, R // 2, stride=P), :]], axis=0)
        xs = (xs * inv_norm).astype(BF16)
        a = jnp.dot(m1_ref[n2], xs, preferred_element_type=F32)
        base = pl.multiple_of(n2 * P, 8)
        a_scr[0, pl.ds(base, R), :] = a[:R]
        a_scr[1, pl.ds(base, R), :] = a[R:]
        return carry

    lax.fori_loop(0, R, stage1, 0, unroll=FFT_UNROLL)

    ct = a_scr.shape[-1]

    def stage2(pair, carry):
        k1s = (2 * pair, 2 * pair + 1)
        rhs = jnp.concatenate([_stack_bf16(a_scr[0, pl.ds(k1, R, stride=P), :],
                                           a_scr[1, pl.ds(k1, R, stride=P), :]) for k1 in k1s], axis=1)
        x = jnp.dot(f2_ref[...], rhs, preferred_element_type=F32)
        for d, k1 in enumerate(k1s):
            base = pl.multiple_of(k1 * R, R)
            o_ref[0, pl.ds(base, R), :] = x[:R, d * ct:(d + 1) * ct].astype(BF16)
            o_ref[1, pl.ds(base, R), :] = x[R:, d * ct:(d + 1) * ct].astype(BF16)
        return carry

    lax.fori_loop(0, R // 2, stage2, 0, unroll=FFT_UNROLL // 2)


def _filter_fft(h_circ, ss, m1, f2):
    ct = LANES
    once = pl.Buffered(1)
    return pl.pallas_call(
        _filter_fft_kernel,
        out_shape=jax.ShapeDtypeStruct((2, FFT_N, HY_WIDTH), BF16),
        grid=(HY_WIDTH // ct,),
        in_specs=[
            pl.BlockSpec((2 * SEQ_P, ct), lambda c: (0, c), pipeline_mode=once),
            pl.BlockSpec((8, ct), lambda c: (0, c)),
            pl.BlockSpec(m1.shape, lambda c: (0, 0, 0), pipeline_mode=once),
            pl.BlockSpec(f2.shape, lambda c: (0, 0), pipeline_mode=once),
        ],
        out_specs=pl.BlockSpec((2, FFT_N, ct), lambda c: (0, 0, c)),
        scratch_shapes=[pltpu.VMEM((2, FFT_R * FFT_PITCH, ct), F32)],
        compiler_params=_params(("parallel",), VMEM_FFT_MIB),
        name="hyena_filter_fft",
    )(h_circ, ss, m1, f2)


def _fft_conv_kernel(z_ref, hs_ref, m1_ref, f2_ref, g2_ref, m1i_ref, y_ref, a_scr):
    R, P = FFT_R, FFT_PITCH
    ct = a_scr.shape[-1]

    def stage1(n2, carry):
        rows = pl.ds(n2, R // 2, stride=P)
        xs = jnp.concatenate([z_ref[0, rows, :], z_ref[1, rows, :]], axis=1).astype(BF16)
        t = jnp.dot(m1_ref[n2, :, 0:R // 2], xs, preferred_element_type=F32)
        base = pl.multiple_of(n2 * P, 8)
        a_scr[0, pl.ds(base, R), :] = t[:R, :ct] - t[R:, ct:]
        a_scr[1, pl.ds(base, R), :] = t[R:, :ct] + t[:R, ct:]
        return carry

    lax.fori_loop(0, R, stage1, 0, unroll=FFT_UNROLL)

    def stage2(pair, carry):
        k1s = (2 * pair, 2 * pair + 1)
        rhs = jnp.concatenate([_stack_bf16(a_scr[0, pl.ds(k1, R, stride=P), :],
                                           a_scr[1, pl.ds(k1, R, stride=P), :]) for k1 in k1s], axis=1)
        x = jnp.dot(f2_ref[...], rhs, preferred_element_type=F32)
        prods = []
        for d, k1 in enumerate(k1s):
            base = pl.multiple_of(k1 * R, R)
            hr = hs_ref[0, pl.ds(base, R), :].astype(F32)
            hi = hs_ref[1, pl.ds(base, R), :].astype(F32)
            xr, xi = x[:R, d * ct:(d + 1) * ct], x[R:, d * ct:(d + 1) * ct]
            prods.append(_stack_bf16(xr * hr - xi * hi, xr * hi + xi * hr))
        bq = jnp.dot(g2_ref[...], jnp.concatenate(prods, axis=1), preferred_element_type=F32)
        for d, k1 in enumerate(k1s):
            a_scr[0, pl.ds(k1, R, stride=P), :] = bq[:R, d * ct:(d + 1) * ct]
            a_scr[1, pl.ds(k1, R, stride=P), :] = bq[R:, d * ct:(d + 1) * ct]
        return carry

    lax.fori_loop(0, R // 2, stage2, 0, unroll=FFT_UNROLL // 2)

    def stage3(n2, carry):
        base = pl.multiple_of(n2 * P, 8)
        br = a_scr[0, pl.ds(base, R), :]
        bi = a_scr[1, pl.ds(base, R), :]
        rhs = jnp.concatenate([jnp.concatenate([br, bi], axis=1),
                               jnp.concatenate([bi, -br], axis=1)], axis=0).astype(BF16)
        y = jnp.dot(m1i_ref[n2], rhs, preferred_element_type=F32)
        rows = pl.ds(n2, R // 2, stride=P)
        y_ref[0, rows, :] = y[:, :ct]
        y_ref[1, rows, :] = y[:, ct:]
        return carry

    lax.fori_loop(0, R, stage3, 0, unroll=FFT_UNROLL)
    for b in range(2):
        for n1 in range(R // 2):
            y_ref[b, n1 * P + R:(n1 + 1) * P, :] = jnp.zeros((P - R, ct), F32)


def _fft_conv(z, hspec, m1, f2, g2, m1i):
    assert z.shape[0] == 2, "the batch pair is packed into one complex signal"
    ct = LANES
    once = pl.Buffered(1)
    return pl.pallas_call(
        _fft_conv_kernel,
        out_shape=jax.ShapeDtypeStruct((2, SEQ_P, HY_WIDTH), F32),
        grid=(HY_WIDTH // ct,),
        in_specs=[
            pl.BlockSpec((2, SEQ_P, ct), lambda c: (0, 0, c), pipeline_mode=once),
            pl.BlockSpec((2, FFT_N, ct), lambda c: (0, 0, c), pipeline_mode=once),
            pl.BlockSpec(m1.shape, lambda c: (0, 0, 0), pipeline_mode=once),
            pl.BlockSpec(f2.shape, lambda c: (0, 0), pipeline_mode=once),
            pl.BlockSpec(g2.shape, lambda c: (0, 0), pipeline_mode=once),
            pl.BlockSpec(m1i.shape, lambda c: (0, 0, 0), pipeline_mode=once),
        ],
        out_specs=pl.BlockSpec((2, SEQ_P, ct), lambda c: (0, 0, c), pipeline_mode=once),
        scratch_shapes=[pltpu.VMEM((2, FFT_R * FFT_PITCH, ct), F32)],
        compiler_params=_params(("arbitrary",), VMEM_FFT_MIB),
        name="hyena_fft_conv",
    )(z, hspec, m1, f2, g2, m1i)


def _ctx_conv_kernel(z_ref, h_ref, ss_ref, fwd_ref, inv_ref, fwd_h_ref, y_ref):
    N = 2 * CTX_LEN
    hn = (h_ref[...] * lax.rsqrt(ss_ref[0:1, :])).astype(BF16)
    hs = jnp.dot(fwd_h_ref[...], hn, preferred_element_type=F32)
    zs = jnp.dot(fwd_ref[:, :CTX_LEN], z_ref[...].astype(BF16), preferred_element_type=F32)
    hr, hi, zr, zi = hs[:N], hs[N:], zs[:N], zs[N:]
    ys = _stack_bf16(zr * hr - zi * hi, zr * hi + zi * hr)
    y_ref[...] = jnp.dot(inv_ref[...], ys, preferred_element_type=F32)


def _ctx_conv(z, h_circ, ss, fwd, inv, fwd_h):
    G = z.shape[0]
    full = lambda a: pl.BlockSpec(a.shape, lambda b: (0,) * a.ndim)
    return pl.pallas_call(
        _ctx_conv_kernel,
        out_shape=jax.ShapeDtypeStruct((G, CTX_LEN, HY_WIDTH), F32),
        grid=(G,),
        in_specs=[pl.BlockSpec((None, CTX_LEN, HY_WIDTH), lambda b: (b, 0, 0)),
                  full(h_circ), full(ss), full(fwd), full(inv), full(fwd_h)],
        out_specs=pl.BlockSpec((None, CTX_LEN, HY_WIDTH), lambda b: (b, 0, 0)),
        compiler_params=_params(("parallel",), VMEM_SMALL_MIB),
        name="hyena_ctx_conv",
    )(z, h_circ, ss, fwd, inv, fwd_h)


def _kv_group(refs, g, col0):
    cols = slice(col0 + g * LANES, col0 + (g + 1) * LANES)
    return refs[0][:, cols] if len(refs) == 1 else jnp.concatenate([r[:, cols] for r in refs], axis=0)


def _attn_logits(q_ref, k_refs, g, masks, col0=0):
    low = lax.broadcasted_iota(jnp.int32, (BLOCK, LANES), 1) < HEAD_DIM
    zero = jnp.zeros((), BF16)
    parts = []
    for hh in range(ATT_GROUP):
        h = g * ATT_GROUP + hh
        tile = q_ref[:, (h // 2) * LANES:(h // 2 + 1) * LANES]
        parts.append(jnp.where(low if h % 2 == 0 else ~low, tile, zero))
    s = lax.dot_general(jnp.concatenate(parts, axis=0), _kv_group(k_refs, g, col0), (((1,), (1,)), ((), ())),
                        preferred_element_type=F32)
    if masks is not None:
        ok_prev, ok_next = masks
        s = jnp.concatenate([jnp.where(ok_prev, s[:, :BLOCK], NEG_INF), s[:, BLOCK:2 * BLOCK],
                             jnp.where(ok_next, s[:, 2 * BLOCK:3 * BLOCK], NEG_INF), s[:, 3 * BLOCK:]], axis=1)
    return s


def _attn_output(s, sink_ref, v_refs, g, o_ref, col0=0):
    low = lax.broadcasted_iota(jnp.int32, (BLOCK, LANES), 1) < HEAD_DIM
    low4 = jnp.concatenate([low] * ATT_GROUP, axis=0)
    v = _kv_group(v_refs, g, col0)
    v_aug = jnp.where(lax.broadcasted_iota(jnp.int32, v.shape, 1) < HEAD_DIM, v, jnp.ones((), BF16))
    sink = jnp.concatenate(
        [jnp.full((BLOCK, 1), sink_ref[g * ATT_GROUP + hh] * LOG2E, F32) for hh in range(ATT_GROUP)], axis=0)
    m = jnp.maximum(jnp.max(s, axis=-1, keepdims=True), sink)
    e = jnp.exp2(s - m).astype(BF16)
    o = jnp.dot(e, v_aug, preferred_element_type=F32) + jnp.where(low4, 0.0, jnp.exp2(sink - m))
    swapped = pltpu.roll(o, HEAD_DIM, axis=1)
    for pair in range(ATT_GROUP // 2):
        ev = slice((2 * pair) * BLOCK, (2 * pair + 1) * BLOCK)
        od = slice((2 * pair + 1) * BLOCK, (2 * pair + 2) * BLOCK)
        even = o[ev] / swapped[ev]
        odd = swapped[od] / o[od]
        t = g * (ATT_GROUP // 2) + pair
        o_ref[:, t * LANES:(t + 1) * LANES] = jnp.where(low, even, odd).astype(o_ref.dtype)


def _attn_ctx_kernel(sink_ref, q_ref, kx_ref, vx_ref, o_ref):
    for g in range(ATT_KV_HEADS):
        _attn_output(_attn_logits(q_ref, (kx_ref,), g, None), sink_ref, (vx_ref,), g, o_ref)


def _attn_local_kernel(sink_ref, q_ref, kv_m2, kv_m1, kv_0, kv_p1, kv_ctx, o_ref, s_even, s_odd, *, n_blocks):
    n = pl.program_id(1)
    k_refs, v_refs, v_col = (kv_m1, kv_0, kv_p1, kv_ctx), (kv_m2, kv_m1, kv_0, kv_ctx), 2 * LANES

    @pl.when(n == 0)
    def _():
        s_odd[...] = jnp.zeros_like(s_odd)

    def step(s_new, s_prev):
        nq = jnp.minimum(n, n_blocks - 1)
        qi = lax.broadcasted_iota(jnp.int32, (ATT_GROUP * BLOCK, BLOCK), 0) % BLOCK
        ki = lax.broadcasted_iota(jnp.int32, (ATT_GROUP * BLOCK, BLOCK), 1)
        masks = ((ki >= qi) & (nq > 0), (ki <= qi) & (nq < n_blocks - 1))
        for g in range(ATT_KV_HEADS):
            s_new[g] = _attn_logits(q_ref, k_refs, g, masks)
        for g in range(ATT_KV_HEADS):
            _attn_output(s_prev[g], sink_ref, v_refs, g, o_ref, v_col)

    @pl.when(n % 2 == 0)
    def _():
        step(s_even, s_odd)

    @pl.when(n % 2 == 1)
    def _():
        step(s_odd, s_even)


def _attention(sink, qkv, qkv_ctx, *, local):
    src = qkv if local else qkv_ctx
    G, R, _ = src.shape
    nb = R // BLOCK
    kcol, vcol = ATT_WIDTH // 256, ATT_WIDTH // 256 + 1
    ctx_k = pl.BlockSpec((None, CTX_LEN, 256), lambda b, n: (b, 0, kcol))
    ctx_v = pl.BlockSpec((None, CTX_LEN, 256), lambda b, n: (b, 0, vcol))
    smem = pl.BlockSpec(memory_space=pltpu.SMEM)
    out_shape = jax.ShapeDtypeStruct((G, R, ATT_WIDTH), BF16)
    if not local:
        rows = pl.BlockSpec((None, BLOCK, ATT_WIDTH), lambda b, n: (b, n, 0))
        return pl.pallas_call(
            _attn_ctx_kernel, out_shape=out_shape, grid=(G, nb),
            in_specs=[smem, rows, ctx_k, ctx_v], out_specs=rows,
            compiler_params=_params(("parallel", "parallel"), VMEM_SMALL_MIB), name="attn_ctx",
        )(sink, qkv_ctx, qkv_ctx, qkv_ctx)

    q_blk = lambda n: jnp.minimum(n, nb - 1)
    o_blk = lambda n: jnp.maximum(n - 1, 0)

    kv_w = EV_QKV - ATT_WIDTH
    kv_col = ATT_WIDTH // kv_w

    def kv_rows(off):
        return pl.BlockSpec((None, BLOCK, kv_w), lambda b, n: (b, jnp.clip(n + off, 0, nb - 1), kv_col))

    in_specs = [smem, pl.BlockSpec((None, BLOCK, ATT_WIDTH), lambda b, n: (b, q_blk(n), 0)),
                kv_rows(-2), kv_rows(-1), kv_rows(0), kv_rows(1),
                pl.BlockSpec((None, CTX_LEN, kv_w), lambda b, n: (b, 0, kv_col))]
    logits_scratch = pltpu.VMEM((ATT_KV_HEADS, ATT_GROUP * BLOCK, 3 * BLOCK + CTX_LEN), F32)
    return pl.pallas_call(
        functools.partial(_attn_local_kernel, n_blocks=nb),
        out_shape=out_shape,
        grid=(G, nb + 1),
        in_specs=in_specs,
        out_specs=pl.BlockSpec((None, BLOCK, ATT_WIDTH), lambda b, n: (b, o_blk(n), 0)),
        scratch_shapes=[logits_scratch, logits_scratch],
        compiler_params=_params(("parallel", "arbitrary"), VMEM_SMALL_MIB),
        name="attn_local",
    )(sink, qkv, qkv, qkv, qkv, qkv, qkv_ctx)


def _mixer_out0(rows, x0_ref, y_ref, zb_ref, att_ref, wo_ref):
    if y_ref.shape[0] != x0_ref.shape[0]:
        y = jnp.concatenate([y_ref[r // FFT_R * FFT_PITCH:r // FFT_R * FFT_PITCH + FFT_R, :]
                             for r in range(rows.start, rows.stop, FFT_R)], axis=0)
    else:
        y = y_ref[rows, :]
    hy = (x0_ref[rows, :] * y + zb_ref[rows, :]).astype(BF16)
    return (jnp.dot(hy, wo_ref[:HY_WIDTH, :], preferred_element_type=F32)
            + jnp.dot(att_ref[rows, :], wo_ref[HY_WIDTH:, :], preferred_element_type=F32))


def _mixer_out1(rows, of_ref, ob_ref, wo_ref):
    a = (of_ref[rows, :].astype(F32) + ob_ref[rows, :].astype(F32)).astype(BF16)
    return jnp.dot(a, wo_ref[...], preferred_element_type=F32)


def _mix_kernel(*refs, mixer_out, n_mix, row, fc, final_norm):
    x_ref = refs[0]
    mix_refs = refs[1:2 + n_mix]
    gta_ref, gm_ref, shm_ref, scm_ref, gtm_ref, w1_ref, w2_ref, fg_ref, o_ref, a_scr = refs[2 + n_mix:]
    r = pl.program_id(0) if row is None else row
    gta, gtm = _mod_row(gta_ref, r), _mod_row(gtm_ref, r)
    gm, shm, scm = gm_ref[...], _mod_row(shm_ref, r), _mod_row(scm_ref, r)
    sub = min(x_ref.shape[0], SUB_ROWS)
    for s in range(x_ref.shape[0] // sub):
        rows = slice(s * sub, (s + 1) * sub)
        x1 = x_ref[rows, :] + gta * mixer_out(rows, *mix_refs)
        h = _norm_mod(x1, gm, shm, scm).astype(BF16)
        for c in range(D_FF // fc):
            a = jnp.maximum(jnp.dot(h, w1_ref[:, c * fc:(c + 1) * fc], preferred_element_type=F32), 0.0)
            a_scr[rows, c * fc:(c + 1) * fc] = (a * a).astype(BF16)
        out = x1 + gtm * jnp.dot(a_scr[rows, :], w2_ref[...], preferred_element_type=F32)
        if final_norm:
            out = (out * lax.rsqrt(jnp.mean(out * out, axis=-1, keepdims=True) + EPS)) * fg_ref[...]
        o_ref[rows, :] = out


def _mix_mlp(kind, x, mix_in, wo, mod, gm, w1, w2, fg, layer, *, is_ctx, tm, fc, final_norm, cast=()):
    G, R, _ = x.shape
    row_spec = lambda w: pl.BlockSpec((None, tm, w), lambda b, i: (b, i, 0))
    modk = lambda k: pl.BlockSpec((None, 8, D_MODEL), lambda b, i: (layer, 0, k))
    vec = pl.BlockSpec((1, D_MODEL), lambda b, i: (0, 0))
    resident = lambda a: pl.BlockSpec(a.shape, lambda b, i: (0, 0), pipeline_mode=pl.Buffered(1))
    if kind == 0:
        mixer_out = _mixer_out0
        y_rows = tm if mix_in[1].shape[1] == R else tm // FFT_R * FFT_PITCH
        y_spec = pl.BlockSpec((None, y_rows, HY_WIDTH), lambda b, i: (b, i, 0))
        mix_specs = [row_spec(HY_WIDTH), y_spec, row_spec(HY_WIDTH), row_spec(ATT_WIDTH)]
    else:
        mixer_out = _mixer_out1
        mix_specs = [row_spec(RET_V)] * 2
    kern = functools.partial(_mix_kernel, mixer_out=mixer_out, n_mix=len(mix_in), row=2 if is_ctx else None,
                             fc=fc, final_norm=final_norm)
    in_specs = [row_spec(D_MODEL)] + mix_specs + [
        resident(wo), modk(2), vec, modk(3), modk(4), modk(5), resident(w1), resident(w2), vec]
    c_in, c_out, c_shapes, c_args = _cast_jobs(cast, R // tm)
    if cast:
        assert G * (R // tm) >= CAST_CHUNKS
        kern = _with_casts(kern, len(in_specs), 1, len(cast))
    res = pl.pallas_call(
        kern,
        out_shape=[jax.ShapeDtypeStruct((G, R, D_MODEL), F32)] + c_shapes,
        grid=(G, R // tm),
        in_specs=in_specs + c_in,
        out_specs=[row_spec(D_MODEL)] + c_out,
        scratch_shapes=[pltpu.VMEM((tm, D_FF), BF16)],
        compiler_params=_params(("arbitrary", "arbitrary") if cast else ("parallel", "parallel"), VMEM_DENSE_MIB),
        name="mix_mlp%d%s" % (kind, "_ctx" if is_ctx else ""),
    )(x, *mix_in, wo, mod, gm, mod, mod, mod, w1, w2, fg, *c_args)
    return res[0], res[1:]


def _ret_kernel(lr_ref, qkv_c, qkv_f, g_f, qkv_b, g_b, cos_ref, sin_ref, of_ref, ob_ref,
                state, dmask, xi, zeta, gch):
    j = pl.program_id(1)
    C = RET_C
    kscale = RET_DK ** -0.5

    @pl.when(j == 0)
    def _():
        state[...] = jnp.zeros_like(state)
        row = lax.broadcasted_iota(jnp.int32, (C, C), 0).astype(F32)
        col = lax.broadcasted_iota(jnp.int32, (C, C), 1).astype(F32)
        for d in range(2):
            e = row - col if d == 0 else col - row
            p = row if d == 0 else (C - 1) - row
            for h in range(RET_HEADS):
                lg = -jnp.exp(jnp.full((C, C), lr_ref[d, h], F32))
                dmask[d, h] = jnp.where(e >= 0, jnp.exp(e * lg), 0.0) * kscale
                xi[d, h] = jnp.exp((p + 1.0) * lg)
                zeta[d, h] = jnp.exp(((C - 1) - p) * lg) * kscale
                gch[d, h] = jnp.exp(C * -jnp.exp(jnp.full((8, C), lr_ref[d, h], F32)))

    half = RET_DK // 2

    def chain(d, h, src_ref, g_ref, o_ref):
        cos, sin = cos_ref[d], sin_ref[d]

        def rot(c0):
            t1 = src_ref[:, c0:c0 + half].astype(F32)
            t2 = src_ref[:, c0 + half:c0 + RET_DK].astype(F32)
            return jnp.concatenate([t1 * cos - t2 * sin, t2 * cos + t1 * sin], axis=1)

        q = rot(h * RET_DK)
        k = rot(RET_QK + h * RET_DK)
        v = src_ref[:, 2 * RET_QK + h * RET_DV:2 * RET_QK + (h + 1) * RET_DV]
        inner = lax.dot_general(q.astype(BF16), k.astype(BF16), (((1,), (1,)), ((), ())),
                                preferred_element_type=F32) * dmask[d, h]
        s_old = state[d, h]
        kv = lax.dot_general((k * zeta[d, h]).astype(BF16), v, (((0,), (0,)), ((), ())),
                             preferred_element_type=F32)
        state[d, h] = gch[d, h, 0:1, 0:1] * s_old + kv
        if o_ref is not None:
            o = (jnp.dot(inner.astype(BF16), v, preferred_element_type=F32)
                 + jnp.dot((q * xi[d, h]).astype(BF16), s_old.astype(BF16), preferred_element_type=F32))
            on = o * lax.rsqrt(jnp.mean(o * o, axis=-1, keepdims=True) + EPS)
            gate = g_ref[:, h * RET_DV:(h + 1) * RET_DV]
            half_gate = gate * 0.5
            silu = half_gate + half_gate * jnp.tanh(half_gate)
            o_ref[:, h * RET_DV:(h + 1) * RET_DV] = (silu.astype(F32) * on).astype(o_ref.dtype)

    @pl.when(j == 0)
    def _():
        for d in range(2):
            for h in range(RET_HEADS):
                chain(d, h, qkv_c, None, None)

    @pl.when(j > 0)
    def _():
        for h in range(RET_HEADS):
            chain(0, h, qkv_f, g_f, of_ref)
            chain(1, h, qkv_b, g_b, ob_ref)


def _retention(log_rate, p_ctx, p_lat):
    G = p_lat.shape[0]
    C = RET_C
    n_lat = SEQ // C
    cos, sin = _rope1d_tables()
    cos, sin = jnp.asarray(cos), jnp.asarray(sin)

    def fwd_chunk(j):
        return jnp.maximum(j - 1, 0)

    def bwd_chunk(j):
        return n_lat - 1 - jnp.maximum(j - 1, 0)

    qkv_w = 2 * RET_QK + RET_V
    gcol = qkv_w // RET_V
    tab = pl.BlockSpec((2, C, RET_DK // 2), lambda b, j: (0, j, 0))
    in_specs = [
        pl.BlockSpec(memory_space=pltpu.SMEM),
        pl.BlockSpec((None, C, qkv_w), lambda b, j: (b, 0, 0)),
        pl.BlockSpec((None, C, qkv_w), lambda b, j: (b, fwd_chunk(j), 0)),
        pl.BlockSpec((None, C, RET_V), lambda b, j: (b, fwd_chunk(j), gcol)),
        pl.BlockSpec((None, C, qkv_w), lambda b, j: (b, bwd_chunk(j), 0)),
        pl.BlockSpec((None, C, RET_V), lambda b, j: (b, bwd_chunk(j), gcol + 1)),
        tab, tab,
    ]
    out = jax.ShapeDtypeStruct((G, SEQ, RET_V), BF16)
    per_chain = lambda *tail: pltpu.VMEM((2, RET_HEADS) + tail, F32)
    return pl.pallas_call(
        _ret_kernel,
        out_shape=(out, out),
        grid=(G, 1 + n_lat),
        in_specs=in_specs,
        out_specs=(pl.BlockSpec((None, C, RET_V), lambda b, j: (b, fwd_chunk(j), 0)),
                   pl.BlockSpec((None, C, RET_V), lambda b, j: (b, bwd_chunk(j), 0))),
        scratch_shapes=[per_chain(RET_DK, RET_DV), per_chain(C, C), per_chain(C, C), per_chain(C, C),
                        per_chain(8, C)],
        compiler_params=_params(("parallel", "arbitrary"), VMEM_SMALL_MIB),
        name="retention",
    )(log_rate, p_ctx, p_lat, p_lat, p_lat, p_lat, cos, sin)


def _ev_weight(w_in):
    i_q = EV_U
    i_k = i_q + ATT_WIDTH
    i_v = i_k + KV_WIDTH
    cols = [w_in[:, :i_q], w_in[:, i_q:i_k] * (HEAD_DIM ** -0.5 * LOG2E)]
    for base in (i_k, i_v):
        for g in range(ATT_KV_HEADS):
            part = w_in[:, base + g * HEAD_DIM: base + (g + 1) * HEAD_DIM]
            cols += [part, part]
    return jnp.concatenate([c.astype(BF16) for c in cols], axis=1)


def kernel(x, c, ctx, c_ctx, ada_w, ada_b, norm_mix_g, norm_mlp_g, mlp_w1, mlp_w2, ev_w_in, ev_w_out, hy_conv_w, hy_conv_b, hy_w1, hy_b1, hy_w2, hy_b2, hy_w3, hy_freq, hy_decay, hy_bias, attn_sink, od_w_in, od_w_out, ret_log_rate, final_g):
    D = D_MODEL
    cvec = jnp.concatenate([c, c_ctx[None, :], jnp.zeros((8 - BATCH - 1, D), F32)], axis=0)
    mod = _ada(cvec, ada_w, ada_b)

    m1, m1i, f2, g2 = (jnp.asarray(a).astype(BF16) for a in _fft_mats())
    cfwd, cinv, cfwd_h = (jnp.asarray(a).astype(BF16) for a in _ctx_fft_mats())
    fg = final_g.reshape(1, D)

    gmix = norm_mix_g[0].reshape(1, D)
    gmlp = norm_mlp_g[0].reshape(1, D)
    w_in = _ev_weight(ev_w_in[0])
    conv = (hy_conv_w[0], hy_conv_b[0].reshape(1, EV_U), hy_bias[0].reshape(1, HY_WIDTH))
    (z_l, x0_l, zb_l, qkv_l), (wo, w1, w2) = _inproj0(
        x, gmix, mod, w_in, *conv, 0, is_ctx=False, tm=512, cast=((ev_w_out, 0), (mlp_w1, 0), (mlp_w2, 0)))
    (z_c, x0_c, zb_c, qkv_c), _ = _inproj0(ctx, gmix, mod, w_in, *conv, 0, is_ctx=True, tm=CTX_LEN)

    filt = (hy_w1[0], hy_b1[0], hy_w2[0], hy_b2[0], hy_w3[0], hy_freq[0], hy_decay[0])
    h_l, ss_l = _hyena_filter(*filt, L=SEQ)
    h_c, ss_c = _hyena_filter(*filt, L=CTX_LEN)
    hspec = _filter_fft(h_l, ss_l, m1, f2)
    y_l = _fft_conv(z_l, hspec, m1, f2, g2, m1i)
    y_c = _ctx_conv(z_c, h_c, ss_c, cfwd, cinv, cfwd_h)

    sink = attn_sink[0]
    att_l = _attention(sink, qkv_l, qkv_c, local=True)
    att_c = _attention(sink, qkv_l, qkv_c, local=False)

    x1, (w_in,) = _mix_mlp(0, x, (x0_l, y_l, zb_l, att_l), wo, mod, gmlp, w1, w2, fg, 0,
                           is_ctx=False, tm=512, fc=512, final_norm=False, cast=((od_w_in, 0),))
    ctx1, _ = _mix_mlp(0, ctx, (x0_c, y_c, zb_c, att_c), wo, mod, gmlp, w1, w2, fg, 0,
                       is_ctx=True, tm=CTX_LEN, fc=512, final_norm=False)

    gmix = norm_mix_g[1].reshape(1, D)
    gmlp = norm_mlp_g[1].reshape(1, D)
    p_l, (wo, w1, w2) = _inproj1(x1, gmix, mod, w_in, 1, is_ctx=False, tm=512, tn=512,
                                 cast=((od_w_out, 0), (mlp_w1, 1), (mlp_w2, 1)))
    p_c, _ = _inproj1(ctx1, gmix, mod, w_in, 1, is_ctx=True, tm=CTX_LEN, tn=512)
    o_fwd, o_bwd = _retention(ret_log_rate[0], p_c, p_l)

    out, _ = _mix_mlp(1, x1, (o_fwd, o_bwd), wo, mod, gmlp, w1, w2, fg, 1,
                      is_ctx=False, tm=512, fc=512, final_norm=True)
    return out
```

```python
import functools
import math

import numpy as np
import jax
import jax.numpy as jnp
from jax import lax
from jax.experimental import pallas as pl
from jax.experimental.pallas import tpu as pltpu

F32 = jnp.float32
BF16 = jnp.bfloat16
HIGHEST = lax.Precision.HIGHEST

D_MODEL = 1024
BATCH = 2
SEQ = 8192
DEPTH = 2
GRID_W = 64
CTX_LEN = 256
EPS = 1e-6
NEG_INF = -1e30
N_MOD = 6
D_FF = 4 * D_MODEL
ROPE_BASE = 10000.0

HY_WIDTH = D_MODEL // 2
HY_EMB = 33
HY_BANDS = (HY_EMB - 1) // 2
HY_HIDDEN = 64

ATT_HEADS = 8
ATT_KV_HEADS = 2
ATT_GROUP = ATT_HEADS // ATT_KV_HEADS
HEAD_DIM = 64
ATT_WIDTH = ATT_HEADS * HEAD_DIM
KV_WIDTH = ATT_KV_HEADS * HEAD_DIM
BLOCK = 128

RET_HEADS = 4
RET_DK = D_MODEL // RET_HEADS
RET_DV = 2 * RET_DK
RET_QK = RET_HEADS * RET_DK
RET_V = RET_HEADS * RET_DV
OD_IN = 2 * RET_QK + 3 * RET_V

LOG2E = 1.4426950408889634
LANES = 128
MIB = 1024 * 1024

EV_U = 3 * HY_WIDTH
EV_QKV = ATT_WIDTH + 4 * LANES
EV_COLS = EV_U + EV_QKV

FFT_N = 2 * SEQ
FFT_R = 128
FFT_PITCH = FFT_R + 8
SEQ_P = SEQ // FFT_R * FFT_PITCH
CONV_ROWS = 128
CONV_PAD = 8
SUB_ROWS = 256
MIX_SUB_ROWS = 512
FFT_UNROLL = 64
RET_C = 256
CAST_CHUNKS = 32


VMEM_SMALL_MIB = 48
VMEM_DENSE_MIB = 56
VMEM_FFT_MIB = 58


def _params(sem, vmem_mib):
    return pltpu.CompilerParams(dimension_semantics=sem, vmem_limit_bytes=vmem_mib * MIB)


@functools.lru_cache(maxsize=None)
def _rope2d_tables():
    quarter = HEAD_DIM // 4
    inv = ROPE_BASE ** (-np.arange(quarter, dtype=np.float64) / quarter)
    t = np.arange(SEQ)
    pos = np.stack([t // GRID_W, t % GRID_W], axis=1).astype(np.float64)
    lane = np.arange(HEAD_DIM)
    half = lane // (HEAD_DIM // 2)
    e = lane % (HEAD_DIM // 2)
    ang = pos[:, half] * inv[e % quarter][None, :]
    sign = np.where(e < quarter, -1.0, 1.0)[None, :]
    cos = np.tile(np.cos(ang), (1, 2)).astype(np.float32)
    sin = np.tile(np.sin(ang) * sign, (1, 2)).astype(np.float32)
    return cos, sin


@functools.lru_cache(maxsize=None)
def _rope1d_tables():
    n = RET_DK // 2
    inv = ROPE_BASE ** (-np.linspace(0.0, 1.0, n))
    pos = np.arange(CTX_LEN + SEQ, dtype=np.float64)
    ang = pos[:, None] * inv[None, :]
    cos, sin = np.cos(ang), np.sin(ang)

    def rev(a):
        return a.reshape(-1, RET_C, n)[:, ::-1].reshape(-1, n)

    cos2 = np.stack([cos, rev(cos)]).astype(np.float32)
    sin2 = np.stack([sin, rev(sin)]).astype(np.float32)
    return cos2, sin2


@functools.lru_cache(maxsize=None)
def _fft_mats():
    N, R = FFT_N, FFT_R
    k1 = np.arange(R)
    n2 = np.arange(R)[:, None, None]
    n1 = np.arange(R // 2)[None, None, :]
    n = np.concatenate([n2 + R * n1, (N - ((R - n2) % R + R * n1)) % N], axis=2)
    idx = (k1[None, :, None] * n) % N
    ang = 2.0 * np.pi * idx / N
    c, s = np.cos(ang), np.sin(ang)
    m1 = np.concatenate([c, -s], axis=1).astype(np.float32)
    m1i = np.concatenate([c.transpose(0, 2, 1), -s.transpose(0, 2, 1)], axis=2)
    m1i = m1i[:, :R // 2].astype(np.float32)
    a2 = 2.0 * np.pi * ((np.arange(R)[:, None] * np.arange(R)[None, :]) % R) / R
    fr, fi = np.cos(a2), -np.sin(a2)
    f2 = np.block([[fr, -fi], [fi, fr]]).astype(np.float32)
    g2 = (np.block([[fr, fi], [-fi, fr]]) / N).astype(np.float32)
    return m1, m1i, f2, g2


@functools.lru_cache(maxsize=None)
def _ctx_fft_mats():
    N = 2 * CTX_LEN
    k = np.arange(N)[:, None]
    n = np.arange(N)[None, :]
    ang = 2.0 * np.pi * ((k * n) % N) / N
    c, s = np.cos(ang), np.sin(ang)
    fwd = np.concatenate([c, -s], axis=0).astype(np.float32)
    inv = (np.concatenate([c, -s], axis=1)[:CTX_LEN] / N).astype(np.float32)
    src = np.concatenate([np.arange(CTX_LEN), (N - np.arange(CTX_LEN)) % N])
    return fwd, inv, fwd[:, src]


def _ada_kernel(c_ref, w_ref, b_ref, o_ref):
    c = c_ref[...]
    a = c * (1.0 / (1.0 + jnp.exp(-c)))
    o_ref[...] = _dot3(a, w_ref[...]) + b_ref[...]


def _ada(cvec, ada_w, ada_b):
    tn = 1536
    return pl.pallas_call(
        _ada_kernel,
        out_shape=jax.ShapeDtypeStruct((DEPTH, 8, N_MOD * D_MODEL), F32),
        grid=(DEPTH, N_MOD * D_MODEL // tn),
        in_specs=[
            pl.BlockSpec((8, D_MODEL), lambda i, j: (0, 0)),
            pl.BlockSpec((None, D_MODEL, tn), lambda i, j: (i, 0, j)),
            pl.BlockSpec((None, 1, tn), lambda i, j: (i, 0, j)),
        ],
        out_specs=pl.BlockSpec((None, 8, tn), lambda i, j: (i, 0, j)),
        compiler_params=_params(("parallel", "parallel"), VMEM_SMALL_MIB),
        name="ada_mod",
    )(cvec, ada_w, ada_b.reshape(DEPTH, 1, N_MOD * D_MODEL))


def _with_casts(body, n_in, n_out, n_cast):
    def kern(*refs):
        srcs = refs[n_in:n_in + n_cast]
        outs_at = n_in + n_cast
        dsts = refs[outs_at + n_out:outs_at + n_out + n_cast]
        for src, dst in zip(srcs, dsts):
            dst[...] = src[...].astype(dst.dtype)
        body(*refs[:n_in], *refs[outs_at:outs_at + n_out], *refs[outs_at + n_out + n_cast:])
    return kern


def _cast_jobs(stacked_weights, steps_per_row):
    in_specs, out_specs, out_shapes, args = [], [], [], []
    chunk = lambda b, i: jnp.minimum(b * steps_per_row + i, CAST_CHUNKS - 1)
    for w, lead in stacked_weights:
        _, rows, cols = w.shape
        blk = rows // CAST_CHUNKS
        in_specs.append(pl.BlockSpec((None, blk, cols), lambda b, i, lead=lead: (lead, chunk(b, i), 0)))
        out_specs.append(pl.BlockSpec((blk, cols), lambda b, i: (chunk(b, i), 0)))
        out_shapes.append(jax.ShapeDtypeStruct((rows, cols), BF16))
        args.append(w)
    return in_specs, out_specs, out_shapes, args


def _mod_row(ref, row):
    if isinstance(row, int):
        return ref[row:row + 1, :]
    return ref[pl.ds(row, 1), :]


def _norm_mod(x, g, shift, scale):
    y = x * lax.rsqrt(jnp.mean(x * x, axis=-1, keepdims=True) + EPS)
    return (y * g) * (1.0 + scale) + shift


def _rope_tile(x, cos, sin_signed):
    lane = lax.broadcasted_iota(jnp.int32, x.shape, 1)
    first = (lane % 32) < 16
    partner = jnp.where(first, pltpu.roll(x, LANES - 16, axis=1), pltpu.roll(x, 16, axis=1))
    return x * cos + partner * sin_signed


def _project0(x_ref, g, sh, sc, w_ref, cos_ref, sin_ref, u_ref, qkv_ref, rope):
    tn = 512
    n_rot = (ATT_WIDTH + 2 * LANES) // LANES
    sub = min(x_ref.shape[0], SUB_ROWS)
    for s in range(x_ref.shape[0] // sub):
        rows = slice(s * sub, (s + 1) * sub)
        urows = slice(CONV_PAD + s * sub, CONV_PAD + (s + 1) * sub)
        h = _norm_mod(x_ref[rows, :], g, sh, sc).astype(BF16)
        for j in range(EV_U // tn):
            u_ref[urows, j * tn:(j + 1) * tn] = jnp.dot(h, w_ref[:, j * tn:(j + 1) * tn],
                                                        preferred_element_type=F32)
        for j in range(EV_QKV // tn):
            y = jnp.dot(h, w_ref[:, EV_U + j * tn:EV_U + (j + 1) * tn], preferred_element_type=F32)
            for t in range(tn // LANES):
                yt = y[:, t * LANES:(t + 1) * LANES]
                if rope and j * (tn // LANES) + t < n_rot:
                    yt = _rope_tile(yt, cos_ref[rows, :], sin_ref[rows, :])
                qkv_ref[rows, j * tn + t * LANES:j * tn + (t + 1) * LANES] = yt.astype(BF16)


def _short_conv_gate(u_ref, cw_ref, cb_ref, bias_ref, z_ref, x0_ref, zb_ref):
    tm = u_ref.shape[0] - 2 * CONV_PAD
    rc = CONV_ROWS
    z_pitched = z_ref.shape[0] != tm

    def conv(r0, c0):
        cols = slice(c0, c0 + LANES)
        taps = [u_ref[CONV_PAD - 1 + k + r0:CONV_PAD - 1 + k + r0 + rc, cols] for k in range(3)]
        return (taps[0] * cw_ref[0:1, cols] + taps[1] * cw_ref[1:2, cols] + taps[2] * cw_ref[2:3, cols]
                + cb_ref[:, cols])

    for r0 in range(0, tm, rc):
        rows = slice(r0, r0 + rc)
        z0 = r0 // FFT_R * FFT_PITCH + r0 % FFT_R if z_pitched else r0
        for c0 in range(0, HY_WIDTH, LANES):
            cols = slice(c0, c0 + LANES)
            x0 = conv(r0, c0)
            z = conv(r0, 2 * HY_WIDTH + c0) * conv(r0, HY_WIDTH + c0)
            z_ref[z0:z0 + rc, cols] = z
            x0_ref[rows, cols] = x0
            zb_ref[rows, cols] = x0 * (z * bias_ref[:, cols])
    if z_pitched:
        for k in range(tm // FFT_R):
            z_ref[k * FFT_PITCH + FFT_R:(k + 1) * FFT_PITCH, :] = jnp.zeros((FFT_PITCH - FFT_R, HY_WIDTH), F32)


def _inproj0_kernel(x_ref, g_ref, sh_ref, sc_ref, w_ref, cos_ref, sin_ref, cw_ref, cb_ref, bias_ref,
                    z_ref, x0_ref, zb_ref, qkv_ref, u_even, u_odd, *, row, rope, n_tiles):
    r = pl.program_id(0) if row is None else row
    i = pl.program_id(1)
    g, sh, sc = g_ref[...], _mod_row(sh_ref, r), _mod_row(sc_ref, r)
    conv_refs = (cw_ref, cb_ref, bias_ref, z_ref, x0_ref, zb_ref)
    zero_row = jnp.zeros((1, EV_U), F32)
    tm = x_ref.shape[0]
    above, first, last, below = CONV_PAD - 1, CONV_PAD, CONV_PAD + tm - 1, CONV_PAD + tm

    if n_tiles == 1:
        u_even[above:first, :] = zero_row
        u_even[below:below + 1, :] = zero_row
        _project0(x_ref, g, sh, sc, w_ref, cos_ref, sin_ref, u_even, qkv_ref, rope)
        _short_conv_gate(u_even, *conv_refs)
        return

    @pl.when(i == 0)
    def _():
        u_even[...] = jnp.zeros_like(u_even)
        u_odd[...] = jnp.zeros_like(u_odd)

    def step(u_new, u_old):
        j = i - 1
        u_old[above:first, :] = jnp.where(j > 0, u_new[last:last + 1, :], zero_row)
        _project0(x_ref, g, sh, sc, w_ref, cos_ref, sin_ref, u_new, qkv_ref, rope)
        u_old[below:below + 1, :] = jnp.where(j < n_tiles - 1, u_new[first:first + 1, :], zero_row)
        _short_conv_gate(u_old, *conv_refs)

    @pl.when((i % 2 == 0) & (i < n_tiles))
    def _():
        step(u_even, u_odd)

    @pl.when((i % 2 == 1) & (i < n_tiles))
    def _():
        step(u_odd, u_even)

    @pl.when(i == n_tiles)
    def _():
        bufs = (u_even, u_odd)
        u_last, u_before = bufs[(n_tiles - 1) % 2], bufs[n_tiles % 2]
        u_last[above:first, :] = u_before[last:last + 1, :]
        u_last[below:below + 1, :] = zero_row
        _short_conv_gate(u_last, *conv_refs)


def _inproj0(x, g, mod, w, conv_w, conv_b, bias, layer, *, is_ctx, tm, cast=()):
    G, R, _ = x.shape
    n_tiles = R // tm
    n_steps = n_tiles + (1 if n_tiles > 1 else 0)
    cos, sin = _rope2d_tables()
    cos, sin = jnp.asarray(cos), jnp.asarray(sin)
    proj_tile = lambda i: jnp.minimum(i, n_tiles - 1)
    conv_tile = (lambda i: jnp.maximum(i - 1, 0)) if n_tiles > 1 else (lambda i: i)
    if is_ctx:
        tab = pl.BlockSpec((tm, LANES), lambda b, i: (0, 0))
    else:
        tab = pl.BlockSpec((tm, LANES), lambda b, i: (proj_tile(i), 0))
    const = lambda a: pl.BlockSpec(a.shape, lambda b, i: (0, 0))
    hy_out = jax.ShapeDtypeStruct((G, R, HY_WIDTH), F32)
    hy_spec = pl.BlockSpec((None, tm, HY_WIDTH), lambda b, i: (b, conv_tile(i), 0))
    if R == SEQ:
        z_out = jax.ShapeDtypeStruct((G, SEQ_P, HY_WIDTH), F32)
        z_spec = pl.BlockSpec((None, tm // FFT_R * FFT_PITCH, HY_WIDTH), lambda b, i: (b, conv_tile(i), 0))
    else:
        z_out, z_spec = hy_out, hy_spec
    kern = functools.partial(_inproj0_kernel, row=2 if is_ctx else None, rope=not is_ctx, n_tiles=n_tiles)
    in_specs = [
        pl.BlockSpec((None, tm, D_MODEL), lambda b, i: (b, proj_tile(i), 0)),
        pl.BlockSpec((1, D_MODEL), lambda b, i: (0, 0)),
        pl.BlockSpec((None, 8, D_MODEL), lambda b, i: (layer, 0, 0)),
        pl.BlockSpec((None, 8, D_MODEL), lambda b, i: (layer, 0, 1)),
        pl.BlockSpec((D_MODEL, EV_COLS), lambda b, i: (0, 0), pipeline_mode=pl.Buffered(1)),
        tab, tab, const(conv_w), const(conv_b), const(bias),
    ]
    out_specs = [z_spec, hy_spec, hy_spec, pl.BlockSpec((None, tm, EV_QKV), lambda b, i: (b, proj_tile(i), 0))]
    out_shape = [z_out, hy_out, hy_out, jax.ShapeDtypeStruct((G, R, EV_QKV), BF16)]
    c_in, c_out, c_shapes, c_args = _cast_jobs(cast, n_steps)
    if cast:
        assert G * n_steps >= CAST_CHUNKS
        kern = _with_casts(kern, len(in_specs), len(out_specs), len(cast))
    res = pl.pallas_call(
        kern,
        out_shape=out_shape + c_shapes,
        grid=(G, n_steps),
        in_specs=in_specs + c_in,
        out_specs=out_specs + c_out,
        scratch_shapes=[pltpu.VMEM((tm + 2 * CONV_PAD, EV_U), F32)] * 2,
        compiler_params=_params(("arbitrary", "arbitrary") if cast else ("parallel", "arbitrary"), VMEM_SMALL_MIB),
        name="inproj0_ctx" if is_ctx else "inproj0",
    )(x, g, mod, mod, w, cos, sin, conv_w, conv_b, bias, *c_args)
    return res[:4], res[4:]


def _inproj1_kernel(x_ref, g_ref, sh_ref, sc_ref, w_ref, o_ref, *, row, tn):
    r = pl.program_id(0) if row is None else row
    g, sh, sc = g_ref[...], _mod_row(sh_ref, r), _mod_row(sc_ref, r)
    sub = min(x_ref.shape[0], SUB_ROWS)
    for s in range(x_ref.shape[0] // sub):
        rows = slice(s * sub, (s + 1) * sub)
        h = _norm_mod(x_ref[rows, :], g, sh, sc).astype(BF16)
        for j in range(w_ref.shape[1] // tn):
            cols = slice(j * tn, (j + 1) * tn)
            o_ref[rows, cols] = jnp.dot(h, w_ref[:, cols], preferred_element_type=F32).astype(o_ref.dtype)


def _inproj1(x, g, mod, w, layer, *, is_ctx, tm, tn, cast=()):
    G, R, _ = x.shape
    N = w.shape[1]
    kern = functools.partial(_inproj1_kernel, row=2 if is_ctx else None, tn=tn)
    in_specs = [
        pl.BlockSpec((None, tm, D_MODEL), lambda b, i: (b, i, 0)),
        pl.BlockSpec((1, D_MODEL), lambda b, i: (0, 0)),
        pl.BlockSpec((None, 8, D_MODEL), lambda b, i: (layer, 0, 0)),
        pl.BlockSpec((None, 8, D_MODEL), lambda b, i: (layer, 0, 1)),
        pl.BlockSpec((D_MODEL, N), lambda b, i: (0, 0), pipeline_mode=pl.Buffered(1)),
    ]
    c_in, c_out, c_shapes, c_args = _cast_jobs(cast, R // tm)
    if cast:
        assert G * (R // tm) >= CAST_CHUNKS
        kern = _with_casts(kern, len(in_specs), 1, len(cast))
    res = pl.pallas_call(
        kern,
        out_shape=[jax.ShapeDtypeStruct((G, R, N), BF16)] + c_shapes,
        grid=(G, R // tm),
        in_specs=in_specs + c_in,
        out_specs=[pl.BlockSpec((None, tm, N), lambda b, i: (b, i, 0))] + c_out,
        compiler_params=_params(("arbitrary", "arbitrary") if cast else ("parallel", "parallel"), VMEM_SMALL_MIB),
        name="inproj1_ctx" if is_ctx else "inproj1",
    )(x, g, mod, mod, w, *c_args)
    return res[0], res[1:]


def _split_bf16(a):
    hi = a.astype(BF16)
    return hi, (a - hi.astype(F32)).astype(BF16)


def _dot3_split(a_hi, a_lo, b):
    b_hi, b_lo = _split_bf16(b)
    dot = functools.partial(jnp.dot, preferred_element_type=F32)
    return dot(a_hi, b_hi) + (dot(a_lo, b_hi) + dot(a_hi, b_lo))


def _dot3(a, b):
    return _dot3_split(*_split_bf16(a), b)


def _store_pitched(ref, lead, value):
    for k in range(value.shape[0] // FFT_R):
        ref[lead, k * FFT_PITCH:k * FFT_PITCH + FFT_R, :] = value[k * FFT_R:(k + 1) * FFT_R]
        ref[lead, k * FFT_PITCH + FFT_R:(k + 1) * FFT_PITCH, :] = jnp.zeros(
            (FFT_PITCH - FFT_R, value.shape[1]), value.dtype)


def _filter_kernel(w1t_ref, b1_ref, w2t_ref, b2_ref, fr_ref, w3_ref, dec_ref, bands_ref, h_ref, ss_ref, *,
                   L, tm, pitched):
    i = pl.program_id(0)
    t_row = (lax.broadcasted_iota(jnp.int32, (1, tm), 1) + i * tm).astype(F32) / L
    ang = (2.0 * math.pi * t_row) * bands_ref[...]
    t8 = jnp.where(lax.broadcasted_iota(jnp.int32, (8, tm), 0) == 0, t_row, 0.0)
    feat = jnp.concatenate([t8, jnp.cos(ang), -jnp.sin(ang)], axis=0)
    fr = fr_ref[...]
    hid = jnp.sin(fr * (jnp.dot(w1t_ref[...], feat, precision=HIGHEST, preferred_element_type=F32) + b1_ref[...]))
    hid = jnp.sin(fr * (jnp.dot(w2t_ref[...], hid, precision=HIGHEST, preferred_element_type=F32) + b2_ref[...]))
    hid_hi, hid_lo = _split_bf16(hid.T)
    m_col = lax.broadcasted_iota(jnp.int32, (tm, 1), 0) + i * tm
    t_col = m_col.astype(F32) / L
    energy = jnp.zeros((1, HY_WIDTH), F32)
    for side in range(2):
        h = _dot3_split(hid_hi, hid_lo, w3_ref[side]) * jnp.exp(-t_col * jnp.abs(dec_ref[side]))
        if side == 1:
            h = jnp.where(m_col == 0, 0.0, h)
        if pitched:
            _store_pitched(h_ref, side, h)
        else:
            h_ref[side] = h
        energy = energy + jnp.sum(h * h, axis=0, keepdims=True)

    @pl.when(i == 0)
    def _():
        ss_ref[...] = jnp.zeros_like(ss_ref)

    ss_ref[...] += jnp.broadcast_to(energy, ss_ref.shape)


def _hyena_filter(w1, b1, w2, b2, w3, freq, decay, *, L):
    tm = min(L, 1024)
    pitched = L == SEQ
    tm_out, l_out = (tm // FFT_R * FFT_PITCH, SEQ_P) if pitched else (tm, L)
    kern = functools.partial(_filter_kernel, L=L, tm=tm, pitched=pitched)
    w1t = jnp.concatenate([w1[0:1], jnp.zeros((7, HY_HIDDEN), F32), w1[1:]], axis=0).T
    col = lambda a: a.reshape(HY_HIDDEN, 1)
    w3s = jnp.stack([w3[:, :HY_WIDTH], w3[:, HY_WIDTH:]])
    bands = jnp.asarray(np.linspace(1e-4, HY_BANDS - 1, HY_BANDS).astype(np.float32).reshape(HY_BANDS, 1))
    args = (w1t, col(b1), w2.T, col(b2), col(freq), w3s, decay.reshape(2, 1, HY_WIDTH), bands)
    full = lambda a: pl.BlockSpec(a.shape, lambda i: (0,) * a.ndim)
    h, ss = pl.pallas_call(
        kern,
        out_shape=(jax.ShapeDtypeStruct((2, l_out, HY_WIDTH), F32), jax.ShapeDtypeStruct((8, HY_WIDTH), F32)),
        grid=(L // tm,),
        in_specs=[full(a) for a in args],
        out_specs=(pl.BlockSpec((2, tm_out, HY_WIDTH), lambda i: (0, i, 0)),
                   pl.BlockSpec((8, HY_WIDTH), lambda i: (0, 0))),
        compiler_params=_params(("arbitrary",), VMEM_SMALL_MIB),
        name="hyena_filter_%d" % L,
    )(*args)
    return h.reshape(2 * l_out, HY_WIDTH), ss


def _stack_bf16(re, im):
    return jnp.concatenate([re, im], axis=0).astype(BF16)


def _filter_fft_kernel(h_ref, ss_ref, m1_ref, f2_ref, o_ref, a_scr):
    R, P = FFT_R, FFT_PITCH
    inv_norm = lax.rsqrt(ss_ref[0:1, :])

    def stage1(n2, carry):
        neg = jnp.where(n2 == 0, 0, R - n2)
        xs = jnp.concatenate([h_ref[pl.ds(n2, R // 2, stride=P), :],
                              h_ref[pl.ds(SEQ_P + neg, R // 2, stride=P), :]], axis=0)
        xs = (xs * inv_norm).astype(BF16)
        a = jnp.dot(m1_ref[n2], xs, preferred_element_type=F32)
        base = pl.multiple_of(n2 * P, 8)
        a_scr[0, pl.ds(base, R), :] = a[:R]
        a_scr[1, pl.ds(base, R), :] = a[R:]
        return carry

    lax.fori_loop(0, R, stage1, 0, unroll=FFT_UNROLL)

    ct = a_scr.shape[-1]

    def stage2(pair, carry):
        k1s = (2 * pair, 2 * pair + 1)
        rhs = jnp.concatenate([_stack_bf16(a_scr[0, pl.ds(k1, R, stride=P), :],
                                           a_scr[1, pl.ds(k1, R, stride=P), :]) for k1 in k1s], axis=1)
        x = jnp.dot(f2_ref[...], rhs, preferred_element_type=F32)
        for d, k1 in enumerate(k1s):
            base = pl.multiple_of(k1 * R, R)
            o_ref[0, pl.ds(base, R), :] = x[:R, d * ct:(d + 1) * ct].astype(BF16)
            o_ref[1, pl.ds(base, R), :] = x[R:, d * ct:(d + 1) * ct].astype(BF16)
        return carry

    lax.fori_loop(0, R // 2, stage2, 0, unroll=FFT_UNROLL // 2)


def _filter_fft(h_circ, ss, m1, f2):
    ct = LANES
    once = pl.Buffered(1)
    return pl.pallas_call(
        _filter_fft_kernel,
        out_shape=jax.ShapeDtypeStruct((2, FFT_N, HY_WIDTH), BF16),
        grid=(HY_WIDTH // ct,),
        in_specs=[
            pl.BlockSpec((2 * SEQ_P, ct), lambda c: (0, c), pipeline_mode=once),
            pl.BlockSpec((8, ct), lambda c: (0, c)),
            pl.BlockSpec(m1.shape, lambda c: (0, 0, 0), pipeline_mode=once),
            pl.BlockSpec(f2.shape, lambda c: (0, 0), pipeline_mode=once),
        ],
        out_specs=pl.BlockSpec((2, FFT_N, ct), lambda c: (0, 0, c)),
        scratch_shapes=[pltpu.VMEM((2, FFT_R * FFT_PITCH, ct), F32)],
        compiler_params=_params(("parallel",), VMEM_FFT_MIB),
        name="hyena_filter_fft",
    )(h_circ, ss, m1, f2)


def _fft_conv_kernel(z_ref, hs_ref, m1_ref, f2_ref, g2_ref, m1i_ref, y_ref, a_scr):
    R, P = FFT_R, FFT_PITCH
    ct = a_scr.shape[-1]

    def stage1(n2, carry):
        rows = pl.ds(n2, R // 2, stride=P)
        xs = jnp.concatenate([z_ref[0, rows, :], z_ref[1, rows, :]], axis=1).astype(BF16)
        t = jnp.dot(m1_ref[n2, :, 0:R // 2], xs, preferred_element_type=F32)
        base = pl.multiple_of(n2 * P, 8)
        a_scr[0, pl.ds(base, R), :] = t[:R, :ct] - t[R:, ct:]
        a_scr[1, pl.ds(base, R), :] = t[R:, :ct] + t[:R, ct:]
        return carry

    lax.fori_loop(0, R, stage1, 0, unroll=FFT_UNROLL)

    def stage2(pair, carry):
        k1s = (2 * pair, 2 * pair + 1)
        rhs = jnp.concatenate([_stack_bf16(a_scr[0, pl.ds(k1, R, stride=P), :],
                                           a_scr[1, pl.ds(k1, R, stride=P), :]) for k1 in k1s], axis=1)
        x = jnp.dot(f2_ref[...], rhs, preferred_element_type=F32)
        prods = []
        for d, k1 in enumerate(k1s):
            base = pl.multiple_of(k1 * R, R)
            hr = hs_ref[0, pl.ds(base, R), :].astype(F32)
            hi = hs_ref[1, pl.ds(base, R), :].astype(F32)
            xr, xi = x[:R, d * ct:(d + 1) * ct], x[R:, d * ct:(d + 1) * ct]
            prods.append(_stack_bf16(xr * hr - xi * hi, xr * hi + xi * hr))
        bq = jnp.dot(g2_ref[...], jnp.concatenate(prods, axis=1), preferred_element_type=F32)
        for d, k1 in enumerate(k1s):
            a_scr[0, pl.ds(k1, R, stride=P), :] = bq[:R, d * ct:(d + 1) * ct]
            a_scr[1, pl.ds(k1, R, stride=P), :] = bq[R:, d * ct:(d + 1) * ct]
        return carry

    lax.fori_loop(0, R // 2, stage2, 0, unroll=FFT_UNROLL // 2)

    def stage3(n2, carry):
        base = pl.multiple_of(n2 * P, 8)
        br = a_scr[0, pl.ds(base, R), :]
        bi = a_scr[1, pl.ds(base, R), :]
        rhs = jnp.concatenate([jnp.concatenate([br, bi], axis=1),
                               jnp.concatenate([bi, -br], axis=1)], axis=0).astype(BF16)
        y = jnp.dot(m1i_ref[n2], rhs, preferred_element_type=F32)
        rows = pl.ds(n2, R // 2, stride=P)
        y_ref[0, rows, :] = y[:, :ct]
        y_ref[1, rows, :] = y[:, ct:]
        return carry

    lax.fori_loop(0, R, stage3, 0, unroll=FFT_UNROLL)
    for b in range(2):
        for n1 in range(R // 2):
            y_ref[b, n1 * P + R:(n1 + 1) * P, :] = jnp.zeros((P - R, ct), F32)


def _fft_conv(z, hspec, m1, f2, g2, m1i):
    assert z.shape[0] == 2, "the batch pair is packed into one complex signal"
    ct = LANES
    once = pl.Buffered(1)
    return pl.pallas_call(
        _fft_conv_kernel,
        out_shape=jax.ShapeDtypeStruct((2, SEQ_P, HY_WIDTH), F32),
        grid=(HY_WIDTH // ct,),
        in_specs=[
            pl.BlockSpec((2, SEQ_P, ct), lambda c: (0, 0, c), pipeline_mode=once),
            pl.BlockSpec((2, FFT_N, ct), lambda c: (0, 0, c), pipeline_mode=once),
            pl.BlockSpec(m1.shape, lambda c: (0, 0, 0), pipeline_mode=once),
            pl.BlockSpec(f2.shape, lambda c: (0, 0), pipeline_mode=once),
            pl.BlockSpec(g2.shape, lambda c: (0, 0), pipeline_mode=once),
            pl.BlockSpec(m1i.shape, lambda c: (0, 0, 0), pipeline_mode=once),
        ],
        out_specs=pl.BlockSpec((2, SEQ_P, ct), lambda c: (0, 0, c), pipeline_mode=once),
        scratch_shapes=[pltpu.VMEM((2, FFT_R * FFT_PITCH, ct), F32)],
        compiler_params=_params(("arbitrary",), VMEM_FFT_MIB),
        name="hyena_fft_conv",
    )(z, hspec, m1, f2, g2, m1i)


def _ctx_conv_kernel(z_ref, h_ref, ss_ref, fwd_ref, inv_ref, fwd_h_ref, y_ref):
    N = 2 * CTX_LEN
    hn = (h_ref[...] * lax.rsqrt(ss_ref[0:1, :])).astype(BF16)
    hs = jnp.dot(fwd_h_ref[...], hn, preferred_element_type=F32)
    zs = jnp.dot(fwd_ref[:, :CTX_LEN], z_ref[...].astype(BF16), preferred_element_type=F32)
    hr, hi, zr, zi = hs[:N], hs[N:], zs[:N], zs[N:]
    ys = _stack_bf16(zr * hr - zi * hi, zr * hi + zi * hr)
    y_ref[...] = jnp.dot(inv_ref[...], ys, preferred_element_type=F32)


def _ctx_conv(z, h_circ, ss, fwd, inv, fwd_h):
    G = z.shape[0]
    full = lambda a: pl.BlockSpec(a.shape, lambda b: (0,) * a.ndim)
    return pl.pallas_call(
        _ctx_conv_kernel,
        out_shape=jax.ShapeDtypeStruct((G, CTX_LEN, HY_WIDTH), F32),
        grid=(G,),
        in_specs=[pl.BlockSpec((None, CTX_LEN, HY_WIDTH), lambda b: (b, 0, 0)),
                  full(h_circ), full(ss), full(fwd), full(inv), full(fwd_h)],
        out_specs=pl.BlockSpec((None, CTX_LEN, HY_WIDTH), lambda b: (b, 0, 0)),
        compiler_params=_params(("parallel",), VMEM_SMALL_MIB),
        name="hyena_ctx_conv",
    )(z, h_circ, ss, fwd, inv, fwd_h)


def _kv_group(refs, g, col0):
    cols = slice(col0 + g * LANES, col0 + (g + 1) * LANES)
    return refs[0][:, cols] if len(refs) == 1 else jnp.concatenate([r[:, cols] for r in refs], axis=0)


def _attn_logits(q_ref, k_refs, g, masks, col0=0):
    low = lax.broadcasted_iota(jnp.int32, (BLOCK, LANES), 1) < HEAD_DIM
    zero = jnp.zeros((), BF16)
    parts = []
    for hh in range(ATT_GROUP):
        h = g * ATT_GROUP + hh
        tile = q_ref[:, (h // 2) * LANES:(h // 2 + 1) * LANES]
        parts.append(jnp.where(low if h % 2 == 0 else ~low, tile, zero))
    s = lax.dot_general(jnp.concatenate(parts, axis=0), _kv_group(k_refs, g, col0), (((1,), (1,)), ((), ())),
                        preferred_element_type=F32)
    if masks is not None:
        ok_prev, ok_next = masks
        s = jnp.concatenate([jnp.where(ok_prev, s[:, :BLOCK], NEG_INF), s[:, BLOCK:2 * BLOCK],
                             jnp.where(ok_next, s[:, 2 * BLOCK:3 * BLOCK], NEG_INF), s[:, 3 * BLOCK:]], axis=1)
    return s


def _attn_output(s, sink_ref, v_refs, g, o_ref, col0=0):
    low = lax.broadcasted_iota(jnp.int32, (BLOCK, LANES), 1) < HEAD_DIM
    low4 = jnp.concatenate([low] * ATT_GROUP, axis=0)
    v = _kv_group(v_refs, g, col0)
    v_aug = jnp.where(lax.broadcasted_iota(jnp.int32, v.shape, 1) < HEAD_DIM, v, jnp.ones((), BF16))
    sink = jnp.concatenate(
        [jnp.full((BLOCK, 1), sink_ref[g * ATT_GROUP + hh] * LOG2E, F32) for hh in range(ATT_GROUP)], axis=0)
    m = jnp.maximum(jnp.max(s, axis=-1, keepdims=True), sink)
    e = jnp.exp2(s - m).astype(BF16)
    o = jnp.dot(e, v_aug, preferred_element_type=F32) + jnp.where(low4, 0.0, jnp.exp2(sink - m))
    swapped = pltpu.roll(o, HEAD_DIM, axis=1)
    for pair in range(ATT_GROUP // 2):
        ev = slice((2 * pair) * BLOCK, (2 * pair + 1) * BLOCK)
        od = slice((2 * pair + 1) * BLOCK, (2 * pair + 2) * BLOCK)
        even = o[ev] / swapped[ev]
        odd = swapped[od] / o[od]
        t = g * (ATT_GROUP // 2) + pair
        o_ref[:, t * LANES:(t + 1) * LANES] = jnp.where(low, even, odd).astype(o_ref.dtype)


def _attn_ctx_kernel(sink_ref, q_ref, kx_ref, vx_ref, o_ref):
    for g in range(ATT_KV_HEADS):
        _attn_output(_attn_logits(q_ref, (kx_ref,), g, None), sink_ref, (vx_ref,), g, o_ref)


def _attn_local_kernel(sink_ref, q_ref, kv_m2, kv_m1, kv_0, kv_p1, kv_ctx, o_ref, s_even, s_odd, *, n_blocks):
    n = pl.program_id(1)
    k_refs, v_refs, v_col = (kv_m1, kv_0, kv_p1, kv_ctx), (kv_m2, kv_m1, kv_0, kv_ctx), 2 * LANES

    @pl.when(n == 0)
    def _():
        s_odd[...] = jnp.zeros_like(s_odd)

    def step(s_new, s_prev):
        nq = jnp.minimum(n, n_blocks - 1)
        qi = lax.broadcasted_iota(jnp.int32, (ATT_GROUP * BLOCK, BLOCK), 0) % BLOCK
        ki = lax.broadcasted_iota(jnp.int32, (ATT_GROUP * BLOCK, BLOCK), 1)
        masks = ((ki >= qi) & (nq > 0), (ki <= qi) & (nq < n_blocks - 1))
        for g in range(ATT_KV_HEADS):
            s_new[g] = _attn_logits(q_ref, k_refs, g, masks)
        for g in range(ATT_KV_HEADS):
            _attn_output(s_prev[g], sink_ref, v_refs, g, o_ref, v_col)

    @pl.when(n % 2 == 0)
    def _():
        step(s_even, s_odd)

    @pl.when(n % 2 == 1)
    def _():
        step(s_odd, s_even)


def _attention(sink, qkv, qkv_ctx, *, local):
    src = qkv if local else qkv_ctx
    G, R, _ = src.shape
    nb = R // BLOCK
    kcol, vcol = ATT_WIDTH // 256, ATT_WIDTH // 256 + 1
    ctx_k = pl.BlockSpec((None, CTX_LEN, 256), lambda b, n: (b, 0, kcol))
    ctx_v = pl.BlockSpec((None, CTX_LEN, 256), lambda b, n: (b, 0, vcol))
    smem = pl.BlockSpec(memory_space=pltpu.SMEM)
    out_shape = jax.ShapeDtypeStruct((G, R, ATT_WIDTH), BF16)
    if not local:
        rows = pl.BlockSpec((None, BLOCK, ATT_WIDTH), lambda b, n: (b, n, 0))
        return pl.pallas_call(
            _attn_ctx_kernel, out_shape=out_shape, grid=(G, nb),
            in_specs=[smem, rows, ctx_k, ctx_v], out_specs=rows,
            compiler_params=_params(("parallel", "parallel"), VMEM_SMALL_MIB), name="attn_ctx",
        )(sink, qkv_ctx, qkv_ctx, qkv_ctx)

    q_blk = lambda n: jnp.minimum(n, nb - 1)
    o_blk = lambda n: jnp.maximum(n - 1, 0)

    kv_w = EV_QKV - ATT_WIDTH
    kv_col = ATT_WIDTH // kv_w

    def kv_rows(off):
        return pl.BlockSpec((None, BLOCK, kv_w), lambda b, n: (b, jnp.clip(n + off, 0, nb - 1), kv_col))

    in_specs = [smem, pl.BlockSpec((None, BLOCK, ATT_WIDTH), lambda b, n: (b, q_blk(n), 0)),
                kv_rows(-2), kv_rows(-1), kv_rows(0), kv_rows(1),
                pl.BlockSpec((None, CTX_LEN, kv_w), lambda b, n: (b, 0, kv_col))]
    logits_scratch = pltpu.VMEM((ATT_KV_HEADS, ATT_GROUP * BLOCK, 3 * BLOCK + CTX_LEN), F32)
    return pl.pallas_call(
        functools.partial(_attn_local_kernel, n_blocks=nb),
        out_shape=out_shape,
        grid=(G, nb + 1),
        in_specs=in_specs,
        out_specs=pl.BlockSpec((None, BLOCK, ATT_WIDTH), lambda b, n: (b, o_blk(n), 0)),
        scratch_shapes=[logits_scratch, logits_scratch],
        compiler_params=_params(("parallel", "arbitrary"), VMEM_SMALL_MIB),
        name="attn_local",
    )(sink, qkv, qkv, qkv, qkv, qkv, qkv_ctx)


def _mixer_out0(rows, x0_ref, y_ref, zb_ref, att_ref, wo_ref):
    if y_ref.shape[0] != x0_ref.shape[0]:
        y = jnp.concatenate([y_ref[r // FFT_R * FFT_PITCH:r // FFT_R * FFT_PITCH + FFT_R, :]
                             for r in range(rows.start, rows.stop, FFT_R)], axis=0)
    else:
        y = y_ref[rows, :]
    hy = (x0_ref[rows, :] * y + zb_ref[rows, :]).astype(BF16)
    return (jnp.dot(hy, wo_ref[:HY_WIDTH, :], preferred_element_type=F32)
            + jnp.dot(att_ref[rows, :], wo_ref[HY_WIDTH:, :], preferred_element_type=F32))


def _mixer_out1(rows, of_ref, ob_ref, wo_ref):
    a = (of_ref[rows, :].astype(F32) + ob_ref[rows, :].astype(F32)).astype(BF16)
    return jnp.dot(a, wo_ref[...], preferred_element_type=F32)


def _mix_kernel(*refs, mixer_out, n_mix, row, fc, final_norm):
    x_ref = refs[0]
    mix_refs = refs[1:2 + n_mix]
    gta_ref, gm_ref, shm_ref, scm_ref, gtm_ref, w1_ref, w2_ref, fg_ref, o_ref, a_scr = refs[2 + n_mix:]
    r = pl.program_id(0) if row is None else row
    gta, gtm = _mod_row(gta_ref, r), _mod_row(gtm_ref, r)
    gm, shm, scm = gm_ref[...], _mod_row(shm_ref, r), _mod_row(scm_ref, r)
    sub = min(x_ref.shape[0], MIX_SUB_ROWS)
    for s in range(x_ref.shape[0] // sub):
        rows = slice(s * sub, (s + 1) * sub)
        x1 = x_ref[rows, :] + gta * mixer_out(rows, *mix_refs)
        h = _norm_mod(x1, gm, shm, scm).astype(BF16)
        for c in range(D_FF // fc):
            a = jnp.maximum(jnp.dot(h, w1_ref[:, c * fc:(c + 1) * fc], preferred_element_type=F32), 0.0)
            a_scr[rows, c * fc:(c + 1) * fc] = (a * a).astype(BF16)
        out = x1 + gtm * jnp.dot(a_scr[rows, :], w2_ref[...], preferred_element_type=F32)
        if final_norm:
            out = (out * lax.rsqrt(jnp.mean(out * out, axis=-1, keepdims=True) + EPS)) * fg_ref[...]
        o_ref[rows, :] = out


def _mix_mlp(kind, x, mix_in, wo, mod, gm, w1, w2, fg, layer, *, is_ctx, tm, fc, final_norm, cast=()):
    G, R, _ = x.shape
    row_spec = lambda w: pl.BlockSpec((None, tm, w), lambda b, i: (b, i, 0))
    modk = lambda k: pl.BlockSpec((None, 8, D_MODEL), lambda b, i: (layer, 0, k))
    vec = pl.BlockSpec((1, D_MODEL), lambda b, i: (0, 0))
    resident = lambda a: pl.BlockSpec(a.shape, lambda b, i: (0, 0), pipeline_mode=pl.Buffered(1))
    if kind == 0:
        mixer_out = _mixer_out0
        y_rows = tm if mix_in[1].shape[1] == R else tm // FFT_R * FFT_PITCH
        y_spec = pl.BlockSpec((None, y_rows, HY_WIDTH), lambda b, i: (b, i, 0))
        mix_specs = [row_spec(HY_WIDTH), y_spec, row_spec(HY_WIDTH), row_spec(ATT_WIDTH)]
    else:
        mixer_out = _mixer_out1
        mix_specs = [row_spec(RET_V)] * 2
    kern = functools.partial(_mix_kernel, mixer_out=mixer_out, n_mix=len(mix_in), row=2 if is_ctx else None,
                             fc=fc, final_norm=final_norm)
    in_specs = [row_spec(D_MODEL)] + mix_specs + [
        resident(wo), modk(2), vec, modk(3), modk(4), modk(5), resident(w1), resident(w2), vec]
    c_in, c_out, c_shapes, c_args = _cast_jobs(cast, R // tm)
    if cast:
        assert G * (R // tm) >= CAST_CHUNKS
        kern = _with_casts(kern, len(in_specs), 1, len(cast))
    res = pl.pallas_call(
        kern,
        out_shape=[jax.ShapeDtypeStruct((G, R, D_MODEL), F32)] + c_shapes,
        grid=(G, R // tm),
        in_specs=in_specs + c_in,
        out_specs=[row_spec(D_MODEL)] + c_out,
        scratch_shapes=[pltpu.VMEM((tm, D_FF), BF16)],
        compiler_params=_params(("arbitrary", "arbitrary") if cast else ("parallel", "parallel"), VMEM_DENSE_MIB),
        name="mix_mlp%d%s" % (kind, "_ctx" if is_ctx else ""),
    )(x, *mix_in, wo, mod, gm, mod, mod, mod, w1, w2, fg, *c_args)
    return res[0], res[1:]


def _ret_kernel(lr_ref, qkv_c, qkv_f, g_f, qkv_b, g_b, cos_ref, sin_ref, of_ref, ob_ref,
                state, dmask, xi, zeta, gch):
    j = pl.program_id(1)
    C = RET_C
    kscale = RET_DK ** -0.5

    @pl.when(j == 0)
    def _():
        state[...] = jnp.zeros_like(state)
        row = lax.broadcasted_iota(jnp.int32, (C, C), 0).astype(F32)
        col = lax.broadcasted_iota(jnp.int32, (C, C), 1).astype(F32)
        for d in range(2):
            e = row - col if d == 0 else col - row
            p = row if d == 0 else (C - 1) - row
            for h in range(RET_HEADS):
                lg = -jnp.exp(jnp.full((C, C), lr_ref[d, h], F32))
                dmask[d, h] = jnp.where(e >= 0, jnp.exp(e * lg), 0.0) * kscale
                xi[d, h] = jnp.exp((p + 1.0) * lg)
                zeta[d, h] = jnp.exp(((C - 1) - p) * lg) * kscale
                gch[d, h] = jnp.exp(C * -jnp.exp(jnp.full((8, C), lr_ref[d, h], F32)))

    half = RET_DK // 2

    def chain(d, h, src_ref, g_ref, o_ref):
        cos, sin = cos_ref[d], sin_ref[d]

        def rot(c0):
            t1 = src_ref[:, c0:c0 + half].astype(F32)
            t2 = src_ref[:, c0 + half:c0 + RET_DK].astype(F32)
            return jnp.concatenate([t1 * cos - t2 * sin, t2 * cos + t1 * sin], axis=1)

        q = rot(h * RET_DK)
        k = rot(RET_QK + h * RET_DK)
        v = src_ref[:, 2 * RET_QK + h * RET_DV:2 * RET_QK + (h + 1) * RET_DV]
        inner = lax.dot_general(q.astype(BF16), k.astype(BF16), (((1,), (1,)), ((), ())),
                                preferred_element_type=F32) * dmask[d, h]
        s_old = state[d, h]
        kv = lax.dot_general((k * zeta[d, h]).astype(BF16), v, (((0,), (0,)), ((), ())),
                             preferred_element_type=F32)
        state[d, h] = gch[d, h, 0:1, 0:1] * s_old + kv
        if o_ref is not None:
            o = (jnp.dot(inner.astype(BF16), v, preferred_element_type=F32)
                 + jnp.dot((q * xi[d, h]).astype(BF16), s_old.astype(BF16), preferred_element_type=F32))
            on = o * lax.rsqrt(jnp.mean(o * o, axis=-1, keepdims=True) + EPS)
            gate = g_ref[:, h * RET_DV:(h + 1) * RET_DV]
            half_gate = gate * 0.5
            silu = half_gate + half_gate * jnp.tanh(half_gate)
            o_ref[:, h * RET_DV:(h + 1) * RET_DV] = (silu.astype(F32) * on).astype(o_ref.dtype)

    @pl.when(j == 0)
    def _():
        for d in range(2):
            for h in range(RET_HEADS):
                chain(d, h, qkv_c, None, None)

    @pl.when(j > 0)
    def _():
        for h in range(RET_HEADS):
            chain(0, h, qkv_f, g_f, of_ref)
            chain(1, h, qkv_b, g_b, ob_ref)


def _retention(log_rate, p_ctx, p_lat):
    G = p_lat.shape[0]
    C = RET_C
    n_lat = SEQ // C
    cos, sin = _rope1d_tables()
    cos, sin = jnp.asarray(cos), jnp.asarray(sin)

    def fwd_chunk(j):
        return jnp.maximum(j - 1, 0)

    def bwd_chunk(j):
        return n_lat - 1 - jnp.maximum(j - 1, 0)

    qkv_w = 2 * RET_QK + RET_V
    gcol = qkv_w // RET_V
    tab = pl.BlockSpec((2, C, RET_DK // 2), lambda b, j: (0, j, 0))
    in_specs = [
        pl.BlockSpec(memory_space=pltpu.SMEM),
        pl.BlockSpec((None, C, qkv_w), lambda b, j: (b, 0, 0)),
        pl.BlockSpec((None, C, qkv_w), lambda b, j: (b, fwd_chunk(j), 0)),
        pl.BlockSpec((None, C, RET_V), lambda b, j: (b, fwd_chunk(j), gcol)),
        pl.BlockSpec((None, C, qkv_w), lambda b, j: (b, bwd_chunk(j), 0)),
        pl.BlockSpec((None, C, RET_V), lambda b, j: (b, bwd_chunk(j), gcol + 1)),
        tab, tab,
    ]
    out = jax.ShapeDtypeStruct((G, SEQ, RET_V), BF16)
    per_chain = lambda *tail: pltpu.VMEM((2, RET_HEADS) + tail, F32)
    return pl.pallas_call(
        _ret_kernel,
        out_shape=(out, out),
        grid=(G, 1 + n_lat),
        in_specs=in_specs,
        out_specs=(pl.BlockSpec((None, C, RET_V), lambda b, j: (b, fwd_chunk(j), 0)),
                   pl.BlockSpec((None, C, RET_V), lambda b, j: (b, bwd_chunk(j), 0))),
        scratch_shapes=[per_chain(RET_DK, RET_DV), per_chain(C, C), per_chain(C, C), per_chain(C, C),
                        per_chain(8, C)],
        compiler_params=_params(("parallel", "arbitrary"), VMEM_SMALL_MIB),
        name="retention",
    )(log_rate, p_ctx, p_lat, p_lat, p_lat, p_lat, cos, sin)


def _ev_weight(w_in):
    i_q = EV_U
    i_k = i_q + ATT_WIDTH
    i_v = i_k + KV_WIDTH
    cols = [w_in[:, :i_q], w_in[:, i_q:i_k] * (HEAD_DIM ** -0.5 * LOG2E)]
    for base in (i_k, i_v):
        for g in range(ATT_KV_HEADS):
            part = w_in[:, base + g * HEAD_DIM: base + (g + 1) * HEAD_DIM]
            cols += [part, part]
    return jnp.concatenate([c.astype(BF16) for c in cols], axis=1)


def kernel(x, c, ctx, c_ctx, ada_w, ada_b, norm_mix_g, norm_mlp_g, mlp_w1, mlp_w2, ev_w_in, ev_w_out, hy_conv_w, hy_conv_b, hy_w1, hy_b1, hy_w2, hy_b2, hy_w3, hy_freq, hy_decay, hy_bias, attn_sink, od_w_in, od_w_out, ret_log_rate, final_g):
    D = D_MODEL
    cvec = jnp.concatenate([c, c_ctx[None, :], jnp.zeros((8 - BATCH - 1, D), F32)], axis=0)
    mod = _ada(cvec, ada_w, ada_b)

    m1, m1i, f2, g2 = (jnp.asarray(a).astype(BF16) for a in _fft_mats())
    cfwd, cinv, cfwd_h = (jnp.asarray(a).astype(BF16) for a in _ctx_fft_mats())
    fg = final_g.reshape(1, D)

    gmix = norm_mix_g[0].reshape(1, D)
    gmlp = norm_mlp_g[0].reshape(1, D)
    w_in = _ev_weight(ev_w_in[0])
    conv = (hy_conv_w[0], hy_conv_b[0].reshape(1, EV_U), hy_bias[0].reshape(1, HY_WIDTH))
    (z_l, x0_l, zb_l, qkv_l), (wo, w1, w2) = _inproj0(
        x, gmix, mod, w_in, *conv, 0, is_ctx=False, tm=512, cast=((ev_w_out, 0), (mlp_w1, 0), (mlp_w2, 0)))
    (z_c, x0_c, zb_c, qkv_c), _ = _inproj0(ctx, gmix, mod, w_in, *conv, 0, is_ctx=True, tm=CTX_LEN)

    filt = (hy_w1[0], hy_b1[0], hy_w2[0], hy_b2[0], hy_w3[0], hy_freq[0], hy_decay[0])
    h_l, ss_l = _hyena_filter(*filt, L=SEQ)
    h_c, ss_c = _hyena_filter(*filt, L=CTX_LEN)
    hspec = _filter_fft(h_l, ss_l, m1, f2)
    y_l = _fft_conv(z_l, hspec, m1, f2, g2, m1i)
    y_c = _ctx_conv(z_c, h_c, ss_c, cfwd, cinv, cfwd_h)

    sink = attn_sink[0]
    att_l = _attention(sink, qkv_l, qkv_c, local=True)
    att_c = _attention(sink, qkv_l, qkv_c, local=False)

    x1, (w_in,) = _mix_mlp(0, x, (x0_l, y_l, zb_l, att_l), wo, mod, gmlp, w1, w2, fg, 0,
                           is_ctx=False, tm=512, fc=512, final_norm=False, cast=((od_w_in, 0),))
    ctx1, _ = _mix_mlp(0, ctx, (x0_c, y_c, zb_c, att_c), wo, mod, gmlp, w1, w2, fg, 0,
                       is_ctx=True, tm=CTX_LEN, fc=512, final_norm=False)

    gmix = norm_mix_g[1].reshape(1, D)
    gmlp = norm_mlp_g[1].reshape(1, D)
    p_l, (wo, w1, w2) = _inproj1(x1, gmix, mod, w_in, 1, is_ctx=False, tm=512, tn=512,
                                 cast=((od_w_out, 0), (mlp_w1, 1), (mlp_w2, 1)))
    p_c, _ = _inproj1(ctx1, gmix, mod, w_in, 1, is_ctx=True, tm=CTX_LEN, tn=512)
    o_fwd, o_bwd = _retention(ret_log_rate[0], p_c, p_l)

    out, _ = _mix_mlp(1, x1, (o_fwd, o_bwd), wo, mod, gmlp, w1, w2, fg, 1,
                      is_ctx=False, tm=512, fc=512, final_norm=True)
    return out
```

```python
import functools
import math

import numpy as np
import jax
import jax.numpy as jnp
from jax import lax
from jax.experimental import pallas as pl
from jax.experimental.pallas import tpu as pltpu

F32 = jnp.float32
BF16 = jnp.bfloat16
HIGHEST = lax.Precision.HIGHEST

D_MODEL = 1024
BATCH = 2
SEQ = 8192
DEPTH = 2
GRID_W = 64
CTX_LEN = 256
EPS = 1e-6
NEG_INF = -1e30
N_MOD = 6
D_FF = 4 * D_MODEL
ROPE_BASE = 10000.0

HY_WIDTH = D_MODEL // 2
HY_EMB = 33
HY_BANDS = (HY_EMB - 1) // 2
HY_HIDDEN = 64

ATT_HEADS = 8
ATT_KV_HEADS = 2
ATT_GROUP = ATT_HEADS // ATT_KV_HEADS
HEAD_DIM = 64
ATT_WIDTH = ATT_HEADS * HEAD_DIM
KV_WIDTH = ATT_KV_HEADS * HEAD_DIM
BLOCK = 128

RET_HEADS = 4
RET_DK = D_MODEL // RET_HEADS
RET_DV = 2 * RET_DK
RET_QK = RET_HEADS * RET_DK
RET_V = RET_HEADS * RET_DV
OD_IN = 2 * RET_QK + 3 * RET_V

LOG2E = 1.4426950408889634
LANES = 128
MIB = 1024 * 1024

EV_U = 3 * HY_WIDTH
EV_QKV = ATT_WIDTH + 4 * LANES
EV_COLS = EV_U + EV_QKV

FFT_N = 2 * SEQ
FFT_R = 128
FFT_PITCH = FFT_R + 8
SEQ_P = SEQ // FFT_R * FFT_PITCH
CONV_ROWS = 64
CONV_PAD = 8
SUB_ROWS = 256
MIX_SUB_ROWS = 512
FFT_UNROLL = 64
RET_C = 256
CAST_CHUNKS = 32


VMEM_SMALL_MIB = 48
VMEM_DENSE_MIB = 56
VMEM_FFT_MIB = 58


def _params(sem, vmem_mib):
    return pltpu.CompilerParams(dimension_semantics=sem, vmem_limit_bytes=vmem_mib * MIB)


@functools.lru_cache(maxsize=None)
def _rope2d_tables():
    quarter = HEAD_DIM // 4
    inv = ROPE_BASE ** (-np.arange(quarter, dtype=np.float64) / quarter)
    t = np.arange(SEQ)
    pos = np.stack([t // GRID_W, t % GRID_W], axis=1).astype(np.float64)
    lane = np.arange(HEAD_DIM)
    half = lane // (HEAD_DIM // 2)
    e = lane % (HEAD_DIM // 2)
    ang = pos[:, half] * inv[e % quarter][None, :]
    sign = np.where(e < quarter, -1.0, 1.0)[None, :]
    cos = np.tile(np.cos(ang), (1, 2)).astype(np.float32)
    sin = np.tile(np.sin(ang) * sign, (1, 2)).astype(np.float32)
    return cos, sin


@functools.lru_cache(maxsize=None)
def _rope1d_tables():
    n = RET_DK // 2
    inv = ROPE_BASE ** (-np.linspace(0.0, 1.0, n))
    pos = np.arange(CTX_LEN + SEQ, dtype=np.float64)
    ang = pos[:, None] * inv[None, :]
    cos, sin = np.cos(ang), np.sin(ang)

    def rev(a):
        return a.reshape(-1, RET_C, n)[:, ::-1].reshape(-1, n)

    cos2 = np.stack([cos, rev(cos)]).astype(np.float32)
    sin2 = np.stack([sin, rev(sin)]).astype(np.float32)
    return cos2, sin2


@functools.lru_cache(maxsize=None)
def _fft_mats():
    N, R = FFT_N, FFT_R
    k1 = np.arange(R)
    n2 = np.arange(R)[:, None, None]
    n1 = np.arange(R // 2)[None, None, :]
    n = np.concatenate([n2 + R * n1, (N - ((R - n2) % R + R * n1)) % N], axis=2)
    idx = (k1[None, :, None] * n) % N
    ang = 2.0 * np.pi * idx / N
    c, s = np.cos(ang), np.sin(ang)
    m1 = np.concatenate([c, -s], axis=1).astype(np.float32)
    m1i = np.concatenate([c.transpose(0, 2, 1), -s.transpose(0, 2, 1)], axis=2)
    m1i = m1i[:, :R // 2].astype(np.float32)
    a2 = 2.0 * np.pi * ((np.arange(R)[:, None] * np.arange(R)[None, :]) % R) / R
    fr, fi = np.cos(a2), -np.sin(a2)
    f2 = np.block([[fr, -fi], [fi, fr]]).astype(np.float32)
    g2 = (np.block([[fr, fi], [-fi, fr]]) / N).astype(np.float32)
    return m1, m1i, f2, g2


@functools.lru_cache(maxsize=None)
def _ctx_fft_mats():
    N = 2 * CTX_LEN
    k = np.arange(N)[:, None]
    n = np.arange(N)[None, :]
    ang = 2.0 * np.pi * ((k * n) % N) / N
    c, s = np.cos(ang), np.sin(ang)
    fwd = np.concatenate([c, -s], axis=0).astype(np.float32)
    inv = (np.concatenate([c, -s], axis=1)[:CTX_LEN] / N).astype(np.float32)
    src = np.concatenate([np.arange(CTX_LEN), (N - np.arange(CTX_LEN)) % N])
    return fwd, inv, fwd[:, src]


def _ada_kernel(c_ref, w_ref, b_ref, o_ref):
    c = c_ref[...]
    a = c * (1.0 / (1.0 + jnp.exp(-c)))
    o_ref[...] = _dot3(a, w_ref[...]) + b_ref[...]


def _ada(cvec, ada_w, ada_b):
    tn = 1536
    return pl.pallas_call(
        _ada_kernel,
        out_shape=jax.ShapeDtypeStruct((DEPTH, 8, N_MOD * D_MODEL), F32),
        grid=(DEPTH, N_MOD * D_MODEL // tn),
        in_specs=[
            pl.BlockSpec((8, D_MODEL), lambda i, j: (0, 0)),
            pl.BlockSpec((None, D_MODEL, tn), lambda i, j: (i, 0, j)),
            pl.BlockSpec((None, 1, tn), lambda i, j: (i, 0, j)),
        ],
        out_specs=pl.BlockSpec((None, 8, tn), lambda i, j: (i, 0, j)),
        compiler_params=_params(("parallel", "parallel"), VMEM_SMALL_MIB),
        name="ada_mod",
    )(cvec, ada_w, ada_b.reshape(DEPTH, 1, N_MOD * D_MODEL))


def _with_casts(body, n_in, n_out, n_cast):
    def kern(*refs):
        srcs = refs[n_in:n_in + n_cast]
        outs_at = n_in + n_cast
        dsts = refs[outs_at + n_out:outs_at + n_out + n_cast]
        for src, dst in zip(srcs, dsts):
            dst[...] = src[...].astype(dst.dtype)
        body(*refs[:n_in], *refs[outs_at:outs_at + n_out], *refs[outs_at + n_out + n_cast:])
    return kern


def _cast_jobs(stacked_weights, steps_per_row):
    in_specs, out_specs, out_shapes, args = [], [], [], []
    chunk = lambda b, i: jnp.minimum(b * steps_per_row + i, CAST_CHUNKS - 1)
    for w, lead in stacked_weights:
        _, rows, cols = w.shape
        blk = rows // CAST_CHUNKS
        in_specs.append(pl.BlockSpec((None, blk, cols), lambda b, i, lead=lead: (lead, chunk(b, i), 0)))
        out_specs.append(pl.BlockSpec((blk, cols), lambda b, i: (chunk(b, i), 0)))
        out_shapes.append(jax.ShapeDtypeStruct((rows, cols), BF16))
        args.append(w)
    return in_specs, out_specs, out_shapes, args


def _mod_row(ref, row):
    if isinstance(row, int):
        return ref[row:row + 1, :]
    return ref[pl.ds(row, 1), :]


def _norm_mod(x, g, shift, scale):
    y = x * lax.rsqrt(jnp.mean(x * x, axis=-1, keepdims=True) + EPS)
    return (y * g) * (1.0 + scale) + shift


def _rope_tile(x, cos, sin_signed):
    lane = lax.broadcasted_iota(jnp.int32, x.shape, 1)
    first = (lane % 32) < 16
    partner = jnp.where(first, pltpu.roll(x, LANES - 16, axis=1), pltpu.roll(x, 16, axis=1))
    return x * cos + partner * sin_signed


def _project0(x_ref, g, sh, sc, w_ref, cos_ref, sin_ref, u_ref, qkv_ref, rope):
    tn = 512
    n_rot = (ATT_WIDTH + 2 * LANES) // LANES
    sub = min(x_ref.shape[0], SUB_ROWS)
    for s in range(x_ref.shape[0] // sub):
        rows = slice(s * sub, (s + 1) * sub)
        urows = slice(CONV_PAD + s * sub, CONV_PAD + (s + 1) * sub)
        h = _norm_mod(x_ref[rows, :], g, sh, sc).astype(BF16)
        for j in range(EV_U // tn):
            u_ref[urows, j * tn:(j + 1) * tn] = jnp.dot(h, w_ref[:, j * tn:(j + 1) * tn],
                                                        preferred_element_type=F32)
        for j in range(EV_QKV // tn):
            y = jnp.dot(h, w_ref[:, EV_U + j * tn:EV_U + (j + 1) * tn], preferred_element_type=F32)
            for t in range(tn // LANES):
                yt = y[:, t * LANES:(t + 1) * LANES]
                if rope and j * (tn // LANES) + t < n_rot:
                    yt = _rope_tile(yt, cos_ref[rows, :], sin_ref[rows, :])
                qkv_ref[rows, j * tn + t * LANES:j * tn + (t + 1) * LANES] = yt.astype(BF16)


def _short_conv_gate(u_ref, cw_ref, cb_ref, bias_ref, z_ref, x0_ref, zb_ref):
    tm = u_ref.shape[0] - 2 * CONV_PAD
    rc = CONV_ROWS
    z_pitched = z_ref.shape[0] != tm

    def conv(r0, c0):
        cols = slice(c0, c0 + LANES)
        taps = [u_ref[CONV_PAD - 1 + k + r0:CONV_PAD - 1 + k + r0 + rc, cols] for k in range(3)]
        return (taps[0] * cw_ref[0:1, cols] + taps[1] * cw_ref[1:2, cols] + taps[2] * cw_ref[2:3, cols]
                + cb_ref[:, cols])

    for r0 in range(0, tm, rc):
        rows = slice(r0, r0 + rc)
        z0 = r0 // FFT_R * FFT_PITCH + r0 % FFT_R if z_pitched else r0
        for c0 in range(0, HY_WIDTH, LANES):
            cols = slice(c0, c0 + LANES)
            x0 = conv(r0, c0)
            z = conv(r0, 2 * HY_WIDTH + c0) * conv(r0, HY_WIDTH + c0)
            z_ref[z0:z0 + rc, cols] = z
            x0_ref[rows, cols] = x0
            zb_ref[rows, cols] = x0 * (z * bias_ref[:, cols])
    if z_pitched:
        for k in range(tm // FFT_R):
            z_ref[k * FFT_PITCH + FFT_R:(k + 1) * FFT_PITCH, :] = jnp.zeros((FFT_PITCH - FFT_R, HY_WIDTH), F32)


def _inproj0_kernel(x_ref, g_ref, sh_ref, sc_ref, w_ref, cos_ref, sin_ref, cw_ref, cb_ref, bias_ref,
                    z_ref, x0_ref, zb_ref, qkv_ref, u_even, u_odd, *, row, rope, n_tiles):
    r = pl.program_id(0) if row is None else row
    i = pl.program_id(1)
    g, sh, sc = g_ref[...], _mod_row(sh_ref, r), _mod_row(sc_ref, r)
    conv_refs = (cw_ref, cb_ref, bias_ref, z_ref, x0_ref, zb_ref)
    zero_row = jnp.zeros((1, EV_U), F32)
    tm = x_ref.shape[0]
    above, first, last, below = CONV_PAD - 1, CONV_PAD, CONV_PAD + tm - 1, CONV_PAD + tm

    if n_tiles == 1:
        u_even[above:first, :] = zero_row
        u_even[below:below + 1, :] = zero_row
        _project0(x_ref, g, sh, sc, w_ref, cos_ref, sin_ref, u_even, qkv_ref, rope)
        _short_conv_gate(u_even, *conv_refs)
        return

    @pl.when(i == 0)
    def _():
        u_even[...] = jnp.zeros_like(u_even)
        u_odd[...] = jnp.zeros_like(u_odd)

    def step(u_new, u_old):
        j = i - 1
        u_old[above:first, :] = jnp.where(j > 0, u_new[last:last + 1, :], zero_row)
        _project0(x_ref, g, sh, sc, w_ref, cos_ref, sin_ref, u_new, qkv_ref, rope)
        u_old[below:below + 1, :] = jnp.where(j < n_tiles - 1, u_new[first:first + 1, :], zero_row)
        _short_conv_gate(u_old, *conv_refs)

    @pl.when((i % 2 == 0) & (i < n_tiles))
    def _():
        step(u_even, u_odd)

    @pl.when((i % 2 == 1) & (i < n_tiles))
    def _():
        step(u_odd, u_even)

    @pl.when(i == n_tiles)
    def _():
        bufs = (u_even, u_odd)
        u_last, u_before = bufs[(n_tiles - 1) % 2], bufs[n_tiles % 2]
        u_last[above:first, :] = u_before[last:last + 1, :]
        u_last[below:below + 1, :] = zero_row
        _short_conv_gate(u_last, *conv_refs)


def _inproj0(x, g, mod, w, conv_w, conv_b, bias, layer, *, is_ctx, tm, cast=()):
    G, R, _ = x.shape
    n_tiles = R // tm
    n_steps = n_tiles + (1 if n_tiles > 1 else 0)
    cos, sin = _rope2d_tables()
    cos, sin = jnp.asarray(cos), jnp.asarray(sin)
    proj_tile = lambda i: jnp.minimum(i, n_tiles - 1)
    conv_tile = (lambda i: jnp.maximum(i - 1, 0)) if n_tiles > 1 else (lambda i: i)
    if is_ctx:
        tab = pl.BlockSpec((tm, LANES), lambda b, i: (0, 0))
    else:
        tab = pl.BlockSpec((tm, LANES), lambda b, i: (proj_tile(i), 0))
    const = lambda a: pl.BlockSpec(a.shape, lambda b, i: (0, 0))
    hy_out = jax.ShapeDtypeStruct((G, R, HY_WIDTH), F32)
    hy_spec = pl.BlockSpec((None, tm, HY_WIDTH), lambda b, i: (b, conv_tile(i), 0))
    if R == SEQ:
        z_out = jax.ShapeDtypeStruct((G, SEQ_P, HY_WIDTH), F32)
        z_spec = pl.BlockSpec((None, tm // FFT_R * FFT_PITCH, HY_WIDTH), lambda b, i: (b, conv_tile(i), 0))
    else:
        z_out, z_spec = hy_out, hy_spec
    kern = functools.partial(_inproj0_kernel, row=2 if is_ctx else None, rope=not is_ctx, n_tiles=n_tiles)
    in_specs = [
        pl.BlockSpec((None, tm, D_MODEL), lambda b, i: (b, proj_tile(i), 0)),
        pl.BlockSpec((1, D_MODEL), lambda b, i: (0, 0)),
        pl.BlockSpec((None, 8, D_MODEL), lambda b, i: (layer, 0, 0)),
        pl.BlockSpec((None, 8, D_MODEL), lambda b, i: (layer, 0, 1)),
        pl.BlockSpec((D_MODEL, EV_COLS), lambda b, i: (0, 0), pipeline_mode=pl.Buffered(1)),
        tab, tab, const(conv_w), const(conv_b), const(bias),
    ]
    out_specs = [z_spec, hy_spec, hy_spec, pl.BlockSpec((None, tm, EV_QKV), lambda b, i: (b, proj_tile(i), 0))]
    out_shape = [z_out, hy_out, hy_out, jax.ShapeDtypeStruct((G, R, EV_QKV), BF16)]
    c_in, c_out, c_shapes, c_args = _cast_jobs(cast, n_steps)
    if cast:
        assert G * n_steps >= CAST_CHUNKS
        kern = _with_casts(kern, len(in_specs), len(out_specs), len(cast))
    res = pl.pallas_call(
        kern,
        out_shape=out_shape + c_shapes,
        grid=(G, n_steps),
        in_specs=in_specs + c_in,
        out_specs=out_specs + c_out,
        scratch_shapes=[pltpu.VMEM((tm + 2 * CONV_PAD, EV_U), F32)] * 2,
        compiler_params=_params(("arbitrary", "arbitrary") if cast else ("parallel", "arbitrary"), VMEM_SMALL_MIB),
        name="inproj0_ctx" if is_ctx else "inproj0",
    )(x, g, mod, mod, w, cos, sin, conv_w, conv_b, bias, *c_args)
    return res[:4], res[4:]


def _inproj1_kernel(x_ref, g_ref, sh_ref, sc_ref, w_ref, o_ref, *, row, tn):
    r = pl.program_id(0) if row is None else row
    g, sh, sc = g_ref[...], _mod_row(sh_ref, r), _mod_row(sc_ref, r)
    sub = min(x_ref.shape[0], MIX_SUB_ROWS)
    for s in range(x_ref.shape[0] // sub):
        rows = slice(s * sub, (s + 1) * sub)
        h = _norm_mod(x_ref[rows, :], g, sh, sc).astype(BF16)
        for j in range(w_ref.shape[1] // tn):
            cols = slice(j * tn, (j + 1) * tn)
            o_ref[rows, cols] = jnp.dot(h, w_ref[:, cols], preferred_element_type=F32).astype(o_ref.dtype)


def _inproj1(x, g, mod, w, layer, *, is_ctx, tm, tn, cast=()):
    G, R, _ = x.shape
    N = w.shape[1]
    kern = functools.partial(_inproj1_kernel, row=2 if is_ctx else None, tn=tn)
    in_specs = [
        pl.BlockSpec((None, tm, D_MODEL), lambda b, i: (b, i, 0)),
        pl.BlockSpec((1, D_MODEL), lambda b, i: (0, 0)),
        pl.BlockSpec((None, 8, D_MODEL), lambda b, i: (layer, 0, 0)),
        pl.BlockSpec((None, 8, D_MODEL), lambda b, i: (layer, 0, 1)),
        pl.BlockSpec((D_MODEL, N), lambda b, i: (0, 0), pipeline_mode=pl.Buffered(1)),
    ]
    c_in, c_out, c_shapes, c_args = _cast_jobs(cast, R // tm)
    if cast:
        assert G * (R // tm) >= CAST_CHUNKS
        kern = _with_casts(kern, len(in_specs), 1, len(cast))
    res = pl.pallas_call(
        kern,
        out_shape=[jax.ShapeDtypeStruct((G, R, N), BF16)] + c_shapes,
        grid=(G, R // tm),
        in_specs=in_specs + c_in,
        out_specs=[pl.BlockSpec((None, tm, N), lambda b, i: (b, i, 0))] + c_out,
        compiler_params=_params(("arbitrary", "arbitrary") if cast else ("parallel", "parallel"), VMEM_SMALL_MIB),
        name="inproj1_ctx" if is_ctx else "inproj1",
    )(x, g, mod, mod, w, *c_args)
    return res[0], res[1:]


def _split_bf16(a):
    hi = a.astype(BF16)
    return hi, (a - hi.astype(F32)).astype(BF16)


def _dot3_split(a_hi, a_lo, b):
    b_hi, b_lo = _split_bf16(b)
    dot = functools.partial(jnp.dot, preferred_element_type=F32)
    return dot(a_hi, b_hi) + (dot(a_lo, b_hi) + dot(a_hi, b_lo))


def _dot3(a, b):
    return _dot3_split(*_split_bf16(a), b)


def _store_pitched(ref, lead, value):
    for k in range(value.shape[0] // FFT_R):
        ref[lead, k * FFT_PITCH:k * FFT_PITCH + FFT_R, :] = value[k * FFT_R:(k + 1) * FFT_R]
        ref[lead, k * FFT_PITCH + FFT_R:(k + 1) * FFT_PITCH, :] = jnp.zeros(
            (FFT_PITCH - FFT_R, value.shape[1]), value.dtype)


def _filter_kernel(w1t_ref, b1_ref, w2t_ref, b2_ref, fr_ref, w3_ref, dec_ref, bands_ref, h_ref, ss_ref, *,
                   L, tm, pitched):
    i = pl.program_id(0)
    t_row = (lax.broadcasted_iota(jnp.int32, (1, tm), 1) + i * tm).astype(F32) / L
    ang = (2.0 * math.pi * t_row) * bands_ref[...]
    t8 = jnp.where(lax.broadcasted_iota(jnp.int32, (8, tm), 0) == 0, t_row, 0.0)
    feat = jnp.concatenate([t8, jnp.cos(ang), -jnp.sin(ang)], axis=0)
    fr = fr_ref[...]
    hid = jnp.sin(fr * (jnp.dot(w1t_ref[...], feat, precision=HIGHEST, preferred_element_type=F32) + b1_ref[...]))
    hid = jnp.sin(fr * (jnp.dot(w2t_ref[...], hid, precision=HIGHEST, preferred_element_type=F32) + b2_ref[...]))
    hid_hi, hid_lo = _split_bf16(hid.T)
    m_col = lax.broadcasted_iota(jnp.int32, (tm, 1), 0) + i * tm
    t_col = m_col.astype(F32) / L
    energy = jnp.zeros((1, HY_WIDTH), F32)
    for side in range(2):
        h = _dot3_split(hid_hi, hid_lo, w3_ref[side]) * jnp.exp(-t_col * jnp.abs(dec_ref[side]))
        if side == 1:
            h = jnp.where(m_col == 0, 0.0, h)
        if pitched:
            _store_pitched(h_ref, side, h)
        else:
            h_ref[side] = h
        energy = energy + jnp.sum(h * h, axis=0, keepdims=True)

    @pl.when(i == 0)
    def _():
        ss_ref[...] = jnp.zeros_like(ss_ref)

    ss_ref[...] += jnp.broadcast_to(energy, ss_ref.shape)


def _hyena_filter(w1, b1, w2, b2, w3, freq, decay, *, L):
    tm = min(L, 1024)
    pitched = L == SEQ
    tm_out, l_out = (tm // FFT_R * FFT_PITCH, SEQ_P) if pitched else (tm, L)
    kern = functools.partial(_filter_kernel, L=L, tm=tm, pitched=pitched)
    w1t = jnp.concatenate([w1[0:1], jnp.zeros((7, HY_HIDDEN), F32), w1[1:]], axis=0).T
    col = lambda a: a.reshape(HY_HIDDEN, 1)
    w3s = jnp.stack([w3[:, :HY_WIDTH], w3[:, HY_WIDTH:]])
    bands = jnp.asarray(np.linspace(1e-4, HY_BANDS - 1, HY_BANDS).astype(np.float32).reshape(HY_BANDS, 1))
    args = (w1t, col(b1), w2.T, col(b2), col(freq), w3s, decay.reshape(2, 1, HY_WIDTH), bands)
    full = lambda a: pl.BlockSpec(a.shape, lambda i: (0,) * a.ndim)
    h, ss = pl.pallas_call(
        kern,
        out_shape=(jax.ShapeDtypeStruct((2, l_out, HY_WIDTH), F32), jax.ShapeDtypeStruct((8, HY_WIDTH), F32)),
        grid=(L // tm,),
        in_specs=[full(a) for a in args],
        out_specs=(pl.BlockSpec((2, tm_out, HY_WIDTH), lambda i: (0, i, 0)),
                   pl.BlockSpec((8, HY_WIDTH), lambda i: (0, 0))),
        compiler_params=_params(("arbitrary",), VMEM_SMALL_MIB),
        name="hyena_filter_%d" % L,
    )(*args)
    return h.reshape(2 * l_out, HY_WIDTH), ss


def _stack_bf16(re, im):
    return jnp.concatenate([re, im], axis=0).astype(BF16)


def _filter_fft_kernel(h_ref, ss_ref, m1_ref, f2_ref, o_ref, a_scr):
    R, P = FFT_R, FFT_PITCH
    inv_norm = lax.rsqrt(ss_ref[0:1, :])

    def stage1(n2, carry):
        neg = jnp.where(n2 == 0, 0, R - n2)
        xs = jnp.concatenate([h_ref[pl.ds(n2, R // 2, stride=P), :],
                              h_ref[pl.ds(SEQ_P + neg, R // 2, stride=P), :]], axis=0)
        xs = (xs * inv_norm).astype(BF16)
        a = jnp.dot(m1_ref[n2], xs, preferred_element_type=F32)
        base = pl.multiple_of(n2 * P, 8)
        a_scr[0, pl.ds(base, R), :] = a[:R]
        a_scr[1, pl.ds(base, R), :] = a[R:]
        return carry

    lax.fori_loop(0, R, stage1, 0, unroll=FFT_UNROLL)

    ct = a_scr.shape[-1]

    def stage2(pair, carry):
        k1s = (2 * pair, 2 * pair + 1)
        rhs = jnp.concatenate([_stack_bf16(a_scr[0, pl.ds(k1, R, stride=P), :],
                                           a_scr[1, pl.ds(k1, R, stride=P), :]) for k1 in k1s], axis=1)
        x = jnp.dot(f2_ref[...], rhs, preferred_element_type=F32)
        for d, k1 in enumerate(k1s):
            base = pl.multiple_of(k1 * R, R)
            o_ref[0, pl.ds(base, R), :] = x[:R, d * ct:(d + 1) * ct].astype(BF16)
            o_ref[1, pl.ds(base, R), :] = x[R:, d * ct:(d + 1) * ct].astype(BF16)
        return carry

    lax.fori_loop(0, R // 2, stage2, 0, unroll=FFT_UNROLL // 2)


def _filter_fft(h_circ, ss, m1, f2):
    ct = LANES
    once = pl.Buffered(1)
    return pl.pallas_call(
        _filter_fft_kernel,
        out_shape=jax.ShapeDtypeStruct((2, FFT_N, HY_WIDTH), BF16),
        grid=(HY_WIDTH // ct,),
        in_specs=[
            pl.BlockSpec((2 * SEQ_P, ct), lambda c: (0, c), pipeline_mode=once),
            pl.BlockSpec((8, ct), lambda c: (0, c)),
            pl.BlockSpec(m1.shape, lambda c: (0, 0, 0), pipeline_mode=once),
            pl.BlockSpec(f2.shape, lambda c: (0, 0), pipeline_mode=once),
        ],
        out_specs=pl.BlockSpec((2, FFT_N, ct), lambda c: (0, 0, c)),
        scratch_shapes=[pltpu.VMEM((2, FFT_R * FFT_PITCH, ct), F32)],
        compiler_params=_params(("parallel",), VMEM_FFT_MIB),
        name="hyena_filter_fft",
    )(h_circ, ss, m1, f2)


def _fft_conv_kernel(z_ref, hs_ref, m1_ref, f2_ref, g2_ref, m1i_ref, y_ref, a_scr):
    R, P = FFT_R, FFT_PITCH
    ct = a_scr.shape[-1]

    def stage1(n2, carry):
        rows = pl.ds(n2, R // 2, stride=P)
        xs = jnp.concatenate([z_ref[0, rows, :], z_ref[1, rows, :]], axis=1).astype(BF16)
        t = jnp.dot(m1_ref[n2, :, 0:R // 2], xs, preferred_element_type=F32)
        base = pl.multiple_of(n2 * P, 8)
        a_scr[0, pl.ds(base, R), :] = t[:R, :ct] - t[R:, ct:]
        a_scr[1, pl.ds(base, R), :] = t[R:, :ct] + t[:R, ct:]
        return carry

    lax.fori_loop(0, R, stage1, 0, unroll=FFT_UNROLL)

    def stage2(pair, carry):
        k1s = (2 * pair, 2 * pair + 1)
        rhs = jnp.concatenate([_stack_bf16(a_scr[0, pl.ds(k1, R, stride=P), :],
                                           a_scr[1, pl.ds(k1, R, stride=P), :]) for k1 in k1s], axis=1)
        x = jnp.dot(f2_ref[...], rhs, preferred_element_type=F32)
        prods = []
        for d, k1 in enumerate(k1s):
            base = pl.multiple_of(k1 * R, R)
            hr = hs_ref[0, pl.ds(base, R), :].astype(F32)
            hi = hs_ref[1, pl.ds(base, R), :].astype(F32)
            xr, xi = x[:R, d * ct:(d + 1) * ct], x[R:, d * ct:(d + 1) * ct]
            prods.append(_stack_bf16(xr * hr - xi * hi, xr * hi + xi * hr))
        bq = jnp.dot(g2_ref[...], jnp.concatenate(prods, axis=1), preferred_element_type=F32)
        for d, k1 in enumerate(k1s):
            a_scr[0, pl.ds(k1, R, stride=P), :] = bq[:R, d * ct:(d + 1) * ct]
            a_scr[1, pl.ds(k1, R, stride=P), :] = bq[R:, d * ct:(d + 1) * ct]
        return carry

    lax.fori_loop(0, R // 2, stage2, 0, unroll=FFT_UNROLL // 2)

    def stage3(n2, carry):
        base = pl.multiple_of(n2 * P, 8)
        br = a_scr[0, pl.ds(base, R), :]
        bi = a_scr[1, pl.ds(base, R), :]
        rhs = jnp.concatenate([jnp.concatenate([br, bi], axis=1),
                               jnp.concatenate([bi, -br], axis=1)], axis=0).astype(BF16)
        y = jnp.dot(m1i_ref[n2], rhs, preferred_element_type=F32)
        rows = pl.ds(n2, R // 2, stride=P)
        y_ref[0, rows, :] = y[:, :ct]
        y_ref[1, rows, :] = y[:, ct:]
        return carry

    lax.fori_loop(0, R, stage3, 0, unroll=FFT_UNROLL)
    for b in range(2):
        for n1 in range(R // 2):
            y_ref[b, n1 * P + R:(n1 + 1) * P, :] = jnp.zeros((P - R, ct), F32)


def _fft_conv(z, hspec, m1, f2, g2, m1i):
    assert z.shape[0] == 2, "the batch pair is packed into one complex signal"
    ct = LANES
    once = pl.Buffered(1)
    return pl.pallas_call(
        _fft_conv_kernel,
        out_shape=jax.ShapeDtypeStruct((2, SEQ_P, HY_WIDTH), F32),
        grid=(HY_WIDTH // ct,),
        in_specs=[
            pl.BlockSpec((2, SEQ_P, ct), lambda c: (0, 0, c), pipeline_mode=once),
            pl.BlockSpec((2, FFT_N, ct), lambda c: (0, 0, c), pipeline_mode=once),
            pl.BlockSpec(m1.shape, lambda c: (0, 0, 0), pipeline_mode=once),
            pl.BlockSpec(f2.shape, lambda c: (0, 0), pipeline_mode=once),
            pl.BlockSpec(g2.shape, lambda c: (0, 0), pipeline_mode=once),
            pl.BlockSpec(m1i.shape, lambda c: (0, 0, 0), pipeline_mode=once),
        ],
        out_specs=pl.BlockSpec((2, SEQ_P, ct), lambda c: (0, 0, c), pipeline_mode=once),
        scratch_shapes=[pltpu.VMEM((2, FFT_R * FFT_PITCH, ct), F32)],
        compiler_params=_params(("arbitrary",), VMEM_FFT_MIB),
        name="hyena_fft_conv",
    )(z, hspec, m1, f2, g2, m1i)


def _ctx_conv_kernel(z_ref, h_ref, ss_ref, fwd_ref, inv_ref, fwd_h_ref, y_ref):
    N = 2 * CTX_LEN
    hn = (h_ref[...] * lax.rsqrt(ss_ref[0:1, :])).astype(BF16)
    hs = jnp.dot(fwd_h_ref[...], hn, preferred_element_type=F32)
    zs = jnp.dot(fwd_ref[:, :CTX_LEN], z_ref[...].astype(BF16), preferred_element_type=F32)
    hr, hi, zr, zi = hs[:N], hs[N:], zs[:N], zs[N:]
    ys = _stack_bf16(zr * hr - zi * hi, zr * hi + zi * hr)
    y_ref[...] = jnp.dot(inv_ref[...], ys, preferred_element_type=F32)


def _ctx_conv(z, h_circ, ss, fwd, inv, fwd_h):
    G = z.shape[0]
    full = lambda a: pl.BlockSpec(a.shape, lambda b: (0,) * a.ndim)
    return pl.pallas_call(
        _ctx_conv_kernel,
        out_shape=jax.ShapeDtypeStruct((G, CTX_LEN, HY_WIDTH), F32),
        grid=(G,),
        in_specs=[pl.BlockSpec((None, CTX_LEN, HY_WIDTH), lambda b: (b, 0, 0)),
                  full(h_circ), full(ss), full(fwd), full(inv), full(fwd_h)],
        out_specs=pl.BlockSpec((None, CTX_LEN, HY_WIDTH), lambda b: (b, 0, 0)),
        compiler_params=_params(("parallel",), VMEM_SMALL_MIB),
        name="hyena_ctx_conv",
    )(z, h_circ, ss, fwd, inv, fwd_h)


def _kv_group(refs, g, col0):
    cols = slice(col0 + g * LANES, col0 + (g + 1) * LANES)
    return refs[0][:, cols] if len(refs) == 1 else jnp.concatenate([r[:, cols] for r in refs], axis=0)


def _attn_logits(q_ref, k_refs, g, masks, col0=0):
    low = lax.broadcasted_iota(jnp.int32, (BLOCK, LANES), 1) < HEAD_DIM
    zero = jnp.zeros((), BF16)
    parts = []
    for hh in range(ATT_GROUP):
        h = g * ATT_GROUP + hh
        tile = q_ref[:, (h // 2) * LANES:(h // 2 + 1) * LANES]
        parts.append(jnp.where(low if h % 2 == 0 else ~low, tile, zero))
    s = lax.dot_general(jnp.concatenate(parts, axis=0), _kv_group(k_refs, g, col0), (((1,), (1,)), ((), ())),
                        preferred_element_type=F32)
    if masks is not None:
        ok_prev, ok_next = masks
        s = jnp.concatenate([jnp.where(ok_prev, s[:, :BLOCK], NEG_INF), s[:, BLOCK:2 * BLOCK],
                             jnp.where(ok_next, s[:, 2 * BLOCK:3 * BLOCK], NEG_INF), s[:, 3 * BLOCK:]], axis=1)
    return s


def _attn_output(s, sink_ref, v_refs, g, o_ref, col0=0):
    low = lax.broadcasted_iota(jnp.int32, (BLOCK, LANES), 1) < HEAD_DIM
    low4 = jnp.concatenate([low] * ATT_GROUP, axis=0)
    v = _kv_group(v_refs, g, col0)
    v_aug = jnp.where(lax.broadcasted_iota(jnp.int32, v.shape, 1) < HEAD_DIM, v, jnp.ones((), BF16))
    sink = jnp.concatenate(
        [jnp.full((BLOCK, 1), sink_ref[g * ATT_GROUP + hh] * LOG2E, F32) for hh in range(ATT_GROUP)], axis=0)
    m = jnp.maximum(jnp.max(s, axis=-1, keepdims=True), sink)
    e = jnp.exp2(s - m).astype(BF16)
    o = jnp.dot(e, v_aug, preferred_element_type=F32) + jnp.where(low4, 0.0, jnp.exp2(sink - m))
    swapped = pltpu.roll(o, HEAD_DIM, axis=1)
    for pair in range(ATT_GROUP // 2):
        ev = slice((2 * pair) * BLOCK, (2 * pair + 1) * BLOCK)
        od = slice((2 * pair + 1) * BLOCK, (2 * pair + 2) * BLOCK)
        even = o[ev] / swapped[ev]
        odd = swapped[od] / o[od]
        t = g * (ATT_GROUP // 2) + pair
        o_ref[:, t * LANES:(t + 1) * LANES] = jnp.where(low, even, odd).astype(o_ref.dtype)


def _attn_ctx_kernel(sink_ref, q_ref, kx_ref, vx_ref, o_ref):
    for g in range(ATT_KV_HEADS):
        _attn_output(_attn_logits(q_ref, (kx_ref,), g, None), sink_ref, (vx_ref,), g, o_ref)


def _attn_local_kernel(sink_ref, q_ref, kv_m2, kv_m1, kv_0, kv_p1, kv_ctx, o_ref, s_even, s_odd, *, n_blocks):
    n = pl.program_id(1)
    k_refs, v_refs, v_col = (kv_m1, kv_0, kv_p1, kv_ctx), (kv_m2, kv_m1, kv_0, kv_ctx), 2 * LANES

    @pl.when(n == 0)
    def _():
        s_odd[...] = jnp.zeros_like(s_odd)

    def step(s_new, s_prev):
        nq = jnp.minimum(n, n_blocks - 1)
        qi = lax.broadcasted_iota(jnp.int32, (ATT_GROUP * BLOCK, BLOCK), 0) % BLOCK
        ki = lax.broadcasted_iota(jnp.int32, (ATT_GROUP * BLOCK, BLOCK), 1)
        masks = ((ki >= qi) & (nq > 0), (ki <= qi) & (nq < n_blocks - 1))
        for g in range(ATT_KV_HEADS):
            s_new[g] = _attn_logits(q_ref, k_refs, g, masks)
        for g in range(ATT_KV_HEADS):
            _attn_output(s_prev[g], sink_ref, v_refs, g, o_ref, v_col)

    @pl.when(n % 2 == 0)
    def _():
        step(s_even, s_odd)

    @pl.when(n % 2 == 1)
    def _():
        step(s_odd, s_even)


def _attention(sink, qkv, qkv_ctx, *, local):
    src = qkv if local else qkv_ctx
    G, R, _ = src.shape
    nb = R // BLOCK
    kcol, vcol = ATT_WIDTH // 256, ATT_WIDTH // 256 + 1
    ctx_k = pl.BlockSpec((None, CTX_LEN, 256), lambda b, n: (b, 0, kcol))
    ctx_v = pl.BlockSpec((None, CTX_LEN, 256), lambda b, n: (b, 0, vcol))
    smem = pl.BlockSpec(memory_space=pltpu.SMEM)
    out_shape = jax.ShapeDtypeStruct((G, R, ATT_WIDTH), BF16)
    if not local:
        rows = pl.BlockSpec((None, BLOCK, ATT_WIDTH), lambda b, n: (b, n, 0))
        return pl.pallas_call(
            _attn_ctx_kernel, out_shape=out_shape, grid=(G, nb),
            in_specs=[smem, rows, ctx_k, ctx_v], out_specs=rows,
            compiler_params=_params(("parallel", "parallel"), VMEM_SMALL_MIB), name="attn_ctx",
        )(sink, qkv_ctx, qkv_ctx, qkv_ctx)

    q_blk = lambda n: jnp.minimum(n, nb - 1)
    o_blk = lambda n: jnp.maximum(n - 1, 0)

    kv_w = EV_QKV - ATT_WIDTH
    kv_col = ATT_WIDTH // kv_w

    def kv_rows(off):
        return pl.BlockSpec((None, BLOCK, kv_w), lambda b, n: (b, jnp.clip(n + off, 0, nb - 1), kv_col))

    in_specs = [smem, pl.BlockSpec((None, BLOCK, ATT_WIDTH), lambda b, n: (b, q_blk(n), 0)),
                kv_rows(-2), kv_rows(-1), kv_rows(0), kv_rows(1),
                pl.BlockSpec((None, CTX_LEN, kv_w), lambda b, n: (b, 0, kv_col))]
    logits_scratch = pltpu.VMEM((ATT_KV_HEADS, ATT_GROUP * BLOCK, 3 * BLOCK + CTX_LEN), F32)
    return pl.pallas_call(
        functools.partial(_attn_local_kernel, n_blocks=nb),
        out_shape=out_shape,
        grid=(G, nb + 1),
        in_specs=in_specs,
        out_specs=pl.BlockSpec((None, BLOCK, ATT_WIDTH), lambda b, n: (b, o_blk(n), 0)),
        scratch_shapes=[logits_scratch, logits_scratch],
        compiler_params=_params(("parallel", "arbitrary"), VMEM_SMALL_MIB),
        name="attn_local",
    )(sink, qkv, qkv, qkv, qkv, qkv, qkv_ctx)


def _mixer_out0(rows, x0_ref, y_ref, zb_ref, att_ref, wo_ref):
    if y_ref.shape[0] != x0_ref.shape[0]:
        y = jnp.concatenate([y_ref[r // FFT_R * FFT_PITCH:r // FFT_R * FFT_PITCH + FFT_R, :]
                             for r in range(rows.start, rows.stop, FFT_R)], axis=0)
    else:
        y = y_ref[rows, :]
    hy = (x0_ref[rows, :] * y + zb_ref[rows, :]).astype(BF16)
    return (jnp.dot(hy, wo_ref[:HY_WIDTH, :], preferred_element_type=F32)
            + jnp.dot(att_ref[rows, :], wo_ref[HY_WIDTH:, :], preferred_element_type=F32))


def _mixer_out1(rows, of_ref, ob_ref, wo_ref):
    a = (of_ref[rows, :].astype(F32) + ob_ref[rows, :].astype(F32)).astype(BF16)
    return jnp.dot(a, wo_ref[...], preferred_element_type=F32)


def _mix_kernel(*refs, mixer_out, n_mix, row, fc, final_norm):
    x_ref = refs[0]
    mix_refs = refs[1:2 + n_mix]
    gta_ref, gm_ref, shm_ref, scm_ref, gtm_ref, w1_ref, w2_ref, fg_ref, o_ref, a_scr = refs[2 + n_mix:]
    r = pl.program_id(0) if row is None else row
    gta, gtm = _mod_row(gta_ref, r), _mod_row(gtm_ref, r)
    gm, shm, scm = gm_ref[...], _mod_row(shm_ref, r), _mod_row(scm_ref, r)
    sub = min(x_ref.shape[0], MIX_SUB_ROWS)
    for s in range(x_ref.shape[0] // sub):
        rows = slice(s * sub, (s + 1) * sub)
        x1 = x_ref[rows, :] + gta * mixer_out(rows, *mix_refs)
        h = _norm_mod(x1, gm, shm, scm).astype(BF16)
        for c in range(D_FF // fc):
            a = jnp.maximum(jnp.dot(h, w1_ref[:, c * fc:(c + 1) * fc], preferred_element_type=F32), 0.0)
            a_scr[rows, c * fc:(c + 1) * fc] = (a * a).astype(BF16)
        out = x1 + gtm * jnp.dot(a_scr[rows, :], w2_ref[...], preferred_element_type=F32)
        if final_norm:
            out = (out * lax.rsqrt(jnp.mean(out * out, axis=-1, keepdims=True) + EPS)) * fg_ref[...]
        o_ref[rows, :] = out


def _mix_mlp(kind, x, mix_in, wo, mod, gm, w1, w2, fg, layer, *, is_ctx, tm, fc, final_norm, cast=()):
    G, R, _ = x.shape
    row_spec = lambda w: pl.BlockSpec((None, tm, w), lambda b, i: (b, i, 0))
    modk = lambda k: pl.BlockSpec((None, 8, D_MODEL), lambda b, i: (layer, 0, k))
    vec = pl.BlockSpec((1, D_MODEL), lambda b, i: (0, 0))
    resident = lambda a: pl.BlockSpec(a.shape, lambda b, i: (0, 0), pipeline_mode=pl.Buffered(1))
    if kind == 0:
        mixer_out = _mixer_out0
        y_rows = tm if mix_in[1].shape[1] == R else tm // FFT_R * FFT_PITCH
        y_spec = pl.BlockSpec((None, y_rows, HY_WIDTH), lambda b, i: (b, i, 0))
        mix_specs = [row_spec(HY_WIDTH), y_spec, row_spec(HY_WIDTH), row_spec(ATT_WIDTH)]
    else:
        mixer_out = _mixer_out1
        mix_specs = [row_spec(RET_V)] * 2
    kern = functools.partial(_mix_kernel, mixer_out=mixer_out, n_mix=len(mix_in), row=2 if is_ctx else None,
                             fc=fc, final_norm=final_norm)
    in_specs = [row_spec(D_MODEL)] + mix_specs + [
        resident(wo), modk(2), vec, modk(3), modk(4), modk(5), resident(w1), resident(w2), vec]
    c_in, c_out, c_shapes, c_args = _cast_jobs(cast, R // tm)
    if cast:
        assert G * (R // tm) >= CAST_CHUNKS
        kern = _with_casts(kern, len(in_specs), 1, len(cast))
    res = pl.pallas_call(
        kern,
        out_shape=[jax.ShapeDtypeStruct((G, R, D_MODEL), F32)] + c_shapes,
        grid=(G, R // tm),
        in_specs=in_specs + c_in,
        out_specs=[row_spec(D_MODEL)] + c_out,
        scratch_shapes=[pltpu.VMEM((tm, D_FF), BF16)],
        compiler_params=_params(("arbitrary", "arbitrary") if cast else ("parallel", "parallel"), VMEM_DENSE_MIB),
        name="mix_mlp%d%s" % (kind, "_ctx" if is_ctx else ""),
    )(x, *mix_in, wo, mod, gm, mod, mod, mod, w1, w2, fg, *c_args)
    return res[0], res[1:]


def _ret_kernel(lr_ref, qkv_c, qkv_f, g_f, qkv_b, g_b, cos_ref, sin_ref, of_ref, ob_ref,
                state, dmask, xi, zeta, gch):
    j = pl.program_id(1)
    C = RET_C
    kscale = RET_DK ** -0.5

    @pl.when(j == 0)
    def _():
        state[...] = jnp.zeros_like(state)
        row = lax.broadcasted_iota(jnp.int32, (C, C), 0).astype(F32)
        col = lax.broadcasted_iota(jnp.int32, (C, C), 1).astype(F32)
        for d in range(2):
            e = row - col if d == 0 else col - row
            p = row if d == 0 else (C - 1) - row
            for h in range(RET_HEADS):
                lg = -jnp.exp(jnp.full((C, C), lr_ref[d, h], F32))
                dmask[d, h] = jnp.where(e >= 0, jnp.exp(e * lg), 0.0) * kscale
                xi[d, h] = jnp.exp((p + 1.0) * lg)
                zeta[d, h] = jnp.exp(((C - 1) - p) * lg) * kscale
                gch[d, h] = jnp.exp(C * -jnp.exp(jnp.full((8, C), lr_ref[d, h], F32)))

    half = RET_DK // 2

    def chain(d, h, src_ref, g_ref, o_ref):
        cos, sin = cos_ref[d], sin_ref[d]

        def rot(c0):
            t1 = src_ref[:, c0:c0 + half].astype(F32)
            t2 = src_ref[:, c0 + half:c0 + RET_DK].astype(F32)
            return jnp.concatenate([t1 * cos - t2 * sin, t2 * cos + t1 * sin], axis=1)

        q = rot(h * RET_DK)
        k = rot(RET_QK + h * RET_DK)
        v = src_ref[:, 2 * RET_QK + h * RET_DV:2 * RET_QK + (h + 1) * RET_DV]
        inner = lax.dot_general(q.astype(BF16), k.astype(BF16), (((1,), (1,)), ((), ())),
                                preferred_element_type=F32) * dmask[d, h]
        s_old = state[d, h]
        kv = lax.dot_general((k * zeta[d, h]).astype(BF16), v, (((0,), (0,)), ((), ())),
                             preferred_element_type=F32)
        state[d, h] = gch[d, h, 0:1, 0:1] * s_old + kv
        if o_ref is not None:
            o = (jnp.dot(inner.astype(BF16), v, preferred_element_type=F32)
                 + jnp.dot((q * xi[d, h]).astype(BF16), s_old.astype(BF16), preferred_element_type=F32))
            on = o * lax.rsqrt(jnp.mean(o * o, axis=-1, keepdims=True) + EPS)
            gate = g_ref[:, h * RET_DV:(h + 1) * RET_DV]
            half_gate = gate * 0.5
            silu = half_gate + half_gate * jnp.tanh(half_gate)
            o_ref[:, h * RET_DV:(h + 1) * RET_DV] = (silu.astype(F32) * on).astype(o_ref.dtype)

    @pl.when(j == 0)
    def _():
        for d in range(2):
            for h in range(RET_HEADS):
                chain(d, h, qkv_c, None, None)

    @pl.when(j > 0)
    def _():
        for h in range(RET_HEADS):
            chain(0, h, qkv_f, g_f, of_ref)
            chain(1, h, qkv_b, g_b, ob_ref)


def _retention(log_rate, p_ctx, p_lat):
    G = p_lat.shape[0]
    C = RET_C
    n_lat = SEQ // C
    cos, sin = _rope1d_tables()
    cos, sin = jnp.asarray(cos), jnp.asarray(sin)

    def fwd_chunk(j):
        return jnp.maximum(j - 1, 0)

    def bwd_chunk(j):
        return n_lat - 1 - jnp.maximum(j - 1, 0)

    qkv_w = 2 * RET_QK + RET_V
    gcol = qkv_w // RET_V
    tab = pl.BlockSpec((2, C, RET_DK // 2), lambda b, j: (0, j, 0))
    in_specs = [
        pl.BlockSpec(memory_space=pltpu.SMEM),
        pl.BlockSpec((None, C, qkv_w), lambda b, j: (b, 0, 0)),
        pl.BlockSpec((None, C, qkv_w), lambda b, j: (b, fwd_chunk(j), 0)),
        pl.BlockSpec((None, C, RET_V), lambda b, j: (b, fwd_chunk(j), gcol)),
        pl.BlockSpec((None, C, qkv_w), lambda b, j: (b, bwd_chunk(j), 0)),
        pl.BlockSpec((None, C, RET_V), lambda b, j: (b, bwd_chunk(j), gcol + 1)),
        tab, tab,
    ]
    out = jax.ShapeDtypeStruct((G, SEQ, RET_V), BF16)
    per_chain = lambda *tail: pltpu.VMEM((2, RET_HEADS) + tail, F32)
    return pl.pallas_call(
        _ret_kernel,
        out_shape=(out, out),
        grid=(G, 1 + n_lat),
        in_specs=in_specs,
        out_specs=(pl.BlockSpec((None, C, RET_V), lambda b, j: (b, fwd_chunk(j), 0)),
                   pl.BlockSpec((None, C, RET_V), lambda b, j: (b, bwd_chunk(j), 0))),
        scratch_shapes=[per_chain(RET_DK, RET_DV), per_chain(C, C), per_chain(C, C), per_chain(C, C),
                        per_chain(8, C)],
        compiler_params=_params(("parallel", "arbitrary"), VMEM_SMALL_MIB),
        name="retention",
    )(log_rate, p_ctx, p_lat, p_lat, p_lat, p_lat, cos, sin)


def _ev_weight(w_in):
    i_q = EV_U
    i_k = i_q + ATT_WIDTH
    i_v = i_k + KV_WIDTH
    cols = [w_in[:, :i_q], w_in[:, i_q:i_k] * (HEAD_DIM ** -0.5 * LOG2E)]
    for base in (i_k, i_v):
        for g in range(ATT_KV_HEADS):
            part = w_in[:, base + g * HEAD_DIM: base + (g + 1) * HEAD_DIM]
            cols += [part, part]
    return jnp.concatenate([c.astype(BF16) for c in cols], axis=1)


def kernel(x, c, ctx, c_ctx, ada_w, ada_b, norm_mix_g, norm_mlp_g, mlp_w1, mlp_w2, ev_w_in, ev_w_out, hy_conv_w, hy_conv_b, hy_w1, hy_b1, hy_w2, hy_b2, hy_w3, hy_freq, hy_decay, hy_bias, attn_sink, od_w_in, od_w_out, ret_log_rate, final_g):
    D = D_MODEL
    cvec = jnp.concatenate([c, c_ctx[None, :], jnp.zeros((8 - BATCH - 1, D), F32)], axis=0)
    mod = _ada(cvec, ada_w, ada_b)

    m1, m1i, f2, g2 = (jnp.asarray(a).astype(BF16) for a in _fft_mats())
    cfwd, cinv, cfwd_h = (jnp.asarray(a).astype(BF16) for a in _ctx_fft_mats())
    fg = final_g.reshape(1, D)

    gmix = norm_mix_g[0].reshape(1, D)
    gmlp = norm_mlp_g[0].reshape(1, D)
    w_in = _ev_weight(ev_w_in[0])
    conv = (hy_conv_w[0], hy_conv_b[0].reshape(1, EV_U), hy_bias[0].reshape(1, HY_WIDTH))
    (z_l, x0_l, zb_l, qkv_l), (wo, w1, w2) = _inproj0(
        x, gmix, mod, w_in, *conv, 0, is_ctx=False, tm=512, cast=((ev_w_out, 0), (mlp_w1, 0), (mlp_w2, 0)))
    (z_c, x0_c, zb_c, qkv_c), _ = _inproj0(ctx, gmix, mod, w_in, *conv, 0, is_ctx=True, tm=CTX_LEN)

    filt = (hy_w1[0], hy_b1[0], hy_w2[0], hy_b2[0], hy_w3[0], hy_freq[0], hy_decay[0])
    h_l, ss_l = _hyena_filter(*filt, L=SEQ)
    h_c, ss_c = _hyena_filter(*filt, L=CTX_LEN)
    hspec = _filter_fft(h_l, ss_l, m1, f2)
    y_l = _fft_conv(z_l, hspec, m1, f2, g2, m1i)
    y_c = _ctx_conv(z_c, h_c, ss_c, cfwd, cinv, cfwd_h)

    sink = attn_sink[0]
    att_l = _attention(sink, qkv_l, qkv_c, local=True)
    att_c = _attention(sink, qkv_l, qkv_c, local=False)

    x1, (w_in,) = _mix_mlp(0, x, (x0_l, y_l, zb_l, att_l), wo, mod, gmlp, w1, w2, fg, 0,
                           is_ctx=False, tm=512, fc=512, final_norm=False, cast=((od_w_in, 0),))
    ctx1, _ = _mix_mlp(0, ctx, (x0_c, y_c, zb_c, att_c), wo, mod, gmlp, w1, w2, fg, 0,
                       is_ctx=True, tm=CTX_LEN, fc=512, final_norm=False)

    gmix = norm_mix_g[1].reshape(1, D)
    gmlp = norm_mlp_g[1].reshape(1, D)
    p_l, (wo, w1, w2) = _inproj1(x1, gmix, mod, w_in, 1, is_ctx=False, tm=512, tn=512,
                                 cast=((od_w_out, 0), (mlp_w1, 1), (mlp_w2, 1)))
    p_c, _ = _inproj1(ctx1, gmix, mod, w_in, 1, is_ctx=True, tm=CTX_LEN, tn=512)
    o_fwd, o_bwd = _retention(ret_log_rate[0], p_c, p_l)

    out, _ = _mix_mlp(1, x1, (o_fwd, o_bwd), wo, mod, gmlp, w1, w2, fg, 1,
                      is_ctx=False, tm=512, fc=512, final_norm=True)
    return out
```

```python
import functools
import math

import numpy as np
import jax
import jax.numpy as jnp
from jax import lax
from jax.experimental import pallas as pl
from jax.experimental.pallas import tpu as pltpu

F32 = jnp.float32
BF16 = jnp.bfloat16
HIGHEST = lax.Precision.HIGHEST

D_MODEL = 1024
BATCH = 2
SEQ = 8192
DEPTH = 2
GRID_W = 64
CTX_LEN = 256
EPS = 1e-6
NEG_INF = -1e30
N_MOD = 6
D_FF = 4 * D_MODEL
ROPE_BASE = 10000.0

HY_WIDTH = D_MODEL // 2
HY_EMB = 33
HY_BANDS = (HY_EMB - 1) // 2
HY_HIDDEN = 64

ATT_HEADS = 8
ATT_KV_HEADS = 2
ATT_GROUP = ATT_HEADS // ATT_KV_HEADS
HEAD_DIM = 64
ATT_WIDTH = ATT_HEADS * HEAD_DIM
KV_WIDTH = ATT_KV_HEADS * HEAD_DIM
BLOCK = 128

RET_HEADS = 4
RET_DK = D_MODEL // RET_HEADS
RET_DV = 2 * RET_DK
RET_QK = RET_HEADS * RET_DK
RET_V = RET_HEADS * RET_DV
OD_IN = 2 * RET_QK + 3 * RET_V

LOG2E = 1.4426950408889634
LANES = 128
MIB = 1024 * 1024

EV_U = 3 * HY_WIDTH
EV_QKV = ATT_WIDTH + 4 * LANES
EV_COLS = EV_U + EV_QKV

FFT_N = 2 * SEQ
FFT_R = 128
FFT_PITCH = FFT_R + 8
SEQ_P = SEQ // FFT_R * FFT_PITCH
CONV_ROWS = 64
CONV_PAD = 8
SUB_ROWS = 256
MIX_SUB_ROWS = 512
FFT_UNROLL = 64
RET_C = 256
CAST_CHUNKS = 32


VMEM_SMALL_MIB = 48
VMEM_DENSE_MIB = 56
VMEM_FFT_MIB = 60


def _params(sem, vmem_mib):
    return pltpu.CompilerParams(dimension_semantics=sem, vmem_limit_bytes=vmem_mib * MIB)


@functools.lru_cache(maxsize=None)
def _rope2d_tables():
    quarter = HEAD_DIM // 4
    inv = ROPE_BASE ** (-np.arange(quarter, dtype=np.float64) / quarter)
    t = np.arange(SEQ)
    pos = np.stack([t // GRID_W, t % GRID_W], axis=1).astype(np.float64)
    lane = np.arange(HEAD_DIM)
    half = lane // (HEAD_DIM // 2)
    e = lane % (HEAD_DIM // 2)
    ang = pos[:, half] * inv[e % quarter][None, :]
    sign = np.where(e < quarter, -1.0, 1.0)[None, :]
    cos = np.tile(np.cos(ang), (1, 2)).astype(np.float32)
    sin = np.tile(np.sin(ang) * sign, (1, 2)).astype(np.float32)
    return cos, sin


@functools.lru_cache(maxsize=None)
def _rope1d_tables():
    n = RET_DK // 2
    inv = ROPE_BASE ** (-np.linspace(0.0, 1.0, n))
    pos = np.arange(CTX_LEN + SEQ, dtype=np.float64)
    ang = pos[:, None] * inv[None, :]
    cos, sin = np.cos(ang), np.sin(ang)

    def rev(a):
        return a.reshape(-1, RET_C, n)[:, ::-1].reshape(-1, n)

    cos2 = np.stack([cos, rev(cos)]).astype(np.float32)
    sin2 = np.stack([sin, rev(sin)]).astype(np.float32)
    return cos2, sin2


@functools.lru_cache(maxsize=None)
def _fft_mats():
    N, R = FFT_N, FFT_R
    k1 = np.arange(R)
    n2 = np.arange(R)[:, None, None]
    n1 = np.arange(R // 2)[None, None, :]
    n = np.concatenate([n2 + R * n1, (N - ((R - n2) % R + R * n1)) % N], axis=2)
    idx = (k1[None, :, None] * n) % N
    ang = 2.0 * np.pi * idx / N
    c, s = np.cos(ang), np.sin(ang)
    m1 = np.concatenate([c, -s], axis=1).astype(np.float32)
    m1i = np.concatenate([c.transpose(0, 2, 1), -s.transpose(0, 2, 1)], axis=2)
    m1i = m1i[:, :R // 2].astype(np.float32)
    a2 = 2.0 * np.pi * ((np.arange(R)[:, None] * np.arange(R)[None, :]) % R) / R
    fr, fi = np.cos(a2), -np.sin(a2)
    f2 = np.block([[fr, -fi], [fi, fr]]).astype(np.float32)
    g2 = (np.block([[fr, fi], [-fi, fr]]) / N).astype(np.float32)
    return m1, m1i, f2, g2


@functools.lru_cache(maxsize=None)
def _ctx_fft_mats():
    N = 2 * CTX_LEN
    k = np.arange(N)[:, None]
    n = np.arange(N)[None, :]
    ang = 2.0 * np.pi * ((k * n) % N) / N
    c, s = np.cos(ang), np.sin(ang)
    fwd = np.concatenate([c, -s], axis=0).astype(np.float32)
    inv = (np.concatenate([c, -s], axis=1)[:CTX_LEN] / N).astype(np.float32)
    src = np.concatenate([np.arange(CTX_LEN), (N - np.arange(CTX_LEN)) % N])
    return fwd, inv, fwd[:, src]


def _ada_kernel(c_ref, w_ref, b_ref, o_ref):
    c = c_ref[...]
    a = c * (1.0 / (1.0 + jnp.exp(-c)))
    o_ref[...] = _dot3(a, w_ref[...]) + b_ref[...]


def _ada(cvec, ada_w, ada_b):
    tn = 1536
    return pl.pallas_call(
        _ada_kernel,
        out_shape=jax.ShapeDtypeStruct((DEPTH, 8, N_MOD * D_MODEL), F32),
        grid=(DEPTH, N_MOD * D_MODEL // tn),
        in_specs=[
            pl.BlockSpec((8, D_MODEL), lambda i, j: (0, 0)),
            pl.BlockSpec((None, D_MODEL, tn), lambda i, j: (i, 0, j)),
            pl.BlockSpec((None, 1, tn), lambda i, j: (i, 0, j)),
        ],
        out_specs=pl.BlockSpec((None, 8, tn), lambda i, j: (i, 0, j)),
        compiler_params=_params(("parallel", "parallel"), VMEM_SMALL_MIB),
        name="ada_mod",
    )(cvec, ada_w, ada_b.reshape(DEPTH, 1, N_MOD * D_MODEL))


def _with_casts(body, n_in, n_out, n_cast):
    def kern(*refs):
        srcs = refs[n_in:n_in + n_cast]
        outs_at = n_in + n_cast
        dsts = refs[outs_at + n_out:outs_at + n_out + n_cast]
        for src, dst in zip(srcs, dsts):
            dst[...] = src[...].astype(dst.dtype)
        body(*refs[:n_in], *refs[outs_at:outs_at + n_out], *refs[outs_at + n_out + n_cast:])
    return kern


def _cast_jobs(stacked_weights, steps_per_row):
    in_specs, out_specs, out_shapes, args = [], [], [], []
    chunk = lambda b, i: jnp.minimum(b * steps_per_row + i, CAST_CHUNKS - 1)
    for w, lead in stacked_weights:
        _, rows, cols = w.shape
        blk = rows // CAST_CHUNKS
        in_specs.append(pl.BlockSpec((None, blk, cols), lambda b, i, lead=lead: (lead, chunk(b, i), 0)))
        out_specs.append(pl.BlockSpec((blk, cols), lambda b, i: (chunk(b, i), 0)))
        out_shapes.append(jax.ShapeDtypeStruct((rows, cols), BF16))
        args.append(w)
    return in_specs, out_specs, out_shapes, args


def _mod_row(ref, row):
    if isinstance(row, int):
        return ref[row:row + 1, :]
    return ref[pl.ds(row, 1), :]


def _norm_mod(x, g, shift, scale):
    y = x * lax.rsqrt(jnp.mean(x * x, axis=-1, keepdims=True) + EPS)
    return (y * g) * (1.0 + scale) + shift


def _rope_tile(x, cos, sin_signed):
    lane = lax.broadcasted_iota(jnp.int32, x.shape, 1)
    first = (lane % 32) < 16
    partner = jnp.where(first, pltpu.roll(x, LANES - 16, axis=1), pltpu.roll(x, 16, axis=1))
    return x * cos + partner * sin_signed


def _project0(x_ref, g, sh, sc, w_ref, cos_ref, sin_ref, u_ref, qkv_ref, rope):
    tn = 512
    n_rot = (ATT_WIDTH + 2 * LANES) // LANES
    sub = min(x_ref.shape[0], SUB_ROWS)
    for s in range(x_ref.shape[0] // sub):
        rows = slice(s * sub, (s + 1) * sub)
        urows = slice(CONV_PAD + s * sub, CONV_PAD + (s + 1) * sub)
        h = _norm_mod(x_ref[rows, :], g, sh, sc).astype(BF16)
        for j in range(EV_U // tn):
            u_ref[urows, j * tn:(j + 1) * tn] = jnp.dot(h, w_ref[:, j * tn:(j + 1) * tn],
                                                        preferred_element_type=F32)
        for j in range(EV_QKV // tn):
            y = jnp.dot(h, w_ref[:, EV_U + j * tn:EV_U + (j + 1) * tn], preferred_element_type=F32)
            for t in range(tn // LANES):
                yt = y[:, t * LANES:(t + 1) * LANES]
                if rope and j * (tn // LANES) + t < n_rot:
                    yt = _rope_tile(yt, cos_ref[rows, :], sin_ref[rows, :])
                qkv_ref[rows, j * tn + t * LANES:j * tn + (t + 1) * LANES] = yt.astype(BF16)


def _short_conv_gate(u_ref, cw_ref, cb_ref, bias_ref, z_ref, x0_ref, zb_ref):
    tm = u_ref.shape[0] - 2 * CONV_PAD
    rc = CONV_ROWS
    z_pitched = z_ref.shape[0] != tm

    def conv(r0, c0):
        cols = slice(c0, c0 + LANES)
        taps = [u_ref[CONV_PAD - 1 + k + r0:CONV_PAD - 1 + k + r0 + rc, cols] for k in range(3)]
        return (taps[0] * cw_ref[0:1, cols] + taps[1] * cw_ref[1:2, cols] + taps[2] * cw_ref[2:3, cols]
                + cb_ref[:, cols])

    for r0 in range(0, tm, rc):
        rows = slice(r0, r0 + rc)
        z0 = r0 // FFT_R * FFT_PITCH + r0 % FFT_R if z_pitched else r0
        for c0 in range(0, HY_WIDTH, LANES):
            cols = slice(c0, c0 + LANES)
            x0 = conv(r0, c0)
            z = conv(r0, 2 * HY_WIDTH + c0) * conv(r0, HY_WIDTH + c0)
            z_ref[z0:z0 + rc, cols] = z
            x0_ref[rows, cols] = x0
            zb_ref[rows, cols] = x0 * (z * bias_ref[:, cols])
    if z_pitched:
        for k in range(tm // FFT_R):
            z_ref[k * FFT_PITCH + FFT_R:(k + 1) * FFT_PITCH, :] = jnp.zeros((FFT_PITCH - FFT_R, HY_WIDTH), F32)


def _inproj0_kernel(x_ref, g_ref, sh_ref, sc_ref, w_ref, cos_ref, sin_ref, cw_ref, cb_ref, bias_ref,
                    z_ref, x0_ref, zb_ref, qkv_ref, u_even, u_odd, *, row, rope, n_tiles):
    r = pl.program_id(0) if row is None else row
    i = pl.program_id(1)
    g, sh, sc = g_ref[...], _mod_row(sh_ref, r), _mod_row(sc_ref, r)
    conv_refs = (cw_ref, cb_ref, bias_ref, z_ref, x0_ref, zb_ref)
    zero_row = jnp.zeros((1, EV_U), F32)
    tm = x_ref.shape[0]
    above, first, last, below = CONV_PAD - 1, CONV_PAD, CONV_PAD + tm - 1, CONV_PAD + tm

    if n_tiles == 1:
        u_even[above:first, :] = zero_row
        u_even[below:below + 1, :] = zero_row
        _project0(x_ref, g, sh, sc, w_ref, cos_ref, sin_ref, u_even, qkv_ref, rope)
        _short_conv_gate(u_even, *conv_refs)
        return

    @pl.when(i == 0)
    def _():
        u_even[...] = jnp.zeros_like(u_even)
        u_odd[...] = jnp.zeros_like(u_odd)

    def step(u_new, u_old):
        j = i - 1
        u_old[above:first, :] = jnp.where(j > 0, u_new[last:last + 1, :], zero_row)
        _project0(x_ref, g, sh, sc, w_ref, cos_ref, sin_ref, u_new, qkv_ref, rope)
        u_old[below:below + 1, :] = jnp.where(j < n_tiles - 1, u_new[first:first + 1, :], zero_row)
        _short_conv_gate(u_old, *conv_refs)

    @pl.when((i % 2 == 0) & (i < n_tiles))
    def _():
        step(u_even, u_odd)

    @pl.when((i % 2 == 1) & (i < n_tiles))
    def _():
        step(u_odd, u_even)

    @pl.when(i == n_tiles)
    def _():
        bufs = (u_even, u_odd)
        u_last, u_before = bufs[(n_tiles - 1) % 2], bufs[n_tiles % 2]
        u_last[above:first, :] = u_before[last:last + 1, :]
        u_last[below:below + 1, :] = zero_row
        _short_conv_gate(u_last, *conv_refs)


def _inproj0(x, g, mod, w, conv_w, conv_b, bias, layer, *, is_ctx, tm, cast=()):
    G, R, _ = x.shape
    n_tiles = R // tm
    n_steps = n_tiles + (1 if n_tiles > 1 else 0)
    cos, sin = _rope2d_tables()
    cos, sin = jnp.asarray(cos), jnp.asarray(sin)
    proj_tile = lambda i: jnp.minimum(i, n_tiles - 1)
    conv_tile = (lambda i: jnp.maximum(i - 1, 0)) if n_tiles > 1 else (lambda i: i)
    if is_ctx:
        tab = pl.BlockSpec((tm, LANES), lambda b, i: (0, 0))
    else:
        tab = pl.BlockSpec((tm, LANES), lambda b, i: (proj_tile(i), 0))
    const = lambda a: pl.BlockSpec(a.shape, lambda b, i: (0, 0))
    hy_out = jax.ShapeDtypeStruct((G, R, HY_WIDTH), F32)
    hy_spec = pl.BlockSpec((None, tm, HY_WIDTH), lambda b, i: (b, conv_tile(i), 0))
    if R == SEQ:
        z_out = jax.ShapeDtypeStruct((G, SEQ_P, HY_WIDTH), F32)
        z_spec = pl.BlockSpec((None, tm // FFT_R * FFT_PITCH, HY_WIDTH), lambda b, i: (b, conv_tile(i), 0))
    else:
        z_out, z_spec = hy_out, hy_spec
    kern = functools.partial(_inproj0_kernel, row=2 if is_ctx else None, rope=not is_ctx, n_tiles=n_tiles)
    in_specs = [
        pl.BlockSpec((None, tm, D_MODEL), lambda b, i: (b, proj_tile(i), 0)),
        pl.BlockSpec((1, D_MODEL), lambda b, i: (0, 0)),
        pl.BlockSpec((None, 8, D_MODEL), lambda b, i: (layer, 0, 0)),
        pl.BlockSpec((None, 8, D_MODEL), lambda b, i: (layer, 0, 1)),
        pl.BlockSpec((D_MODEL, EV_COLS), lambda b, i: (0, 0), pipeline_mode=pl.Buffered(1)),
        tab, tab, const(conv_w), const(conv_b), const(bias),
    ]
    out_specs = [z_spec, hy_spec, hy_spec, pl.BlockSpec((None, tm, EV_QKV), lambda b, i: (b, proj_tile(i), 0))]
    out_shape = [z_out, hy_out, hy_out, jax.ShapeDtypeStruct((G, R, EV_QKV), BF16)]
    c_in, c_out, c_shapes, c_args = _cast_jobs(cast, n_steps)
    if cast:
        assert G * n_steps >= CAST_CHUNKS
        kern = _with_casts(kern, len(in_specs), len(out_specs), len(cast))
    res = pl.pallas_call(
        kern,
        out_shape=out_shape + c_shapes,
        grid=(G, n_steps),
        in_specs=in_specs + c_in,
        out_specs=out_specs + c_out,
        scratch_shapes=[pltpu.VMEM((tm + 2 * CONV_PAD, EV_U), F32)] * 2,
        compiler_params=_params(("arbitrary", "arbitrary") if cast else ("parallel", "arbitrary"), VMEM_SMALL_MIB),
        name="inproj0_ctx" if is_ctx else "inproj0",
    )(x, g, mod, mod, w, cos, sin, conv_w, conv_b, bias, *c_args)
    return res[:4], res[4:]


def _inproj1_kernel(x_ref, g_ref, sh_ref, sc_ref, w_ref, o_ref, *, row, tn):
    r = pl.program_id(0) if row is None else row
    g, sh, sc = g_ref[...], _mod_row(sh_ref, r), _mod_row(sc_ref, r)
    sub = min(x_ref.shape[0], SUB_ROWS)
    for s in range(x_ref.shape[0] // sub):
        rows = slice(s * sub, (s + 1) * sub)
        h = _norm_mod(x_ref[rows, :], g, sh, sc).astype(BF16)
        for j in range(w_ref.shape[1] // tn):
            cols = slice(j * tn, (j + 1) * tn)
            o_ref[rows, cols] = jnp.dot(h, w_ref[:, cols], preferred_element_type=F32).astype(o_ref.dtype)


def _inproj1(x, g, mod, w, layer, *, is_ctx, tm, tn, cast=()):
    G, R, _ = x.shape
    N = w.shape[1]
    kern = functools.partial(_inproj1_kernel, row=2 if is_ctx else None, tn=tn)
    in_specs = [
        pl.BlockSpec((None, tm, D_MODEL), lambda b, i: (b, i, 0)),
        pl.BlockSpec((1, D_MODEL), lambda b, i: (0, 0)),
        pl.BlockSpec((None, 8, D_MODEL), lambda b, i: (layer, 0, 0)),
        pl.BlockSpec((None, 8, D_MODEL), lambda b, i: (layer, 0, 1)),
        pl.BlockSpec((D_MODEL, N), lambda b, i: (0, 0), pipeline_mode=pl.Buffered(1)),
    ]
    c_in, c_out, c_shapes, c_args = _cast_jobs(cast, R // tm)
    if cast:
        assert G * (R // tm) >= CAST_CHUNKS
        kern = _with_casts(kern, len(in_specs), 1, len(cast))
    res = pl.pallas_call(
        kern,
        out_shape=[jax.ShapeDtypeStruct((G, R, N), BF16)] + c_shapes,
        grid=(G, R // tm),
        in_specs=in_specs + c_in,
        out_specs=[pl.BlockSpec((None, tm, N), lambda b, i: (b, i, 0))] + c_out,
        compiler_params=_params(("arbitrary", "arbitrary") if cast else ("parallel", "parallel"), VMEM_SMALL_MIB),
        name="inproj1_ctx" if is_ctx else "inproj1",
    )(x, g, mod, mod, w, *c_args)
    return res[0], res[1:]


def _split_bf16(a):
    hi = a.astype(BF16)
    return hi, (a - hi.astype(F32)).astype(BF16)


def _dot3_split(a_hi, a_lo, b):
    b_hi, b_lo = _split_bf16(b)
    dot = functools.partial(jnp.dot, preferred_element_type=F32)
    return dot(a_hi, b_hi) + (dot(a_lo, b_hi) + dot(a_hi, b_lo))


def _dot3(a, b):
    return _dot3_split(*_split_bf16(a), b)


def _store_pitched(ref, lead, value):
    for k in range(value.shape[0] // FFT_R):
        ref[lead, k * FFT_PITCH:k * FFT_PITCH + FFT_R, :] = value[k * FFT_R:(k + 1) * FFT_R]
        ref[lead, k * FFT_PITCH + FFT_R:(k + 1) * FFT_PITCH, :] = jnp.zeros(
            (FFT_PITCH - FFT_R, value.shape[1]), value.dtype)


def _filter_kernel(w1t_ref, b1_ref, w2t_ref, b2_ref, fr_ref, w3_ref, dec_ref, bands_ref, h_ref, ss_ref, *,
                   L, tm, pitched):
    i = pl.program_id(0)
    t_row = (lax.broadcasted_iota(jnp.int32, (1, tm), 1) + i * tm).astype(F32) / L
    ang = (2.0 * math.pi * t_row) * bands_ref[...]
    t8 = jnp.where(lax.broadcasted_iota(jnp.int32, (8, tm), 0) == 0, t_row, 0.0)
    feat = jnp.concatenate([t8, jnp.cos(ang), -jnp.sin(ang)], axis=0)
    fr = fr_ref[...]
    hid = jnp.sin(fr * (jnp.dot(w1t_ref[...], feat, precision=HIGHEST, preferred_element_type=F32) + b1_ref[...]))
    hid = jnp.sin(fr * (jnp.dot(w2t_ref[...], hid, precision=HIGHEST, preferred_element_type=F32) + b2_ref[...]))
    hid_hi, hid_lo = _split_bf16(hid.T)
    m_col = lax.broadcasted_iota(jnp.int32, (tm, 1), 0) + i * tm
    t_col = m_col.astype(F32) / L
    energy = jnp.zeros((1, HY_WIDTH), F32)
    for side in range(2):
        h = _dot3_split(hid_hi, hid_lo, w3_ref[side]) * jnp.exp(-t_col * jnp.abs(dec_ref[side]))
        if side == 1:
            h = jnp.where(m_col == 0, 0.0, h)
        if pitched:
            _store_pitched(h_ref, side, h)
        else:
            h_ref[side] = h
        energy = energy + jnp.sum(h * h, axis=0, keepdims=True)

    @pl.when(i == 0)
    def _():
        ss_ref[...] = jnp.zeros_like(ss_ref)

    ss_ref[...] += jnp.broadcast_to(energy, ss_ref.shape)


def _hyena_filter(w1, b1, w2, b2, w3, freq, decay, *, L):
    tm = min(L, 1024)
    pitched = L == SEQ
    tm_out, l_out = (tm // FFT_R * FFT_PITCH, SEQ_P) if pitched else (tm, L)
    kern = functools.partial(_filter_kernel, L=L, tm=tm, pitched=pitched)
    w1t = jnp.concatenate([w1[0:1], jnp.zeros((7, HY_HIDDEN), F32), w1[1:]], axis=0).T
    col = lambda a: a.reshape(HY_HIDDEN, 1)
    w3s = jnp.stack([w3[:, :HY_WIDTH], w3[:, HY_WIDTH:]])
    bands = jnp.asarray(np.linspace(1e-4, HY_BANDS - 1, HY_BANDS).astype(np.float32).reshape(HY_BANDS, 1))
    args = (w1t, col(b1), w2.T, col(b2), col(freq), w3s, decay.reshape(2, 1, HY_WIDTH), bands)
    full = lambda a: pl.BlockSpec(a.shape, lambda i: (0,) * a.ndim)
    h, ss = pl.pallas_call(
        kern,
        out_shape=(jax.ShapeDtypeStruct((2, l_out, HY_WIDTH), F32), jax.ShapeDtypeStruct((8, HY_WIDTH), F32)),
        grid=(L // tm,),
        in_specs=[full(a) for a in args],
        out_specs=(pl.BlockSpec((2, tm_out, HY_WIDTH), lambda i: (0, i, 0)),
                   pl.BlockSpec((8, HY_WIDTH), lambda i: (0, 0))),
        compiler_params=_params(("arbitrary",), VMEM_SMALL_MIB),
        name="hyena_filter_%d" % L,
    )(*args)
    return h.reshape(2 * l_out, HY_WIDTH), ss


def _stack_bf16(re, im):
    return jnp.concatenate([re, im], axis=0).astype(BF16)


def _filter_fft_kernel(h_ref, ss_ref, m1_ref, f2_ref, o_ref, a_scr):
    R, P = FFT_R, FFT_PITCH
    inv_norm = lax.rsqrt(ss_ref[0:1, :])

    def stage1(n2, carry):
        neg = jnp.where(n2 == 0, 0, R - n2)
        xs = jnp.concatenate([h_ref[pl.ds(n2, R // 2, stride=P), :],
                              h_ref[pl.ds(SEQ_P + neg, R // 2, stride=P), :]], axis=0)
        xs = (xs * inv_norm).astype(BF16)
        a = jnp.dot(m1_ref[n2], xs, preferred_element_type=F32)
        base = pl.multiple_of(n2 * P, 8)
        a_scr[0, pl.ds(base, R), :] = a[:R]
        a_scr[1, pl.ds(base, R), :] = a[R:]
        return carry

    lax.fori_loop(0, R, stage1, 0, unroll=FFT_UNROLL)

    ct = a_scr.shape[-1]

    def stage2(pair, carry):
        k1s = (2 * pair, 2 * pair + 1)
        rhs = jnp.concatenate([_stack_bf16(a_scr[0, pl.ds(k1, R, stride=P), :],
                                           a_scr[1, pl.ds(k1, R, stride=P), :]) for k1 in k1s], axis=1)
        x = jnp.dot(f2_ref[...], rhs, preferred_element_type=F32)
        for d, k1 in enumerate(k1s):
            base = pl.multiple_of(k1 * R, R)
            o_ref[0, pl.ds(base, R), :] = x[:R, d * ct:(d + 1) * ct].astype(BF16)
            o_ref[1, pl.ds(base, R), :] = x[R:, d * ct:(d + 1) * ct].astype(BF16)
        return carry

    lax.fori_loop(0, R // 2, stage2, 0, unroll=FFT_UNROLL // 2)


def _filter_fft(h_circ, ss, m1, f2):
    ct = LANES
    once = pl.Buffered(1)
    return pl.pallas_call(
        _filter_fft_kernel,
        out_shape=jax.ShapeDtypeStruct((2, FFT_N, HY_WIDTH), BF16),
        grid=(HY_WIDTH // ct,),
        in_specs=[
            pl.BlockSpec((2 * SEQ_P, ct), lambda c: (0, c)),
            pl.BlockSpec((8, ct), lambda c: (0, c)),
            pl.BlockSpec(m1.shape, lambda c: (0, 0, 0), pipeline_mode=once),
            pl.BlockSpec(f2.shape, lambda c: (0, 0), pipeline_mode=once),
        ],
        out_specs=pl.BlockSpec((2, FFT_N, ct), lambda c: (0, 0, c)),
        scratch_shapes=[pltpu.VMEM((2, FFT_R * FFT_PITCH, ct), F32)],
        compiler_params=_params(("parallel",), VMEM_FFT_MIB),
        name="hyena_filter_fft",
    )(h_circ, ss, m1, f2)


def _fft_conv_kernel(z_ref, hs_ref, m1_ref, f2_ref, g2_ref, m1i_ref, y_ref, a_scr):
    R, P = FFT_R, FFT_PITCH
    ct = a_scr.shape[-1]

    def stage1(n2, carry):
        rows = pl.ds(n2, R // 2, stride=P)
        xs = jnp.concatenate([z_ref[0, rows, :], z_ref[1, rows, :]], axis=1).astype(BF16)
        t = jnp.dot(m1_ref[n2, :, 0:R // 2], xs, preferred_element_type=F32)
        base = pl.multiple_of(n2 * P, 8)
        a_scr[0, pl.ds(base, R), :] = t[:R, :ct] - t[R:, ct:]
        a_scr[1, pl.ds(base, R), :] = t[R:, :ct] + t[:R, ct:]
        return carry

    lax.fori_loop(0, R, stage1, 0, unroll=FFT_UNROLL)

    def stage2(pair, carry):
        k1s = (2 * pair, 2 * pair + 1)
        rhs = jnp.concatenate([_stack_bf16(a_scr[0, pl.ds(k1, R, stride=P), :],
                                           a_scr[1, pl.ds(k1, R, stride=P), :]) for k1 in k1s], axis=1)
        x = jnp.dot(f2_ref[...], rhs, preferred_element_type=F32)
        prods = []
        for d, k1 in enumerate(k1s):
            base = pl.multiple_of(k1 * R, R)
            hr = hs_ref[0, pl.ds(base, R), :].astype(F32)
            hi = hs_ref[1, pl.ds(base, R), :].astype(F32)
            xr, xi = x[:R, d * ct:(d + 1) * ct], x[R:, d * ct:(d + 1) * ct]
            prods.append(_stack_bf16(xr * hr - xi * hi, xr * hi + xi * hr))
        bq = jnp.dot(g2_ref[...], jnp.concatenate(prods, axis=1), preferred_element_type=F32)
        for d, k1 in enumerate(k1s):
            a_scr[0, pl.ds(k1, R, stride=P), :] = bq[:R, d * ct:(d + 1) * ct]
            a_scr[1, pl.ds(k1, R, stride=P), :] = bq[R:, d * ct:(d + 1) * ct]
        return carry

    lax.fori_loop(0, R // 2, stage2, 0, unroll=FFT_UNROLL // 2)

    def stage3(n2, carry):
        base = pl.multiple_of(n2 * P, 8)
        br = a_scr[0, pl.ds(base, R), :]
        bi = a_scr[1, pl.ds(base, R), :]
        rhs = jnp.concatenate([jnp.concatenate([br, bi], axis=1),
                               jnp.concatenate([bi, -br], axis=1)], axis=0).astype(BF16)
        y = jnp.dot(m1i_ref[n2], rhs, preferred_element_type=F32)
        rows = pl.ds(n2, R // 2, stride=P)
        y_ref[0, rows, :] = y[:, :ct]
        y_ref[1, rows, :] = y[:, ct:]
        return carry

    lax.fori_loop(0, R, stage3, 0, unroll=FFT_UNROLL)
    for b in range(2):
        for n1 in range(R // 2):
            y_ref[b, n1 * P + R:(n1 + 1) * P, :] = jnp.zeros((P - R, ct), F32)


def _fft_conv(z, hspec, m1, f2, g2, m1i):
    assert z.shape[0] == 2, "the batch pair is packed into one complex signal"
    ct = LANES
    once = pl.Buffered(1)
    return pl.pallas_call(
        _fft_conv_kernel,
        out_shape=jax.ShapeDtypeStruct((2, SEQ_P, HY_WIDTH), F32),
        grid=(HY_WIDTH // ct,),
        in_specs=[
            pl.BlockSpec((2, SEQ_P, ct), lambda c: (0, 0, c), pipeline_mode=once),
            pl.BlockSpec((2, FFT_N, ct), lambda c: (0, 0, c), pipeline_mode=once),
            pl.BlockSpec(m1.shape, lambda c: (0, 0, 0), pipeline_mode=once),
            pl.BlockSpec(f2.shape, lambda c: (0, 0), pipeline_mode=once),
            pl.BlockSpec(g2.shape, lambda c: (0, 0), pipeline_mode=once),
            pl.BlockSpec(m1i.shape, lambda c: (0, 0, 0), pipeline_mode=once),
        ],
        out_specs=pl.BlockSpec((2, SEQ_P, ct), lambda c: (0, 0, c), pipeline_mode=once),
        scratch_shapes=[pltpu.VMEM((2, FFT_R * FFT_PITCH, ct), F32)],
        compiler_params=_params(("arbitrary",), VMEM_FFT_MIB),
        name="hyena_fft_conv",
    )(z, hspec, m1, f2, g2, m1i)


def _ctx_conv_kernel(z_ref, h_ref, ss_ref, fwd_ref, inv_ref, fwd_h_ref, y_ref):
    N = 2 * CTX_LEN
    hn = (h_ref[...] * lax.rsqrt(ss_ref[0:1, :])).astype(BF16)
    hs = jnp.dot(fwd_h_ref[...], hn, preferred_element_type=F32)
    zs = jnp.dot(fwd_ref[:, :CTX_LEN], z_ref[...].astype(BF16), preferred_element_type=F32)
    hr, hi, zr, zi = hs[:N], hs[N:], zs[:N], zs[N:]
    ys = _stack_bf16(zr * hr - zi * hi, zr * hi + zi * hr)
    y_ref[...] = jnp.dot(inv_ref[...], ys, preferred_element_type=F32)


def _ctx_conv(z, h_circ, ss, fwd, inv, fwd_h):
    G = z.shape[0]
    full = lambda a: pl.BlockSpec(a.shape, lambda b: (0,) * a.ndim)
    return pl.pallas_call(
        _ctx_conv_kernel,
        out_shape=jax.ShapeDtypeStruct((G, CTX_LEN, HY_WIDTH), F32),
        grid=(G,),
        in_specs=[pl.BlockSpec((None, CTX_LEN, HY_WIDTH), lambda b: (b, 0, 0)),
                  full(h_circ), full(ss), full(fwd), full(inv), full(fwd_h)],
        out_specs=pl.BlockSpec((None, CTX_LEN, HY_WIDTH), lambda b: (b, 0, 0)),
        compiler_params=_params(("parallel",), VMEM_SMALL_MIB),
        name="hyena_ctx_conv",
    )(z, h_circ, ss, fwd, inv, fwd_h)


def _kv_group(refs, g, col0):
    cols = slice(col0 + g * LANES, col0 + (g + 1) * LANES)
    return refs[0][:, cols] if len(refs) == 1 else jnp.concatenate([r[:, cols] for r in refs], axis=0)


def _attn_logits(q_ref, k_refs, g, masks, col0=0):
    low = lax.broadcasted_iota(jnp.int32, (BLOCK, LANES), 1) < HEAD_DIM
    zero = jnp.zeros((), BF16)
    parts = []
    for hh in range(ATT_GROUP):
        h = g * ATT_GROUP + hh
        tile = q_ref[:, (h // 2) * LANES:(h // 2 + 1) * LANES]
        parts.append(jnp.where(low if h % 2 == 0 else ~low, tile, zero))
    s = lax.dot_general(jnp.concatenate(parts, axis=0), _kv_group(k_refs, g, col0), (((1,), (1,)), ((), ())),
                        preferred_element_type=F32)
    if masks is not None:
        ok_prev, ok_next = masks
        s = jnp.concatenate([jnp.where(ok_prev, s[:, :BLOCK], NEG_INF), s[:, BLOCK:2 * BLOCK],
                             jnp.where(ok_next, s[:, 2 * BLOCK:3 * BLOCK], NEG_INF), s[:, 3 * BLOCK:]], axis=1)
    return s


def _attn_output(s, sink_ref, v_refs, g, o_ref, col0=0):
    low = lax.broadcasted_iota(jnp.int32, (BLOCK, LANES), 1) < HEAD_DIM
    low4 = jnp.concatenate([low] * ATT_GROUP, axis=0)
    v = _kv_group(v_refs, g, col0)
    v_aug = jnp.where(lax.broadcasted_iota(jnp.int32, v.shape, 1) < HEAD_DIM, v, jnp.ones((), BF16))
    sink = jnp.concatenate(
        [jnp.full((BLOCK, 1), sink_ref[g * ATT_GROUP + hh] * LOG2E, F32) for hh in range(ATT_GROUP)], axis=0)
    m = jnp.maximum(jnp.max(s, axis=-1, keepdims=True), sink)
    e = jnp.exp2(s - m).astype(BF16)
    o = jnp.dot(e, v_aug, preferred_element_type=F32) + jnp.where(low4, 0.0, jnp.exp2(sink - m))
    swapped = pltpu.roll(o, HEAD_DIM, axis=1)
    for pair in range(ATT_GROUP // 2):
        ev = slice((2 * pair) * BLOCK, (2 * pair + 1) * BLOCK)
        od = slice((2 * pair + 1) * BLOCK, (2 * pair + 2) * BLOCK)
        even = o[ev] / swapped[ev]
        odd = swapped[od] / o[od]
        t = g * (ATT_GROUP // 2) + pair
        o_ref[:, t * LANES:(t + 1) * LANES] = jnp.where(low, even, odd).astype(o_ref.dtype)


def _attn_ctx_kernel(sink_ref, q_ref, kx_ref, vx_ref, o_ref):
    for g in range(ATT_KV_HEADS):
        _attn_output(_attn_logits(q_ref, (kx_ref,), g, None), sink_ref, (vx_ref,), g, o_ref)


def _attn_local_kernel(sink_ref, q_ref, kv_m2, kv_m1, kv_0, kv_p1, kv_ctx, o_ref, s_even, s_odd, *, n_blocks):
    n = pl.program_id(1)
    k_refs, v_refs, v_col = (kv_m1, kv_0, kv_p1, kv_ctx), (kv_m2, kv_m1, kv_0, kv_ctx), 2 * LANES

    @pl.when(n == 0)
    def _():
        s_odd[...] = jnp.zeros_like(s_odd)

    def step(s_new, s_prev):
        nq = jnp.minimum(n, n_blocks - 1)
        qi = lax.broadcasted_iota(jnp.int32, (ATT_GROUP * BLOCK, BLOCK), 0) % BLOCK
        ki = lax.broadcasted_iota(jnp.int32, (ATT_GROUP * BLOCK, BLOCK), 1)
        masks = ((ki >= qi) & (nq > 0), (ki <= qi) & (nq < n_blocks - 1))
        for g in range(ATT_KV_HEADS):
            s_new[g] = _attn_logits(q_ref, k_refs, g, masks)
        for g in range(ATT_KV_HEADS):
            _attn_output(s_prev[g], sink_ref, v_refs, g, o_ref, v_col)

    @pl.when(n % 2 == 0)
    def _():
        step(s_even, s_odd)

    @pl.when(n % 2 == 1)
    def _():
        step(s_odd, s_even)


def _attention(sink, qkv, qkv_ctx, *, local):
    src = qkv if local else qkv_ctx
    G, R, _ = src.shape
    nb = R // BLOCK
    kcol, vcol = ATT_WIDTH // 256, ATT_WIDTH // 256 + 1
    ctx_k = pl.BlockSpec((None, CTX_LEN, 256), lambda b, n: (b, 0, kcol))
    ctx_v = pl.BlockSpec((None, CTX_LEN, 256), lambda b, n: (b, 0, vcol))
    smem = pl.BlockSpec(memory_space=pltpu.SMEM)
    out_shape = jax.ShapeDtypeStruct((G, R, ATT_WIDTH), BF16)
    if not local:
        rows = pl.BlockSpec((None, BLOCK, ATT_WIDTH), lambda b, n: (b, n, 0))
        return pl.pallas_call(
            _attn_ctx_kernel, out_shape=out_shape, grid=(G, nb),
            in_specs=[smem, rows, ctx_k, ctx_v], out_specs=rows,
            compiler_params=_params(("parallel", "parallel"), VMEM_SMALL_MIB), name="attn_ctx",
        )(sink, qkv_ctx, qkv_ctx, qkv_ctx)

    q_blk = lambda n: jnp.minimum(n, nb - 1)
    o_blk = lambda n: jnp.maximum(n - 1, 0)

    kv_w = EV_QKV - ATT_WIDTH
    kv_col = ATT_WIDTH // kv_w

    def kv_rows(off):
        return pl.BlockSpec((None, BLOCK, kv_w), lambda b, n: (b, jnp.clip(n + off, 0, nb - 1), kv_col))

    in_specs = [smem, pl.BlockSpec((None, BLOCK, ATT_WIDTH), lambda b, n: (b, q_blk(n), 0)),
                kv_rows(-2), kv_rows(-1), kv_rows(0), kv_rows(1),
                pl.BlockSpec((None, CTX_LEN, kv_w), lambda b, n: (b, 0, kv_col))]
    logits_scratch = pltpu.VMEM((ATT_KV_HEADS, ATT_GROUP * BLOCK, 3 * BLOCK + CTX_LEN), F32)
    return pl.pallas_call(
        functools.partial(_attn_local_kernel, n_blocks=nb),
        out_shape=out_shape,
        grid=(G, nb + 1),
        in_specs=in_specs,
        out_specs=pl.BlockSpec((None, BLOCK, ATT_WIDTH), lambda b, n: (b, o_blk(n), 0)),
        scratch_shapes=[logits_scratch, logits_scratch],
        compiler_params=_params(("parallel", "arbitrary"), VMEM_SMALL_MIB),
        name="attn_local",
    )(sink, qkv, qkv, qkv, qkv, qkv, qkv_ctx)


def _mixer_out0(rows, x0_ref, y_ref, zb_ref, att_ref, wo_ref):
    if y_ref.shape[0] != x0_ref.shape[0]:
        y = jnp.concatenate([y_ref[r // FFT_R * FFT_PITCH:r // FFT_R * FFT_PITCH + FFT_R, :]
                             for r in range(rows.start, rows.stop, FFT_R)], axis=0)
    else:
        y = y_ref[rows, :]
    hy = (x0_ref[rows, :] * y + zb_ref[rows, :]).astype(BF16)
    return (jnp.dot(hy, wo_ref[:HY_WIDTH, :], preferred_element_type=F32)
            + jnp.dot(att_ref[rows, :], wo_ref[HY_WIDTH:, :], preferred_element_type=F32))


def _mixer_out1(rows, of_ref, ob_ref, wo_ref):
    a = (of_ref[rows, :].astype(F32) + ob_ref[rows, :].astype(F32)).astype(BF16)
    return jnp.dot(a, wo_ref[...], preferred_element_type=F32)


def _mix_kernel(*refs, mixer_out, n_mix, row, fc, final_norm):
    x_ref = refs[0]
    mix_refs = refs[1:2 + n_mix]
    gta_ref, gm_ref, shm_ref, scm_ref, gtm_ref, w1_ref, w2_ref, fg_ref, o_ref, a_scr = refs[2 + n_mix:]
    r = pl.program_id(0) if row is None else row
    gta, gtm = _mod_row(gta_ref, r), _mod_row(gtm_ref, r)
    gm, shm, scm = gm_ref[...], _mod_row(shm_ref, r), _mod_row(scm_ref, r)
    sub = min(x_ref.shape[0], MIX_SUB_ROWS)
    for s in range(x_ref.shape[0] // sub):
        rows = slice(s * sub, (s + 1) * sub)
        x1 = x_ref[rows, :] + gta * mixer_out(rows, *mix_refs)
        h = _norm_mod(x1, gm, shm, scm).astype(BF16)
        for c in range(D_FF // fc):
            a = jnp.maximum(jnp.dot(h, w1_ref[:, c * fc:(c + 1) * fc], preferred_element_type=F32), 0.0)
            a_scr[rows, c * fc:(c + 1) * fc] = (a * a).astype(BF16)
        out = x1 + gtm * jnp.dot(a_scr[rows, :], w2_ref[...], preferred_element_type=F32)
        if final_norm:
            out = (out * lax.rsqrt(jnp.mean(out * out, axis=-1, keepdims=True) + EPS)) * fg_ref[...]
        o_ref[rows, :] = out


def _mix_mlp(kind, x, mix_in, wo, mod, gm, w1, w2, fg, layer, *, is_ctx, tm, fc, final_norm, cast=()):
    G, R, _ = x.shape
    row_spec = lambda w: pl.BlockSpec((None, tm, w), lambda b, i: (b, i, 0))
    modk = lambda k: pl.BlockSpec((None, 8, D_MODEL), lambda b, i: (layer, 0, k))
    vec = pl.BlockSpec((1, D_MODEL), lambda b, i: (0, 0))
    resident = lambda a: pl.BlockSpec(a.shape, lambda b, i: (0, 0), pipeline_mode=pl.Buffered(1))
    if kind == 0:
        mixer_out = _mixer_out0
        y_rows = tm if mix_in[1].shape[1] == R else tm // FFT_R * FFT_PITCH
        y_spec = pl.BlockSpec((None, y_rows, HY_WIDTH), lambda b, i: (b, i, 0))
        mix_specs = [row_spec(HY_WIDTH), y_spec, row_spec(HY_WIDTH), row_spec(ATT_WIDTH)]
    else:
        mixer_out = _mixer_out1
        mix_specs = [row_spec(RET_V)] * 2
    kern = functools.partial(_mix_kernel, mixer_out=mixer_out, n_mix=len(mix_in), row=2 if is_ctx else None,
                             fc=fc, final_norm=final_norm)
    in_specs = [row_spec(D_MODEL)] + mix_specs + [
        resident(wo), modk(2), vec, modk(3), modk(4), modk(5), resident(w1), resident(w2), vec]
    c_in, c_out, c_shapes, c_args = _cast_jobs(cast, R // tm)
    if cast:
        assert G * (R // tm) >= CAST_CHUNKS
        kern = _with_casts(kern, len(in_specs), 1, len(cast))
    res = pl.pallas_call(
        kern,
        out_shape=[jax.ShapeDtypeStruct((G, R, D_MODEL), F32)] + c_shapes,
        grid=(G, R // tm),
        in_specs=in_specs + c_in,
        out_specs=[row_spec(D_MODEL)] + c_out,
        scratch_shapes=[pltpu.VMEM((tm, D_FF), BF16)],
        compiler_params=_params(("arbitrary", "arbitrary") if cast else ("parallel", "parallel"), VMEM_DENSE_MIB),
        name="mix_mlp%d%s" % (kind, "_ctx" if is_ctx else ""),
    )(x, *mix_in, wo, mod, gm, mod, mod, mod, w1, w2, fg, *c_args)
    return res[0], res[1:]


def _ret_kernel(lr_ref, qkv_c, qkv_f, g_f, qkv_b, g_b, cos_ref, sin_ref, of_ref, ob_ref,
                state, dmask, xi, zeta, gch):
    j = pl.program_id(1)
    C = RET_C
    kscale = RET_DK ** -0.5

    @pl.when(j == 0)
    def _():
        state[...] = jnp.zeros_like(state)
        row = lax.broadcasted_iota(jnp.int32, (C, C), 0).astype(F32)
        col = lax.broadcasted_iota(jnp.int32, (C, C), 1).astype(F32)
        for d in range(2):
            e = row - col if d == 0 else col - row
            p = row if d == 0 else (C - 1) - row
            for h in range(RET_HEADS):
                lg = -jnp.exp(jnp.full((C, C), lr_ref[d, h], F32))
                dmask[d, h] = jnp.where(e >= 0, jnp.exp(e * lg), 0.0) * kscale
                xi[d, h] = jnp.exp((p + 1.0) * lg)
                zeta[d, h] = jnp.exp(((C - 1) - p) * lg) * kscale
                gch[d, h] = jnp.exp(C * -jnp.exp(jnp.full((8, C), lr_ref[d, h], F32)))

    half = RET_DK // 2

    def chain(d, h, src_ref, g_ref, o_ref):
        cos, sin = cos_ref[d], sin_ref[d]

        def rot(c0):
            t1 = src_ref[:, c0:c0 + half].astype(F32)
            t2 = src_ref[:, c0 + half:c0 + RET_DK].astype(F32)
            return jnp.concatenate([t1 * cos - t2 * sin, t2 * cos + t1 * sin], axis=1)

        q = rot(h * RET_DK)
        k = rot(RET_QK + h * RET_DK)
        v = src_ref[:, 2 * RET_QK + h * RET_DV:2 * RET_QK + (h + 1) * RET_DV]
        inner = lax.dot_general(q.astype(BF16), k.astype(BF16), (((1,), (1,)), ((), ())),
                                preferred_element_type=F32) * dmask[d, h]
        s_old = state[d, h]
        kv = lax.dot_general((k * zeta[d, h]).astype(BF16), v, (((0,), (0,)), ((), ())),
                             preferred_element_type=F32)
        state[d, h] = gch[d, h, 0:1, 0:1] * s_old + kv
        if o_ref is not None:
            o = (jnp.dot(inner.astype(BF16), v, preferred_element_type=F32)
                 + jnp.dot((q * xi[d, h]).astype(BF16), s_old.astype(BF16), preferred_element_type=F32))
            on = o * lax.rsqrt(jnp.mean(o * o, axis=-1, keepdims=True) + EPS)
            gate = g_ref[:, h * RET_DV:(h + 1) * RET_DV]
            half_gate = gate * 0.5
            silu = half_gate + half_gate * jnp.tanh(half_gate)
            o_ref[:, h * RET_DV:(h + 1) * RET_DV] = (silu.astype(F32) * on).astype(o_ref.dtype)

    @pl.when(j == 0)
    def _():
        for d in range(2):
            for h in range(RET_HEADS):
                chain(d, h, qkv_c, None, None)

    @pl.when(j > 0)
    def _():
        for h in range(RET_HEADS):
            chain(0, h, qkv_f, g_f, of_ref)
            chain(1, h, qkv_b, g_b, ob_ref)


def _retention(log_rate, p_ctx, p_lat):
    G = p_lat.shape[0]
    C = RET_C
    n_lat = SEQ // C
    cos, sin = _rope1d_tables()
    cos, sin = jnp.asarray(cos), jnp.asarray(sin)

    def fwd_chunk(j):
        return jnp.maximum(j - 1, 0)

    def bwd_chunk(j):
        return n_lat - 1 - jnp.maximum(j - 1, 0)

    qkv_w = 2 * RET_QK + RET_V
    gcol = qkv_w // RET_V
    tab = pl.BlockSpec((2, C, RET_DK // 2), lambda b, j: (0, j, 0))
    in_specs = [
        pl.BlockSpec(memory_space=pltpu.SMEM),
        pl.BlockSpec((None, C, qkv_w), lambda b, j: (b, 0, 0)),
        pl.BlockSpec((None, C, qkv_w), lambda b, j: (b, fwd_chunk(j), 0)),
        pl.BlockSpec((None, C, RET_V), lambda b, j: (b, fwd_chunk(j), gcol)),
        pl.BlockSpec((None, C, qkv_w), lambda b, j: (b, bwd_chunk(j), 0)),
        pl.BlockSpec((None, C, RET_V), lambda b, j: (b, bwd_chunk(j), gcol + 1)),
        tab, tab,
    ]
    out = jax.ShapeDtypeStruct((G, SEQ, RET_V), BF16)
    per_chain = lambda *tail: pltpu.VMEM((2, RET_HEADS) + tail, F32)
    return pl.pallas_call(
        _ret_kernel,
        out_shape=(out, out),
        grid=(G, 1 + n_lat),
        in_specs=in_specs,
        out_specs=(pl.BlockSpec((None, C, RET_V), lambda b, j: (b, fwd_chunk(j), 0)),
                   pl.BlockSpec((None, C, RET_V), lambda b, j: (b, bwd_chunk(j), 0))),
        scratch_shapes=[per_chain(RET_DK, RET_DV), per_chain(C, C), per_chain(C, C), per_chain(C, C),
                        per_chain(8, C)],
        compiler_params=_params(("parallel", "arbitrary"), VMEM_SMALL_MIB),
        name="retention",
    )(log_rate, p_ctx, p_lat, p_lat, p_lat, p_lat, cos, sin)


def _ev_weight(w_in):
    i_q = EV_U
    i_k = i_q + ATT_WIDTH
    i_v = i_k + KV_WIDTH
    cols = [w_in[:, :i_q], w_in[:, i_q:i_k] * (HEAD_DIM ** -0.5 * LOG2E)]
    for base in (i_k, i_v):
        for g in range(ATT_KV_HEADS):
            part = w_in[:, base + g * HEAD_DIM: base + (g + 1) * HEAD_DIM]
            cols += [part, part]
    return jnp.concatenate([c.astype(BF16) for c in cols], axis=1)


def kernel(x, c, ctx, c_ctx, ada_w, ada_b, norm_mix_g, norm_mlp_g, mlp_w1, mlp_w2, ev_w_in, ev_w_out, hy_conv_w, hy_conv_b, hy_w1, hy_b1, hy_w2, hy_b2, hy_w3, hy_freq, hy_decay, hy_bias, attn_sink, od_w_in, od_w_out, ret_log_rate, final_g):
    D = D_MODEL
    cvec = jnp.concatenate([c, c_ctx[None, :], jnp.zeros((8 - BATCH - 1, D), F32)], axis=0)
    mod = _ada(cvec, ada_w, ada_b)

    m1, m1i, f2, g2 = (jnp.asarray(a).astype(BF16) for a in _fft_mats())
    cfwd, cinv, cfwd_h = (jnp.asarray(a).astype(BF16) for a in _ctx_fft_mats())
    fg = final_g.reshape(1, D)

    gmix = norm_mix_g[0].reshape(1, D)
    gmlp = norm_mlp_g[0].reshape(1, D)
    w_in = _ev_weight(ev_w_in[0])
    conv = (hy_conv_w[0], hy_conv_b[0].reshape(1, EV_U), hy_bias[0].reshape(1, HY_WIDTH))
    (z_l, x0_l, zb_l, qkv_l), (wo, w1, w2) = _inproj0(
        x, gmix, mod, w_in, *conv, 0, is_ctx=False, tm=512, cast=((ev_w_out, 0), (mlp_w1, 0), (mlp_w2, 0)))
    (z_c, x0_c, zb_c, qkv_c), _ = _inproj0(ctx, gmix, mod, w_in, *conv, 0, is_ctx=True, tm=CTX_LEN)

    filt = (hy_w1[0], hy_b1[0], hy_w2[0], hy_b2[0], hy_w3[0], hy_freq[0], hy_decay[0])
    h_l, ss_l = _hyena_filter(*filt, L=SEQ)
    h_c, ss_c = _hyena_filter(*filt, L=CTX_LEN)
    hspec = _filter_fft(h_l, ss_l, m1, f2)
    y_l = _fft_conv(z_l, hspec, m1, f2, g2, m1i)
    y_c = _ctx_conv(z_c, h_c, ss_c, cfwd, cinv, cfwd_h)

    sink = attn_sink[0]
    att_l = _attention(sink, qkv_l, qkv_c, local=True)
    att_c = _attention(sink, qkv_l, qkv_c, local=False)

    x1, (w_in,) = _mix_mlp(0, x, (x0_l, y_l, zb_l, att_l), wo, mod, gmlp, w1, w2, fg, 0,
                           is_ctx=False, tm=512, fc=512, final_norm=False, cast=((od_w_in, 0),))
    ctx1, _ = _mix_mlp(0, ctx, (x0_c, y_c, zb_c, att_c), wo, mod, gmlp, w1, w2, fg, 0,
                       is_ctx=True, tm=CTX_LEN, fc=512, final_norm=False)

    gmix = norm_mix_g[1].reshape(1, D)
    gmlp = norm_mlp_g[1].reshape(1, D)
    p_l, (wo, w1, w2) = _inproj1(x1, gmix, mod, w_in, 1, is_ctx=False, tm=512, tn=512,
                                 cast=((od_w_out, 0), (mlp_w1, 1), (mlp_w2, 1)))
    p_c, _ = _inproj1(ctx1, gmix, mod, w_in, 1, is_ctx=True, tm=CTX_LEN, tn=512)
    o_fwd, o_bwd = _retention(ret_log_rate[0], p_c, p_l)

    out, _ = _mix_mlp(1, x1, (o_fwd, o_bwd), wo, mod, gmlp, w1, w2, fg, 1,
                      is_ctx=False, tm=512, fc=512, final_norm=True)
    return out
```

```python
import functools
import math

import numpy as np
import jax
import jax.numpy as jnp
from jax import lax
from jax.experimental import pallas as pl
from jax.experimental.pallas import tpu as pltpu

F32 = jnp.float32
BF16 = jnp.bfloat16
HIGHEST = lax.Precision.HIGHEST

D_MODEL = 1024
BATCH = 2
SEQ = 8192
DEPTH = 2
GRID_W = 64
CTX_LEN = 256
EPS = 1e-6
NEG_INF = -1e30
N_MOD = 6
D_FF = 4 * D_MODEL
ROPE_BASE = 10000.0

HY_WIDTH = D_MODEL // 2
HY_EMB = 33
HY_BANDS = (HY_EMB - 1) // 2
HY_HIDDEN = 64

ATT_HEADS = 8
ATT_KV_HEADS = 2
ATT_GROUP = ATT_HEADS // ATT_KV_HEADS
HEAD_DIM = 64
ATT_WIDTH = ATT_HEADS * HEAD_DIM
KV_WIDTH = ATT_KV_HEADS * HEAD_DIM
BLOCK = 128

RET_HEADS = 4
RET_DK = D_MODEL // RET_HEADS
RET_DV = 2 * RET_DK
RET_QK = RET_HEADS * RET_DK
RET_V = RET_HEADS * RET_DV
OD_IN = 2 * RET_QK + 3 * RET_V

LOG2E = 1.4426950408889634
LANES = 128
MIB = 1024 * 1024

EV_U = 3 * HY_WIDTH
EV_QKV = ATT_WIDTH + 4 * LANES
EV_COLS = EV_U + EV_QKV

FFT_N = 2 * SEQ
FFT_R = 128
FFT_PITCH = FFT_R + 8
SEQ_P = SEQ // FFT_R * FFT_PITCH
CONV_ROWS = 64
CONV_PAD = 8
SUB_ROWS = 256
MIX_SUB_ROWS = 512
FFT_UNROLL = 64
RET_C = 256
CAST_CHUNKS = 32


VMEM_SMALL_MIB = 48
VMEM_DENSE_MIB = 56
VMEM_FFT_MIB = 60


def _params(sem, vmem_mib):
    return pltpu.CompilerParams(dimension_semantics=sem, vmem_limit_bytes=vmem_mib * MIB)


@functools.lru_cache(maxsize=None)
def _rope2d_tables():
    quarter = HEAD_DIM // 4
    inv = ROPE_BASE ** (-np.arange(quarter, dtype=np.float64) / quarter)
    t = np.arange(SEQ)
    pos = np.stack([t // GRID_W, t % GRID_W], axis=1).astype(np.float64)
    lane = np.arange(HEAD_DIM)
    half = lane // (HEAD_DIM // 2)
    e = lane % (HEAD_DIM // 2)
    ang = pos[:, half] * inv[e % quarter][None, :]
    sign = np.where(e < quarter, -1.0, 1.0)[None, :]
    cos = np.tile(np.cos(ang), (1, 2)).astype(np.float32)
    sin = np.tile(np.sin(ang) * sign, (1, 2)).astype(np.float32)
    return cos, sin


@functools.lru_cache(maxsize=None)
def _rope1d_tables():
    n = RET_DK // 2
    inv = ROPE_BASE ** (-np.linspace(0.0, 1.0, n))
    pos = np.arange(CTX_LEN + SEQ, dtype=np.float64)
    ang = pos[:, None] * inv[None, :]
    cos, sin = np.cos(ang), np.sin(ang)

    def rev(a):
        return a.reshape(-1, RET_C, n)[:, ::-1].reshape(-1, n)

    cos2 = np.stack([cos, rev(cos)]).astype(np.float32)
    sin2 = np.stack([sin, rev(sin)]).astype(np.float32)
    return cos2, sin2


@functools.lru_cache(maxsize=None)
def _fft_mats():
    N, R = FFT_N, FFT_R
    k1 = np.arange(R)
    n2 = np.arange(R)[:, None, None]
    n1 = np.arange(R // 2)[None, None, :]
    n = np.concatenate([n2 + R * n1, (N - ((R - n2) % R + R * n1)) % N], axis=2)
    idx = (k1[None, :, None] * n) % N
    ang = 2.0 * np.pi * idx / N
    c, s = np.cos(ang), np.sin(ang)
    m1 = np.concatenate([c, -s], axis=1).astype(np.float32)
    m1i = np.concatenate([c.transpose(0, 2, 1), -s.transpose(0, 2, 1)], axis=2)
    m1i = m1i[:, :R // 2].astype(np.float32)
    a2 = 2.0 * np.pi * ((np.arange(R)[:, None] * np.arange(R)[None, :]) % R) / R
    fr, fi = np.cos(a2), -np.sin(a2)
    f2 = np.block([[fr, -fi], [fi, fr]]).astype(np.float32)
    g2 = (np.block([[fr, fi], [-fi, fr]]) / N).astype(np.float32)
    m1c = np.concatenate([c[:, :, :R // 2], -s[:, :, :R // 2]], axis=2).astype(np.float32)
    return m1, m1i, f2, g2, m1c


@functools.lru_cache(maxsize=None)
def _ctx_fft_mats():
    N = 2 * CTX_LEN
    k = np.arange(N)[:, None]
    n = np.arange(N)[None, :]
    ang = 2.0 * np.pi * ((k * n) % N) / N
    c, s = np.cos(ang), np.sin(ang)
    fwd = np.concatenate([c, -s], axis=0).astype(np.float32)
    inv = (np.concatenate([c, -s], axis=1)[:CTX_LEN] / N).astype(np.float32)
    src = np.concatenate([np.arange(CTX_LEN), (N - np.arange(CTX_LEN)) % N])
    return fwd, inv, fwd[:, src]


def _ada_kernel(c_ref, w_ref, b_ref, o_ref):
    c = c_ref[...]
    a = c * (1.0 / (1.0 + jnp.exp(-c)))
    o_ref[...] = _dot3(a, w_ref[...]) + b_ref[...]


def _ada(cvec, ada_w, ada_b):
    tn = 1536
    return pl.pallas_call(
        _ada_kernel,
        out_shape=jax.ShapeDtypeStruct((DEPTH, 8, N_MOD * D_MODEL), F32),
        grid=(DEPTH, N_MOD * D_MODEL // tn),
        in_specs=[
            pl.BlockSpec((8, D_MODEL), lambda i, j: (0, 0)),
            pl.BlockSpec((None, D_MODEL, tn), lambda i, j: (i, 0, j)),
            pl.BlockSpec((None, 1, tn), lambda i, j: (i, 0, j)),
        ],
        out_specs=pl.BlockSpec((None, 8, tn), lambda i, j: (i, 0, j)),
        compiler_params=_params(("parallel", "parallel"), VMEM_SMALL_MIB),
        name="ada_mod",
    )(cvec, ada_w, ada_b.reshape(DEPTH, 1, N_MOD * D_MODEL))


def _with_casts(body, n_in, n_out, n_cast):
    def kern(*refs):
        srcs = refs[n_in:n_in + n_cast]
        outs_at = n_in + n_cast
        dsts = refs[outs_at + n_out:outs_at + n_out + n_cast]
        for src, dst in zip(srcs, dsts):
            dst[...] = src[...].astype(dst.dtype)
        body(*refs[:n_in], *refs[outs_at:outs_at + n_out], *refs[outs_at + n_out + n_cast:])
    return kern


def _cast_jobs(stacked_weights, steps_per_row):
    in_specs, out_specs, out_shapes, args = [], [], [], []
    chunk = lambda b, i: jnp.minimum(b * steps_per_row + i, CAST_CHUNKS - 1)
    for w, lead in stacked_weights:
        _, rows, cols = w.shape
        blk = rows // CAST_CHUNKS
        in_specs.append(pl.BlockSpec((None, blk, cols), lambda b, i, lead=lead: (lead, chunk(b, i), 0)))
        out_specs.append(pl.BlockSpec((blk, cols), lambda b, i: (chunk(b, i), 0)))
        out_shapes.append(jax.ShapeDtypeStruct((rows, cols), BF16))
        args.append(w)
    return in_specs, out_specs, out_shapes, args


def _mod_row(ref, row):
    if isinstance(row, int):
        return ref[row:row + 1, :]
    return ref[pl.ds(row, 1), :]


def _norm_mod(x, g, shift, scale):
    y = x * lax.rsqrt(jnp.mean(x * x, axis=-1, keepdims=True) + EPS)
    return (y * g) * (1.0 + scale) + shift


def _rope_tile(x, cos, sin_signed):
    lane = lax.broadcasted_iota(jnp.int32, x.shape, 1)
    first = (lane % 32) < 16
    partner = jnp.where(first, pltpu.roll(x, LANES - 16, axis=1), pltpu.roll(x, 16, axis=1))
    return x * cos + partner * sin_signed


def _project0(x_ref, g, sh, sc, w_ref, cos_ref, sin_ref, u_ref, qkv_ref, rope):
    tn = 512
    n_rot = (ATT_WIDTH + 2 * LANES) // LANES
    sub = min(x_ref.shape[0], SUB_ROWS)
    for s in range(x_ref.shape[0] // sub):
        rows = slice(s * sub, (s + 1) * sub)
        urows = slice(CONV_PAD + s * sub, CONV_PAD + (s + 1) * sub)
        h = _norm_mod(x_ref[rows, :], g, sh, sc).astype(BF16)
        for j in range(EV_U // tn):
            u_ref[urows, j * tn:(j + 1) * tn] = jnp.dot(h, w_ref[:, j * tn:(j + 1) * tn],
                                                        preferred_element_type=F32)
        for j in range(EV_QKV // tn):
            y = jnp.dot(h, w_ref[:, EV_U + j * tn:EV_U + (j + 1) * tn], preferred_element_type=F32)
            for t in range(tn // LANES):
                yt = y[:, t * LANES:(t + 1) * LANES]
                if rope and j * (tn // LANES) + t < n_rot:
                    yt = _rope_tile(yt, cos_ref[rows, :], sin_ref[rows, :])
                qkv_ref[rows, j * tn + t * LANES:j * tn + (t + 1) * LANES] = yt.astype(BF16)


def _short_conv_gate(u_ref, cw_ref, cb_ref, bias_ref, z_ref, x0_ref, zb_ref):
    tm = u_ref.shape[0] - 2 * CONV_PAD
    rc = CONV_ROWS
    z_pitched = z_ref.shape[0] != tm

    def conv(r0, c0):
        cols = slice(c0, c0 + LANES)
        taps = [u_ref[CONV_PAD - 1 + k + r0:CONV_PAD - 1 + k + r0 + rc, cols] for k in range(3)]
        return (taps[0] * cw_ref[0:1, cols] + taps[1] * cw_ref[1:2, cols] + taps[2] * cw_ref[2:3, cols]
                + cb_ref[:, cols])

    for r0 in range(0, tm, rc):
        rows = slice(r0, r0 + rc)
        z0 = r0 // FFT_R * FFT_PITCH + r0 % FFT_R if z_pitched else r0
        for c0 in range(0, HY_WIDTH, LANES):
            cols = slice(c0, c0 + LANES)
            x0 = conv(r0, c0)
            z = conv(r0, 2 * HY_WIDTH + c0) * conv(r0, HY_WIDTH + c0)
            z_ref[z0:z0 + rc, cols] = z
            x0_ref[rows, cols] = x0
            zb_ref[rows, cols] = x0 * (z * bias_ref[:, cols])
    if z_pitched:
        for k in range(tm // FFT_R):
            z_ref[k * FFT_PITCH + FFT_R:(k + 1) * FFT_PITCH, :] = jnp.zeros((FFT_PITCH - FFT_R, HY_WIDTH), F32)


def _inproj0_kernel(x_ref, g_ref, sh_ref, sc_ref, w_ref, cos_ref, sin_ref, cw_ref, cb_ref, bias_ref,
                    z_ref, x0_ref, zb_ref, qkv_ref, u_even, u_odd, *, row, rope, n_tiles):
    r = pl.program_id(0) if row is None else row
    i = pl.program_id(1)
    g, sh, sc = g_ref[...], _mod_row(sh_ref, r), _mod_row(sc_ref, r)
    conv_refs = (cw_ref, cb_ref, bias_ref, z_ref, x0_ref, zb_ref)
    zero_row = jnp.zeros((1, EV_U), F32)
    tm = x_ref.shape[0]
    above, first, last, below = CONV_PAD - 1, CONV_PAD, CONV_PAD + tm - 1, CONV_PAD + tm

    if n_tiles == 1:
        u_even[above:first, :] = zero_row
        u_even[below:below + 1, :] = zero_row
        _project0(x_ref, g, sh, sc, w_ref, cos_ref, sin_ref, u_even, qkv_ref, rope)
        _short_conv_gate(u_even, *conv_refs)
        return

    @pl.when(i == 0)
    def _():
        u_even[...] = jnp.zeros_like(u_even)
        u_odd[...] = jnp.zeros_like(u_odd)

    def step(u_new, u_old):
        j = i - 1
        u_old[above:first, :] = jnp.where(j > 0, u_new[last:last + 1, :], zero_row)
        _project0(x_ref, g, sh, sc, w_ref, cos_ref, sin_ref, u_new, qkv_ref, rope)
        u_old[below:below + 1, :] = jnp.where(j < n_tiles - 1, u_new[first:first + 1, :], zero_row)
        _short_conv_gate(u_old, *conv_refs)

    @pl.when((i % 2 == 0) & (i < n_tiles))
    def _():
        step(u_even, u_odd)

    @pl.when((i % 2 == 1) & (i < n_tiles))
    def _():
        step(u_odd, u_even)

    @pl.when(i == n_tiles)
    def _():
        bufs = (u_even, u_odd)
        u_last, u_before = bufs[(n_tiles - 1) % 2], bufs[n_tiles % 2]
        u_last[above:first, :] = u_before[last:last + 1, :]
        u_last[below:below + 1, :] = zero_row
        _short_conv_gate(u_last, *conv_refs)


def _inproj0(x, g, mod, w, conv_w, conv_b, bias, layer, *, is_ctx, tm, cast=()):
    G, R, _ = x.shape
    n_tiles = R // tm
    n_steps = n_tiles + (1 if n_tiles > 1 else 0)
    cos, sin = _rope2d_tables()
    cos, sin = jnp.asarray(cos), jnp.asarray(sin)
    proj_tile = lambda i: jnp.minimum(i, n_tiles - 1)
    conv_tile = (lambda i: jnp.maximum(i - 1, 0)) if n_tiles > 1 else (lambda i: i)
    if is_ctx:
        tab = pl.BlockSpec((tm, LANES), lambda b, i: (0, 0))
    else:
        tab = pl.BlockSpec((tm, LANES), lambda b, i: (proj_tile(i), 0))
    const = lambda a: pl.BlockSpec(a.shape, lambda b, i: (0, 0))
    hy_out = jax.ShapeDtypeStruct((G, R, HY_WIDTH), F32)
    hy_spec = pl.BlockSpec((None, tm, HY_WIDTH), lambda b, i: (b, conv_tile(i), 0))
    if R == SEQ:
        z_out = jax.ShapeDtypeStruct((G, SEQ_P, HY_WIDTH), F32)
        z_spec = pl.BlockSpec((None, tm // FFT_R * FFT_PITCH, HY_WIDTH), lambda b, i: (b, conv_tile(i), 0))
    else:
        z_out, z_spec = hy_out, hy_spec
    kern = functools.partial(_inproj0_kernel, row=2 if is_ctx else None, rope=not is_ctx, n_tiles=n_tiles)
    in_specs = [
        pl.BlockSpec((None, tm, D_MODEL), lambda b, i: (b, proj_tile(i), 0)),
        pl.BlockSpec((1, D_MODEL), lambda b, i: (0, 0)),
        pl.BlockSpec((None, 8, D_MODEL), lambda b, i: (layer, 0, 0)),
        pl.BlockSpec((None, 8, D_MODEL), lambda b, i: (layer, 0, 1)),
        pl.BlockSpec((D_MODEL, EV_COLS), lambda b, i: (0, 0), pipeline_mode=pl.Buffered(1)),
        tab, tab, const(conv_w), const(conv_b), const(bias),
    ]
    out_specs = [z_spec, hy_spec, hy_spec, pl.BlockSpec((None, tm, EV_QKV), lambda b, i: (b, proj_tile(i), 0))]
    out_shape = [z_out, hy_out, hy_out, jax.ShapeDtypeStruct((G, R, EV_QKV), BF16)]
    c_in, c_out, c_shapes, c_args = _cast_jobs(cast, n_steps)
    if cast:
        assert G * n_steps >= CAST_CHUNKS
        kern = _with_casts(kern, len(in_specs), len(out_specs), len(cast))
    res = pl.pallas_call(
        kern,
        out_shape=out_shape + c_shapes,
        grid=(G, n_steps),
        in_specs=in_specs + c_in,
        out_specs=out_specs + c_out,
        scratch_shapes=[pltpu.VMEM((tm + 2 * CONV_PAD, EV_U), F32)] * 2,
        compiler_params=_params(("arbitrary", "arbitrary") if cast else ("parallel", "arbitrary"), VMEM_SMALL_MIB),
        name="inproj0_ctx" if is_ctx else "inproj0",
    )(x, g, mod, mod, w, cos, sin, conv_w, conv_b, bias, *c_args)
    return res[:4], res[4:]


def _inproj1_kernel(x_ref, g_ref, sh_ref, sc_ref, w_ref, o_ref, *, row, tn):
    r = pl.program_id(0) if row is None else row
    g, sh, sc = g_ref[...], _mod_row(sh_ref, r), _mod_row(sc_ref, r)
    sub = min(x_ref.shape[0], SUB_ROWS)
    for s in range(x_ref.shape[0] // sub):
        rows = slice(s * sub, (s + 1) * sub)
        h = _norm_mod(x_ref[rows, :], g, sh, sc).astype(BF16)
        for j in range(w_ref.shape[1] // tn):
            cols = slice(j * tn, (j + 1) * tn)
            o_ref[rows, cols] = jnp.dot(h, w_ref[:, cols], preferred_element_type=F32).astype(o_ref.dtype)


def _inproj1(x, g, mod, w, layer, *, is_ctx, tm, tn, cast=()):
    G, R, _ = x.shape
    N = w.shape[1]
    kern = functools.partial(_inproj1_kernel, row=2 if is_ctx else None, tn=tn)
    in_specs = [
        pl.BlockSpec((None, tm, D_MODEL), lambda b, i: (b, i, 0)),
        pl.BlockSpec((1, D_MODEL), lambda b, i: (0, 0)),
        pl.BlockSpec((None, 8, D_MODEL), lambda b, i: (layer, 0, 0)),
        pl.BlockSpec((None, 8, D_MODEL), lambda b, i: (layer, 0, 1)),
        pl.BlockSpec((D_MODEL, N), lambda b, i: (0, 0), pipeline_mode=pl.Buffered(1)),
    ]
    c_in, c_out, c_shapes, c_args = _cast_jobs(cast, R // tm)
    if cast:
        assert G * (R // tm) >= CAST_CHUNKS
        kern = _with_casts(kern, len(in_specs), 1, len(cast))
    res = pl.pallas_call(
        kern,
        out_shape=[jax.ShapeDtypeStruct((G, R, N), BF16)] + c_shapes,
        grid=(G, R // tm),
        in_specs=in_specs + c_in,
        out_specs=[pl.BlockSpec((None, tm, N), lambda b, i: (b, i, 0))] + c_out,
        compiler_params=_params(("arbitrary", "arbitrary") if cast else ("parallel", "parallel"), VMEM_SMALL_MIB),
        name="inproj1_ctx" if is_ctx else "inproj1",
    )(x, g, mod, mod, w, *c_args)
    return res[0], res[1:]


def _split_bf16(a):
    hi = a.astype(BF16)
    return hi, (a - hi.astype(F32)).astype(BF16)


def _dot3_split(a_hi, a_lo, b):
    b_hi, b_lo = _split_bf16(b)
    dot = functools.partial(jnp.dot, preferred_element_type=F32)
    return dot(a_hi, b_hi) + (dot(a_lo, b_hi) + dot(a_hi, b_lo))


def _dot3(a, b):
    return _dot3_split(*_split_bf16(a), b)


def _store_pitched(ref, lead, value):
    for k in range(value.shape[0] // FFT_R):
        ref[lead, k * FFT_PITCH:k * FFT_PITCH + FFT_R, :] = value[k * FFT_R:(k + 1) * FFT_R]
        ref[lead, k * FFT_PITCH + FFT_R:(k + 1) * FFT_PITCH, :] = jnp.zeros(
            (FFT_PITCH - FFT_R, value.shape[1]), value.dtype)


def _filter_kernel(w1t_ref, b1_ref, w2t_ref, b2_ref, fr_ref, w3_ref, dec_ref, bands_ref, h_ref, ss_ref, *,
                   L, tm, pitched):
    i = pl.program_id(0)
    t_row = (lax.broadcasted_iota(jnp.int32, (1, tm), 1) + i * tm).astype(F32) / L
    ang = (2.0 * math.pi * t_row) * bands_ref[...]
    t8 = jnp.where(lax.broadcasted_iota(jnp.int32, (8, tm), 0) == 0, t_row, 0.0)
    feat = jnp.concatenate([t8, jnp.cos(ang), -jnp.sin(ang)], axis=0)
    fr = fr_ref[...]
    hid = jnp.sin(fr * (jnp.dot(w1t_ref[...], feat, precision=HIGHEST, preferred_element_type=F32) + b1_ref[...]))
    hid = jnp.sin(fr * (jnp.dot(w2t_ref[...], hid, precision=HIGHEST, preferred_element_type=F32) + b2_ref[...]))
    hid_hi, hid_lo = _split_bf16(hid.T)
    m_col = lax.broadcasted_iota(jnp.int32, (tm, 1), 0) + i * tm
    t_col = m_col.astype(F32) / L
    energy = jnp.zeros((1, HY_WIDTH), F32)
    for side in range(2):
        h = _dot3_split(hid_hi, hid_lo, w3_ref[side]) * jnp.exp(-t_col * jnp.abs(dec_ref[side]))
        if side == 1:
            h = jnp.where(m_col == 0, 0.0, h)
        if pitched:
            _store_pitched(h_ref, side, h)
        else:
            h_ref[side] = h
        energy = energy + jnp.sum(h * h, axis=0, keepdims=True)

    @pl.when(i == 0)
    def _():
        ss_ref[...] = jnp.zeros_like(ss_ref)

    ss_ref[...] += jnp.broadcast_to(energy, ss_ref.shape)


def _hyena_filter(w1, b1, w2, b2, w3, freq, decay, *, L):
    tm = min(L, 1024)
    pitched = L == SEQ
    tm_out, l_out = (tm // FFT_R * FFT_PITCH, SEQ_P) if pitched else (tm, L)
    kern = functools.partial(_filter_kernel, L=L, tm=tm, pitched=pitched)
    w1t = jnp.concatenate([w1[0:1], jnp.zeros((7, HY_HIDDEN), F32), w1[1:]], axis=0).T
    col = lambda a: a.reshape(HY_HIDDEN, 1)
    w3s = jnp.stack([w3[:, :HY_WIDTH], w3[:, HY_WIDTH:]])
    bands = jnp.asarray(np.linspace(1e-4, HY_BANDS - 1, HY_BANDS).astype(np.float32).reshape(HY_BANDS, 1))
    args = (w1t, col(b1), w2.T, col(b2), col(freq), w3s, decay.reshape(2, 1, HY_WIDTH), bands)
    full = lambda a: pl.BlockSpec(a.shape, lambda i: (0,) * a.ndim)
    h, ss = pl.pallas_call(
        kern,
        out_shape=(jax.ShapeDtypeStruct((2, l_out, HY_WIDTH), F32), jax.ShapeDtypeStruct((8, HY_WIDTH), F32)),
        grid=(L // tm,),
        in_specs=[full(a) for a in args],
        out_specs=(pl.BlockSpec((2, tm_out, HY_WIDTH), lambda i: (0, i, 0)),
                   pl.BlockSpec((8, HY_WIDTH), lambda i: (0, 0))),
        compiler_params=_params(("arbitrary",), VMEM_SMALL_MIB),
        name="hyena_filter_%d" % L,
    )(*args)
    return h.reshape(2 * l_out, HY_WIDTH), ss


def _stack_bf16(re, im):
    return jnp.concatenate([re, im], axis=0).astype(BF16)


def _filter_fft_kernel(h_ref, ss_ref, m1_ref, f2_ref, o_ref, a_scr):
    R, P = FFT_R, FFT_PITCH
    inv_norm = lax.rsqrt(ss_ref[0:1, :])

    def stage1(n2, carry):
        neg = jnp.where(n2 == 0, 0, R - n2)
        xs = jnp.concatenate([h_ref[pl.ds(n2, R // 2, stride=P), :],
                              h_ref[pl.ds(SEQ_P + neg, R // 2, stride=P), :]], axis=0)
        xs = (xs * inv_norm).astype(BF16)
        a = jnp.dot(m1_ref[n2], xs, preferred_element_type=F32)
        base = pl.multiple_of(n2 * P, 8)
        a_scr[0, pl.ds(base, R), :] = a[:R]
        a_scr[1, pl.ds(base, R), :] = a[R:]
        return carry

    lax.fori_loop(0, R, stage1, 0, unroll=FFT_UNROLL)

    ct = a_scr.shape[-1]

    def stage2(pair, carry):
        k1s = (2 * pair, 2 * pair + 1)
        rhs = jnp.concatenate([_stack_bf16(a_scr[0, pl.ds(k1, R, stride=P), :],
                                           a_scr[1, pl.ds(k1, R, stride=P), :]) for k1 in k1s], axis=1)
        x = jnp.dot(f2_ref[...], rhs, preferred_element_type=F32)
        for d, k1 in enumerate(k1s):
            base = pl.multiple_of(k1 * R, R)
            o_ref[0, pl.ds(base, R), :] = x[:R, d * ct:(d + 1) * ct].astype(BF16)
            o_ref[1, pl.ds(base, R), :] = x[R:, d * ct:(d + 1) * ct].astype(BF16)
        return carry

    lax.fori_loop(0, R // 2, stage2, 0, unroll=FFT_UNROLL // 2)


def _filter_fft(h_circ, ss, m1, f2):
    ct = LANES
    once = pl.Buffered(1)
    return pl.pallas_call(
        _filter_fft_kernel,
        out_shape=jax.ShapeDtypeStruct((2, FFT_N, HY_WIDTH), BF16),
        grid=(HY_WIDTH // ct,),
        in_specs=[
            pl.BlockSpec((2 * SEQ_P, ct), lambda c: (0, c)),
            pl.BlockSpec((8, ct), lambda c: (0, c)),
            pl.BlockSpec(m1.shape, lambda c: (0, 0, 0), pipeline_mode=once),
            pl.BlockSpec(f2.shape, lambda c: (0, 0), pipeline_mode=once),
        ],
        out_specs=pl.BlockSpec((2, FFT_N, ct), lambda c: (0, 0, c)),
        scratch_shapes=[pltpu.VMEM((2, FFT_R * FFT_PITCH, ct), F32)],
        compiler_params=_params(("parallel",), VMEM_FFT_MIB),
        name="hyena_filter_fft",
    )(h_circ, ss, m1, f2)


def _fft_conv_kernel(z_ref, hs_ref, m1_ref, f2_ref, g2_ref, m1i_ref, y_ref, a_scr):
    R, P = FFT_R, FFT_PITCH
    ct = a_scr.shape[-1]

    def stage1(n2, carry):
        rows = pl.ds(n2, R // 2, stride=P)
        xs = jnp.concatenate([z_ref[0, rows, :], z_ref[1, rows, :]], axis=1).astype(BF16)
        zero = jnp.zeros_like(xs)
        rhs = jnp.concatenate([jnp.concatenate([xs, zero], axis=1),
                               jnp.concatenate([zero, xs], axis=1)], axis=0)
        t = jnp.dot(m1_ref[n2], rhs, preferred_element_type=F32)
        base = pl.multiple_of(n2 * P, 8)
        a_scr[0, pl.ds(base, R), :] = t[:, 0:ct] - t[:, 3 * ct:4 * ct]
        a_scr[1, pl.ds(base, R), :] = t[:, 2 * ct:3 * ct] + t[:, ct:2 * ct]
        return carry

    lax.fori_loop(0, R, stage1, 0, unroll=FFT_UNROLL)

    def stage2(pair, carry):
        k1s = (2 * pair, 2 * pair + 1)
        rhs = jnp.concatenate([_stack_bf16(a_scr[0, pl.ds(k1, R, stride=P), :],
                                           a_scr[1, pl.ds(k1, R, stride=P), :]) for k1 in k1s], axis=1)
        x = jnp.dot(f2_ref[...], rhs, preferred_element_type=F32)
        prods = []
        for d, k1 in enumerate(k1s):
            base = pl.multiple_of(k1 * R, R)
            hr = hs_ref[0, pl.ds(base, R), :].astype(F32)
            hi = hs_ref[1, pl.ds(base, R), :].astype(F32)
            xr, xi = x[:R, d * ct:(d + 1) * ct], x[R:, d * ct:(d + 1) * ct]
            prods.append(_stack_bf16(xr * hr - xi * hi, xr * hi + xi * hr))
        bq = jnp.dot(g2_ref[...], jnp.concatenate(prods, axis=1), preferred_element_type=F32)
        for d, k1 in enumerate(k1s):
            a_scr[0, pl.ds(k1, R, stride=P), :] = bq[:R, d * ct:(d + 1) * ct]
            a_scr[1, pl.ds(k1, R, stride=P), :] = bq[R:, d * ct:(d + 1) * ct]
        return carry

    lax.fori_loop(0, R // 2, stage2, 0, unroll=FFT_UNROLL // 2)

    def stage3(n2, carry):
        base = pl.multiple_of(n2 * P, 8)
        br = a_scr[0, pl.ds(base, R), :]
        bi = a_scr[1, pl.ds(base, R), :]
        rhs = jnp.concatenate([jnp.concatenate([br, bi], axis=1),
                               jnp.concatenate([bi, -br], axis=1)], axis=0).astype(BF16)
        y = jnp.dot(m1i_ref[n2], rhs, preferred_element_type=F32)
        rows = pl.ds(n2, R // 2, stride=P)
        y_ref[0, rows, :] = y[:, :ct]
        y_ref[1, rows, :] = y[:, ct:]
        return carry

    lax.fori_loop(0, R, stage3, 0, unroll=FFT_UNROLL)
    for b in range(2):
        for n1 in range(R // 2):
            y_ref[b, n1 * P + R:(n1 + 1) * P, :] = jnp.zeros((P - R, ct), F32)


def _fft_conv(z, hspec, m1, f2, g2, m1i):
    assert z.shape[0] == 2, "the batch pair is packed into one complex signal"
    ct = LANES
    once = pl.Buffered(1)
    return pl.pallas_call(
        _fft_conv_kernel,
        out_shape=jax.ShapeDtypeStruct((2, SEQ_P, HY_WIDTH), F32),
        grid=(HY_WIDTH // ct,),
        in_specs=[
            pl.BlockSpec((2, SEQ_P, ct), lambda c: (0, 0, c), pipeline_mode=once),
            pl.BlockSpec((2, FFT_N, ct), lambda c: (0, 0, c)),
            pl.BlockSpec(m1.shape, lambda c: (0, 0, 0), pipeline_mode=once),
            pl.BlockSpec(f2.shape, lambda c: (0, 0), pipeline_mode=once),
            pl.BlockSpec(g2.shape, lambda c: (0, 0), pipeline_mode=once),
            pl.BlockSpec(m1i.shape, lambda c: (0, 0, 0), pipeline_mode=once),
        ],
        out_specs=pl.BlockSpec((2, SEQ_P, ct), lambda c: (0, 0, c), pipeline_mode=once),
        scratch_shapes=[pltpu.VMEM((2, FFT_R * FFT_PITCH, ct), F32)],
        compiler_params=_params(("arbitrary",), VMEM_FFT_MIB),
        name="hyena_fft_conv",
    )(z, hspec, m1, f2, g2, m1i)


def _ctx_conv_kernel(z_ref, h_ref, ss_ref, fwd_ref, inv_ref, fwd_h_ref, y_ref):
    N = 2 * CTX_LEN
    hn = (h_ref[...] * lax.rsqrt(ss_ref[0:1, :])).astype(BF16)
    hs = jnp.dot(fwd_h_ref[...], hn, preferred_element_type=F32)
    zs = jnp.dot(fwd_ref[:, :CTX_LEN], z_ref[...].astype(BF16), preferred_element_type=F32)
    hr, hi, zr, zi = hs[:N], hs[N:], zs[:N], zs[N:]
    ys = _stack_bf16(zr * hr - zi * hi, zr * hi + zi * hr)
    y_ref[...] = jnp.dot(inv_ref[...], ys, preferred_element_type=F32)


def _ctx_conv(z, h_circ, ss, fwd, inv, fwd_h):
    G = z.shape[0]
    full = lambda a: pl.BlockSpec(a.shape, lambda b: (0,) * a.ndim)
    return pl.pallas_call(
        _ctx_conv_kernel,
        out_shape=jax.ShapeDtypeStruct((G, CTX_LEN, HY_WIDTH), F32),
        grid=(G,),
        in_specs=[pl.BlockSpec((None, CTX_LEN, HY_WIDTH), lambda b: (b, 0, 0)),
                  full(h_circ), full(ss), full(fwd), full(inv), full(fwd_h)],
        out_specs=pl.BlockSpec((None, CTX_LEN, HY_WIDTH), lambda b: (b, 0, 0)),
        compiler_params=_params(("parallel",), VMEM_SMALL_MIB),
        name="hyena_ctx_conv",
    )(z, h_circ, ss, fwd, inv, fwd_h)


def _kv_group(refs, g, col0):
    cols = slice(col0 + g * LANES, col0 + (g + 1) * LANES)
    return refs[0][:, cols] if len(refs) == 1 else jnp.concatenate([r[:, cols] for r in refs], axis=0)


def _attn_logits(q_ref, k_refs, g, masks, col0=0):
    low = lax.broadcasted_iota(jnp.int32, (BLOCK, LANES), 1) < HEAD_DIM
    zero = jnp.zeros((), BF16)
    parts = []
    for hh in range(ATT_GROUP):
        h = g * ATT_GROUP + hh
        tile = q_ref[:, (h // 2) * LANES:(h // 2 + 1) * LANES]
        parts.append(jnp.where(low if h % 2 == 0 else ~low, tile, zero))
    s = lax.dot_general(jnp.concatenate(parts, axis=0), _kv_group(k_refs, g, col0), (((1,), (1,)), ((), ())),
                        preferred_element_type=F32)
    if masks is not None:
        ok_prev, ok_next = masks
        s = jnp.concatenate([jnp.where(ok_prev, s[:, :BLOCK], NEG_INF), s[:, BLOCK:2 * BLOCK],
                             jnp.where(ok_next, s[:, 2 * BLOCK:3 * BLOCK], NEG_INF), s[:, 3 * BLOCK:]], axis=1)
    return s


def _attn_output(s, sink_ref, v_refs, g, o_ref, col0=0):
    low = lax.broadcasted_iota(jnp.int32, (BLOCK, LANES), 1) < HEAD_DIM
    low4 = jnp.concatenate([low] * ATT_GROUP, axis=0)
    v = _kv_group(v_refs, g, col0)
    v_aug = jnp.where(lax.broadcasted_iota(jnp.int32, v.shape, 1) < HEAD_DIM, v, jnp.ones((), BF16))
    sink = jnp.concatenate(
        [jnp.full((BLOCK, 1), sink_ref[g * ATT_GROUP + hh] * LOG2E, F32) for hh in range(ATT_GROUP)], axis=0)
    m = jnp.maximum(jnp.max(s, axis=-1, keepdims=True), sink)
    e = jnp.exp2(s - m).astype(BF16)
    o = jnp.dot(e, v_aug, preferred_element_type=F32) + jnp.where(low4, 0.0, jnp.exp2(sink - m))
    swapped = pltpu.roll(o, HEAD_DIM, axis=1)
    for pair in range(ATT_GROUP // 2):
        ev = slice((2 * pair) * BLOCK, (2 * pair + 1) * BLOCK)
        od = slice((2 * pair + 1) * BLOCK, (2 * pair + 2) * BLOCK)
        even = o[ev] / swapped[ev]
        odd = swapped[od] / o[od]
        t = g * (ATT_GROUP // 2) + pair
        o_ref[:, t * LANES:(t + 1) * LANES] = jnp.where(low, even, odd).astype(o_ref.dtype)


def _attn_ctx_kernel(sink_ref, q_ref, kx_ref, vx_ref, o_ref):
    for g in range(ATT_KV_HEADS):
        _attn_output(_attn_logits(q_ref, (kx_ref,), g, None), sink_ref, (vx_ref,), g, o_ref)


def _attn_local_kernel(sink_ref, q_ref, kv_m2, kv_m1, kv_0, kv_p1, kv_ctx, o_ref, s_even, s_odd, *, n_blocks):
    n = pl.program_id(1)
    k_refs, v_refs, v_col = (kv_m1, kv_0, kv_p1, kv_ctx), (kv_m2, kv_m1, kv_0, kv_ctx), 2 * LANES

    @pl.when(n == 0)
    def _():
        s_odd[...] = jnp.zeros_like(s_odd)

    def step(s_new, s_prev):
        nq = jnp.minimum(n, n_blocks - 1)
        qi = lax.broadcasted_iota(jnp.int32, (ATT_GROUP * BLOCK, BLOCK), 0) % BLOCK
        ki = lax.broadcasted_iota(jnp.int32, (ATT_GROUP * BLOCK, BLOCK), 1)
        masks = ((ki >= qi) & (nq > 0), (ki <= qi) & (nq < n_blocks - 1))
        for g in range(ATT_KV_HEADS):
            s_new[g] = _attn_logits(q_ref, k_refs, g, masks)
        for g in range(ATT_KV_HEADS):
            _attn_output(s_prev[g], sink_ref, v_refs, g, o_ref, v_col)

    @pl.when(n % 2 == 0)
    def _():
        step(s_even, s_odd)

    @pl.when(n % 2 == 1)
    def _():
        step(s_odd, s_even)


def _attention(sink, qkv, qkv_ctx, *, local):
    src = qkv if local else qkv_ctx
    G, R, _ = src.shape
    nb = R // BLOCK
    kcol, vcol = ATT_WIDTH // 256, ATT_WIDTH // 256 + 1
    ctx_k = pl.BlockSpec((None, CTX_LEN, 256), lambda b, n: (b, 0, kcol))
    ctx_v = pl.BlockSpec((None, CTX_LEN, 256), lambda b, n: (b, 0, vcol))
    smem = pl.BlockSpec(memory_space=pltpu.SMEM)
    out_shape = jax.ShapeDtypeStruct((G, R, ATT_WIDTH), BF16)
    if not local:
        rows = pl.BlockSpec((None, BLOCK, ATT_WIDTH), lambda b, n: (b, n, 0))
        return pl.pallas_call(
            _attn_ctx_kernel, out_shape=out_shape, grid=(G, nb),
            in_specs=[smem, rows, ctx_k, ctx_v], out_specs=rows,
            compiler_params=_params(("parallel", "parallel"), VMEM_SMALL_MIB), name="attn_ctx",
        )(sink, qkv_ctx, qkv_ctx, qkv_ctx)

    q_blk = lambda n: jnp.minimum(n, nb - 1)
    o_blk = lambda n: jnp.maximum(n - 1, 0)

    kv_w = EV_QKV - ATT_WIDTH
    kv_col = ATT_WIDTH // kv_w

    def kv_rows(off):
        return pl.BlockSpec((None, BLOCK, kv_w), lambda b, n: (b, jnp.clip(n + off, 0, nb - 1), kv_col))

    in_specs = [smem, pl.BlockSpec((None, BLOCK, ATT_WIDTH), lambda b, n: (b, q_blk(n), 0)),
                kv_rows(-2), kv_rows(-1), kv_rows(0), kv_rows(1),
                pl.BlockSpec((None, CTX_LEN, kv_w), lambda b, n: (b, 0, kv_col))]
    logits_scratch = pltpu.VMEM((ATT_KV_HEADS, ATT_GROUP * BLOCK, 3 * BLOCK + CTX_LEN), F32)
    return pl.pallas_call(
        functools.partial(_attn_local_kernel, n_blocks=nb),
        out_shape=out_shape,
        grid=(G, nb + 1),
        in_specs=in_specs,
        out_specs=pl.BlockSpec((None, BLOCK, ATT_WIDTH), lambda b, n: (b, o_blk(n), 0)),
        scratch_shapes=[logits_scratch, logits_scratch],
        compiler_params=_params(("parallel", "arbitrary"), VMEM_SMALL_MIB),
        name="attn_local",
    )(sink, qkv, qkv, qkv, qkv, qkv, qkv_ctx)


def _mixer_out0(rows, x0_ref, y_ref, zb_ref, att_ref, wo_ref):
    if y_ref.shape[0] != x0_ref.shape[0]:
        y = jnp.concatenate([y_ref[r // FFT_R * FFT_PITCH:r // FFT_R * FFT_PITCH + FFT_R, :]
                             for r in range(rows.start, rows.stop, FFT_R)], axis=0)
    else:
        y = y_ref[rows, :]
    hy = (x0_ref[rows, :] * y + zb_ref[rows, :]).astype(BF16)
    return (jnp.dot(hy, wo_ref[:HY_WIDTH, :], preferred_element_type=F32)
            + jnp.dot(att_ref[rows, :], wo_ref[HY_WIDTH:, :], preferred_element_type=F32))


def _mixer_out1(rows, of_ref, ob_ref, wo_ref):
    a = (of_ref[rows, :].astype(F32) + ob_ref[rows, :].astype(F32)).astype(BF16)
    return jnp.dot(a, wo_ref[...], preferred_element_type=F32)


def _mix_kernel(*refs, mixer_out, n_mix, row, fc, final_norm):
    x_ref = refs[0]
    mix_refs = refs[1:2 + n_mix]
    gta_ref, gm_ref, shm_ref, scm_ref, gtm_ref, w1_ref, w2_ref, fg_ref, o_ref, a_scr = refs[2 + n_mix:]
    r = pl.program_id(0) if row is None else row
    gta, gtm = _mod_row(gta_ref, r), _mod_row(gtm_ref, r)
    gm, shm, scm = gm_ref[...], _mod_row(shm_ref, r), _mod_row(scm_ref, r)
    sub = min(x_ref.shape[0], MIX_SUB_ROWS)
    for s in range(x_ref.shape[0] // sub):
        rows = slice(s * sub, (s + 1) * sub)
        x1 = x_ref[rows, :] + gta * mixer_out(rows, *mix_refs)
        h = _norm_mod(x1, gm, shm, scm).astype(BF16)
        for c in range(D_FF // fc):
            a = jnp.maximum(jnp.dot(h, w1_ref[:, c * fc:(c + 1) * fc], preferred_element_type=F32), 0.0)
            a_scr[rows, c * fc:(c + 1) * fc] = (a * a).astype(BF16)
        out = x1 + gtm * jnp.dot(a_scr[rows, :], w2_ref[...], preferred_element_type=F32)
        if final_norm:
            out = (out * lax.rsqrt(jnp.mean(out * out, axis=-1, keepdims=True) + EPS)) * fg_ref[...]
        o_ref[rows, :] = out


def _mix_mlp(kind, x, mix_in, wo, mod, gm, w1, w2, fg, layer, *, is_ctx, tm, fc, final_norm, cast=()):
    G, R, _ = x.shape
    row_spec = lambda w: pl.BlockSpec((None, tm, w), lambda b, i: (b, i, 0))
    modk = lambda k: pl.BlockSpec((None, 8, D_MODEL), lambda b, i: (layer, 0, k))
    vec = pl.BlockSpec((1, D_MODEL), lambda b, i: (0, 0))
    resident = lambda a: pl.BlockSpec(a.shape, lambda b, i: (0, 0), pipeline_mode=pl.Buffered(1))
    if kind == 0:
        mixer_out = _mixer_out0
        y_rows = tm if mix_in[1].shape[1] == R else tm // FFT_R * FFT_PITCH
        y_spec = pl.BlockSpec((None, y_rows, HY_WIDTH), lambda b, i: (b, i, 0))
        mix_specs = [row_spec(HY_WIDTH), y_spec, row_spec(HY_WIDTH), row_spec(ATT_WIDTH)]
    else:
        mixer_out = _mixer_out1
        mix_specs = [row_spec(RET_V)] * 2
    kern = functools.partial(_mix_kernel, mixer_out=mixer_out, n_mix=len(mix_in), row=2 if is_ctx else None,
                             fc=fc, final_norm=final_norm)
    in_specs = [row_spec(D_MODEL)] + mix_specs + [
        resident(wo), modk(2), vec, modk(3), modk(4), modk(5), resident(w1), resident(w2), vec]
    c_in, c_out, c_shapes, c_args = _cast_jobs(cast, R // tm)
    if cast:
        assert G * (R // tm) >= CAST_CHUNKS
        kern = _with_casts(kern, len(in_specs), 1, len(cast))
    res = pl.pallas_call(
        kern,
        out_shape=[jax.ShapeDtypeStruct((G, R, D_MODEL), F32)] + c_shapes,
        grid=(G, R // tm),
        in_specs=in_specs + c_in,
        out_specs=[row_spec(D_MODEL)] + c_out,
        scratch_shapes=[pltpu.VMEM((tm, D_FF), BF16)],
        compiler_params=_params(("arbitrary", "arbitrary") if cast else ("parallel", "parallel"), VMEM_DENSE_MIB),
        name="mix_mlp%d%s" % (kind, "_ctx" if is_ctx else ""),
    )(x, *mix_in, wo, mod, gm, mod, mod, mod, w1, w2, fg, *c_args)
    return res[0], res[1:]


def _ret_kernel(lr_ref, qkv_c, qkv_f, g_f, qkv_b, g_b, cos_ref, sin_ref, of_ref, ob_ref,
                state, dmask, xi, zeta, gch):
    j = pl.program_id(1)
    C = RET_C
    kscale = RET_DK ** -0.5

    @pl.when(j == 0)
    def _():
        state[...] = jnp.zeros_like(state)
        row = lax.broadcasted_iota(jnp.int32, (C, C), 0).astype(F32)
        col = lax.broadcasted_iota(jnp.int32, (C, C), 1).astype(F32)
        for d in range(2):
            e = row - col if d == 0 else col - row
            p = row if d == 0 else (C - 1) - row
            for h in range(RET_HEADS):
                lg = -jnp.exp(jnp.full((C, C), lr_ref[d, h], F32))
                dmask[d, h] = jnp.where(e >= 0, jnp.exp(e * lg), 0.0) * kscale
                xi[d, h] = jnp.exp((p + 1.0) * lg)
                zeta[d, h] = jnp.exp(((C - 1) - p) * lg) * kscale
                gch[d, h] = jnp.exp(C * -jnp.exp(jnp.full((8, C), lr_ref[d, h], F32)))

    half = RET_DK // 2

    def chain(d, h, src_ref, g_ref, o_ref):
        cos, sin = cos_ref[d], sin_ref[d]

        def rot(c0):
            t1 = src_ref[:, c0:c0 + half].astype(F32)
            t2 = src_ref[:, c0 + half:c0 + RET_DK].astype(F32)
            return jnp.concatenate([t1 * cos - t2 * sin, t2 * cos + t1 * sin], axis=1)

        q = rot(h * RET_DK)
        k = rot(RET_QK + h * RET_DK)
        v = src_ref[:, 2 * RET_QK + h * RET_DV:2 * RET_QK + (h + 1) * RET_DV]
        inner = lax.dot_general(q.astype(BF16), k.astype(BF16), (((1,), (1,)), ((), ())),
                                preferred_element_type=F32) * dmask[d, h]
        s_old = state[d, h]
        kv = lax.dot_general((k * zeta[d, h]).astype(BF16), v, (((0,), (0,)), ((), ())),
                             preferred_element_type=F32)
        state[d, h] = gch[d, h, 0:1, 0:1] * s_old + kv
        if o_ref is not None:
            o = (jnp.dot(inner.astype(BF16), v, preferred_element_type=F32)
                 + jnp.dot((q * xi[d, h]).astype(BF16), s_old.astype(BF16), preferred_element_type=F32))
            on = o * lax.rsqrt(jnp.mean(o * o, axis=-1, keepdims=True) + EPS)
            gate = g_ref[:, h * RET_DV:(h + 1) * RET_DV]
            half_gate = gate * 0.5
            silu = half_gate + half_gate * jnp.tanh(half_gate)
            o_ref[:, h * RET_DV:(h + 1) * RET_DV] = (silu.astype(F32) * on).astype(o_ref.dtype)

    @pl.when(j == 0)
    def _():
        for d in range(2):
            for h in range(RET_HEADS):
                chain(d, h, qkv_c, None, None)

    @pl.when(j > 0)
    def _():
        for h in range(RET_HEADS):
            chain(0, h, qkv_f, g_f, of_ref)
            chain(1, h, qkv_b, g_b, ob_ref)


def _retention(log_rate, p_ctx, p_lat):
    G = p_lat.shape[0]
    C = RET_C
    n_lat = SEQ // C
    cos, sin = _rope1d_tables()
    cos, sin = jnp.asarray(cos), jnp.asarray(sin)

    def fwd_chunk(j):
        return jnp.maximum(j - 1, 0)

    def bwd_chunk(j):
        return n_lat - 1 - jnp.maximum(j - 1, 0)

    qkv_w = 2 * RET_QK + RET_V
    gcol = qkv_w // RET_V
    tab = pl.BlockSpec((2, C, RET_DK // 2), lambda b, j: (0, j, 0))
    in_specs = [
        pl.BlockSpec(memory_space=pltpu.SMEM),
        pl.BlockSpec((None, C, qkv_w), lambda b, j: (b, 0, 0)),
        pl.BlockSpec((None, C, qkv_w), lambda b, j: (b, fwd_chunk(j), 0)),
        pl.BlockSpec((None, C, RET_V), lambda b, j: (b, fwd_chunk(j), gcol)),
        pl.BlockSpec((None, C, qkv_w), lambda b, j: (b, bwd_chunk(j), 0)),
        pl.BlockSpec((None, C, RET_V), lambda b, j: (b, bwd_chunk(j), gcol + 1)),
        tab, tab,
    ]
    out = jax.ShapeDtypeStruct((G, SEQ, RET_V), BF16)
    per_chain = lambda *tail: pltpu.VMEM((2, RET_HEADS) + tail, F32)
    return pl.pallas_call(
        _ret_kernel,
        out_shape=(out, out),
        grid=(G, 1 + n_lat),
        in_specs=in_specs,
        out_specs=(pl.BlockSpec((None, C, RET_V), lambda b, j: (b, fwd_chunk(j), 0)),
                   pl.BlockSpec((None, C, RET_V), lambda b, j: (b, bwd_chunk(j), 0))),
        scratch_shapes=[per_chain(RET_DK, RET_DV), per_chain(C, C), per_chain(C, C), per_chain(C, C),
                        per_chain(8, C)],
        compiler_params=_params(("parallel", "arbitrary"), VMEM_SMALL_MIB),
        name="retention",
    )(log_rate, p_ctx, p_lat, p_lat, p_lat, p_lat, cos, sin)


def _ev_weight(w_in):
    i_q = EV_U
    i_k = i_q + ATT_WIDTH
    i_v = i_k + KV_WIDTH
    cols = [w_in[:, :i_q], w_in[:, i_q:i_k] * (HEAD_DIM ** -0.5 * LOG2E)]
    for base in (i_k, i_v):
        for g in range(ATT_KV_HEADS):
            part = w_in[:, base + g * HEAD_DIM: base + (g + 1) * HEAD_DIM]
            cols += [part, part]
    return jnp.concatenate([c.astype(BF16) for c in cols], axis=1)


def kernel(x, c, ctx, c_ctx, ada_w, ada_b, norm_mix_g, norm_mlp_g, mlp_w1, mlp_w2, ev_w_in, ev_w_out, hy_conv_w, hy_conv_b, hy_w1, hy_b1, hy_w2, hy_b2, hy_w3, hy_freq, hy_decay, hy_bias, attn_sink, od_w_in, od_w_out, ret_log_rate, final_g):
    D = D_MODEL
    cvec = jnp.concatenate([c, c_ctx[None, :], jnp.zeros((8 - BATCH - 1, D), F32)], axis=0)
    mod = _ada(cvec, ada_w, ada_b)

    m1, m1i, f2, g2, m1c = (jnp.asarray(a).astype(BF16) for a in _fft_mats())
    cfwd, cinv, cfwd_h = (jnp.asarray(a).astype(BF16) for a in _ctx_fft_mats())
    fg = final_g.reshape(1, D)

    gmix = norm_mix_g[0].reshape(1, D)
    gmlp = norm_mlp_g[0].reshape(1, D)
    w_in = _ev_weight(ev_w_in[0])
    conv = (hy_conv_w[0], hy_conv_b[0].reshape(1, EV_U), hy_bias[0].reshape(1, HY_WIDTH))
    (z_l, x0_l, zb_l, qkv_l), (wo, w1, w2) = _inproj0(
        x, gmix, mod, w_in, *conv, 0, is_ctx=False, tm=512, cast=((ev_w_out, 0), (mlp_w1, 0), (mlp_w2, 0)))
    (z_c, x0_c, zb_c, qkv_c), _ = _inproj0(ctx, gmix, mod, w_in, *conv, 0, is_ctx=True, tm=CTX_LEN)

    filt = (hy_w1[0], hy_b1[0], hy_w2[0], hy_b2[0], hy_w3[0], hy_freq[0], hy_decay[0])
    h_l, ss_l = _hyena_filter(*filt, L=SEQ)
    h_c, ss_c = _hyena_filter(*filt, L=CTX_LEN)
    hspec = _filter_fft(h_l, ss_l, m1, f2)
    y_l = _fft_conv(z_l, hspec, m1c, f2, g2, m1i)
    y_c = _ctx_conv(z_c, h_c, ss_c, cfwd, cinv, cfwd_h)

    sink = attn_sink[0]
    att_l = _attention(sink, qkv_l, qkv_c, local=True)
    att_c = _attention(sink, qkv_l, qkv_c, local=False)

    x1, (w_in,) = _mix_mlp(0, x, (x0_l, y_l, zb_l, att_l), wo, mod, gmlp, w1, w2, fg, 0,
                           is_ctx=False, tm=512, fc=512, final_norm=False, cast=((od_w_in, 0),))
    ctx1, _ = _mix_mlp(0, ctx, (x0_c, y_c, zb_c, att_c), wo, mod, gmlp, w1, w2, fg, 0,
                       is_ctx=True, tm=CTX_LEN, fc=512, final_norm=False)

    gmix = norm_mix_g[1].reshape(1, D)
    gmlp = norm_mlp_g[1].reshape(1, D)
    p_l, (wo, w1, w2) = _inproj1(x1, gmix, mod, w_in, 1, is_ctx=False, tm=512, tn=512,
                                 cast=((od_w_out, 0), (mlp_w1, 1), (mlp_w2, 1)))
    p_c, _ = _inproj1(ctx1, gmix, mod, w_in, 1, is_ctx=True, tm=CTX_LEN, tn=512)
    o_fwd, o_bwd = _retention(ret_log_rate[0], p_c, p_l)

    out, _ = _mix_mlp(1, x1, (o_fwd, o_bwd), wo, mod, gmlp, w1, w2, fg, 1,
                      is_ctx=False, tm=512, fc=512, final_norm=True)
    return out
```

```python
import functools
import math

import numpy as np
import jax
import jax.numpy as jnp
from jax import lax
from jax.experimental import pallas as pl
from jax.experimental.pallas import tpu as pltpu

F32 = jnp.float32
BF16 = jnp.bfloat16
HIGHEST = lax.Precision.HIGHEST

D_MODEL = 1024
BATCH = 2
SEQ = 8192
DEPTH = 2
GRID_W = 64
CTX_LEN = 256
EPS = 1e-6
NEG_INF = -1e30
N_MOD = 6
D_FF = 4 * D_MODEL
ROPE_BASE = 10000.0

HY_WIDTH = D_MODEL // 2
HY_EMB = 33
HY_BANDS = (HY_EMB - 1) // 2
HY_HIDDEN = 64

ATT_HEADS = 8
ATT_KV_HEADS = 2
ATT_GROUP = ATT_HEADS // ATT_KV_HEADS
HEAD_DIM = 64
ATT_WIDTH = ATT_HEADS * HEAD_DIM
KV_WIDTH = ATT_KV_HEADS * HEAD_DIM
BLOCK = 128

RET_HEADS = 4
RET_DK = D_MODEL // RET_HEADS
RET_DV = 2 * RET_DK
RET_QK = RET_HEADS * RET_DK
RET_V = RET_HEADS * RET_DV
OD_IN = 2 * RET_QK + 3 * RET_V

LOG2E = 1.4426950408889634
LANES = 128
MIB = 1024 * 1024

EV_U = 3 * HY_WIDTH
EV_QKV = ATT_WIDTH + 4 * LANES
EV_COLS = EV_U + EV_QKV

FFT_N = 2 * SEQ
FFT_R = 128
FFT_PITCH = FFT_R + 8
SEQ_P = SEQ // FFT_R * FFT_PITCH
CONV_ROWS = 64
CONV_PAD = 8
SUB_ROWS = 256
MIX_SUB_ROWS = 512
FFT_UNROLL = 64
RET_C = 256
CAST_CHUNKS = 32


VMEM_SMALL_MIB = 48
VMEM_DENSE_MIB = 56
VMEM_FFT_MIB = 60


def _params(sem, vmem_mib):
    return pltpu.CompilerParams(dimension_semantics=sem, vmem_limit_bytes=vmem_mib * MIB)


@functools.lru_cache(maxsize=None)
def _rope2d_tables():
    quarter = HEAD_DIM // 4
    inv = ROPE_BASE ** (-np.arange(quarter, dtype=np.float64) / quarter)
    t = np.arange(SEQ)
    pos = np.stack([t // GRID_W, t % GRID_W], axis=1).astype(np.float64)
    lane = np.arange(HEAD_DIM)
    half = lane // (HEAD_DIM // 2)
    e = lane % (HEAD_DIM // 2)
    ang = pos[:, half] * inv[e % quarter][None, :]
    sign = np.where(e < quarter, -1.0, 1.0)[None, :]
    cos = np.tile(np.cos(ang), (1, 2)).astype(np.float32)
    sin = np.tile(np.sin(ang) * sign, (1, 2)).astype(np.float32)
    return cos, sin


@functools.lru_cache(maxsize=None)
def _rope1d_tables():
    n = RET_DK // 2
    inv = ROPE_BASE ** (-np.linspace(0.0, 1.0, n))
    pos = np.arange(CTX_LEN + SEQ, dtype=np.float64)
    ang = pos[:, None] * inv[None, :]
    cos, sin = np.cos(ang), np.sin(ang)

    def rev(a):
        return a.reshape(-1, RET_C, n)[:, ::-1].reshape(-1, n)

    cos2 = np.stack([cos, rev(cos)]).astype(np.float32)
    sin2 = np.stack([sin, rev(sin)]).astype(np.float32)
    return cos2, sin2


@functools.lru_cache(maxsize=None)
def _fft_mats():
    N, R = FFT_N, FFT_R
    k1 = np.arange(R)
    n2 = np.arange(R)[:, None, None]
    n1 = np.arange(R // 2)[None, None, :]
    n = np.concatenate([n2 + R * n1, (N - ((R - n2) % R + R * n1)) % N], axis=2)
    idx = (k1[None, :, None] * n) % N
    ang = 2.0 * np.pi * idx / N
    c, s = np.cos(ang), np.sin(ang)
    m1 = np.concatenate([c, -s], axis=1).astype(np.float32)
    m1i = np.concatenate([c.transpose(0, 2, 1), -s.transpose(0, 2, 1)], axis=2)
    m1i = m1i[:, :R // 2].astype(np.float32)
    a2 = 2.0 * np.pi * ((np.arange(R)[:, None] * np.arange(R)[None, :]) % R) / R
    fr, fi = np.cos(a2), -np.sin(a2)
    f2 = np.block([[fr, -fi], [fi, fr]]).astype(np.float32)
    g2 = (np.block([[fr, fi], [-fi, fr]]) / N).astype(np.float32)
    m1c = np.concatenate([c[:, :, :R // 2], -s[:, :, :R // 2]], axis=2).astype(np.float32)
    return m1, m1i, f2, g2, m1c


@functools.lru_cache(maxsize=None)
def _ctx_fft_mats():
    N = 2 * CTX_LEN
    k = np.arange(N)[:, None]
    n = np.arange(N)[None, :]
    ang = 2.0 * np.pi * ((k * n) % N) / N
    c, s = np.cos(ang), np.sin(ang)
    fwd = np.concatenate([c, -s], axis=0).astype(np.float32)
    inv = (np.concatenate([c, -s], axis=1)[:CTX_LEN] / N).astype(np.float32)
    src = np.concatenate([np.arange(CTX_LEN), (N - np.arange(CTX_LEN)) % N])
    return fwd, inv, fwd[:, src]


def _ada_kernel(c_ref, w_ref, b_ref, o_ref):
    c = c_ref[...]
    a = c * (1.0 / (1.0 + jnp.exp(-c)))
    o_ref[...] = _dot3(a, w_ref[...]) + b_ref[...]


def _ada(cvec, ada_w, ada_b):
    tn = 1536
    return pl.pallas_call(
        _ada_kernel,
        out_shape=jax.ShapeDtypeStruct((DEPTH, 8, N_MOD * D_MODEL), F32),
        grid=(DEPTH, N_MOD * D_MODEL // tn),
        in_specs=[
            pl.BlockSpec((8, D_MODEL), lambda i, j: (0, 0)),
            pl.BlockSpec((None, D_MODEL, tn), lambda i, j: (i, 0, j)),
            pl.BlockSpec((None, 1, tn), lambda i, j: (i, 0, j)),
        ],
        out_specs=pl.BlockSpec((None, 8, tn), lambda i, j: (i, 0, j)),
        compiler_params=_params(("parallel", "parallel"), VMEM_SMALL_MIB),
        name="ada_mod",
    )(cvec, ada_w, ada_b.reshape(DEPTH, 1, N_MOD * D_MODEL))


def _with_casts(body, n_in, n_out, n_cast):
    def kern(*refs):
        srcs = refs[n_in:n_in + n_cast]
        outs_at = n_in + n_cast
        dsts = refs[outs_at + n_out:outs_at + n_out + n_cast]
        for src, dst in zip(srcs, dsts):
            dst[...] = src[...].astype(dst.dtype)
        body(*refs[:n_in], *refs[outs_at:outs_at + n_out], *refs[outs_at + n_out + n_cast:])
    return kern


def _cast_jobs(stacked_weights, steps_per_row):
    in_specs, out_specs, out_shapes, args = [], [], [], []
    chunk = lambda b, i: jnp.minimum(b * steps_per_row + i, CAST_CHUNKS - 1)
    for w, lead in stacked_weights:
        _, rows, cols = w.shape
        blk = rows // CAST_CHUNKS
        in_specs.append(pl.BlockSpec((None, blk, cols), lambda b, i, lead=lead: (lead, chunk(b, i), 0)))
        out_specs.append(pl.BlockSpec((blk, cols), lambda b, i: (chunk(b, i), 0)))
        out_shapes.append(jax.ShapeDtypeStruct((rows, cols), BF16))
        args.append(w)
    return in_specs, out_specs, out_shapes, args


def _mod_row(ref, row):
    if isinstance(row, int):
        return ref[row:row + 1, :]
    return ref[pl.ds(row, 1), :]


def _norm_mod(x, g, shift, scale):
    y = x * lax.rsqrt(jnp.mean(x * x, axis=-1, keepdims=True) + EPS)
    return (y * g) * (1.0 + scale) + shift


def _rope_tile(x, cos, sin_signed):
    lane = lax.broadcasted_iota(jnp.int32, x.shape, 1)
    first = (lane % 32) < 16
    partner = jnp.where(first, pltpu.roll(x, LANES - 16, axis=1), pltpu.roll(x, 16, axis=1))
    return x * cos + partner * sin_signed


def _project0(x_ref, g, sh, sc, w_ref, cos_ref, sin_ref, u_ref, qkv_ref, rope):
    tn = 512
    n_rot = (ATT_WIDTH + 2 * LANES) // LANES
    sub = min(x_ref.shape[0], SUB_ROWS)
    for s in range(x_ref.shape[0] // sub):
        rows = slice(s * sub, (s + 1) * sub)
        urows = slice(CONV_PAD + s * sub, CONV_PAD + (s + 1) * sub)
        h = _norm_mod(x_ref[rows, :], g, sh, sc).astype(BF16)
        for j in range(EV_U // tn):
            u_ref[urows, j * tn:(j + 1) * tn] = jnp.dot(h, w_ref[:, j * tn:(j + 1) * tn],
                                                        preferred_element_type=F32)
        for j in range(EV_QKV // tn):
            y = jnp.dot(h, w_ref[:, EV_U + j * tn:EV_U + (j + 1) * tn], preferred_element_type=F32)
            for t in range(tn // LANES):
                yt = y[:, t * LANES:(t + 1) * LANES]
                if rope and j * (tn // LANES) + t < n_rot:
                    yt = _rope_tile(yt, cos_ref[rows, :], sin_ref[rows, :])
                qkv_ref[rows, j * tn + t * LANES:j * tn + (t + 1) * LANES] = yt.astype(BF16)


def _short_conv_gate(u_ref, cw_ref, cb_ref, bias_ref, z_ref, x0_ref, zb_ref):
    tm = u_ref.shape[0] - 2 * CONV_PAD
    rc = CONV_ROWS
    z_pitched = z_ref.shape[0] != tm

    def conv(r0, c0):
        cols = slice(c0, c0 + LANES)
        taps = [u_ref[CONV_PAD - 1 + k + r0:CONV_PAD - 1 + k + r0 + rc, cols] for k in range(3)]
        return (taps[0] * cw_ref[0:1, cols] + taps[1] * cw_ref[1:2, cols] + taps[2] * cw_ref[2:3, cols]
                + cb_ref[:, cols])

    for r0 in range(0, tm, rc):
        rows = slice(r0, r0 + rc)
        z0 = r0 // FFT_R * FFT_PITCH + r0 % FFT_R if z_pitched else r0
        for c0 in range(0, HY_WIDTH, LANES):
            cols = slice(c0, c0 + LANES)
            x0 = conv(r0, c0)
            z = conv(r0, 2 * HY_WIDTH + c0) * conv(r0, HY_WIDTH + c0)
            z_ref[z0:z0 + rc, cols] = z
            x0_ref[rows, cols] = x0
            zb_ref[rows, cols] = x0 * (z * bias_ref[:, cols])
    if z_pitched:
        for k in range(tm // FFT_R):
            z_ref[k * FFT_PITCH + FFT_R:(k + 1) * FFT_PITCH, :] = jnp.zeros((FFT_PITCH - FFT_R, HY_WIDTH), F32)


def _inproj0_kernel(x_ref, g_ref, sh_ref, sc_ref, w_ref, cos_ref, sin_ref, cw_ref, cb_ref, bias_ref,
                    z_ref, x0_ref, zb_ref, qkv_ref, u_even, u_odd, *, row, rope, n_tiles):
    r = pl.program_id(0) if row is None else row
    i = pl.program_id(1)
    g, sh, sc = g_ref[...], _mod_row(sh_ref, r), _mod_row(sc_ref, r)
    conv_refs = (cw_ref, cb_ref, bias_ref, z_ref, x0_ref, zb_ref)
    zero_row = jnp.zeros((1, EV_U), F32)
    tm = x_ref.shape[0]
    above, first, last, below = CONV_PAD - 1, CONV_PAD, CONV_PAD + tm - 1, CONV_PAD + tm

    if n_tiles == 1:
        u_even[above:first, :] = zero_row
        u_even[below:below + 1, :] = zero_row
        _project0(x_ref, g, sh, sc, w_ref, cos_ref, sin_ref, u_even, qkv_ref, rope)
        _short_conv_gate(u_even, *conv_refs)
        return

    @pl.when(i == 0)
    def _():
        u_even[...] = jnp.zeros_like(u_even)
        u_odd[...] = jnp.zeros_like(u_odd)

    def step(u_new, u_old):
        j = i - 1
        u_old[above:first, :] = jnp.where(j > 0, u_new[last:last + 1, :], zero_row)
        _project0(x_ref, g, sh, sc, w_ref, cos_ref, sin_ref, u_new, qkv_ref, rope)
        u_old[below:below + 1, :] = jnp.where(j < n_tiles - 1, u_new[first:first + 1, :], zero_row)
        _short_conv_gate(u_old, *conv_refs)

    @pl.when((i % 2 == 0) & (i < n_tiles))
    def _():
        step(u_even, u_odd)

    @pl.when((i % 2 == 1) & (i < n_tiles))
    def _():
        step(u_odd, u_even)

    @pl.when(i == n_tiles)
    def _():
        bufs = (u_even, u_odd)
        u_last, u_before = bufs[(n_tiles - 1) % 2], bufs[n_tiles % 2]
        u_last[above:first, :] = u_before[last:last + 1, :]
        u_last[below:below + 1, :] = zero_row
        _short_conv_gate(u_last, *conv_refs)


def _inproj0(x, g, mod, w, conv_w, conv_b, bias, layer, *, is_ctx, tm, cast=()):
    G, R, _ = x.shape
    n_tiles = R // tm
    n_steps = n_tiles + (1 if n_tiles > 1 else 0)
    cos, sin = _rope2d_tables()
    cos, sin = jnp.asarray(cos), jnp.asarray(sin)
    proj_tile = lambda i: jnp.minimum(i, n_tiles - 1)
    conv_tile = (lambda i: jnp.maximum(i - 1, 0)) if n_tiles > 1 else (lambda i: i)
    if is_ctx:
        tab = pl.BlockSpec((tm, LANES), lambda b, i: (0, 0))
    else:
        tab = pl.BlockSpec((tm, LANES), lambda b, i: (proj_tile(i), 0))
    const = lambda a: pl.BlockSpec(a.shape, lambda b, i: (0, 0))
    hy_out = jax.ShapeDtypeStruct((G, R, HY_WIDTH), F32)
    hy_spec = pl.BlockSpec((None, tm, HY_WIDTH), lambda b, i: (b, conv_tile(i), 0))
    if R == SEQ:
        z_out = jax.ShapeDtypeStruct((G, SEQ_P, HY_WIDTH), F32)
        z_spec = pl.BlockSpec((None, tm // FFT_R * FFT_PITCH, HY_WIDTH), lambda b, i: (b, conv_tile(i), 0))
    else:
        z_out, z_spec = hy_out, hy_spec
    kern = functools.partial(_inproj0_kernel, row=2 if is_ctx else None, rope=not is_ctx, n_tiles=n_tiles)
    in_specs = [
        pl.BlockSpec((None, tm, D_MODEL), lambda b, i: (b, proj_tile(i), 0)),
        pl.BlockSpec((1, D_MODEL), lambda b, i: (0, 0)),
        pl.BlockSpec((None, 8, D_MODEL), lambda b, i: (layer, 0, 0)),
        pl.BlockSpec((None, 8, D_MODEL), lambda b, i: (layer, 0, 1)),
        pl.BlockSpec((D_MODEL, EV_COLS), lambda b, i: (0, 0), pipeline_mode=pl.Buffered(1)),
        tab, tab, const(conv_w), const(conv_b), const(bias),
    ]
    out_specs = [z_spec, hy_spec, hy_spec, pl.BlockSpec((None, tm, EV_QKV), lambda b, i: (b, proj_tile(i), 0))]
    out_shape = [z_out, hy_out, hy_out, jax.ShapeDtypeStruct((G, R, EV_QKV), BF16)]
    c_in, c_out, c_shapes, c_args = _cast_jobs(cast, n_steps)
    if cast:
        assert G * n_steps >= CAST_CHUNKS
        kern = _with_casts(kern, len(in_specs), len(out_specs), len(cast))
    res = pl.pallas_call(
        kern,
        out_shape=out_shape + c_shapes,
        grid=(G, n_steps),
        in_specs=in_specs + c_in,
        out_specs=out_specs + c_out,
        scratch_shapes=[pltpu.VMEM((tm + 2 * CONV_PAD, EV_U), F32)] * 2,
        compiler_params=_params(("arbitrary", "arbitrary") if cast else ("parallel", "arbitrary"), VMEM_SMALL_MIB),
        name="inproj0_ctx" if is_ctx else "inproj0",
    )(x, g, mod, mod, w, cos, sin, conv_w, conv_b, bias, *c_args)
    return res[:4], res[4:]


def _inproj1_kernel(x_ref, g_ref, sh_ref, sc_ref, w_ref, o_ref, *, row, tn):
    r = pl.program_id(0) if row is None else row
    g, sh, sc = g_ref[...], _mod_row(sh_ref, r), _mod_row(sc_ref, r)
    sub = min(x_ref.shape[0], SUB_ROWS)
    for s in range(x_ref.shape[0] // sub):
        rows = slice(s * sub, (s + 1) * sub)
        h = _norm_mod(x_ref[rows, :], g, sh, sc).astype(BF16)
        for j in range(w_ref.shape[1] // tn):
            cols = slice(j * tn, (j + 1) * tn)
            o_ref[rows, cols] = jnp.dot(h, w_ref[:, cols], preferred_element_type=F32).astype(o_ref.dtype)


def _inproj1(x, g, mod, w, layer, *, is_ctx, tm, tn, cast=()):
    G, R, _ = x.shape
    N = w.shape[1]
    kern = functools.partial(_inproj1_kernel, row=2 if is_ctx else None, tn=tn)
    in_specs = [
        pl.BlockSpec((None, tm, D_MODEL), lambda b, i: (b, i, 0)),
        pl.BlockSpec((1, D_MODEL), lambda b, i: (0, 0)),
        pl.BlockSpec((None, 8, D_MODEL), lambda b, i: (layer, 0, 0)),
        pl.BlockSpec((None, 8, D_MODEL), lambda b, i: (layer, 0, 1)),
        pl.BlockSpec((D_MODEL, N), lambda b, i: (0, 0), pipeline_mode=pl.Buffered(1)),
    ]
    c_in, c_out, c_shapes, c_args = _cast_jobs(cast, R // tm)
    if cast:
        assert G * (R // tm) >= CAST_CHUNKS
        kern = _with_casts(kern, len(in_specs), 1, len(cast))
    res = pl.pallas_call(
        kern,
        out_shape=[jax.ShapeDtypeStruct((G, R, N), BF16)] + c_shapes,
        grid=(G, R // tm),
        in_specs=in_specs + c_in,
        out_specs=[pl.BlockSpec((None, tm, N), lambda b, i: (b, i, 0))] + c_out,
        compiler_params=_params(("arbitrary", "arbitrary") if cast else ("parallel", "parallel"), VMEM_SMALL_MIB),
        name="inproj1_ctx" if is_ctx else "inproj1",
    )(x, g, mod, mod, w, *c_args)
    return res[0], res[1:]


def _split_bf16(a):
    hi = a.astype(BF16)
    return hi, (a - hi.astype(F32)).astype(BF16)


def _dot3_split(a_hi, a_lo, b):
    b_hi, b_lo = _split_bf16(b)
    dot = functools.partial(jnp.dot, preferred_element_type=F32)
    return dot(a_hi, b_hi) + (dot(a_lo, b_hi) + dot(a_hi, b_lo))


def _dot3(a, b):
    return _dot3_split(*_split_bf16(a), b)


def _store_pitched(ref, lead, value):
    for k in range(value.shape[0] // FFT_R):
        ref[lead, k * FFT_PITCH:k * FFT_PITCH + FFT_R, :] = value[k * FFT_R:(k + 1) * FFT_R]
        ref[lead, k * FFT_PITCH + FFT_R:(k + 1) * FFT_PITCH, :] = jnp.zeros(
            (FFT_PITCH - FFT_R, value.shape[1]), value.dtype)


def _filter_kernel(w1t_ref, b1_ref, w2t_ref, b2_ref, fr_ref, w3_ref, dec_ref, bands_ref, h_ref, ss_ref, *,
                   L, tm, pitched):
    i = pl.program_id(0)
    t_row = (lax.broadcasted_iota(jnp.int32, (1, tm), 1) + i * tm).astype(F32) / L
    ang = (2.0 * math.pi * t_row) * bands_ref[...]
    t8 = jnp.where(lax.broadcasted_iota(jnp.int32, (8, tm), 0) == 0, t_row, 0.0)
    feat = jnp.concatenate([t8, jnp.cos(ang), -jnp.sin(ang)], axis=0)
    fr = fr_ref[...]
    hid = jnp.sin(fr * (jnp.dot(w1t_ref[...], feat, precision=HIGHEST, preferred_element_type=F32) + b1_ref[...]))
    hid = jnp.sin(fr * (jnp.dot(w2t_ref[...], hid, precision=HIGHEST, preferred_element_type=F32) + b2_ref[...]))
    hid_hi, hid_lo = _split_bf16(hid.T)
    m_col = lax.broadcasted_iota(jnp.int32, (tm, 1), 0) + i * tm
    t_col = m_col.astype(F32) / L
    energy = jnp.zeros((1, HY_WIDTH), F32)
    for side in range(2):
        h = _dot3_split(hid_hi, hid_lo, w3_ref[side]) * jnp.exp(-t_col * jnp.abs(dec_ref[side]))
        if side == 1:
            h = jnp.where(m_col == 0, 0.0, h)
        if pitched:
            _store_pitched(h_ref, side, h)
        else:
            h_ref[side] = h
        energy = energy + jnp.sum(h * h, axis=0, keepdims=True)

    @pl.when(i == 0)
    def _():
        ss_ref[...] = jnp.zeros_like(ss_ref)

    ss_ref[...] += jnp.broadcast_to(energy, ss_ref.shape)


def _hyena_filter(w1, b1, w2, b2, w3, freq, decay, *, L):
    tm = min(L, 1024)
    pitched = L == SEQ
    tm_out, l_out = (tm // FFT_R * FFT_PITCH, SEQ_P) if pitched else (tm, L)
    kern = functools.partial(_filter_kernel, L=L, tm=tm, pitched=pitched)
    w1t = jnp.concatenate([w1[0:1], jnp.zeros((7, HY_HIDDEN), F32), w1[1:]], axis=0).T
    col = lambda a: a.reshape(HY_HIDDEN, 1)
    w3s = jnp.stack([w3[:, :HY_WIDTH], w3[:, HY_WIDTH:]])
    bands = jnp.asarray(np.linspace(1e-4, HY_BANDS - 1, HY_BANDS).astype(np.float32).reshape(HY_BANDS, 1))
    args = (w1t, col(b1), w2.T, col(b2), col(freq), w3s, decay.reshape(2, 1, HY_WIDTH), bands)
    full = lambda a: pl.BlockSpec(a.shape, lambda i: (0,) * a.ndim)
    h, ss = pl.pallas_call(
        kern,
        out_shape=(jax.ShapeDtypeStruct((2, l_out, HY_WIDTH), F32), jax.ShapeDtypeStruct((8, HY_WIDTH), F32)),
        grid=(L // tm,),
        in_specs=[full(a) for a in args],
        out_specs=(pl.BlockSpec((2, tm_out, HY_WIDTH), lambda i: (0, i, 0)),
                   pl.BlockSpec((8, HY_WIDTH), lambda i: (0, 0))),
        compiler_params=_params(("arbitrary",), VMEM_SMALL_MIB),
        name="hyena_filter_%d" % L,
    )(*args)
    return h.reshape(2 * l_out, HY_WIDTH), ss


def _stack_bf16(re, im):
    return jnp.concatenate([re, im], axis=0).astype(BF16)


def _filter_fft_kernel(h_ref, ss_ref, m1_ref, f2_ref, o_ref, a_scr):
    R, P = FFT_R, FFT_PITCH
    inv_norm = lax.rsqrt(ss_ref[0:1, :])

    def stage1(n2, carry):
        neg = jnp.where(n2 == 0, 0, R - n2)
        xs = jnp.concatenate([h_ref[pl.ds(n2, R // 2, stride=P), :],
                              h_ref[pl.ds(SEQ_P + neg, R // 2, stride=P), :]], axis=0)
        xs = (xs * inv_norm).astype(BF16)
        a = jnp.dot(m1_ref[n2], xs, preferred_element_type=F32)
        base = pl.multiple_of(n2 * P, 8)
        a_scr[0, pl.ds(base, R), :] = a[:R]
        a_scr[1, pl.ds(base, R), :] = a[R:]
        return carry

    lax.fori_loop(0, R, stage1, 0, unroll=FFT_UNROLL)

    ct = a_scr.shape[-1]

    def stage2(pair, carry):
        k1s = (2 * pair, 2 * pair + 1)
        rhs = jnp.concatenate([_stack_bf16(a_scr[0, pl.ds(k1, R, stride=P), :],
                                           a_scr[1, pl.ds(k1, R, stride=P), :]) for k1 in k1s], axis=1)
        x = jnp.dot(f2_ref[...], rhs, preferred_element_type=F32)
        for d, k1 in enumerate(k1s):
            base = pl.multiple_of(k1 * R, R)
            o_ref[0, pl.ds(base, R), :] = x[:R, d * ct:(d + 1) * ct].astype(BF16)
            o_ref[1, pl.ds(base, R), :] = x[R:, d * ct:(d + 1) * ct].astype(BF16)
        return carry

    lax.fori_loop(0, R // 2, stage2, 0, unroll=FFT_UNROLL // 2)


def _filter_fft(h_circ, ss, m1, f2):
    ct = LANES
    once = pl.Buffered(1)
    return pl.pallas_call(
        _filter_fft_kernel,
        out_shape=jax.ShapeDtypeStruct((2, FFT_N, HY_WIDTH), BF16),
        grid=(HY_WIDTH // ct,),
        in_specs=[
            pl.BlockSpec((2 * SEQ_P, ct), lambda c: (0, c)),
            pl.BlockSpec((8, ct), lambda c: (0, c)),
            pl.BlockSpec(m1.shape, lambda c: (0, 0, 0), pipeline_mode=once),
            pl.BlockSpec(f2.shape, lambda c: (0, 0), pipeline_mode=once),
        ],
        out_specs=pl.BlockSpec((2, FFT_N, ct), lambda c: (0, 0, c)),
        scratch_shapes=[pltpu.VMEM((2, FFT_R * FFT_PITCH, ct), F32)],
        compiler_params=_params(("parallel",), VMEM_FFT_MIB),
        name="hyena_filter_fft",
    )(h_circ, ss, m1, f2)


def _fft_conv_kernel(z_ref, hs_ref, m1_ref, f2_ref, g2_ref, m1i_ref, y_ref, a_scr):
    R, P = FFT_R, FFT_PITCH
    ct = a_scr.shape[-1]

    def stage1(n2, carry):
        rows = pl.ds(n2, R // 2, stride=P)
        xs = jnp.concatenate([z_ref[0, rows, :], z_ref[1, rows, :]], axis=1).astype(BF16)
        zero = jnp.zeros_like(xs)
        rhs = jnp.concatenate([jnp.concatenate([xs, zero], axis=1),
                               jnp.concatenate([zero, xs], axis=1)], axis=0)
        t = jnp.dot(m1_ref[n2], rhs, preferred_element_type=F32)
        base = pl.multiple_of(n2 * P, 8)
        a_scr[0, pl.ds(base, R), :] = t[:, 0:ct] - t[:, 3 * ct:4 * ct]
        a_scr[1, pl.ds(base, R), :] = t[:, 2 * ct:3 * ct] + t[:, ct:2 * ct]
        return carry

    lax.fori_loop(0, R, stage1, 0, unroll=FFT_UNROLL)

    def stage2(pair, carry):
        k1s = (2 * pair, 2 * pair + 1)
        rhs = jnp.concatenate([_stack_bf16(a_scr[0, pl.ds(k1, R, stride=P), :],
                                           a_scr[1, pl.ds(k1, R, stride=P), :]) for k1 in k1s], axis=1)
        x = jnp.dot(f2_ref[...], rhs, preferred_element_type=F32)
        prods = []
        for d, k1 in enumerate(k1s):
            base = pl.multiple_of(k1 * R, R)
            hr = hs_ref[0, pl.ds(base, R), :].astype(F32)
            hi = hs_ref[1, pl.ds(base, R), :].astype(F32)
            xr, xi = x[:R, d * ct:(d + 1) * ct], x[R:, d * ct:(d + 1) * ct]
            prods.append(_stack_bf16(xr * hr - xi * hi, xr * hi + xi * hr))
        bq = jnp.dot(g2_ref[...], jnp.concatenate(prods, axis=1), preferred_element_type=F32)
        for d, k1 in enumerate(k1s):
            a_scr[0, pl.ds(k1, R, stride=P), :] = bq[:R, d * ct:(d + 1) * ct]
            a_scr[1, pl.ds(k1, R, stride=P), :] = bq[R:, d * ct:(d + 1) * ct]
        return carry

    lax.fori_loop(0, R // 2, stage2, 0, unroll=FFT_UNROLL // 2)

    def stage3(n2, carry):
        base = pl.multiple_of(n2 * P, 8)
        br = a_scr[0, pl.ds(base, R), :]
        bi = a_scr[1, pl.ds(base, R), :]
        rhs = jnp.concatenate([jnp.concatenate([br, bi], axis=1),
                               jnp.concatenate([bi, -br], axis=1)], axis=0).astype(BF16)
        y = jnp.dot(m1i_ref[n2], rhs, preferred_element_type=F32)
        rows = pl.ds(n2, R // 2, stride=P)
        y_ref[0, rows, :] = y[:, :ct]
        y_ref[1, rows, :] = y[:, ct:]
        return carry

    lax.fori_loop(0, R, stage3, 0, unroll=FFT_UNROLL)
    for b in range(2):
        for n1 in range(R // 2):
            y_ref[b, n1 * P + R:(n1 + 1) * P, :] = jnp.zeros((P - R, ct), F32)


def _fft_conv(z, hspec, m1, f2, g2, m1i):
    assert z.shape[0] == 2, "the batch pair is packed into one complex signal"
    ct = LANES
    once = pl.Buffered(1)
    return pl.pallas_call(
        _fft_conv_kernel,
        out_shape=jax.ShapeDtypeStruct((2, SEQ_P, HY_WIDTH), F32),
        grid=(HY_WIDTH // ct,),
        in_specs=[
            pl.BlockSpec((2, SEQ_P, ct), lambda c: (0, 0, c), pipeline_mode=once),
            pl.BlockSpec((2, FFT_N, ct), lambda c: (0, 0, c)),
            pl.BlockSpec(m1.shape, lambda c: (0, 0, 0), pipeline_mode=once),
            pl.BlockSpec(f2.shape, lambda c: (0, 0), pipeline_mode=once),
            pl.BlockSpec(g2.shape, lambda c: (0, 0), pipeline_mode=once),
            pl.BlockSpec(m1i.shape, lambda c: (0, 0, 0), pipeline_mode=once),
        ],
        out_specs=pl.BlockSpec((2, SEQ_P, ct), lambda c: (0, 0, c), pipeline_mode=once),
        scratch_shapes=[pltpu.VMEM((2, FFT_R * FFT_PITCH, ct), F32)],
        compiler_params=_params(("arbitrary",), VMEM_FFT_MIB),
        name="hyena_fft_conv",
    )(z, hspec, m1, f2, g2, m1i)


def _ctx_conv_kernel(z_ref, h_ref, ss_ref, fwd_ref, inv_ref, fwd_h_ref, y_ref):
    N = 2 * CTX_LEN
    hn = (h_ref[...] * lax.rsqrt(ss_ref[0:1, :])).astype(BF16)
    hs = jnp.dot(fwd_h_ref[...], hn, preferred_element_type=F32)
    zs = jnp.dot(fwd_ref[:, :CTX_LEN], z_ref[...].astype(BF16), preferred_element_type=F32)
    hr, hi, zr, zi = hs[:N], hs[N:], zs[:N], zs[N:]
    ys = _stack_bf16(zr * hr - zi * hi, zr * hi + zi * hr)
    y_ref[...] = jnp.dot(inv_ref[...], ys, preferred_element_type=F32)


def _ctx_conv(z, h_circ, ss, fwd, inv, fwd_h):
    G = z.shape[0]
    full = lambda a: pl.BlockSpec(a.shape, lambda b: (0,) * a.ndim)
    return pl.pallas_call(
        _ctx_conv_kernel,
        out_shape=jax.ShapeDtypeStruct((G, CTX_LEN, HY_WIDTH), F32),
        grid=(G,),
        in_specs=[pl.BlockSpec((None, CTX_LEN, HY_WIDTH), lambda b: (b, 0, 0)),
                  full(h_circ), full(ss), full(fwd), full(inv), full(fwd_h)],
        out_specs=pl.BlockSpec((None, CTX_LEN, HY_WIDTH), lambda b: (b, 0, 0)),
        compiler_params=_params(("parallel",), VMEM_SMALL_MIB),
        name="hyena_ctx_conv",
    )(z, h_circ, ss, fwd, inv, fwd_h)


def _kv_group(refs, g, col0):
    cols = slice(col0 + g * LANES, col0 + (g + 1) * LANES)
    return refs[0][:, cols] if len(refs) == 1 else jnp.concatenate([r[:, cols] for r in refs], axis=0)


def _attn_logits(q_ref, k_refs, g, masks, col0=0):
    low = lax.broadcasted_iota(jnp.int32, (BLOCK, LANES), 1) < HEAD_DIM
    zero = jnp.zeros((), BF16)
    parts = []
    for hh in range(ATT_GROUP):
        h = g * ATT_GROUP + hh
        tile = q_ref[:, (h // 2) * LANES:(h // 2 + 1) * LANES]
        parts.append(jnp.where(low if h % 2 == 0 else ~low, tile, zero))
    s = lax.dot_general(jnp.concatenate(parts, axis=0), _kv_group(k_refs, g, col0), (((1,), (1,)), ((), ())),
                        preferred_element_type=F32)
    if masks is not None:
        ok_prev, ok_next = masks
        s = jnp.concatenate([jnp.where(ok_prev, s[:, :BLOCK], NEG_INF), s[:, BLOCK:2 * BLOCK],
                             jnp.where(ok_next, s[:, 2 * BLOCK:3 * BLOCK], NEG_INF), s[:, 3 * BLOCK:]], axis=1)
    return s


def _attn_output(s, sink_ref, v_refs, g, o_ref, col0=0):
    low = lax.broadcasted_iota(jnp.int32, (BLOCK, LANES), 1) < HEAD_DIM
    low4 = jnp.concatenate([low] * ATT_GROUP, axis=0)
    v = _kv_group(v_refs, g, col0)
    v_aug = jnp.where(lax.broadcasted_iota(jnp.int32, v.shape, 1) < HEAD_DIM, v, jnp.ones((), BF16))
    sink = jnp.concatenate(
        [jnp.full((BLOCK, 1), sink_ref[g * ATT_GROUP + hh] * LOG2E, F32) for hh in range(ATT_GROUP)], axis=0)
    m = jnp.maximum(jnp.max(s, axis=-1, keepdims=True), sink)
    e = jnp.exp2(s - m).astype(BF16)
    o = jnp.dot(e, v_aug, preferred_element_type=F32) + jnp.where(low4, 0.0, jnp.exp2(sink - m))
    swapped = pltpu.roll(o, HEAD_DIM, axis=1)
    for pair in range(ATT_GROUP // 2):
        ev = slice((2 * pair) * BLOCK, (2 * pair + 1) * BLOCK)
        od = slice((2 * pair + 1) * BLOCK, (2 * pair + 2) * BLOCK)
        even = o[ev] / swapped[ev]
        odd = swapped[od] / o[od]
        t = g * (ATT_GROUP // 2) + pair
        o_ref[:, t * LANES:(t + 1) * LANES] = jnp.where(low, even, odd).astype(o_ref.dtype)


def _attn_ctx_kernel(sink_ref, q_ref, kx_ref, vx_ref, o_ref):
    for g in range(ATT_KV_HEADS):
        _attn_output(_attn_logits(q_ref, (kx_ref,), g, None), sink_ref, (vx_ref,), g, o_ref)


def _attn_local_kernel(sink_ref, q_ref, kv_m2, kv_m1, kv_0, kv_p1, kv_ctx, o_ref, s_even, s_odd, *, n_blocks):
    n = pl.program_id(1)
    k_refs, v_refs, v_col = (kv_m1, kv_0, kv_p1, kv_ctx), (kv_m2, kv_m1, kv_0, kv_ctx), 2 * LANES

    @pl.when(n == 0)
    def _():
        s_odd[...] = jnp.zeros_like(s_odd)

    def step(s_new, s_prev):
        nq = jnp.minimum(n, n_blocks - 1)
        qi = lax.broadcasted_iota(jnp.int32, (ATT_GROUP * BLOCK, BLOCK), 0) % BLOCK
        ki = lax.broadcasted_iota(jnp.int32, (ATT_GROUP * BLOCK, BLOCK), 1)
        masks = ((ki >= qi) & (nq > 0), (ki <= qi) & (nq < n_blocks - 1))
        for g in range(ATT_KV_HEADS):
            s_new[g] = _attn_logits(q_ref, k_refs, g, masks)
        for g in range(ATT_KV_HEADS):
            _attn_output(s_prev[g], sink_ref, v_refs, g, o_ref, v_col)

    @pl.when(n % 2 == 0)
    def _():
        step(s_even, s_odd)

    @pl.when(n % 2 == 1)
    def _():
        step(s_odd, s_even)


def _attention(sink, qkv, qkv_ctx, *, local):
    src = qkv if local else qkv_ctx
    G, R, _ = src.shape
    nb = R // BLOCK
    kcol, vcol = ATT_WIDTH // 256, ATT_WIDTH // 256 + 1
    ctx_k = pl.BlockSpec((None, CTX_LEN, 256), lambda b, n: (b, 0, kcol))
    ctx_v = pl.BlockSpec((None, CTX_LEN, 256), lambda b, n: (b, 0, vcol))
    smem = pl.BlockSpec(memory_space=pltpu.SMEM)
    out_shape = jax.ShapeDtypeStruct((G, R, ATT_WIDTH), BF16)
    if not local:
        rows = pl.BlockSpec((None, BLOCK, ATT_WIDTH), lambda b, n: (b, n, 0))
        return pl.pallas_call(
            _attn_ctx_kernel, out_shape=out_shape, grid=(G, nb),
            in_specs=[smem, rows, ctx_k, ctx_v], out_specs=rows,
            compiler_params=_params(("parallel", "parallel"), VMEM_SMALL_MIB), name="attn_ctx",
        )(sink, qkv_ctx, qkv_ctx, qkv_ctx)

    q_blk = lambda n: jnp.minimum(n, nb - 1)
    o_blk = lambda n: jnp.maximum(n - 1, 0)

    kv_w = EV_QKV - ATT_WIDTH
    kv_col = ATT_WIDTH // kv_w

    def kv_rows(off):
        return pl.BlockSpec((None, BLOCK, kv_w), lambda b, n: (b, jnp.clip(n + off, 0, nb - 1), kv_col))

    in_specs = [smem, pl.BlockSpec((None, BLOCK, ATT_WIDTH), lambda b, n: (b, q_blk(n), 0)),
                kv_rows(-2), kv_rows(-1), kv_rows(0), kv_rows(1),
                pl.BlockSpec((None, CTX_LEN, kv_w), lambda b, n: (b, 0, kv_col))]
    logits_scratch = pltpu.VMEM((ATT_KV_HEADS, ATT_GROUP * BLOCK, 3 * BLOCK + CTX_LEN), F32)
    return pl.pallas_call(
        functools.partial(_attn_local_kernel, n_blocks=nb),
        out_shape=out_shape,
        grid=(G, nb + 1),
        in_specs=in_specs,
        out_specs=pl.BlockSpec((None, BLOCK, ATT_WIDTH), lambda b, n: (b, o_blk(n), 0)),
        scratch_shapes=[logits_scratch, logits_scratch],
        compiler_params=_params(("parallel", "arbitrary"), VMEM_SMALL_MIB),
        name="attn_local",
    )(sink, qkv, qkv, qkv, qkv, qkv, qkv_ctx)


def _mixer_out0(rows, x0_ref, y_ref, zb_ref, att_ref, wo_ref):
    if y_ref.shape[0] != x0_ref.shape[0]:
        y = jnp.concatenate([y_ref[r // FFT_R * FFT_PITCH:r // FFT_R * FFT_PITCH + FFT_R, :]
                             for r in range(rows.start, rows.stop, FFT_R)], axis=0)
    else:
        y = y_ref[rows, :]
    hy = (x0_ref[rows, :] * y + zb_ref[rows, :]).astype(BF16)
    return (jnp.dot(hy, wo_ref[:HY_WIDTH, :], preferred_element_type=F32)
            + jnp.dot(att_ref[rows, :], wo_ref[HY_WIDTH:, :], preferred_element_type=F32))


def _mixer_out1(rows, of_ref, ob_ref, wo_ref):
    a = (of_ref[rows, :].astype(F32) + ob_ref[rows, :].astype(F32)).astype(BF16)
    return jnp.dot(a, wo_ref[...], preferred_element_type=F32)


def _mix_kernel(*refs, mixer_out, n_mix, row, fc, final_norm):
    x_ref = refs[0]
    mix_refs = refs[1:2 + n_mix]
    gta_ref, gm_ref, shm_ref, scm_ref, gtm_ref, w1_ref, w2_ref, fg_ref, o_ref, a_scr = refs[2 + n_mix:]
    r = pl.program_id(0) if row is None else row
    gta, gtm = _mod_row(gta_ref, r), _mod_row(gtm_ref, r)
    gm, shm, scm = gm_ref[...], _mod_row(shm_ref, r), _mod_row(scm_ref, r)
    sub = min(x_ref.shape[0], MIX_SUB_ROWS)
    for s in range(x_ref.shape[0] // sub):
        rows = slice(s * sub, (s + 1) * sub)
        x1 = x_ref[rows, :] + gta * mixer_out(rows, *mix_refs)
        h = _norm_mod(x1, gm, shm, scm).astype(BF16)
        for c in range(D_FF // fc):
            a = jnp.maximum(jnp.dot(h, w1_ref[:, c * fc:(c + 1) * fc], preferred_element_type=F32), 0.0)
            a_scr[rows, c * fc:(c + 1) * fc] = (a * a).astype(BF16)
        out = x1 + gtm * jnp.dot(a_scr[rows, :], w2_ref[...], preferred_element_type=F32)
        if final_norm:
            out = (out * lax.rsqrt(jnp.mean(out * out, axis=-1, keepdims=True) + EPS)) * fg_ref[...]
        o_ref[rows, :] = out


def _mix_mlp(kind, x, mix_in, wo, mod, gm, w1, w2, fg, layer, *, is_ctx, tm, fc, final_norm, cast=()):
    G, R, _ = x.shape
    row_spec = lambda w: pl.BlockSpec((None, tm, w), lambda b, i: (b, i, 0))
    modk = lambda k: pl.BlockSpec((None, 8, D_MODEL), lambda b, i: (layer, 0, k))
    vec = pl.BlockSpec((1, D_MODEL), lambda b, i: (0, 0))
    resident = lambda a: pl.BlockSpec(a.shape, lambda b, i: (0, 0), pipeline_mode=pl.Buffered(1))
    if kind == 0:
        mixer_out = _mixer_out0
        y_rows = tm if mix_in[1].shape[1] == R else tm // FFT_R * FFT_PITCH
        y_spec = pl.BlockSpec((None, y_rows, HY_WIDTH), lambda b, i: (b, i, 0))
        mix_specs = [row_spec(HY_WIDTH), y_spec, row_spec(HY_WIDTH), row_spec(ATT_WIDTH)]
    else:
        mixer_out = _mixer_out1
        mix_specs = [row_spec(RET_V)] * 2
    kern = functools.partial(_mix_kernel, mixer_out=mixer_out, n_mix=len(mix_in), row=2 if is_ctx else None,
                             fc=fc, final_norm=final_norm)
    in_specs = [row_spec(D_MODEL)] + mix_specs + [
        resident(wo), modk(2), vec, modk(3), modk(4), modk(5), resident(w1), resident(w2), vec]
    c_in, c_out, c_shapes, c_args = _cast_jobs(cast, R // tm)
    if cast:
        assert G * (R // tm) >= CAST_CHUNKS
        kern = _with_casts(kern, len(in_specs), 1, len(cast))
    res = pl.pallas_call(
        kern,
        out_shape=[jax.ShapeDtypeStruct((G, R, D_MODEL), F32)] + c_shapes,
        grid=(G, R // tm),
        in_specs=in_specs + c_in,
        out_specs=[row_spec(D_MODEL)] + c_out,
        scratch_shapes=[pltpu.VMEM((tm, D_FF), BF16)],
        compiler_params=_params(("arbitrary", "arbitrary") if cast else ("parallel", "parallel"), VMEM_DENSE_MIB),
        name="mix_mlp%d%s" % (kind, "_ctx" if is_ctx else ""),
    )(x, *mix_in, wo, mod, gm, mod, mod, mod, w1, w2, fg, *c_args)
    return res[0], res[1:]


def _ret_kernel(lr_ref, qkv_c, qkv_f, g_f, qkv_b, g_b, cos_a, sin_a, cos_b, sin_b, of_ref, ob_ref,
                state, dmask, xi, zeta, gch):
    j = pl.program_id(1)
    C = RET_C
    kscale = RET_DK ** -0.5

    @pl.when(j == 0)
    def _():
        state[...] = jnp.zeros_like(state)
        row = lax.broadcasted_iota(jnp.int32, (C, C), 0).astype(F32)
        col = lax.broadcasted_iota(jnp.int32, (C, C), 1).astype(F32)
        for d in range(2):
            e = row - col if d == 0 else col - row
            p = row if d == 0 else (C - 1) - row
            for h in range(RET_HEADS):
                lg = -jnp.exp(jnp.full((C, C), lr_ref[d, h], F32))
                dmask[d, h] = jnp.where(e >= 0, jnp.exp(e * lg), 0.0) * kscale
                xi[d, h] = jnp.exp((p + 1.0) * lg)
                zeta[d, h] = jnp.exp(((C - 1) - p) * lg) * kscale
                gch[d, h] = jnp.exp(C * -jnp.exp(jnp.full((8, C), lr_ref[d, h], F32)))

    half = RET_DK // 2

    def chain(d, h, src_ref, g_ref, o_ref, tab_ref, r0):
        cos, sin = tab_ref[0][d], tab_ref[1][d]
        rows = slice(r0, r0 + C)

        def rot(c0):
            t1 = src_ref[rows, c0:c0 + half].astype(F32)
            t2 = src_ref[rows, c0 + half:c0 + RET_DK].astype(F32)
            return jnp.concatenate([t1 * cos - t2 * sin, t2 * cos + t1 * sin], axis=1)

        q = rot(h * RET_DK)
        k = rot(RET_QK + h * RET_DK)
        v = src_ref[rows, 2 * RET_QK + h * RET_DV:2 * RET_QK + (h + 1) * RET_DV]
        inner = lax.dot_general(q.astype(BF16), k.astype(BF16), (((1,), (1,)), ((), ())),
                                preferred_element_type=F32) * dmask[d, h]
        s_old = state[d, h]
        kv = lax.dot_general((k * zeta[d, h]).astype(BF16), v, (((0,), (0,)), ((), ())),
                             preferred_element_type=F32)
        state[d, h] = gch[d, h, 0:1, 0:1] * s_old + kv
        if o_ref is not None:
            o = (jnp.dot(inner.astype(BF16), v, preferred_element_type=F32)
                 + jnp.dot((q * xi[d, h]).astype(BF16), s_old.astype(BF16), preferred_element_type=F32))
            on = o * lax.rsqrt(jnp.mean(o * o, axis=-1, keepdims=True) + EPS)
            gate = g_ref[rows, h * RET_DV:(h + 1) * RET_DV]
            half_gate = gate * 0.5
            silu = half_gate + half_gate * jnp.tanh(half_gate)
            o_ref[rows, h * RET_DV:(h + 1) * RET_DV] = (silu.astype(F32) * on).astype(o_ref.dtype)

    tab_a, tab_b = (cos_a, sin_a), (cos_b, sin_b)

    @pl.when(j == 0)
    def _():
        for d in range(2):
            for h in range(RET_HEADS):
                chain(d, h, qkv_c, None, None, tab_a, 0)

    @pl.when(j > 0)
    def _():
        for tab, r_fwd, r_bwd in ((tab_a, 0, C), (tab_b, C, 0)):
            for h in range(RET_HEADS):
                chain(0, h, qkv_f, g_f, of_ref, tab, r_fwd)
                chain(1, h, qkv_b, g_b, ob_ref, tab, r_bwd)


def _retention(log_rate, p_ctx, p_lat):
    G = p_lat.shape[0]
    C = RET_C
    n_pairs = SEQ // (2 * C)
    cos, sin = _rope1d_tables()
    cos, sin = jnp.asarray(cos), jnp.asarray(sin)

    def fwd_pair(j):
        return jnp.maximum(j - 1, 0)

    def bwd_pair(j):
        return n_pairs - jnp.maximum(j, 1)

    qkv_w = 2 * RET_QK + RET_V
    gcol = qkv_w // RET_V
    tab_a = pl.BlockSpec((2, C, RET_DK // 2), lambda b, j: (0, jnp.maximum(2 * j - 1, 0), 0))
    tab_b = pl.BlockSpec((2, C, RET_DK // 2), lambda b, j: (0, 2 * j, 0))
    in_specs = [
        pl.BlockSpec(memory_space=pltpu.SMEM),
        pl.BlockSpec((None, C, qkv_w), lambda b, j: (b, 0, 0)),
        pl.BlockSpec((None, 2 * C, qkv_w), lambda b, j: (b, fwd_pair(j), 0)),
        pl.BlockSpec((None, 2 * C, RET_V), lambda b, j: (b, fwd_pair(j), gcol)),
        pl.BlockSpec((None, 2 * C, qkv_w), lambda b, j: (b, bwd_pair(j), 0)),
        pl.BlockSpec((None, 2 * C, RET_V), lambda b, j: (b, bwd_pair(j), gcol + 1)),
        tab_a, tab_a, tab_b, tab_b,
    ]
    out = jax.ShapeDtypeStruct((G, SEQ, RET_V), BF16)
    per_chain = lambda *tail: pltpu.VMEM((2, RET_HEADS) + tail, F32)
    return pl.pallas_call(
        _ret_kernel,
        out_shape=(out, out),
        grid=(G, 1 + n_pairs),
        in_specs=in_specs,
        out_specs=(pl.BlockSpec((None, 2 * C, RET_V), lambda b, j: (b, fwd_pair(j), 0)),
                   pl.BlockSpec((None, 2 * C, RET_V), lambda b, j: (b, bwd_pair(j), 0))),
        scratch_shapes=[per_chain(RET_DK, RET_DV), per_chain(C, C), per_chain(C, C), per_chain(C, C),
                        per_chain(8, C)],
        compiler_params=_params(("parallel", "arbitrary"), VMEM_DENSE_MIB),
        name="retention",
    )(log_rate, p_ctx, p_lat, p_lat, p_lat, p_lat, cos, sin, cos, sin)


def _ev_weight(w_in):
    i_q = EV_U
    i_k = i_q + ATT_WIDTH
    i_v = i_k + KV_WIDTH
    cols = [w_in[:, :i_q], w_in[:, i_q:i_k] * (HEAD_DIM ** -0.5 * LOG2E)]
    for base in (i_k, i_v):
        for g in range(ATT_KV_HEADS):
            part = w_in[:, base + g * HEAD_DIM: base + (g + 1) * HEAD_DIM]
            cols += [part, part]
    return jnp.concatenate([c.astype(BF16) for c in cols], axis=1)


def kernel(x, c, ctx, c_ctx, ada_w, ada_b, norm_mix_g, norm_mlp_g, mlp_w1, mlp_w2, ev_w_in, ev_w_out, hy_conv_w, hy_conv_b, hy_w1, hy_b1, hy_w2, hy_b2, hy_w3, hy_freq, hy_decay, hy_bias, attn_sink, od_w_in, od_w_out, ret_log_rate, final_g):
    D = D_MODEL
    cvec = jnp.concatenate([c, c_ctx[None, :], jnp.zeros((8 - BATCH - 1, D), F32)], axis=0)
    mod = _ada(cvec, ada_w, ada_b)

    m1, m1i, f2, g2, m1c = (jnp.asarray(a).astype(BF16) for a in _fft_mats())
    cfwd, cinv, cfwd_h = (jnp.asarray(a).astype(BF16) for a in _ctx_fft_mats())
    fg = final_g.reshape(1, D)

    gmix = norm_mix_g[0].reshape(1, D)
    gmlp = norm_mlp_g[0].reshape(1, D)
    w_in = _ev_weight(ev_w_in[0])
    conv = (hy_conv_w[0], hy_conv_b[0].reshape(1, EV_U), hy_bias[0].reshape(1, HY_WIDTH))
    (z_l, x0_l, zb_l, qkv_l), (wo, w1, w2) = _inproj0(
        x, gmix, mod, w_in, *conv, 0, is_ctx=False, tm=512, cast=((ev_w_out, 0), (mlp_w1, 0), (mlp_w2, 0)))
    (z_c, x0_c, zb_c, qkv_c), _ = _inproj0(ctx, gmix, mod, w_in, *conv, 0, is_ctx=True, tm=CTX_LEN)

    filt = (hy_w1[0], hy_b1[0], hy_w2[0], hy_b2[0], hy_w3[0], hy_freq[0], hy_decay[0])
    h_l, ss_l = _hyena_filter(*filt, L=SEQ)
    h_c, ss_c = _hyena_filter(*filt, L=CTX_LEN)
    hspec = _filter_fft(h_l, ss_l, m1, f2)
    y_l = _fft_conv(z_l, hspec, m1c, f2, g2, m1i)
    y_c = _ctx_conv(z_c, h_c, ss_c, cfwd, cinv, cfwd_h)

    sink = attn_sink[0]
    att_l = _attention(sink, qkv_l, qkv_c, local=True)
    att_c = _attention(sink, qkv_l, qkv_c, local=False)

    x1, (w_in,) = _mix_mlp(0, x, (x0_l, y_l, zb_l, att_l), wo, mod, gmlp, w1, w2, fg, 0,
                           is_ctx=False, tm=512, fc=512, final_norm=False, cast=((od_w_in, 0),))
    ctx1, _ = _mix_mlp(0, ctx, (x0_c, y_c, zb_c, att_c), wo, mod, gmlp, w1, w2, fg, 0,
                       is_ctx=True, tm=CTX_LEN, fc=512, final_norm=False)

    gmix = norm_mix_g[1].reshape(1, D)
    gmlp = norm_mlp_g[1].reshape(1, D)
    p_l, (wo, w1, w2) = _inproj1(x1, gmix, mod, w_in, 1, is_ctx=False, tm=512, tn=512,
                                 cast=((od_w_out, 0), (mlp_w1, 1), (mlp_w2, 1)))
    p_c, _ = _inproj1(ctx1, gmix, mod, w_in, 1, is_ctx=True, tm=CTX_LEN, tn=512)
    o_fwd, o_bwd = _retention(ret_log_rate[0], p_c, p_l)

    out, _ = _mix_mlp(1, x1, (o_fwd, o_bwd), wo, mod, gmlp, w1, w2, fg, 1,
                      is_ctx=False, tm=512, fc=512, final_norm=True)
    return out
```

```python
import functools
import math

import numpy as np
import jax
import jax.numpy as jnp
from jax import lax
from jax.experimental import pallas as pl
from jax.experimental.pallas import tpu as pltpu

F32 = jnp.float32
BF16 = jnp.bfloat16
HIGHEST = lax.Precision.HIGHEST

D_MODEL = 1024
BATCH = 2
SEQ = 8192
DEPTH = 2
GRID_W = 64
CTX_LEN = 256
EPS = 1e-6
NEG_INF = -1e30
N_MOD = 6
D_FF = 4 * D_MODEL
ROPE_BASE = 10000.0

HY_WIDTH = D_MODEL // 2
HY_EMB = 33
HY_BANDS = (HY_EMB - 1) // 2
HY_HIDDEN = 64

ATT_HEADS = 8
ATT_KV_HEADS = 2
ATT_GROUP = ATT_HEADS // ATT_KV_HEADS
HEAD_DIM = 64
ATT_WIDTH = ATT_HEADS * HEAD_DIM
KV_WIDTH = ATT_KV_HEADS * HEAD_DIM
BLOCK = 128

RET_HEADS = 4
RET_DK = D_MODEL // RET_HEADS
RET_DV = 2 * RET_DK
RET_QK = RET_HEADS * RET_DK
RET_V = RET_HEADS * RET_DV
OD_IN = 2 * RET_QK + 3 * RET_V

LOG2E = 1.4426950408889634
LANES = 128
MIB = 1024 * 1024

EV_U = 3 * HY_WIDTH
EV_QKV = ATT_WIDTH + 4 * LANES
EV_COLS = EV_U + EV_QKV

FFT_N = 2 * SEQ
FFT_R = 128
FFT_PITCH = FFT_R + 8
SEQ_P = SEQ // FFT_R * FFT_PITCH
CONV_ROWS = 64
CONV_PAD = 8
SUB_ROWS = 256
MIX_SUB_ROWS = 512
FFT_UNROLL = 64
RET_C = 256
CAST_CHUNKS = 32


VMEM_SMALL_MIB = 48
VMEM_DENSE_MIB = 56
VMEM_FFT_MIB = 60


def _params(sem, vmem_mib):
    return pltpu.CompilerParams(dimension_semantics=sem, vmem_limit_bytes=vmem_mib * MIB)


@functools.lru_cache(maxsize=None)
def _rope2d_tables():
    quarter = HEAD_DIM // 4
    inv = ROPE_BASE ** (-np.arange(quarter, dtype=np.float64) / quarter)
    t = np.arange(SEQ)
    pos = np.stack([t // GRID_W, t % GRID_W], axis=1).astype(np.float64)
    lane = np.arange(HEAD_DIM)
    half = lane // (HEAD_DIM // 2)
    e = lane % (HEAD_DIM // 2)
    ang = pos[:, half] * inv[e % quarter][None, :]
    sign = np.where(e < quarter, -1.0, 1.0)[None, :]
    cos = np.tile(np.cos(ang), (1, 2)).astype(np.float32)
    sin = np.tile(np.sin(ang) * sign, (1, 2)).astype(np.float32)
    return cos, sin


@functools.lru_cache(maxsize=None)
def _rope1d_tables():
    n = RET_DK // 2
    inv = ROPE_BASE ** (-np.linspace(0.0, 1.0, n))
    pos = np.arange(CTX_LEN + SEQ, dtype=np.float64)
    ang = pos[:, None] * inv[None, :]
    cos, sin = np.cos(ang), np.sin(ang)

    def rev(a):
        return a.reshape(-1, RET_C, n)[:, ::-1].reshape(-1, n)

    cos2 = np.stack([cos, rev(cos)]).astype(np.float32)
    sin2 = np.stack([sin, rev(sin)]).astype(np.float32)
    return cos2, sin2


@functools.lru_cache(maxsize=None)
def _fft_mats():
    N, R = FFT_N, FFT_R
    k1 = np.arange(R)
    n2 = np.arange(R)[:, None, None]
    n1 = np.arange(R // 2)[None, None, :]
    n = np.concatenate([n2 + R * n1, (N - ((R - n2) % R + R * n1)) % N], axis=2)
    idx = (k1[None, :, None] * n) % N
    ang = 2.0 * np.pi * idx / N
    c, s = np.cos(ang), np.sin(ang)
    m1 = np.concatenate([c, -s], axis=1).astype(np.float32)
    m1i = np.concatenate([c.transpose(0, 2, 1), -s.transpose(0, 2, 1)], axis=2)
    m1i = m1i[:, :R // 2].astype(np.float32)
    a2 = 2.0 * np.pi * ((np.arange(R)[:, None] * np.arange(R)[None, :]) % R) / R
    fr, fi = np.cos(a2), -np.sin(a2)
    f2 = np.block([[fr, -fi], [fi, fr]]).astype(np.float32)
    g2 = (np.block([[fr, fi], [-fi, fr]]) / N).astype(np.float32)
    m1c = np.concatenate([c[:, :, :R // 2], -s[:, :, :R // 2]], axis=2).astype(np.float32)
    return m1, m1i, f2, g2, m1c


@functools.lru_cache(maxsize=None)
def _ctx_fft_mats():
    N = 2 * CTX_LEN
    k = np.arange(N)[:, None]
    n = np.arange(N)[None, :]
    ang = 2.0 * np.pi * ((k * n) % N) / N
    c, s = np.cos(ang), np.sin(ang)
    fwd = np.concatenate([c, -s], axis=0).astype(np.float32)
    inv = (np.concatenate([c, -s], axis=1)[:CTX_LEN] / N).astype(np.float32)
    src = np.concatenate([np.arange(CTX_LEN), (N - np.arange(CTX_LEN)) % N])
    return fwd, inv, fwd[:, src]


def _ada_kernel(c_ref, w_ref, b_ref, o_ref):
    c = c_ref[...]
    a = c * (1.0 / (1.0 + jnp.exp(-c)))
    o_ref[...] = _dot3(a, w_ref[...]) + b_ref[...]


def _ada(cvec, ada_w, ada_b):
    tn = 1536
    return pl.pallas_call(
        _ada_kernel,
        out_shape=jax.ShapeDtypeStruct((DEPTH, 8, N_MOD * D_MODEL), F32),
        grid=(DEPTH, N_MOD * D_MODEL // tn),
        in_specs=[
            pl.BlockSpec((8, D_MODEL), lambda i, j: (0, 0)),
            pl.BlockSpec((None, D_MODEL, tn), lambda i, j: (i, 0, j)),
            pl.BlockSpec((None, 1, tn), lambda i, j: (i, 0, j)),
        ],
        out_specs=pl.BlockSpec((None, 8, tn), lambda i, j: (i, 0, j)),
        compiler_params=_params(("parallel", "parallel"), VMEM_SMALL_MIB),
        name="ada_mod",
    )(cvec, ada_w, ada_b.reshape(DEPTH, 1, N_MOD * D_MODEL))


def _with_casts(body, n_in, n_out, n_cast):
    def kern(*refs):
        srcs = refs[n_in:n_in + n_cast]
        outs_at = n_in + n_cast
        dsts = refs[outs_at + n_out:outs_at + n_out + n_cast]
        for src, dst in zip(srcs, dsts):
            dst[...] = src[...].astype(dst.dtype)
        body(*refs[:n_in], *refs[outs_at:outs_at + n_out], *refs[outs_at + n_out + n_cast:])
    return kern


def _cast_jobs(stacked_weights, steps_per_row):
    in_specs, out_specs, out_shapes, args = [], [], [], []
    chunk = lambda b, i: jnp.minimum(b * steps_per_row + i, CAST_CHUNKS - 1)
    for w, lead in stacked_weights:
        _, rows, cols = w.shape
        blk = rows // CAST_CHUNKS
        in_specs.append(pl.BlockSpec((None, blk, cols), lambda b, i, lead=lead: (lead, chunk(b, i), 0)))
        out_specs.append(pl.BlockSpec((blk, cols), lambda b, i: (chunk(b, i), 0)))
        out_shapes.append(jax.ShapeDtypeStruct((rows, cols), BF16))
        args.append(w)
    return in_specs, out_specs, out_shapes, args


def _mod_row(ref, row):
    if isinstance(row, int):
        return ref[row:row + 1, :]
    return ref[pl.ds(row, 1), :]


def _norm_mod(x, g, shift, scale):
    y = x * lax.rsqrt(jnp.mean(x * x, axis=-1, keepdims=True) + EPS)
    return (y * g) * (1.0 + scale) + shift


def _rope_tile(x, cos, sin_signed):
    lane = lax.broadcasted_iota(jnp.int32, x.shape, 1)
    first = (lane % 32) < 16
    partner = jnp.where(first, pltpu.roll(x, LANES - 16, axis=1), pltpu.roll(x, 16, axis=1))
    return x * cos + partner * sin_signed


def _project0(x_ref, g, sh, sc, w_ref, cos_ref, sin_ref, u_ref, qkv_ref, rope):
    tn = 512
    n_rot = (ATT_WIDTH + 2 * LANES) // LANES
    sub = min(x_ref.shape[0], SUB_ROWS)
    for s in range(x_ref.shape[0] // sub):
        rows = slice(s * sub, (s + 1) * sub)
        urows = slice(CONV_PAD + s * sub, CONV_PAD + (s + 1) * sub)
        h = _norm_mod(x_ref[rows, :], g, sh, sc).astype(BF16)
        for j in range(EV_U // tn):
            u_ref[urows, j * tn:(j + 1) * tn] = jnp.dot(h, w_ref[:, j * tn:(j + 1) * tn],
                                                        preferred_element_type=F32)
        for j in range(EV_QKV // tn):
            y = jnp.dot(h, w_ref[:, EV_U + j * tn:EV_U + (j + 1) * tn], preferred_element_type=F32)
            for t in range(tn // LANES):
                yt = y[:, t * LANES:(t + 1) * LANES]
                if rope and j * (tn // LANES) + t < n_rot:
                    yt = _rope_tile(yt, cos_ref[rows, :], sin_ref[rows, :])
                qkv_ref[rows, j * tn + t * LANES:j * tn + (t + 1) * LANES] = yt.astype(BF16)


def _short_conv_gate(u_ref, cw_ref, cb_ref, bias_ref, z_ref, x0_ref, zb_ref):
    tm = u_ref.shape[0] - 2 * CONV_PAD
    rc = CONV_ROWS
    z_pitched = z_ref.shape[0] != tm

    def conv(r0, c0):
        cols = slice(c0, c0 + LANES)
        taps = [u_ref[CONV_PAD - 1 + k + r0:CONV_PAD - 1 + k + r0 + rc, cols] for k in range(3)]
        return (taps[0] * cw_ref[0:1, cols] + taps[1] * cw_ref[1:2, cols] + taps[2] * cw_ref[2:3, cols]
                + cb_ref[:, cols])

    for r0 in range(0, tm, rc):
        rows = slice(r0, r0 + rc)
        z0 = r0 // FFT_R * FFT_PITCH + r0 % FFT_R if z_pitched else r0
        for c0 in range(0, HY_WIDTH, LANES):
            cols = slice(c0, c0 + LANES)
            x0 = conv(r0, c0)
            z = conv(r0, 2 * HY_WIDTH + c0) * conv(r0, HY_WIDTH + c0)
            z_ref[z0:z0 + rc, cols] = z
            x0_ref[rows, cols] = x0
            zb_ref[rows, cols] = x0 * (z * bias_ref[:, cols])
    if z_pitched:
        for k in range(tm // FFT_R):
            z_ref[k * FFT_PITCH + FFT_R:(k + 1) * FFT_PITCH, :] = jnp.zeros((FFT_PITCH - FFT_R, HY_WIDTH), F32)


def _inproj0_kernel(x_ref, g_ref, sh_ref, sc_ref, w_ref, cos_ref, sin_ref, cw_ref, cb_ref, bias_ref,
                    z_ref, x0_ref, zb_ref, qkv_ref, u_even, u_odd, *, row, rope, n_tiles):
    r = pl.program_id(0) if row is None else row
    i = pl.program_id(1)
    g, sh, sc = g_ref[...], _mod_row(sh_ref, r), _mod_row(sc_ref, r)
    conv_refs = (cw_ref, cb_ref, bias_ref, z_ref, x0_ref, zb_ref)
    zero_row = jnp.zeros((1, EV_U), F32)
    tm = x_ref.shape[0]
    above, first, last, below = CONV_PAD - 1, CONV_PAD, CONV_PAD + tm - 1, CONV_PAD + tm

    if n_tiles == 1:
        u_even[above:first, :] = zero_row
        u_even[below:below + 1, :] = zero_row
        _project0(x_ref, g, sh, sc, w_ref, cos_ref, sin_ref, u_even, qkv_ref, rope)
        _short_conv_gate(u_even, *conv_refs)
        return

    @pl.when(i == 0)
    def _():
        u_even[...] = jnp.zeros_like(u_even)
        u_odd[...] = jnp.zeros_like(u_odd)

    def step(u_new, u_old):
        j = i - 1
        u_old[above:first, :] = jnp.where(j > 0, u_new[last:last + 1, :], zero_row)
        _project0(x_ref, g, sh, sc, w_ref, cos_ref, sin_ref, u_new, qkv_ref, rope)
        u_old[below:below + 1, :] = jnp.where(j < n_tiles - 1, u_new[first:first + 1, :], zero_row)
        _short_conv_gate(u_old, *conv_refs)

    @pl.when((i % 2 == 0) & (i < n_tiles))
    def _():
        step(u_even, u_odd)

    @pl.when((i % 2 == 1) & (i < n_tiles))
    def _():
        step(u_odd, u_even)

    @pl.when(i == n_tiles)
    def _():
        bufs = (u_even, u_odd)
        u_last, u_before = bufs[(n_tiles - 1) % 2], bufs[n_tiles % 2]
        u_last[above:first, :] = u_before[last:last + 1, :]
        u_last[below:below + 1, :] = zero_row
        _short_conv_gate(u_last, *conv_refs)


def _inproj0(x, g, mod, w, conv_w, conv_b, bias, layer, *, is_ctx, tm, cast=()):
    G, R, _ = x.shape
    n_tiles = R // tm
    n_steps = n_tiles + (1 if n_tiles > 1 else 0)
    cos, sin = _rope2d_tables()
    cos, sin = jnp.asarray(cos), jnp.asarray(sin)
    proj_tile = lambda i: jnp.minimum(i, n_tiles - 1)
    conv_tile = (lambda i: jnp.maximum(i - 1, 0)) if n_tiles > 1 else (lambda i: i)
    if is_ctx:
        tab = pl.BlockSpec((tm, LANES), lambda b, i: (0, 0))
    else:
        tab = pl.BlockSpec((tm, LANES), lambda b, i: (proj_tile(i), 0))
    const = lambda a: pl.BlockSpec(a.shape, lambda b, i: (0, 0))
    hy_out = jax.ShapeDtypeStruct((G, R, HY_WIDTH), F32)
    hy_spec = pl.BlockSpec((None, tm, HY_WIDTH), lambda b, i: (b, conv_tile(i), 0))
    if R == SEQ:
        z_out = jax.ShapeDtypeStruct((G, SEQ_P, HY_WIDTH), F32)
        z_spec = pl.BlockSpec((None, tm // FFT_R * FFT_PITCH, HY_WIDTH), lambda b, i: (b, conv_tile(i), 0))
    else:
        z_out, z_spec = hy_out, hy_spec
    kern = functools.partial(_inproj0_kernel, row=2 if is_ctx else None, rope=not is_ctx, n_tiles=n_tiles)
    in_specs = [
        pl.BlockSpec((None, tm, D_MODEL), lambda b, i: (b, proj_tile(i), 0)),
        pl.BlockSpec((1, D_MODEL), lambda b, i: (0, 0)),
        pl.BlockSpec((None, 8, D_MODEL), lambda b, i: (layer, 0, 0)),
        pl.BlockSpec((None, 8, D_MODEL), lambda b, i: (layer, 0, 1)),
        pl.BlockSpec((D_MODEL, EV_COLS), lambda b, i: (0, 0), pipeline_mode=pl.Buffered(1)),
        tab, tab, const(conv_w), const(conv_b), const(bias),
    ]
    out_specs = [z_spec, hy_spec, hy_spec, pl.BlockSpec((None, tm, EV_QKV), lambda b, i: (b, proj_tile(i), 0))]
    out_shape = [z_out, hy_out, hy_out, jax.ShapeDtypeStruct((G, R, EV_QKV), BF16)]
    c_in, c_out, c_shapes, c_args = _cast_jobs(cast, n_steps)
    if cast:
        assert G * n_steps >= CAST_CHUNKS
        kern = _with_casts(kern, len(in_specs), len(out_specs), len(cast))
    res = pl.pallas_call(
        kern,
        out_shape=out_shape + c_shapes,
        grid=(G, n_steps),
        in_specs=in_specs + c_in,
        out_specs=out_specs + c_out,
        scratch_shapes=[pltpu.VMEM((tm + 2 * CONV_PAD, EV_U), F32)] * 2,
        compiler_params=_params(("arbitrary", "arbitrary") if cast else ("parallel", "arbitrary"), VMEM_SMALL_MIB),
        name="inproj0_ctx" if is_ctx else "inproj0",
    )(x, g, mod, mod, w, cos, sin, conv_w, conv_b, bias, *c_args)
    return res[:4], res[4:]


def _inproj1_kernel(x_ref, g_ref, sh_ref, sc_ref, w_ref, o_ref, *, row, tn):
    r = pl.program_id(0) if row is None else row
    g, sh, sc = g_ref[...], _mod_row(sh_ref, r), _mod_row(sc_ref, r)
    sub = min(x_ref.shape[0], SUB_ROWS)
    for s in range(x_ref.shape[0] // sub):
        rows = slice(s * sub, (s + 1) * sub)
        h = _norm_mod(x_ref[rows, :], g, sh, sc).astype(BF16)
        for j in range(w_ref.shape[1] // tn):
            cols = slice(j * tn, (j + 1) * tn)
            o_ref[rows, cols] = jnp.dot(h, w_ref[:, cols], preferred_element_type=F32).astype(o_ref.dtype)


def _inproj1(x, g, mod, w, layer, *, is_ctx, tm, tn, cast=()):
    G, R, _ = x.shape
    N = w.shape[1]
    kern = functools.partial(_inproj1_kernel, row=2 if is_ctx else None, tn=tn)
    in_specs = [
        pl.BlockSpec((None, tm, D_MODEL), lambda b, i: (b, i, 0)),
        pl.BlockSpec((1, D_MODEL), lambda b, i: (0, 0)),
        pl.BlockSpec((None, 8, D_MODEL), lambda b, i: (layer, 0, 0)),
        pl.BlockSpec((None, 8, D_MODEL), lambda b, i: (layer, 0, 1)),
        pl.BlockSpec((D_MODEL, N), lambda b, i: (0, 0), pipeline_mode=pl.Buffered(1)),
    ]
    c_in, c_out, c_shapes, c_args = _cast_jobs(cast, R // tm)
    if cast:
        assert G * (R // tm) >= CAST_CHUNKS
        kern = _with_casts(kern, len(in_specs), 1, len(cast))
    res = pl.pallas_call(
        kern,
        out_shape=[jax.ShapeDtypeStruct((G, R, N), BF16)] + c_shapes,
        grid=(G, R // tm),
        in_specs=in_specs + c_in,
        out_specs=[pl.BlockSpec((None, tm, N), lambda b, i: (b, i, 0))] + c_out,
        compiler_params=_params(("arbitrary", "arbitrary") if cast else ("parallel", "parallel"), VMEM_SMALL_MIB),
        name="inproj1_ctx" if is_ctx else "inproj1",
    )(x, g, mod, mod, w, *c_args)
    return res[0], res[1:]


def _split_bf16(a):
    hi = a.astype(BF16)
    return hi, (a - hi.astype(F32)).astype(BF16)


def _dot3_split(a_hi, a_lo, b):
    b_hi, b_lo = _split_bf16(b)
    dot = functools.partial(jnp.dot, preferred_element_type=F32)
    return dot(a_hi, b_hi) + (dot(a_lo, b_hi) + dot(a_hi, b_lo))


def _dot3(a, b):
    return _dot3_split(*_split_bf16(a), b)


def _store_pitched(ref, lead, value):
    for k in range(value.shape[0] // FFT_R):
        ref[lead, k * FFT_PITCH:k * FFT_PITCH + FFT_R, :] = value[k * FFT_R:(k + 1) * FFT_R]
        ref[lead, k * FFT_PITCH + FFT_R:(k + 1) * FFT_PITCH, :] = jnp.zeros(
            (FFT_PITCH - FFT_R, value.shape[1]), value.dtype)


def _filter_kernel(w1t_ref, b1_ref, w2t_ref, b2_ref, fr_ref, w3_ref, dec_ref, bands_ref, h_ref, ss_ref, *,
                   L, tm, pitched):
    i = pl.program_id(0)
    t_row = (lax.broadcasted_iota(jnp.int32, (1, tm), 1) + i * tm).astype(F32) / L
    ang = (2.0 * math.pi * t_row) * bands_ref[...]
    t8 = jnp.where(lax.broadcasted_iota(jnp.int32, (8, tm), 0) == 0, t_row, 0.0)
    feat = jnp.concatenate([t8, jnp.cos(ang), -jnp.sin(ang)], axis=0)
    fr = fr_ref[...]
    hid = jnp.sin(fr * (jnp.dot(w1t_ref[...], feat, precision=HIGHEST, preferred_element_type=F32) + b1_ref[...]))
    hid = jnp.sin(fr * (jnp.dot(w2t_ref[...], hid, precision=HIGHEST, preferred_element_type=F32) + b2_ref[...]))
    hid_hi, hid_lo = _split_bf16(hid.T)
    m_col = lax.broadcasted_iota(jnp.int32, (tm, 1), 0) + i * tm
    t_col = m_col.astype(F32) / L
    energy = jnp.zeros((1, HY_WIDTH), F32)
    for side in range(2):
        h = _dot3_split(hid_hi, hid_lo, w3_ref[side]) * jnp.exp(-t_col * jnp.abs(dec_ref[side]))
        if side == 1:
            h = jnp.where(m_col == 0, 0.0, h)
        if pitched:
            _store_pitched(h_ref, side, h)
        else:
            h_ref[side] = h
        energy = energy + jnp.sum(h * h, axis=0, keepdims=True)

    @pl.when(i == 0)
    def _():
        ss_ref[...] = jnp.zeros_like(ss_ref)

    ss_ref[...] += jnp.broadcast_to(energy, ss_ref.shape)


def _hyena_filter(w1, b1, w2, b2, w3, freq, decay, *, L):
    tm = min(L, 1024)
    pitched = L == SEQ
    tm_out, l_out = (tm // FFT_R * FFT_PITCH, SEQ_P) if pitched else (tm, L)
    kern = functools.partial(_filter_kernel, L=L, tm=tm, pitched=pitched)
    w1t = jnp.concatenate([w1[0:1], jnp.zeros((7, HY_HIDDEN), F32), w1[1:]], axis=0).T
    col = lambda a: a.reshape(HY_HIDDEN, 1)
    w3s = jnp.stack([w3[:, :HY_WIDTH], w3[:, HY_WIDTH:]])
    bands = jnp.asarray(np.linspace(1e-4, HY_BANDS - 1, HY_BANDS).astype(np.float32).reshape(HY_BANDS, 1))
    args = (w1t, col(b1), w2.T, col(b2), col(freq), w3s, decay.reshape(2, 1, HY_WIDTH), bands)
    full = lambda a: pl.BlockSpec(a.shape, lambda i: (0,) * a.ndim)
    h, ss = pl.pallas_call(
        kern,
        out_shape=(jax.ShapeDtypeStruct((2, l_out, HY_WIDTH), F32), jax.ShapeDtypeStruct((8, HY_WIDTH), F32)),
        grid=(L // tm,),
        in_specs=[full(a) for a in args],
        out_specs=(pl.BlockSpec((2, tm_out, HY_WIDTH), lambda i: (0, i, 0)),
                   pl.BlockSpec((8, HY_WIDTH), lambda i: (0, 0))),
        compiler_params=_params(("arbitrary",), VMEM_SMALL_MIB),
        name="hyena_filter_%d" % L,
    )(*args)
    return h.reshape(2 * l_out, HY_WIDTH), ss


def _stack_bf16(re, im):
    return jnp.concatenate([re, im], axis=0).astype(BF16)


def _filter_fft_kernel(h_ref, ss_ref, m1_ref, f2_ref, o_ref, a_scr):
    R, P = FFT_R, FFT_PITCH
    inv_norm = lax.rsqrt(ss_ref[0:1, :])

    def stage1(n2, carry):
        neg = jnp.where(n2 == 0, 0, R - n2)
        xs = jnp.concatenate([h_ref[pl.ds(n2, R // 2, stride=P), :],
                              h_ref[pl.ds(SEQ_P + neg, R // 2, stride=P), :]], axis=0)
        xs = (xs * inv_norm).astype(BF16)
        a = jnp.dot(m1_ref[n2], xs, preferred_element_type=F32)
        base = pl.multiple_of(n2 * P, 8)
        a_scr[0, pl.ds(base, R), :] = a[:R]
        a_scr[1, pl.ds(base, R), :] = a[R:]
        return carry

    lax.fori_loop(0, R, stage1, 0, unroll=FFT_UNROLL)

    ct = a_scr.shape[-1]

    def stage2(pair, carry):
        k1s = (2 * pair, 2 * pair + 1)
        rhs = jnp.concatenate([_stack_bf16(a_scr[0, pl.ds(k1, R, stride=P), :],
                                           a_scr[1, pl.ds(k1, R, stride=P), :]) for k1 in k1s], axis=1)
        x = jnp.dot(f2_ref[...], rhs, preferred_element_type=F32)
        for d, k1 in enumerate(k1s):
            base = pl.multiple_of(k1 * R, R)
            o_ref[0, pl.ds(base, R), :] = x[:R, d * ct:(d + 1) * ct].astype(BF16)
            o_ref[1, pl.ds(base, R), :] = x[R:, d * ct:(d + 1) * ct].astype(BF16)
        return carry

    lax.fori_loop(0, R // 2, stage2, 0, unroll=FFT_UNROLL // 2)


def _filter_fft(h_circ, ss, m1, f2):
    ct = LANES
    once = pl.Buffered(1)
    return pl.pallas_call(
        _filter_fft_kernel,
        out_shape=jax.ShapeDtypeStruct((2, FFT_N, HY_WIDTH), BF16),
        grid=(HY_WIDTH // ct,),
        in_specs=[
            pl.BlockSpec((2 * SEQ_P, ct), lambda c: (0, c)),
            pl.BlockSpec((8, ct), lambda c: (0, c)),
            pl.BlockSpec(m1.shape, lambda c: (0, 0, 0), pipeline_mode=once),
            pl.BlockSpec(f2.shape, lambda c: (0, 0), pipeline_mode=once),
        ],
        out_specs=pl.BlockSpec((2, FFT_N, ct), lambda c: (0, 0, c)),
        scratch_shapes=[pltpu.VMEM((2, FFT_R * FFT_PITCH, ct), F32)],
        compiler_params=_params(("parallel",), VMEM_FFT_MIB),
        name="hyena_filter_fft",
    )(h_circ, ss, m1, f2)


def _fft_conv_kernel(z_hbm, hs_ref, m1_ref, f2_ref, g2_ref, m1i_ref, y_hbm, a_scr, z_ref, y_ref, sems):
    R, P = FFT_R, FFT_PITCH
    ct = a_scr.shape[-1]
    c = pl.program_id(0)
    n_c = pl.num_programs(0)

    def z_copy(tile):
        return pltpu.make_async_copy(z_hbm.at[:, :, pl.ds(pl.multiple_of(tile * ct, ct), ct)], z_ref, sems.at[0])

    def y_copy(tile):
        return pltpu.make_async_copy(y_ref, y_hbm.at[:, :, pl.ds(pl.multiple_of(tile * ct, ct), ct)], sems.at[1])

    @pl.when(c == 0)
    def _():
        z_copy(0).start()

    z_copy(c).wait()

    def stage1(n2, carry):
        rows = pl.ds(n2, R // 2, stride=P)
        xs = jnp.concatenate([z_ref[0, rows, :], z_ref[1, rows, :]], axis=1).astype(BF16)
        zero = jnp.zeros_like(xs)
        rhs = jnp.concatenate([jnp.concatenate([xs, zero], axis=1),
                               jnp.concatenate([zero, xs], axis=1)], axis=0)
        t = jnp.dot(m1_ref[n2], rhs, preferred_element_type=F32)
        base = pl.multiple_of(n2 * P, 8)
        a_scr[0, pl.ds(base, R), :] = t[:, 0:ct] - t[:, 3 * ct:4 * ct]
        a_scr[1, pl.ds(base, R), :] = t[:, 2 * ct:3 * ct] + t[:, ct:2 * ct]
        return carry

    lax.fori_loop(0, R, stage1, 0, unroll=FFT_UNROLL)

    @pl.when(c + 1 < n_c)
    def _():
        z_copy(c + 1).start()

    def stage2(pair, carry):
        k1s = (2 * pair, 2 * pair + 1)
        rhs = jnp.concatenate([_stack_bf16(a_scr[0, pl.ds(k1, R, stride=P), :],
                                           a_scr[1, pl.ds(k1, R, stride=P), :]) for k1 in k1s], axis=1)
        x = jnp.dot(f2_ref[...], rhs, preferred_element_type=F32)
        prods = []
        for d, k1 in enumerate(k1s):
            base = pl.multiple_of(k1 * R, R)
            hr = hs_ref[0, pl.ds(base, R), :].astype(F32)
            hi = hs_ref[1, pl.ds(base, R), :].astype(F32)
            xr, xi = x[:R, d * ct:(d + 1) * ct], x[R:, d * ct:(d + 1) * ct]
            prods.append(_stack_bf16(xr * hr - xi * hi, xr * hi + xi * hr))
        bq = jnp.dot(g2_ref[...], jnp.concatenate(prods, axis=1), preferred_element_type=F32)
        for d, k1 in enumerate(k1s):
            a_scr[0, pl.ds(k1, R, stride=P), :] = bq[:R, d * ct:(d + 1) * ct]
            a_scr[1, pl.ds(k1, R, stride=P), :] = bq[R:, d * ct:(d + 1) * ct]
        return carry

    lax.fori_loop(0, R // 2, stage2, 0, unroll=FFT_UNROLL // 2)

    @pl.when(c > 0)
    def _():
        y_copy(c - 1).wait()

    def stage3(n2, carry):
        base = pl.multiple_of(n2 * P, 8)
        br = a_scr[0, pl.ds(base, R), :]
        bi = a_scr[1, pl.ds(base, R), :]
        rhs = jnp.concatenate([jnp.concatenate([br, bi], axis=1),
                               jnp.concatenate([bi, -br], axis=1)], axis=0).astype(BF16)
        y = jnp.dot(m1i_ref[n2], rhs, preferred_element_type=F32)
        rows = pl.ds(n2, R // 2, stride=P)
        y_ref[0, rows, :] = y[:, :ct]
        y_ref[1, rows, :] = y[:, ct:]
        return carry

    lax.fori_loop(0, R, stage3, 0, unroll=FFT_UNROLL)
    for b in range(2):
        for n1 in range(R // 2):
            y_ref[b, n1 * P + R:(n1 + 1) * P, :] = jnp.zeros((P - R, ct), F32)
    y_copy(c).start()

    @pl.when(c == n_c - 1)
    def _():
        y_copy(c).wait()


def _fft_conv(z, hspec, m1, f2, g2, m1i):
    assert z.shape[0] == 2, "the batch pair is packed into one complex signal"
    ct = LANES
    once = pl.Buffered(1)
    return pl.pallas_call(
        _fft_conv_kernel,
        out_shape=jax.ShapeDtypeStruct((2, SEQ_P, HY_WIDTH), F32),
        grid=(HY_WIDTH // ct,),
        in_specs=[
            pl.BlockSpec(memory_space=pl.ANY),
            pl.BlockSpec((2, FFT_N, ct), lambda c: (0, 0, c)),
            pl.BlockSpec(m1.shape, lambda c: (0, 0, 0), pipeline_mode=once),
            pl.BlockSpec(f2.shape, lambda c: (0, 0), pipeline_mode=once),
            pl.BlockSpec(g2.shape, lambda c: (0, 0), pipeline_mode=once),
            pl.BlockSpec(m1i.shape, lambda c: (0, 0, 0), pipeline_mode=once),
        ],
        out_specs=pl.BlockSpec(memory_space=pl.ANY),
        scratch_shapes=[pltpu.VMEM((2, FFT_R * FFT_PITCH, ct), F32), pltpu.VMEM((2, SEQ_P, ct), F32),
                        pltpu.VMEM((2, SEQ_P, ct), F32), pltpu.SemaphoreType.DMA((2,))],
        compiler_params=_params(("arbitrary",), VMEM_FFT_MIB),
        name="hyena_fft_conv",
    )(z, hspec, m1, f2, g2, m1i)


def _ctx_conv_kernel(z_ref, h_ref, ss_ref, fwd_ref, inv_ref, fwd_h_ref, y_ref):
    N = 2 * CTX_LEN
    hn = (h_ref[...] * lax.rsqrt(ss_ref[0:1, :])).astype(BF16)
    hs = jnp.dot(fwd_h_ref[...], hn, preferred_element_type=F32)
    zs = jnp.dot(fwd_ref[:, :CTX_LEN], z_ref[...].astype(BF16), preferred_element_type=F32)
    hr, hi, zr, zi = hs[:N], hs[N:], zs[:N], zs[N:]
    ys = _stack_bf16(zr * hr - zi * hi, zr * hi + zi * hr)
    y_ref[...] = jnp.dot(inv_ref[...], ys, preferred_element_type=F32)


def _ctx_conv(z, h_circ, ss, fwd, inv, fwd_h):
    G = z.shape[0]
    full = lambda a: pl.BlockSpec(a.shape, lambda b: (0,) * a.ndim)
    return pl.pallas_call(
        _ctx_conv_kernel,
        out_shape=jax.ShapeDtypeStruct((G, CTX_LEN, HY_WIDTH), F32),
        grid=(G,),
        in_specs=[pl.BlockSpec((None, CTX_LEN, HY_WIDTH), lambda b: (b, 0, 0)),
                  full(h_circ), full(ss), full(fwd), full(inv), full(fwd_h)],
        out_specs=pl.BlockSpec((None, CTX_LEN, HY_WIDTH), lambda b: (b, 0, 0)),
        compiler_params=_params(("parallel",), VMEM_SMALL_MIB),
        name="hyena_ctx_conv",
    )(z, h_circ, ss, fwd, inv, fwd_h)


def _kv_group(refs, g, col0):
    cols = slice(col0 + g * LANES, col0 + (g + 1) * LANES)
    return refs[0][:, cols] if len(refs) == 1 else jnp.concatenate([r[:, cols] for r in refs], axis=0)


def _attn_logits(q_ref, k_refs, g, masks, col0=0):
    low = lax.broadcasted_iota(jnp.int32, (BLOCK, LANES), 1) < HEAD_DIM
    zero = jnp.zeros((), BF16)
    parts = []
    for hh in range(ATT_GROUP):
        h = g * ATT_GROUP + hh
        tile = q_ref[:, (h // 2) * LANES:(h // 2 + 1) * LANES]
        parts.append(jnp.where(low if h % 2 == 0 else ~low, tile, zero))
    s = lax.dot_general(jnp.concatenate(parts, axis=0), _kv_group(k_refs, g, col0), (((1,), (1,)), ((), ())),
                        preferred_element_type=F32)
    if masks is not None:
        ok_prev, ok_next = masks
        s = jnp.concatenate([jnp.where(ok_prev, s[:, :BLOCK], NEG_INF), s[:, BLOCK:2 * BLOCK],
                             jnp.where(ok_next, s[:, 2 * BLOCK:3 * BLOCK], NEG_INF), s[:, 3 * BLOCK:]], axis=1)
    return s


def _attn_output(s, sink_ref, v_refs, g, o_ref, col0=0):
    low = lax.broadcasted_iota(jnp.int32, (BLOCK, LANES), 1) < HEAD_DIM
    low4 = jnp.concatenate([low] * ATT_GROUP, axis=0)
    v = _kv_group(v_refs, g, col0)
    v_aug = jnp.where(lax.broadcasted_iota(jnp.int32, v.shape, 1) < HEAD_DIM, v, jnp.ones((), BF16))
    sink = jnp.concatenate(
        [jnp.full((BLOCK, 1), sink_ref[g * ATT_GROUP + hh] * LOG2E, F32) for hh in range(ATT_GROUP)], axis=0)
    m = jnp.maximum(jnp.max(s, axis=-1, keepdims=True), sink)
    e = jnp.exp2(s - m).astype(BF16)
    o = jnp.dot(e, v_aug, preferred_element_type=F32) + jnp.where(low4, 0.0, jnp.exp2(sink - m))
    swapped = pltpu.roll(o, HEAD_DIM, axis=1)
    for pair in range(ATT_GROUP // 2):
        ev = slice((2 * pair) * BLOCK, (2 * pair + 1) * BLOCK)
        od = slice((2 * pair + 1) * BLOCK, (2 * pair + 2) * BLOCK)
        even = o[ev] / swapped[ev]
        odd = swapped[od] / o[od]
        t = g * (ATT_GROUP // 2) + pair
        o_ref[:, t * LANES:(t + 1) * LANES] = jnp.where(low, even, odd).astype(o_ref.dtype)


def _attn_ctx_kernel(sink_ref, q_ref, kx_ref, vx_ref, o_ref):
    for g in range(ATT_KV_HEADS):
        _attn_output(_attn_logits(q_ref, (kx_ref,), g, None), sink_ref, (vx_ref,), g, o_ref)


def _attn_local_kernel(sink_ref, q_ref, kv_m2, kv_m1, kv_0, kv_p1, kv_ctx, o_ref, s_even, s_odd, *, n_blocks):
    n = pl.program_id(1)
    k_refs, v_refs, v_col = (kv_m1, kv_0, kv_p1, kv_ctx), (kv_m2, kv_m1, kv_0, kv_ctx), 2 * LANES

    @pl.when(n == 0)
    def _():
        s_odd[...] = jnp.zeros_like(s_odd)

    def step(s_new, s_prev):
        nq = jnp.minimum(n, n_blocks - 1)
        qi = lax.broadcasted_iota(jnp.int32, (ATT_GROUP * BLOCK, BLOCK), 0) % BLOCK
        ki = lax.broadcasted_iota(jnp.int32, (ATT_GROUP * BLOCK, BLOCK), 1)
        masks = ((ki >= qi) & (nq > 0), (ki <= qi) & (nq < n_blocks - 1))
        for g in range(ATT_KV_HEADS):
            s_new[g] = _attn_logits(q_ref, k_refs, g, masks)
        for g in range(ATT_KV_HEADS):
            _attn_output(s_prev[g], sink_ref, v_refs, g, o_ref, v_col)

    @pl.when(n % 2 == 0)
    def _():
        step(s_even, s_odd)

    @pl.when(n % 2 == 1)
    def _():
        step(s_odd, s_even)


def _attention(sink, qkv, qkv_ctx, *, local):
    src = qkv if local else qkv_ctx
    G, R, _ = src.shape
    nb = R // BLOCK
    kcol, vcol = ATT_WIDTH // 256, ATT_WIDTH // 256 + 1
    ctx_k = pl.BlockSpec((None, CTX_LEN, 256), lambda b, n: (b, 0, kcol))
    ctx_v = pl.BlockSpec((None, CTX_LEN, 256), lambda b, n: (b, 0, vcol))
    smem = pl.BlockSpec(memory_space=pltpu.SMEM)
    out_shape = jax.ShapeDtypeStruct((G, R, ATT_WIDTH), BF16)
    if not local:
        rows = pl.BlockSpec((None, BLOCK, ATT_WIDTH), lambda b, n: (b, n, 0))
        return pl.pallas_call(
            _attn_ctx_kernel, out_shape=out_shape, grid=(G, nb),
            in_specs=[smem, rows, ctx_k, ctx_v], out_specs=rows,
            compiler_params=_params(("parallel", "parallel"), VMEM_SMALL_MIB), name="attn_ctx",
        )(sink, qkv_ctx, qkv_ctx, qkv_ctx)

    q_blk = lambda n: jnp.minimum(n, nb - 1)
    o_blk = lambda n: jnp.maximum(n - 1, 0)

    kv_w = EV_QKV - ATT_WIDTH
    kv_col = ATT_WIDTH // kv_w

    def kv_rows(off):
        return pl.BlockSpec((None, BLOCK, kv_w), lambda b, n: (b, jnp.clip(n + off, 0, nb - 1), kv_col))

    in_specs = [smem, pl.BlockSpec((None, BLOCK, ATT_WIDTH), lambda b, n: (b, q_blk(n), 0)),
                kv_rows(-2), kv_rows(-1), kv_rows(0), kv_rows(1),
                pl.BlockSpec((None, CTX_LEN, kv_w), lambda b, n: (b, 0, kv_col))]
    logits_scratch = pltpu.VMEM((ATT_KV_HEADS, ATT_GROUP * BLOCK, 3 * BLOCK + CTX_LEN), F32)
    return pl.pallas_call(
        functools.partial(_attn_local_kernel, n_blocks=nb),
        out_shape=out_shape,
        grid=(G, nb + 1),
        in_specs=in_specs,
        out_specs=pl.BlockSpec((None, BLOCK, ATT_WIDTH), lambda b, n: (b, o_blk(n), 0)),
        scratch_shapes=[logits_scratch, logits_scratch],
        compiler_params=_params(("parallel", "arbitrary"), VMEM_SMALL_MIB),
        name="attn_local",
    )(sink, qkv, qkv, qkv, qkv, qkv, qkv_ctx)


def _mixer_out0(rows, x0_ref, y_ref, zb_ref, att_ref, wo_ref):
    if y_ref.shape[0] != x0_ref.shape[0]:
        y = jnp.concatenate([y_ref[r // FFT_R * FFT_PITCH:r // FFT_R * FFT_PITCH + FFT_R, :]
                             for r in range(rows.start, rows.stop, FFT_R)], axis=0)
    else:
        y = y_ref[rows, :]
    hy = (x0_ref[rows, :] * y + zb_ref[rows, :]).astype(BF16)
    return (jnp.dot(hy, wo_ref[:HY_WIDTH, :], preferred_element_type=F32)
            + jnp.dot(att_ref[rows, :], wo_ref[HY_WIDTH:, :], preferred_element_type=F32))


def _mixer_out1(rows, of_ref, ob_ref, wo_ref):
    a = (of_ref[rows, :].astype(F32) + ob_ref[rows, :].astype(F32)).astype(BF16)
    return jnp.dot(a, wo_ref[...], preferred_element_type=F32)


def _mix_kernel(*refs, mixer_out, n_mix, row, fc, final_norm):
    x_ref = refs[0]
    mix_refs = refs[1:2 + n_mix]
    gta_ref, gm_ref, shm_ref, scm_ref, gtm_ref, w1_ref, w2_ref, fg_ref, o_ref, a_scr = refs[2 + n_mix:]
    r = pl.program_id(0) if row is None else row
    gta, gtm = _mod_row(gta_ref, r), _mod_row(gtm_ref, r)
    gm, shm, scm = gm_ref[...], _mod_row(shm_ref, r), _mod_row(scm_ref, r)
    sub = min(x_ref.shape[0], MIX_SUB_ROWS)
    for s in range(x_ref.shape[0] // sub):
        rows = slice(s * sub, (s + 1) * sub)
        x1 = x_ref[rows, :] + gta * mixer_out(rows, *mix_refs)
        h = _norm_mod(x1, gm, shm, scm).astype(BF16)
        for c in range(D_FF // fc):
            a = jnp.maximum(jnp.dot(h, w1_ref[:, c * fc:(c + 1) * fc], preferred_element_type=F32), 0.0)
            a_scr[rows, c * fc:(c + 1) * fc] = (a * a).astype(BF16)
        out = x1 + gtm * jnp.dot(a_scr[rows, :], w2_ref[...], preferred_element_type=F32)
        if final_norm:
            out = (out * lax.rsqrt(jnp.mean(out * out, axis=-1, keepdims=True) + EPS)) * fg_ref[...]
        o_ref[rows, :] = out


def _mix_mlp(kind, x, mix_in, wo, mod, gm, w1, w2, fg, layer, *, is_ctx, tm, fc, final_norm, cast=()):
    G, R, _ = x.shape
    row_spec = lambda w: pl.BlockSpec((None, tm, w), lambda b, i: (b, i, 0))
    modk = lambda k: pl.BlockSpec((None, 8, D_MODEL), lambda b, i: (layer, 0, k))
    vec = pl.BlockSpec((1, D_MODEL), lambda b, i: (0, 0))
    resident = lambda a: pl.BlockSpec(a.shape, lambda b, i: (0, 0), pipeline_mode=pl.Buffered(1))
    if kind == 0:
        mixer_out = _mixer_out0
        y_rows = tm if mix_in[1].shape[1] == R else tm // FFT_R * FFT_PITCH
        y_spec = pl.BlockSpec((None, y_rows, HY_WIDTH), lambda b, i: (b, i, 0))
        mix_specs = [row_spec(HY_WIDTH), y_spec, row_spec(HY_WIDTH), row_spec(ATT_WIDTH)]
    else:
        mixer_out = _mixer_out1
        mix_specs = [row_spec(RET_V)] * 2
    kern = functools.partial(_mix_kernel, mixer_out=mixer_out, n_mix=len(mix_in), row=2 if is_ctx else None,
                             fc=fc, final_norm=final_norm)
    in_specs = [row_spec(D_MODEL)] + mix_specs + [
        resident(wo), modk(2), vec, modk(3), modk(4), modk(5), resident(w1), resident(w2), vec]
    c_in, c_out, c_shapes, c_args = _cast_jobs(cast, R // tm)
    if cast:
        assert G * (R // tm) >= CAST_CHUNKS
        kern = _with_casts(kern, len(in_specs), 1, len(cast))
    res = pl.pallas_call(
        kern,
        out_shape=[jax.ShapeDtypeStruct((G, R, D_MODEL), F32)] + c_shapes,
        grid=(G, R // tm),
        in_specs=in_specs + c_in,
        out_specs=[row_spec(D_MODEL)] + c_out,
        scratch_shapes=[pltpu.VMEM((tm, D_FF), BF16)],
        compiler_params=_params(("arbitrary", "arbitrary") if cast else ("parallel", "parallel"), VMEM_DENSE_MIB),
        name="mix_mlp%d%s" % (kind, "_ctx" if is_ctx else ""),
    )(x, *mix_in, wo, mod, gm, mod, mod, mod, w1, w2, fg, *c_args)
    return res[0], res[1:]


def _ret_kernel(lr_ref, qkv_c, qkv_f, g_f, qkv_b, g_b, cos_a, sin_a, cos_b, sin_b, of_ref, ob_ref,
                state, dmask, xi, zeta, gch):
    j = pl.program_id(1)
    C = RET_C
    kscale = RET_DK ** -0.5

    @pl.when(j == 0)
    def _():
        state[...] = jnp.zeros_like(state)
        row = lax.broadcasted_iota(jnp.int32, (C, C), 0).astype(F32)
        col = lax.broadcasted_iota(jnp.int32, (C, C), 1).astype(F32)
        for d in range(2):
            e = row - col if d == 0 else col - row
            p = row if d == 0 else (C - 1) - row
            for h in range(RET_HEADS):
                lg = -jnp.exp(jnp.full((C, C), lr_ref[d, h], F32))
                dmask[d, h] = jnp.where(e >= 0, jnp.exp(e * lg), 0.0) * kscale
                xi[d, h] = jnp.exp((p + 1.0) * lg)
                zeta[d, h] = jnp.exp(((C - 1) - p) * lg) * kscale
                gch[d, h] = jnp.exp(C * -jnp.exp(jnp.full((8, C), lr_ref[d, h], F32)))

    half = RET_DK // 2

    def chain(d, h, src_ref, g_ref, o_ref, tab_ref, r0):
        cos, sin = tab_ref[0][d], tab_ref[1][d]
        rows = slice(r0, r0 + C)

        def rot(c0):
            t1 = src_ref[rows, c0:c0 + half].astype(F32)
            t2 = src_ref[rows, c0 + half:c0 + RET_DK].astype(F32)
            return jnp.concatenate([t1 * cos - t2 * sin, t2 * cos + t1 * sin], axis=1)

        q = rot(h * RET_DK)
        k = rot(RET_QK + h * RET_DK)
        v = src_ref[rows, 2 * RET_QK + h * RET_DV:2 * RET_QK + (h + 1) * RET_DV]
        inner = lax.dot_general(q.astype(BF16), k.astype(BF16), (((1,), (1,)), ((), ())),
                                preferred_element_type=F32) * dmask[d, h]
        s_old = state[d, h]
        kv = lax.dot_general((k * zeta[d, h]).astype(BF16), v, (((0,), (0,)), ((), ())),
                             preferred_element_type=F32)
        state[d, h] = gch[d, h, 0:1, 0:1] * s_old + kv
        if o_ref is not None:
            o = (jnp.dot(inner.astype(BF16), v, preferred_element_type=F32)
                 + jnp.dot((q * xi[d, h]).astype(BF16), s_old.astype(BF16), preferred_element_type=F32))
            on = o * lax.rsqrt(jnp.mean(o * o, axis=-1, keepdims=True) + EPS)
            gate = g_ref[rows, h * RET_DV:(h + 1) * RET_DV]
            half_gate = gate * 0.5
            silu = half_gate + half_gate * jnp.tanh(half_gate)
            o_ref[rows, h * RET_DV:(h + 1) * RET_DV] = (silu.astype(F32) * on).astype(o_ref.dtype)

    tab_a, tab_b = (cos_a, sin_a), (cos_b, sin_b)

    @pl.when(j == 0)
    def _():
        for d in range(2):
            for h in range(RET_HEADS):
                chain(d, h, qkv_c, None, None, tab_a, 0)

    @pl.when(j > 0)
    def _():
        for tab, r_fwd, r_bwd in ((tab_a, 0, C), (tab_b, C, 0)):
            for h in range(RET_HEADS):
                chain(0, h, qkv_f, g_f, of_ref, tab, r_fwd)
                chain(1, h, qkv_b, g_b, ob_ref, tab, r_bwd)


def _retention(log_rate, p_ctx, p_lat):
    G = p_lat.shape[0]
    C = RET_C
    n_pairs = SEQ // (2 * C)
    cos, sin = _rope1d_tables()
    cos, sin = jnp.asarray(cos), jnp.asarray(sin)

    def fwd_pair(j):
        return jnp.maximum(j - 1, 0)

    def bwd_pair(j):
        return n_pairs - jnp.maximum(j, 1)

    qkv_w = 2 * RET_QK + RET_V
    gcol = qkv_w // RET_V
    tab_a = pl.BlockSpec((2, C, RET_DK // 2), lambda b, j: (0, jnp.maximum(2 * j - 1, 0), 0))
    tab_b = pl.BlockSpec((2, C, RET_DK // 2), lambda b, j: (0, 2 * j, 0))
    in_specs = [
        pl.BlockSpec(memory_space=pltpu.SMEM),
        pl.BlockSpec((None, C, qkv_w), lambda b, j: (b, 0, 0)),
        pl.BlockSpec((None, 2 * C, qkv_w), lambda b, j: (b, fwd_pair(j), 0)),
        pl.BlockSpec((None, 2 * C, RET_V), lambda b, j: (b, fwd_pair(j), gcol)),
        pl.BlockSpec((None, 2 * C, qkv_w), lambda b, j: (b, bwd_pair(j), 0)),
        pl.BlockSpec((None, 2 * C, RET_V), lambda b, j: (b, bwd_pair(j), gcol + 1)),
        tab_a, tab_a, tab_b, tab_b,
    ]
    out = jax.ShapeDtypeStruct((G, SEQ, RET_V), BF16)
    per_chain = lambda *tail: pltpu.VMEM((2, RET_HEADS) + tail, F32)
    return pl.pallas_call(
        _ret_kernel,
        out_shape=(out, out),
        grid=(G, 1 + n_pairs),
        in_specs=in_specs,
        out_specs=(pl.BlockSpec((None, 2 * C, RET_V), lambda b, j: (b, fwd_pair(j), 0)),
                   pl.BlockSpec((None, 2 * C, RET_V), lambda b, j: (b, bwd_pair(j), 0))),
        scratch_shapes=[per_chain(RET_DK, RET_DV), per_chain(C, C), per_chain(C, C), per_chain(C, C),
                        per_chain(8, C)],
        compiler_params=_params(("parallel", "arbitrary"), VMEM_DENSE_MIB),
        name="retention",
    )(log_rate, p_ctx, p_lat, p_lat, p_lat, p_lat, cos, sin, cos, sin)


def _ev_weight(w_in):
    i_q = EV_U
    i_k = i_q + ATT_WIDTH
    i_v = i_k + KV_WIDTH
    cols = [w_in[:, :i_q], w_in[:, i_q:i_k] * (HEAD_DIM ** -0.5 * LOG2E)]
    for base in (i_k, i_v):
        for g in range(ATT_KV_HEADS):
            part = w_in[:, base + g * HEAD_DIM: base + (g + 1) * HEAD_DIM]
            cols += [part, part]
    return jnp.concatenate([c.astype(BF16) for c in cols], axis=1)


def kernel(x, c, ctx, c_ctx, ada_w, ada_b, norm_mix_g, norm_mlp_g, mlp_w1, mlp_w2, ev_w_in, ev_w_out, hy_conv_w, hy_conv_b, hy_w1, hy_b1, hy_w2, hy_b2, hy_w3, hy_freq, hy_decay, hy_bias, attn_sink, od_w_in, od_w_out, ret_log_rate, final_g):
    D = D_MODEL
    cvec = jnp.concatenate([c, c_ctx[None, :], jnp.zeros((8 - BATCH - 1, D), F32)], axis=0)
    mod = _ada(cvec, ada_w, ada_b)

    m1, m1i, f2, g2, m1c = (jnp.asarray(a).astype(BF16) for a in _fft_mats())
    cfwd, cinv, cfwd_h = (jnp.asarray(a).astype(BF16) for a in _ctx_fft_mats())
    fg = final_g.reshape(1, D)

    gmix = norm_mix_g[0].reshape(1, D)
    gmlp = norm_mlp_g[0].reshape(1, D)
    w_in = _ev_weight(ev_w_in[0])
    conv = (hy_conv_w[0], hy_conv_b[0].reshape(1, EV_U), hy_bias[0].reshape(1, HY_WIDTH))
    (z_l, x0_l, zb_l, qkv_l), (wo, w1, w2) = _inproj0(
        x, gmix, mod, w_in, *conv, 0, is_ctx=False, tm=512, cast=((ev_w_out, 0), (mlp_w1, 0), (mlp_w2, 0)))
    (z_c, x0_c, zb_c, qkv_c), _ = _inproj0(ctx, gmix, mod, w_in, *conv, 0, is_ctx=True, tm=CTX_LEN)

    filt = (hy_w1[0], hy_b1[0], hy_w2[0], hy_b2[0], hy_w3[0], hy_freq[0], hy_decay[0])
    h_l, ss_l = _hyena_filter(*filt, L=SEQ)
    h_c, ss_c = _hyena_filter(*filt, L=CTX_LEN)
    hspec = _filter_fft(h_l, ss_l, m1, f2)
    y_l = _fft_conv(z_l, hspec, m1c, f2, g2, m1i)
    y_c = _ctx_conv(z_c, h_c, ss_c, cfwd, cinv, cfwd_h)

    sink = attn_sink[0]
    att_l = _attention(sink, qkv_l, qkv_c, local=True)
    att_c = _attention(sink, qkv_l, qkv_c, local=False)

    x1, (w_in,) = _mix_mlp(0, x, (x0_l, y_l, zb_l, att_l), wo, mod, gmlp, w1, w2, fg, 0,
                           is_ctx=False, tm=512, fc=512, final_norm=False, cast=((od_w_in, 0),))
    ctx1, _ = _mix_mlp(0, ctx, (x0_c, y_c, zb_c, att_c), wo, mod, gmlp, w1, w2, fg, 0,
                       is_ctx=True, tm=CTX_LEN, fc=512, final_norm=False)

    gmix = norm_mix_g[1].reshape(1, D)
    gmlp = norm_mlp_g[1].reshape(1, D)
    p_l, (wo, w1, w2) = _inproj1(x1, gmix, mod, w_in, 1, is_ctx=False, tm=512, tn=512,
                                 cast=((od_w_out, 0), (mlp_w1, 1), (mlp_w2, 1)))
    p_c, _ = _inproj1(ctx1, gmix, mod, w_in, 1, is_ctx=True, tm=CTX_LEN, tn=512)
    o_fwd, o_bwd = _retention(ret_log_rate[0], p_c, p_l)

    out, _ = _mix_mlp(1, x1, (o_fwd, o_bwd), wo, mod, gmlp, w1, w2, fg, 1,
                      is_ctx=False, tm=512, fc=512, final_norm=True)
    return out
```
